```python
import jax, jax.numpy as jnp
from jax import lax
import numpy as np

D_MODEL = 1024
BATCH = 8
SEQ = 4096
DEPTH = 1
DEC_BATCH = 16
DEC_SEQ = 64
PAST_LEN = 4096

CHUNK = 64
N_META = 16
RET_HEADS = 4
RET_DK = 64
RET_DV = 128
SWA_HEADS = 8
SWA_KV_HEADS = 2
SWA_HD = 64
WINDOW = 128
ROPE_THETA = 10000.0
N_GROUPS = 4
EXPERTS_PER_GROUP = 8
N_EXPERTS = N_GROUPS * EXPERTS_PER_GROUP
TOP_K_IN_GROUP = 2
EXPERT_FF = 256
EPS = 1e-6
NEG_INF = -1e30
RET_Q = RET_HEADS * RET_DK
RET_V = RET_HEADS * RET_DV
SWA_Q = SWA_HEADS * SWA_HD
SWA_KV = SWA_KV_HEADS * SWA_HD
MIX_WIDTH = RET_V + SWA_Q
IN_WIDTH = 2 * RET_Q + 2 * RET_V + SWA_Q + 2 * SWA_KV
SPLIT_POINTS = (RET_Q, 2 * RET_Q, 2 * RET_Q + RET_V, 2 * RET_Q + 2 * RET_V,
                2 * RET_Q + 2 * RET_V + SWA_Q, 2 * RET_Q + 2 * RET_V + SWA_Q + SWA_KV)

kernel_name = "hymba_retention_swa_hmoe_stream_step"


def rmsnorm(x, g):
    xf = x.astype(jnp.float32)
    y = xf * lax.rsqrt(jnp.mean(xf * xf, axis=-1, keepdims=True) + EPS)
    return (y * g.astype(jnp.float32)).astype(x.dtype)


def rope(x, pos):
    half = x.shape[-1] // 2
    inv = ROPE_THETA ** (-jnp.arange(half, dtype=jnp.float32) / half)
    ang = pos.astype(jnp.float32)[:, None] * inv[None, :]
    cos = jnp.cos(ang)[:, None, :]
    sin = jnp.sin(ang)[:, None, :]
    xf = x.astype(jnp.float32)
    x1, x2 = xf[..., :half], xf[..., half:]
    return jnp.concatenate([x1 * cos - x2 * sin, x2 * cos + x1 * sin], axis=-1).astype(x.dtype)


def ret_log_decay():
    return jnp.log1p(-jnp.exp2(-5.0 - jnp.arange(RET_HEADS, dtype=jnp.float32)))


def retention_output(q, k, v, S):
    c = q.shape[1]
    log_g = ret_log_decay()
    i = jnp.arange(c, dtype=jnp.float32)
    diff = i[:, None] - i[None, :]
    decay = jnp.where(diff[None] >= 0, jnp.exp(jnp.maximum(diff, 0.0)[None] * log_g[:, None, None]), 0.0)
    intra = jnp.einsum('bihd,bjhd->bhij', q, k) * decay.astype(q.dtype)[None]
    o = jnp.einsum('bhij,bjhe->bihe', intra, v)
    cross_scale = jnp.exp((i + 1.0)[:, None] * log_g[None, :]).astype(q.dtype)
    cross = jnp.einsum('bihd,bhde->bihe', q, S.astype(q.dtype)) * cross_scale[None, :, :, None]
    return o + cross


def retention_update(k, v, S):
    c = k.shape[1]
    log_g = ret_log_decay()
    i = jnp.arange(c, dtype=jnp.float32)
    w = jnp.exp((c - 1.0 - i)[:, None] * log_g[None, :]).astype(k.dtype)
    kw = k * w[None, :, :, None]
    full = jnp.exp(c * log_g)[None, :, None, None]
    S_new = full * S.astype(jnp.float32) + jnp.einsum('bjhd,bjhe->bhde', kw, v).astype(jnp.float32)
    return S_new.astype(S.dtype)


def to_chunks(t, n, c):
    return jnp.moveaxis(t.reshape((t.shape[0], n, c) + t.shape[2:]), 1, 0)


def project(xn, w_in, q_norm_g, k_norm_g, pos):
    b, l, _ = xn.shape
    h = xn @ w_in
    rq, rk, rv, rg, sq, sk, sv = jnp.split(h, SPLIT_POINTS, axis=-1)
    rq = rope(rq.reshape(b, l, RET_HEADS, RET_DK), pos)
    rk = rope(rk.reshape(b, l, RET_HEADS, RET_DK), pos) * (RET_DK ** -0.5)
    rv = rv.reshape(b, l, RET_HEADS, RET_DV)
    sq = rope(rmsnorm(sq.reshape(b, l, SWA_HEADS, SWA_HD), q_norm_g), pos)
    sk = rope(rmsnorm(sk.reshape(b, l, SWA_KV_HEADS, SWA_HD), k_norm_g), pos)
    sv = sv.reshape(b, l, SWA_KV_HEADS, SWA_HD)
    return rq, rk, rv, rg, sq, sk, sv


def swa_attention(q, k_full, v_full, valid, meta_k, meta_v, has_meta, sinks, c):
    b, l, h, d = q.shape
    n = l // c
    s = WINDOW + c
    g = h // SWA_KV_HEADS
    idx = jnp.arange(n)[:, None] * c + jnp.arange(s)[None, :]
    kb = k_full[:, idx]
    vb = v_full[:, idx]
    band_ok = valid[idx]
    meta_ok = jnp.full((N_META,), has_meta)
    qb = q.reshape(b, n, c, SWA_KV_HEADS, g, d)
    scale = d ** -0.5
    s_meta = jnp.einsum('bnqkgd,mkd->bnkgqm', qb, meta_k).astype(jnp.float32) * scale
    s_band = jnp.einsum('bnqkgd,bnskd->bnkgqs', qb, kb).astype(jnp.float32) * scale
    s_meta = jnp.where(meta_ok[None, None, None, None, None, :], s_meta, NEG_INF)
    s_band = jnp.where(band_ok[None, :, None, None, None, :], s_band, NEG_INF)
    sink = jnp.broadcast_to(sinks.astype(jnp.float32).reshape(SWA_KV_HEADS, g)[None, None, :, :, None, None],
                            (b, n, SWA_KV_HEADS, g, c, 1))
    p = jax.nn.softmax(jnp.concatenate([s_meta, s_band, sink], axis=-1), axis=-1)
    p_meta = p[..., :N_META].astype(v_full.dtype)
    p_band = p[..., N_META:N_META + s].astype(v_full.dtype)
    o = (jnp.einsum('bnkgqm,mkd->bnqkgd', p_meta, meta_v)
         + jnp.einsum('bnkgqs,bnskd->bnqkgd', p_band, vb))
    return o.reshape(b, l, h * d)


def hier_moe(hn, w_group, w_expert, w_gate, w_up, w_down):
    b, l, d = hn.shape
    t = hn.reshape(-1, d)
    pg = jax.nn.softmax((t @ w_group).astype(jnp.float32), axis=-1)
    g_sel = jnp.argmax(pg, axis=-1)
    p_sel = jnp.take_along_axis(pg, g_sel[:, None], axis=-1)
    el = (t @ w_expert).astype(jnp.float32).reshape(-1, N_GROUPS, EXPERTS_PER_GROUP)
    el_sel = jnp.take_along_axis(el, g_sel[:, None, None], axis=1)[:, 0]
    top_p, top_i = lax.top_k(jax.nn.softmax(el_sel, axis=-1), TOP_K_IN_GROUP)
    w = p_sel * top_p / jnp.sum(top_p, axis=-1, keepdims=True)
    eid = g_sel[:, None] * EXPERTS_PER_GROUP + top_i
    comb = jnp.sum(jax.nn.one_hot(eid, N_EXPERTS, dtype=jnp.float32) * w[..., None], axis=1)
    out = jnp.zeros(t.shape, jnp.float32)
    for e in range(N_EXPERTS):
        a = jax.nn.silu(t @ w_gate[e]) * (t @ w_up[e])
        out = out + comb[:, e:e + 1] * (a @ w_down[e]).astype(jnp.float32)
    return out.astype(hn.dtype).reshape(b, l, d)


def layer_forward(x, pos, ret_s0, hist_k, hist_v, has_hist, meta_k, meta_v, has_meta,
                  norm1_g, w_in, q_norm_g, k_norm_g, ret_norm_g, attn_sinks, w_out,
                  norm2_g, w_group, w_expert, w_gate, w_up, w_down):
    b, l, _ = x.shape
    rq, rk, rv, rg, sq, sk, sv = project(rmsnorm(x, norm1_g), w_in, q_norm_g, k_norm_g, pos)
    c = min(CHUNK, l)
    n = l // c
    def body(S, qkv):
        q_c, k_c, v_c = qkv
        return retention_update(k_c, v_c, S), retention_output(q_c, k_c, v_c, S)
    ret_s, o_r = lax.scan(body, ret_s0, (to_chunks(rq, n, c), to_chunks(rk, n, c), to_chunks(rv, n, c)))
    o_r = jnp.moveaxis(o_r, 0, 1).reshape(b, l, RET_HEADS, RET_DV)
    o_r = rmsnorm(o_r, ret_norm_g.reshape(RET_HEADS, RET_DV)).reshape(b, l, RET_V) * jax.nn.silu(rg)
    k_full = jnp.concatenate([hist_k, sk], axis=1)
    v_full = jnp.concatenate([hist_v, sv], axis=1)
    valid = jnp.concatenate([jnp.full((WINDOW,), has_hist), jnp.ones((l,), dtype=bool)])
    o_s = swa_attention(sq, k_full, v_full, valid, meta_k, meta_v, has_meta, attn_sinks, c)
    x = x + jnp.concatenate([o_r, o_s], axis=-1) @ w_out
    x = x + hier_moe(rmsnorm(x, norm2_g), w_group, w_expert, w_gate, w_up, w_down)
    return x, ret_s, k_full[:, -WINDOW:], v_full[:, -WINDOW:]


def setup_inputs(seed: int = 0) -> dict:
    key = jax.random.key(seed)
    ks = jax.random.split(key, 20)

    def nrm(k, shape, scale):
        return jax.random.normal(k, shape, jnp.float32) * scale

    return {
        "x_prompt": nrm(ks[0], (BATCH, SEQ, D_MODEL), 1.0),
        "x_sample": nrm(ks[1], (DEC_BATCH, DEC_SEQ, D_MODEL), 1.0),
        "cache_ret_state": nrm(ks[2], (DEPTH, DEC_BATCH, RET_HEADS, RET_DK, RET_DV), 0.1),
        "cache_swa_k": nrm(ks[3], (DEPTH, DEC_BATCH, WINDOW, SWA_KV_HEADS, SWA_HD), 1.0),
        "cache_swa_v": nrm(ks[4], (DEPTH, DEC_BATCH, WINDOW, SWA_KV_HEADS, SWA_HD), 1.0),
        "meta_tokens": nrm(ks[5], (N_META, D_MODEL), 1.0),
        "norm1_g": 1.0 + nrm(ks[6], (DEPTH, D_MODEL), 0.01),
        "w_in": nrm(ks[7], (DEPTH, D_MODEL, IN_WIDTH), D_MODEL ** -0.5),
        "q_norm_g": 1.0 + nrm(ks[8], (DEPTH, SWA_HD), 0.01),
        "k_norm_g": 1.0 + nrm(ks[9], (DEPTH, SWA_HD), 0.01),
        "ret_norm_g": 1.0 + nrm(ks[10], (DEPTH, RET_V), 0.01),
        "attn_sinks": nrm(ks[11], (DEPTH, SWA_HEADS), 0.5),
        "w_out": nrm(ks[12], (DEPTH, MIX_WIDTH, D_MODEL), MIX_WIDTH ** -0.5),
        "norm2_g": 1.0 + nrm(ks[13], (DEPTH, D_MODEL), 0.01),
        "w_group": nrm(ks[14], (DEPTH, D_MODEL, N_GROUPS), D_MODEL ** -0.5),
        "w_expert": nrm(ks[15], (DEPTH, D_MODEL, N_EXPERTS), D_MODEL ** -0.5),
        "w_gate": nrm(ks[16], (DEPTH, N_EXPERTS, D_MODEL, EXPERT_FF), D_MODEL ** -0.5),
        "w_up": nrm(ks[17], (DEPTH, N_EXPERTS, D_MODEL, EXPERT_FF), D_MODEL ** -0.5),
        "w_down": nrm(ks[18], (DEPTH, N_EXPERTS, EXPERT_FF, D_MODEL), EXPERT_FF ** -0.5),
    }


def reference(x_prompt, x_sample, cache_ret_state, cache_swa_k, cache_swa_v, meta_tokens,
              norm1_g, w_in, q_norm_g, k_norm_g, ret_norm_g, attn_sinks, w_out, norm2_g,
              w_group, w_expert, w_gate, w_up, w_down):
    b = x_prompt.shape[0]
    meta_pos = jnp.arange(N_META, dtype=jnp.int32)
    prompt_pos = N_META + jnp.arange(x_prompt.shape[1], dtype=jnp.int32)
    sample_pos = N_META + PAST_LEN + jnp.arange(x_sample.shape[1], dtype=jnp.int32)
    xp, xs, m = x_prompt, x_sample, meta_tokens[None]
    rs_p, k_p, v_p, rs_s, k_s, v_s = [], [], [], [], [], []
    for layer in range(DEPTH):
        lw = (norm1_g[layer], w_in[layer], q_norm_g[layer], k_norm_g[layer], ret_norm_g[layer],
              attn_sinks[layer], w_out[layer], norm2_g[layer], w_group[layer], w_expert[layer],
              w_gate[layer], w_up[layer], w_down[layer])
        ret_zero = jnp.zeros((1, RET_HEADS, RET_DK, RET_DV), m.dtype)
        _, m_rk, m_rv, _, _, m_sk, m_sv = project(rmsnorm(m, lw[0]), lw[1], lw[2], lw[3], meta_pos)
        s_meta = retention_update(m_rk, m_rv, ret_zero)
        meta_k, meta_v = m_sk[0], m_sv[0]
        hist0 = jnp.zeros((b, WINDOW, SWA_KV_HEADS, SWA_HD), xp.dtype)
        s0 = jnp.broadcast_to(s_meta, (b,) + s_meta.shape[1:])
        xp, sp, kp, vp = layer_forward(xp, prompt_pos, s0, hist0, hist0, False,
                                       meta_k, meta_v, True, *lw)
        xs, ss, ksn, vsn = layer_forward(xs, sample_pos, cache_ret_state[layer], cache_swa_k[layer],
                                         cache_swa_v[layer], True, meta_k, meta_v, True, *lw)
        if layer + 1 < DEPTH:
            mz = jnp.zeros((1, WINDOW, SWA_KV_HEADS, SWA_HD), m.dtype)
            m = layer_forward(m, meta_pos, ret_zero, mz, mz, False, jnp.zeros_like(meta_k),
                              jnp.zeros_like(meta_v), False, *lw)[0]
        rs_p.append(sp)
        k_p.append(kp)
        v_p.append(vp)
        rs_s.append(ss)
        k_s.append(ksn)
        v_s.append(vsn)
    return (xp, xs, jnp.stack(rs_p), jnp.stack(k_p), jnp.stack(v_p),
            jnp.stack(rs_s), jnp.stack(k_s), jnp.stack(v_s))
```

```python
import functools

import numpy as np
import jax
import jax.numpy as jnp
from jax import lax
from jax.experimental import pallas as pl
from jax.experimental.pallas import tpu as pltpu

F32 = jnp.float32
BF16 = jnp.bfloat16

D_MODEL = 1024
PAST_LEN = 4096
CHUNK = 64
N_META = 16
RET_HEADS = 4
RET_DK = 64
RET_DV = 128
SWA_HEADS = 8
SWA_KV_HEADS = 2
SWA_HD = 64
WINDOW = 128
ROPE_THETA = 10000.0
N_GROUPS = 4
EXPERTS_PER_GROUP = 8
N_EXPERTS = N_GROUPS * EXPERTS_PER_GROUP
EXPERT_FF = 256
EPS = 1e-6
NEG_INF = -1e30
RET_Q = RET_HEADS * RET_DK
RET_V = RET_HEADS * RET_DV
SWA_Q = SWA_HEADS * SWA_HD
SWA_KV = SWA_KV_HEADS * SWA_HD
MIX_WIDTH = RET_V + SWA_Q
IN_WIDTH = 2 * RET_Q + 2 * RET_V + SWA_Q + 2 * SWA_KV

LANES = 128
TOKEN_TILE = 512
ATTN_TILE = 256
EXPERT_TILE = 256
MOVE_TILE = 256
EXPERT_LANE0 = 32
META_ROWS = 64
VMEM_LIMIT = 56 * 1024 * 1024

_LOG_G = [float(np.log1p(-np.exp2(-5.0 - h))) for h in range(RET_HEADS)]


def _params(n_axes):
    return pltpu.CompilerParams(dimension_semantics=("arbitrary",) * n_axes, vmem_limit_bytes=VMEM_LIMIT)


def _rope(t, c, s1, s2):
    return t * c + pltpu.roll(t, 96, 1) * s1 + pltpu.roll(t, 32, 1) * s2


def _head_rms(t, g, lo_mask):
    ss = t * t
    lo = jnp.sum(jnp.where(lo_mask, ss, 0.0), axis=-1, keepdims=True)
    hi = jnp.sum(jnp.where(lo_mask, 0.0, ss), axis=-1, keepdims=True)
    ms = jnp.where(lo_mask, lo, hi) * (1.0 / SWA_HD)
    return t * lax.rsqrt(ms + EPS) * g


def _proj_kernel(x_ref, g1_ref, w_ref, qg_ref, kg_ref, cos_ref, s1_ref, s2_ref,
                 rqk_ref, rv_ref, gate_ref, sq_ref, skv_ref):
    x = x_ref[...]
    ms = jnp.mean(x * x, axis=-1, keepdims=True)
    xn = (x * lax.rsqrt(ms + EPS) * g1_ref[...]).astype(BF16)
    c, s1, s2 = cos_ref[...], s1_ref[...], s2_ref[...]
    lo_mask = lax.broadcasted_iota(jnp.int32, c.shape, 1) < SWA_HD

    def seg(a, b):
        return jnp.dot(xn, w_ref[:, a:b], preferred_element_type=F32)

    def tile(h, j):
        return h[:, j * LANES:(j + 1) * LANES]

    h = seg(0, 2 * RET_Q)
    for j in range(2):
        rqk_ref[:, j * LANES:(j + 1) * LANES] = _rope(tile(h, j), c, s1, s2).astype(BF16)
    for j in range(2, 4):
        rqk_ref[:, j * LANES:(j + 1) * LANES] = (_rope(tile(h, j), c, s1, s2) * (RET_DK ** -0.5)).astype(BF16)
    a = 2 * RET_Q
    rv_ref[...] = seg(a, a + RET_V).astype(BF16)
    a += RET_V
    g = seg(a, a + RET_V)
    gate_ref[...] = (g * jax.nn.sigmoid(g)).astype(BF16)
    a += RET_V
    h = seg(a, a + SWA_Q)
    qg = qg_ref[...]
    for j in range(SWA_Q // LANES):
        sq_ref[:, j * LANES:(j + 1) * LANES] = _rope(_head_rms(tile(h, j), qg, lo_mask), c, s1, s2).astype(BF16)
    a += SWA_Q
    h = seg(a, a + 2 * SWA_KV)
    skv_ref[:, 0:LANES] = _rope(_head_rms(tile(h, 0), kg_ref[...], lo_mask), c, s1, s2)
    skv_ref[:, LANES:2 * LANES] = tile(h, 1)


def _rope_tables(pos):
    half = SWA_HD // 2
    inv = ROPE_THETA ** (-jnp.arange(half, dtype=F32) / half)
    ang = pos.astype(F32)[:, None] * inv[None, :]
    cos, sin = jnp.cos(ang), jnp.sin(ang)
    z = jnp.zeros_like(sin)
    return (jnp.tile(cos, (1, 4)),
            jnp.tile(jnp.concatenate([-sin, z], axis=1), (1, 2)),
            jnp.tile(jnp.concatenate([z, sin], axis=1), (1, 2)))


def _proj(x2d, pos_rows, tm, g1, w_in_bf, qg2, kg2):
    t_rows = x2d.shape[0]
    n_tiles = t_rows // tm
    n_pos_tiles = pos_rows.shape[0] // tm
    cos, s1, s2 = _rope_tables(pos_rows)
    row = lambda i: (i, 0)
    const = lambda i: (0, 0)
    tab = lambda i: (i % n_pos_tiles, 0)
    return pl.pallas_call(
        _proj_kernel,
        grid=(n_tiles,),
        in_specs=[pl.BlockSpec((tm, D_MODEL), row),
                  pl.BlockSpec((1, D_MODEL), const),
                  pl.BlockSpec((D_MODEL, IN_WIDTH), const),
                  pl.BlockSpec((1, LANES), const),
                  pl.BlockSpec((1, LANES), const),
                  pl.BlockSpec((tm, LANES), tab),
                  pl.BlockSpec((tm, LANES), tab),
                  pl.BlockSpec((tm, LANES), tab)],
        out_specs=[pl.BlockSpec((tm, 2 * RET_Q), row),
                   pl.BlockSpec((tm, RET_V), row),
                   pl.BlockSpec((tm, RET_V), row),
                   pl.BlockSpec((tm, SWA_Q), row),
                   pl.BlockSpec((tm, 2 * SWA_KV), row)],
        out_shape=[jax.ShapeDtypeStruct((t_rows, 2 * RET_Q), BF16),
                   jax.ShapeDtypeStruct((t_rows, RET_V), BF16),
                   jax.ShapeDtypeStruct((t_rows, RET_V), BF16),
                   jax.ShapeDtypeStruct((t_rows, SWA_Q), BF16),
                   jax.ShapeDtypeStruct((t_rows, 2 * SWA_KV), F32)],
        compiler_params=_params(1),
        name="proj",
    )(x2d, g1, w_in_bf, qg2, kg2, cos, s1, s2)


def _pair_update(k_bf, v0_bf, v1_bf, wt):
    kw = (k_bf.astype(F32) * wt).astype(BF16)
    dn = (((0,), (0,)), ((), ()))
    a0 = lax.dot_general(kw, v0_bf, dn, preferred_element_type=F32)
    a1 = lax.dot_general(kw, v1_bf, dn, preferred_element_type=F32)
    top = lax.broadcasted_iota(jnp.int32, a0.shape, 0) < RET_DK
    return jnp.where(top, a0, a1)


def _decay_rows(n, pair, rows_back_from):
    i = lax.broadcasted_iota(jnp.int32, (n, LANES), 0).astype(F32)
    lane = lax.broadcasted_iota(jnp.int32, (n, LANES), 1)
    lg = jnp.where(lane < RET_DK, _LOG_G[2 * pair], _LOG_G[2 * pair + 1])
    return jnp.exp((rows_back_from - i) * lg)


def _meta_state_kernel(rqk_ref, rv_ref, s_ref, *, n_rows):
    for p in range(RET_HEADS // 2):
        k = rqk_ref[:, RET_Q + p * LANES:RET_Q + (p + 1) * LANES]
        wt = _decay_rows(n_rows, p, float(N_META - 1))
        s_ref[p] = _pair_update(k, rv_ref[:, (2 * p) * LANES:(2 * p + 1) * LANES],
                                rv_ref[:, (2 * p + 1) * LANES:(2 * p + 2) * LANES], wt)


def _meta_state(m_rqk, m_rv):
    n_rows = m_rqk.shape[0]
    return pl.pallas_call(
        functools.partial(_meta_state_kernel, n_rows=n_rows),
        out_shape=jax.ShapeDtypeStruct((RET_HEADS // 2, 2 * RET_DK, RET_DV), F32),
        name="meta_state",
    )(m_rqk, m_rv)


def _dup_halves(a, lo_mask):
    sw = pltpu.roll(a, SWA_HD, 1)
    return jnp.where(lo_mask, a, sw), jnp.where(lo_mask, sw, a)


def _attn_kernel(rqk_ref, rv_ref, gate_ref, sq_ref, skv_ref, meta_ref, hist_ref, s0_ref, rng_ref, sink_ref,
                 omix_ref, sout_ref, kvout_ref,
                 s_scr, kd_scr, vd_scr, mk_scr, mv_scr, dec_scr, wt_scr, cs_scr, gam_scr,
                 *, tl, has_hist):
    b = pl.program_id(0)
    t = pl.program_id(1)
    nt = pl.num_programs(1)
    n_chunks = tl // CHUNK
    n_pairs = RET_HEADS // 2
    lo_tl = lax.broadcasted_iota(jnp.int32, (tl, LANES), 1) < SWA_HD
    lo_c = lax.broadcasted_iota(jnp.int32, (CHUNK, LANES), 1) < SWA_HD

    @pl.when((b == 0) & (t == 0))
    def _tables():
        i = lax.broadcasted_iota(jnp.int32, (tl, tl), 0)
        j = lax.broadcasted_iota(jnp.int32, (tl, tl), 1)
        diff = (i - j).astype(F32)
        row = lax.broadcasted_iota(jnp.int32, (tl, LANES), 0).astype(F32)
        for h in range(RET_HEADS):
            dec_scr[h] = jnp.where(diff >= 0.0, jnp.exp(jnp.maximum(diff, 0.0) * _LOG_G[h]), 0.0)
            cs_scr[h] = jnp.exp((row + 1.0) * _LOG_G[h])
        top = lax.broadcasted_iota(jnp.int32, (2 * RET_DK, RET_DV), 0) < RET_DK
        for p in range(n_pairs):
            wt_scr[p] = _decay_rows(tl, p, float(tl - 1))
            gam_scr[p] = jnp.where(top, jnp.exp(jnp.float32(tl * _LOG_G[2 * p])), jnp.exp(jnp.float32(tl * _LOG_G[2 * p + 1])))
        lo_m = lax.broadcasted_iota(jnp.int32, (N_META, LANES), 1) < SWA_HD
        mk0, mk1 = _dup_halves(meta_ref[:, 0:LANES], lo_m)
        mv0, mv1 = _dup_halves(meta_ref[:, LANES:2 * LANES], lo_m)
        mk_scr[...] = jnp.zeros(mk_scr.shape, BF16)
        mv_scr[...] = jnp.zeros(mv_scr.shape, BF16)
        mk_scr[0, 0:N_META] = mk0.astype(BF16)
        mk_scr[1, 0:N_META] = mk1.astype(BF16)
        mv_scr[0, 0:N_META] = mv0.astype(BF16)
        mv_scr[1, 0:N_META] = mv1.astype(BF16)

    @pl.when(t == 0)
    def _stream_start():
        s_scr[...] = s0_ref[0]
        if has_hist:
            lo_w = lax.broadcasted_iota(jnp.int32, (WINDOW, LANES), 1) < SWA_HD
            k0, k1 = _dup_halves(hist_ref[0, :, 0:LANES], lo_w)
            v0, v1 = _dup_halves(hist_ref[0, :, LANES:2 * LANES], lo_w)
            kd_scr[0, 0:WINDOW] = k0.astype(BF16)
            kd_scr[1, 0:WINDOW] = k1.astype(BF16)
            vd_scr[0, 0:WINDOW] = v0.astype(BF16)
            vd_scr[1, 0:WINDOW] = v1.astype(BF16)
        else:
            z = jnp.zeros((WINDOW, LANES), BF16)
            for kv in range(SWA_KV_HEADS):
                kd_scr[kv, 0:WINDOW] = z
                vd_scr[kv, 0:WINDOW] = z

    k0, k1 = _dup_halves(skv_ref[:, 0:LANES], lo_tl)
    v0, v1 = _dup_halves(skv_ref[:, LANES:2 * LANES], lo_tl)
    kd_scr[0, WINDOW:WINDOW + tl] = k0.astype(BF16)
    kd_scr[1, WINDOW:WINDOW + tl] = k1.astype(BF16)
    vd_scr[0, WINDOW:WINDOW + tl] = v0.astype(BF16)
    vd_scr[1, WINDOW:WINDOW + tl] = v1.astype(BF16)

    band = WINDOW + CHUNK
    n_keys = META_ROWS + band
    col = lax.broadcasted_iota(jnp.int32, (4 * CHUNK, n_keys), 1)
    zero_c = jnp.zeros((CHUNK, LANES), BF16)
    for c in range(n_chunks):
        if has_hist:
            first_valid = META_ROWS
        else:
            first_valid = jnp.where(t == 0, max(META_ROWS + WINDOW - c * CHUNK, META_ROWS), META_ROWS)
        valid = (col < N_META) | (col >= first_valid)
        r0 = c * CHUNK
        for kv in range(SWA_KV_HEADS):
            keys = jnp.concatenate([mk_scr[kv], kd_scr[kv, r0:r0 + band]], axis=0)
            vals = jnp.concatenate([mv_scr[kv], vd_scr[kv, r0:r0 + band]], axis=0)
            qa = sq_ref[r0:r0 + CHUNK, (2 * kv) * LANES:(2 * kv + 1) * LANES]
            qb = sq_ref[r0:r0 + CHUNK, (2 * kv + 1) * LANES:(2 * kv + 2) * LANES]
            lhs = jnp.concatenate([jnp.where(lo_c, qa, zero_c), jnp.where(lo_c, zero_c, qa),
                                   jnp.where(lo_c, qb, zero_c), jnp.where(lo_c, zero_c, qb)], axis=0)
            s = lax.dot_general(lhs, keys, (((1,), (1,)), ((), ())), preferred_element_type=F32) * (SWA_HD ** -0.5)
            s = jnp.where(valid, s, NEG_INF)
            sink = sink_ref[kv][:, 0:1]
            m = jnp.maximum(jnp.max(s, axis=-1, keepdims=True), sink)
            e = jnp.exp(s - m)
            den = jnp.sum(e, axis=-1, keepdims=True) + jnp.exp(sink - m)
            p = (e * (1.0 / den)).astype(BF16)
            o = jnp.dot(p, vals, preferred_element_type=F32)
            oa = jnp.where(lo_c, o[0:CHUNK], o[CHUNK:2 * CHUNK])
            ob = jnp.where(lo_c, o[2 * CHUNK:3 * CHUNK], o[3 * CHUNK:4 * CHUNK])
            base = RET_V + (2 * kv) * LANES
            omix_ref[r0:r0 + CHUNK, base:base + LANES] = oa.astype(BF16)
            omix_ref[r0:r0 + CHUNK, base + LANES:base + 2 * LANES] = ob.astype(BF16)

    zero_t = jnp.zeros((tl, LANES), BF16)
    for p in range(n_pairs):
        q = rqk_ref[:, p * LANES:(p + 1) * LANES]
        k = rqk_ref[:, RET_Q + p * LANES:RET_Q + (p + 1) * LANES]
        lhs = jnp.concatenate([jnp.where(lo_tl, q, zero_t), jnp.where(lo_tl, zero_t, q)], axis=0)
        s = lax.dot_general(lhs, k, (((1,), (1,)), ((), ())), preferred_element_type=F32)
        cross = jnp.dot(lhs, s_scr[p].astype(BF16), preferred_element_type=F32)
        for i in range(2):
            h = 2 * p + i
            v = rv_ref[:, h * LANES:(h + 1) * LANES]
            a = (s[i * tl:(i + 1) * tl] * dec_scr[h]).astype(BF16)
            o = jnp.dot(a, v, preferred_element_type=F32) + cross[i * tl:(i + 1) * tl] * cs_scr[h]
            r = o * lax.rsqrt(jnp.mean(o * o, axis=-1, keepdims=True) + EPS) * rng_ref[h:h + 1, :]
            omix_ref[:, h * LANES:(h + 1) * LANES] = (r * gate_ref[:, h * LANES:(h + 1) * LANES].astype(F32)).astype(BF16)
        u = _pair_update(k, rv_ref[:, (2 * p) * LANES:(2 * p + 1) * LANES],
                         rv_ref[:, (2 * p + 1) * LANES:(2 * p + 2) * LANES], wt_scr[p])
        s_scr[p] = gam_scr[p] * s_scr[p] + u

    if tl >= WINDOW:
        @pl.when(t + 1 < nt)
        def _carry_window():
            for kv in range(SWA_KV_HEADS):
                kd_scr[kv, 0:WINDOW] = kd_scr[kv, tl:tl + WINDOW]
                vd_scr[kv, 0:WINDOW] = vd_scr[kv, tl:tl + WINDOW]

    @pl.when(t + 1 == nt)
    def _stream_end():
        sout_ref[0] = s_scr[...]
        if tl >= WINDOW:
            kvout_ref[0] = skv_ref[tl - WINDOW:tl, :]
        else:
            kvout_ref[0, 0:WINDOW - tl] = hist_ref[0, tl:WINDOW, :]
            kvout_ref[0, WINDOW - tl:WINDOW] = skv_ref[...]


def _attention(rqk, rv, gate, sq, skv, meta_kv, hist_kv, s0, rng, sink_tab, *, n_streams, seq, tl, has_hist):
    nt = seq // tl
    assert tl % CHUNK == 0 and seq % tl == 0
    assert tl >= WINDOW or (nt == 1 and has_hist)
    n_pairs = RET_HEADS // 2
    s0_shared = s0.shape[0] == 1
    row = lambda b, t: (b * nt + t, 0)
    const2 = lambda b, t: (0, 0)
    const3 = lambda b, t: (0, 0, 0)
    per_b3 = lambda b, t: (b, 0, 0)
    s0_map = (lambda b, t: (0, 0, 0, 0)) if s0_shared else (lambda b, t: (b, 0, 0, 0))
    hist_map = per_b3 if has_hist else const3
    rows = n_streams * seq
    return pl.pallas_call(
        functools.partial(_attn_kernel, tl=tl, has_hist=has_hist),
        grid=(n_streams, nt),
        in_specs=[pl.BlockSpec((tl, 2 * RET_Q), row),
                  pl.BlockSpec((tl, RET_V), row),
                  pl.BlockSpec((tl, RET_V), row),
                  pl.BlockSpec((tl, SWA_Q), row),
                  pl.BlockSpec((tl, 2 * SWA_KV), row),
                  pl.BlockSpec((N_META, 2 * SWA_KV), const2),
                  pl.BlockSpec((1, WINDOW, 2 * SWA_KV), hist_map),
                  pl.BlockSpec((1, n_pairs, 2 * RET_DK, RET_DV), s0_map),
                  pl.BlockSpec((RET_HEADS, RET_DV), const2),
                  pl.BlockSpec((SWA_KV_HEADS, 4 * CHUNK, LANES), const3)],
        out_specs=[pl.BlockSpec((tl, MIX_WIDTH), row),
                   pl.BlockSpec((1, n_pairs, 2 * RET_DK, RET_DV), lambda b, t: (b, 0, 0, 0)),
                   pl.BlockSpec((1, WINDOW, 2 * SWA_KV), per_b3)],
        out_shape=[jax.ShapeDtypeStruct((rows, MIX_WIDTH), BF16),
                   jax.ShapeDtypeStruct((n_streams, n_pairs, 2 * RET_DK, RET_DV), F32),
                   jax.ShapeDtypeStruct((n_streams, WINDOW, 2 * SWA_KV), F32)],
        scratch_shapes=[pltpu.VMEM((n_pairs, 2 * RET_DK, RET_DV), F32),
                        pltpu.VMEM((SWA_KV_HEADS, WINDOW + tl, LANES), BF16),
                        pltpu.VMEM((SWA_KV_HEADS, WINDOW + tl, LANES), BF16),
                        pltpu.VMEM((SWA_KV_HEADS, META_ROWS, LANES), BF16),
                        pltpu.VMEM((SWA_KV_HEADS, META_ROWS, LANES), BF16),
                        pltpu.VMEM((RET_HEADS, tl, tl), F32),
                        pltpu.VMEM((n_pairs, tl, LANES), F32),
                        pltpu.VMEM((RET_HEADS, tl, RET_DV), F32),
                        pltpu.VMEM((n_pairs, 2 * RET_DK, RET_DV), F32)],
        compiler_params=_params(2),
        name="attention",
    )(rqk, rv, gate, sq, skv, meta_kv, hist_kv, s0, rng, sink_tab)


def _lane_pick(lane, idx, val):
    return jnp.sum(jnp.where(lane == idx, val, 0.0), axis=-1, keepdims=True)


def _post_kernel(omix_ref, x_ref, wout_ref, g2_ref, wr_ref, base_ref,
                 xmid_ref, hn_ref, route_ref, cnt_ref, tri_scr, run_scr):
    i = pl.program_id(0)
    tm = x_ref.shape[0]

    @pl.when(i == 0)
    def _init():
        r = lax.broadcasted_iota(jnp.int32, (tm, tm), 0)
        c = lax.broadcasted_iota(jnp.int32, (tm, tm), 1)
        tri_scr[...] = jnp.where(c < r, 1.0, 0.0).astype(BF16)
        run_scr[...] = base_ref[...]

    xm = x_ref[...] + jnp.dot(omix_ref[...], wout_ref[...], preferred_element_type=F32)
    xmid_ref[...] = xm
    hn = xm * lax.rsqrt(jnp.mean(xm * xm, axis=-1, keepdims=True) + EPS) * g2_ref[...]
    hn_ref[...] = hn
    logits = jnp.dot(hn.astype(BF16), wr_ref[...], preferred_element_type=F32)

    lane = lax.broadcasted_iota(jnp.int32, logits.shape, 1)
    big = jnp.int32(LANES)
    gl = jnp.where(lane < N_GROUPS, logits, NEG_INF)
    gmax = jnp.max(gl, axis=-1, keepdims=True)
    gsum = jnp.sum(jnp.exp(gl - gmax), axis=-1, keepdims=True)
    g_sel = jnp.min(jnp.where(gl == gmax, lane, big), axis=-1, keepdims=True)
    p_sel = 1.0 / gsum
    e_lo = EXPERT_LANE0 + g_sel * EXPERTS_PER_GROUP
    el = jnp.where((lane >= e_lo) & (lane < e_lo + EXPERTS_PER_GROUP), logits, NEG_INF)
    m1 = jnp.max(el, axis=-1, keepdims=True)
    i1 = jnp.min(jnp.where(el == m1, lane, big), axis=-1, keepdims=True)
    el2 = jnp.where(lane == i1, NEG_INF, el)
    m2 = jnp.max(el2, axis=-1, keepdims=True)
    i2 = jnp.min(jnp.where(el2 == m2, lane, big), axis=-1, keepdims=True)
    e2 = jnp.exp(m2 - m1)
    inv = 1.0 / (1.0 + e2)
    w1 = p_sel * inv
    w2 = p_sel * (e2 * inv)

    hit = (lane == i1) | (lane == i2)
    pref = jnp.dot(tri_scr[...], jnp.where(hit, 1.0, 0.0).astype(BF16), preferred_element_type=F32) + run_scr[...]
    r1 = _lane_pick(lane, i1, pref)
    r2 = _lane_pick(lane, i2, pref)
    run_scr[...] = run_scr[...] + jnp.sum(jnp.where(hit, 1.0, 0.0), axis=0, keepdims=True)
    cnt_ref[...] = jnp.broadcast_to(run_scr[...], cnt_ref.shape)

    vals = [(i1 - EXPERT_LANE0).astype(F32), (i2 - EXPERT_LANE0).astype(F32), w1, w2, r1, r2]
    out = jnp.zeros(logits.shape, F32)
    for k, v in enumerate(vals):
        out = jnp.where(lane == k, v, out)
    route_ref[...] = out


def _post(omix, x2d, w_out_bf, g2, w_router_bf, base_cnt, tm):
    t_rows = x2d.shape[0]
    row = lambda i: (i, 0)
    const = lambda i: (0, 0)
    return pl.pallas_call(
        _post_kernel,
        grid=(t_rows // tm,),
        in_specs=[pl.BlockSpec((tm, MIX_WIDTH), row),
                  pl.BlockSpec((tm, D_MODEL), row),
                  pl.BlockSpec((MIX_WIDTH, D_MODEL), const),
                  pl.BlockSpec((1, D_MODEL), const),
                  pl.BlockSpec((D_MODEL, LANES), const),
                  pl.BlockSpec((1, LANES), const)],
        out_specs=[pl.BlockSpec((tm, D_MODEL), row),
                   pl.BlockSpec((tm, D_MODEL), row),
                   pl.BlockSpec((tm, LANES), row),
                   pl.BlockSpec((8, LANES), const)],
        out_shape=[jax.ShapeDtypeStruct((t_rows, D_MODEL), F32),
                   jax.ShapeDtypeStruct((t_rows, D_MODEL), F32),
                   jax.ShapeDtypeStruct((t_rows, LANES), F32),
                   jax.ShapeDtypeStruct((8, LANES), F32)],
        scratch_shapes=[pltpu.VMEM((tm, tm), BF16), pltpu.VMEM((1, LANES), F32)],
        compiler_params=_params(1),
        name="post",
    )(omix, x2d, w_out_bf, g2, w_router_bf, base_cnt)


def _row_copy(src_ref, src_row, dst_ref, dst_row, sem):
    return pltpu.make_async_copy(src_ref.at[pl.ds(src_row, 1)], dst_ref.at[pl.ds(dst_row, 1)], sem)


def _dispatch_kernel(pos_ref, pad_ref, *refs, tm, n_pad, group_steps):
    hn_refs = refs[:len(group_steps)]
    zero_ref, xs_ref, sem = refs[len(group_steps):]
    i = pl.program_id(0)

    first = 0
    for hn_ref, steps in zip(hn_refs, group_steps):
        @pl.when((i >= first) & (i < first + steps))
        def _rows(hn_ref=hn_ref, first=first):
            base = (i - first) * tm

            def body(r, carry):
                _row_copy(hn_ref, base + r, xs_ref, pos_ref[0, 0, 2 * r], sem).start()
                _row_copy(hn_ref, base + r, xs_ref, pos_ref[0, 0, 2 * r + 1], sem).start()
                return carry

            lax.fori_loop(0, tm, body, 0, unroll=8)
        first += steps

    def pad_body(j, carry):
        _row_copy(zero_ref, 0, xs_ref, pad_ref[0, 0, j], sem).start()
        return carry

    lax.fori_loop(0, n_pad, pad_body, 0, unroll=True)
    n_rows = 2 * tm + n_pad
    pltpu.make_async_copy(xs_ref.at[pl.ds(0, n_rows)], xs_ref.at[pl.ds(0, n_rows)], sem).wait()


def _dispatch(pos_list, pad, hn_list, zero_row, n_rows_out, tm):
    group_steps = tuple(hn.shape[0] // tm for hn in hn_list)
    n_steps = sum(group_steps)
    n_pad = pad.shape[0] // n_steps
    assert pad.shape[0] == n_steps * n_pad
    pos3 = jnp.concatenate([p.reshape(-1, 1, 2 * tm) for p in pos_list], axis=0)
    pad3 = pad.reshape(n_steps, 1, n_pad)
    smem = lambda n: pl.BlockSpec((1, 1, n), lambda i: (i, 0, 0), memory_space=pltpu.SMEM)
    anyspec = pl.BlockSpec(memory_space=pl.ANY)
    return pl.pallas_call(
        functools.partial(_dispatch_kernel, tm=tm, n_pad=n_pad, group_steps=group_steps),
        grid=(n_steps,),
        in_specs=[smem(2 * tm), smem(n_pad)] + [anyspec] * (len(hn_list) + 1),
        out_specs=anyspec,
        out_shape=jax.ShapeDtypeStruct((n_rows_out, D_MODEL), F32),
        scratch_shapes=[pltpu.SemaphoreType.DMA(())],
        compiler_params=_params(1),
        name="dispatch",
    )(pos3, pad3, *hn_list, zero_row)


def _expert_kernel(te_ref, nv_ref, x_ref, wgu_ref, wd_ref, y_ref):
    i = pl.program_id(0)

    @pl.when(i < nv_ref[0])
    def _compute():
        x = x_ref[...].astype(BF16)
        gu = jnp.dot(x, wgu_ref[0], preferred_element_type=F32)
        g = gu[:, 0:EXPERT_FF]
        a = (g * jax.nn.sigmoid(g) * gu[:, EXPERT_FF:2 * EXPERT_FF]).astype(BF16)
        y_ref[...] = jnp.dot(a, wd_ref[0], preferred_element_type=F32)

    @pl.when(i >= nv_ref[0])
    def _skip():
        y_ref[...] = jnp.zeros(y_ref.shape, F32)


def _experts(tile_expert, n_valid, xs, wgu_bf, wd_bf, n_tiles, tm):
    last = lambda i, nv: jnp.minimum(i, nv[0] - 1)
    return pl.pallas_call(
        _expert_kernel,
        grid_spec=pltpu.PrefetchScalarGridSpec(
            num_scalar_prefetch=2,
            grid=(n_tiles,),
            in_specs=[pl.BlockSpec((tm, D_MODEL), lambda i, te, nv: (last(i, nv), 0)),
                      pl.BlockSpec((1, D_MODEL, 2 * EXPERT_FF), lambda i, te, nv: (te[last(i, nv)], 0, 0)),
                      pl.BlockSpec((1, EXPERT_FF, D_MODEL), lambda i, te, nv: (te[last(i, nv)], 0, 0))],
            out_specs=pl.BlockSpec((tm, D_MODEL), lambda i, te, nv: (i, 0))),
        out_shape=jax.ShapeDtypeStruct((n_tiles * tm, D_MODEL), F32),
        compiler_params=_params(1),
        name="experts",
    )(tile_expert, n_valid, xs, wgu_bf, wd_bf)


def _combine_kernel(pos_ref, ys_ref, xmid_ref, route_ref, out_ref, ybuf, sem, *, tm):
    def body(r, carry):
        pltpu.make_async_copy(ys_ref.at[pl.ds(pos_ref[0, 0, 2 * r], 1)], ybuf.at[0, pl.ds(r, 1)], sem).start()
        pltpu.make_async_copy(ys_ref.at[pl.ds(pos_ref[0, 0, 2 * r + 1], 1)], ybuf.at[1, pl.ds(r, 1)], sem).start()
        return carry

    lax.fori_loop(0, tm, body, 0, unroll=8)
    for k in range(2):
        pltpu.make_async_copy(ys_ref.at[pl.ds(0, tm)], ybuf.at[k], sem).wait()
    w = route_ref[...]
    out_ref[...] = xmid_ref[...] + w[:, 2:3] * ybuf[0] + w[:, 3:4] * ybuf[1]


def _combine(pos, ys, xmid, route, tm):
    t_rows = xmid.shape[0]
    n_steps = t_rows // tm
    pos3 = pos.reshape(n_steps, 1, 2 * tm)
    row = lambda i: (i, 0)
    return pl.pallas_call(
        functools.partial(_combine_kernel, tm=tm),
        grid=(n_steps,),
        in_specs=[pl.BlockSpec((1, 1, 2 * tm), lambda i: (i, 0, 0), memory_space=pltpu.SMEM),
                  pl.BlockSpec(memory_space=pl.ANY),
                  pl.BlockSpec((tm, D_MODEL), row),
                  pl.BlockSpec((tm, LANES), row)],
        out_specs=pl.BlockSpec((tm, D_MODEL), row),
        out_shape=jax.ShapeDtypeStruct((t_rows, D_MODEL), F32),
        scratch_shapes=[pltpu.VMEM((2, tm, D_MODEL), F32), pltpu.SemaphoreType.DMA(())],
        compiler_params=_params(1),
        name="combine",
    )(pos3, ys, xmid, route)


def _bucket(ends, idx):
    n = jnp.sum((ends[None, :] <= idx[:, None]).astype(jnp.int32), axis=1)
    return jnp.minimum(n, ends.shape[0] - 1)


def _tile_for(rows, pref):
    tm = min(pref, rows)
    assert rows % tm == 0
    return tm


def kernel(x_prompt, x_sample, cache_ret_state, cache_swa_k, cache_swa_v, meta_tokens, norm1_g, w_in, q_norm_g,
           k_norm_g, ret_norm_g, attn_sinks, w_out, norm2_g, w_group, w_expert, w_gate, w_up, w_down):
    assert norm1_g.shape[0] == 1, "single-layer trunk"
    bp, lp, _ = x_prompt.shape
    bs, ls, _ = x_sample.shape
    n_pairs = RET_HEADS // 2

    g1 = norm1_g[0][None, :]
    g2 = norm2_g[0][None, :]
    w_in_bf = w_in[0].astype(BF16)
    w_out_bf = w_out[0].astype(BF16)
    qg2 = jnp.tile(q_norm_g[0], 2)[None, :]
    kg2 = jnp.tile(k_norm_g[0], 2)[None, :]
    rng = ret_norm_g[0].reshape(RET_HEADS, RET_DV)
    sink_tab = jnp.broadcast_to(jnp.repeat(attn_sinks[0], CHUNK).reshape(SWA_KV_HEADS, 4 * CHUNK, 1),
                                (SWA_KV_HEADS, 4 * CHUNK, LANES))
    w_router = jnp.zeros((D_MODEL, LANES), F32)
    w_router = w_router.at[:, 0:N_GROUPS].set(w_group[0]).at[:, EXPERT_LANE0:EXPERT_LANE0 + N_EXPERTS].set(w_expert[0])
    w_router_bf = w_router.astype(BF16)
    wgu_bf = jnp.concatenate([w_gate[0], w_up[0]], axis=-1).astype(BF16)
    wd_bf = w_down[0].astype(BF16)

    meta_rows = 2 * CHUNK
    m_pad = jnp.zeros((meta_rows, D_MODEL), F32).at[0:N_META].set(meta_tokens)
    m_rqk, m_rv, _, _, m_skv = _proj(m_pad, jnp.arange(meta_rows, dtype=jnp.int32), meta_rows, g1, w_in_bf, qg2, kg2)
    s_meta = _meta_state(m_rqk, m_rv)[None]
    meta_kv = m_skv[0:N_META]

    groups = [
        dict(x=x_prompt.reshape(bp * lp, D_MODEL), n=bp, seq=lp, pos0=N_META, has_hist=False, s0=s_meta,
             hist=jnp.zeros((1, WINDOW, 2 * SWA_KV), F32)),
        dict(x=x_sample.reshape(bs * ls, D_MODEL), n=bs, seq=ls, pos0=N_META + PAST_LEN, has_hist=True,
             s0=cache_ret_state[0].reshape(bs, n_pairs, 2 * RET_DK, RET_DV),
             hist=jnp.concatenate([cache_swa_k[0].reshape(bs, WINDOW, SWA_KV),
                                   cache_swa_v[0].reshape(bs, WINDOW, SWA_KV)], axis=-1)),
    ]

    base_cnt = jnp.zeros((1, LANES), F32)
    for g in groups:
        rows = g["n"] * g["seq"]
        tm = _tile_for(rows, TOKEN_TILE)
        pos = g["pos0"] + jnp.arange(g["seq"], dtype=jnp.int32)
        if g["seq"] < tm:
            assert tm % g["seq"] == 0
            pos = jnp.tile(pos, tm // g["seq"])
        else:
            assert g["seq"] % tm == 0
        rqk, rv, gate, sq, skv = _proj(g["x"], pos, tm, g1, w_in_bf, qg2, kg2)
        tl = min(ATTN_TILE, g["seq"])
        omix, s_out, kv_out = _attention(rqk, rv, gate, sq, skv, meta_kv, g["hist"], g["s0"], rng, sink_tab,
                                         n_streams=g["n"], seq=g["seq"], tl=tl, has_hist=g["has_hist"])
        xmid, hn, route, cnt = _post(omix, g["x"], w_out_bf, g2, w_router_bf, base_cnt, tm)
        base_cnt = cnt[0:1]
        g.update(xmid=xmid, hn=hn, route=route, s_out=s_out, kv_out=kv_out)

    te = EXPERT_TILE
    total_rows = sum(g["n"] * g["seq"] for g in groups)
    n_tiles = (2 * total_rows) // te + N_EXPERTS
    n_slots = n_tiles * te
    counts = base_cnt[0, EXPERT_LANE0:EXPERT_LANE0 + N_EXPERTS].astype(jnp.int32)
    tiles_e = (counts + te - 1) // te
    padded = tiles_e * te
    off = jnp.cumsum(padded) - padded
    tile_end = jnp.cumsum(tiles_e)
    n_valid = tile_end[-1:].astype(jnp.int32)
    tile_expert = _bucket(tile_end, jnp.arange(n_tiles, dtype=jnp.int32))
    pad_e = padded - counts
    pad_end = jnp.cumsum(pad_e)
    n_fill = n_slots - 2 * total_rows
    move_steps = sum((g["n"] * g["seq"]) // _tile_for(g["n"] * g["seq"], MOVE_TILE) for g in groups)
    fill_per_step = 8 * (-(-n_fill // (8 * move_steps)))
    j = jnp.arange(fill_per_step * move_steps, dtype=jnp.int32)
    e_of = _bucket(pad_end, j)
    in_expert = (off + counts)[e_of] + (j - (pad_end - pad_e)[e_of])
    in_tail = (off[-1] + padded[-1]) + (j - pad_end[-1])
    fill_slots = jnp.where(j < pad_end[-1], in_expert, jnp.where(j < n_fill, in_tail, n_slots + (j - n_fill)))
    fill_slots = fill_slots.astype(jnp.int32)
    n_rows_xs = n_slots + fill_per_step * move_steps - n_fill
    zero_row = jnp.zeros((1, D_MODEL), F32)

    for g in groups:
        eid = g["route"][:, 0:2].astype(jnp.int32)
        g["pos"] = (off[eid] + g["route"][:, 4:6].astype(jnp.int32)).astype(jnp.int32)
    xs = _dispatch([g["pos"] for g in groups], fill_slots, [g["hn"] for g in groups], zero_row, n_rows_xs, MOVE_TILE)

    ys = _experts(tile_expert, n_valid, xs, wgu_bf, wd_bf, n_tiles, te)

    outs = []
    for g in groups:
        rows = g["n"] * g["seq"]
        y = _combine(g["pos"], ys, g["xmid"], g["route"], _tile_for(rows, MOVE_TILE))
        outs.append(y.reshape(g["n"], g["seq"], D_MODEL))

    def caches(g):
        kv = g["kv_out"]
        k = kv[:, :, 0:SWA_KV].reshape(g["n"], WINDOW, SWA_KV_HEADS, SWA_HD)[None]
        v = kv[:, :, SWA_KV:2 * SWA_KV].reshape(g["n"], WINDOW, SWA_KV_HEADS, SWA_HD)[None]
        s = g["s_out"].reshape(g["n"], RET_HEADS, RET_DK, RET_DV)[None]
        return s, k, v

    sp, kp, vp = caches(groups[0])
    ss, ks, vs = caches(groups[1])
    return (outs[0], outs[1], sp, kp, vp, ss, ks, vs)
```

```python
import functools

import numpy as np
import jax
import jax.numpy as jnp
from jax import lax
from jax.experimental import pallas as pl
from jax.experimental.pallas import tpu as pltpu

F32 = jnp.float32
BF16 = jnp.bfloat16

D_MODEL = 1024
PAST_LEN = 4096
CHUNK = 64
N_META = 16
RET_HEADS = 4
RET_DK = 64
RET_DV = 128
SWA_HEADS = 8
SWA_KV_HEADS = 2
SWA_HD = 64
WINDOW = 128
ROPE_THETA = 10000.0
N_GROUPS = 4
EXPERTS_PER_GROUP = 8
N_EXPERTS = N_GROUPS * EXPERTS_PER_GROUP
EXPERT_FF = 256
EPS = 1e-6
NEG_INF = -1e30
RET_Q = RET_HEADS * RET_DK
RET_V = RET_HEADS * RET_DV
SWA_Q = SWA_HEADS * SWA_HD
SWA_KV = SWA_KV_HEADS * SWA_HD
MIX_WIDTH = RET_V + SWA_Q
IN_WIDTH = 2 * RET_Q + 2 * RET_V + SWA_Q + 2 * SWA_KV

LANES = 128
TOKEN_TILE = 512
ATTN_TILE = 256
EXPERT_TILE = 256
MOVE_TILE = 256
ISSUE_UNROLL = 8
EXPERT_LANE0 = 32
META_ROWS = 64
VMEM_LIMIT = 56 * 1024 * 1024

_LOG_G = [float(np.log1p(-np.exp2(-5.0 - h))) for h in range(RET_HEADS)]


def _params(n_axes):
    return pltpu.CompilerParams(dimension_semantics=("arbitrary",) * n_axes, vmem_limit_bytes=VMEM_LIMIT)


def _rope(t, c, s1, s2):
    return t * c + pltpu.roll(t, 96, 1) * s1 + pltpu.roll(t, 32, 1) * s2


def _head_rms(t, g, lo_mask):
    ss = t * t
    lo = jnp.sum(jnp.where(lo_mask, ss, 0.0), axis=-1, keepdims=True)
    hi = jnp.sum(jnp.where(lo_mask, 0.0, ss), axis=-1, keepdims=True)
    ms = jnp.where(lo_mask, lo, hi) * (1.0 / SWA_HD)
    return t * lax.rsqrt(ms + EPS) * g


def _proj_kernel(x_ref, g1_ref, w_ref, qg_ref, kg_ref, cos_ref, s1_ref, s2_ref,
                 rqk_ref, rv_ref, gate_ref, sq_ref, skv_ref):
    x = x_ref[...]
    ms = jnp.mean(x * x, axis=-1, keepdims=True)
    xn = (x * lax.rsqrt(ms + EPS) * g1_ref[...]).astype(BF16)
    c, s1, s2 = cos_ref[...], s1_ref[...], s2_ref[...]
    lo_mask = lax.broadcasted_iota(jnp.int32, c.shape, 1) < SWA_HD

    def seg(a, b):
        return jnp.dot(xn, w_ref[:, a:b], preferred_element_type=F32)

    def tile(h, j):
        return h[:, j * LANES:(j + 1) * LANES]

    h = seg(0, 2 * RET_Q)
    for j in range(2):
        rqk_ref[:, j * LANES:(j + 1) * LANES] = _rope(tile(h, j), c, s1, s2).astype(BF16)
    for j in range(2, 4):
        rqk_ref[:, j * LANES:(j + 1) * LANES] = (_rope(tile(h, j), c, s1, s2) * (RET_DK ** -0.5)).astype(BF16)
    a = 2 * RET_Q
    rv_ref[...] = seg(a, a + RET_V).astype(BF16)
    a += RET_V
    g = seg(a, a + RET_V)
    gate_ref[...] = (g * jax.nn.sigmoid(g)).astype(BF16)
    a += RET_V
    h = seg(a, a + SWA_Q)
    qg = qg_ref[...]
    for j in range(SWA_Q // LANES):
        sq_ref[:, j * LANES:(j + 1) * LANES] = _rope(_head_rms(tile(h, j), qg, lo_mask), c, s1, s2).astype(BF16)
    a += SWA_Q
    h = seg(a, a + 2 * SWA_KV)
    skv_ref[:, 0:LANES] = _rope(_head_rms(tile(h, 0), kg_ref[...], lo_mask), c, s1, s2)
    skv_ref[:, LANES:2 * LANES] = tile(h, 1)


def _rope_tables(pos):
    half = SWA_HD // 2
    inv = ROPE_THETA ** (-jnp.arange(half, dtype=F32) / half)
    ang = pos.astype(F32)[:, None] * inv[None, :]
    cos, sin = jnp.cos(ang), jnp.sin(ang)
    z = jnp.zeros_like(sin)
    return (jnp.tile(cos, (1, 4)),
            jnp.tile(jnp.concatenate([-sin, z], axis=1), (1, 2)),
            jnp.tile(jnp.concatenate([z, sin], axis=1), (1, 2)))


def _proj(x2d, pos_rows, tm, g1, w_in_bf, qg2, kg2):
    t_rows = x2d.shape[0]
    n_tiles = t_rows // tm
    n_pos_tiles = pos_rows.shape[0] // tm
    cos, s1, s2 = _rope_tables(pos_rows)
    row = lambda i: (i, 0)
    const = lambda i: (0, 0)
    tab = lambda i: (i % n_pos_tiles, 0)
    return pl.pallas_call(
        _proj_kernel,
        grid=(n_tiles,),
        in_specs=[pl.BlockSpec((tm, D_MODEL), row),
                  pl.BlockSpec((1, D_MODEL), const),
                  pl.BlockSpec((D_MODEL, IN_WIDTH), const),
                  pl.BlockSpec((1, LANES), const),
                  pl.BlockSpec((1, LANES), const),
                  pl.BlockSpec((tm, LANES), tab),
                  pl.BlockSpec((tm, LANES), tab),
                  pl.BlockSpec((tm, LANES), tab)],
        out_specs=[pl.BlockSpec((tm, 2 * RET_Q), row),
                   pl.BlockSpec((tm, RET_V), row),
                   pl.BlockSpec((tm, RET_V), row),
                   pl.BlockSpec((tm, SWA_Q), row),
                   pl.BlockSpec((tm, 2 * SWA_KV), row)],
        out_shape=[jax.ShapeDtypeStruct((t_rows, 2 * RET_Q), BF16),
                   jax.ShapeDtypeStruct((t_rows, RET_V), BF16),
                   jax.ShapeDtypeStruct((t_rows, RET_V), BF16),
                   jax.ShapeDtypeStruct((t_rows, SWA_Q), BF16),
                   jax.ShapeDtypeStruct((t_rows, 2 * SWA_KV), F32)],
        compiler_params=_params(1),
        name="proj",
    )(x2d, g1, w_in_bf, qg2, kg2, cos, s1, s2)


def _pair_update(k_bf, v0_bf, v1_bf, wt):
    kw = (k_bf.astype(F32) * wt).astype(BF16)
    dn = (((0,), (0,)), ((), ()))
    a0 = lax.dot_general(kw, v0_bf, dn, preferred_element_type=F32)
    a1 = lax.dot_general(kw, v1_bf, dn, preferred_element_type=F32)
    top = lax.broadcasted_iota(jnp.int32, a0.shape, 0) < RET_DK
    return jnp.where(top, a0, a1)


def _decay_rows(n, pair, rows_back_from):
    i = lax.broadcasted_iota(jnp.int32, (n, LANES), 0).astype(F32)
    lane = lax.broadcasted_iota(jnp.int32, (n, LANES), 1)
    lg = jnp.where(lane < RET_DK, _LOG_G[2 * pair], _LOG_G[2 * pair + 1])
    return jnp.exp((rows_back_from - i) * lg)


def _meta_state_kernel(rqk_ref, rv_ref, s_ref, *, n_rows):
    for p in range(RET_HEADS // 2):
        k = rqk_ref[:, RET_Q + p * LANES:RET_Q + (p + 1) * LANES]
        wt = _decay_rows(n_rows, p, float(N_META - 1))
        s_ref[p] = _pair_update(k, rv_ref[:, (2 * p) * LANES:(2 * p + 1) * LANES],
                                rv_ref[:, (2 * p + 1) * LANES:(2 * p + 2) * LANES], wt)


def _meta_state(m_rqk, m_rv):
    n_rows = m_rqk.shape[0]
    return pl.pallas_call(
        functools.partial(_meta_state_kernel, n_rows=n_rows),
        out_shape=jax.ShapeDtypeStruct((RET_HEADS // 2, 2 * RET_DK, RET_DV), F32),
        name="meta_state",
    )(m_rqk, m_rv)


def _dup_halves(a, lo_mask):
    sw = pltpu.roll(a, SWA_HD, 1)
    return jnp.where(lo_mask, a, sw), jnp.where(lo_mask, sw, a)


def _attn_kernel(rqk_ref, rv_ref, gate_ref, sq_ref, skv_ref, meta_ref, hist_ref, s0_ref, rng_ref, sink_ref,
                 omix_ref, sout_ref, kvout_ref,
                 s_scr, kd_scr, vd_scr, mk_scr, mv_scr, dec_scr, wt_scr, cs_scr, gam_scr,
                 *, tl, has_hist):
    b = pl.program_id(0)
    t = pl.program_id(1)
    nt = pl.num_programs(1)
    n_chunks = tl // CHUNK
    n_pairs = RET_HEADS // 2
    lo_tl = lax.broadcasted_iota(jnp.int32, (tl, LANES), 1) < SWA_HD
    lo_c = lax.broadcasted_iota(jnp.int32, (CHUNK, LANES), 1) < SWA_HD

    @pl.when((b == 0) & (t == 0))
    def _tables():
        i = lax.broadcasted_iota(jnp.int32, (tl, tl), 0)
        j = lax.broadcasted_iota(jnp.int32, (tl, tl), 1)
        diff = (i - j).astype(F32)
        row = lax.broadcasted_iota(jnp.int32, (tl, LANES), 0).astype(F32)
        for h in range(RET_HEADS):
            dec_scr[h] = jnp.where(diff >= 0.0, jnp.exp(jnp.maximum(diff, 0.0) * _LOG_G[h]), 0.0)
            cs_scr[h] = jnp.exp((row + 1.0) * _LOG_G[h])
        top = lax.broadcasted_iota(jnp.int32, (2 * RET_DK, RET_DV), 0) < RET_DK
        for p in range(n_pairs):
            wt_scr[p] = _decay_rows(tl, p, float(tl - 1))
            gam_scr[p] = jnp.where(top, jnp.exp(jnp.float32(tl * _LOG_G[2 * p])), jnp.exp(jnp.float32(tl * _LOG_G[2 * p + 1])))
        lo_m = lax.broadcasted_iota(jnp.int32, (N_META, LANES), 1) < SWA_HD
        mk0, mk1 = _dup_halves(meta_ref[:, 0:LANES], lo_m)
        mv0, mv1 = _dup_halves(meta_ref[:, LANES:2 * LANES], lo_m)
        mk_scr[...] = jnp.zeros(mk_scr.shape, BF16)
        mv_scr[...] = jnp.zeros(mv_scr.shape, BF16)
        mk_scr[0, 0:N_META] = mk0.astype(BF16)
        mk_scr[1, 0:N_META] = mk1.astype(BF16)
        mv_scr[0, 0:N_META] = mv0.astype(BF16)
        mv_scr[1, 0:N_META] = mv1.astype(BF16)

    @pl.when(t == 0)
    def _stream_start():
        s_scr[...] = s0_ref[0]
        if has_hist:
            lo_w = lax.broadcasted_iota(jnp.int32, (WINDOW, LANES), 1) < SWA_HD
            k0, k1 = _dup_halves(hist_ref[0, :, 0:LANES], lo_w)
            v0, v1 = _dup_halves(hist_ref[0, :, LANES:2 * LANES], lo_w)
            kd_scr[0, 0:WINDOW] = k0.astype(BF16)
            kd_scr[1, 0:WINDOW] = k1.astype(BF16)
            vd_scr[0, 0:WINDOW] = v0.astype(BF16)
            vd_scr[1, 0:WINDOW] = v1.astype(BF16)
        else:
            z = jnp.zeros((WINDOW, LANES), BF16)
            for kv in range(SWA_KV_HEADS):
                kd_scr[kv, 0:WINDOW] = z
                vd_scr[kv, 0:WINDOW] = z

    k0, k1 = _dup_halves(skv_ref[:, 0:LANES], lo_tl)
    v0, v1 = _dup_halves(skv_ref[:, LANES:2 * LANES], lo_tl)
    kd_scr[0, WINDOW:WINDOW + tl] = k0.astype(BF16)
    kd_scr[1, WINDOW:WINDOW + tl] = k1.astype(BF16)
    vd_scr[0, WINDOW:WINDOW + tl] = v0.astype(BF16)
    vd_scr[1, WINDOW:WINDOW + tl] = v1.astype(BF16)

    band = WINDOW + CHUNK
    n_keys = META_ROWS + band
    col = lax.broadcasted_iota(jnp.int32, (4 * CHUNK, n_keys), 1)
    zero_c = jnp.zeros((CHUNK, LANES), BF16)
    for c in range(n_chunks):
        if has_hist:
            first_valid = META_ROWS
        else:
            first_valid = jnp.where(t == 0, max(META_ROWS + WINDOW - c * CHUNK, META_ROWS), META_ROWS)
        valid = (col < N_META) | (col >= first_valid)
        r0 = c * CHUNK
        for kv in range(SWA_KV_HEADS):
            keys = jnp.concatenate([mk_scr[kv], kd_scr[kv, r0:r0 + band]], axis=0)
            vals = jnp.concatenate([mv_scr[kv], vd_scr[kv, r0:r0 + band]], axis=0)
            qa = sq_ref[r0:r0 + CHUNK, (2 * kv) * LANES:(2 * kv + 1) * LANES]
            qb = sq_ref[r0:r0 + CHUNK, (2 * kv + 1) * LANES:(2 * kv + 2) * LANES]
            lhs = jnp.concatenate([jnp.where(lo_c, qa, zero_c), jnp.where(lo_c, zero_c, qa),
                                   jnp.where(lo_c, qb, zero_c), jnp.where(lo_c, zero_c, qb)], axis=0)
            s = lax.dot_general(lhs, keys, (((1,), (1,)), ((), ())), preferred_element_type=F32) * (SWA_HD ** -0.5)
            s = jnp.where(valid, s, NEG_INF)
            sink = sink_ref[kv][:, 0:1]
            m = jnp.maximum(jnp.max(s, axis=-1, keepdims=True), sink)
            e = jnp.exp(s - m)
            den = jnp.sum(e, axis=-1, keepdims=True) + jnp.exp(sink - m)
            p = (e * (1.0 / den)).astype(BF16)
            o = jnp.dot(p, vals, preferred_element_type=F32)
            oa = jnp.where(lo_c, o[0:CHUNK], o[CHUNK:2 * CHUNK])
            ob = jnp.where(lo_c, o[2 * CHUNK:3 * CHUNK], o[3 * CHUNK:4 * CHUNK])
            base = RET_V + (2 * kv) * LANES
            omix_ref[r0:r0 + CHUNK, base:base + LANES] = oa.astype(BF16)
            omix_ref[r0:r0 + CHUNK, base + LANES:base + 2 * LANES] = ob.astype(BF16)

    zero_t = jnp.zeros((tl, LANES), BF16)
    for p in range(n_pairs):
        q = rqk_ref[:, p * LANES:(p + 1) * LANES]
        k = rqk_ref[:, RET_Q + p * LANES:RET_Q + (p + 1) * LANES]
        lhs = jnp.concatenate([jnp.where(lo_tl, q, zero_t), jnp.where(lo_tl, zero_t, q)], axis=0)
        s = lax.dot_general(lhs, k, (((1,), (1,)), ((), ())), preferred_element_type=F32)
        cross = jnp.dot(lhs, s_scr[p].astype(BF16), preferred_element_type=F32)
        for i in range(2):
            h = 2 * p + i
            v = rv_ref[:, h * LANES:(h + 1) * LANES]
            a = (s[i * tl:(i + 1) * tl] * dec_scr[h]).astype(BF16)
            o = jnp.dot(a, v, preferred_element_type=F32) + cross[i * tl:(i + 1) * tl] * cs_scr[h]
            r = o * lax.rsqrt(jnp.mean(o * o, axis=-1, keepdims=True) + EPS) * rng_ref[h:h + 1, :]
            omix_ref[:, h * LANES:(h + 1) * LANES] = (r * gate_ref[:, h * LANES:(h + 1) * LANES].astype(F32)).astype(BF16)
        u = _pair_update(k, rv_ref[:, (2 * p) * LANES:(2 * p + 1) * LANES],
                         rv_ref[:, (2 * p + 1) * LANES:(2 * p + 2) * LANES], wt_scr[p])
        s_scr[p] = gam_scr[p] * s_scr[p] + u

    if tl >= WINDOW:
        @pl.when(t + 1 < nt)
        def _carry_window():
            for kv in range(SWA_KV_HEADS):
                kd_scr[kv, 0:WINDOW] = kd_scr[kv, tl:tl + WINDOW]
                vd_scr[kv, 0:WINDOW] = vd_scr[kv, tl:tl + WINDOW]

    @pl.when(t + 1 == nt)
    def _stream_end():
        sout_ref[0] = s_scr[...]
        if tl >= WINDOW:
            kvout_ref[0] = skv_ref[tl - WINDOW:tl, :]
        else:
            kvout_ref[0, 0:WINDOW - tl] = hist_ref[0, tl:WINDOW, :]
            kvout_ref[0, WINDOW - tl:WINDOW] = skv_ref[...]


def _attention(rqk, rv, gate, sq, skv, meta_kv, hist_kv, s0, rng, sink_tab, *, n_streams, seq, tl, has_hist):
    nt = seq // tl
    assert tl % CHUNK == 0 and seq % tl == 0
    assert tl >= WINDOW or (nt == 1 and has_hist)
    n_pairs = RET_HEADS // 2
    s0_shared = s0.shape[0] == 1
    row = lambda b, t: (b * nt + t, 0)
    const2 = lambda b, t: (0, 0)
    const3 = lambda b, t: (0, 0, 0)
    per_b3 = lambda b, t: (b, 0, 0)
    s0_map = (lambda b, t: (0, 0, 0, 0)) if s0_shared else (lambda b, t: (b, 0, 0, 0))
    hist_map = per_b3 if has_hist else const3
    rows = n_streams * seq
    return pl.pallas_call(
        functools.partial(_attn_kernel, tl=tl, has_hist=has_hist),
        grid=(n_streams, nt),
        in_specs=[pl.BlockSpec((tl, 2 * RET_Q), row),
                  pl.BlockSpec((tl, RET_V), row),
                  pl.BlockSpec((tl, RET_V), row),
                  pl.BlockSpec((tl, SWA_Q), row),
                  pl.BlockSpec((tl, 2 * SWA_KV), row),
                  pl.BlockSpec((N_META, 2 * SWA_KV), const2),
                  pl.BlockSpec((1, WINDOW, 2 * SWA_KV), hist_map),
                  pl.BlockSpec((1, n_pairs, 2 * RET_DK, RET_DV), s0_map),
                  pl.BlockSpec((RET_HEADS, RET_DV), const2),
                  pl.BlockSpec((SWA_KV_HEADS, 4 * CHUNK, LANES), const3)],
        out_specs=[pl.BlockSpec((tl, MIX_WIDTH), row),
                   pl.BlockSpec((1, n_pairs, 2 * RET_DK, RET_DV), lambda b, t: (b, 0, 0, 0)),
                   pl.BlockSpec((1, WINDOW, 2 * SWA_KV), per_b3)],
        out_shape=[jax.ShapeDtypeStruct((rows, MIX_WIDTH), BF16),
                   jax.ShapeDtypeStruct((n_streams, n_pairs, 2 * RET_DK, RET_DV), F32),
                   jax.ShapeDtypeStruct((n_streams, WINDOW, 2 * SWA_KV), F32)],
        scratch_shapes=[pltpu.VMEM((n_pairs, 2 * RET_DK, RET_DV), F32),
                        pltpu.VMEM((SWA_KV_HEADS, WINDOW + tl, LANES), BF16),
                        pltpu.VMEM((SWA_KV_HEADS, WINDOW + tl, LANES), BF16),
                        pltpu.VMEM((SWA_KV_HEADS, META_ROWS, LANES), BF16),
                        pltpu.VMEM((SWA_KV_HEADS, META_ROWS, LANES), BF16),
                        pltpu.VMEM((RET_HEADS, tl, tl), F32),
                        pltpu.VMEM((n_pairs, tl, LANES), F32),
                        pltpu.VMEM((RET_HEADS, tl, RET_DV), F32),
                        pltpu.VMEM((n_pairs, 2 * RET_DK, RET_DV), F32)],
        compiler_params=_params(2),
        name="attention",
    )(rqk, rv, gate, sq, skv, meta_kv, hist_kv, s0, rng, sink_tab)


def _lane_pick(lane, idx, val):
    return jnp.sum(jnp.where(lane == idx, val, 0.0), axis=-1, keepdims=True)


def _post_kernel(omix_ref, x_ref, wout_ref, g2_ref, wr_ref, base_ref,
                 xmid_ref, hn_ref, route_ref, cnt_ref, tri_scr, run_scr):
    i = pl.program_id(0)
    tm = x_ref.shape[0]

    @pl.when(i == 0)
    def _init():
        r = lax.broadcasted_iota(jnp.int32, (tm, tm), 0)
        c = lax.broadcasted_iota(jnp.int32, (tm, tm), 1)
        tri_scr[...] = jnp.where(c < r, 1.0, 0.0).astype(BF16)
        run_scr[...] = base_ref[...]

    xm = x_ref[...] + jnp.dot(omix_ref[...], wout_ref[...], preferred_element_type=F32)
    xmid_ref[...] = xm
    hn = xm * lax.rsqrt(jnp.mean(xm * xm, axis=-1, keepdims=True) + EPS) * g2_ref[...]
    hn_ref[...] = hn
    logits = jnp.dot(hn.astype(BF16), wr_ref[...], preferred_element_type=F32)

    lane = lax.broadcasted_iota(jnp.int32, logits.shape, 1)
    big = jnp.int32(LANES)
    gl = jnp.where(lane < N_GROUPS, logits, NEG_INF)
    gmax = jnp.max(gl, axis=-1, keepdims=True)
    gsum = jnp.sum(jnp.exp(gl - gmax), axis=-1, keepdims=True)
    g_sel = jnp.min(jnp.where(gl == gmax, lane, big), axis=-1, keepdims=True)
    p_sel = 1.0 / gsum
    e_lo = EXPERT_LANE0 + g_sel * EXPERTS_PER_GROUP
    el = jnp.where((lane >= e_lo) & (lane < e_lo + EXPERTS_PER_GROUP), logits, NEG_INF)
    m1 = jnp.max(el, axis=-1, keepdims=True)
    i1 = jnp.min(jnp.where(el == m1, lane, big), axis=-1, keepdims=True)
    el2 = jnp.where(lane == i1, NEG_INF, el)
    m2 = jnp.max(el2, axis=-1, keepdims=True)
    i2 = jnp.min(jnp.where(el2 == m2, lane, big), axis=-1, keepdims=True)
    e2 = jnp.exp(m2 - m1)
    inv = 1.0 / (1.0 + e2)
    w1 = p_sel * inv
    w2 = p_sel * (e2 * inv)

    hit = (lane == i1) | (lane == i2)
    pref = jnp.dot(tri_scr[...], jnp.where(hit, 1.0, 0.0).astype(BF16), preferred_element_type=F32) + run_scr[...]
    r1 = _lane_pick(lane, i1, pref)
    r2 = _lane_pick(lane, i2, pref)
    run_scr[...] = run_scr[...] + jnp.sum(jnp.where(hit, 1.0, 0.0), axis=0, keepdims=True)
    cnt_ref[...] = jnp.broadcast_to(run_scr[...], cnt_ref.shape)

    vals = [(i1 - EXPERT_LANE0).astype(F32), (i2 - EXPERT_LANE0).astype(F32), w1, w2, r1, r2]
    out = jnp.zeros(logits.shape, F32)
    for k, v in enumerate(vals):
        out = jnp.where(lane == k, v, out)
    route_ref[...] = out


def _post(omix, x2d, w_out_bf, g2, w_router_bf, base_cnt, tm):
    t_rows = x2d.shape[0]
    row = lambda i: (i, 0)
    const = lambda i: (0, 0)
    return pl.pallas_call(
        _post_kernel,
        grid=(t_rows // tm,),
        in_specs=[pl.BlockSpec((tm, MIX_WIDTH), row),
                  pl.BlockSpec((tm, D_MODEL), row),
                  pl.BlockSpec((MIX_WIDTH, D_MODEL), const),
                  pl.BlockSpec((1, D_MODEL), const),
                  pl.BlockSpec((D_MODEL, LANES), const),
                  pl.BlockSpec((1, LANES), const)],
        out_specs=[pl.BlockSpec((tm, D_MODEL), row),
                   pl.BlockSpec((tm, D_MODEL), row),
                   pl.BlockSpec((tm, LANES), row),
                   pl.BlockSpec((8, LANES), const)],
        out_shape=[jax.ShapeDtypeStruct((t_rows, D_MODEL), F32),
                   jax.ShapeDtypeStruct((t_rows, D_MODEL), F32),
                   jax.ShapeDtypeStruct((t_rows, LANES), F32),
                   jax.ShapeDtypeStruct((8, LANES), F32)],
        scratch_shapes=[pltpu.VMEM((tm, tm), BF16), pltpu.VMEM((1, LANES), F32)],
        compiler_params=_params(1),
        name="post",
    )(omix, x2d, w_out_bf, g2, w_router_bf, base_cnt)


def _row_copy(src_ref, src_row, dst_ref, dst_row, sem):
    return pltpu.make_async_copy(src_ref.at[pl.ds(src_row, 1)], dst_ref.at[pl.ds(dst_row, 1)], sem)


def _dispatch_kernel(pos_ref, pad_ref, *refs, tm, n_pad, group_steps):
    hn_refs = refs[:len(group_steps)]
    xs_ref, zero_scr, sem = refs[len(group_steps):]
    i = pl.program_id(0)

    @pl.when(i == 0)
    def _init():
        zero_scr[...] = jnp.zeros(zero_scr.shape, F32)

    first = 0
    for hn_ref, steps in zip(hn_refs, group_steps):
        @pl.when((i >= first) & (i < first + steps))
        def _rows(hn_ref=hn_ref):
            def body(blk, carry):
                for k in range(ISSUE_UNROLL):
                    r = blk * ISSUE_UNROLL + k
                    _row_copy(hn_ref, r, xs_ref, pos_ref[0, 0, 2 * r], sem).start(priority=0)
                    _row_copy(hn_ref, r, xs_ref, pos_ref[0, 0, 2 * r + 1], sem).start(priority=1)
                return carry

            lax.fori_loop(0, tm // ISSUE_UNROLL, body, 0)
        first += steps

    for j in range(n_pad):
        _row_copy(zero_scr, 0, xs_ref, pad_ref[0, 0, j], sem).start(priority=j % 2)
    n_rows = 2 * tm + n_pad
    pltpu.make_async_copy(xs_ref.at[pl.ds(0, n_rows)], xs_ref.at[pl.ds(0, n_rows)], sem).wait()


def _dispatch(pos_list, pad, hn_list, n_rows_out, tm):
    group_steps = tuple(hn.shape[0] // tm for hn in hn_list)
    n_steps = sum(group_steps)
    n_pad = pad.shape[0] // n_steps
    assert pad.shape[0] == n_steps * n_pad and tm % ISSUE_UNROLL == 0
    pos3 = jnp.concatenate([p.reshape(-1, 1, 2 * tm) for p in pos_list], axis=0)
    pad3 = pad.reshape(n_steps, 1, n_pad)
    smem = lambda n: pl.BlockSpec((1, 1, n), lambda i: (i, 0, 0), memory_space=pltpu.SMEM)
    hn_specs = []
    first = 0
    for steps in group_steps:
        hn_specs.append(pl.BlockSpec((tm, D_MODEL),
                                     lambda i, first=first, steps=steps: (jnp.clip(i - first, 0, steps - 1), 0)))
        first += steps
    return pl.pallas_call(
        functools.partial(_dispatch_kernel, tm=tm, n_pad=n_pad, group_steps=group_steps),
        grid=(n_steps,),
        in_specs=[smem(2 * tm), smem(n_pad)] + hn_specs,
        out_specs=pl.BlockSpec(memory_space=pl.ANY),
        out_shape=jax.ShapeDtypeStruct((n_rows_out, D_MODEL), F32),
        scratch_shapes=[pltpu.VMEM((8, D_MODEL), F32), pltpu.SemaphoreType.DMA(())],
        compiler_params=_params(1),
        name="dispatch",
    )(pos3, pad3, *hn_list)


def _expert_kernel(te_ref, nv_ref, x_ref, wgu_ref, wd_ref, y_ref):
    i = pl.program_id(0)

    @pl.when(i < nv_ref[0])
    def _compute():
        x = x_ref[...].astype(BF16)
        gu = jnp.dot(x, wgu_ref[0], preferred_element_type=F32)
        g = gu[:, 0:EXPERT_FF]
        a = (g * jax.nn.sigmoid(g) * gu[:, EXPERT_FF:2 * EXPERT_FF]).astype(BF16)
        y_ref[...] = jnp.dot(a, wd_ref[0], preferred_element_type=F32)

    @pl.when(i >= nv_ref[0])
    def _skip():
        y_ref[...] = jnp.zeros(y_ref.shape, F32)


def _experts(tile_expert, n_valid, xs, wgu_bf, wd_bf, n_tiles, tm):
    last = lambda i, nv: jnp.minimum(i, nv[0] - 1)
    return pl.pallas_call(
        _expert_kernel,
        grid_spec=pltpu.PrefetchScalarGridSpec(
            num_scalar_prefetch=2,
            grid=(n_tiles,),
            in_specs=[pl.BlockSpec((tm, D_MODEL), lambda i, te, nv: (last(i, nv), 0)),
                      pl.BlockSpec((1, D_MODEL, 2 * EXPERT_FF), lambda i, te, nv: (te[last(i, nv)], 0, 0)),
                      pl.BlockSpec((1, EXPERT_FF, D_MODEL), lambda i, te, nv: (te[last(i, nv)], 0, 0))],
            out_specs=pl.BlockSpec((tm, D_MODEL), lambda i, te, nv: (i, 0))),
        out_shape=jax.ShapeDtypeStruct((n_tiles * tm, D_MODEL), F32),
        compiler_params=_params(1),
        name="experts",
    )(tile_expert, n_valid, xs, wgu_bf, wd_bf)


def _combine_kernel(pos_ref, ys_ref, xmid_ref, route_ref, out_ref, ybuf, sem, *, tm):
    def body(blk, carry):
        for k in range(ISSUE_UNROLL):
            r = blk * ISSUE_UNROLL + k
            for e in range(2):
                pltpu.make_async_copy(ys_ref.at[pl.ds(pos_ref[0, 0, 2 * r + e], 1)], ybuf.at[e, pl.ds(r, 1)],
                                      sem).start(priority=e)
        return carry

    lax.fori_loop(0, tm // ISSUE_UNROLL, body, 0)
    for k in range(2):
        pltpu.make_async_copy(ys_ref.at[pl.ds(0, tm)], ybuf.at[k], sem).wait()
    w = route_ref[...]
    out_ref[...] = xmid_ref[...] + w[:, 2:3] * ybuf[0] + w[:, 3:4] * ybuf[1]


def _combine(pos, ys, xmid, route, tm):
    t_rows = xmid.shape[0]
    n_steps = t_rows // tm
    pos3 = pos.reshape(n_steps, 1, 2 * tm)
    row = lambda i: (i, 0)
    return pl.pallas_call(
        functools.partial(_combine_kernel, tm=tm),
        grid=(n_steps,),
        in_specs=[pl.BlockSpec((1, 1, 2 * tm), lambda i: (i, 0, 0), memory_space=pltpu.SMEM),
                  pl.BlockSpec(memory_space=pl.ANY),
                  pl.BlockSpec((tm, D_MODEL), row),
                  pl.BlockSpec((tm, LANES), row)],
        out_specs=pl.BlockSpec((tm, D_MODEL), row),
        out_shape=jax.ShapeDtypeStruct((t_rows, D_MODEL), F32),
        scratch_shapes=[pltpu.VMEM((2, tm, D_MODEL), F32), pltpu.SemaphoreType.DMA(())],
        compiler_params=_params(1),
        name="combine",
    )(pos3, ys, xmid, route)


def _bucket(ends, idx):
    n = jnp.sum((ends[None, :] <= idx[:, None]).astype(jnp.int32), axis=1)
    return jnp.minimum(n, ends.shape[0] - 1)


def _tile_for(rows, pref):
    tm = min(pref, rows)
    assert rows % tm == 0
    return tm


def kernel(x_prompt, x_sample, cache_ret_state, cache_swa_k, cache_swa_v, meta_tokens, norm1_g, w_in, q_norm_g,
           k_norm_g, ret_norm_g, attn_sinks, w_out, norm2_g, w_group, w_expert, w_gate, w_up, w_down):
    assert norm1_g.shape[0] == 1, "single-layer trunk"
    bp, lp, _ = x_prompt.shape
    bs, ls, _ = x_sample.shape
    n_pairs = RET_HEADS // 2

    g1 = norm1_g[0][None, :]
    g2 = norm2_g[0][None, :]
    w_in_bf = w_in[0].astype(BF16)
    w_out_bf = w_out[0].astype(BF16)
    qg2 = jnp.tile(q_norm_g[0], 2)[None, :]
    kg2 = jnp.tile(k_norm_g[0], 2)[None, :]
    rng = ret_norm_g[0].reshape(RET_HEADS, RET_DV)
    sink_tab = jnp.broadcast_to(jnp.repeat(attn_sinks[0], CHUNK).reshape(SWA_KV_HEADS, 4 * CHUNK, 1),
                                (SWA_KV_HEADS, 4 * CHUNK, LANES))
    w_router = jnp.zeros((D_MODEL, LANES), F32)
    w_router = w_router.at[:, 0:N_GROUPS].set(w_group[0]).at[:, EXPERT_LANE0:EXPERT_LANE0 + N_EXPERTS].set(w_expert[0])
    w_router_bf = w_router.astype(BF16)
    wgu_bf = jnp.concatenate([w_gate[0], w_up[0]], axis=-1).astype(BF16)
    wd_bf = w_down[0].astype(BF16)

    meta_rows = 2 * CHUNK
    m_pad = jnp.zeros((meta_rows, D_MODEL), F32).at[0:N_META].set(meta_tokens)
    m_rqk, m_rv, _, _, m_skv = _proj(m_pad, jnp.arange(meta_rows, dtype=jnp.int32), meta_rows, g1, w_in_bf, qg2, kg2)
    s_meta = _meta_state(m_rqk, m_rv)[None]
    meta_kv = m_skv[0:N_META]

    groups = [
        dict(x=x_prompt.reshape(bp * lp, D_MODEL), n=bp, seq=lp, pos0=N_META, has_hist=False, s0=s_meta,
             hist=jnp.zeros((1, WINDOW, 2 * SWA_KV), F32)),
        dict(x=x_sample.reshape(bs * ls, D_MODEL), n=bs, seq=ls, pos0=N_META + PAST_LEN, has_hist=True,
             s0=cache_ret_state[0].reshape(bs, n_pairs, 2 * RET_DK, RET_DV),
             hist=jnp.concatenate([cache_swa_k[0].reshape(bs, WINDOW, SWA_KV),
                                   cache_swa_v[0].reshape(bs, WINDOW, SWA_KV)], axis=-1)),
    ]

    base_cnt = jnp.zeros((1, LANES), F32)
    for g in groups:
        rows = g["n"] * g["seq"]
        tm = _tile_for(rows, TOKEN_TILE)
        pos = g["pos0"] + jnp.arange(g["seq"], dtype=jnp.int32)
        if g["seq"] < tm:
            assert tm % g["seq"] == 0
            pos = jnp.tile(pos, tm // g["seq"])
        else:
            assert g["seq"] % tm == 0
        rqk, rv, gate, sq, skv = _proj(g["x"], pos, tm, g1, w_in_bf, qg2, kg2)
        tl = min(ATTN_TILE, g["seq"])
        omix, s_out, kv_out = _attention(rqk, rv, gate, sq, skv, meta_kv, g["hist"], g["s0"], rng, sink_tab,
                                         n_streams=g["n"], seq=g["seq"], tl=tl, has_hist=g["has_hist"])
        xmid, hn, route, cnt = _post(omix, g["x"], w_out_bf, g2, w_router_bf, base_cnt, tm)
        base_cnt = cnt[0:1]
        g.update(xmid=xmid, hn=hn, route=route, s_out=s_out, kv_out=kv_out)

    te = EXPERT_TILE
    total_rows = sum(g["n"] * g["seq"] for g in groups)
    n_tiles = (2 * total_rows) // te + N_EXPERTS
    n_slots = n_tiles * te
    counts = base_cnt[0, EXPERT_LANE0:EXPERT_LANE0 + N_EXPERTS].astype(jnp.int32)
    tiles_e = (counts + te - 1) // te
    padded = tiles_e * te
    off = jnp.cumsum(padded) - padded
    tile_end = jnp.cumsum(tiles_e)
    n_valid = tile_end[-1:].astype(jnp.int32)
    tile_expert = _bucket(tile_end, jnp.arange(n_tiles, dtype=jnp.int32))
    pad_e = padded - counts
    pad_end = jnp.cumsum(pad_e)
    n_fill = n_slots - 2 * total_rows
    move_steps = sum((g["n"] * g["seq"]) // _tile_for(g["n"] * g["seq"], MOVE_TILE) for g in groups)
    fill_per_step = 8 * (-(-n_fill // (8 * move_steps)))
    j = jnp.arange(fill_per_step * move_steps, dtype=jnp.int32)
    e_of = _bucket(pad_end, j)
    in_expert = (off + counts)[e_of] + (j - (pad_end - pad_e)[e_of])
    in_tail = (off[-1] + padded[-1]) + (j - pad_end[-1])
    fill_slots = jnp.where(j < pad_end[-1], in_expert, jnp.where(j < n_fill, in_tail, n_slots + (j - n_fill)))
    fill_slots = fill_slots.astype(jnp.int32)
    n_rows_xs = n_slots + fill_per_step * move_steps - n_fill

    for g in groups:
        eid = g["route"][:, 0:2].astype(jnp.int32)
        g["pos"] = (off[eid] + g["route"][:, 4:6].astype(jnp.int32)).astype(jnp.int32)
    xs = _dispatch([g["pos"] for g in groups], fill_slots, [g["hn"] for g in groups], n_rows_xs, MOVE_TILE)

    ys = _experts(tile_expert, n_valid, xs, wgu_bf, wd_bf, n_tiles, te)

    outs = []
    for g in groups:
        rows = g["n"] * g["seq"]
        y = _combine(g["pos"], ys, g["xmid"], g["route"], _tile_for(rows, MOVE_TILE))
        outs.append(y.reshape(g["n"], g["seq"], D_MODEL))

    def caches(g):
        kv = g["kv_out"]
        k = kv[:, :, 0:SWA_KV].reshape(g["n"], WINDOW, SWA_KV_HEADS, SWA_HD)[None]
        v = kv[:, :, SWA_KV:2 * SWA_KV].reshape(g["n"], WINDOW, SWA_KV_HEADS, SWA_HD)[None]
        s = g["s_out"].reshape(g["n"], RET_HEADS, RET_DK, RET_DV)[None]
        return s, k, v

    sp, kp, vp = caches(groups[0])
    ss, ks, vs = caches(groups[1])
    return (outs[0], outs[1], sp, kp, vp, ss, ks, vs)
```

```python
import functools

import numpy as np
import jax
import jax.numpy as jnp
from jax import lax
from jax.experimental import pallas as pl
from jax.experimental.pallas import tpu as pltpu

F32 = jnp.float32
BF16 = jnp.bfloat16

D_MODEL = 1024
PAST_LEN = 4096
CHUNK = 64
N_META = 16
RET_HEADS = 4
RET_DK = 64
RET_DV = 128
SWA_HEADS = 8
SWA_KV_HEADS = 2
SWA_HD = 64
WINDOW = 128
ROPE_THETA = 10000.0
N_GROUPS = 4
EXPERTS_PER_GROUP = 8
N_EXPERTS = N_GROUPS * EXPERTS_PER_GROUP
EXPERT_FF = 256
EPS = 1e-6
NEG_INF = -1e30
LOG2E = float(np.log2(np.e))
RET_Q = RET_HEADS * RET_DK
RET_V = RET_HEADS * RET_DV
SWA_Q = SWA_HEADS * SWA_HD
SWA_KV = SWA_KV_HEADS * SWA_HD
MIX_WIDTH = RET_V + SWA_Q
IN_WIDTH = 2 * RET_Q + 2 * RET_V + SWA_Q + 2 * SWA_KV

LANES = 128
TOKEN_TILE = 512
ATTN_TILE = 256
EXPERT_TILE = 256
MOVE_TILE = 256
SUBLANES = 8
EXPERT_LANE0 = 32
META_ROWS = 64
VMEM_LIMIT = 56 * 1024 * 1024

_LOG_G = [float(np.log1p(-np.exp2(-5.0 - h))) for h in range(RET_HEADS)]


def _params(n_axes):
    return pltpu.CompilerParams(dimension_semantics=("arbitrary",) * n_axes, vmem_limit_bytes=VMEM_LIMIT)


def _rope(t, c, s1, s2):
    return t * c + pltpu.roll(t, 96, 1) * s1 + pltpu.roll(t, 32, 1) * s2


def _head_rms(t, g, lo_mask):
    ss = t * t
    lo = jnp.sum(jnp.where(lo_mask, ss, 0.0), axis=-1, keepdims=True)
    hi = jnp.sum(jnp.where(lo_mask, 0.0, ss), axis=-1, keepdims=True)
    ms = jnp.where(lo_mask, lo, hi) * (1.0 / SWA_HD)
    return t * lax.rsqrt(ms + EPS) * g


def _proj_kernel(x_ref, g1_ref, w_ref, qg_ref, kg_ref, cos_ref, s1_ref, s2_ref,
                 rqk_ref, rv_ref, gate_ref, sq_ref, skv_ref):
    x = x_ref[...]
    ms = jnp.mean(x * x, axis=-1, keepdims=True)
    xn = (x * lax.rsqrt(ms + EPS) * g1_ref[...]).astype(BF16)
    c, s1, s2 = cos_ref[...], s1_ref[...], s2_ref[...]
    lo_mask = lax.broadcasted_iota(jnp.int32, c.shape, 1) < SWA_HD

    def seg(a, b):
        return jnp.dot(xn, w_ref[:, a:b], preferred_element_type=F32)

    def tile(h, j):
        return h[:, j * LANES:(j + 1) * LANES]

    h = seg(0, 2 * RET_Q)
    for j in range(2):
        rqk_ref[:, j * LANES:(j + 1) * LANES] = _rope(tile(h, j), c, s1, s2).astype(BF16)
    for j in range(2, 4):
        rqk_ref[:, j * LANES:(j + 1) * LANES] = (_rope(tile(h, j), c, s1, s2) * (RET_DK ** -0.5)).astype(BF16)
    a = 2 * RET_Q
    rv_ref[...] = seg(a, a + RET_V).astype(BF16)
    a += RET_V
    g = seg(a, a + RET_V)
    gate_ref[...] = (g * jax.nn.sigmoid(g)).astype(BF16)
    a += RET_V
    h = seg(a, a + SWA_Q)
    qg = qg_ref[...]
    for j in range(SWA_Q // LANES):
        sq_ref[:, j * LANES:(j + 1) * LANES] = _rope(_head_rms(tile(h, j), qg, lo_mask), c, s1, s2).astype(BF16)
    a += SWA_Q
    h = seg(a, a + 2 * SWA_KV)
    skv_ref[:, 0:LANES] = _rope(_head_rms(tile(h, 0), kg_ref[...], lo_mask), c, s1, s2)
    skv_ref[:, LANES:2 * LANES] = tile(h, 1)


def _rope_tables(pos):
    half = SWA_HD // 2
    inv = ROPE_THETA ** (-jnp.arange(half, dtype=F32) / half)
    ang = pos.astype(F32)[:, None] * inv[None, :]
    cos, sin = jnp.cos(ang), jnp.sin(ang)
    z = jnp.zeros_like(sin)
    return (jnp.tile(cos, (1, 4)),
            jnp.tile(jnp.concatenate([-sin, z], axis=1), (1, 2)),
            jnp.tile(jnp.concatenate([z, sin], axis=1), (1, 2)))


def _proj(x2d, pos_rows, tm, g1, w_in_bf, qg2, kg2):
    t_rows = x2d.shape[0]
    n_tiles = t_rows // tm
    n_pos_tiles = pos_rows.shape[0] // tm
    cos, s1, s2 = _rope_tables(pos_rows)
    row = lambda i: (i, 0)
    const = lambda i: (0, 0)
    tab = lambda i: (i % n_pos_tiles, 0)
    return pl.pallas_call(
        _proj_kernel,
        grid=(n_tiles,),
        in_specs=[pl.BlockSpec((tm, D_MODEL), row),
                  pl.BlockSpec((1, D_MODEL), const),
                  pl.BlockSpec((D_MODEL, IN_WIDTH), const),
                  pl.BlockSpec((1, LANES), const),
                  pl.BlockSpec((1, LANES), const),
                  pl.BlockSpec((tm, LANES), tab),
                  pl.BlockSpec((tm, LANES), tab),
                  pl.BlockSpec((tm, LANES), tab)],
        out_specs=[pl.BlockSpec((tm, 2 * RET_Q), row),
                   pl.BlockSpec((tm, RET_V), row),
                   pl.BlockSpec((tm, RET_V), row),
                   pl.BlockSpec((tm, SWA_Q), row),
                   pl.BlockSpec((tm, 2 * SWA_KV), row)],
        out_shape=[jax.ShapeDtypeStruct((t_rows, 2 * RET_Q), BF16),
                   jax.ShapeDtypeStruct((t_rows, RET_V), BF16),
                   jax.ShapeDtypeStruct((t_rows, RET_V), BF16),
                   jax.ShapeDtypeStruct((t_rows, SWA_Q), BF16),
                   jax.ShapeDtypeStruct((t_rows, 2 * SWA_KV), F32)],
        compiler_params=_params(1),
        name="proj",
    )(x2d, g1, w_in_bf, qg2, kg2, cos, s1, s2)


def _pair_update(k_bf, v0_bf, v1_bf, wt):
    kw = (k_bf.astype(F32) * wt).astype(BF16)
    dn = (((0,), (0,)), ((), ()))
    a0 = lax.dot_general(kw, v0_bf, dn, preferred_element_type=F32)
    a1 = lax.dot_general(kw, v1_bf, dn, preferred_element_type=F32)
    top = lax.broadcasted_iota(jnp.int32, a0.shape, 0) < RET_DK
    return jnp.where(top, a0, a1)


def _decay_rows(n, pair, rows_back_from):
    i = lax.broadcasted_iota(jnp.int32, (n, LANES), 0).astype(F32)
    lane = lax.broadcasted_iota(jnp.int32, (n, LANES), 1)
    lg = jnp.where(lane < RET_DK, _LOG_G[2 * pair], _LOG_G[2 * pair + 1])
    return jnp.exp((rows_back_from - i) * lg)


def _meta_state_kernel(rqk_ref, rv_ref, s_ref, *, n_rows):
    for p in range(RET_HEADS // 2):
        k = rqk_ref[:, RET_Q + p * LANES:RET_Q + (p + 1) * LANES]
        wt = _decay_rows(n_rows, p, float(N_META - 1))
        s_ref[p] = _pair_update(k, rv_ref[:, (2 * p) * LANES:(2 * p + 1) * LANES],
                                rv_ref[:, (2 * p + 1) * LANES:(2 * p + 2) * LANES], wt)


def _meta_state(m_rqk, m_rv):
    n_rows = m_rqk.shape[0]
    return pl.pallas_call(
        functools.partial(_meta_state_kernel, n_rows=n_rows),
        out_shape=jax.ShapeDtypeStruct((RET_HEADS // 2, 2 * RET_DK, RET_DV), F32),
        name="meta_state",
    )(m_rqk, m_rv)


def _split_bf16(a):
    hi = a.astype(BF16)
    return hi, (a - hi.astype(F32)).astype(BF16)


def _lane_sum(a):
    hi, lo = _split_bf16(a)
    ones = jnp.ones((2 * LANES, LANES), BF16)
    return jnp.dot(jnp.concatenate([hi, lo], axis=1), ones, preferred_element_type=F32)


def _dup_halves(a, lo_mask):
    sw = pltpu.roll(a, SWA_HD, 1)
    return jnp.where(lo_mask, a, sw), jnp.where(lo_mask, sw, a)


def _attn_kernel(rqk_ref, rv_ref, gate_ref, sq_ref, skv_ref, meta_ref, hist_ref, s0_ref, rng_ref, sink_ref,
                 omix_ref, sout_ref, kvout_ref,
                 s_scr, kd_scr, vd_scr, mk_scr, mv_scr, dec_scr, wt_scr, cs_scr, gam_scr,
                 *, tl, has_hist):
    b = pl.program_id(0)
    t = pl.program_id(1)
    nt = pl.num_programs(1)
    n_chunks = tl // CHUNK
    n_pairs = RET_HEADS // 2
    lo_tl = lax.broadcasted_iota(jnp.int32, (tl, LANES), 1) < SWA_HD
    lo_c = lax.broadcasted_iota(jnp.int32, (CHUNK, LANES), 1) < SWA_HD

    @pl.when((b == 0) & (t == 0))
    def _tables():
        i = lax.broadcasted_iota(jnp.int32, (tl, tl), 0)
        j = lax.broadcasted_iota(jnp.int32, (tl, tl), 1)
        diff = (i - j).astype(F32)
        row = lax.broadcasted_iota(jnp.int32, (tl, LANES), 0).astype(F32)
        for h in range(RET_HEADS):
            dec_scr[h] = jnp.where(diff >= 0.0, jnp.exp(jnp.maximum(diff, 0.0) * _LOG_G[h]), 0.0)
            cs_scr[h] = jnp.exp((row + 1.0) * _LOG_G[h])
        top = lax.broadcasted_iota(jnp.int32, (2 * RET_DK, RET_DV), 0) < RET_DK
        for p in range(n_pairs):
            wt_scr[p] = _decay_rows(tl, p, float(tl - 1))
            gam_scr[p] = jnp.where(top, jnp.exp(jnp.float32(tl * _LOG_G[2 * p])), jnp.exp(jnp.float32(tl * _LOG_G[2 * p + 1])))
        lo_m = lax.broadcasted_iota(jnp.int32, (N_META, LANES), 1) < SWA_HD
        mk0, mk1 = _dup_halves(meta_ref[:, 0:LANES], lo_m)
        mv0, mv1 = _dup_halves(meta_ref[:, LANES:2 * LANES], lo_m)
        mk_scr[...] = jnp.zeros(mk_scr.shape, BF16)
        mv_scr[...] = jnp.zeros(mv_scr.shape, BF16)
        mk_scr[0, 0:N_META] = mk0.astype(BF16)
        mk_scr[1, 0:N_META] = mk1.astype(BF16)
        mv_scr[0, 0:N_META] = mv0.astype(BF16)
        mv_scr[1, 0:N_META] = mv1.astype(BF16)

    @pl.when(t == 0)
    def _stream_start():
        s_scr[...] = s0_ref[0]
        if has_hist:
            lo_w = lax.broadcasted_iota(jnp.int32, (WINDOW, LANES), 1) < SWA_HD
            k0, k1 = _dup_halves(hist_ref[0, :, 0:LANES], lo_w)
            v0, v1 = _dup_halves(hist_ref[0, :, LANES:2 * LANES], lo_w)
            kd_scr[0, 0:WINDOW] = k0.astype(BF16)
            kd_scr[1, 0:WINDOW] = k1.astype(BF16)
            vd_scr[0, 0:WINDOW] = v0.astype(BF16)
            vd_scr[1, 0:WINDOW] = v1.astype(BF16)
        else:
            z = jnp.zeros((WINDOW, LANES), BF16)
            for kv in range(SWA_KV_HEADS):
                kd_scr[kv, 0:WINDOW] = z
                vd_scr[kv, 0:WINDOW] = z

    k0, k1 = _dup_halves(skv_ref[:, 0:LANES], lo_tl)
    v0, v1 = _dup_halves(skv_ref[:, LANES:2 * LANES], lo_tl)
    kd_scr[0, WINDOW:WINDOW + tl] = k0.astype(BF16)
    kd_scr[1, WINDOW:WINDOW + tl] = k1.astype(BF16)
    vd_scr[0, WINDOW:WINDOW + tl] = v0.astype(BF16)
    vd_scr[1, WINDOW:WINDOW + tl] = v1.astype(BF16)

    band = WINDOW + CHUNK
    n_keys = META_ROWS + band
    n_q = 4 * CHUNK
    scale2 = (SWA_HD ** -0.5) * LOG2E
    krow = lax.broadcasted_iota(jnp.int32, (n_keys, n_q), 0)
    zero_c = jnp.zeros((CHUNK, LANES), BF16)
    ones_v = jnp.ones((n_keys, LANES), BF16)
    for c in range(n_chunks):
        if has_hist:
            first_valid = META_ROWS
        else:
            first_valid = jnp.where(t == 0, max(META_ROWS + WINDOW - c * CHUNK, META_ROWS), META_ROWS)
        valid_t = (krow < N_META) | (krow >= first_valid)
        r0 = c * CHUNK
        for kv in range(SWA_KV_HEADS):
            keys = jnp.concatenate([mk_scr[kv], kd_scr[kv, r0:r0 + band]], axis=0)
            vals = jnp.concatenate([mv_scr[kv], vd_scr[kv, r0:r0 + band]], axis=0)
            qa = sq_ref[r0:r0 + CHUNK, (2 * kv) * LANES:(2 * kv + 1) * LANES]
            qb = sq_ref[r0:r0 + CHUNK, (2 * kv + 1) * LANES:(2 * kv + 2) * LANES]
            lhs = jnp.concatenate([jnp.where(lo_c, qa, zero_c), jnp.where(lo_c, zero_c, qa),
                                   jnp.where(lo_c, qb, zero_c), jnp.where(lo_c, zero_c, qb)], axis=0)
            s_t = lax.dot_general(keys, lhs, (((1,), (1,)), ((), ())), preferred_element_type=F32) * scale2
            s_t = jnp.where(valid_t, s_t, NEG_INF)
            s_t = jnp.where(krow == N_META, sink_ref[kv, 0:1, :] * LOG2E, s_t)
            e_t = jnp.exp2(s_t - jnp.max(s_t, axis=0, keepdims=True)).astype(BF16)
            ov = lax.dot_general(e_t, jnp.concatenate([vals, ones_v], axis=1), (((0,), (0,)), ((), ())),
                                 preferred_element_type=F32)
            o = ov[:, 0:LANES] * (1.0 / ov[:, LANES:2 * LANES])
            oa = jnp.where(lo_c, o[0:CHUNK], o[CHUNK:2 * CHUNK])
            ob = jnp.where(lo_c, o[2 * CHUNK:3 * CHUNK], o[3 * CHUNK:4 * CHUNK])
            base = RET_V + (2 * kv) * LANES
            omix_ref[r0:r0 + CHUNK, base:base + LANES] = oa.astype(BF16)
            omix_ref[r0:r0 + CHUNK, base + LANES:base + 2 * LANES] = ob.astype(BF16)

    zero_t = jnp.zeros((tl, LANES), BF16)
    for p in range(n_pairs):
        q = rqk_ref[:, p * LANES:(p + 1) * LANES]
        k = rqk_ref[:, RET_Q + p * LANES:RET_Q + (p + 1) * LANES]
        lhs = jnp.concatenate([jnp.where(lo_tl, q, zero_t), jnp.where(lo_tl, zero_t, q)], axis=0)
        s = lax.dot_general(lhs, k, (((1,), (1,)), ((), ())), preferred_element_type=F32)
        cross = jnp.dot(lhs, s_scr[p].astype(BF16), preferred_element_type=F32)
        for i in range(2):
            h = 2 * p + i
            v = rv_ref[:, h * LANES:(h + 1) * LANES]
            a = (s[i * tl:(i + 1) * tl] * dec_scr[h]).astype(BF16)
            o = jnp.dot(a, v, preferred_element_type=F32) + cross[i * tl:(i + 1) * tl] * cs_scr[h]
            r = o * lax.rsqrt(_lane_sum(o * o) * (1.0 / RET_DV) + EPS) * rng_ref[h:h + 1, :]
            omix_ref[:, h * LANES:(h + 1) * LANES] = (r * gate_ref[:, h * LANES:(h + 1) * LANES].astype(F32)).astype(BF16)
        u = _pair_update(k, rv_ref[:, (2 * p) * LANES:(2 * p + 1) * LANES],
                         rv_ref[:, (2 * p + 1) * LANES:(2 * p + 2) * LANES], wt_scr[p])
        s_scr[p] = gam_scr[p] * s_scr[p] + u

    if tl >= WINDOW:
        @pl.when(t + 1 < nt)
        def _carry_window():
            for kv in range(SWA_KV_HEADS):
                kd_scr[kv, 0:WINDOW] = kd_scr[kv, tl:tl + WINDOW]
                vd_scr[kv, 0:WINDOW] = vd_scr[kv, tl:tl + WINDOW]

    @pl.when(t + 1 == nt)
    def _stream_end():
        sout_ref[0] = s_scr[...]
        if tl >= WINDOW:
            kvout_ref[0] = skv_ref[tl - WINDOW:tl, :]
        else:
            kvout_ref[0, 0:WINDOW - tl] = hist_ref[0, tl:WINDOW, :]
            kvout_ref[0, WINDOW - tl:WINDOW] = skv_ref[...]


def _attention(rqk, rv, gate, sq, skv, meta_kv, hist_kv, s0, rng, sink_tab, *, n_streams, seq, tl, has_hist):
    nt = seq // tl
    assert tl % CHUNK == 0 and seq % tl == 0
    assert tl >= WINDOW or (nt == 1 and has_hist)
    n_pairs = RET_HEADS // 2
    s0_shared = s0.shape[0] == 1
    row = lambda b, t: (b * nt + t, 0)
    const2 = lambda b, t: (0, 0)
    const3 = lambda b, t: (0, 0, 0)
    per_b3 = lambda b, t: (b, 0, 0)
    s0_map = (lambda b, t: (0, 0, 0, 0)) if s0_shared else (lambda b, t: (b, 0, 0, 0))
    hist_map = per_b3 if has_hist else const3
    rows = n_streams * seq
    return pl.pallas_call(
        functools.partial(_attn_kernel, tl=tl, has_hist=has_hist),
        grid=(n_streams, nt),
        in_specs=[pl.BlockSpec((tl, 2 * RET_Q), row),
                  pl.BlockSpec((tl, RET_V), row),
                  pl.BlockSpec((tl, RET_V), row),
                  pl.BlockSpec((tl, SWA_Q), row),
                  pl.BlockSpec((tl, 2 * SWA_KV), row),
                  pl.BlockSpec((N_META, 2 * SWA_KV), const2),
                  pl.BlockSpec((1, WINDOW, 2 * SWA_KV), hist_map),
                  pl.BlockSpec((1, n_pairs, 2 * RET_DK, RET_DV), s0_map),
                  pl.BlockSpec((RET_HEADS, RET_DV), const2),
                  pl.BlockSpec((SWA_KV_HEADS, SUBLANES, 4 * CHUNK), const3)],
        out_specs=[pl.BlockSpec((tl, MIX_WIDTH), row),
                   pl.BlockSpec((1, n_pairs, 2 * RET_DK, RET_DV), lambda b, t: (b, 0, 0, 0)),
                   pl.BlockSpec((1, WINDOW, 2 * SWA_KV), per_b3)],
        out_shape=[jax.ShapeDtypeStruct((rows, MIX_WIDTH), BF16),
                   jax.ShapeDtypeStruct((n_streams, n_pairs, 2 * RET_DK, RET_DV), F32),
                   jax.ShapeDtypeStruct((n_streams, WINDOW, 2 * SWA_KV), F32)],
        scratch_shapes=[pltpu.VMEM((n_pairs, 2 * RET_DK, RET_DV), F32),
                        pltpu.VMEM((SWA_KV_HEADS, WINDOW + tl, LANES), BF16),
                        pltpu.VMEM((SWA_KV_HEADS, WINDOW + tl, LANES), BF16),
                        pltpu.VMEM((SWA_KV_HEADS, META_ROWS, LANES), BF16),
                        pltpu.VMEM((SWA_KV_HEADS, META_ROWS, LANES), BF16),
                        pltpu.VMEM((RET_HEADS, tl, tl), F32),
                        pltpu.VMEM((n_pairs, tl, LANES), F32),
                        pltpu.VMEM((RET_HEADS, tl, RET_DV), F32),
                        pltpu.VMEM((n_pairs, 2 * RET_DK, RET_DV), F32)],
        compiler_params=_params(2),
        name="attention",
    )(rqk, rv, gate, sq, skv, meta_kv, hist_kv, s0, rng, sink_tab)


def _lane_pick(lane, idx, val):
    return jnp.sum(jnp.where(lane == idx, val, 0.0), axis=-1, keepdims=True)


def _post_kernel(omix_ref, x_ref, wout_ref, g2_ref, wr_ref, base_ref,
                 xmid_ref, hn_ref, route_ref, cnt_ref, tri_scr, run_scr):
    i = pl.program_id(0)
    tm = x_ref.shape[0]

    @pl.when(i == 0)
    def _init():
        r = lax.broadcasted_iota(jnp.int32, (tm, tm), 0)
        c = lax.broadcasted_iota(jnp.int32, (tm, tm), 1)
        tri_scr[...] = jnp.where(c < r, 1.0, 0.0).astype(BF16)
        run_scr[...] = base_ref[...]

    xm = x_ref[...] + jnp.dot(omix_ref[...], wout_ref[...], preferred_element_type=F32)
    xmid_ref[...] = xm
    hn = xm * lax.rsqrt(jnp.mean(xm * xm, axis=-1, keepdims=True) + EPS) * g2_ref[...]
    hn_ref[...] = hn
    logits = jnp.dot(hn.astype(BF16), wr_ref[...], preferred_element_type=F32)

    lane = lax.broadcasted_iota(jnp.int32, logits.shape, 1)
    big = jnp.int32(LANES)
    gl = jnp.where(lane < N_GROUPS, logits, NEG_INF)
    gmax = jnp.max(gl, axis=-1, keepdims=True)
    gsum = jnp.sum(jnp.exp(gl - gmax), axis=-1, keepdims=True)
    g_sel = jnp.min(jnp.where(gl == gmax, lane, big), axis=-1, keepdims=True)
    p_sel = 1.0 / gsum
    e_lo = EXPERT_LANE0 + g_sel * EXPERTS_PER_GROUP
    el = jnp.where((lane >= e_lo) & (lane < e_lo + EXPERTS_PER_GROUP), logits, NEG_INF)
    m1 = jnp.max(el, axis=-1, keepdims=True)
    i1 = jnp.min(jnp.where(el == m1, lane, big), axis=-1, keepdims=True)
    el2 = jnp.where(lane == i1, NEG_INF, el)
    m2 = jnp.max(el2, axis=-1, keepdims=True)
    i2 = jnp.min(jnp.where(el2 == m2, lane, big), axis=-1, keepdims=True)
    e2 = jnp.exp(m2 - m1)
    inv = 1.0 / (1.0 + e2)
    w1 = p_sel * inv
    w2 = p_sel * (e2 * inv)

    hit = (lane == i1) | (lane == i2)
    pref = jnp.dot(tri_scr[...], jnp.where(hit, 1.0, 0.0).astype(BF16), preferred_element_type=F32) + run_scr[...]
    r1 = _lane_pick(lane, i1, pref)
    r2 = _lane_pick(lane, i2, pref)
    run_scr[...] = run_scr[...] + jnp.sum(jnp.where(hit, 1.0, 0.0), axis=0, keepdims=True)
    cnt_ref[...] = jnp.broadcast_to(run_scr[...], cnt_ref.shape)

    vals = [(i1 - EXPERT_LANE0).astype(F32), (i2 - EXPERT_LANE0).astype(F32), w1, w2, r1, r2]
    out = jnp.zeros(logits.shape, F32)
    for k, v in enumerate(vals):
        out = jnp.where(lane == k, v, out)
    route_ref[...] = out


def _post(omix, x2d, w_out_bf, g2, w_router_bf, base_cnt, tm):
    t_rows = x2d.shape[0]
    row = lambda i: (i, 0)
    const = lambda i: (0, 0)
    return pl.pallas_call(
        _post_kernel,
        grid=(t_rows // tm,),
        in_specs=[pl.BlockSpec((tm, MIX_WIDTH), row),
                  pl.BlockSpec((tm, D_MODEL), row),
                  pl.BlockSpec((MIX_WIDTH, D_MODEL), const),
                  pl.BlockSpec((1, D_MODEL), const),
                  pl.BlockSpec((D_MODEL, LANES), const),
                  pl.BlockSpec((1, LANES), const)],
        out_specs=[pl.BlockSpec((tm, D_MODEL), row),
                   pl.BlockSpec((tm, D_MODEL), row),
                   pl.BlockSpec((tm, LANES), row),
                   pl.BlockSpec((8, LANES), const)],
        out_shape=[jax.ShapeDtypeStruct((t_rows, D_MODEL), F32),
                   jax.ShapeDtypeStruct((t_rows, D_MODEL), F32),
                   jax.ShapeDtypeStruct((t_rows, LANES), F32),
                   jax.ShapeDtypeStruct((8, LANES), F32)],
        scratch_shapes=[pltpu.VMEM((tm, tm), BF16), pltpu.VMEM((1, LANES), F32)],
        compiler_params=_params(1),
        name="post",
    )(omix, x2d, w_out_bf, g2, w_router_bf, base_cnt)


def _row_copy(src_ref, src_row, dst_ref, dst_row, sem):
    return pltpu.make_async_copy(src_ref.at[pl.ds(src_row, 1)], dst_ref.at[pl.ds(dst_row, 1)], sem)


def _dispatch_kernel(pos_ref, pad_ref, *refs, tm, n_pad, group_steps):
    hn_refs = refs[:len(group_steps)]
    xs_ref, zero_scr, sem = refs[len(group_steps):]
    i = pl.program_id(0)

    @pl.when(i == 0)
    def _init():
        zero_scr[...] = jnp.zeros(zero_scr.shape, F32)

    first = 0
    for hn_ref, steps in zip(hn_refs, group_steps):
        @pl.when((i >= first) & (i < first + steps))
        def _rows(hn_ref=hn_ref):
            def body(blk, carry):
                for k in range(SUBLANES):
                    r = blk * SUBLANES + k
                    for e in range(2):
                        pltpu.make_async_copy(hn_ref.at[blk, pl.ds(k, 1)], xs_ref.at[pl.ds(pos_ref[0, 0, 2 * r + e], 1)],
                                              sem).start(priority=e)
                return carry

            lax.fori_loop(0, tm // SUBLANES, body, 0)
        first += steps

    for j in range(n_pad):
        _row_copy(zero_scr, 0, xs_ref, pad_ref[0, 0, j], sem).start(priority=j % 2)
    n_rows = 2 * tm + n_pad
    pltpu.make_async_copy(xs_ref.at[pl.ds(0, n_rows)], xs_ref.at[pl.ds(0, n_rows)], sem).wait()


def _dispatch(pos_list, pad, hn_list, n_rows_out, tm):
    group_steps = tuple(hn.shape[0] // tm for hn in hn_list)
    n_steps = sum(group_steps)
    n_pad = pad.shape[0] // n_steps
    assert pad.shape[0] == n_steps * n_pad and tm % SUBLANES == 0
    pos3 = jnp.concatenate([p.reshape(-1, 1, 2 * tm) for p in pos_list], axis=0)
    pad3 = pad.reshape(n_steps, 1, n_pad)
    smem = lambda n: pl.BlockSpec((1, 1, n), lambda i: (i, 0, 0), memory_space=pltpu.SMEM)
    hn_specs = []
    first = 0
    for steps in group_steps:
        hn_specs.append(pl.BlockSpec((tm // SUBLANES, SUBLANES, D_MODEL),
                                     lambda i, first=first, steps=steps: (jnp.clip(i - first, 0, steps - 1), 0, 0)))
        first += steps
    hn_list = [hn.reshape(hn.shape[0] // SUBLANES, SUBLANES, D_MODEL) for hn in hn_list]
    return pl.pallas_call(
        functools.partial(_dispatch_kernel, tm=tm, n_pad=n_pad, group_steps=group_steps),
        grid=(n_steps,),
        in_specs=[smem(2 * tm), smem(n_pad)] + hn_specs,
        out_specs=pl.BlockSpec(memory_space=pl.ANY),
        out_shape=jax.ShapeDtypeStruct((n_rows_out, D_MODEL), F32),
        scratch_shapes=[pltpu.VMEM((8, D_MODEL), F32), pltpu.SemaphoreType.DMA(())],
        compiler_params=_params(1),
        name="dispatch",
    )(pos3, pad3, *hn_list)


def _expert_kernel(te_ref, nv_ref, x_ref, wgu_ref, wd_ref, y_ref):
    i = pl.program_id(0)

    @pl.when(i < nv_ref[0])
    def _compute():
        x = x_ref[...].astype(BF16)
        gu = jnp.dot(x, wgu_ref[0], preferred_element_type=F32)
        g = gu[:, 0:EXPERT_FF]
        a = (g * jax.nn.sigmoid(g) * gu[:, EXPERT_FF:2 * EXPERT_FF]).astype(BF16)
        y_ref[...] = jnp.dot(a, wd_ref[0], preferred_element_type=F32)

    @pl.when(i >= nv_ref[0])
    def _skip():
        y_ref[...] = jnp.zeros(y_ref.shape, F32)


def _experts(tile_expert, n_valid, xs, wgu_bf, wd_bf, n_tiles, tm):
    last = lambda i, nv: jnp.minimum(i, nv[0] - 1)
    return pl.pallas_call(
        _expert_kernel,
        grid_spec=pltpu.PrefetchScalarGridSpec(
            num_scalar_prefetch=2,
            grid=(n_tiles,),
            in_specs=[pl.BlockSpec((tm, D_MODEL), lambda i, te, nv: (last(i, nv), 0)),
                      pl.BlockSpec((1, D_MODEL, 2 * EXPERT_FF), lambda i, te, nv: (te[last(i, nv)], 0, 0)),
                      pl.BlockSpec((1, EXPERT_FF, D_MODEL), lambda i, te, nv: (te[last(i, nv)], 0, 0))],
            out_specs=pl.BlockSpec((tm, D_MODEL), lambda i, te, nv: (i, 0))),
        out_shape=jax.ShapeDtypeStruct((n_tiles * tm, D_MODEL), F32),
        compiler_params=_params(1),
        name="experts",
    )(tile_expert, n_valid, xs, wgu_bf, wd_bf)


def _combine_kernel(pos_ref, nxt_ref, ys_ref, xmid_ref, route_ref, out_ref, ybuf, sems, *, tm, n_steps):
    i = pl.program_id(0)
    slot = i % 2

    def issue(p_ref, to_slot):
        def body(blk, carry):
            for k in range(SUBLANES):
                r = blk * SUBLANES + k
                for e in range(2):
                    pltpu.make_async_copy(ys_ref.at[pl.ds(p_ref[0, 0, 2 * r + e], 1)],
                                          ybuf.at[to_slot, e, blk, pl.ds(k, 1)], sems.at[to_slot]).start(priority=e)
            return carry

        lax.fori_loop(0, tm // SUBLANES, body, 0)

    @pl.when(i == 0)
    def _first():
        issue(pos_ref, 0)

    @pl.when(i + 1 < n_steps)
    def _ahead():
        issue(nxt_ref, 1 - slot)

    for e in range(2):
        pltpu.make_async_copy(ybuf.at[slot, e], ybuf.at[slot, e], sems.at[slot]).wait()
    w = route_ref[...]
    y0 = ybuf[slot, 0].reshape(tm, D_MODEL)
    y1 = ybuf[slot, 1].reshape(tm, D_MODEL)
    out_ref[...] = xmid_ref[...] + w[:, 2:3] * y0 + w[:, 3:4] * y1


def _combine(pos, ys, xmid, route, tm):
    t_rows = xmid.shape[0]
    n_steps = t_rows // tm
    assert tm % SUBLANES == 0
    pos3 = pos.reshape(n_steps, 1, 2 * tm)
    row = lambda i: (i, 0)
    return pl.pallas_call(
        functools.partial(_combine_kernel, tm=tm, n_steps=n_steps),
        grid=(n_steps,),
        in_specs=[pl.BlockSpec((1, 1, 2 * tm), lambda i: (i, 0, 0), memory_space=pltpu.SMEM),
                  pl.BlockSpec((1, 1, 2 * tm), lambda i: (jnp.minimum(i + 1, n_steps - 1), 0, 0),
                               memory_space=pltpu.SMEM),
                  pl.BlockSpec(memory_space=pl.ANY),
                  pl.BlockSpec((tm, D_MODEL), row),
                  pl.BlockSpec((tm, LANES), row)],
        out_specs=pl.BlockSpec((tm, D_MODEL), row),
        out_shape=jax.ShapeDtypeStruct((t_rows, D_MODEL), F32),
        scratch_shapes=[pltpu.VMEM((2, 2, tm // SUBLANES, SUBLANES, D_MODEL), F32), pltpu.SemaphoreType.DMA((2,))],
        compiler_params=_params(1),
        name="combine",
    )(pos3, pos3, ys, xmid, route)


def _bucket(ends, idx):
    n = jnp.sum((ends[None, :] <= idx[:, None]).astype(jnp.int32), axis=1)
    return jnp.minimum(n, ends.shape[0] - 1)


def _tile_for(rows, pref):
    tm = min(pref, rows)
    assert rows % tm == 0
    return tm


def kernel(x_prompt, x_sample, cache_ret_state, cache_swa_k, cache_swa_v, meta_tokens, norm1_g, w_in, q_norm_g,
           k_norm_g, ret_norm_g, attn_sinks, w_out, norm2_g, w_group, w_expert, w_gate, w_up, w_down):
    assert norm1_g.shape[0] == 1, "single-layer trunk"
    bp, lp, _ = x_prompt.shape
    bs, ls, _ = x_sample.shape
    n_pairs = RET_HEADS // 2

    g1 = norm1_g[0][None, :]
    g2 = norm2_g[0][None, :]
    w_in_bf = w_in[0].astype(BF16)
    w_out_bf = w_out[0].astype(BF16)
    qg2 = jnp.tile(q_norm_g[0], 2)[None, :]
    kg2 = jnp.tile(k_norm_g[0], 2)[None, :]
    rng = ret_norm_g[0].reshape(RET_HEADS, RET_DV)
    sink_tab = jnp.broadcast_to(jnp.repeat(attn_sinks[0], CHUNK).reshape(SWA_KV_HEADS, 1, 4 * CHUNK),
                                (SWA_KV_HEADS, SUBLANES, 4 * CHUNK))
    w_router = jnp.zeros((D_MODEL, LANES), F32)
    w_router = w_router.at[:, 0:N_GROUPS].set(w_group[0]).at[:, EXPERT_LANE0:EXPERT_LANE0 + N_EXPERTS].set(w_expert[0])
    w_router_bf = w_router.astype(BF16)
    wgu_bf = jnp.concatenate([w_gate[0], w_up[0]], axis=-1).astype(BF16)
    wd_bf = w_down[0].astype(BF16)

    meta_rows = 2 * CHUNK
    m_pad = jnp.zeros((meta_rows, D_MODEL), F32).at[0:N_META].set(meta_tokens)
    m_rqk, m_rv, _, _, m_skv = _proj(m_pad, jnp.arange(meta_rows, dtype=jnp.int32), meta_rows, g1, w_in_bf, qg2, kg2)
    s_meta = _meta_state(m_rqk, m_rv)[None]
    meta_kv = m_skv[0:N_META]

    groups = [
        dict(x=x_prompt.reshape(bp * lp, D_MODEL), n=bp, seq=lp, pos0=N_META, has_hist=False, s0=s_meta,
             hist=jnp.zeros((1, WINDOW, 2 * SWA_KV), F32)),
        dict(x=x_sample.reshape(bs * ls, D_MODEL), n=bs, seq=ls, pos0=N_META + PAST_LEN, has_hist=True,
             s0=cache_ret_state[0].reshape(bs, n_pairs, 2 * RET_DK, RET_DV),
             hist=jnp.concatenate([cache_swa_k[0].reshape(bs, WINDOW, SWA_KV),
                                   cache_swa_v[0].reshape(bs, WINDOW, SWA_KV)], axis=-1)),
    ]

    base_cnt = jnp.zeros((1, LANES), F32)
    for g in groups:
        rows = g["n"] * g["seq"]
        tm = _tile_for(rows, TOKEN_TILE)
        pos = g["pos0"] + jnp.arange(g["seq"], dtype=jnp.int32)
        if g["seq"] < tm:
            assert tm % g["seq"] == 0
            pos = jnp.tile(pos, tm // g["seq"])
        else:
            assert g["seq"] % tm == 0
        rqk, rv, gate, sq, skv = _proj(g["x"], pos, tm, g1, w_in_bf, qg2, kg2)
        tl = min(ATTN_TILE, g["seq"])
        omix, s_out, kv_out = _attention(rqk, rv, gate, sq, skv, meta_kv, g["hist"], g["s0"], rng, sink_tab,
                                         n_streams=g["n"], seq=g["seq"], tl=tl, has_hist=g["has_hist"])
        xmid, hn, route, cnt = _post(omix, g["x"], w_out_bf, g2, w_router_bf, base_cnt, tm)
        base_cnt = cnt[0:1]
        g.update(xmid=xmid, hn=hn, route=route, s_out=s_out, kv_out=kv_out)

    te = EXPERT_TILE
    total_rows = sum(g["n"] * g["seq"] for g in groups)
    n_tiles = (2 * total_rows) // te + N_EXPERTS
    n_slots = n_tiles * te
    counts = base_cnt[0, EXPERT_LANE0:EXPERT_LANE0 + N_EXPERTS].astype(jnp.int32)
    tiles_e = (counts + te - 1) // te
    padded = tiles_e * te
    off = jnp.cumsum(padded) - padded
    tile_end = jnp.cumsum(tiles_e)
    n_valid = tile_end[-1:].astype(jnp.int32)
    tile_expert = _bucket(tile_end, jnp.arange(n_tiles, dtype=jnp.int32))
    pad_e = padded - counts
    pad_end = jnp.cumsum(pad_e)
    n_fill = n_slots - 2 * total_rows
    move_steps = sum((g["n"] * g["seq"]) // _tile_for(g["n"] * g["seq"], MOVE_TILE) for g in groups)
    fill_per_step = 8 * (-(-n_fill // (8 * move_steps)))
    j = jnp.arange(fill_per_step * move_steps, dtype=jnp.int32)
    e_of = _bucket(pad_end, j)
    in_expert = (off + counts)[e_of] + (j - (pad_end - pad_e)[e_of])
    in_tail = (off[-1] + padded[-1]) + (j - pad_end[-1])
    fill_slots = jnp.where(j < pad_end[-1], in_expert, jnp.where(j < n_fill, in_tail, n_slots + (j - n_fill)))
    fill_slots = fill_slots.astype(jnp.int32)
    n_rows_xs = n_slots + fill_per_step * move_steps - n_fill

    for g in groups:
        eid = g["route"][:, 0:2].astype(jnp.int32)
        g["pos"] = (off[eid] + g["route"][:, 4:6].astype(jnp.int32)).astype(jnp.int32)
    xs = _dispatch([g["pos"] for g in groups], fill_slots, [g["hn"] for g in groups], n_rows_xs, MOVE_TILE)

    ys = _experts(tile_expert, n_valid, xs, wgu_bf, wd_bf, n_tiles, te)

    outs = []
    for g in groups:
        rows = g["n"] * g["seq"]
        y = _combine(g["pos"], ys, g["xmid"], g["route"], _tile_for(rows, MOVE_TILE))
        outs.append(y.reshape(g["n"], g["seq"], D_MODEL))

    def caches(g):
        kv = g["kv_out"]
        k = kv[:, :, 0:SWA_KV].reshape(g["n"], WINDOW, SWA_KV_HEADS, SWA_HD)[None]
        v = kv[:, :, SWA_KV:2 * SWA_KV].reshape(g["n"], WINDOW, SWA_KV_HEADS, SWA_HD)[None]
        s = g["s_out"].reshape(g["n"], RET_HEADS, RET_DK, RET_DV)[None]
        return s, k, v

    sp, kp, vp = caches(groups[0])
    ss, ks, vs = caches(groups[1])
    return (outs[0], outs[1], sp, kp, vp, ss, ks, vs)
```

```python
import functools

import numpy as np
import jax
import jax.numpy as jnp
from jax import lax
from jax.experimental import pallas as pl
from jax.experimental.pallas import tpu as pltpu

F32 = jnp.float32
BF16 = jnp.bfloat16

D_MODEL = 1024
PAST_LEN = 4096
CHUNK = 64
N_META = 16
RET_HEADS = 4
RET_DK = 64
RET_DV = 128
SWA_HEADS = 8
SWA_KV_HEADS = 2
SWA_HD = 64
WINDOW = 128
ROPE_THETA = 10000.0
N_GROUPS = 4
EXPERTS_PER_GROUP = 8
N_EXPERTS = N_GROUPS * EXPERTS_PER_GROUP
EXPERT_FF = 256
EPS = 1e-6
NEG_INF = -1e30
LOG2E = float(np.log2(np.e))
RET_Q = RET_HEADS * RET_DK
RET_V = RET_HEADS * RET_DV
SWA_Q = SWA_HEADS * SWA_HD
SWA_KV = SWA_KV_HEADS * SWA_HD
MIX_WIDTH = RET_V + SWA_Q
IN_WIDTH = 2 * RET_Q + 2 * RET_V + SWA_Q + 2 * SWA_KV

LANES = 128
TOKEN_TILE = 512
ATTN_TILE = 256
EXPERT_TILE = 256
MOVE_TILE = 256
SUBLANES = 8
ROUTER_EXPERT_ROW0 = 8
ROUTER_ROWS = 64
META_ROWS = 64
VMEM_LIMIT = 56 * 1024 * 1024

_LOG_G = [float(np.log1p(-np.exp2(-5.0 - h))) for h in range(RET_HEADS)]


def _params(n_axes):
    return pltpu.CompilerParams(dimension_semantics=("arbitrary",) * n_axes, vmem_limit_bytes=VMEM_LIMIT)


def _split_bf16(a):
    hi = a.astype(BF16)
    return hi, (a - hi.astype(F32)).astype(BF16)


def _split_dot(a, w2):
    hi, lo = _split_bf16(a)
    return jnp.dot(jnp.concatenate([hi, lo], axis=1), w2, preferred_element_type=F32)


def _lane_sum(a):
    return _split_dot(a, jnp.ones((2 * LANES, LANES), BF16))


def _head_sum_matrix():
    i = lax.broadcasted_iota(jnp.int32, (2 * LANES, LANES), 0) % LANES
    j = lax.broadcasted_iota(jnp.int32, (2 * LANES, LANES), 1)
    return jnp.where((i < SWA_HD) == (j < SWA_HD), 1.0, 0.0).astype(BF16)


def _rotate_half_matrix():
    half = SWA_HD // 2
    i = lax.broadcasted_iota(jnp.int32, (2 * LANES, LANES), 0) % LANES
    j = lax.broadcasted_iota(jnp.int32, (2 * LANES, LANES), 1)
    first = (j % SWA_HD) < half
    return jnp.where(first & (i == j + half), -1.0, jnp.where((~first) & (i == j - half), 1.0, 0.0)).astype(BF16)


def _rope(t, c, s, rot_w):
    return t * c + _split_dot(t, rot_w) * s


def _head_rms(t, g, head_w):
    ms = _split_dot(t * t, head_w) * (1.0 / SWA_HD)
    return t * lax.rsqrt(ms + EPS) * g


def _row_rms(x):
    n_tiles = x.shape[1] // LANES
    ss = x[:, 0:LANES] * x[:, 0:LANES]
    for j in range(1, n_tiles):
        ss = ss + x[:, j * LANES:(j + 1) * LANES] * x[:, j * LANES:(j + 1) * LANES]
    r = lax.rsqrt(_lane_sum(ss) * (1.0 / x.shape[1]) + EPS)
    return x * jnp.concatenate([r] * n_tiles, axis=1)


def _proj_kernel(x_ref, g1_ref, w_ref, qg_ref, kg_ref, cos_ref, sin_ref,
                 rqk_ref, rv_ref, gate_ref, sq_ref, skv_ref):
    xn = (_row_rms(x_ref[...]) * g1_ref[...]).astype(BF16)
    c, s = cos_ref[...], sin_ref[...]
    rot_w = _rotate_half_matrix()
    head_w = _head_sum_matrix()

    def seg(a, b):
        return jnp.dot(xn, w_ref[:, a:b], preferred_element_type=F32)

    def tile(h, j):
        return h[:, j * LANES:(j + 1) * LANES]

    h = seg(0, 2 * RET_Q)
    for j in range(2):
        rqk_ref[:, j * LANES:(j + 1) * LANES] = _rope(tile(h, j), c, s, rot_w).astype(BF16)
    for j in range(2, 4):
        rqk_ref[:, j * LANES:(j + 1) * LANES] = (_rope(tile(h, j), c, s, rot_w) * (RET_DK ** -0.5)).astype(BF16)
    a = 2 * RET_Q
    rv_ref[...] = seg(a, a + RET_V).astype(BF16)
    a += RET_V
    g = seg(a, a + RET_V)
    gate_ref[...] = (g * jax.nn.sigmoid(g)).astype(BF16)
    a += RET_V
    h = seg(a, a + SWA_Q)
    qg = qg_ref[...]
    for j in range(SWA_Q // LANES):
        sq_ref[:, j * LANES:(j + 1) * LANES] = _rope(_head_rms(tile(h, j), qg, head_w), c, s, rot_w).astype(BF16)
    a += SWA_Q
    h = seg(a, a + 2 * SWA_KV)
    skv_ref[:, 0:LANES] = _rope(_head_rms(tile(h, 0), kg_ref[...], head_w), c, s, rot_w)
    skv_ref[:, LANES:2 * LANES] = tile(h, 1)


def _rope_tables(pos):
    half = SWA_HD // 2
    inv = ROPE_THETA ** (-jnp.arange(half, dtype=F32) / half)
    ang = pos.astype(F32)[:, None] * inv[None, :]
    return jnp.tile(jnp.cos(ang), (1, 4)), jnp.tile(jnp.sin(ang), (1, 4))


def _proj(x2d, pos_rows, tm, g1, w_in_bf, qg2, kg2):
    t_rows = x2d.shape[0]
    n_tiles = t_rows // tm
    n_pos_tiles = pos_rows.shape[0] // tm
    cos, sin = _rope_tables(pos_rows)
    row = lambda i: (i, 0)
    const = lambda i: (0, 0)
    tab = lambda i: (i % n_pos_tiles, 0)
    return pl.pallas_call(
        _proj_kernel,
        grid=(n_tiles,),
        in_specs=[pl.BlockSpec((tm, D_MODEL), row),
                  pl.BlockSpec((1, D_MODEL), const),
                  pl.BlockSpec((D_MODEL, IN_WIDTH), const),
                  pl.BlockSpec((1, LANES), const),
                  pl.BlockSpec((1, LANES), const),
                  pl.BlockSpec((tm, LANES), tab),
                  pl.BlockSpec((tm, LANES), tab)],
        out_specs=[pl.BlockSpec((tm, 2 * RET_Q), row),
                   pl.BlockSpec((tm, RET_V), row),
                   pl.BlockSpec((tm, RET_V), row),
                   pl.BlockSpec((tm, SWA_Q), row),
                   pl.BlockSpec((tm, 2 * SWA_KV), row)],
        out_shape=[jax.ShapeDtypeStruct((t_rows, 2 * RET_Q), BF16),
                   jax.ShapeDtypeStruct((t_rows, RET_V), BF16),
                   jax.ShapeDtypeStruct((t_rows, RET_V), BF16),
                   jax.ShapeDtypeStruct((t_rows, SWA_Q), BF16),
                   jax.ShapeDtypeStruct((t_rows, 2 * SWA_KV), F32)],
        compiler_params=_params(1),
        name="proj",
    )(x2d, g1, w_in_bf, qg2, kg2, cos, sin)


def _pair_update(k_bf, v0_bf, v1_bf, wt):
    kw = (k_bf.astype(F32) * wt).astype(BF16)
    dn = (((0,), (0,)), ((), ()))
    a0 = lax.dot_general(kw, v0_bf, dn, preferred_element_type=F32)
    a1 = lax.dot_general(kw, v1_bf, dn, preferred_element_type=F32)
    top = lax.broadcasted_iota(jnp.int32, a0.shape, 0) < RET_DK
    return jnp.where(top, a0, a1)


def _decay_rows(n, pair, rows_back_from):
    i = lax.broadcasted_iota(jnp.int32, (n, LANES), 0).astype(F32)
    lane = lax.broadcasted_iota(jnp.int32, (n, LANES), 1)
    lg = jnp.where(lane < RET_DK, _LOG_G[2 * pair], _LOG_G[2 * pair + 1])
    return jnp.exp((rows_back_from - i) * lg)


def _meta_state_kernel(rqk_ref, rv_ref, s_ref, *, n_rows):
    for p in range(RET_HEADS // 2):
        k = rqk_ref[:, RET_Q + p * LANES:RET_Q + (p + 1) * LANES]
        wt = _decay_rows(n_rows, p, float(N_META - 1))
        s_ref[p] = _pair_update(k, rv_ref[:, (2 * p) * LANES:(2 * p + 1) * LANES],
                                rv_ref[:, (2 * p + 1) * LANES:(2 * p + 2) * LANES], wt)


def _meta_state(m_rqk, m_rv):
    n_rows = m_rqk.shape[0]
    return pl.pallas_call(
        functools.partial(_meta_state_kernel, n_rows=n_rows),
        out_shape=jax.ShapeDtypeStruct((RET_HEADS // 2, 2 * RET_DK, RET_DV), F32),
        name="meta_state",
    )(m_rqk, m_rv)


def _dup_halves(a, lo_mask):
    sw = pltpu.roll(a, SWA_HD, 1)
    return jnp.where(lo_mask, a, sw), jnp.where(lo_mask, sw, a)


def _attn_kernel(rqk_ref, rv_ref, gate_ref, sq_ref, skv_ref, meta_ref, hist_ref, s0_ref, rng_ref, sink_ref,
                 omix_ref, sout_ref, kvout_ref,
                 s_scr, kd_scr, vd_scr, mk_scr, mv_scr, dec_scr, wt_scr, cs_scr, gam_scr,
                 *, tl, has_hist):
    b = pl.program_id(0)
    t = pl.program_id(1)
    nt = pl.num_programs(1)
    n_chunks = tl // CHUNK
    n_pairs = RET_HEADS // 2
    lo_tl = lax.broadcasted_iota(jnp.int32, (tl, LANES), 1) < SWA_HD
    lo_c = lax.broadcasted_iota(jnp.int32, (CHUNK, LANES), 1) < SWA_HD

    @pl.when((b == 0) & (t == 0))
    def _tables():
        i = lax.broadcasted_iota(jnp.int32, (tl, tl), 0)
        j = lax.broadcasted_iota(jnp.int32, (tl, tl), 1)
        diff = (i - j).astype(F32)
        row = lax.broadcasted_iota(jnp.int32, (tl, LANES), 0).astype(F32)
        for h in range(RET_HEADS):
            dec_scr[h] = jnp.where(diff >= 0.0, jnp.exp(jnp.maximum(diff, 0.0) * _LOG_G[h]), 0.0)
            cs_scr[h] = jnp.exp((row + 1.0) * _LOG_G[h])
        top = lax.broadcasted_iota(jnp.int32, (2 * RET_DK, RET_DV), 0) < RET_DK
        for p in range(n_pairs):
            wt_scr[p] = _decay_rows(tl, p, float(tl - 1))
            gam_scr[p] = jnp.where(top, jnp.exp(jnp.float32(tl * _LOG_G[2 * p])), jnp.exp(jnp.float32(tl * _LOG_G[2 * p + 1])))
        lo_m = lax.broadcasted_iota(jnp.int32, (N_META, LANES), 1) < SWA_HD
        mk0, mk1 = _dup_halves(meta_ref[:, 0:LANES], lo_m)
        mv0, mv1 = _dup_halves(meta_ref[:, LANES:2 * LANES], lo_m)
        mk_scr[...] = jnp.zeros(mk_scr.shape, BF16)
        mv_scr[...] = jnp.zeros(mv_scr.shape, BF16)
        mk_scr[0, 0:N_META] = mk0.astype(BF16)
        mk_scr[1, 0:N_META] = mk1.astype(BF16)
        mv_scr[0, 0:N_META] = mv0.astype(BF16)
        mv_scr[1, 0:N_META] = mv1.astype(BF16)

    @pl.when(t == 0)
    def _stream_start():
        s_scr[...] = s0_ref[0]
        if has_hist:
            lo_w = lax.broadcasted_iota(jnp.int32, (WINDOW, LANES), 1) < SWA_HD
            k0, k1 = _dup_halves(hist_ref[0, :, 0:LANES], lo_w)
            v0, v1 = _dup_halves(hist_ref[0, :, LANES:2 * LANES], lo_w)
            kd_scr[0, 0:WINDOW] = k0.astype(BF16)
            kd_scr[1, 0:WINDOW] = k1.astype(BF16)
            vd_scr[0, 0:WINDOW] = v0.astype(BF16)
            vd_scr[1, 0:WINDOW] = v1.astype(BF16)
        else:
            z = jnp.zeros((WINDOW, LANES), BF16)
            for kv in range(SWA_KV_HEADS):
                kd_scr[kv, 0:WINDOW] = z
                vd_scr[kv, 0:WINDOW] = z

    k0, k1 = _dup_halves(skv_ref[:, 0:LANES], lo_tl)
    v0, v1 = _dup_halves(skv_ref[:, LANES:2 * LANES], lo_tl)
    kd_scr[0, WINDOW:WINDOW + tl] = k0.astype(BF16)
    kd_scr[1, WINDOW:WINDOW + tl] = k1.astype(BF16)
    vd_scr[0, WINDOW:WINDOW + tl] = v0.astype(BF16)
    vd_scr[1, WINDOW:WINDOW + tl] = v1.astype(BF16)

    band = WINDOW + CHUNK
    n_keys = META_ROWS + band
    n_q = 4 * CHUNK
    scale2 = (SWA_HD ** -0.5) * LOG2E
    krow = lax.broadcasted_iota(jnp.int32, (n_keys, n_q), 0)
    zero_c = jnp.zeros((CHUNK, LANES), BF16)
    ones_v = jnp.ones((n_keys, LANES), BF16)
    for c in range(n_chunks):
        if has_hist:
            first_valid = META_ROWS
        else:
            first_valid = jnp.where(t == 0, max(META_ROWS + WINDOW - c * CHUNK, META_ROWS), META_ROWS)
        valid_t = (krow < N_META) | (krow >= first_valid)
        r0 = c * CHUNK
        for kv in range(SWA_KV_HEADS):
            keys = jnp.concatenate([mk_scr[kv], kd_scr[kv, r0:r0 + band]], axis=0)
            vals = jnp.concatenate([mv_scr[kv], vd_scr[kv, r0:r0 + band]], axis=0)
            qa = sq_ref[r0:r0 + CHUNK, (2 * kv) * LANES:(2 * kv + 1) * LANES]
            qb = sq_ref[r0:r0 + CHUNK, (2 * kv + 1) * LANES:(2 * kv + 2) * LANES]
            lhs = jnp.concatenate([jnp.where(lo_c, qa, zero_c), jnp.where(lo_c, zero_c, qa),
                                   jnp.where(lo_c, qb, zero_c), jnp.where(lo_c, zero_c, qb)], axis=0)
            s_t = lax.dot_general(keys, lhs, (((1,), (1,)), ((), ())), preferred_element_type=F32) * scale2
            s_t = jnp.where(valid_t, s_t, NEG_INF)
            s_t = jnp.where(krow == N_META, sink_ref[kv, 0:1, :] * LOG2E, s_t)
            e_t = jnp.exp2(s_t - jnp.max(s_t, axis=0, keepdims=True)).astype(BF16)
            ov = lax.dot_general(e_t, jnp.concatenate([vals, ones_v], axis=1), (((0,), (0,)), ((), ())),
                                 preferred_element_type=F32)
            o = ov[:, 0:LANES] * (1.0 / ov[:, LANES:2 * LANES])
            oa = jnp.where(lo_c, o[0:CHUNK], o[CHUNK:2 * CHUNK])
            ob = jnp.where(lo_c, o[2 * CHUNK:3 * CHUNK], o[3 * CHUNK:4 * CHUNK])
            base = RET_V + (2 * kv) * LANES
            omix_ref[r0:r0 + CHUNK, base:base + LANES] = oa.astype(BF16)
            omix_ref[r0:r0 + CHUNK, base + LANES:base + 2 * LANES] = ob.astype(BF16)

    zero_t = jnp.zeros((tl, LANES), BF16)
    for p in range(n_pairs):
        q = rqk_ref[:, p * LANES:(p + 1) * LANES]
        k = rqk_ref[:, RET_Q + p * LANES:RET_Q + (p + 1) * LANES]
        lhs = jnp.concatenate([jnp.where(lo_tl, q, zero_t), jnp.where(lo_tl, zero_t, q)], axis=0)
        s = lax.dot_general(lhs, k, (((1,), (1,)), ((), ())), preferred_element_type=F32)
        cross = jnp.dot(lhs, s_scr[p].astype(BF16), preferred_element_type=F32)
        for i in range(2):
            h = 2 * p + i
            v = rv_ref[:, h * LANES:(h + 1) * LANES]
            a = (s[i * tl:(i + 1) * tl] * dec_scr[h]).astype(BF16)
            o = jnp.dot(a, v, preferred_element_type=F32) + cross[i * tl:(i + 1) * tl] * cs_scr[h]
            r = o * lax.rsqrt(_lane_sum(o * o) * (1.0 / RET_DV) + EPS) * rng_ref[h:h + 1, :]
            omix_ref[:, h * LANES:(h + 1) * LANES] = (r * gate_ref[:, h * LANES:(h + 1) * LANES].astype(F32)).astype(BF16)
        u = _pair_update(k, rv_ref[:, (2 * p) * LANES:(2 * p + 1) * LANES],
                         rv_ref[:, (2 * p + 1) * LANES:(2 * p + 2) * LANES], wt_scr[p])
        s_scr[p] = gam_scr[p] * s_scr[p] + u

    if tl >= WINDOW:
        @pl.when(t + 1 < nt)
        def _carry_window():
            for kv in range(SWA_KV_HEADS):
                kd_scr[kv, 0:WINDOW] = kd_scr[kv, tl:tl + WINDOW]
                vd_scr[kv, 0:WINDOW] = vd_scr[kv, tl:tl + WINDOW]

    @pl.when(t + 1 == nt)
    def _stream_end():
        sout_ref[0] = s_scr[...]
        if tl >= WINDOW:
            kvout_ref[0] = skv_ref[tl - WINDOW:tl, :]
        else:
            kvout_ref[0, 0:WINDOW - tl] = hist_ref[0, tl:WINDOW, :]
            kvout_ref[0, WINDOW - tl:WINDOW] = skv_ref[...]


def _attention(rqk, rv, gate, sq, skv, meta_kv, hist_kv, s0, rng, sink_tab, *, n_streams, seq, tl, has_hist):
    nt = seq // tl
    assert tl % CHUNK == 0 and seq % tl == 0
    assert tl >= WINDOW or (nt == 1 and has_hist)
    n_pairs = RET_HEADS // 2
    s0_shared = s0.shape[0] == 1
    row = lambda b, t: (b * nt + t, 0)
    const2 = lambda b, t: (0, 0)
    const3 = lambda b, t: (0, 0, 0)
    per_b3 = lambda b, t: (b, 0, 0)
    s0_map = (lambda b, t: (0, 0, 0, 0)) if s0_shared else (lambda b, t: (b, 0, 0, 0))
    hist_map = per_b3 if has_hist else const3
    rows = n_streams * seq
    return pl.pallas_call(
        functools.partial(_attn_kernel, tl=tl, has_hist=has_hist),
        grid=(n_streams, nt),
        in_specs=[pl.BlockSpec((tl, 2 * RET_Q), row),
                  pl.BlockSpec((tl, RET_V), row),
                  pl.BlockSpec((tl, RET_V), row),
                  pl.BlockSpec((tl, SWA_Q), row),
                  pl.BlockSpec((tl, 2 * SWA_KV), row),
                  pl.BlockSpec((N_META, 2 * SWA_KV), const2),
                  pl.BlockSpec((1, WINDOW, 2 * SWA_KV), hist_map),
                  pl.BlockSpec((1, n_pairs, 2 * RET_DK, RET_DV), s0_map),
                  pl.BlockSpec((RET_HEADS, RET_DV), const2),
                  pl.BlockSpec((SWA_KV_HEADS, SUBLANES, 4 * CHUNK), const3)],
        out_specs=[pl.BlockSpec((tl, MIX_WIDTH), row),
                   pl.BlockSpec((1, n_pairs, 2 * RET_DK, RET_DV), lambda b, t: (b, 0, 0, 0)),
                   pl.BlockSpec((1, WINDOW, 2 * SWA_KV), per_b3)],
        out_shape=[jax.ShapeDtypeStruct((rows, MIX_WIDTH), BF16),
                   jax.ShapeDtypeStruct((n_streams, n_pairs, 2 * RET_DK, RET_DV), F32),
                   jax.ShapeDtypeStruct((n_streams, WINDOW, 2 * SWA_KV), F32)],
        scratch_shapes=[pltpu.VMEM((n_pairs, 2 * RET_DK, RET_DV), F32),
                        pltpu.VMEM((SWA_KV_HEADS, WINDOW + tl, LANES), BF16),
                        pltpu.VMEM((SWA_KV_HEADS, WINDOW + tl, LANES), BF16),
                        pltpu.VMEM((SWA_KV_HEADS, META_ROWS, LANES), BF16),
                        pltpu.VMEM((SWA_KV_HEADS, META_ROWS, LANES), BF16),
                        pltpu.VMEM((RET_HEADS, tl, tl), F32),
                        pltpu.VMEM((n_pairs, tl, LANES), F32),
                        pltpu.VMEM((RET_HEADS, tl, RET_DV), F32),
                        pltpu.VMEM((n_pairs, 2 * RET_DK, RET_DV), F32)],
        compiler_params=_params(2),
        name="attention",
    )(rqk, rv, gate, sq, skv, meta_kv, hist_kv, s0, rng, sink_tab)


def _post_kernel(omix_ref, x_ref, wout_ref, g2_ref, wrt_ref, base_ref,
                 xmid_ref, hn_ref, wcol_ref, rt_ref, cnt_ref, tri_scr, run_scr):
    i = pl.program_id(0)
    tm = x_ref.shape[0]

    @pl.when(i == 0)
    def _init():
        r = lax.broadcasted_iota(jnp.int32, (tm, tm), 0)
        c = lax.broadcasted_iota(jnp.int32, (tm, tm), 1)
        tri_scr[...] = jnp.where(r < c, 1.0, 0.0).astype(BF16)
        run_scr[...] = base_ref[...]

    xm = x_ref[...] + jnp.dot(omix_ref[...], wout_ref[...], preferred_element_type=F32)
    xmid_ref[...] = xm
    hn = _row_rms(xm) * g2_ref[...]
    hn_ref[...] = hn
    lt = lax.dot_general(wrt_ref[...], hn.astype(BF16), (((1,), (1,)), ((), ())), preferred_element_type=F32)
    row8 = lax.broadcasted_iota(jnp.int32, (SUBLANES, tm), 0)
    big = jnp.int32(SUBLANES)
    gl = jnp.where(row8 < N_GROUPS, lt[0:SUBLANES], NEG_INF)
    gmax = jnp.max(gl, axis=0, keepdims=True)
    gsum = jnp.sum(jnp.exp(gl - gmax), axis=0, keepdims=True)
    g_sel = jnp.min(jnp.where(gl == gmax, row8, big), axis=0, keepdims=True)
    p_sel = 1.0 / gsum
    el = lt[ROUTER_EXPERT_ROW0:ROUTER_EXPERT_ROW0 + EXPERTS_PER_GROUP]
    for g in range(1, N_GROUPS):
        lo = ROUTER_EXPERT_ROW0 + g * EXPERTS_PER_GROUP
        el = jnp.where(g_sel == g, lt[lo:lo + EXPERTS_PER_GROUP], el)
    m1 = jnp.max(el, axis=0, keepdims=True)
    i1 = jnp.min(jnp.where(el == m1, row8, big), axis=0, keepdims=True)
    el2 = jnp.where(row8 == i1, NEG_INF, el)
    m2 = jnp.max(el2, axis=0, keepdims=True)
    i2 = jnp.min(jnp.where(el2 == m2, row8, big), axis=0, keepdims=True)
    e2 = jnp.exp(m2 - m1)
    inv = 1.0 / (1.0 + e2)
    w1 = p_sel * inv
    w2 = p_sel * (e2 * inv)
    eid1 = g_sel * EXPERTS_PER_GROUP + i1
    eid2 = g_sel * EXPERTS_PER_GROUP + i2

    rowe = lax.broadcasted_iota(jnp.int32, (N_EXPERTS, tm), 0)
    oh = jnp.where((rowe == eid1) | (rowe == eid2), 1.0, 0.0).astype(BF16)
    run = run_scr[...]
    pref = jnp.dot(oh, tri_scr[...], preferred_element_type=F32) + jnp.concatenate([run] * (tm // LANES), axis=1)
    r1 = jnp.sum(jnp.where(rowe == eid1, pref, 0.0), axis=0, keepdims=True)
    r2 = jnp.sum(jnp.where(rowe == eid2, pref, 0.0), axis=0, keepdims=True)
    run = run + jnp.dot(oh, jnp.ones((tm, LANES), BF16), preferred_element_type=F32)
    run_scr[...] = run
    cnt_ref[...] = run

    out = jnp.zeros((SUBLANES, tm), F32)
    for k, v in enumerate([eid1.astype(F32), eid2.astype(F32), w1, w2, r1, r2]):
        out = jnp.where(row8 == k, v, out)
    rt_ref[...] = out
    rowl = lax.broadcasted_iota(jnp.int32, (LANES, tm), 0)
    wcol_ref[...] = jnp.where(rowl == 0, w1, jnp.where(rowl == 1, w2, 0.0)).T


def _post(omix, x2d, w_out_bf, g2, w_router_t_bf, base_cnt, tm):
    t_rows = x2d.shape[0]
    assert tm % LANES == 0
    row = lambda i: (i, 0)
    const = lambda i: (0, 0)
    return pl.pallas_call(
        _post_kernel,
        grid=(t_rows // tm,),
        in_specs=[pl.BlockSpec((tm, MIX_WIDTH), row),
                  pl.BlockSpec((tm, D_MODEL), row),
                  pl.BlockSpec((MIX_WIDTH, D_MODEL), const),
                  pl.BlockSpec((1, D_MODEL), const),
                  pl.BlockSpec((ROUTER_ROWS, D_MODEL), const),
                  pl.BlockSpec((N_EXPERTS, LANES), const)],
        out_specs=[pl.BlockSpec((tm, D_MODEL), row),
                   pl.BlockSpec((tm, D_MODEL), row),
                   pl.BlockSpec((tm, LANES), row),
                   pl.BlockSpec((SUBLANES, tm), lambda i: (0, i)),
                   pl.BlockSpec((N_EXPERTS, LANES), const)],
        out_shape=[jax.ShapeDtypeStruct((t_rows, D_MODEL), F32),
                   jax.ShapeDtypeStruct((t_rows, D_MODEL), F32),
                   jax.ShapeDtypeStruct((t_rows, LANES), F32),
                   jax.ShapeDtypeStruct((SUBLANES, t_rows), F32),
                   jax.ShapeDtypeStruct((N_EXPERTS, LANES), F32)],
        scratch_shapes=[pltpu.VMEM((tm, tm), BF16), pltpu.VMEM((N_EXPERTS, LANES), F32)],
        compiler_params=_params(1),
        name="post",
    )(omix, x2d, w_out_bf, g2, w_router_t_bf, base_cnt)


def _step_major(pos, tm):
    return pos.reshape(2, -1, tm).transpose(1, 0, 2)


def _row_copy(src_ref, src_row, dst_ref, dst_row, sem):
    return pltpu.make_async_copy(src_ref.at[pl.ds(src_row, 1)], dst_ref.at[pl.ds(dst_row, 1)], sem)


def _dispatch_kernel(pos_ref, pad_ref, *refs, tm, n_pad, group_steps):
    hn_refs = refs[:len(group_steps)]
    xs_ref, zero_scr, sem = refs[len(group_steps):]
    i = pl.program_id(0)

    @pl.when(i == 0)
    def _init():
        zero_scr[...] = jnp.zeros(zero_scr.shape, F32)

    first = 0
    for hn_ref, steps in zip(hn_refs, group_steps):
        @pl.when((i >= first) & (i < first + steps))
        def _rows(hn_ref=hn_ref):
            def body(blk, carry):
                for k in range(SUBLANES):
                    r = blk * SUBLANES + k
                    for e in range(2):
                        pltpu.make_async_copy(hn_ref.at[blk, pl.ds(k, 1)], xs_ref.at[pl.ds(pos_ref[0, e, r], 1)],
                                              sem).start(priority=e)
                return carry

            lax.fori_loop(0, tm // SUBLANES, body, 0)
        first += steps

    for j in range(n_pad):
        _row_copy(zero_scr, 0, xs_ref, pad_ref[0, 0, j], sem).start(priority=j % 2)
    n_rows = 2 * tm + n_pad
    pltpu.make_async_copy(xs_ref.at[pl.ds(0, n_rows)], xs_ref.at[pl.ds(0, n_rows)], sem).wait()


def _dispatch(pos_list, pad, hn_list, n_rows_out, tm):
    group_steps = tuple(hn.shape[0] // tm for hn in hn_list)
    n_steps = sum(group_steps)
    n_pad = pad.shape[0] // n_steps
    assert pad.shape[0] == n_steps * n_pad and tm % SUBLANES == 0
    pos3 = jnp.concatenate([_step_major(p, tm) for p in pos_list], axis=0)
    pad3 = pad.reshape(n_steps, 1, n_pad)
    smem = lambda r, n: pl.BlockSpec((1, r, n), lambda i: (i, 0, 0), memory_space=pltpu.SMEM)
    hn_specs = []
    first = 0
    for steps in group_steps:
        hn_specs.append(pl.BlockSpec((tm // SUBLANES, SUBLANES, D_MODEL),
                                     lambda i, first=first, steps=steps: (jnp.clip(i - first, 0, steps - 1), 0, 0)))
        first += steps
    hn_list = [hn.reshape(hn.shape[0] // SUBLANES, SUBLANES, D_MODEL) for hn in hn_list]
    return pl.pallas_call(
        functools.partial(_dispatch_kernel, tm=tm, n_pad=n_pad, group_steps=group_steps),
        grid=(n_steps,),
        in_specs=[smem(2, tm), smem(1, n_pad)] + hn_specs,
        out_specs=pl.BlockSpec(memory_space=pl.ANY),
        out_shape=jax.ShapeDtypeStruct((n_rows_out, D_MODEL), F32),
        scratch_shapes=[pltpu.VMEM((8, D_MODEL), F32), pltpu.SemaphoreType.DMA(())],
        compiler_params=_params(1),
        name="dispatch",
    )(pos3, pad3, *hn_list)


def _expert_kernel(te_ref, nv_ref, x_ref, wgu_ref, wd_ref, y_ref):
    i = pl.program_id(0)

    @pl.when(i < nv_ref[0])
    def _compute():
        x = x_ref[...].astype(BF16)
        gu = jnp.dot(x, wgu_ref[0], preferred_element_type=F32)
        g = gu[:, 0:EXPERT_FF]
        a = (g * jax.nn.sigmoid(g) * gu[:, EXPERT_FF:2 * EXPERT_FF]).astype(BF16)
        y_ref[...] = jnp.dot(a, wd_ref[0], preferred_element_type=F32)

    @pl.when(i >= nv_ref[0])
    def _skip():
        y_ref[...] = jnp.zeros(y_ref.shape, F32)


def _experts(tile_expert, n_valid, xs, wgu_bf, wd_bf, n_tiles, tm):
    last = lambda i, nv: jnp.minimum(i, nv[0] - 1)
    return pl.pallas_call(
        _expert_kernel,
        grid_spec=pltpu.PrefetchScalarGridSpec(
            num_scalar_prefetch=2,
            grid=(n_tiles,),
            in_specs=[pl.BlockSpec((tm, D_MODEL), lambda i, te, nv: (last(i, nv), 0)),
                      pl.BlockSpec((1, D_MODEL, 2 * EXPERT_FF), lambda i, te, nv: (te[last(i, nv)], 0, 0)),
                      pl.BlockSpec((1, EXPERT_FF, D_MODEL), lambda i, te, nv: (te[last(i, nv)], 0, 0))],
            out_specs=pl.BlockSpec((tm, D_MODEL), lambda i, te, nv: (i, 0))),
        out_shape=jax.ShapeDtypeStruct((n_tiles * tm, D_MODEL), F32),
        compiler_params=_params(1),
        name="experts",
    )(tile_expert, n_valid, xs, wgu_bf, wd_bf)


def _combine_kernel(pos_ref, nxt_ref, ys_ref, xmid_ref, wcol_ref, out_ref, ybuf, sems, *, tm, n_steps):
    i = pl.program_id(0)
    slot = i % 2

    def issue(p_ref, to_slot):
        def body(blk, carry):
            for k in range(SUBLANES):
                r = blk * SUBLANES + k
                for e in range(2):
                    pltpu.make_async_copy(ys_ref.at[pl.ds(p_ref[0, e, r], 1)],
                                          ybuf.at[to_slot, e, blk, pl.ds(k, 1)], sems.at[to_slot]).start(priority=e)
            return carry

        lax.fori_loop(0, tm // SUBLANES, body, 0)

    @pl.when(i == 0)
    def _first():
        issue(pos_ref, 0)

    @pl.when(i + 1 < n_steps)
    def _ahead():
        issue(nxt_ref, 1 - slot)

    for e in range(2):
        pltpu.make_async_copy(ybuf.at[slot, e], ybuf.at[slot, e], sems.at[slot]).wait()
    w = wcol_ref[...]
    y0 = ybuf[slot, 0].reshape(tm, D_MODEL)
    y1 = ybuf[slot, 1].reshape(tm, D_MODEL)
    out_ref[...] = xmid_ref[...] + w[:, 0:1] * y0 + w[:, 1:2] * y1


def _combine(pos, ys, xmid, wcol, tm):
    t_rows = xmid.shape[0]
    n_steps = t_rows // tm
    assert tm % SUBLANES == 0
    pos3 = _step_major(pos, tm)
    row = lambda i: (i, 0)
    return pl.pallas_call(
        functools.partial(_combine_kernel, tm=tm, n_steps=n_steps),
        grid=(n_steps,),
        in_specs=[pl.BlockSpec((1, 2, tm), lambda i: (i, 0, 0), memory_space=pltpu.SMEM),
                  pl.BlockSpec((1, 2, tm), lambda i: (jnp.minimum(i + 1, n_steps - 1), 0, 0),
                               memory_space=pltpu.SMEM),
                  pl.BlockSpec(memory_space=pl.ANY),
                  pl.BlockSpec((tm, D_MODEL), row),
                  pl.BlockSpec((tm, LANES), row)],
        out_specs=pl.BlockSpec((tm, D_MODEL), row),
        out_shape=jax.ShapeDtypeStruct((t_rows, D_MODEL), F32),
        scratch_shapes=[pltpu.VMEM((2, 2, tm // SUBLANES, SUBLANES, D_MODEL), F32), pltpu.SemaphoreType.DMA((2,))],
        compiler_params=_params(1),
        name="combine",
    )(pos3, pos3, ys, xmid, wcol)


def _bucket(ends, idx):
    n = jnp.sum((ends[None, :] <= idx[:, None]).astype(jnp.int32), axis=1)
    return jnp.minimum(n, ends.shape[0] - 1)


def _tile_for(rows, pref):
    tm = min(pref, rows)
    assert rows % tm == 0
    return tm


def kernel(x_prompt, x_sample, cache_ret_state, cache_swa_k, cache_swa_v, meta_tokens, norm1_g, w_in, q_norm_g,
           k_norm_g, ret_norm_g, attn_sinks, w_out, norm2_g, w_group, w_expert, w_gate, w_up, w_down):
    assert norm1_g.shape[0] == 1, "single-layer trunk"
    bp, lp, _ = x_prompt.shape
    bs, ls, _ = x_sample.shape
    n_pairs = RET_HEADS // 2

    g1 = norm1_g[0][None, :]
    g2 = norm2_g[0][None, :]
    w_in_bf = w_in[0].astype(BF16)
    w_out_bf = w_out[0].astype(BF16)
    qg2 = jnp.tile(q_norm_g[0], 2)[None, :]
    kg2 = jnp.tile(k_norm_g[0], 2)[None, :]
    rng = ret_norm_g[0].reshape(RET_HEADS, RET_DV)
    sink_tab = jnp.broadcast_to(jnp.repeat(attn_sinks[0], CHUNK).reshape(SWA_KV_HEADS, 1, 4 * CHUNK),
                                (SWA_KV_HEADS, SUBLANES, 4 * CHUNK))
    w_router_t = jnp.zeros((ROUTER_ROWS, D_MODEL), F32)
    w_router_t = w_router_t.at[0:N_GROUPS].set(w_group[0].T)
    w_router_t = w_router_t.at[ROUTER_EXPERT_ROW0:ROUTER_EXPERT_ROW0 + N_EXPERTS].set(w_expert[0].T)
    w_router_t_bf = w_router_t.astype(BF16)
    wgu_bf = jnp.concatenate([w_gate[0], w_up[0]], axis=-1).astype(BF16)
    wd_bf = w_down[0].astype(BF16)

    meta_rows = 2 * CHUNK
    m_pad = jnp.zeros((meta_rows, D_MODEL), F32).at[0:N_META].set(meta_tokens)
    m_rqk, m_rv, _, _, m_skv = _proj(m_pad, jnp.arange(meta_rows, dtype=jnp.int32), meta_rows, g1, w_in_bf, qg2, kg2)
    s_meta = _meta_state(m_rqk, m_rv)[None]
    meta_kv = m_skv[0:N_META]

    groups = [
        dict(x=x_prompt.reshape(bp * lp, D_MODEL), n=bp, seq=lp, pos0=N_META, has_hist=False, s0=s_meta,
             hist=jnp.zeros((1, WINDOW, 2 * SWA_KV), F32)),
        dict(x=x_sample.reshape(bs * ls, D_MODEL), n=bs, seq=ls, pos0=N_META + PAST_LEN, has_hist=True,
             s0=cache_ret_state[0].reshape(bs, n_pairs, 2 * RET_DK, RET_DV),
             hist=jnp.concatenate([cache_swa_k[0].reshape(bs, WINDOW, SWA_KV),
                                   cache_swa_v[0].reshape(bs, WINDOW, SWA_KV)], axis=-1)),
    ]

    base_cnt = jnp.zeros((N_EXPERTS, LANES), F32)
    for g in groups:
        rows = g["n"] * g["seq"]
        tm = _tile_for(rows, TOKEN_TILE)
        pos = g["pos0"] + jnp.arange(g["seq"], dtype=jnp.int32)
        if g["seq"] < tm:
            assert tm % g["seq"] == 0
            pos = jnp.tile(pos, tm // g["seq"])
        else:
            assert g["seq"] % tm == 0
        rqk, rv, gate, sq, skv = _proj(g["x"], pos, tm, g1, w_in_bf, qg2, kg2)
        tl = min(ATTN_TILE, g["seq"])
        omix, s_out, kv_out = _attention(rqk, rv, gate, sq, skv, meta_kv, g["hist"], g["s0"], rng, sink_tab,
                                         n_streams=g["n"], seq=g["seq"], tl=tl, has_hist=g["has_hist"])
        xmid, hn, wcol, route_t, base_cnt = _post(omix, g["x"], w_out_bf, g2, w_router_t_bf, base_cnt, tm)
        g.update(xmid=xmid, hn=hn, wcol=wcol, route_t=route_t, s_out=s_out, kv_out=kv_out)

    te = EXPERT_TILE
    total_rows = sum(g["n"] * g["seq"] for g in groups)
    n_tiles = (2 * total_rows) // te + N_EXPERTS
    n_slots = n_tiles * te
    counts = base_cnt[:, 0].astype(jnp.int32)
    tiles_e = (counts + te - 1) // te
    padded = tiles_e * te
    off = jnp.cumsum(padded) - padded
    tile_end = jnp.cumsum(tiles_e)
    n_valid = tile_end[-1:].astype(jnp.int32)
    tile_expert = _bucket(tile_end, jnp.arange(n_tiles, dtype=jnp.int32))
    pad_e = padded - counts
    pad_end = jnp.cumsum(pad_e)
    n_fill = n_slots - 2 * total_rows
    move_steps = sum((g["n"] * g["seq"]) // _tile_for(g["n"] * g["seq"], MOVE_TILE) for g in groups)
    fill_per_step = 8 * (-(-n_fill // (8 * move_steps)))
    j = jnp.arange(fill_per_step * move_steps, dtype=jnp.int32)
    e_of = _bucket(pad_end, j)
    in_expert = (off + counts)[e_of] + (j - (pad_end - pad_e)[e_of])
    in_tail = (off[-1] + padded[-1]) + (j - pad_end[-1])
    fill_slots = jnp.where(j < pad_end[-1], in_expert, jnp.where(j < n_fill, in_tail, n_slots + (j - n_fill)))
    fill_slots = fill_slots.astype(jnp.int32)
    n_rows_xs = n_slots + fill_per_step * move_steps - n_fill

    for g in groups:
        eid = g["route_t"][0:2].astype(jnp.int32)
        off_sel = jnp.sum(jnp.where(eid[None] == jnp.arange(N_EXPERTS, dtype=jnp.int32)[:, None, None],
                                    off[:, None, None], 0), axis=0)
        g["pos"] = (off_sel + g["route_t"][4:6].astype(jnp.int32)).astype(jnp.int32)
    xs = _dispatch([g["pos"] for g in groups], fill_slots, [g["hn"] for g in groups], n_rows_xs, MOVE_TILE)

    ys = _experts(tile_expert, n_valid, xs, wgu_bf, wd_bf, n_tiles, te)

    outs = []
    for g in groups:
        rows = g["n"] * g["seq"]
        y = _combine(g["pos"], ys, g["xmid"], g["wcol"], _tile_for(rows, MOVE_TILE))
        outs.append(y.reshape(g["n"], g["seq"], D_MODEL))

    def caches(g):
        kv = g["kv_out"]
        k = kv[:, :, 0:SWA_KV].reshape(g["n"], WINDOW, SWA_KV_HEADS, SWA_HD)[None]
        v = kv[:, :, SWA_KV:2 * SWA_KV].reshape(g["n"], WINDOW, SWA_KV_HEADS, SWA_HD)[None]
        s = g["s_out"].reshape(g["n"], RET_HEADS, RET_DK, RET_DV)[None]
        return s, k, v

    sp, kp, vp = caches(groups[0])
    ss, ks, vs = caches(groups[1])
    return (outs[0], outs[1], sp, kp, vp, ss, ks, vs)
```

```python
import functools

import numpy as np
import jax
import jax.numpy as jnp
from jax import lax
from jax.experimental import pallas as pl
from jax.experimental.pallas import tpu as pltpu

F32 = jnp.float32
BF16 = jnp.bfloat16

D_MODEL = 1024
PAST_LEN = 4096
CHUNK = 64
N_META = 16
RET_HEADS = 4
RET_DK = 64
RET_DV = 128
SWA_HEADS = 8
SWA_KV_HEADS = 2
SWA_HD = 64
WINDOW = 128
ROPE_THETA = 10000.0
N_GROUPS = 4
EXPERTS_PER_GROUP = 8
N_EXPERTS = N_GROUPS * EXPERTS_PER_GROUP
EXPERT_FF = 256
EPS = 1e-6
NEG_INF = -1e30
LOG2E = float(np.log2(np.e))
RET_Q = RET_HEADS * RET_DK
RET_V = RET_HEADS * RET_DV
SWA_Q = SWA_HEADS * SWA_HD
SWA_KV = SWA_KV_HEADS * SWA_HD
MIX_WIDTH = RET_V + SWA_Q
IN_WIDTH = 2 * RET_Q + 2 * RET_V + SWA_Q + 2 * SWA_KV

LANES = 128
TOKEN_TILE = 512
ATTN_TILE = 256
EXPERT_TILE = 256
MOVE_TILE = 256
SUBLANES = 8
ROUTER_EXPERT_ROW0 = 8
ROUTER_ROWS = 64
META_ROWS = 64
VMEM_LIMIT = 56 * 1024 * 1024

_LOG_G = [float(np.log1p(-np.exp2(-5.0 - h))) for h in range(RET_HEADS)]


def _params(n_axes):
    return pltpu.CompilerParams(dimension_semantics=("arbitrary",) * n_axes, vmem_limit_bytes=VMEM_LIMIT)


def _split_bf16(a):
    hi = a.astype(BF16)
    return hi, (a - hi.astype(F32)).astype(BF16)


def _split_dot(a, w2):
    hi, lo = _split_bf16(a)
    return jnp.dot(jnp.concatenate([hi, lo], axis=1), w2, preferred_element_type=F32)


def _lane_sum(a):
    return _split_dot(a, jnp.ones((2 * LANES, LANES), BF16))


def _head_sum_matrix():
    i = lax.broadcasted_iota(jnp.int32, (2 * LANES, LANES), 0) % LANES
    j = lax.broadcasted_iota(jnp.int32, (2 * LANES, LANES), 1)
    return jnp.where((i < SWA_HD) == (j < SWA_HD), 1.0, 0.0).astype(BF16)


def _rope(t, c, s1, s2):
    half = SWA_HD // 2
    return t * c + pltpu.roll(t, LANES - half, 1) * s1 + pltpu.roll(t, half, 1) * s2


def _head_rms(t, g, head_w):
    ms = _split_dot(t * t, head_w) * (1.0 / SWA_HD)
    return t * lax.rsqrt(ms + EPS) * g


def _row_rms(x):
    n_tiles = x.shape[1] // LANES
    ss = x[:, 0:LANES] * x[:, 0:LANES]
    for j in range(1, n_tiles):
        ss = ss + x[:, j * LANES:(j + 1) * LANES] * x[:, j * LANES:(j + 1) * LANES]
    r = lax.rsqrt(_lane_sum(ss) * (1.0 / x.shape[1]) + EPS)
    return x * jnp.concatenate([r] * n_tiles, axis=1)


def _proj_kernel(x_ref, g1_ref, w_ref, qg_ref, kg_ref, cos_ref, s1_ref, s2_ref,
                 rqk_ref, rv_ref, gate_ref, sq_ref, skv_ref):
    xn = (_row_rms(x_ref[...]) * g1_ref[...]).astype(BF16)
    c, s1, s2 = cos_ref[...], s1_ref[...], s2_ref[...]
    head_w = _head_sum_matrix()

    def seg(a, b):
        return jnp.dot(xn, w_ref[:, a:b], preferred_element_type=F32)

    def tile(h, j):
        return h[:, j * LANES:(j + 1) * LANES]

    h = seg(0, 2 * RET_Q)
    for j in range(2):
        rqk_ref[:, j * LANES:(j + 1) * LANES] = _rope(tile(h, j), c, s1, s2).astype(BF16)
    for j in range(2, 4):
        rqk_ref[:, j * LANES:(j + 1) * LANES] = (_rope(tile(h, j), c, s1, s2) * (RET_DK ** -0.5)).astype(BF16)
    a = 2 * RET_Q
    rv_ref[...] = seg(a, a + RET_V).astype(BF16)
    a += RET_V
    g = seg(a, a + RET_V)
    gate_ref[...] = (g * jax.nn.sigmoid(g)).astype(BF16)
    a += RET_V
    h = seg(a, a + SWA_Q)
    qg = qg_ref[...]
    for j in range(SWA_Q // LANES):
        sq_ref[:, j * LANES:(j + 1) * LANES] = _rope(_head_rms(tile(h, j), qg, head_w), c, s1, s2).astype(BF16)
    a += SWA_Q
    h = seg(a, a + 2 * SWA_KV)
    skv_ref[:, 0:LANES] = _rope(_head_rms(tile(h, 0), kg_ref[...], head_w), c, s1, s2)
    skv_ref[:, LANES:2 * LANES] = tile(h, 1)


def _rope_tables(pos):
    half = SWA_HD // 2
    inv = ROPE_THETA ** (-jnp.arange(half, dtype=F32) / half)
    ang = pos.astype(F32)[:, None] * inv[None, :]
    cos, sin = jnp.cos(ang), jnp.sin(ang)
    z = jnp.zeros_like(sin)
    return (jnp.tile(cos, (1, 4)),
            jnp.tile(jnp.concatenate([-sin, z], axis=1), (1, 2)),
            jnp.tile(jnp.concatenate([z, sin], axis=1), (1, 2)))


def _proj(x2d, pos_rows, tm, g1, w_in_bf, qg2, kg2):
    t_rows = x2d.shape[0]
    n_tiles = t_rows // tm
    n_pos_tiles = pos_rows.shape[0] // tm
    cos, s1, s2 = _rope_tables(pos_rows)
    row = lambda i: (i, 0)
    const = lambda i: (0, 0)
    tab = lambda i: (i % n_pos_tiles, 0)
    return pl.pallas_call(
        _proj_kernel,
        grid=(n_tiles,),
        in_specs=[pl.BlockSpec((tm, D_MODEL), row),
                  pl.BlockSpec((1, D_MODEL), const),
                  pl.BlockSpec((D_MODEL, IN_WIDTH), const),
                  pl.BlockSpec((1, LANES), const),
                  pl.BlockSpec((1, LANES), const),
                  pl.BlockSpec((tm, LANES), tab),
                  pl.BlockSpec((tm, LANES), tab),
                  pl.BlockSpec((tm, LANES), tab)],
        out_specs=[pl.BlockSpec((tm, 2 * RET_Q), row),
                   pl.BlockSpec((tm, RET_V), row),
                   pl.BlockSpec((tm, RET_V), row),
                   pl.BlockSpec((tm, SWA_Q), row),
                   pl.BlockSpec((tm, 2 * SWA_KV), row)],
        out_shape=[jax.ShapeDtypeStruct((t_rows, 2 * RET_Q), BF16),
                   jax.ShapeDtypeStruct((t_rows, RET_V), BF16),
                   jax.ShapeDtypeStruct((t_rows, RET_V), BF16),
                   jax.ShapeDtypeStruct((t_rows, SWA_Q), BF16),
                   jax.ShapeDtypeStruct((t_rows, 2 * SWA_KV), F32)],
        compiler_params=_params(1),
        name="proj",
    )(x2d, g1, w_in_bf, qg2, kg2, cos, s1, s2)


def _pair_update(k_bf, v0_bf, v1_bf, wt):
    kw = (k_bf.astype(F32) * wt).astype(BF16)
    dn = (((0,), (0,)), ((), ()))
    a0 = lax.dot_general(kw, v0_bf, dn, preferred_element_type=F32)
    a1 = lax.dot_general(kw, v1_bf, dn, preferred_element_type=F32)
    top = lax.broadcasted_iota(jnp.int32, a0.shape, 0) < RET_DK
    return jnp.where(top, a0, a1)


def _decay_rows(n, pair, rows_back_from):
    i = lax.broadcasted_iota(jnp.int32, (n, LANES), 0).astype(F32)
    lane = lax.broadcasted_iota(jnp.int32, (n, LANES), 1)
    lg = jnp.where(lane < RET_DK, _LOG_G[2 * pair], _LOG_G[2 * pair + 1])
    return jnp.exp((rows_back_from - i) * lg)


def _meta_state_kernel(rqk_ref, rv_ref, s_ref, *, n_rows):
    for p in range(RET_HEADS // 2):
        k = rqk_ref[:, RET_Q + p * LANES:RET_Q + (p + 1) * LANES]
        wt = _decay_rows(n_rows, p, float(N_META - 1))
        s_ref[p] = _pair_update(k, rv_ref[:, (2 * p) * LANES:(2 * p + 1) * LANES],
                                rv_ref[:, (2 * p + 1) * LANES:(2 * p + 2) * LANES], wt)


def _meta_state(m_rqk, m_rv):
    n_rows = m_rqk.shape[0]
    return pl.pallas_call(
        functools.partial(_meta_state_kernel, n_rows=n_rows),
        out_shape=jax.ShapeDtypeStruct((RET_HEADS // 2, 2 * RET_DK, RET_DV), F32),
        name="meta_state",
    )(m_rqk, m_rv)


def _dup_halves(a, lo_mask):
    sw = pltpu.roll(a, SWA_HD, 1)
    return jnp.where(lo_mask, a, sw), jnp.where(lo_mask, sw, a)


def _attn_kernel(rqk_ref, rv_ref, gate_ref, sq_ref, skv_ref, meta_ref, hist_ref, s0_ref, rng_ref, sink_ref,
                 omix_ref, sout_ref, kvout_ref,
                 s_scr, kd_scr, vd_scr, mk_scr, mv_scr, dec_scr, wt_scr, cs_scr, gam_scr,
                 *, tl, has_hist):
    b = pl.program_id(0)
    t = pl.program_id(1)
    nt = pl.num_programs(1)
    n_chunks = tl // CHUNK
    n_pairs = RET_HEADS // 2
    lo_tl = lax.broadcasted_iota(jnp.int32, (tl, LANES), 1) < SWA_HD
    lo_c = lax.broadcasted_iota(jnp.int32, (CHUNK, LANES), 1) < SWA_HD

    @pl.when((b == 0) & (t == 0))
    def _tables():
        i = lax.broadcasted_iota(jnp.int32, (tl, tl), 0)
        j = lax.broadcasted_iota(jnp.int32, (tl, tl), 1)
        diff = (i - j).astype(F32)
        row = lax.broadcasted_iota(jnp.int32, (tl, LANES), 0).astype(F32)
        for h in range(RET_HEADS):
            dec_scr[h] = jnp.where(diff >= 0.0, jnp.exp(jnp.maximum(diff, 0.0) * _LOG_G[h]), 0.0)
            cs_scr[h] = jnp.exp((row + 1.0) * _LOG_G[h])
        top = lax.broadcasted_iota(jnp.int32, (2 * RET_DK, RET_DV), 0) < RET_DK
        for p in range(n_pairs):
            wt_scr[p] = _decay_rows(tl, p, float(tl - 1))
            gam_scr[p] = jnp.where(top, jnp.exp(jnp.float32(tl * _LOG_G[2 * p])), jnp.exp(jnp.float32(tl * _LOG_G[2 * p + 1])))
        lo_m = lax.broadcasted_iota(jnp.int32, (N_META, LANES), 1) < SWA_HD
        mk0, mk1 = _dup_halves(meta_ref[:, 0:LANES], lo_m)
        mv0, mv1 = _dup_halves(meta_ref[:, LANES:2 * LANES], lo_m)
        mk_scr[...] = jnp.zeros(mk_scr.shape, BF16)
        mv_scr[...] = jnp.zeros(mv_scr.shape, BF16)
        mk_scr[0, 0:N_META] = mk0.astype(BF16)
        mk_scr[1, 0:N_META] = mk1.astype(BF16)
        mv_scr[0, 0:N_META] = mv0.astype(BF16)
        mv_scr[1, 0:N_META] = mv1.astype(BF16)

    @pl.when(t == 0)
    def _stream_start():
        s_scr[...] = s0_ref[0]
        if has_hist:
            lo_w = lax.broadcasted_iota(jnp.int32, (WINDOW, LANES), 1) < SWA_HD
            k0, k1 = _dup_halves(hist_ref[0, :, 0:LANES], lo_w)
            v0, v1 = _dup_halves(hist_ref[0, :, LANES:2 * LANES], lo_w)
            kd_scr[0, 0:WINDOW] = k0.astype(BF16)
            kd_scr[1, 0:WINDOW] = k1.astype(BF16)
            vd_scr[0, 0:WINDOW] = v0.astype(BF16)
            vd_scr[1, 0:WINDOW] = v1.astype(BF16)
        else:
            z = jnp.zeros((WINDOW, LANES), BF16)
            for kv in range(SWA_KV_HEADS):
                kd_scr[kv, 0:WINDOW] = z
                vd_scr[kv, 0:WINDOW] = z

    k0, k1 = _dup_halves(skv_ref[:, 0:LANES], lo_tl)
    v0, v1 = _dup_halves(skv_ref[:, LANES:2 * LANES], lo_tl)
    kd_scr[0, WINDOW:WINDOW + tl] = k0.astype(BF16)
    kd_scr[1, WINDOW:WINDOW + tl] = k1.astype(BF16)
    vd_scr[0, WINDOW:WINDOW + tl] = v0.astype(BF16)
    vd_scr[1, WINDOW:WINDOW + tl] = v1.astype(BF16)

    band = WINDOW + CHUNK
    n_keys = META_ROWS + band
    n_q = 4 * CHUNK
    scale2 = (SWA_HD ** -0.5) * LOG2E
    krow = lax.broadcasted_iota(jnp.int32, (n_keys, n_q), 0)
    zero_c = jnp.zeros((CHUNK, LANES), BF16)
    ones_v = jnp.ones((n_keys, LANES), BF16)
    for c in range(n_chunks):
        if has_hist:
            first_valid = META_ROWS
        else:
            first_valid = jnp.where(t == 0, max(META_ROWS + WINDOW - c * CHUNK, META_ROWS), META_ROWS)
        valid_t = (krow < N_META) | (krow >= first_valid)
        r0 = c * CHUNK
        for kv in range(SWA_KV_HEADS):
            keys = jnp.concatenate([mk_scr[kv], kd_scr[kv, r0:r0 + band]], axis=0)
            vals = jnp.concatenate([mv_scr[kv], vd_scr[kv, r0:r0 + band]], axis=0)
            qa = sq_ref[r0:r0 + CHUNK, (2 * kv) * LANES:(2 * kv + 1) * LANES]
            qb = sq_ref[r0:r0 + CHUNK, (2 * kv + 1) * LANES:(2 * kv + 2) * LANES]
            lhs = jnp.concatenate([jnp.where(lo_c, qa, zero_c), jnp.where(lo_c, zero_c, qa),
                                   jnp.where(lo_c, qb, zero_c), jnp.where(lo_c, zero_c, qb)], axis=0)
            s_t = lax.dot_general(keys, lhs, (((1,), (1,)), ((), ())), preferred_element_type=F32) * scale2
            s_t = jnp.where(valid_t, s_t, NEG_INF)
            s_t = jnp.where(krow == N_META, sink_ref[kv, 0:1, :] * LOG2E, s_t)
            e_t = jnp.exp2(s_t - jnp.max(s_t, axis=0, keepdims=True)).astype(BF16)
            ov = lax.dot_general(e_t, jnp.concatenate([vals, ones_v], axis=1), (((0,), (0,)), ((), ())),
                                 preferred_element_type=F32)
            o = ov[:, 0:LANES] * (1.0 / ov[:, LANES:2 * LANES])
            oa = jnp.where(lo_c, o[0:CHUNK], o[CHUNK:2 * CHUNK])
            ob = jnp.where(lo_c, o[2 * CHUNK:3 * CHUNK], o[3 * CHUNK:4 * CHUNK])
            base = RET_V + (2 * kv) * LANES
            omix_ref[r0:r0 + CHUNK, base:base + LANES] = oa.astype(BF16)
            omix_ref[r0:r0 + CHUNK, base + LANES:base + 2 * LANES] = ob.astype(BF16)

    zero_t = jnp.zeros((tl, LANES), BF16)
    for p in range(n_pairs):
        q = rqk_ref[:, p * LANES:(p + 1) * LANES]
        k = rqk_ref[:, RET_Q + p * LANES:RET_Q + (p + 1) * LANES]
        lhs = jnp.concatenate([jnp.where(lo_tl, q, zero_t), jnp.where(lo_tl, zero_t, q)], axis=0)
        s = lax.dot_general(lhs, k, (((1,), (1,)), ((), ())), preferred_element_type=F32)
        cross = jnp.dot(lhs, s_scr[p].astype(BF16), preferred_element_type=F32)
        for i in range(2):
            h = 2 * p + i
            v = rv_ref[:, h * LANES:(h + 1) * LANES]
            a = (s[i * tl:(i + 1) * tl] * dec_scr[h]).astype(BF16)
            o = jnp.dot(a, v, preferred_element_type=F32) + cross[i * tl:(i + 1) * tl] * cs_scr[h]
            r = o * lax.rsqrt(_lane_sum(o * o) * (1.0 / RET_DV) + EPS) * rng_ref[h:h + 1, :]
            omix_ref[:, h * LANES:(h + 1) * LANES] = (r * gate_ref[:, h * LANES:(h + 1) * LANES].astype(F32)).astype(BF16)
        u = _pair_update(k, rv_ref[:, (2 * p) * LANES:(2 * p + 1) * LANES],
                         rv_ref[:, (2 * p + 1) * LANES:(2 * p + 2) * LANES], wt_scr[p])
        s_scr[p] = gam_scr[p] * s_scr[p] + u

    if tl >= WINDOW:
        @pl.when(t + 1 < nt)
        def _carry_window():
            for kv in range(SWA_KV_HEADS):
                kd_scr[kv, 0:WINDOW] = kd_scr[kv, tl:tl + WINDOW]
                vd_scr[kv, 0:WINDOW] = vd_scr[kv, tl:tl + WINDOW]

    @pl.when(t + 1 == nt)
    def _stream_end():
        sout_ref[0] = s_scr[...]
        if tl >= WINDOW:
            kvout_ref[0] = skv_ref[tl - WINDOW:tl, :]
        else:
            kvout_ref[0, 0:WINDOW - tl] = hist_ref[0, tl:WINDOW, :]
            kvout_ref[0, WINDOW - tl:WINDOW] = skv_ref[...]


def _attention(rqk, rv, gate, sq, skv, meta_kv, hist_kv, s0, rng, sink_tab, *, n_streams, seq, tl, has_hist):
    nt = seq // tl
    assert tl % CHUNK == 0 and seq % tl == 0
    assert tl >= WINDOW or (nt == 1 and has_hist)
    n_pairs = RET_HEADS // 2
    s0_shared = s0.shape[0] == 1
    row = lambda b, t: (b * nt + t, 0)
    const2 = lambda b, t: (0, 0)
    const3 = lambda b, t: (0, 0, 0)
    per_b3 = lambda b, t: (b, 0, 0)
    s0_map = (lambda b, t: (0, 0, 0, 0)) if s0_shared else (lambda b, t: (b, 0, 0, 0))
    hist_map = per_b3 if has_hist else const3
    rows = n_streams * seq
    return pl.pallas_call(
        functools.partial(_attn_kernel, tl=tl, has_hist=has_hist),
        grid=(n_streams, nt),
        in_specs=[pl.BlockSpec((tl, 2 * RET_Q), row),
                  pl.BlockSpec((tl, RET_V), row),
                  pl.BlockSpec((tl, RET_V), row),
                  pl.BlockSpec((tl, SWA_Q), row),
                  pl.BlockSpec((tl, 2 * SWA_KV), row),
                  pl.BlockSpec((N_META, 2 * SWA_KV), const2),
                  pl.BlockSpec((1, WINDOW, 2 * SWA_KV), hist_map),
                  pl.BlockSpec((1, n_pairs, 2 * RET_DK, RET_DV), s0_map),
                  pl.BlockSpec((RET_HEADS, RET_DV), const2),
                  pl.BlockSpec((SWA_KV_HEADS, SUBLANES, 4 * CHUNK), const3)],
        out_specs=[pl.BlockSpec((tl, MIX_WIDTH), row),
                   pl.BlockSpec((1, n_pairs, 2 * RET_DK, RET_DV), lambda b, t: (b, 0, 0, 0)),
                   pl.BlockSpec((1, WINDOW, 2 * SWA_KV), per_b3)],
        out_shape=[jax.ShapeDtypeStruct((rows, MIX_WIDTH), BF16),
                   jax.ShapeDtypeStruct((n_streams, n_pairs, 2 * RET_DK, RET_DV), F32),
                   jax.ShapeDtypeStruct((n_streams, WINDOW, 2 * SWA_KV), F32)],
        scratch_shapes=[pltpu.VMEM((n_pairs, 2 * RET_DK, RET_DV), F32),
                        pltpu.VMEM((SWA_KV_HEADS, WINDOW + tl, LANES), BF16),
                        pltpu.VMEM((SWA_KV_HEADS, WINDOW + tl, LANES), BF16),
                        pltpu.VMEM((SWA_KV_HEADS, META_ROWS, LANES), BF16),
                        pltpu.VMEM((SWA_KV_HEADS, META_ROWS, LANES), BF16),
                        pltpu.VMEM((RET_HEADS, tl, tl), F32),
                        pltpu.VMEM((n_pairs, tl, LANES), F32),
                        pltpu.VMEM((RET_HEADS, tl, RET_DV), F32),
                        pltpu.VMEM((n_pairs, 2 * RET_DK, RET_DV), F32)],
        compiler_params=_params(2),
        name="attention",
    )(rqk, rv, gate, sq, skv, meta_kv, hist_kv, s0, rng, sink_tab)


def _post_kernel(omix_ref, x_ref, wout_ref, g2_ref, wrt_ref, base_ref,
                 xmid_ref, hn_ref, wcol_ref, rt_ref, cnt_ref, tri_scr, run_scr):
    i = pl.program_id(0)
    tm = x_ref.shape[0]

    @pl.when(i == 0)
    def _init():
        r = lax.broadcasted_iota(jnp.int32, (tm, tm), 0)
        c = lax.broadcasted_iota(jnp.int32, (tm, tm), 1)
        tri_scr[...] = jnp.where(r < c, 1.0, 0.0).astype(BF16)
        run_scr[...] = base_ref[...]

    xm = x_ref[...] + jnp.dot(omix_ref[...], wout_ref[...], preferred_element_type=F32)
    xmid_ref[...] = xm
    hn = _row_rms(xm) * g2_ref[...]
    hn_ref[...] = hn
    lt = lax.dot_general(wrt_ref[...], hn.astype(BF16), (((1,), (1,)), ((), ())), preferred_element_type=F32)
    row8 = lax.broadcasted_iota(jnp.int32, (SUBLANES, tm), 0)
    big = jnp.int32(SUBLANES)
    gl = jnp.where(row8 < N_GROUPS, lt[0:SUBLANES], NEG_INF)
    gmax = jnp.max(gl, axis=0, keepdims=True)
    gsum = jnp.sum(jnp.exp(gl - gmax), axis=0, keepdims=True)
    g_sel = jnp.min(jnp.where(gl == gmax, row8, big), axis=0, keepdims=True)
    p_sel = 1.0 / gsum
    el = lt[ROUTER_EXPERT_ROW0:ROUTER_EXPERT_ROW0 + EXPERTS_PER_GROUP]
    for g in range(1, N_GROUPS):
        lo = ROUTER_EXPERT_ROW0 + g * EXPERTS_PER_GROUP
        el = jnp.where(g_sel == g, lt[lo:lo + EXPERTS_PER_GROUP], el)
    m1 = jnp.max(el, axis=0, keepdims=True)
    i1 = jnp.min(jnp.where(el == m1, row8, big), axis=0, keepdims=True)
    el2 = jnp.where(row8 == i1, NEG_INF, el)
    m2 = jnp.max(el2, axis=0, keepdims=True)
    i2 = jnp.min(jnp.where(el2 == m2, row8, big), axis=0, keepdims=True)
    e2 = jnp.exp(m2 - m1)
    inv = 1.0 / (1.0 + e2)
    w1 = p_sel * inv
    w2 = p_sel * (e2 * inv)
    eid1 = g_sel * EXPERTS_PER_GROUP + i1
    eid2 = g_sel * EXPERTS_PER_GROUP + i2

    rowe = lax.broadcasted_iota(jnp.int32, (N_EXPERTS, tm), 0)
    oh = jnp.where((rowe == eid1) | (rowe == eid2), 1.0, 0.0).astype(BF16)
    run = run_scr[...]
    pref = jnp.dot(oh, tri_scr[...], preferred_element_type=F32) + jnp.concatenate([run] * (tm // LANES), axis=1)
    r1 = jnp.sum(jnp.where(rowe == eid1, pref, 0.0), axis=0, keepdims=True)
    r2 = jnp.sum(jnp.where(rowe == eid2, pref, 0.0), axis=0, keepdims=True)
    run = run + jnp.dot(oh, jnp.ones((tm, LANES), BF16), preferred_element_type=F32)
    run_scr[...] = run
    cnt_ref[...] = run

    out = jnp.zeros((SUBLANES, tm), F32)
    for k, v in enumerate([eid1.astype(F32), eid2.astype(F32), w1, w2, r1, r2]):
        out = jnp.where(row8 == k, v, out)
    rt_ref[...] = out
    rowl = lax.broadcasted_iota(jnp.int32, (LANES, tm), 0)
    wcol_ref[...] = jnp.where(rowl == 0, w1, jnp.where(rowl == 1, w2, 0.0)).T


def _post(omix, x2d, w_out_bf, g2, w_router_t_bf, base_cnt, tm):
    t_rows = x2d.shape[0]
    assert tm % LANES == 0
    row = lambda i: (i, 0)
    const = lambda i: (0, 0)
    return pl.pallas_call(
        _post_kernel,
        grid=(t_rows // tm,),
        in_specs=[pl.BlockSpec((tm, MIX_WIDTH), row),
                  pl.BlockSpec((tm, D_MODEL), row),
                  pl.BlockSpec((MIX_WIDTH, D_MODEL), const),
                  pl.BlockSpec((1, D_MODEL), const),
                  pl.BlockSpec((ROUTER_ROWS, D_MODEL), const),
                  pl.BlockSpec((N_EXPERTS, LANES), const)],
        out_specs=[pl.BlockSpec((tm, D_MODEL), row),
                   pl.BlockSpec((tm, D_MODEL), row),
                   pl.BlockSpec((tm, LANES), row),
                   pl.BlockSpec((SUBLANES, tm), lambda i: (0, i)),
                   pl.BlockSpec((N_EXPERTS, LANES), const)],
        out_shape=[jax.ShapeDtypeStruct((t_rows, D_MODEL), F32),
                   jax.ShapeDtypeStruct((t_rows, D_MODEL), F32),
                   jax.ShapeDtypeStruct((t_rows, LANES), F32),
                   jax.ShapeDtypeStruct((SUBLANES, t_rows), F32),
                   jax.ShapeDtypeStruct((N_EXPERTS, LANES), F32)],
        scratch_shapes=[pltpu.VMEM((tm, tm), BF16), pltpu.VMEM((N_EXPERTS, LANES), F32)],
        compiler_params=_params(1),
        name="post",
    )(omix, x2d, w_out_bf, g2, w_router_t_bf, base_cnt)


def _step_major(pos, tm):
    return [pos[e].reshape(-1, 1, tm) for e in range(2)]


def _row_copy(src_ref, src_row, dst_ref, dst_row, sem):
    return pltpu.make_async_copy(src_ref.at[pl.ds(src_row, 1)], dst_ref.at[pl.ds(dst_row, 1)], sem)


def _dispatch_kernel(pos0_ref, pos1_ref, pad_ref, *refs, tm, n_pad, group_steps):
    pos_refs = (pos0_ref, pos1_ref)
    hn_refs = refs[:len(group_steps)]
    xs_ref, zero_scr, sem = refs[len(group_steps):]
    i = pl.program_id(0)

    @pl.when(i == 0)
    def _init():
        zero_scr[...] = jnp.zeros(zero_scr.shape, F32)

    first = 0
    for hn_ref, steps in zip(hn_refs, group_steps):
        @pl.when((i >= first) & (i < first + steps))
        def _rows(hn_ref=hn_ref):
            def body(blk, carry):
                for k in range(SUBLANES):
                    r = blk * SUBLANES + k
                    for e in range(2):
                        pltpu.make_async_copy(hn_ref.at[blk, pl.ds(k, 1)], xs_ref.at[pl.ds(pos_refs[e][0, 0, r], 1)],
                                              sem).start(priority=e)
                return carry

            lax.fori_loop(0, tm // SUBLANES, body, 0)
        first += steps

    for j in range(n_pad):
        _row_copy(zero_scr, 0, xs_ref, pad_ref[0, 0, j], sem).start(priority=j % 2)
    n_rows = 2 * tm + n_pad
    pltpu.make_async_copy(xs_ref.at[pl.ds(0, n_rows)], xs_ref.at[pl.ds(0, n_rows)], sem).wait()


def _dispatch(pos_list, pad, hn_list, n_rows_out, tm):
    group_steps = tuple(hn.shape[0] // tm for hn in hn_list)
    n_steps = sum(group_steps)
    n_pad = pad.shape[0] // n_steps
    assert pad.shape[0] == n_steps * n_pad and tm % SUBLANES == 0
    pos3 = [jnp.concatenate(parts, axis=0) for parts in zip(*[_step_major(p, tm) for p in pos_list])]
    pad3 = pad.reshape(n_steps, 1, n_pad)
    smem = lambda n: pl.BlockSpec((1, 1, n), lambda i: (i, 0, 0), memory_space=pltpu.SMEM)
    hn_specs = []
    first = 0
    for steps in group_steps:
        hn_specs.append(pl.BlockSpec((tm // SUBLANES, SUBLANES, D_MODEL),
                                     lambda i, first=first, steps=steps: (jnp.clip(i - first, 0, steps - 1), 0, 0)))
        first += steps
    hn_list = [hn.reshape(hn.shape[0] // SUBLANES, SUBLANES, D_MODEL) for hn in hn_list]
    return pl.pallas_call(
        functools.partial(_dispatch_kernel, tm=tm, n_pad=n_pad, group_steps=group_steps),
        grid=(n_steps,),
        in_specs=[smem(tm), smem(tm), smem(n_pad)] + hn_specs,
        out_specs=pl.BlockSpec(memory_space=pl.ANY),
        out_shape=jax.ShapeDtypeStruct((n_rows_out, D_MODEL), F32),
        scratch_shapes=[pltpu.VMEM((8, D_MODEL), F32), pltpu.SemaphoreType.DMA(())],
        compiler_params=_params(1),
        name="dispatch",
    )(*pos3, pad3, *hn_list)


def _expert_kernel(te_ref, nv_ref, x_ref, wgu_ref, wd_ref, y_ref):
    i = pl.program_id(0)

    @pl.when(i < nv_ref[0])
    def _compute():
        x = x_ref[...].astype(BF16)
        gu = jnp.dot(x, wgu_ref[0], preferred_element_type=F32)
        g = gu[:, 0:EXPERT_FF]
        a = (g * jax.nn.sigmoid(g) * gu[:, EXPERT_FF:2 * EXPERT_FF]).astype(BF16)
        y_ref[...] = jnp.dot(a, wd_ref[0], preferred_element_type=F32)

    @pl.when(i >= nv_ref[0])
    def _skip():
        y_ref[...] = jnp.zeros(y_ref.shape, F32)


def _experts(tile_expert, n_valid, xs, wgu_bf, wd_bf, n_tiles, tm):
    last = lambda i, nv: jnp.minimum(i, nv[0] - 1)
    return pl.pallas_call(
        _expert_kernel,
        grid_spec=pltpu.PrefetchScalarGridSpec(
            num_scalar_prefetch=2,
            grid=(n_tiles,),
            in_specs=[pl.BlockSpec((tm, D_MODEL), lambda i, te, nv: (last(i, nv), 0)),
                      pl.BlockSpec((1, D_MODEL, 2 * EXPERT_FF), lambda i, te, nv: (te[last(i, nv)], 0, 0)),
                      pl.BlockSpec((1, EXPERT_FF, D_MODEL), lambda i, te, nv: (te[last(i, nv)], 0, 0))],
            out_specs=pl.BlockSpec((tm, D_MODEL), lambda i, te, nv: (i, 0))),
        out_shape=jax.ShapeDtypeStruct((n_tiles * tm, D_MODEL), F32),
        compiler_params=_params(1),
        name="experts",
    )(tile_expert, n_valid, xs, wgu_bf, wd_bf)


def _combine_kernel(pos0_ref, pos1_ref, nxt0_ref, nxt1_ref, ys_ref, xmid_ref, wcol_ref, out_ref, ybuf, sems,
                    *, tm, n_steps):
    i = pl.program_id(0)
    slot = i % 2

    def issue(p_refs, to_slot):
        def body(blk, carry):
            for k in range(SUBLANES):
                r = blk * SUBLANES + k
                for e in range(2):
                    pltpu.make_async_copy(ys_ref.at[pl.ds(p_refs[e][0, 0, r], 1)],
                                          ybuf.at[to_slot, e, blk, pl.ds(k, 1)], sems.at[to_slot]).start(priority=e)
            return carry

        lax.fori_loop(0, tm // SUBLANES, body, 0)

    @pl.when(i == 0)
    def _first():
        issue((pos0_ref, pos1_ref), 0)

    @pl.when(i + 1 < n_steps)
    def _ahead():
        issue((nxt0_ref, nxt1_ref), 1 - slot)

    for e in range(2):
        pltpu.make_async_copy(ybuf.at[slot, e], ybuf.at[slot, e], sems.at[slot]).wait()
    w = wcol_ref[...]
    y0 = ybuf[slot, 0].reshape(tm, D_MODEL)
    y1 = ybuf[slot, 1].reshape(tm, D_MODEL)
    out_ref[...] = xmid_ref[...] + w[:, 0:1] * y0 + w[:, 1:2] * y1


def _combine(pos, ys, xmid, wcol, tm):
    t_rows = xmid.shape[0]
    n_steps = t_rows // tm
    assert tm % SUBLANES == 0
    pos3 = _step_major(pos, tm)
    row = lambda i: (i, 0)
    cur = pl.BlockSpec((1, 1, tm), lambda i: (i, 0, 0), memory_space=pltpu.SMEM)
    nxt = pl.BlockSpec((1, 1, tm), lambda i: (jnp.minimum(i + 1, n_steps - 1), 0, 0), memory_space=pltpu.SMEM)
    return pl.pallas_call(
        functools.partial(_combine_kernel, tm=tm, n_steps=n_steps),
        grid=(n_steps,),
        in_specs=[cur, cur, nxt, nxt,
                  pl.BlockSpec(memory_space=pl.ANY),
                  pl.BlockSpec((tm, D_MODEL), row),
                  pl.BlockSpec((tm, LANES), row)],
        out_specs=pl.BlockSpec((tm, D_MODEL), row),
        out_shape=jax.ShapeDtypeStruct((t_rows, D_MODEL), F32),
        scratch_shapes=[pltpu.VMEM((2, 2, tm // SUBLANES, SUBLANES, D_MODEL), F32), pltpu.SemaphoreType.DMA((2,))],
        compiler_params=_params(1),
        name="combine",
    )(*pos3, *pos3, ys, xmid, wcol)


def _bucket(ends, idx):
    n = jnp.sum((ends[None, :] <= idx[:, None]).astype(jnp.int32), axis=1)
    return jnp.minimum(n, ends.shape[0] - 1)


def _tile_for(rows, pref):
    tm = min(pref, rows)
    assert rows % tm == 0
    return tm


def kernel(x_prompt, x_sample, cache_ret_state, cache_swa_k, cache_swa_v, meta_tokens, norm1_g, w_in, q_norm_g,
           k_norm_g, ret_norm_g, attn_sinks, w_out, norm2_g, w_group, w_expert, w_gate, w_up, w_down):
    assert norm1_g.shape[0] == 1, "single-layer trunk"
    bp, lp, _ = x_prompt.shape
    bs, ls, _ = x_sample.shape
    n_pairs = RET_HEADS // 2

    g1 = norm1_g[0][None, :]
    g2 = norm2_g[0][None, :]
    w_in_bf = w_in[0].astype(BF16)
    w_out_bf = w_out[0].astype(BF16)
    qg2 = jnp.tile(q_norm_g[0], 2)[None, :]
    kg2 = jnp.tile(k_norm_g[0], 2)[None, :]
    rng = ret_norm_g[0].reshape(RET_HEADS, RET_DV)
    sink_tab = jnp.broadcast_to(jnp.repeat(attn_sinks[0], CHUNK).reshape(SWA_KV_HEADS, 1, 4 * CHUNK),
                                (SWA_KV_HEADS, SUBLANES, 4 * CHUNK))
    w_router_t = jnp.zeros((ROUTER_ROWS, D_MODEL), F32)
    w_router_t = w_router_t.at[0:N_GROUPS].set(w_group[0].T)
    w_router_t = w_router_t.at[ROUTER_EXPERT_ROW0:ROUTER_EXPERT_ROW0 + N_EXPERTS].set(w_expert[0].T)
    w_router_t_bf = w_router_t.astype(BF16)
    wgu_bf = jnp.concatenate([w_gate[0], w_up[0]], axis=-1).astype(BF16)
    wd_bf = w_down[0].astype(BF16)

    meta_rows = 2 * CHUNK
    m_pad = jnp.zeros((meta_rows, D_MODEL), F32).at[0:N_META].set(meta_tokens)
    m_rqk, m_rv, _, _, m_skv = _proj(m_pad, jnp.arange(meta_rows, dtype=jnp.int32), meta_rows, g1, w_in_bf, qg2, kg2)
    s_meta = _meta_state(m_rqk, m_rv)[None]
    meta_kv = m_skv[0:N_META]

    groups = [
        dict(x=x_prompt.reshape(bp * lp, D_MODEL), n=bp, seq=lp, pos0=N_META, has_hist=False, s0=s_meta,
             hist=jnp.zeros((1, WINDOW, 2 * SWA_KV), F32)),
        dict(x=x_sample.reshape(bs * ls, D_MODEL), n=bs, seq=ls, pos0=N_META + PAST_LEN, has_hist=True,
             s0=cache_ret_state[0].reshape(bs, n_pairs, 2 * RET_DK, RET_DV),
             hist=jnp.concatenate([cache_swa_k[0].reshape(bs, WINDOW, SWA_KV),
                                   cache_swa_v[0].reshape(bs, WINDOW, SWA_KV)], axis=-1)),
    ]

    base_cnt = jnp.zeros((N_EXPERTS, LANES), F32)
    for g in groups:
        rows = g["n"] * g["seq"]
        tm = _tile_for(rows, TOKEN_TILE)
        pos = g["pos0"] + jnp.arange(g["seq"], dtype=jnp.int32)
        if g["seq"] < tm:
            assert tm % g["seq"] == 0
            pos = jnp.tile(pos, tm // g["seq"])
        else:
            assert g["seq"] % tm == 0
        rqk, rv, gate, sq, skv = _proj(g["x"], pos, tm, g1, w_in_bf, qg2, kg2)
        tl = min(ATTN_TILE, g["seq"])
        omix, s_out, kv_out = _attention(rqk, rv, gate, sq, skv, meta_kv, g["hist"], g["s0"], rng, sink_tab,
                                         n_streams=g["n"], seq=g["seq"], tl=tl, has_hist=g["has_hist"])
        xmid, hn, wcol, route_t, base_cnt = _post(omix, g["x"], w_out_bf, g2, w_router_t_bf, base_cnt, tm)
        g.update(xmid=xmid, hn=hn, wcol=wcol, route_t=route_t, s_out=s_out, kv_out=kv_out)

    te = EXPERT_TILE
    total_rows = sum(g["n"] * g["seq"] for g in groups)
    n_tiles = (2 * total_rows) // te + N_EXPERTS
    n_slots = n_tiles * te
    counts = base_cnt[:, 0].astype(jnp.int32)
    tiles_e = (counts + te - 1) // te
    padded = tiles_e * te
    off = jnp.cumsum(padded) - padded
    tile_end = jnp.cumsum(tiles_e)
    n_valid = tile_end[-1:].astype(jnp.int32)
    tile_expert = _bucket(tile_end, jnp.arange(n_tiles, dtype=jnp.int32))
    pad_e = padded - counts
    pad_end = jnp.cumsum(pad_e)
    n_fill = n_slots - 2 * total_rows
    move_steps = sum((g["n"] * g["seq"]) // _tile_for(g["n"] * g["seq"], MOVE_TILE) for g in groups)
    fill_per_step = 8 * (-(-n_fill // (8 * move_steps)))
    j = jnp.arange(fill_per_step * move_steps, dtype=jnp.int32)
    e_of = _bucket(pad_end, j)
    in_expert = (off + counts)[e_of] + (j - (pad_end - pad_e)[e_of])
    in_tail = (off[-1] + padded[-1]) + (j - pad_end[-1])
    fill_slots = jnp.where(j < pad_end[-1], in_expert, jnp.where(j < n_fill, in_tail, n_slots + (j - n_fill)))
    fill_slots = fill_slots.astype(jnp.int32)
    n_rows_xs = n_slots + fill_per_step * move_steps - n_fill

    for g in groups:
        eid = g["route_t"][0:2].astype(jnp.int32)
        off_sel = jnp.sum(jnp.where(eid[None] == jnp.arange(N_EXPERTS, dtype=jnp.int32)[:, None, None],
                                    off[:, None, None], 0), axis=0)
        g["pos"] = (off_sel + g["route_t"][4:6].astype(jnp.int32)).astype(jnp.int32)
    xs = _dispatch([g["pos"] for g in groups], fill_slots, [g["hn"] for g in groups], n_rows_xs, MOVE_TILE)

    ys = _experts(tile_expert, n_valid, xs, wgu_bf, wd_bf, n_tiles, te)

    outs = []
    for g in groups:
        rows = g["n"] * g["seq"]
        y = _combine(g["pos"], ys, g["xmid"], g["wcol"], _tile_for(rows, MOVE_TILE))
        outs.append(y.reshape(g["n"], g["seq"], D_MODEL))

    def caches(g):
        kv = g["kv_out"]
        k = kv[:, :, 0:SWA_KV].reshape(g["n"], WINDOW, SWA_KV_HEADS, SWA_HD)[None]
        v = kv[:, :, SWA_KV:2 * SWA_KV].reshape(g["n"], WINDOW, SWA_KV_HEADS, SWA_HD)[None]
        s = g["s_out"].reshape(g["n"], RET_HEADS, RET_DK, RET_DV)[None]
        return s, k, v

    sp, kp, vp = caches(groups[0])
    ss, ks, vs = caches(groups[1])
    return (outs[0], outs[1], sp, kp, vp, ss, ks, vs)
```

```python
import functools

import numpy as np
import jax
import jax.numpy as jnp
from jax import lax
from jax.experimental import pallas as pl
from jax.experimental.pallas import tpu as pltpu

F32 = jnp.float32
BF16 = jnp.bfloat16

D_MODEL = 1024
PAST_LEN = 4096
CHUNK = 64
N_META = 16
RET_HEADS = 4
RET_DK = 64
RET_DV = 128
SWA_HEADS = 8
SWA_KV_HEADS = 2
SWA_HD = 64
WINDOW = 128
ROPE_THETA = 10000.0
N_GROUPS = 4
EXPERTS_PER_GROUP = 8
N_EXPERTS = N_GROUPS * EXPERTS_PER_GROUP
EXPERT_FF = 256
EPS = 1e-6
NEG_INF = -1e30
LOG2E = float(np.log2(np.e))
RET_Q = RET_HEADS * RET_DK
RET_V = RET_HEADS * RET_DV
SWA_Q = SWA_HEADS * SWA_HD
SWA_KV = SWA_KV_HEADS * SWA_HD
MIX_WIDTH = RET_V + SWA_Q
IN_WIDTH = 2 * RET_Q + 2 * RET_V + SWA_Q + 2 * SWA_KV

LANES = 128
TOKEN_TILE = 512
ATTN_TILE = 256
EXPERT_TILE = 256
MOVE_TILE = 256
SUBLANES = 8
ROUTER_EXPERT_ROW0 = 8
ROUTER_ROWS = 64
META_ROWS = 64
VMEM_LIMIT = 56 * 1024 * 1024

_LOG_G = [float(np.log1p(-np.exp2(-5.0 - h))) for h in range(RET_HEADS)]


def _params(n_axes):
    return pltpu.CompilerParams(dimension_semantics=("arbitrary",) * n_axes, vmem_limit_bytes=VMEM_LIMIT)


def _split_bf16(a):
    hi = a.astype(BF16)
    return hi, (a - hi.astype(F32)).astype(BF16)


def _split_dot(a, w2):
    hi, lo = _split_bf16(a)
    return jnp.dot(jnp.concatenate([hi, lo], axis=1), w2, preferred_element_type=F32)


def _lane_sum(a):
    return _split_dot(a, jnp.ones((2 * LANES, LANES), BF16))


def _head_sum_matrix():
    i = lax.broadcasted_iota(jnp.int32, (2 * LANES, LANES), 0) % LANES
    j = lax.broadcasted_iota(jnp.int32, (2 * LANES, LANES), 1)
    return jnp.where((i < SWA_HD) == (j < SWA_HD), 1.0, 0.0).astype(BF16)


def _rope(t, c, s1, s2):
    half = SWA_HD // 2
    return t * c + pltpu.roll(t, LANES - half, 1) * s1 + pltpu.roll(t, half, 1) * s2


def _head_rms(t, g, head_w):
    ms = _split_dot(t * t, head_w) * (1.0 / SWA_HD)
    return t * lax.rsqrt(ms + EPS) * g


def _row_rms(x):
    n_tiles = x.shape[1] // LANES
    ss = x[:, 0:LANES] * x[:, 0:LANES]
    for j in range(1, n_tiles):
        ss = ss + x[:, j * LANES:(j + 1) * LANES] * x[:, j * LANES:(j + 1) * LANES]
    r = lax.rsqrt(_lane_sum(ss) * (1.0 / x.shape[1]) + EPS)
    return x * jnp.concatenate([r] * n_tiles, axis=1)


def _proj_kernel(x_ref, g1_ref, w_ref, qg_ref, kg_ref, cos_ref, s1_ref, s2_ref,
                 rqk_ref, rv_ref, gate_ref, sq_ref, skv_ref):
    xn = (_row_rms(x_ref[...]) * g1_ref[...]).astype(BF16)
    c, s1, s2 = cos_ref[...], s1_ref[...], s2_ref[...]
    head_w = _head_sum_matrix()

    def seg(a, b):
        return jnp.dot(xn, w_ref[:, a:b], preferred_element_type=F32)

    def tile(h, j):
        return h[:, j * LANES:(j + 1) * LANES]

    h = seg(0, 2 * RET_Q)
    for j in range(2):
        rqk_ref[:, j * LANES:(j + 1) * LANES] = _rope(tile(h, j), c, s1, s2).astype(BF16)
    for j in range(2, 4):
        rqk_ref[:, j * LANES:(j + 1) * LANES] = (_rope(tile(h, j), c, s1, s2) * (RET_DK ** -0.5)).astype(BF16)
    a = 2 * RET_Q
    rv_ref[...] = seg(a, a + RET_V).astype(BF16)
    a += RET_V
    g = seg(a, a + RET_V)
    gate_ref[...] = (g * jax.nn.sigmoid(g)).astype(BF16)
    a += RET_V
    h = seg(a, a + SWA_Q)
    qg = qg_ref[...]
    for j in range(SWA_Q // LANES):
        sq_ref[:, j * LANES:(j + 1) * LANES] = _rope(_head_rms(tile(h, j), qg, head_w), c, s1, s2).astype(BF16)
    a += SWA_Q
    h = seg(a, a + 2 * SWA_KV)
    skv_ref[:, 0:LANES] = _rope(_head_rms(tile(h, 0), kg_ref[...], head_w), c, s1, s2)
    skv_ref[:, LANES:2 * LANES] = tile(h, 1)


def _rope_tables(pos):
    half = SWA_HD // 2
    inv = ROPE_THETA ** (-jnp.arange(half, dtype=F32) / half)
    ang = pos.astype(F32)[:, None] * inv[None, :]
    cos, sin = jnp.cos(ang), jnp.sin(ang)
    z = jnp.zeros_like(sin)
    return (jnp.tile(cos, (1, 4)),
            jnp.tile(jnp.concatenate([-sin, z], axis=1), (1, 2)),
            jnp.tile(jnp.concatenate([z, sin], axis=1), (1, 2)))


def _proj(x2d, pos_rows, tm, g1, w_in_bf, qg2, kg2):
    t_rows = x2d.shape[0]
    n_tiles = t_rows // tm
    n_pos_tiles = pos_rows.shape[0] // tm
    cos, s1, s2 = _rope_tables(pos_rows)
    row = lambda i: (i, 0)
    const = lambda i: (0, 0)
    tab = lambda i: (i % n_pos_tiles, 0)
    return pl.pallas_call(
        _proj_kernel,
        grid=(n_tiles,),
        in_specs=[pl.BlockSpec((tm, D_MODEL), row),
                  pl.BlockSpec((1, D_MODEL), const),
                  pl.BlockSpec((D_MODEL, IN_WIDTH), const),
                  pl.BlockSpec((1, LANES), const),
                  pl.BlockSpec((1, LANES), const),
                  pl.BlockSpec((tm, LANES), tab),
                  pl.BlockSpec((tm, LANES), tab),
                  pl.BlockSpec((tm, LANES), tab)],
        out_specs=[pl.BlockSpec((tm, 2 * RET_Q), row),
                   pl.BlockSpec((tm, RET_V), row),
                   pl.BlockSpec((tm, RET_V), row),
                   pl.BlockSpec((tm, SWA_Q), row),
                   pl.BlockSpec((tm, 2 * SWA_KV), row)],
        out_shape=[jax.ShapeDtypeStruct((t_rows, 2 * RET_Q), BF16),
                   jax.ShapeDtypeStruct((t_rows, RET_V), BF16),
                   jax.ShapeDtypeStruct((t_rows, RET_V), BF16),
                   jax.ShapeDtypeStruct((t_rows, SWA_Q), BF16),
                   jax.ShapeDtypeStruct((t_rows, 2 * SWA_KV), F32)],
        compiler_params=_params(1),
        name="proj",
    )(x2d, g1, w_in_bf, qg2, kg2, cos, s1, s2)


def _pair_update(k_bf, v0_bf, v1_bf, wt):
    kw = (k_bf.astype(F32) * wt).astype(BF16)
    dn = (((0,), (0,)), ((), ()))
    a0 = lax.dot_general(kw, v0_bf, dn, preferred_element_type=F32)
    a1 = lax.dot_general(kw, v1_bf, dn, preferred_element_type=F32)
    top = lax.broadcasted_iota(jnp.int32, a0.shape, 0) < RET_DK
    return jnp.where(top, a0, a1)


def _decay_rows(n, pair, rows_back_from):
    i = lax.broadcasted_iota(jnp.int32, (n, LANES), 0).astype(F32)
    lane = lax.broadcasted_iota(jnp.int32, (n, LANES), 1)
    lg = jnp.where(lane < RET_DK, _LOG_G[2 * pair], _LOG_G[2 * pair + 1])
    return jnp.exp((rows_back_from - i) * lg)


def _meta_state_kernel(rqk_ref, rv_ref, s_ref, *, n_rows):
    for p in range(RET_HEADS // 2):
        k = rqk_ref[:, RET_Q + p * LANES:RET_Q + (p + 1) * LANES]
        wt = _decay_rows(n_rows, p, float(N_META - 1))
        s_ref[p] = _pair_update(k, rv_ref[:, (2 * p) * LANES:(2 * p + 1) * LANES],
                                rv_ref[:, (2 * p + 1) * LANES:(2 * p + 2) * LANES], wt)


def _meta_state(m_rqk, m_rv):
    n_rows = m_rqk.shape[0]
    return pl.pallas_call(
        functools.partial(_meta_state_kernel, n_rows=n_rows),
        out_shape=jax.ShapeDtypeStruct((RET_HEADS // 2, 2 * RET_DK, RET_DV), F32),
        name="meta_state",
    )(m_rqk, m_rv)


def _dup_halves(a, lo_mask):
    sw = pltpu.roll(a, SWA_HD, 1)
    return jnp.where(lo_mask, a, sw), jnp.where(lo_mask, sw, a)


def _attn_kernel(rqk_ref, rv_ref, gate_ref, sq_ref, skv_ref, meta_ref, hist_ref, s0_ref, rng_ref, sink_ref,
                 omix_ref, sout_ref, kvout_ref,
                 s_scr, kd_scr, vd_scr, mk_scr, mv_scr, dec_scr, wt_scr, cs_scr, gam_scr,
                 *, tl, has_hist):
    b = pl.program_id(0)
    t = pl.program_id(1)
    nt = pl.num_programs(1)
    n_chunks = tl // CHUNK
    n_pairs = RET_HEADS // 2
    lo_tl = lax.broadcasted_iota(jnp.int32, (tl, LANES), 1) < SWA_HD
    lo_c = lax.broadcasted_iota(jnp.int32, (CHUNK, LANES), 1) < SWA_HD

    @pl.when((b == 0) & (t == 0))
    def _tables():
        i = lax.broadcasted_iota(jnp.int32, (tl, tl), 0)
        j = lax.broadcasted_iota(jnp.int32, (tl, tl), 1)
        diff = (i - j).astype(F32)
        row = lax.broadcasted_iota(jnp.int32, (tl, LANES), 0).astype(F32)
        for h in range(RET_HEADS):
            dec_scr[h] = jnp.where(diff >= 0.0, jnp.exp(jnp.maximum(diff, 0.0) * _LOG_G[h]), 0.0)
            cs_scr[h] = jnp.exp((row + 1.0) * _LOG_G[h])
        top = lax.broadcasted_iota(jnp.int32, (2 * RET_DK, RET_DV), 0) < RET_DK
        for p in range(n_pairs):
            wt_scr[p] = _decay_rows(tl, p, float(tl - 1))
            gam_scr[p] = jnp.where(top, jnp.exp(jnp.float32(tl * _LOG_G[2 * p])), jnp.exp(jnp.float32(tl * _LOG_G[2 * p + 1])))
        lo_m = lax.broadcasted_iota(jnp.int32, (N_META, LANES), 1) < SWA_HD
        mk0, mk1 = _dup_halves(meta_ref[:, 0:LANES], lo_m)
        mv0, mv1 = _dup_halves(meta_ref[:, LANES:2 * LANES], lo_m)
        mk_scr[...] = jnp.zeros(mk_scr.shape, BF16)
        mv_scr[...] = jnp.zeros(mv_scr.shape, BF16)
        mk_scr[0, 0:N_META] = mk0.astype(BF16)
        mk_scr[1, 0:N_META] = mk1.astype(BF16)
        mv_scr[0, 0:N_META] = mv0.astype(BF16)
        mv_scr[1, 0:N_META] = mv1.astype(BF16)

    @pl.when(t == 0)
    def _stream_start():
        s_scr[...] = s0_ref[0]
        if has_hist:
            lo_w = lax.broadcasted_iota(jnp.int32, (WINDOW, LANES), 1) < SWA_HD
            k0, k1 = _dup_halves(hist_ref[0, :, 0:LANES], lo_w)
            v0, v1 = _dup_halves(hist_ref[0, :, LANES:2 * LANES], lo_w)
            kd_scr[0, 0:WINDOW] = k0.astype(BF16)
            kd_scr[1, 0:WINDOW] = k1.astype(BF16)
            vd_scr[0, 0:WINDOW] = v0.astype(BF16)
            vd_scr[1, 0:WINDOW] = v1.astype(BF16)
        else:
            z = jnp.zeros((WINDOW, LANES), BF16)
            for kv in range(SWA_KV_HEADS):
                kd_scr[kv, 0:WINDOW] = z
                vd_scr[kv, 0:WINDOW] = z

    k0, k1 = _dup_halves(skv_ref[:, 0:LANES], lo_tl)
    v0, v1 = _dup_halves(skv_ref[:, LANES:2 * LANES], lo_tl)
    kd_scr[0, WINDOW:WINDOW + tl] = k0.astype(BF16)
    kd_scr[1, WINDOW:WINDOW + tl] = k1.astype(BF16)
    vd_scr[0, WINDOW:WINDOW + tl] = v0.astype(BF16)
    vd_scr[1, WINDOW:WINDOW + tl] = v1.astype(BF16)

    band = WINDOW + CHUNK
    n_keys = META_ROWS + band
    n_q = 4 * CHUNK
    scale2 = (SWA_HD ** -0.5) * LOG2E
    krow = lax.broadcasted_iota(jnp.int32, (n_keys, n_q), 0)
    zero_c = jnp.zeros((CHUNK, LANES), BF16)
    ones_v = jnp.ones((n_keys, LANES), BF16)
    for c in range(n_chunks):
        if has_hist:
            first_valid = META_ROWS
        else:
            first_valid = jnp.where(t == 0, max(META_ROWS + WINDOW - c * CHUNK, META_ROWS), META_ROWS)
        valid_t = (krow < N_META) | (krow >= first_valid)
        r0 = c * CHUNK
        for kv in range(SWA_KV_HEADS):
            keys = jnp.concatenate([mk_scr[kv], kd_scr[kv, r0:r0 + band]], axis=0)
            vals = jnp.concatenate([mv_scr[kv], vd_scr[kv, r0:r0 + band]], axis=0)
            qa = sq_ref[r0:r0 + CHUNK, (2 * kv) * LANES:(2 * kv + 1) * LANES]
            qb = sq_ref[r0:r0 + CHUNK, (2 * kv + 1) * LANES:(2 * kv + 2) * LANES]
            lhs = jnp.concatenate([jnp.where(lo_c, qa, zero_c), jnp.where(lo_c, zero_c, qa),
                                   jnp.where(lo_c, qb, zero_c), jnp.where(lo_c, zero_c, qb)], axis=0)
            s_t = lax.dot_general(keys, lhs, (((1,), (1,)), ((), ())), preferred_element_type=F32) * scale2
            s_t = jnp.where(valid_t, s_t, NEG_INF)
            s_t = jnp.where(krow == N_META, sink_ref[kv, 0:1, :] * LOG2E, s_t)
            e_t = jnp.exp2(s_t - jnp.max(s_t, axis=0, keepdims=True)).astype(BF16)
            ov = lax.dot_general(e_t, jnp.concatenate([vals, ones_v], axis=1), (((0,), (0,)), ((), ())),
                                 preferred_element_type=F32)
            o = ov[:, 0:LANES] * (1.0 / ov[:, LANES:2 * LANES])
            oa = jnp.where(lo_c, o[0:CHUNK], o[CHUNK:2 * CHUNK])
            ob = jnp.where(lo_c, o[2 * CHUNK:3 * CHUNK], o[3 * CHUNK:4 * CHUNK])
            base = RET_V + (2 * kv) * LANES
            omix_ref[r0:r0 + CHUNK, base:base + LANES] = oa.astype(BF16)
            omix_ref[r0:r0 + CHUNK, base + LANES:base + 2 * LANES] = ob.astype(BF16)

    zero_t = jnp.zeros((tl, LANES), BF16)
    for p in range(n_pairs):
        q = rqk_ref[:, p * LANES:(p + 1) * LANES]
        k = rqk_ref[:, RET_Q + p * LANES:RET_Q + (p + 1) * LANES]
        lhs = jnp.concatenate([jnp.where(lo_tl, q, zero_t), jnp.where(lo_tl, zero_t, q)], axis=0)
        s = lax.dot_general(lhs, k, (((1,), (1,)), ((), ())), preferred_element_type=F32)
        cross = jnp.dot(lhs, s_scr[p].astype(BF16), preferred_element_type=F32)
        for i in range(2):
            h = 2 * p + i
            v = rv_ref[:, h * LANES:(h + 1) * LANES]
            a = (s[i * tl:(i + 1) * tl] * dec_scr[h]).astype(BF16)
            o = jnp.dot(a, v, preferred_element_type=F32) + cross[i * tl:(i + 1) * tl] * cs_scr[h]
            r = o * lax.rsqrt(_lane_sum(o * o) * (1.0 / RET_DV) + EPS) * rng_ref[h:h + 1, :]
            omix_ref[:, h * LANES:(h + 1) * LANES] = (r * gate_ref[:, h * LANES:(h + 1) * LANES].astype(F32)).astype(BF16)
        u = _pair_update(k, rv_ref[:, (2 * p) * LANES:(2 * p + 1) * LANES],
                         rv_ref[:, (2 * p + 1) * LANES:(2 * p + 2) * LANES], wt_scr[p])
        s_scr[p] = gam_scr[p] * s_scr[p] + u

    if tl >= WINDOW:
        @pl.when(t + 1 < nt)
        def _carry_window():
            for kv in range(SWA_KV_HEADS):
                kd_scr[kv, 0:WINDOW] = kd_scr[kv, tl:tl + WINDOW]
                vd_scr[kv, 0:WINDOW] = vd_scr[kv, tl:tl + WINDOW]

    @pl.when(t + 1 == nt)
    def _stream_end():
        sout_ref[0] = s_scr[...]
        if tl >= WINDOW:
            kvout_ref[0] = skv_ref[tl - WINDOW:tl, :]
        else:
            kvout_ref[0, 0:WINDOW - tl] = hist_ref[0, tl:WINDOW, :]
            kvout_ref[0, WINDOW - tl:WINDOW] = skv_ref[...]


def _attention(rqk, rv, gate, sq, skv, meta_kv, hist_kv, s0, rng, sink_tab, *, n_streams, seq, tl, has_hist):
    nt = seq // tl
    assert tl % CHUNK == 0 and seq % tl == 0
    assert tl >= WINDOW or (nt == 1 and has_hist)
    n_pairs = RET_HEADS // 2
    s0_shared = s0.shape[0] == 1
    row = lambda b, t: (b * nt + t, 0)
    const2 = lambda b, t: (0, 0)
    const3 = lambda b, t: (0, 0, 0)
    per_b3 = lambda b, t: (b, 0, 0)
    s0_map = (lambda b, t: (0, 0, 0, 0)) if s0_shared else (lambda b, t: (b, 0, 0, 0))
    hist_map = per_b3 if has_hist else const3
    rows = n_streams * seq
    return pl.pallas_call(
        functools.partial(_attn_kernel, tl=tl, has_hist=has_hist),
        grid=(n_streams, nt),
        in_specs=[pl.BlockSpec((tl, 2 * RET_Q), row),
                  pl.BlockSpec((tl, RET_V), row),
                  pl.BlockSpec((tl, RET_V), row),
                  pl.BlockSpec((tl, SWA_Q), row),
                  pl.BlockSpec((tl, 2 * SWA_KV), row),
                  pl.BlockSpec((N_META, 2 * SWA_KV), const2),
                  pl.BlockSpec((1, WINDOW, 2 * SWA_KV), hist_map),
                  pl.BlockSpec((1, n_pairs, 2 * RET_DK, RET_DV), s0_map),
                  pl.BlockSpec((RET_HEADS, RET_DV), const2),
                  pl.BlockSpec((SWA_KV_HEADS, SUBLANES, 4 * CHUNK), const3)],
        out_specs=[pl.BlockSpec((tl, MIX_WIDTH), row),
                   pl.BlockSpec((1, n_pairs, 2 * RET_DK, RET_DV), lambda b, t: (b, 0, 0, 0)),
                   pl.BlockSpec((1, WINDOW, 2 * SWA_KV), per_b3)],
        out_shape=[jax.ShapeDtypeStruct((rows, MIX_WIDTH), BF16),
                   jax.ShapeDtypeStruct((n_streams, n_pairs, 2 * RET_DK, RET_DV), F32),
                   jax.ShapeDtypeStruct((n_streams, WINDOW, 2 * SWA_KV), F32)],
        scratch_shapes=[pltpu.VMEM((n_pairs, 2 * RET_DK, RET_DV), F32),
                        pltpu.VMEM((SWA_KV_HEADS, WINDOW + tl, LANES), BF16),
                        pltpu.VMEM((SWA_KV_HEADS, WINDOW + tl, LANES), BF16),
                        pltpu.VMEM((SWA_KV_HEADS, META_ROWS, LANES), BF16),
                        pltpu.VMEM((SWA_KV_HEADS, META_ROWS, LANES), BF16),
                        pltpu.VMEM((RET_HEADS, tl, tl), F32),
                        pltpu.VMEM((n_pairs, tl, LANES), F32),
                        pltpu.VMEM((RET_HEADS, tl, RET_DV), F32),
                        pltpu.VMEM((n_pairs, 2 * RET_DK, RET_DV), F32)],
        compiler_params=_params(2),
        name="attention",
    )(rqk, rv, gate, sq, skv, meta_kv, hist_kv, s0, rng, sink_tab)


def _post_kernel(*refs, group_steps):
    n_g = len(group_steps)
    in_refs = refs[:2 * n_g]
    wout_ref, g2_ref, wrt_ref = refs[2 * n_g:2 * n_g + 3]
    xmid_ref, hn_ref, wcol_ref, rt_ref, cnt_ref, tri_scr, run_scr = refs[2 * n_g + 3:]
    i = pl.program_id(0)
    tm = hn_ref.shape[0]

    @pl.when(i == 0)
    def _init():
        r = lax.broadcasted_iota(jnp.int32, (tm, tm), 0)
        c = lax.broadcasted_iota(jnp.int32, (tm, tm), 1)
        tri_scr[...] = jnp.where(r < c, 1.0, 0.0).astype(BF16)
        run_scr[...] = jnp.zeros(run_scr.shape, F32)

    omix, x = in_refs[0][...], in_refs[1][...]
    first = group_steps[0]
    for g in range(1, n_g):
        omix = jnp.where(i >= first, in_refs[2 * g][...], omix)
        x = jnp.where(i >= first, in_refs[2 * g + 1][...], x)
        first += group_steps[g]

    xm = x + jnp.dot(omix, wout_ref[...], preferred_element_type=F32)
    xmid_ref[...] = xm
    hn = _row_rms(xm) * g2_ref[...]
    hn_ref[...] = hn
    lt = lax.dot_general(wrt_ref[...], hn.astype(BF16), (((1,), (1,)), ((), ())), preferred_element_type=F32)
    row8 = lax.broadcasted_iota(jnp.int32, (SUBLANES, tm), 0)
    big = jnp.int32(SUBLANES)
    gl = jnp.where(row8 < N_GROUPS, lt[0:SUBLANES], NEG_INF)
    gmax = jnp.max(gl, axis=0, keepdims=True)
    gsum = jnp.sum(jnp.exp(gl - gmax), axis=0, keepdims=True)
    g_sel = jnp.min(jnp.where(gl == gmax, row8, big), axis=0, keepdims=True)
    p_sel = 1.0 / gsum
    el = lt[ROUTER_EXPERT_ROW0:ROUTER_EXPERT_ROW0 + EXPERTS_PER_GROUP]
    for g in range(1, N_GROUPS):
        lo = ROUTER_EXPERT_ROW0 + g * EXPERTS_PER_GROUP
        el = jnp.where(g_sel == g, lt[lo:lo + EXPERTS_PER_GROUP], el)
    m1 = jnp.max(el, axis=0, keepdims=True)
    i1 = jnp.min(jnp.where(el == m1, row8, big), axis=0, keepdims=True)
    el2 = jnp.where(row8 == i1, NEG_INF, el)
    m2 = jnp.max(el2, axis=0, keepdims=True)
    i2 = jnp.min(jnp.where(el2 == m2, row8, big), axis=0, keepdims=True)
    e2 = jnp.exp(m2 - m1)
    inv = 1.0 / (1.0 + e2)
    w1 = p_sel * inv
    w2 = p_sel * (e2 * inv)
    eid1 = g_sel * EXPERTS_PER_GROUP + i1
    eid2 = g_sel * EXPERTS_PER_GROUP + i2

    rowe = lax.broadcasted_iota(jnp.int32, (N_EXPERTS, tm), 0)
    oh = jnp.where((rowe == eid1) | (rowe == eid2), 1.0, 0.0).astype(BF16)
    run = run_scr[...]
    pref = jnp.dot(oh, tri_scr[...], preferred_element_type=F32) + jnp.concatenate([run] * (tm // LANES), axis=1)
    r1 = jnp.sum(jnp.where(rowe == eid1, pref, 0.0), axis=0, keepdims=True)
    r2 = jnp.sum(jnp.where(rowe == eid2, pref, 0.0), axis=0, keepdims=True)
    run = run + jnp.dot(oh, jnp.ones((tm, LANES), BF16), preferred_element_type=F32)
    run_scr[...] = run
    cnt_ref[...] = run

    out = jnp.zeros((SUBLANES, tm), F32)
    for k, v in enumerate([eid1.astype(F32), eid2.astype(F32), w1, w2, r1, r2]):
        out = jnp.where(row8 == k, v, out)
    rt_ref[...] = out
    rowl = lax.broadcasted_iota(jnp.int32, (LANES, tm), 0)
    wcol_ref[...] = jnp.where(rowl == 0, w1, jnp.where(rowl == 1, w2, 0.0)).T


def _post(omix_list, x_list, w_out_bf, g2, w_router_t_bf, tm):
    group_steps = tuple(x.shape[0] // tm for x in x_list)
    t_rows = sum(x.shape[0] for x in x_list)
    assert tm % LANES == 0 and all(x.shape[0] % tm == 0 for x in x_list)
    row = lambda i: (i, 0)
    const = lambda i: (0, 0)
    in_specs, args = [], []
    first = 0
    for omix, x, steps in zip(omix_list, x_list, group_steps):
        held = lambda i, first=first, steps=steps: (jnp.clip(i - first, 0, steps - 1), 0)
        in_specs += [pl.BlockSpec((tm, MIX_WIDTH), held), pl.BlockSpec((tm, D_MODEL), held)]
        args += [omix, x]
        first += steps
    return pl.pallas_call(
        functools.partial(_post_kernel, group_steps=group_steps),
        grid=(sum(group_steps),),
        in_specs=in_specs + [pl.BlockSpec((MIX_WIDTH, D_MODEL), const),
                             pl.BlockSpec((1, D_MODEL), const),
                             pl.BlockSpec((ROUTER_ROWS, D_MODEL), const)],
        out_specs=[pl.BlockSpec((tm, D_MODEL), row),
                   pl.BlockSpec((tm, D_MODEL), row),
                   pl.BlockSpec((tm, LANES), row),
                   pl.BlockSpec((SUBLANES, tm), lambda i: (0, i)),
                   pl.BlockSpec((N_EXPERTS, LANES), const)],
        out_shape=[jax.ShapeDtypeStruct((t_rows, D_MODEL), F32),
                   jax.ShapeDtypeStruct((t_rows, D_MODEL), F32),
                   jax.ShapeDtypeStruct((t_rows, LANES), F32),
                   jax.ShapeDtypeStruct((SUBLANES, t_rows), F32),
                   jax.ShapeDtypeStruct((N_EXPERTS, LANES), F32)],
        scratch_shapes=[pltpu.VMEM((tm, tm), BF16), pltpu.VMEM((N_EXPERTS, LANES), F32)],
        compiler_params=_params(1),
        name="post",
    )(*args, w_out_bf, g2, w_router_t_bf)


def _expert_kernel(te_ref, nv_ref, tok_ref, nxt_ref, hn_ref, wgu_ref, wd_ref, y_ref, xbuf, sems, *, tm):
    i = pl.program_id(0)
    n_valid = nv_ref[0]
    slot = i % 2

    def issue(t_ref, to_slot):
        def body(blk, carry):
            for k in range(SUBLANES):
                pltpu.make_async_copy(hn_ref.at[pl.ds(t_ref[0, 0, blk * SUBLANES + k], 1)],
                                      xbuf.at[to_slot, blk, pl.ds(k, 1)], sems.at[to_slot]).start(priority=k % 2)
            return carry

        lax.fori_loop(0, tm // SUBLANES, body, 0)

    @pl.when(i == 0)
    def _first():
        issue(tok_ref, 0)

    @pl.when(i + 1 < n_valid)
    def _ahead():
        issue(nxt_ref, 1 - slot)

    @pl.when(i < n_valid)
    def _compute():
        pltpu.make_async_copy(xbuf.at[slot], xbuf.at[slot], sems.at[slot]).wait()
        x = xbuf[slot].reshape(tm, D_MODEL).astype(BF16)
        gu = jnp.dot(x, wgu_ref[0], preferred_element_type=F32)
        g = gu[:, 0:EXPERT_FF]
        a = (g * jax.nn.sigmoid(g) * gu[:, EXPERT_FF:2 * EXPERT_FF]).astype(BF16)
        y_ref[...] = jnp.dot(a, wd_ref[0], preferred_element_type=F32)

    @pl.when(i >= n_valid)
    def _skip():
        y_ref[...] = jnp.zeros(y_ref.shape, F32)


def _experts(tile_expert, n_valid, tok_of_slot, hn, wgu_bf, wd_bf, n_tiles, tm):
    last = lambda i, nv: jnp.minimum(i, nv[0] - 1)
    tok3 = tok_of_slot.reshape(n_tiles, 1, tm)
    return pl.pallas_call(
        functools.partial(_expert_kernel, tm=tm),
        grid_spec=pltpu.PrefetchScalarGridSpec(
            num_scalar_prefetch=2,
            grid=(n_tiles,),
            in_specs=[pl.BlockSpec((1, 1, tm), lambda i, te, nv: (last(i, nv), 0, 0), memory_space=pltpu.SMEM),
                      pl.BlockSpec((1, 1, tm), lambda i, te, nv: (last(i + 1, nv), 0, 0), memory_space=pltpu.SMEM),
                      pl.BlockSpec(memory_space=pl.ANY),
                      pl.BlockSpec((1, D_MODEL, 2 * EXPERT_FF), lambda i, te, nv: (te[last(i, nv)], 0, 0)),
                      pl.BlockSpec((1, EXPERT_FF, D_MODEL), lambda i, te, nv: (te[last(i, nv)], 0, 0))],
            out_specs=pl.BlockSpec((tm, D_MODEL), lambda i, te, nv: (i, 0)),
            scratch_shapes=[pltpu.VMEM((2, tm // SUBLANES, SUBLANES, D_MODEL), F32), pltpu.SemaphoreType.DMA((2,))]),
        out_shape=jax.ShapeDtypeStruct((n_tiles * tm, D_MODEL), F32),
        compiler_params=_params(1),
        name="experts",
    )(tile_expert, n_valid, tok3, tok3, hn, wgu_bf, wd_bf)


def _step_major(pos, tm):
    return [pos[e].reshape(-1, 1, tm) for e in range(2)]


def _combine_kernel(pos0_ref, pos1_ref, nxt0_ref, nxt1_ref, ys_ref, xmid_ref, wcol_ref, *refs,
                    tm, group_steps):
    out_refs = refs[:len(group_steps)]
    ybuf, sems = refs[len(group_steps):]
    n_steps = sum(group_steps)
    i = pl.program_id(0)
    slot = i % 2

    def issue(p_refs, to_slot):
        def body(blk, carry):
            for k in range(SUBLANES):
                r = blk * SUBLANES + k
                for e in range(2):
                    pltpu.make_async_copy(ys_ref.at[pl.ds(p_refs[e][0, 0, r], 1)],
                                          ybuf.at[to_slot, e, blk, pl.ds(k, 1)], sems.at[to_slot]).start(priority=e)
            return carry

        lax.fori_loop(0, tm // SUBLANES, body, 0)

    @pl.when(i == 0)
    def _first():
        issue((pos0_ref, pos1_ref), 0)

    @pl.when(i + 1 < n_steps)
    def _ahead():
        issue((nxt0_ref, nxt1_ref), 1 - slot)

    for e in range(2):
        pltpu.make_async_copy(ybuf.at[slot, e], ybuf.at[slot, e], sems.at[slot]).wait()
    w = wcol_ref[...]
    y0 = ybuf[slot, 0].reshape(tm, D_MODEL)
    y1 = ybuf[slot, 1].reshape(tm, D_MODEL)
    out = xmid_ref[...] + w[:, 0:1] * y0 + w[:, 1:2] * y1
    first = 0
    for out_ref, steps in zip(out_refs, group_steps):
        @pl.when((i >= first) & (i < first + steps))
        def _store(out_ref=out_ref):
            out_ref[...] = out
        first += steps


def _combine(pos, ys, xmid, wcol, group_rows, tm):
    group_steps = tuple(r // tm for r in group_rows)
    n_steps = sum(group_steps)
    assert tm % SUBLANES == 0 and all(r % tm == 0 for r in group_rows)
    pos3 = _step_major(pos, tm)
    row = lambda i: (i, 0)
    cur = pl.BlockSpec((1, 1, tm), lambda i: (i, 0, 0), memory_space=pltpu.SMEM)
    nxt = pl.BlockSpec((1, 1, tm), lambda i: (jnp.minimum(i + 1, n_steps - 1), 0, 0), memory_space=pltpu.SMEM)
    out_specs, first = [], 0
    for steps in group_steps:
        out_specs.append(pl.BlockSpec((tm, D_MODEL),
                                      lambda i, first=first, steps=steps: (jnp.clip(i - first, 0, steps - 1), 0)))
        first += steps
    return pl.pallas_call(
        functools.partial(_combine_kernel, tm=tm, group_steps=group_steps),
        grid=(n_steps,),
        in_specs=[cur, cur, nxt, nxt,
                  pl.BlockSpec(memory_space=pl.ANY),
                  pl.BlockSpec((tm, D_MODEL), row),
                  pl.BlockSpec((tm, LANES), row)],
        out_specs=out_specs,
        out_shape=[jax.ShapeDtypeStruct((r, D_MODEL), F32) for r in group_rows],
        scratch_shapes=[pltpu.VMEM((2, 2, tm // SUBLANES, SUBLANES, D_MODEL), F32), pltpu.SemaphoreType.DMA((2,))],
        compiler_params=_params(1),
        name="combine",
    )(*pos3, *pos3, ys, xmid, wcol)


def _bucket(ends, idx):
    n = jnp.sum((ends[None, :] <= idx[:, None]).astype(jnp.int32), axis=1)
    return jnp.minimum(n, ends.shape[0] - 1)


def _tile_for(rows, pref):
    tm = min(pref, rows)
    assert rows % tm == 0
    return tm


def kernel(x_prompt, x_sample, cache_ret_state, cache_swa_k, cache_swa_v, meta_tokens, norm1_g, w_in, q_norm_g,
           k_norm_g, ret_norm_g, attn_sinks, w_out, norm2_g, w_group, w_expert, w_gate, w_up, w_down):
    assert norm1_g.shape[0] == 1, "single-layer trunk"
    bp, lp, _ = x_prompt.shape
    bs, ls, _ = x_sample.shape
    n_pairs = RET_HEADS // 2

    g1 = norm1_g[0][None, :]
    g2 = norm2_g[0][None, :]
    w_in_bf = w_in[0].astype(BF16)
    w_out_bf = w_out[0].astype(BF16)
    qg2 = jnp.tile(q_norm_g[0], 2)[None, :]
    kg2 = jnp.tile(k_norm_g[0], 2)[None, :]
    rng = ret_norm_g[0].reshape(RET_HEADS, RET_DV)
    sink_tab = jnp.broadcast_to(jnp.repeat(attn_sinks[0], CHUNK).reshape(SWA_KV_HEADS, 1, 4 * CHUNK),
                                (SWA_KV_HEADS, SUBLANES, 4 * CHUNK))
    w_router_t = jnp.zeros((ROUTER_ROWS, D_MODEL), F32)
    w_router_t = w_router_t.at[0:N_GROUPS].set(w_group[0].T)
    w_router_t = w_router_t.at[ROUTER_EXPERT_ROW0:ROUTER_EXPERT_ROW0 + N_EXPERTS].set(w_expert[0].T)
    w_router_t_bf = w_router_t.astype(BF16)
    wgu_bf = jnp.concatenate([w_gate[0], w_up[0]], axis=-1).astype(BF16)
    wd_bf = w_down[0].astype(BF16)

    meta_rows = 2 * CHUNK
    m_pad = jnp.zeros((meta_rows, D_MODEL), F32).at[0:N_META].set(meta_tokens)
    m_rqk, m_rv, _, _, m_skv = _proj(m_pad, jnp.arange(meta_rows, dtype=jnp.int32), meta_rows, g1, w_in_bf, qg2, kg2)
    s_meta = _meta_state(m_rqk, m_rv)[None]
    meta_kv = m_skv[0:N_META]

    groups = [
        dict(x=x_prompt.reshape(bp * lp, D_MODEL), n=bp, seq=lp, pos0=N_META, has_hist=False, s0=s_meta,
             hist=jnp.zeros((1, WINDOW, 2 * SWA_KV), F32)),
        dict(x=x_sample.reshape(bs * ls, D_MODEL), n=bs, seq=ls, pos0=N_META + PAST_LEN, has_hist=True,
             s0=cache_ret_state[0].reshape(bs, n_pairs, 2 * RET_DK, RET_DV),
             hist=jnp.concatenate([cache_swa_k[0].reshape(bs, WINDOW, SWA_KV),
                                   cache_swa_v[0].reshape(bs, WINDOW, SWA_KV)], axis=-1)),
    ]

    for g in groups:
        rows = g["n"] * g["seq"]
        tm = _tile_for(rows, TOKEN_TILE)
        pos = g["pos0"] + jnp.arange(g["seq"], dtype=jnp.int32)
        if g["seq"] < tm:
            assert tm % g["seq"] == 0
            pos = jnp.tile(pos, tm // g["seq"])
        else:
            assert g["seq"] % tm == 0
        rqk, rv, gate, sq, skv = _proj(g["x"], pos, tm, g1, w_in_bf, qg2, kg2)
        tl = min(ATTN_TILE, g["seq"])
        omix, s_out, kv_out = _attention(rqk, rv, gate, sq, skv, meta_kv, g["hist"], g["s0"], rng, sink_tab,
                                         n_streams=g["n"], seq=g["seq"], tl=tl, has_hist=g["has_hist"])
        g.update(omix=omix, s_out=s_out, kv_out=kv_out, rows=rows)

    group_rows = [g["rows"] for g in groups]
    total_rows = sum(group_rows)
    xmid, hn, wcol, route_t, cnt = _post([g["omix"] for g in groups], [g["x"] for g in groups], w_out_bf, g2,
                                         w_router_t_bf, _tile_for(min(group_rows), TOKEN_TILE))

    te = EXPERT_TILE
    n_tiles = (2 * total_rows) // te + N_EXPERTS
    counts = cnt[:, 0].astype(jnp.int32)
    tiles_e = (counts + te - 1) // te
    off = (jnp.cumsum(tiles_e) - tiles_e) * te
    tile_end = jnp.cumsum(tiles_e)
    n_valid = tile_end[-1:].astype(jnp.int32)
    tile_expert = _bucket(tile_end, jnp.arange(n_tiles, dtype=jnp.int32))
    eid = route_t[0:2].astype(jnp.int32)
    off_sel = jnp.sum(jnp.where(eid[None] == jnp.arange(N_EXPERTS, dtype=jnp.int32)[:, None, None],
                                off[:, None, None], 0), axis=0)
    pos = (off_sel + route_t[4:6].astype(jnp.int32)).astype(jnp.int32)
    tok = jnp.tile(jnp.arange(total_rows, dtype=jnp.int32), 2)
    tok_of_slot = jnp.zeros((n_tiles * te,), jnp.int32).at[pos.reshape(-1)].set(tok, unique_indices=True)

    ys = _experts(tile_expert, n_valid, tok_of_slot, hn, wgu_bf, wd_bf, n_tiles, te)
    outs = _combine(pos, ys, xmid, wcol, group_rows, _tile_for(min(group_rows), MOVE_TILE))
    outs = [y.reshape(g["n"], g["seq"], D_MODEL) for y, g in zip(outs, groups)]

    def caches(g):
        kv = g["kv_out"]
        k = kv[:, :, 0:SWA_KV].reshape(g["n"], WINDOW, SWA_KV_HEADS, SWA_HD)[None]
        v = kv[:, :, SWA_KV:2 * SWA_KV].reshape(g["n"], WINDOW, SWA_KV_HEADS, SWA_HD)[None]
        s = g["s_out"].reshape(g["n"], RET_HEADS, RET_DK, RET_DV)[None]
        return s, k, v

    sp, kp, vp = caches(groups[0])
    ss, ks, vs = caches(groups[1])
    return (outs[0], outs[1], sp, kp, vp, ss, ks, vs)
```

```python
import functools

import numpy as np
import jax
import jax.numpy as jnp
from jax import lax
from jax.experimental import pallas as pl
from jax.experimental.pallas import tpu as pltpu

F32 = jnp.float32
BF16 = jnp.bfloat16

D_MODEL = 1024
PAST_LEN = 4096
CHUNK = 64
N_META = 16
RET_HEADS = 4
RET_DK = 64
RET_DV = 128
SWA_HEADS = 8
SWA_KV_HEADS = 2
SWA_HD = 64
WINDOW = 128
ROPE_THETA = 10000.0
N_GROUPS = 4
EXPERTS_PER_GROUP = 8
N_EXPERTS = N_GROUPS * EXPERTS_PER_GROUP
EXPERT_FF = 256
EPS = 1e-6
NEG_INF = -1e30
LOG2E = float(np.log2(np.e))
RET_Q = RET_HEADS * RET_DK
RET_V = RET_HEADS * RET_DV
SWA_Q = SWA_HEADS * SWA_HD
SWA_KV = SWA_KV_HEADS * SWA_HD
MIX_WIDTH = RET_V + SWA_Q
IN_WIDTH = 2 * RET_Q + 2 * RET_V + SWA_Q + 2 * SWA_KV

LANES = 128
TOKEN_TILE = 512
ATTN_TILE = 256
EXPERT_TILE = 256
MOVE_TILE = 256
SUBLANES = 8
ROUTER_EXPERT_ROW0 = 8
ROUTER_ROWS = 64
META_ROWS = 64
VMEM_LIMIT = 56 * 1024 * 1024

_LOG_G = [float(np.log1p(-np.exp2(-5.0 - h))) for h in range(RET_HEADS)]


def _params(n_axes):
    return pltpu.CompilerParams(dimension_semantics=("arbitrary",) * n_axes, vmem_limit_bytes=VMEM_LIMIT)


def _split_bf16(a):
    hi = a.astype(BF16)
    return hi, (a - hi.astype(F32)).astype(BF16)


def _split_dot(a, w2):
    hi, lo = _split_bf16(a)
    return jnp.dot(jnp.concatenate([hi, lo], axis=1), w2, preferred_element_type=F32)


def _lane_sum(a):
    return _split_dot(a, jnp.ones((2 * LANES, LANES), BF16))


def _head_sum_matrix():
    i = lax.broadcasted_iota(jnp.int32, (2 * LANES, LANES), 0) % LANES
    j = lax.broadcasted_iota(jnp.int32, (2 * LANES, LANES), 1)
    return jnp.where((i < SWA_HD) == (j < SWA_HD), 1.0, 0.0).astype(BF16)


def _rope(t, c, s1, s2):
    half = SWA_HD // 2
    return t * c + pltpu.roll(t, LANES - half, 1) * s1 + pltpu.roll(t, half, 1) * s2


def _head_rms(t, g, head_w):
    ms = _split_dot(t * t, head_w) * (1.0 / SWA_HD)
    return t * lax.rsqrt(ms + EPS) * g


def _row_rms(x):
    n_tiles = x.shape[1] // LANES
    ss = x[:, 0:LANES] * x[:, 0:LANES]
    for j in range(1, n_tiles):
        ss = ss + x[:, j * LANES:(j + 1) * LANES] * x[:, j * LANES:(j + 1) * LANES]
    r = lax.rsqrt(_lane_sum(ss) * (1.0 / x.shape[1]) + EPS)
    return x * jnp.concatenate([r] * n_tiles, axis=1)


def _proj_kernel(x_ref, g1_ref, w_ref, qg_ref, kg_ref, cos_ref, s1_ref, s2_ref,
                 rqk_ref, rv_ref, gate_ref, sq_ref, skv_ref):
    xn = (_row_rms(x_ref[...]) * g1_ref[...]).astype(BF16)
    c, s1, s2 = cos_ref[...], s1_ref[...], s2_ref[...]
    head_w = _head_sum_matrix()

    def seg(a, b):
        return jnp.dot(xn, w_ref[:, a:b], preferred_element_type=F32)

    def tile(h, j):
        return h[:, j * LANES:(j + 1) * LANES]

    h = seg(0, 2 * RET_Q)
    for j in range(2):
        rqk_ref[:, j * LANES:(j + 1) * LANES] = _rope(tile(h, j), c, s1, s2).astype(BF16)
    for j in range(2, 4):
        rqk_ref[:, j * LANES:(j + 1) * LANES] = (_rope(tile(h, j), c, s1, s2) * (RET_DK ** -0.5)).astype(BF16)
    a = 2 * RET_Q
    rv_ref[...] = seg(a, a + RET_V).astype(BF16)
    a += RET_V
    g = seg(a, a + RET_V)
    gate_ref[...] = (g * jax.nn.sigmoid(g)).astype(BF16)
    a += RET_V
    h = seg(a, a + SWA_Q)
    qg = qg_ref[...]
    for j in range(SWA_Q // LANES):
        sq_ref[:, j * LANES:(j + 1) * LANES] = _rope(_head_rms(tile(h, j), qg, head_w), c, s1, s2).astype(BF16)
    a += SWA_Q
    h = seg(a, a + 2 * SWA_KV)
    skv_ref[:, 0:LANES] = _rope(_head_rms(tile(h, 0), kg_ref[...], head_w), c, s1, s2)
    skv_ref[:, LANES:2 * LANES] = tile(h, 1)


def _rope_tables(pos):
    half = SWA_HD // 2
    inv = ROPE_THETA ** (-jnp.arange(half, dtype=F32) / half)
    ang = pos.astype(F32)[:, None] * inv[None, :]
    cos, sin = jnp.cos(ang), jnp.sin(ang)
    z = jnp.zeros_like(sin)
    return (jnp.tile(cos, (1, 4)),
            jnp.tile(jnp.concatenate([-sin, z], axis=1), (1, 2)),
            jnp.tile(jnp.concatenate([z, sin], axis=1), (1, 2)))


def _proj(x2d, pos_rows, tm, g1, w_in_bf, qg2, kg2):
    t_rows = x2d.shape[0]
    n_tiles = t_rows // tm
    n_pos_tiles = pos_rows.shape[0] // tm
    cos, s1, s2 = _rope_tables(pos_rows)
    row = lambda i: (i, 0)
    const = lambda i: (0, 0)
    tab = lambda i: (i % n_pos_tiles, 0)
    return pl.pallas_call(
        _proj_kernel,
        grid=(n_tiles,),
        in_specs=[pl.BlockSpec((tm, D_MODEL), row),
                  pl.BlockSpec((1, D_MODEL), const),
                  pl.BlockSpec((D_MODEL, IN_WIDTH), const),
                  pl.BlockSpec((1, LANES), const),
                  pl.BlockSpec((1, LANES), const),
                  pl.BlockSpec((tm, LANES), tab),
                  pl.BlockSpec((tm, LANES), tab),
                  pl.BlockSpec((tm, LANES), tab)],
        out_specs=[pl.BlockSpec((tm, 2 * RET_Q), row),
                   pl.BlockSpec((tm, RET_V), row),
                   pl.BlockSpec((tm, RET_V), row),
                   pl.BlockSpec((tm, SWA_Q), row),
                   pl.BlockSpec((tm, 2 * SWA_KV), row)],
        out_shape=[jax.ShapeDtypeStruct((t_rows, 2 * RET_Q), BF16),
                   jax.ShapeDtypeStruct((t_rows, RET_V), BF16),
                   jax.ShapeDtypeStruct((t_rows, RET_V), BF16),
                   jax.ShapeDtypeStruct((t_rows, SWA_Q), BF16),
                   jax.ShapeDtypeStruct((t_rows, 2 * SWA_KV), F32)],
        compiler_params=_params(1),
        name="proj",
    )(x2d, g1, w_in_bf, qg2, kg2, cos, s1, s2)


def _pair_update(k_bf, v0_bf, v1_bf, wt):
    kw = (k_bf.astype(F32) * wt).astype(BF16)
    dn = (((0,), (0,)), ((), ()))
    a0 = lax.dot_general(kw, v0_bf, dn, preferred_element_type=F32)
    a1 = lax.dot_general(kw, v1_bf, dn, preferred_element_type=F32)
    top = lax.broadcasted_iota(jnp.int32, a0.shape, 0) < RET_DK
    return jnp.where(top, a0, a1)


def _decay_rows(n, pair, rows_back_from):
    i = lax.broadcasted_iota(jnp.int32, (n, LANES), 0).astype(F32)
    lane = lax.broadcasted_iota(jnp.int32, (n, LANES), 1)
    lg = jnp.where(lane < RET_DK, _LOG_G[2 * pair], _LOG_G[2 * pair + 1])
    return jnp.exp((rows_back_from - i) * lg)


def _meta_state_kernel(rqk_ref, rv_ref, s_ref, *, n_rows):
    for p in range(RET_HEADS // 2):
        k = rqk_ref[:, RET_Q + p * LANES:RET_Q + (p + 1) * LANES]
        wt = _decay_rows(n_rows, p, float(N_META - 1))
        s_ref[p] = _pair_update(k, rv_ref[:, (2 * p) * LANES:(2 * p + 1) * LANES],
                                rv_ref[:, (2 * p + 1) * LANES:(2 * p + 2) * LANES], wt)


def _meta_state(m_rqk, m_rv):
    n_rows = m_rqk.shape[0]
    return pl.pallas_call(
        functools.partial(_meta_state_kernel, n_rows=n_rows),
        out_shape=jax.ShapeDtypeStruct((RET_HEADS // 2, 2 * RET_DK, RET_DV), F32),
        name="meta_state",
    )(m_rqk, m_rv)


def _dup_halves(a, lo_mask):
    sw = pltpu.roll(a, SWA_HD, 1)
    return jnp.where(lo_mask, a, sw), jnp.where(lo_mask, sw, a)


def _attn_kernel(rqk_ref, rv_ref, gate_ref, sq_ref, skv_ref, meta_ref, hist_ref, s0_ref, rng_ref, sink_ref,
                 omix_ref, sout_ref, kvout_ref,
                 s_scr, kd_scr, vd_scr, mk_scr, mv_scr, dec_scr, wt_scr, cs_scr, gam_scr,
                 *, tl, has_hist):
    b = pl.program_id(0)
    t = pl.program_id(1)
    nt = pl.num_programs(1)
    n_chunks = tl // CHUNK
    n_pairs = RET_HEADS // 2
    lo_tl = lax.broadcasted_iota(jnp.int32, (tl, LANES), 1) < SWA_HD
    lo_c = lax.broadcasted_iota(jnp.int32, (CHUNK, LANES), 1) < SWA_HD

    @pl.when((b == 0) & (t == 0))
    def _tables():
        i = lax.broadcasted_iota(jnp.int32, (tl, tl), 0)
        j = lax.broadcasted_iota(jnp.int32, (tl, tl), 1)
        diff = (i - j).astype(F32)
        row = lax.broadcasted_iota(jnp.int32, (tl, LANES), 0).astype(F32)
        for h in range(RET_HEADS):
            dec_scr[h] = jnp.where(diff >= 0.0, jnp.exp(jnp.maximum(diff, 0.0) * _LOG_G[h]), 0.0)
            cs_scr[h] = jnp.exp((row + 1.0) * _LOG_G[h])
        top = lax.broadcasted_iota(jnp.int32, (2 * RET_DK, RET_DV), 0) < RET_DK
        for p in range(n_pairs):
            wt_scr[p] = _decay_rows(tl, p, float(tl - 1))
            gam_scr[p] = jnp.where(top, jnp.exp(jnp.float32(tl * _LOG_G[2 * p])), jnp.exp(jnp.float32(tl * _LOG_G[2 * p + 1])))
        lo_m = lax.broadcasted_iota(jnp.int32, (N_META, LANES), 1) < SWA_HD
        mk0, mk1 = _dup_halves(meta_ref[:, 0:LANES], lo_m)
        mv0, mv1 = _dup_halves(meta_ref[:, LANES:2 * LANES], lo_m)
        mk_scr[...] = jnp.zeros(mk_scr.shape, BF16)
        mv_scr[...] = jnp.zeros(mv_scr.shape, BF16)
        mk_scr[0, 0:N_META] = mk0.astype(BF16)
        mk_scr[1, 0:N_META] = mk1.astype(BF16)
        mv_scr[0, 0:N_META] = mv0.astype(BF16)
        mv_scr[1, 0:N_META] = mv1.astype(BF16)

    @pl.when(t == 0)
    def _stream_start():
        s_scr[...] = s0_ref[0]
        if has_hist:
            lo_w = lax.broadcasted_iota(jnp.int32, (WINDOW, LANES), 1) < SWA_HD
            k0, k1 = _dup_halves(hist_ref[0, :, 0:LANES], lo_w)
            v0, v1 = _dup_halves(hist_ref[0, :, LANES:2 * LANES], lo_w)
            kd_scr[0, 0:WINDOW] = k0.astype(BF16)
            kd_scr[1, 0:WINDOW] = k1.astype(BF16)
            vd_scr[0, 0:WINDOW] = v0.astype(BF16)
            vd_scr[1, 0:WINDOW] = v1.astype(BF16)
        else:
            z = jnp.zeros((WINDOW, LANES), BF16)
            for kv in range(SWA_KV_HEADS):
                kd_scr[kv, 0:WINDOW] = z
                vd_scr[kv, 0:WINDOW] = z

    k0, k1 = _dup_halves(skv_ref[:, 0:LANES], lo_tl)
    v0, v1 = _dup_halves(skv_ref[:, LANES:2 * LANES], lo_tl)
    kd_scr[0, WINDOW:WINDOW + tl] = k0.astype(BF16)
    kd_scr[1, WINDOW:WINDOW + tl] = k1.astype(BF16)
    vd_scr[0, WINDOW:WINDOW + tl] = v0.astype(BF16)
    vd_scr[1, WINDOW:WINDOW + tl] = v1.astype(BF16)

    band = WINDOW + CHUNK
    n_keys = META_ROWS + band
    n_q = 4 * CHUNK
    scale2 = (SWA_HD ** -0.5) * LOG2E
    krow = lax.broadcasted_iota(jnp.int32, (n_keys, n_q), 0)
    zero_c = jnp.zeros((CHUNK, LANES), BF16)
    ones_v = jnp.ones((n_keys, LANES), BF16)
    for c in range(n_chunks):
        if has_hist:
            first_valid = META_ROWS
        else:
            first_valid = jnp.where(t == 0, max(META_ROWS + WINDOW - c * CHUNK, META_ROWS), META_ROWS)
        valid_t = (krow < N_META) | (krow >= first_valid)
        r0 = c * CHUNK
        for kv in range(SWA_KV_HEADS):
            keys = jnp.concatenate([mk_scr[kv], kd_scr[kv, r0:r0 + band]], axis=0)
            vals = jnp.concatenate([mv_scr[kv], vd_scr[kv, r0:r0 + band]], axis=0)
            qa = sq_ref[r0:r0 + CHUNK, (2 * kv) * LANES:(2 * kv + 1) * LANES]
            qb = sq_ref[r0:r0 + CHUNK, (2 * kv + 1) * LANES:(2 * kv + 2) * LANES]
            lhs = jnp.concatenate([jnp.where(lo_c, qa, zero_c), jnp.where(lo_c, zero_c, qa),
                                   jnp.where(lo_c, qb, zero_c), jnp.where(lo_c, zero_c, qb)], axis=0)
            s_t = lax.dot_general(keys, lhs, (((1,), (1,)), ((), ())), preferred_element_type=F32) * scale2
            s_t = jnp.where(valid_t, s_t, NEG_INF)
            s_t = jnp.where(krow == N_META, sink_ref[kv, 0:1, :] * LOG2E, s_t)
            e_t = jnp.exp2(s_t - jnp.max(s_t, axis=0, keepdims=True)).astype(BF16)
            ov = lax.dot_general(e_t, jnp.concatenate([vals, ones_v], axis=1), (((0,), (0,)), ((), ())),
                                 preferred_element_type=F32)
            o = ov[:, 0:LANES] * (1.0 / ov[:, LANES:2 * LANES])
            oa = jnp.where(lo_c, o[0:CHUNK], o[CHUNK:2 * CHUNK])
            ob = jnp.where(lo_c, o[2 * CHUNK:3 * CHUNK], o[3 * CHUNK:4 * CHUNK])
            base = RET_V + (2 * kv) * LANES
            omix_ref[r0:r0 + CHUNK, base:base + LANES] = oa.astype(BF16)
            omix_ref[r0:r0 + CHUNK, base + LANES:base + 2 * LANES] = ob.astype(BF16)

    zero_t = jnp.zeros((tl, LANES), BF16)
    for p in range(n_pairs):
        q = rqk_ref[:, p * LANES:(p + 1) * LANES]
        k = rqk_ref[:, RET_Q + p * LANES:RET_Q + (p + 1) * LANES]
        lhs = jnp.concatenate([jnp.where(lo_tl, q, zero_t), jnp.where(lo_tl, zero_t, q)], axis=0)
        s = lax.dot_general(lhs, k, (((1,), (1,)), ((), ())), preferred_element_type=F32)
        cross = jnp.dot(lhs, s_scr[p].astype(BF16), preferred_element_type=F32)
        for i in range(2):
            h = 2 * p + i
            v = rv_ref[:, h * LANES:(h + 1) * LANES]
            a = (s[i * tl:(i + 1) * tl] * dec_scr[h]).astype(BF16)
            o = jnp.dot(a, v, preferred_element_type=F32) + cross[i * tl:(i + 1) * tl] * cs_scr[h]
            r = o * lax.rsqrt(_lane_sum(o * o) * (1.0 / RET_DV) + EPS) * rng_ref[h:h + 1, :]
            omix_ref[:, h * LANES:(h + 1) * LANES] = (r * gate_ref[:, h * LANES:(h + 1) * LANES].astype(F32)).astype(BF16)
        u = _pair_update(k, rv_ref[:, (2 * p) * LANES:(2 * p + 1) * LANES],
                         rv_ref[:, (2 * p + 1) * LANES:(2 * p + 2) * LANES], wt_scr[p])
        s_scr[p] = gam_scr[p] * s_scr[p] + u

    if tl >= WINDOW:
        @pl.when(t + 1 < nt)
        def _carry_window():
            for kv in range(SWA_KV_HEADS):
                kd_scr[kv, 0:WINDOW] = kd_scr[kv, tl:tl + WINDOW]
                vd_scr[kv, 0:WINDOW] = vd_scr[kv, tl:tl + WINDOW]

    @pl.when(t + 1 == nt)
    def _stream_end():
        sout_ref[0] = s_scr[...]
        if tl >= WINDOW:
            kvout_ref[0] = skv_ref[tl - WINDOW:tl, :]
        else:
            kvout_ref[0, 0:WINDOW - tl] = hist_ref[0, tl:WINDOW, :]
            kvout_ref[0, WINDOW - tl:WINDOW] = skv_ref[...]


def _attention(rqk, rv, gate, sq, skv, meta_kv, hist_kv, s0, rng, sink_tab, *, n_streams, seq, tl, has_hist):
    nt = seq // tl
    assert tl % CHUNK == 0 and seq % tl == 0
    assert tl >= WINDOW or (nt == 1 and has_hist)
    n_pairs = RET_HEADS // 2
    s0_shared = s0.shape[0] == 1
    row = lambda b, t: (b * nt + t, 0)
    const2 = lambda b, t: (0, 0)
    const3 = lambda b, t: (0, 0, 0)
    per_b3 = lambda b, t: (b, 0, 0)
    s0_map = (lambda b, t: (0, 0, 0, 0)) if s0_shared else (lambda b, t: (b, 0, 0, 0))
    hist_map = per_b3 if has_hist else const3
    rows = n_streams * seq
    return pl.pallas_call(
        functools.partial(_attn_kernel, tl=tl, has_hist=has_hist),
        grid=(n_streams, nt),
        in_specs=[pl.BlockSpec((tl, 2 * RET_Q), row),
                  pl.BlockSpec((tl, RET_V), row),
                  pl.BlockSpec((tl, RET_V), row),
                  pl.BlockSpec((tl, SWA_Q), row),
                  pl.BlockSpec((tl, 2 * SWA_KV), row),
                  pl.BlockSpec((N_META, 2 * SWA_KV), const2),
                  pl.BlockSpec((1, WINDOW, 2 * SWA_KV), hist_map),
                  pl.BlockSpec((1, n_pairs, 2 * RET_DK, RET_DV), s0_map),
                  pl.BlockSpec((RET_HEADS, RET_DV), const2),
                  pl.BlockSpec((SWA_KV_HEADS, SUBLANES, 4 * CHUNK), const3)],
        out_specs=[pl.BlockSpec((tl, MIX_WIDTH), row),
                   pl.BlockSpec((1, n_pairs, 2 * RET_DK, RET_DV), lambda b, t: (b, 0, 0, 0)),
                   pl.BlockSpec((1, WINDOW, 2 * SWA_KV), per_b3)],
        out_shape=[jax.ShapeDtypeStruct((rows, MIX_WIDTH), BF16),
                   jax.ShapeDtypeStruct((n_streams, n_pairs, 2 * RET_DK, RET_DV), F32),
                   jax.ShapeDtypeStruct((n_streams, WINDOW, 2 * SWA_KV), F32)],
        scratch_shapes=[pltpu.VMEM((n_pairs, 2 * RET_DK, RET_DV), F32),
                        pltpu.VMEM((SWA_KV_HEADS, WINDOW + tl, LANES), BF16),
                        pltpu.VMEM((SWA_KV_HEADS, WINDOW + tl, LANES), BF16),
                        pltpu.VMEM((SWA_KV_HEADS, META_ROWS, LANES), BF16),
                        pltpu.VMEM((SWA_KV_HEADS, META_ROWS, LANES), BF16),
                        pltpu.VMEM((RET_HEADS, tl, tl), F32),
                        pltpu.VMEM((n_pairs, tl, LANES), F32),
                        pltpu.VMEM((RET_HEADS, tl, RET_DV), F32),
                        pltpu.VMEM((n_pairs, 2 * RET_DK, RET_DV), F32)],
        compiler_params=_params(2),
        name="attention",
    )(rqk, rv, gate, sq, skv, meta_kv, hist_kv, s0, rng, sink_tab)


def _post_kernel(omix_ref, x_ref, wout_ref, g2_ref, wrt_ref, base_ref,
                 xmid_ref, hn_ref, wcol_ref, rt_ref, cnt_ref, tri_scr, run_scr):
    i = pl.program_id(0)
    tm = x_ref.shape[0]

    @pl.when(i == 0)
    def _init():
        r = lax.broadcasted_iota(jnp.int32, (tm, tm), 0)
        c = lax.broadcasted_iota(jnp.int32, (tm, tm), 1)
        tri_scr[...] = jnp.where(r < c, 1.0, 0.0).astype(BF16)
        run_scr[...] = base_ref[...]

    xm = x_ref[...] + jnp.dot(omix_ref[...], wout_ref[...], preferred_element_type=F32)
    xmid_ref[...] = xm
    hn = _row_rms(xm) * g2_ref[...]
    hn_ref[...] = hn
    lt = lax.dot_general(wrt_ref[...], hn.astype(BF16), (((1,), (1,)), ((), ())), preferred_element_type=F32)
    row8 = lax.broadcasted_iota(jnp.int32, (SUBLANES, tm), 0)
    big = jnp.int32(SUBLANES)
    gl = jnp.where(row8 < N_GROUPS, lt[0:SUBLANES], NEG_INF)
    gmax = jnp.max(gl, axis=0, keepdims=True)
    gsum = jnp.sum(jnp.exp(gl - gmax), axis=0, keepdims=True)
    g_sel = jnp.min(jnp.where(gl == gmax, row8, big), axis=0, keepdims=True)
    p_sel = 1.0 / gsum
    el = lt[ROUTER_EXPERT_ROW0:ROUTER_EXPERT_ROW0 + EXPERTS_PER_GROUP]
    for g in range(1, N_GROUPS):
        lo = ROUTER_EXPERT_ROW0 + g * EXPERTS_PER_GROUP
        el = jnp.where(g_sel == g, lt[lo:lo + EXPERTS_PER_GROUP], el)
    m1 = jnp.max(el, axis=0, keepdims=True)
    i1 = jnp.min(jnp.where(el == m1, row8, big), axis=0, keepdims=True)
    el2 = jnp.where(row8 == i1, NEG_INF, el)
    m2 = jnp.max(el2, axis=0, keepdims=True)
    i2 = jnp.min(jnp.where(el2 == m2, row8, big), axis=0, keepdims=True)
    e2 = jnp.exp(m2 - m1)
    inv = 1.0 / (1.0 + e2)
    w1 = p_sel * inv
    w2 = p_sel * (e2 * inv)
    eid1 = g_sel * EXPERTS_PER_GROUP + i1
    eid2 = g_sel * EXPERTS_PER_GROUP + i2

    rowe = lax.broadcasted_iota(jnp.int32, (N_EXPERTS, tm), 0)
    oh = jnp.where((rowe == eid1) | (rowe == eid2), 1.0, 0.0).astype(BF16)
    run = run_scr[...]
    pref = jnp.dot(oh, tri_scr[...], preferred_element_type=F32) + jnp.concatenate([run] * (tm // LANES), axis=1)
    r1 = jnp.sum(jnp.where(rowe == eid1, pref, 0.0), axis=0, keepdims=True)
    r2 = jnp.sum(jnp.where(rowe == eid2, pref, 0.0), axis=0, keepdims=True)
    run = run + jnp.dot(oh, jnp.ones((tm, LANES), BF16), preferred_element_type=F32)
    run_scr[...] = run
    cnt_ref[...] = run

    out = jnp.zeros((SUBLANES, tm), F32)
    for k, v in enumerate([eid1.astype(F32), eid2.astype(F32), w1, w2, r1, r2]):
        out = jnp.where(row8 == k, v, out)
    rt_ref[...] = out
    rowl = lax.broadcasted_iota(jnp.int32, (LANES, tm), 0)
    wcol_ref[...] = jnp.where(rowl == 0, w1, jnp.where(rowl == 1, w2, 0.0)).T


def _post(omix, x2d, w_out_bf, g2, w_router_t_bf, base_cnt, tm):
    t_rows = x2d.shape[0]
    assert tm % LANES == 0
    row = lambda i: (i, 0)
    const = lambda i: (0, 0)
    return pl.pallas_call(
        _post_kernel,
        grid=(t_rows // tm,),
        in_specs=[pl.BlockSpec((tm, MIX_WIDTH), row),
                  pl.BlockSpec((tm, D_MODEL), row),
                  pl.BlockSpec((MIX_WIDTH, D_MODEL), const),
                  pl.BlockSpec((1, D_MODEL), const),
                  pl.BlockSpec((ROUTER_ROWS, D_MODEL), const),
                  pl.BlockSpec((N_EXPERTS, LANES), const)],
        out_specs=[pl.BlockSpec((tm, D_MODEL), row),
                   pl.BlockSpec((tm, D_MODEL), row),
                   pl.BlockSpec((tm, LANES), row),
                   pl.BlockSpec((SUBLANES, tm), lambda i: (0, i)),
                   pl.BlockSpec((N_EXPERTS, LANES), const)],
        out_shape=[jax.ShapeDtypeStruct((t_rows, D_MODEL), F32),
                   jax.ShapeDtypeStruct((t_rows, D_MODEL), F32),
                   jax.ShapeDtypeStruct((t_rows, LANES), F32),
                   jax.ShapeDtypeStruct((SUBLANES, t_rows), F32),
                   jax.ShapeDtypeStruct((N_EXPERTS, LANES), F32)],
        scratch_shapes=[pltpu.VMEM((tm, tm), BF16), pltpu.VMEM((N_EXPERTS, LANES), F32)],
        compiler_params=_params(1),
        name="post",
    )(omix, x2d, w_out_bf, g2, w_router_t_bf, base_cnt)


def _step_major(pos, tm):
    return [pos[e].reshape(-1, 1, tm) for e in range(2)]


def _dispatch_kernel(pos0_ref, pos1_ref, *refs, tm, group_steps):
    pos_refs = (pos0_ref, pos1_ref)
    hn_refs = refs[:len(group_steps)]
    xs_ref, sem = refs[len(group_steps):]
    i = pl.program_id(0)

    first = 0
    for hn_ref, steps in zip(hn_refs, group_steps):
        @pl.when((i >= first) & (i < first + steps))
        def _rows(hn_ref=hn_ref):
            def body(blk, carry):
                for k in range(SUBLANES):
                    r = blk * SUBLANES + k
                    for e in range(2):
                        pltpu.make_async_copy(hn_ref.at[blk, pl.ds(k, 1)], xs_ref.at[pl.ds(pos_refs[e][0, 0, r], 1)],
                                              sem).start(priority=e)
                return carry

            lax.fori_loop(0, tm // SUBLANES, body, 0)
        first += steps

    pltpu.make_async_copy(xs_ref.at[pl.ds(0, 2 * tm)], xs_ref.at[pl.ds(0, 2 * tm)], sem).wait()


def _dispatch(pos_list, hn_list, tm):
    group_steps = tuple(hn.shape[0] // tm for hn in hn_list)
    n_steps = sum(group_steps)
    assert tm % SUBLANES == 0
    pos3 = [jnp.concatenate(parts, axis=0) for parts in zip(*[_step_major(p, tm) for p in pos_list])]
    smem = pl.BlockSpec((1, 1, tm), lambda i: (i, 0, 0), memory_space=pltpu.SMEM)
    hn_specs = []
    first = 0
    for steps in group_steps:
        hn_specs.append(pl.BlockSpec((tm // SUBLANES, SUBLANES, D_MODEL),
                                     lambda i, first=first, steps=steps: (jnp.clip(i - first, 0, steps - 1), 0, 0)))
        first += steps
    n_rows_out = 2 * sum(hn.shape[0] for hn in hn_list)
    hn_list = [hn.reshape(hn.shape[0] // SUBLANES, SUBLANES, D_MODEL) for hn in hn_list]
    return pl.pallas_call(
        functools.partial(_dispatch_kernel, tm=tm, group_steps=group_steps),
        grid=(n_steps,),
        in_specs=[smem, smem] + hn_specs,
        out_specs=pl.BlockSpec(memory_space=pl.ANY),
        out_shape=jax.ShapeDtypeStruct((n_rows_out, D_MODEL), F32),
        scratch_shapes=[pltpu.SemaphoreType.DMA(())],
        compiler_params=_params(1),
        name="dispatch",
    )(*pos3, *hn_list)


def _expert_kernel(vt_ref, ve_ref, lo_ref, hi_ref, nv_ref, x_ref, wgu_ref, wd_ref, y_ref):
    v = pl.program_id(0)

    @pl.when(v < nv_ref[0])
    def _compute():
        x = x_ref[...].astype(BF16)
        gu = jnp.dot(x, wgu_ref[0], preferred_element_type=F32)
        g = gu[:, 0:EXPERT_FF]
        a = (g * jax.nn.sigmoid(g) * gu[:, EXPERT_FF:2 * EXPERT_FF]).astype(BF16)
        y = jnp.dot(a, wd_ref[0], preferred_element_type=F32)
        lo, hi = lo_ref[v], hi_ref[v]

        @pl.when(lo == 0)
        def _first_visit():
            y_ref[...] = y

        @pl.when(lo > 0)
        def _later_visit():
            row = lax.broadcasted_iota(jnp.int32, y.shape, 0)
            y_ref[...] = jnp.where((row >= lo) & (row < hi), y, y_ref[...])


def _experts(vis_tile, vis_expert, vis_lo, vis_hi, n_vis, xs, wgu_bf, wd_bf, tm):
    n_steps = vis_tile.shape[0]
    last = lambda v, nv: jnp.minimum(v, nv[0] - 1)
    return pl.pallas_call(
        _expert_kernel,
        grid_spec=pltpu.PrefetchScalarGridSpec(
            num_scalar_prefetch=5,
            grid=(n_steps,),
            in_specs=[pl.BlockSpec((tm, D_MODEL), lambda v, vt, ve, lo, hi, nv: (vt[last(v, nv)], 0)),
                      pl.BlockSpec((1, D_MODEL, 2 * EXPERT_FF), lambda v, vt, ve, lo, hi, nv: (ve[last(v, nv)], 0, 0)),
                      pl.BlockSpec((1, EXPERT_FF, D_MODEL), lambda v, vt, ve, lo, hi, nv: (ve[last(v, nv)], 0, 0))],
            out_specs=pl.BlockSpec((tm, D_MODEL), lambda v, vt, ve, lo, hi, nv: (vt[last(v, nv)], 0))),
        out_shape=jax.ShapeDtypeStruct(xs.shape, F32),
        compiler_params=_params(1),
        name="experts",
    )(vis_tile, vis_expert, vis_lo, vis_hi, n_vis, xs, wgu_bf, wd_bf)


def _combine_kernel(pos0_ref, pos1_ref, nxt0_ref, nxt1_ref, ys_ref, xmid_ref, wcol_ref, out_ref, ybuf, sems,
                    *, tm, n_steps):
    i = pl.program_id(0)
    slot = i % 2

    def issue(p_refs, to_slot):
        def body(blk, carry):
            for k in range(SUBLANES):
                r = blk * SUBLANES + k
                for e in range(2):
                    pltpu.make_async_copy(ys_ref.at[pl.ds(p_refs[e][0, 0, r], 1)],
                                          ybuf.at[to_slot, e, blk, pl.ds(k, 1)], sems.at[to_slot]).start(priority=e)
            return carry

        lax.fori_loop(0, tm // SUBLANES, body, 0)

    @pl.when(i == 0)
    def _first():
        issue((pos0_ref, pos1_ref), 0)

    @pl.when(i + 1 < n_steps)
    def _ahead():
        issue((nxt0_ref, nxt1_ref), 1 - slot)

    for e in range(2):
        pltpu.make_async_copy(ybuf.at[slot, e], ybuf.at[slot, e], sems.at[slot]).wait()
    w = wcol_ref[...]
    y0 = ybuf[slot, 0].reshape(tm, D_MODEL)
    y1 = ybuf[slot, 1].reshape(tm, D_MODEL)
    out_ref[...] = xmid_ref[...] + w[:, 0:1] * y0 + w[:, 1:2] * y1


def _combine(pos, ys, xmid, wcol, tm):
    t_rows = xmid.shape[0]
    n_steps = t_rows // tm
    assert tm % SUBLANES == 0
    pos3 = _step_major(pos, tm)
    row = lambda i: (i, 0)
    cur = pl.BlockSpec((1, 1, tm), lambda i: (i, 0, 0), memory_space=pltpu.SMEM)
    nxt = pl.BlockSpec((1, 1, tm), lambda i: (jnp.minimum(i + 1, n_steps - 1), 0, 0), memory_space=pltpu.SMEM)
    return pl.pallas_call(
        functools.partial(_combine_kernel, tm=tm, n_steps=n_steps),
        grid=(n_steps,),
        in_specs=[cur, cur, nxt, nxt,
                  pl.BlockSpec(memory_space=pl.ANY),
                  pl.BlockSpec((tm, D_MODEL), row),
                  pl.BlockSpec((tm, LANES), row)],
        out_specs=pl.BlockSpec((tm, D_MODEL), row),
        out_shape=jax.ShapeDtypeStruct((t_rows, D_MODEL), F32),
        scratch_shapes=[pltpu.VMEM((2, 2, tm // SUBLANES, SUBLANES, D_MODEL), F32), pltpu.SemaphoreType.DMA((2,))],
        compiler_params=_params(1),
        name="combine",
    )(*pos3, *pos3, ys, xmid, wcol)


def _bucket(ends, idx):
    n = jnp.sum((ends[None, :] <= idx[:, None]).astype(jnp.int32), axis=1)
    return jnp.minimum(n, ends.shape[0] - 1)


def _tile_for(rows, pref):
    tm = min(pref, rows)
    assert rows % tm == 0
    return tm


def kernel(x_prompt, x_sample, cache_ret_state, cache_swa_k, cache_swa_v, meta_tokens, norm1_g, w_in, q_norm_g,
           k_norm_g, ret_norm_g, attn_sinks, w_out, norm2_g, w_group, w_expert, w_gate, w_up, w_down):
    assert norm1_g.shape[0] == 1, "single-layer trunk"
    bp, lp, _ = x_prompt.shape
    bs, ls, _ = x_sample.shape
    n_pairs = RET_HEADS // 2

    g1 = norm1_g[0][None, :]
    g2 = norm2_g[0][None, :]
    w_in_bf = w_in[0].astype(BF16)
    w_out_bf = w_out[0].astype(BF16)
    qg2 = jnp.tile(q_norm_g[0], 2)[None, :]
    kg2 = jnp.tile(k_norm_g[0], 2)[None, :]
    rng = ret_norm_g[0].reshape(RET_HEADS, RET_DV)
    sink_tab = jnp.broadcast_to(jnp.repeat(attn_sinks[0], CHUNK).reshape(SWA_KV_HEADS, 1, 4 * CHUNK),
                                (SWA_KV_HEADS, SUBLANES, 4 * CHUNK))
    w_router_t = jnp.zeros((ROUTER_ROWS, D_MODEL), F32)
    w_router_t = w_router_t.at[0:N_GROUPS].set(w_group[0].T)
    w_router_t = w_router_t.at[ROUTER_EXPERT_ROW0:ROUTER_EXPERT_ROW0 + N_EXPERTS].set(w_expert[0].T)
    w_router_t_bf = w_router_t.astype(BF16)
    wgu_bf = jnp.concatenate([w_gate[0], w_up[0]], axis=-1).astype(BF16)
    wd_bf = w_down[0].astype(BF16)

    meta_rows = 2 * CHUNK
    m_pad = jnp.zeros((meta_rows, D_MODEL), F32).at[0:N_META].set(meta_tokens)
    m_rqk, m_rv, _, _, m_skv = _proj(m_pad, jnp.arange(meta_rows, dtype=jnp.int32), meta_rows, g1, w_in_bf, qg2, kg2)
    s_meta = _meta_state(m_rqk, m_rv)[None]
    meta_kv = m_skv[0:N_META]

    groups = [
        dict(x=x_prompt.reshape(bp * lp, D_MODEL), n=bp, seq=lp, pos0=N_META, has_hist=False, s0=s_meta,
             hist=jnp.zeros((1, WINDOW, 2 * SWA_KV), F32)),
        dict(x=x_sample.reshape(bs * ls, D_MODEL), n=bs, seq=ls, pos0=N_META + PAST_LEN, has_hist=True,
             s0=cache_ret_state[0].reshape(bs, n_pairs, 2 * RET_DK, RET_DV),
             hist=jnp.concatenate([cache_swa_k[0].reshape(bs, WINDOW, SWA_KV),
                                   cache_swa_v[0].reshape(bs, WINDOW, SWA_KV)], axis=-1)),
    ]

    base_cnt = jnp.zeros((N_EXPERTS, LANES), F32)
    for g in groups:
        rows = g["n"] * g["seq"]
        tm = _tile_for(rows, TOKEN_TILE)
        pos = g["pos0"] + jnp.arange(g["seq"], dtype=jnp.int32)
        if g["seq"] < tm:
            assert tm % g["seq"] == 0
            pos = jnp.tile(pos, tm // g["seq"])
        else:
            assert g["seq"] % tm == 0
        rqk, rv, gate, sq, skv = _proj(g["x"], pos, tm, g1, w_in_bf, qg2, kg2)
        tl = min(ATTN_TILE, g["seq"])
        omix, s_out, kv_out = _attention(rqk, rv, gate, sq, skv, meta_kv, g["hist"], g["s0"], rng, sink_tab,
                                         n_streams=g["n"], seq=g["seq"], tl=tl, has_hist=g["has_hist"])
        xmid, hn, wcol, route_t, base_cnt = _post(omix, g["x"], w_out_bf, g2, w_router_t_bf, base_cnt, tm)
        g.update(xmid=xmid, hn=hn, wcol=wcol, route_t=route_t, s_out=s_out, kv_out=kv_out)

    te = EXPERT_TILE
    total_rows = sum(g["n"] * g["seq"] for g in groups)
    assert (2 * total_rows) % te == 0
    n_row_tiles = (2 * total_rows) // te
    counts = base_cnt[:, 0].astype(jnp.int32)
    off = jnp.cumsum(counts) - counts
    first_tile = off // te
    n_vis_e = jnp.where(counts > 0, (off + counts - 1) // te - first_tile + 1, 0)
    vis_end = jnp.cumsum(n_vis_e)
    n_vis = vis_end[-1:].astype(jnp.int32)
    v = jnp.arange(n_row_tiles + N_EXPERTS, dtype=jnp.int32)
    vis_expert = _bucket(vis_end, v)
    pick = lambda table: jnp.sum(jnp.where(vis_expert[:, None] == jnp.arange(N_EXPERTS, dtype=jnp.int32)[None, :],
                                           table[None, :], 0), axis=1)
    vis_tile = jnp.clip(pick(first_tile) + v - pick(vis_end - n_vis_e), 0, n_row_tiles - 1).astype(jnp.int32)
    vis_lo = jnp.clip(pick(off) - vis_tile * te, 0, te).astype(jnp.int32)
    vis_hi = jnp.clip(pick(off + counts) - vis_tile * te, 0, te).astype(jnp.int32)

    for g in groups:
        eid = g["route_t"][0:2].astype(jnp.int32)
        off_sel = jnp.sum(jnp.where(eid[None] == jnp.arange(N_EXPERTS, dtype=jnp.int32)[:, None, None],
                                    off[:, None, None], 0), axis=0)
        g["pos"] = (off_sel + g["route_t"][4:6].astype(jnp.int32)).astype(jnp.int32)
    xs = _dispatch([g["pos"] for g in groups], [g["hn"] for g in groups], MOVE_TILE)

    ys = _experts(vis_tile, vis_expert.astype(jnp.int32), vis_lo, vis_hi, n_vis, xs, wgu_bf, wd_bf, te)

    outs = []
    for g in groups:
        rows = g["n"] * g["seq"]
        y = _combine(g["pos"], ys, g["xmid"], g["wcol"], _tile_for(rows, MOVE_TILE))
        outs.append(y.reshape(g["n"], g["seq"], D_MODEL))

    def caches(g):
        kv = g["kv_out"]
        k = kv[:, :, 0:SWA_KV].reshape(g["n"], WINDOW, SWA_KV_HEADS, SWA_HD)[None]
        v = kv[:, :, SWA_KV:2 * SWA_KV].reshape(g["n"], WINDOW, SWA_KV_HEADS, SWA_HD)[None]
        s = g["s_out"].reshape(g["n"], RET_HEADS, RET_DK, RET_DV)[None]
        return s, k, v

    sp, kp, vp = caches(groups[0])
    ss, ks, vs = caches(groups[1])
    return (outs[0], outs[1], sp, kp, vp, ss, ks, vs)
```

```python
import functools

import numpy as np
import jax
import jax.numpy as jnp
from jax import lax
from jax.experimental import pallas as pl
from jax.experimental.pallas import tpu as pltpu

F32 = jnp.float32
BF16 = jnp.bfloat16

D_MODEL = 1024
PAST_LEN = 4096
CHUNK = 64
N_META = 16
RET_HEADS = 4
RET_DK = 64
RET_DV = 128
SWA_HEADS = 8
SWA_KV_HEADS = 2
SWA_HD = 64
WINDOW = 128
ROPE_THETA = 10000.0
N_GROUPS = 4
EXPERTS_PER_GROUP = 8
N_EXPERTS = N_GROUPS * EXPERTS_PER_GROUP
EXPERT_FF = 256
EPS = 1e-6
NEG_INF = -1e30
LOG2E = float(np.log2(np.e))
RET_Q = RET_HEADS * RET_DK
RET_V = RET_HEADS * RET_DV
SWA_Q = SWA_HEADS * SWA_HD
SWA_KV = SWA_KV_HEADS * SWA_HD
MIX_WIDTH = RET_V + SWA_Q
IN_WIDTH = 2 * RET_Q + 2 * RET_V + SWA_Q + 2 * SWA_KV

LANES = 128
PROJ_TILE = 1024
POST_TILE = 512
ATTN_TILE = 256
EXPERT_TILE = 512
MOVE_TILE = 256
SUBLANES = 8
ROUTER_EXPERT_ROW0 = 8
ROUTER_ROWS = 64
META_ROWS = 64
VMEM_LIMIT = 56 * 1024 * 1024

_LOG_G = [float(np.log1p(-np.exp2(-5.0 - h))) for h in range(RET_HEADS)]


def _params(n_axes):
    return pltpu.CompilerParams(dimension_semantics=("arbitrary",) * n_axes, vmem_limit_bytes=VMEM_LIMIT)


def _split_bf16(a):
    hi = a.astype(BF16)
    return hi, (a - hi.astype(F32)).astype(BF16)


def _split_dot(a, w2):
    hi, lo = _split_bf16(a)
    return jnp.dot(jnp.concatenate([hi, lo], axis=1), w2, preferred_element_type=F32)


def _lane_sum(a):
    return _split_dot(a, jnp.ones((2 * LANES, LANES), BF16))


def _head_sum_matrix():
    i = lax.broadcasted_iota(jnp.int32, (2 * LANES, LANES), 0) % LANES
    j = lax.broadcasted_iota(jnp.int32, (2 * LANES, LANES), 1)
    return jnp.where((i < SWA_HD) == (j < SWA_HD), 1.0, 0.0).astype(BF16)


def _rope(t, c, s1, s2):
    half = SWA_HD // 2
    return t * c + pltpu.roll(t, LANES - half, 1) * s1 + pltpu.roll(t, half, 1) * s2


def _head_rms(t, g, head_w):
    ms = _split_dot(t * t, head_w) * (1.0 / SWA_HD)
    return t * lax.rsqrt(ms + EPS) * g


def _row_rms(x):
    n_tiles = x.shape[1] // LANES
    ss = x[:, 0:LANES] * x[:, 0:LANES]
    for j in range(1, n_tiles):
        ss = ss + x[:, j * LANES:(j + 1) * LANES] * x[:, j * LANES:(j + 1) * LANES]
    r = lax.rsqrt(_lane_sum(ss) * (1.0 / x.shape[1]) + EPS)
    return x * jnp.concatenate([r] * n_tiles, axis=1)


def _proj_kernel(x_ref, g1_ref, w_ref, qg_ref, kg_ref, cos_ref, s1_ref, s2_ref,
                 rqk_ref, rv_ref, gate_ref, sq_ref, skv_ref):
    xn = (_row_rms(x_ref[...]) * g1_ref[...]).astype(BF16)
    c, s1, s2 = cos_ref[...], s1_ref[...], s2_ref[...]
    head_w = _head_sum_matrix()

    def seg(a, b):
        return jnp.dot(xn, w_ref[:, a:b], preferred_element_type=F32)

    def tile(h, j):
        return h[:, j * LANES:(j + 1) * LANES]

    h = seg(0, 2 * RET_Q)
    for j in range(2):
        rqk_ref[:, j * LANES:(j + 1) * LANES] = _rope(tile(h, j), c, s1, s2).astype(BF16)
    for j in range(2, 4):
        rqk_ref[:, j * LANES:(j + 1) * LANES] = (_rope(tile(h, j), c, s1, s2) * (RET_DK ** -0.5)).astype(BF16)
    a = 2 * RET_Q
    rv_ref[...] = seg(a, a + RET_V).astype(BF16)
    a += RET_V
    g = seg(a, a + RET_V)
    gate_ref[...] = (g * jax.nn.sigmoid(g)).astype(BF16)
    a += RET_V
    h = seg(a, a + SWA_Q)
    qg = qg_ref[...]
    for j in range(SWA_Q // LANES):
        sq_ref[:, j * LANES:(j + 1) * LANES] = _rope(_head_rms(tile(h, j), qg, head_w), c, s1, s2).astype(BF16)
    a += SWA_Q
    h = seg(a, a + 2 * SWA_KV)
    skv_ref[:, 0:LANES] = _rope(_head_rms(tile(h, 0), kg_ref[...], head_w), c, s1, s2)
    skv_ref[:, LANES:2 * LANES] = tile(h, 1)


def _rope_tables(pos):
    half = SWA_HD // 2
    inv = ROPE_THETA ** (-jnp.arange(half, dtype=F32) / half)
    ang = pos.astype(F32)[:, None] * inv[None, :]
    cos, sin = jnp.cos(ang), jnp.sin(ang)
    z = jnp.zeros_like(sin)
    return (jnp.tile(cos, (1, 4)),
            jnp.tile(jnp.concatenate([-sin, z], axis=1), (1, 2)),
            jnp.tile(jnp.concatenate([z, sin], axis=1), (1, 2)))


def _proj(x2d, pos_rows, tm, g1, w_in_bf, qg2, kg2):
    t_rows = x2d.shape[0]
    n_tiles = t_rows // tm
    n_pos_tiles = pos_rows.shape[0] // tm
    cos, s1, s2 = _rope_tables(pos_rows)
    row = lambda i: (i, 0)
    const = lambda i: (0, 0)
    tab = lambda i: (i % n_pos_tiles, 0)
    return pl.pallas_call(
        _proj_kernel,
        grid=(n_tiles,),
        in_specs=[pl.BlockSpec((tm, D_MODEL), row),
                  pl.BlockSpec((1, D_MODEL), const),
                  pl.BlockSpec((D_MODEL, IN_WIDTH), const),
                  pl.BlockSpec((1, LANES), const),
                  pl.BlockSpec((1, LANES), const),
                  pl.BlockSpec((tm, LANES), tab),
                  pl.BlockSpec((tm, LANES), tab),
                  pl.BlockSpec((tm, LANES), tab)],
        out_specs=[pl.BlockSpec((tm, 2 * RET_Q), row),
                   pl.BlockSpec((tm, RET_V), row),
                   pl.BlockSpec((tm, RET_V), row),
                   pl.BlockSpec((tm, SWA_Q), row),
                   pl.BlockSpec((tm, 2 * SWA_KV), row)],
        out_shape=[jax.ShapeDtypeStruct((t_rows, 2 * RET_Q), BF16),
                   jax.ShapeDtypeStruct((t_rows, RET_V), BF16),
                   jax.ShapeDtypeStruct((t_rows, RET_V), BF16),
                   jax.ShapeDtypeStruct((t_rows, SWA_Q), BF16),
                   jax.ShapeDtypeStruct((t_rows, 2 * SWA_KV), F32)],
        compiler_params=_params(1),
        name="proj",
    )(x2d, g1, w_in_bf, qg2, kg2, cos, s1, s2)


def _pair_update(k_bf, v0_bf, v1_bf, wt):
    kw = (k_bf.astype(F32) * wt).astype(BF16)
    dn = (((0,), (0,)), ((), ()))
    a0 = lax.dot_general(kw, v0_bf, dn, preferred_element_type=F32)
    a1 = lax.dot_general(kw, v1_bf, dn, preferred_element_type=F32)
    top = lax.broadcasted_iota(jnp.int32, a0.shape, 0) < RET_DK
    return jnp.where(top, a0, a1)


def _decay_rows(n, pair, rows_back_from):
    i = lax.broadcasted_iota(jnp.int32, (n, LANES), 0).astype(F32)
    lane = lax.broadcasted_iota(jnp.int32, (n, LANES), 1)
    lg = jnp.where(lane < RET_DK, _LOG_G[2 * pair], _LOG_G[2 * pair + 1])
    return jnp.exp((rows_back_from - i) * lg)


def _meta_state_kernel(rqk_ref, rv_ref, s_ref, *, n_rows):
    for p in range(RET_HEADS // 2):
        k = rqk_ref[:, RET_Q + p * LANES:RET_Q + (p + 1) * LANES]
        wt = _decay_rows(n_rows, p, float(N_META - 1))
        s_ref[p] = _pair_update(k, rv_ref[:, (2 * p) * LANES:(2 * p + 1) * LANES],
                                rv_ref[:, (2 * p + 1) * LANES:(2 * p + 2) * LANES], wt)


def _meta_state(m_rqk, m_rv):
    n_rows = m_rqk.shape[0]
    return pl.pallas_call(
        functools.partial(_meta_state_kernel, n_rows=n_rows),
        out_shape=jax.ShapeDtypeStruct((RET_HEADS // 2, 2 * RET_DK, RET_DV), F32),
        name="meta_state",
    )(m_rqk, m_rv)


def _dup_halves(a, lo_mask):
    sw = pltpu.roll(a, SWA_HD, 1)
    return jnp.where(lo_mask, a, sw), jnp.where(lo_mask, sw, a)


def _attn_kernel(rqk_ref, rv_ref, gate_ref, sq_ref, skv_ref, meta_ref, hist_ref, s0_ref, rng_ref, sink_ref,
                 omix_ref, sout_ref, kvout_ref,
                 s_scr, kd_scr, vd_scr, mk_scr, mv_scr, dec_scr, wt_scr, cs_scr, gam_scr,
                 *, tl, has_hist):
    b = pl.program_id(0)
    t = pl.program_id(1)
    nt = pl.num_programs(1)
    n_chunks = tl // CHUNK
    n_pairs = RET_HEADS // 2
    lo_tl = lax.broadcasted_iota(jnp.int32, (tl, LANES), 1) < SWA_HD
    lo_c = lax.broadcasted_iota(jnp.int32, (CHUNK, LANES), 1) < SWA_HD

    @pl.when((b == 0) & (t == 0))
    def _tables():
        i = lax.broadcasted_iota(jnp.int32, (tl, tl), 0)
        j = lax.broadcasted_iota(jnp.int32, (tl, tl), 1)
        diff = (i - j).astype(F32)
        row = lax.broadcasted_iota(jnp.int32, (tl, LANES), 0).astype(F32)
        for h in range(RET_HEADS):
            dec_scr[h] = jnp.where(diff >= 0.0, jnp.exp(jnp.maximum(diff, 0.0) * _LOG_G[h]), 0.0)
            cs_scr[h] = jnp.exp((row + 1.0) * _LOG_G[h])
        top = lax.broadcasted_iota(jnp.int32, (2 * RET_DK, RET_DV), 0) < RET_DK
        for p in range(n_pairs):
            wt_scr[p] = _decay_rows(tl, p, float(tl - 1))
            gam_scr[p] = jnp.where(top, jnp.exp(jnp.float32(tl * _LOG_G[2 * p])), jnp.exp(jnp.float32(tl * _LOG_G[2 * p + 1])))
        lo_m = lax.broadcasted_iota(jnp.int32, (N_META, LANES), 1) < SWA_HD
        mk0, mk1 = _dup_halves(meta_ref[:, 0:LANES], lo_m)
        mv0, mv1 = _dup_halves(meta_ref[:, LANES:2 * LANES], lo_m)
        mk_scr[...] = jnp.zeros(mk_scr.shape, BF16)
        mv_scr[...] = jnp.zeros(mv_scr.shape, BF16)
        mk_scr[0, 0:N_META] = mk0.astype(BF16)
        mk_scr[1, 0:N_META] = mk1.astype(BF16)
        mv_scr[0, 0:N_META] = mv0.astype(BF16)
        mv_scr[1, 0:N_META] = mv1.astype(BF16)

    @pl.when(t == 0)
    def _stream_start():
        s_scr[...] = s0_ref[0]
        if has_hist:
            lo_w = lax.broadcasted_iota(jnp.int32, (WINDOW, LANES), 1) < SWA_HD
            k0, k1 = _dup_halves(hist_ref[0, :, 0:LANES], lo_w)
            v0, v1 = _dup_halves(hist_ref[0, :, LANES:2 * LANES], lo_w)
            kd_scr[0, 0:WINDOW] = k0.astype(BF16)
            kd_scr[1, 0:WINDOW] = k1.astype(BF16)
            vd_scr[0, 0:WINDOW] = v0.astype(BF16)
            vd_scr[1, 0:WINDOW] = v1.astype(BF16)
        else:
            z = jnp.zeros((WINDOW, LANES), BF16)
            for kv in range(SWA_KV_HEADS):
                kd_scr[kv, 0:WINDOW] = z
                vd_scr[kv, 0:WINDOW] = z

    k0, k1 = _dup_halves(skv_ref[:, 0:LANES], lo_tl)
    v0, v1 = _dup_halves(skv_ref[:, LANES:2 * LANES], lo_tl)
    kd_scr[0, WINDOW:WINDOW + tl] = k0.astype(BF16)
    kd_scr[1, WINDOW:WINDOW + tl] = k1.astype(BF16)
    vd_scr[0, WINDOW:WINDOW + tl] = v0.astype(BF16)
    vd_scr[1, WINDOW:WINDOW + tl] = v1.astype(BF16)

    band = WINDOW + CHUNK
    n_keys = META_ROWS + band
    n_q = 4 * CHUNK
    scale2 = (SWA_HD ** -0.5) * LOG2E
    krow = lax.broadcasted_iota(jnp.int32, (n_keys, n_q), 0)
    zero_c = jnp.zeros((CHUNK, LANES), BF16)
    ones_v = jnp.ones((n_keys, LANES), BF16)
    for c in range(n_chunks):
        if has_hist:
            first_valid = META_ROWS
        else:
            first_valid = jnp.where(t == 0, max(META_ROWS + WINDOW - c * CHUNK, META_ROWS), META_ROWS)
        valid_t = (krow < N_META) | (krow >= first_valid)
        r0 = c * CHUNK
        for kv in range(SWA_KV_HEADS):
            keys = jnp.concatenate([mk_scr[kv], kd_scr[kv, r0:r0 + band]], axis=0)
            vals = jnp.concatenate([mv_scr[kv], vd_scr[kv, r0:r0 + band]], axis=0)
            qa = sq_ref[r0:r0 + CHUNK, (2 * kv) * LANES:(2 * kv + 1) * LANES]
            qb = sq_ref[r0:r0 + CHUNK, (2 * kv + 1) * LANES:(2 * kv + 2) * LANES]
            lhs = jnp.concatenate([jnp.where(lo_c, qa, zero_c), jnp.where(lo_c, zero_c, qa),
                                   jnp.where(lo_c, qb, zero_c), jnp.where(lo_c, zero_c, qb)], axis=0)
            s_t = lax.dot_general(keys, lhs, (((1,), (1,)), ((), ())), preferred_element_type=F32) * scale2
            s_t = jnp.where(valid_t, s_t, NEG_INF)
            s_t = jnp.where(krow == N_META, sink_ref[kv, 0:1, :] * LOG2E, s_t)
            e_t = jnp.exp2(s_t - jnp.max(s_t, axis=0, keepdims=True)).astype(BF16)
            ov = lax.dot_general(e_t, jnp.concatenate([vals, ones_v], axis=1), (((0,), (0,)), ((), ())),
                                 preferred_element_type=F32)
            o = ov[:, 0:LANES] * (1.0 / ov[:, LANES:2 * LANES])
            oa = jnp.where(lo_c, o[0:CHUNK], o[CHUNK:2 * CHUNK])
            ob = jnp.where(lo_c, o[2 * CHUNK:3 * CHUNK], o[3 * CHUNK:4 * CHUNK])
            base = RET_V + (2 * kv) * LANES
            omix_ref[r0:r0 + CHUNK, base:base + LANES] = oa.astype(BF16)
            omix_ref[r0:r0 + CHUNK, base + LANES:base + 2 * LANES] = ob.astype(BF16)

    zero_t = jnp.zeros((tl, LANES), BF16)
    for p in range(n_pairs):
        q = rqk_ref[:, p * LANES:(p + 1) * LANES]
        k = rqk_ref[:, RET_Q + p * LANES:RET_Q + (p + 1) * LANES]
        lhs = jnp.concatenate([jnp.where(lo_tl, q, zero_t), jnp.where(lo_tl, zero_t, q)], axis=0)
        s = lax.dot_general(lhs, k, (((1,), (1,)), ((), ())), preferred_element_type=F32)
        cross = jnp.dot(lhs, s_scr[p].astype(BF16), preferred_element_type=F32)
        for i in range(2):
            h = 2 * p + i
            v = rv_ref[:, h * LANES:(h + 1) * LANES]
            a = (s[i * tl:(i + 1) * tl] * dec_scr[h]).astype(BF16)
            o = jnp.dot(a, v, preferred_element_type=F32) + cross[i * tl:(i + 1) * tl] * cs_scr[h]
            r = o * lax.rsqrt(_lane_sum(o * o) * (1.0 / RET_DV) + EPS) * rng_ref[h:h + 1, :]
            omix_ref[:, h * LANES:(h + 1) * LANES] = (r * gate_ref[:, h * LANES:(h + 1) * LANES].astype(F32)).astype(BF16)
        u = _pair_update(k, rv_ref[:, (2 * p) * LANES:(2 * p + 1) * LANES],
                         rv_ref[:, (2 * p + 1) * LANES:(2 * p + 2) * LANES], wt_scr[p])
        s_scr[p] = gam_scr[p] * s_scr[p] + u

    if tl >= WINDOW:
        @pl.when(t + 1 < nt)
        def _carry_window():
            for kv in range(SWA_KV_HEADS):
                kd_scr[kv, 0:WINDOW] = kd_scr[kv, tl:tl + WINDOW]
                vd_scr[kv, 0:WINDOW] = vd_scr[kv, tl:tl + WINDOW]

    @pl.when(t + 1 == nt)
    def _stream_end():
        sout_ref[0] = s_scr[...]
        if tl >= WINDOW:
            kvout_ref[0] = skv_ref[tl - WINDOW:tl, :]
        else:
            kvout_ref[0, 0:WINDOW - tl] = hist_ref[0, tl:WINDOW, :]
            kvout_ref[0, WINDOW - tl:WINDOW] = skv_ref[...]


def _attention(rqk, rv, gate, sq, skv, meta_kv, hist_kv, s0, rng, sink_tab, *, n_streams, seq, tl, has_hist):
    nt = seq // tl
    assert tl % CHUNK == 0 and seq % tl == 0
    assert tl >= WINDOW or (nt == 1 and has_hist)
    n_pairs = RET_HEADS // 2
    s0_shared = s0.shape[0] == 1
    row = lambda b, t: (b * nt + t, 0)
    const2 = lambda b, t: (0, 0)
    const3 = lambda b, t: (0, 0, 0)
    per_b3 = lambda b, t: (b, 0, 0)
    s0_map = (lambda b, t: (0, 0, 0, 0)) if s0_shared else (lambda b, t: (b, 0, 0, 0))
    hist_map = per_b3 if has_hist else const3
    rows = n_streams * seq
    return pl.pallas_call(
        functools.partial(_attn_kernel, tl=tl, has_hist=has_hist),
        grid=(n_streams, nt),
        in_specs=[pl.BlockSpec((tl, 2 * RET_Q), row),
                  pl.BlockSpec((tl, RET_V), row),
                  pl.BlockSpec((tl, RET_V), row),
                  pl.BlockSpec((tl, SWA_Q), row),
                  pl.BlockSpec((tl, 2 * SWA_KV), row),
                  pl.BlockSpec((N_META, 2 * SWA_KV), const2),
                  pl.BlockSpec((1, WINDOW, 2 * SWA_KV), hist_map),
                  pl.BlockSpec((1, n_pairs, 2 * RET_DK, RET_DV), s0_map),
                  pl.BlockSpec((RET_HEADS, RET_DV), const2),
                  pl.BlockSpec((SWA_KV_HEADS, SUBLANES, 4 * CHUNK), const3)],
        out_specs=[pl.BlockSpec((tl, MIX_WIDTH), row),
                   pl.BlockSpec((1, n_pairs, 2 * RET_DK, RET_DV), lambda b, t: (b, 0, 0, 0)),
                   pl.BlockSpec((1, WINDOW, 2 * SWA_KV), per_b3)],
        out_shape=[jax.ShapeDtypeStruct((rows, MIX_WIDTH), BF16),
                   jax.ShapeDtypeStruct((n_streams, n_pairs, 2 * RET_DK, RET_DV), F32),
                   jax.ShapeDtypeStruct((n_streams, WINDOW, 2 * SWA_KV), F32)],
        scratch_shapes=[pltpu.VMEM((n_pairs, 2 * RET_DK, RET_DV), F32),
                        pltpu.VMEM((SWA_KV_HEADS, WINDOW + tl, LANES), BF16),
                        pltpu.VMEM((SWA_KV_HEADS, WINDOW + tl, LANES), BF16),
                        pltpu.VMEM((SWA_KV_HEADS, META_ROWS, LANES), BF16),
                        pltpu.VMEM((SWA_KV_HEADS, META_ROWS, LANES), BF16),
                        pltpu.VMEM((RET_HEADS, tl, tl), F32),
                        pltpu.VMEM((n_pairs, tl, LANES), F32),
                        pltpu.VMEM((RET_HEADS, tl, RET_DV), F32),
                        pltpu.VMEM((n_pairs, 2 * RET_DK, RET_DV), F32)],
        compiler_params=_params(2),
        name="attention",
    )(rqk, rv, gate, sq, skv, meta_kv, hist_kv, s0, rng, sink_tab)


def _post_kernel(omix_ref, x_ref, wout_ref, g2_ref, wrt_ref, base_ref,
                 xmid_ref, hn_ref, wcol_ref, rt_ref, cnt_ref, tri_scr, run_scr):
    i = pl.program_id(0)
    tm = x_ref.shape[0]

    @pl.when(i == 0)
    def _init():
        r = lax.broadcasted_iota(jnp.int32, (tm, tm), 0)
        c = lax.broadcasted_iota(jnp.int32, (tm, tm), 1)
        tri_scr[...] = jnp.where(r < c, 1.0, 0.0).astype(BF16)
        run_scr[...] = base_ref[...]

    xm = x_ref[...] + jnp.dot(omix_ref[...], wout_ref[...], preferred_element_type=F32)
    xmid_ref[...] = xm
    hn = _row_rms(xm) * g2_ref[...]
    hn_ref[...] = hn
    lt = lax.dot_general(wrt_ref[...], hn.astype(BF16), (((1,), (1,)), ((), ())), preferred_element_type=F32)
    row8 = lax.broadcasted_iota(jnp.int32, (SUBLANES, tm), 0)
    big = jnp.int32(SUBLANES)
    gl = jnp.where(row8 < N_GROUPS, lt[0:SUBLANES], NEG_INF)
    gmax = jnp.max(gl, axis=0, keepdims=True)
    gsum = jnp.sum(jnp.exp(gl - gmax), axis=0, keepdims=True)
    g_sel = jnp.min(jnp.where(gl == gmax, row8, big), axis=0, keepdims=True)
    p_sel = 1.0 / gsum
    el = lt[ROUTER_EXPERT_ROW0:ROUTER_EXPERT_ROW0 + EXPERTS_PER_GROUP]
    for g in range(1, N_GROUPS):
        lo = ROUTER_EXPERT_ROW0 + g * EXPERTS_PER_GROUP
        el = jnp.where(g_sel == g, lt[lo:lo + EXPERTS_PER_GROUP], el)
    m1 = jnp.max(el, axis=0, keepdims=True)
    i1 = jnp.min(jnp.where(el == m1, row8, big), axis=0, keepdims=True)
    el2 = jnp.where(row8 == i1, NEG_INF, el)
    m2 = jnp.max(el2, axis=0, keepdims=True)
    i2 = jnp.min(jnp.where(el2 == m2, row8, big), axis=0, keepdims=True)
    e2 = jnp.exp(m2 - m1)
    inv = 1.0 / (1.0 + e2)
    w1 = p_sel * inv
    w2 = p_sel * (e2 * inv)
    eid1 = g_sel * EXPERTS_PER_GROUP + i1
    eid2 = g_sel * EXPERTS_PER_GROUP + i2

    rowe = lax.broadcasted_iota(jnp.int32, (N_EXPERTS, tm), 0)
    oh = jnp.where((rowe == eid1) | (rowe == eid2), 1.0, 0.0).astype(BF16)
    run = run_scr[...]
    pref = jnp.dot(oh, tri_scr[...], preferred_element_type=F32) + jnp.concatenate([run] * (tm // LANES), axis=1)
    r1 = jnp.sum(jnp.where(rowe == eid1, pref, 0.0), axis=0, keepdims=True)
    r2 = jnp.sum(jnp.where(rowe == eid2, pref, 0.0), axis=0, keepdims=True)
    run = run + jnp.dot(oh, jnp.ones((tm, LANES), BF16), preferred_element_type=F32)
    run_scr[...] = run
    cnt_ref[...] = run

    out = jnp.zeros((SUBLANES, tm), F32)
    for k, v in enumerate([eid1.astype(F32), eid2.astype(F32), w1, w2, r1, r2]):
        out = jnp.where(row8 == k, v, out)
    rt_ref[...] = out
    rowl = lax.broadcasted_iota(jnp.int32, (LANES, tm), 0)
    wcol_ref[...] = jnp.where(rowl == 0, w1, jnp.where(rowl == 1, w2, 0.0)).T


def _post(omix, x2d, w_out_bf, g2, w_router_t_bf, base_cnt, tm):
    t_rows = x2d.shape[0]
    assert tm % LANES == 0
    row = lambda i: (i, 0)
    const = lambda i: (0, 0)
    return pl.pallas_call(
        _post_kernel,
        grid=(t_rows // tm,),
        in_specs=[pl.BlockSpec((tm, MIX_WIDTH), row),
                  pl.BlockSpec((tm, D_MODEL), row),
                  pl.BlockSpec((MIX_WIDTH, D_MODEL), const),
                  pl.BlockSpec((1, D_MODEL), const),
                  pl.BlockSpec((ROUTER_ROWS, D_MODEL), const),
                  pl.BlockSpec((N_EXPERTS, LANES), const)],
        out_specs=[pl.BlockSpec((tm, D_MODEL), row),
                   pl.BlockSpec((tm, D_MODEL), row),
                   pl.BlockSpec((tm, LANES), row),
                   pl.BlockSpec((SUBLANES, tm), lambda i: (0, i)),
                   pl.BlockSpec((N_EXPERTS, LANES), const)],
        out_shape=[jax.ShapeDtypeStruct((t_rows, D_MODEL), F32),
                   jax.ShapeDtypeStruct((t_rows, D_MODEL), F32),
                   jax.ShapeDtypeStruct((t_rows, LANES), F32),
                   jax.ShapeDtypeStruct((SUBLANES, t_rows), F32),
                   jax.ShapeDtypeStruct((N_EXPERTS, LANES), F32)],
        scratch_shapes=[pltpu.VMEM((tm, tm), BF16), pltpu.VMEM((N_EXPERTS, LANES), F32)],
        compiler_params=_params(1),
        name="post",
    )(omix, x2d, w_out_bf, g2, w_router_t_bf, base_cnt)


def _step_major(pos, tm):
    return [pos[e].reshape(-1, 1, tm) for e in range(2)]


def _dispatch_kernel(pos0_ref, pos1_ref, *refs, tm, group_steps):
    pos_refs = (pos0_ref, pos1_ref)
    hn_refs = refs[:len(group_steps)]
    xs_ref, sem = refs[len(group_steps):]
    i = pl.program_id(0)

    first = 0
    for hn_ref, steps in zip(hn_refs, group_steps):
        @pl.when((i >= first) & (i < first + steps))
        def _rows(hn_ref=hn_ref):
            def body(blk, carry):
                for k in range(SUBLANES):
                    r = blk * SUBLANES + k
                    for e in range(2):
                        pltpu.make_async_copy(hn_ref.at[blk, pl.ds(k, 1)], xs_ref.at[pl.ds(pos_refs[e][0, 0, r], 1)],
                                              sem).start(priority=e)
                return carry

            lax.fori_loop(0, tm // SUBLANES, body, 0)
        first += steps

    pltpu.make_async_copy(xs_ref.at[pl.ds(0, 2 * tm)], xs_ref.at[pl.ds(0, 2 * tm)], sem).wait()


def _dispatch(pos_list, hn_list, tm):
    group_steps = tuple(hn.shape[0] // tm for hn in hn_list)
    n_steps = sum(group_steps)
    assert tm % SUBLANES == 0
    pos3 = [jnp.concatenate(parts, axis=0) for parts in zip(*[_step_major(p, tm) for p in pos_list])]
    smem = pl.BlockSpec((1, 1, tm), lambda i: (i, 0, 0), memory_space=pltpu.SMEM)
    hn_specs = []
    first = 0
    for steps in group_steps:
        hn_specs.append(pl.BlockSpec((tm // SUBLANES, SUBLANES, D_MODEL),
                                     lambda i, first=first, steps=steps: (jnp.clip(i - first, 0, steps - 1), 0, 0)))
        first += steps
    n_rows_out = 2 * sum(hn.shape[0] for hn in hn_list)
    hn_list = [hn.reshape(hn.shape[0] // SUBLANES, SUBLANES, D_MODEL) for hn in hn_list]
    return pl.pallas_call(
        functools.partial(_dispatch_kernel, tm=tm, group_steps=group_steps),
        grid=(n_steps,),
        in_specs=[smem, smem] + hn_specs,
        out_specs=pl.BlockSpec(memory_space=pl.ANY),
        out_shape=jax.ShapeDtypeStruct((n_rows_out, D_MODEL), F32),
        scratch_shapes=[pltpu.SemaphoreType.DMA(())],
        compiler_params=_params(1),
        name="dispatch",
    )(*pos3, *hn_list)


def _expert_kernel(vt_ref, ve_ref, lo_ref, hi_ref, nv_ref, x_ref, wgu_ref, wd_ref, y_ref):
    v = pl.program_id(0)

    @pl.when(v < nv_ref[0])
    def _compute():
        x = x_ref[...].astype(BF16)
        gu = jnp.dot(x, wgu_ref[0], preferred_element_type=F32)
        g = gu[:, 0:EXPERT_FF]
        a = (g * jax.nn.sigmoid(g) * gu[:, EXPERT_FF:2 * EXPERT_FF]).astype(BF16)
        y = jnp.dot(a, wd_ref[0], preferred_element_type=F32)
        lo, hi = lo_ref[v], hi_ref[v]

        @pl.when(lo == 0)
        def _first_visit():
            y_ref[...] = y

        @pl.when(lo > 0)
        def _later_visit():
            row = lax.broadcasted_iota(jnp.int32, y.shape, 0)
            y_ref[...] = jnp.where((row >= lo) & (row < hi), y, y_ref[...])


def _experts(vis_tile, vis_expert, vis_lo, vis_hi, n_vis, xs, wgu_bf, wd_bf, tm):
    n_steps = vis_tile.shape[0]
    last = lambda v, nv: jnp.minimum(v, nv[0] - 1)
    return pl.pallas_call(
        _expert_kernel,
        grid_spec=pltpu.PrefetchScalarGridSpec(
            num_scalar_prefetch=5,
            grid=(n_steps,),
            in_specs=[pl.BlockSpec((tm, D_MODEL), lambda v, vt, ve, lo, hi, nv: (vt[last(v, nv)], 0)),
                      pl.BlockSpec((1, D_MODEL, 2 * EXPERT_FF), lambda v, vt, ve, lo, hi, nv: (ve[last(v, nv)], 0, 0)),
                      pl.BlockSpec((1, EXPERT_FF, D_MODEL), lambda v, vt, ve, lo, hi, nv: (ve[last(v, nv)], 0, 0))],
            out_specs=pl.BlockSpec((tm, D_MODEL), lambda v, vt, ve, lo, hi, nv: (vt[last(v, nv)], 0))),
        out_shape=jax.ShapeDtypeStruct(xs.shape, F32),
        compiler_params=_params(1),
        name="experts",
    )(vis_tile, vis_expert, vis_lo, vis_hi, n_vis, xs, wgu_bf, wd_bf)


def _combine_kernel(pos0_ref, pos1_ref, nxt0_ref, nxt1_ref, ys_ref, xmid_ref, wcol_ref, out_ref, ybuf, sems,
                    *, tm, n_steps):
    i = pl.program_id(0)
    slot = i % 2

    def issue(p_refs, to_slot):
        def body(blk, carry):
            for k in range(SUBLANES):
                r = blk * SUBLANES + k
                for e in range(2):
                    pltpu.make_async_copy(ys_ref.at[pl.ds(p_refs[e][0, 0, r], 1)],
                                          ybuf.at[to_slot, e, blk, pl.ds(k, 1)], sems.at[to_slot]).start(priority=e)
            return carry

        lax.fori_loop(0, tm // SUBLANES, body, 0)

    @pl.when(i == 0)
    def _first():
        issue((pos0_ref, pos1_ref), 0)

    @pl.when(i + 1 < n_steps)
    def _ahead():
        issue((nxt0_ref, nxt1_ref), 1 - slot)

    for e in range(2):
        pltpu.make_async_copy(ybuf.at[slot, e], ybuf.at[slot, e], sems.at[slot]).wait()
    w = wcol_ref[...]
    y0 = ybuf[slot, 0].reshape(tm, D_MODEL)
    y1 = ybuf[slot, 1].reshape(tm, D_MODEL)
    out_ref[...] = xmid_ref[...] + w[:, 0:1] * y0 + w[:, 1:2] * y1


def _combine(pos, ys, xmid, wcol, tm):
    t_rows = xmid.shape[0]
    n_steps = t_rows // tm
    assert tm % SUBLANES == 0
    pos3 = _step_major(pos, tm)
    row = lambda i: (i, 0)
    cur = pl.BlockSpec((1, 1, tm), lambda i: (i, 0, 0), memory_space=pltpu.SMEM)
    nxt = pl.BlockSpec((1, 1, tm), lambda i: (jnp.minimum(i + 1, n_steps - 1), 0, 0), memory_space=pltpu.SMEM)
    return pl.pallas_call(
        functools.partial(_combine_kernel, tm=tm, n_steps=n_steps),
        grid=(n_steps,),
        in_specs=[cur, cur, nxt, nxt,
                  pl.BlockSpec(memory_space=pl.ANY),
                  pl.BlockSpec((tm, D_MODEL), row),
                  pl.BlockSpec((tm, LANES), row)],
        out_specs=pl.BlockSpec((tm, D_MODEL), row),
        out_shape=jax.ShapeDtypeStruct((t_rows, D_MODEL), F32),
        scratch_shapes=[pltpu.VMEM((2, 2, tm // SUBLANES, SUBLANES, D_MODEL), F32), pltpu.SemaphoreType.DMA((2,))],
        compiler_params=_params(1),
        name="combine",
    )(*pos3, *pos3, ys, xmid, wcol)


def _bucket(ends, idx):
    n = jnp.sum((ends[None, :] <= idx[:, None]).astype(jnp.int32), axis=1)
    return jnp.minimum(n, ends.shape[0] - 1)


def _tile_for(rows, pref):
    tm = min(pref, rows)
    assert rows % tm == 0
    return tm


def kernel(x_prompt, x_sample, cache_ret_state, cache_swa_k, cache_swa_v, meta_tokens, norm1_g, w_in, q_norm_g,
           k_norm_g, ret_norm_g, attn_sinks, w_out, norm2_g, w_group, w_expert, w_gate, w_up, w_down):
    assert norm1_g.shape[0] == 1, "single-layer trunk"
    bp, lp, _ = x_prompt.shape
    bs, ls, _ = x_sample.shape
    n_pairs = RET_HEADS // 2

    g1 = norm1_g[0][None, :]
    g2 = norm2_g[0][None, :]
    w_in_bf = w_in[0].astype(BF16)
    w_out_bf = w_out[0].astype(BF16)
    qg2 = jnp.tile(q_norm_g[0], 2)[None, :]
    kg2 = jnp.tile(k_norm_g[0], 2)[None, :]
    rng = ret_norm_g[0].reshape(RET_HEADS, RET_DV)
    sink_tab = jnp.broadcast_to(jnp.repeat(attn_sinks[0], CHUNK).reshape(SWA_KV_HEADS, 1, 4 * CHUNK),
                                (SWA_KV_HEADS, SUBLANES, 4 * CHUNK))
    w_router_t = jnp.zeros((ROUTER_ROWS, D_MODEL), F32)
    w_router_t = w_router_t.at[0:N_GROUPS].set(w_group[0].T)
    w_router_t = w_router_t.at[ROUTER_EXPERT_ROW0:ROUTER_EXPERT_ROW0 + N_EXPERTS].set(w_expert[0].T)
    w_router_t_bf = w_router_t.astype(BF16)
    wgu_bf = jnp.concatenate([w_gate[0], w_up[0]], axis=-1).astype(BF16)
    wd_bf = w_down[0].astype(BF16)

    meta_rows = 2 * CHUNK
    m_pad = jnp.zeros((meta_rows, D_MODEL), F32).at[0:N_META].set(meta_tokens)
    m_rqk, m_rv, _, _, m_skv = _proj(m_pad, jnp.arange(meta_rows, dtype=jnp.int32), meta_rows, g1, w_in_bf, qg2, kg2)
    s_meta = _meta_state(m_rqk, m_rv)[None]
    meta_kv = m_skv[0:N_META]

    groups = [
        dict(x=x_prompt.reshape(bp * lp, D_MODEL), n=bp, seq=lp, pos0=N_META, has_hist=False, s0=s_meta,
             hist=jnp.zeros((1, WINDOW, 2 * SWA_KV), F32)),
        dict(x=x_sample.reshape(bs * ls, D_MODEL), n=bs, seq=ls, pos0=N_META + PAST_LEN, has_hist=True,
             s0=cache_ret_state[0].reshape(bs, n_pairs, 2 * RET_DK, RET_DV),
             hist=jnp.concatenate([cache_swa_k[0].reshape(bs, WINDOW, SWA_KV),
                                   cache_swa_v[0].reshape(bs, WINDOW, SWA_KV)], axis=-1)),
    ]

    base_cnt = jnp.zeros((N_EXPERTS, LANES), F32)
    for g in groups:
        rows = g["n"] * g["seq"]
        tm = _tile_for(rows, PROJ_TILE)
        pos = g["pos0"] + jnp.arange(g["seq"], dtype=jnp.int32)
        if g["seq"] < tm:
            assert tm % g["seq"] == 0
            pos = jnp.tile(pos, tm // g["seq"])
        else:
            assert g["seq"] % tm == 0
        rqk, rv, gate, sq, skv = _proj(g["x"], pos, tm, g1, w_in_bf, qg2, kg2)
        tl = min(ATTN_TILE, g["seq"])
        omix, s_out, kv_out = _attention(rqk, rv, gate, sq, skv, meta_kv, g["hist"], g["s0"], rng, sink_tab,
                                         n_streams=g["n"], seq=g["seq"], tl=tl, has_hist=g["has_hist"])
        xmid, hn, wcol, route_t, base_cnt = _post(omix, g["x"], w_out_bf, g2, w_router_t_bf, base_cnt,
                                                  _tile_for(rows, POST_TILE))
        g.update(xmid=xmid, hn=hn, wcol=wcol, route_t=route_t, s_out=s_out, kv_out=kv_out)

    te = EXPERT_TILE
    total_rows = sum(g["n"] * g["seq"] for g in groups)
    assert (2 * total_rows) % te == 0
    n_row_tiles = (2 * total_rows) // te
    counts = base_cnt[:, 0].astype(jnp.int32)
    off = jnp.cumsum(counts) - counts
    first_tile = off // te
    n_vis_e = jnp.where(counts > 0, (off + counts - 1) // te - first_tile + 1, 0)
    vis_end = jnp.cumsum(n_vis_e)
    n_vis = vis_end[-1:].astype(jnp.int32)
    v = jnp.arange(n_row_tiles + N_EXPERTS, dtype=jnp.int32)
    vis_expert = _bucket(vis_end, v)
    pick = lambda table: jnp.sum(jnp.where(vis_expert[:, None] == jnp.arange(N_EXPERTS, dtype=jnp.int32)[None, :],
                                           table[None, :], 0), axis=1)
    vis_tile = jnp.clip(pick(first_tile) + v - pick(vis_end - n_vis_e), 0, n_row_tiles - 1).astype(jnp.int32)
    vis_lo = jnp.clip(pick(off) - vis_tile * te, 0, te).astype(jnp.int32)
    vis_hi = jnp.clip(pick(off + counts) - vis_tile * te, 0, te).astype(jnp.int32)

    for g in groups:
        eid = g["route_t"][0:2].astype(jnp.int32)
        off_sel = jnp.sum(jnp.where(eid[None] == jnp.arange(N_EXPERTS, dtype=jnp.int32)[:, None, None],
                                    off[:, None, None], 0), axis=0)
        g["pos"] = (off_sel + g["route_t"][4:6].astype(jnp.int32)).astype(jnp.int32)
    xs = _dispatch([g["pos"] for g in groups], [g["hn"] for g in groups], MOVE_TILE)

    ys = _experts(vis_tile, vis_expert.astype(jnp.int32), vis_lo, vis_hi, n_vis, xs, wgu_bf, wd_bf, te)

    outs = []
    for g in groups:
        rows = g["n"] * g["seq"]
        y = _combine(g["pos"], ys, g["xmid"], g["wcol"], _tile_for(rows, MOVE_TILE))
        outs.append(y.reshape(g["n"], g["seq"], D_MODEL))

    def caches(g):
        kv = g["kv_out"]
        k = kv[:, :, 0:SWA_KV].reshape(g["n"], WINDOW, SWA_KV_HEADS, SWA_HD)[None]
        v = kv[:, :, SWA_KV:2 * SWA_KV].reshape(g["n"], WINDOW, SWA_KV_HEADS, SWA_HD)[None]
        s = g["s_out"].reshape(g["n"], RET_HEADS, RET_DK, RET_DV)[None]
        return s, k, v

    sp, kp, vp = caches(groups[0])
    ss, ks, vs = caches(groups[1])
    return (outs[0], outs[1], sp, kp, vp, ss, ks, vs)
```

```python
import functools

import numpy as np
import jax
import jax.numpy as jnp
from jax import lax
from jax.experimental import pallas as pl
from jax.experimental.pallas import tpu as pltpu

F32 = jnp.float32
BF16 = jnp.bfloat16

D_MODEL = 1024
PAST_LEN = 4096
CHUNK = 64
N_META = 16
RET_HEADS = 4
RET_DK = 64
RET_DV = 128
SWA_HEADS = 8
SWA_KV_HEADS = 2
SWA_HD = 64
WINDOW = 128
ROPE_THETA = 10000.0
N_GROUPS = 4
EXPERTS_PER_GROUP = 8
N_EXPERTS = N_GROUPS * EXPERTS_PER_GROUP
EXPERT_FF = 256
EPS = 1e-6
NEG_INF = -1e30
LOG2E = float(np.log2(np.e))
RET_Q = RET_HEADS * RET_DK
RET_V = RET_HEADS * RET_DV
SWA_Q = SWA_HEADS * SWA_HD
SWA_KV = SWA_KV_HEADS * SWA_HD
MIX_WIDTH = RET_V + SWA_Q
IN_WIDTH = 2 * RET_Q + 2 * RET_V + SWA_Q + 2 * SWA_KV

LANES = 128
PROJ_TILE = 1024
POST_TILE = 512
ATTN_TILE = 256
EXPERT_TILE = 512
MOVE_TILE = 256
SUBLANES = 8
ROUTER_EXPERT_ROW0 = 8
ROUTER_ROWS = 64
META_ROWS = 64
VMEM_LIMIT = 56 * 1024 * 1024

_LOG_G = [float(np.log1p(-np.exp2(-5.0 - h))) for h in range(RET_HEADS)]


def _params(n_axes):
    return pltpu.CompilerParams(dimension_semantics=("arbitrary",) * n_axes, vmem_limit_bytes=VMEM_LIMIT)


def _split_bf16(a):
    hi = a.astype(BF16)
    return hi, (a - hi.astype(F32)).astype(BF16)


def _split_dot(a, w2):
    hi, lo = _split_bf16(a)
    return jnp.dot(jnp.concatenate([hi, lo], axis=1), w2, preferred_element_type=F32)


def _lane_sum(a):
    return _split_dot(a, jnp.ones((2 * LANES, LANES), BF16))


def _head_sum_matrix():
    i = lax.broadcasted_iota(jnp.int32, (2 * LANES, LANES), 0) % LANES
    j = lax.broadcasted_iota(jnp.int32, (2 * LANES, LANES), 1)
    return jnp.where((i < SWA_HD) == (j < SWA_HD), 1.0, 0.0).astype(BF16)


def _rope(t, c, s1, s2):
    half = SWA_HD // 2
    return t * c + pltpu.roll(t, LANES - half, 1) * s1 + pltpu.roll(t, half, 1) * s2


def _head_rms(t, g, head_w):
    ms = _split_dot(t * t, head_w) * (1.0 / SWA_HD)
    return t * lax.rsqrt(ms + EPS) * g


def _row_rms(x):
    n_tiles = x.shape[1] // LANES
    ss = x[:, 0:LANES] * x[:, 0:LANES]
    for j in range(1, n_tiles):
        ss = ss + x[:, j * LANES:(j + 1) * LANES] * x[:, j * LANES:(j + 1) * LANES]
    r = lax.rsqrt(_lane_sum(ss) * (1.0 / x.shape[1]) + EPS)
    return x * jnp.concatenate([r] * n_tiles, axis=1)


def _proj_kernel(x_ref, g1_ref, w_ref, qg_ref, kg_ref, cos_ref, s1_ref, s2_ref,
                 rqk_ref, rv_ref, gate_ref, sq_ref, skv_ref):
    xn = (_row_rms(x_ref[...]) * g1_ref[...]).astype(BF16)
    c, s1, s2 = cos_ref[...], s1_ref[...], s2_ref[...]
    head_w = _head_sum_matrix()

    def seg(a, b):
        return jnp.dot(xn, w_ref[:, a:b], preferred_element_type=F32)

    def tile(h, j):
        return h[:, j * LANES:(j + 1) * LANES]

    h = seg(0, 2 * RET_Q)
    for j in range(2):
        rqk_ref[:, j * LANES:(j + 1) * LANES] = _rope(tile(h, j), c, s1, s2).astype(BF16)
    for j in range(2, 4):
        rqk_ref[:, j * LANES:(j + 1) * LANES] = (_rope(tile(h, j), c, s1, s2) * (RET_DK ** -0.5)).astype(BF16)
    a = 2 * RET_Q
    rv_ref[...] = seg(a, a + RET_V).astype(BF16)
    a += RET_V
    g = seg(a, a + RET_V)
    gate_ref[...] = (g * jax.nn.sigmoid(g)).astype(BF16)
    a += RET_V
    h = seg(a, a + SWA_Q)
    qg = qg_ref[...]
    for j in range(SWA_Q // LANES):
        sq_ref[:, j * LANES:(j + 1) * LANES] = _rope(_head_rms(tile(h, j), qg, head_w), c, s1, s2).astype(BF16)
    a += SWA_Q
    h = seg(a, a + 2 * SWA_KV)
    skv_ref[:, 0:LANES] = _rope(_head_rms(tile(h, 0), kg_ref[...], head_w), c, s1, s2)
    skv_ref[:, LANES:2 * LANES] = tile(h, 1)


def _rope_tables(pos):
    half = SWA_HD // 2
    inv = ROPE_THETA ** (-jnp.arange(half, dtype=F32) / half)
    ang = pos.astype(F32)[:, None] * inv[None, :]
    cos, sin = jnp.cos(ang), jnp.sin(ang)
    z = jnp.zeros_like(sin)
    return (jnp.tile(cos, (1, 4)),
            jnp.tile(jnp.concatenate([-sin, z], axis=1), (1, 2)),
            jnp.tile(jnp.concatenate([z, sin], axis=1), (1, 2)))


def _proj(x2d, pos_rows, tm, g1, w_in_bf, qg2, kg2):
    t_rows = x2d.shape[0]
    n_tiles = t_rows // tm
    n_pos_tiles = pos_rows.shape[0] // tm
    cos, s1, s2 = _rope_tables(pos_rows)
    row = lambda i: (i, 0)
    const = lambda i: (0, 0)
    tab = lambda i: (i % n_pos_tiles, 0)
    return pl.pallas_call(
        _proj_kernel,
        grid=(n_tiles,),
        in_specs=[pl.BlockSpec((tm, D_MODEL), row),
                  pl.BlockSpec((1, D_MODEL), const),
                  pl.BlockSpec((D_MODEL, IN_WIDTH), const),
                  pl.BlockSpec((1, LANES), const),
                  pl.BlockSpec((1, LANES), const),
                  pl.BlockSpec((tm, LANES), tab),
                  pl.BlockSpec((tm, LANES), tab),
                  pl.BlockSpec((tm, LANES), tab)],
        out_specs=[pl.BlockSpec((tm, 2 * RET_Q), row),
                   pl.BlockSpec((tm, RET_V), row),
                   pl.BlockSpec((tm, RET_V), row),
                   pl.BlockSpec((tm, SWA_Q), row),
                   pl.BlockSpec((tm, 2 * SWA_KV), row)],
        out_shape=[jax.ShapeDtypeStruct((t_rows, 2 * RET_Q), BF16),
                   jax.ShapeDtypeStruct((t_rows, RET_V), BF16),
                   jax.ShapeDtypeStruct((t_rows, RET_V), BF16),
                   jax.ShapeDtypeStruct((t_rows, SWA_Q), BF16),
                   jax.ShapeDtypeStruct((t_rows, 2 * SWA_KV), F32)],
        compiler_params=_params(1),
        name="proj",
    )(x2d, g1, w_in_bf, qg2, kg2, cos, s1, s2)


def _pair_update(k_bf, v0_bf, v1_bf, wt):
    kw = (k_bf.astype(F32) * wt).astype(BF16)
    dn = (((0,), (0,)), ((), ()))
    a0 = lax.dot_general(kw, v0_bf, dn, preferred_element_type=F32)
    a1 = lax.dot_general(kw, v1_bf, dn, preferred_element_type=F32)
    top = lax.broadcasted_iota(jnp.int32, a0.shape, 0) < RET_DK
    return jnp.where(top, a0, a1)


def _decay_rows(n, pair, rows_back_from):
    i = lax.broadcasted_iota(jnp.int32, (n, LANES), 0).astype(F32)
    lane = lax.broadcasted_iota(jnp.int32, (n, LANES), 1)
    lg = jnp.where(lane < RET_DK, _LOG_G[2 * pair], _LOG_G[2 * pair + 1])
    return jnp.exp((rows_back_from - i) * lg)


def _meta_state_kernel(rqk_ref, rv_ref, s_ref, *, n_rows):
    for p in range(RET_HEADS // 2):
        k = rqk_ref[:, RET_Q + p * LANES:RET_Q + (p + 1) * LANES]
        wt = _decay_rows(n_rows, p, float(N_META - 1))
        s_ref[p] = _pair_update(k, rv_ref[:, (2 * p) * LANES:(2 * p + 1) * LANES],
                                rv_ref[:, (2 * p + 1) * LANES:(2 * p + 2) * LANES], wt)


def _meta_state(m_rqk, m_rv):
    n_rows = m_rqk.shape[0]
    return pl.pallas_call(
        functools.partial(_meta_state_kernel, n_rows=n_rows),
        out_shape=jax.ShapeDtypeStruct((RET_HEADS // 2, 2 * RET_DK, RET_DV), F32),
        name="meta_state",
    )(m_rqk, m_rv)


def _dup_halves(a, lo_mask):
    sw = pltpu.roll(a, SWA_HD, 1)
    return jnp.where(lo_mask, a, sw), jnp.where(lo_mask, sw, a)


def _attn_kernel(rqk_ref, rv_ref, gate_ref, sq_ref, skv_ref, meta_ref, hist_ref, s0_ref, rng_ref, sink_ref,
                 omix_ref, sout_ref, kvout_ref,
                 s_scr, kd_scr, vd_scr, mk_scr, mv_scr, dec_scr, wt_scr, cs_scr, gam_scr,
                 *, tl, has_hist):
    b = pl.program_id(0)
    t = pl.program_id(1)
    nt = pl.num_programs(1)
    n_chunks = tl // CHUNK
    n_pairs = RET_HEADS // 2
    lo_tl = lax.broadcasted_iota(jnp.int32, (tl, LANES), 1) < SWA_HD
    lo_c = lax.broadcasted_iota(jnp.int32, (CHUNK, LANES), 1) < SWA_HD

    @pl.when((b == 0) & (t == 0))
    def _tables():
        i = lax.broadcasted_iota(jnp.int32, (tl, tl), 0)
        j = lax.broadcasted_iota(jnp.int32, (tl, tl), 1)
        diff = (i - j).astype(F32)
        row = lax.broadcasted_iota(jnp.int32, (tl, LANES), 0).astype(F32)
        for h in range(RET_HEADS):
            dec_scr[h] = jnp.where(diff >= 0.0, jnp.exp(jnp.maximum(diff, 0.0) * _LOG_G[h]), 0.0)
            cs_scr[h] = jnp.exp((row + 1.0) * _LOG_G[h])
        top = lax.broadcasted_iota(jnp.int32, (2 * RET_DK, RET_DV), 0) < RET_DK
        for p in range(n_pairs):
            wt_scr[p] = _decay_rows(tl, p, float(tl - 1))
            gam_scr[p] = jnp.where(top, jnp.exp(jnp.float32(tl * _LOG_G[2 * p])), jnp.exp(jnp.float32(tl * _LOG_G[2 * p + 1])))
        lo_m = lax.broadcasted_iota(jnp.int32, (N_META, LANES), 1) < SWA_HD
        mk0, mk1 = _dup_halves(meta_ref[:, 0:LANES], lo_m)
        mv0, mv1 = _dup_halves(meta_ref[:, LANES:2 * LANES], lo_m)
        mk_scr[...] = jnp.zeros(mk_scr.shape, BF16)
        mv_scr[...] = jnp.zeros(mv_scr.shape, BF16)
        mk_scr[0, 0:N_META] = mk0.astype(BF16)
        mk_scr[1, 0:N_META] = mk1.astype(BF16)
        mv_scr[0, 0:N_META] = mv0.astype(BF16)
        mv_scr[1, 0:N_META] = mv1.astype(BF16)

    @pl.when(t == 0)
    def _stream_start():
        s_scr[...] = s0_ref[0]
        if has_hist:
            lo_w = lax.broadcasted_iota(jnp.int32, (WINDOW, LANES), 1) < SWA_HD
            k0, k1 = _dup_halves(hist_ref[0, :, 0:LANES], lo_w)
            v0, v1 = _dup_halves(hist_ref[0, :, LANES:2 * LANES], lo_w)
            kd_scr[0, 0:WINDOW] = k0.astype(BF16)
            kd_scr[1, 0:WINDOW] = k1.astype(BF16)
            vd_scr[0, 0:WINDOW] = v0.astype(BF16)
            vd_scr[1, 0:WINDOW] = v1.astype(BF16)
        else:
            z = jnp.zeros((WINDOW, LANES), BF16)
            for kv in range(SWA_KV_HEADS):
                kd_scr[kv, 0:WINDOW] = z
                vd_scr[kv, 0:WINDOW] = z

    k0, k1 = _dup_halves(skv_ref[:, 0:LANES], lo_tl)
    v0, v1 = _dup_halves(skv_ref[:, LANES:2 * LANES], lo_tl)
    kd_scr[0, WINDOW:WINDOW + tl] = k0.astype(BF16)
    kd_scr[1, WINDOW:WINDOW + tl] = k1.astype(BF16)
    vd_scr[0, WINDOW:WINDOW + tl] = v0.astype(BF16)
    vd_scr[1, WINDOW:WINDOW + tl] = v1.astype(BF16)

    band = WINDOW + CHUNK
    n_keys = META_ROWS + band
    n_q = 4 * CHUNK
    scale2 = (SWA_HD ** -0.5) * LOG2E
    krow = lax.broadcasted_iota(jnp.int32, (n_keys, n_q), 0)
    zero_c = jnp.zeros((CHUNK, LANES), BF16)
    ones_v = jnp.ones((n_keys, LANES), BF16)
    for c in range(n_chunks):
        if has_hist:
            first_valid = META_ROWS
        else:
            first_valid = jnp.where(t == 0, max(META_ROWS + WINDOW - c * CHUNK, META_ROWS), META_ROWS)
        valid_t = (krow < N_META) | (krow >= first_valid)
        r0 = c * CHUNK
        for kv in range(SWA_KV_HEADS):
            keys = jnp.concatenate([mk_scr[kv], kd_scr[kv, r0:r0 + band]], axis=0)
            vals = jnp.concatenate([mv_scr[kv], vd_scr[kv, r0:r0 + band]], axis=0)
            qa = sq_ref[r0:r0 + CHUNK, (2 * kv) * LANES:(2 * kv + 1) * LANES]
            qb = sq_ref[r0:r0 + CHUNK, (2 * kv + 1) * LANES:(2 * kv + 2) * LANES]
            lhs = jnp.concatenate([jnp.where(lo_c, qa, zero_c), jnp.where(lo_c, zero_c, qa),
                                   jnp.where(lo_c, qb, zero_c), jnp.where(lo_c, zero_c, qb)], axis=0)
            s_t = lax.dot_general(keys, lhs, (((1,), (1,)), ((), ())), preferred_element_type=F32) * scale2
            s_t = jnp.where(valid_t, s_t, NEG_INF)
            s_t = jnp.where(krow == N_META, sink_ref[kv, 0:1, :] * LOG2E, s_t)
            e_t = jnp.exp2(s_t - jnp.max(s_t, axis=0, keepdims=True)).astype(BF16)
            ov = lax.dot_general(e_t, jnp.concatenate([vals, ones_v], axis=1), (((0,), (0,)), ((), ())),
                                 preferred_element_type=F32)
            o = ov[:, 0:LANES] * (1.0 / ov[:, LANES:2 * LANES])
            oa = jnp.where(lo_c, o[0:CHUNK], o[CHUNK:2 * CHUNK])
            ob = jnp.where(lo_c, o[2 * CHUNK:3 * CHUNK], o[3 * CHUNK:4 * CHUNK])
            base = RET_V + (2 * kv) * LANES
            omix_ref[r0:r0 + CHUNK, base:base + LANES] = oa.astype(BF16)
            omix_ref[r0:r0 + CHUNK, base + LANES:base + 2 * LANES] = ob.astype(BF16)

    zero_t = jnp.zeros((tl, LANES), BF16)
    for p in range(n_pairs):
        q = rqk_ref[:, p * LANES:(p + 1) * LANES]
        k = rqk_ref[:, RET_Q + p * LANES:RET_Q + (p + 1) * LANES]
        lhs = jnp.concatenate([jnp.where(lo_tl, q, zero_t), jnp.where(lo_tl, zero_t, q)], axis=0)
        s = lax.dot_general(lhs, k, (((1,), (1,)), ((), ())), preferred_element_type=F32)
        cross = jnp.dot(lhs, s_scr[p].astype(BF16), preferred_element_type=F32)
        for i in range(2):
            h = 2 * p + i
            v = rv_ref[:, h * LANES:(h + 1) * LANES]
            a = (s[i * tl:(i + 1) * tl] * dec_scr[h]).astype(BF16)
            o = jnp.dot(a, v, preferred_element_type=F32) + cross[i * tl:(i + 1) * tl] * cs_scr[h]
            r = o * lax.rsqrt(_lane_sum(o * o) * (1.0 / RET_DV) + EPS) * rng_ref[h:h + 1, :]
            omix_ref[:, h * LANES:(h + 1) * LANES] = (r * gate_ref[:, h * LANES:(h + 1) * LANES].astype(F32)).astype(BF16)
        u = _pair_update(k, rv_ref[:, (2 * p) * LANES:(2 * p + 1) * LANES],
                         rv_ref[:, (2 * p + 1) * LANES:(2 * p + 2) * LANES], wt_scr[p])
        s_scr[p] = gam_scr[p] * s_scr[p] + u

    if tl >= WINDOW:
        @pl.when(t + 1 < nt)
        def _carry_window():
            for kv in range(SWA_KV_HEADS):
                kd_scr[kv, 0:WINDOW] = kd_scr[kv, tl:tl + WINDOW]
                vd_scr[kv, 0:WINDOW] = vd_scr[kv, tl:tl + WINDOW]

    @pl.when(t + 1 == nt)
    def _stream_end():
        sout_ref[0] = s_scr[...]
        if tl >= WINDOW:
            kvout_ref[0] = skv_ref[tl - WINDOW:tl, :]
        else:
            kvout_ref[0, 0:WINDOW - tl] = hist_ref[0, tl:WINDOW, :]
            kvout_ref[0, WINDOW - tl:WINDOW] = skv_ref[...]


def _attn_post_kernel(rqk_ref, rv_ref, gate_ref, sq_ref, skv_ref, meta_ref, hist_ref, s0_ref, rng_ref, sink_ref,
                      x_ref, wout_ref, g2_ref, wrt_ref, base_ref,
                      sout_ref, kvout_ref, xmid_ref, hn_ref, wcol_ref, rt_ref, cnt_ref,
                      s_scr, kd_scr, vd_scr, mk_scr, mv_scr, dec_scr, wt_scr, cs_scr, gam_scr,
                      omix_scr, tri_scr, run_scr, *, tl, has_hist):
    _attn_kernel(rqk_ref, rv_ref, gate_ref, sq_ref, skv_ref, meta_ref, hist_ref, s0_ref, rng_ref, sink_ref,
                 omix_scr, sout_ref, kvout_ref,
                 s_scr, kd_scr, vd_scr, mk_scr, mv_scr, dec_scr, wt_scr, cs_scr, gam_scr, tl=tl, has_hist=has_hist)
    first_step = (pl.program_id(0) == 0) & (pl.program_id(1) == 0)
    _post_body(first_step, omix_scr[...], x_ref[...], wout_ref, g2_ref, wrt_ref, base_ref,
               xmid_ref, hn_ref, wcol_ref, rt_ref, cnt_ref, tri_scr, run_scr)


def _attention(rqk, rv, gate, sq, skv, meta_kv, hist_kv, s0, rng, sink_tab, *, n_streams, seq, tl, has_hist,
               post=None):
    nt = seq // tl
    assert tl % CHUNK == 0 and seq % tl == 0
    assert tl >= WINDOW or (nt == 1 and has_hist)
    n_pairs = RET_HEADS // 2
    s0_shared = s0.shape[0] == 1
    row = lambda b, t: (b * nt + t, 0)
    const2 = lambda b, t: (0, 0)
    const3 = lambda b, t: (0, 0, 0)
    per_b3 = lambda b, t: (b, 0, 0)
    s0_map = (lambda b, t: (0, 0, 0, 0)) if s0_shared else (lambda b, t: (b, 0, 0, 0))
    hist_map = per_b3 if has_hist else const3
    rows = n_streams * seq
    in_specs = [pl.BlockSpec((tl, 2 * RET_Q), row),
                pl.BlockSpec((tl, RET_V), row),
                pl.BlockSpec((tl, RET_V), row),
                pl.BlockSpec((tl, SWA_Q), row),
                pl.BlockSpec((tl, 2 * SWA_KV), row),
                pl.BlockSpec((N_META, 2 * SWA_KV), const2),
                pl.BlockSpec((1, WINDOW, 2 * SWA_KV), hist_map),
                pl.BlockSpec((1, n_pairs, 2 * RET_DK, RET_DV), s0_map),
                pl.BlockSpec((RET_HEADS, RET_DV), const2),
                pl.BlockSpec((SWA_KV_HEADS, SUBLANES, 4 * CHUNK), const3)]
    state_specs = [pl.BlockSpec((1, n_pairs, 2 * RET_DK, RET_DV), lambda b, t: (b, 0, 0, 0)),
                   pl.BlockSpec((1, WINDOW, 2 * SWA_KV), per_b3)]
    state_shapes = [jax.ShapeDtypeStruct((n_streams, n_pairs, 2 * RET_DK, RET_DV), F32),
                    jax.ShapeDtypeStruct((n_streams, WINDOW, 2 * SWA_KV), F32)]
    scratch = [pltpu.VMEM((n_pairs, 2 * RET_DK, RET_DV), F32),
               pltpu.VMEM((SWA_KV_HEADS, WINDOW + tl, LANES), BF16),
               pltpu.VMEM((SWA_KV_HEADS, WINDOW + tl, LANES), BF16),
               pltpu.VMEM((SWA_KV_HEADS, META_ROWS, LANES), BF16),
               pltpu.VMEM((SWA_KV_HEADS, META_ROWS, LANES), BF16),
               pltpu.VMEM((RET_HEADS, tl, tl), F32),
               pltpu.VMEM((n_pairs, tl, LANES), F32),
               pltpu.VMEM((RET_HEADS, tl, RET_DV), F32),
               pltpu.VMEM((n_pairs, 2 * RET_DK, RET_DV), F32)]
    args = (rqk, rv, gate, sq, skv, meta_kv, hist_kv, s0, rng, sink_tab)
    if post is None:
        return pl.pallas_call(
            functools.partial(_attn_kernel, tl=tl, has_hist=has_hist),
            grid=(n_streams, nt),
            in_specs=in_specs,
            out_specs=[pl.BlockSpec((tl, MIX_WIDTH), row)] + state_specs,
            out_shape=[jax.ShapeDtypeStruct((rows, MIX_WIDTH), BF16)] + state_shapes,
            scratch_shapes=scratch,
            compiler_params=_params(2),
            name="attention",
        )(*args)
    assert tl % LANES == 0
    return pl.pallas_call(
        functools.partial(_attn_post_kernel, tl=tl, has_hist=has_hist),
        grid=(n_streams, nt),
        in_specs=in_specs + [pl.BlockSpec((tl, D_MODEL), row),
                             pl.BlockSpec((MIX_WIDTH, D_MODEL), const2),
                             pl.BlockSpec((1, D_MODEL), const2),
                             pl.BlockSpec((ROUTER_ROWS, D_MODEL), const2),
                             pl.BlockSpec((N_EXPERTS, LANES), const2)],
        out_specs=state_specs + [pl.BlockSpec((tl, D_MODEL), row),
                                 pl.BlockSpec((tl, D_MODEL), row),
                                 pl.BlockSpec((tl, LANES), row),
                                 pl.BlockSpec((SUBLANES, tl), lambda b, t: (0, b * nt + t)),
                                 pl.BlockSpec((N_EXPERTS, LANES), const2)],
        out_shape=state_shapes + [jax.ShapeDtypeStruct((rows, D_MODEL), F32),
                                  jax.ShapeDtypeStruct((rows, D_MODEL), F32),
                                  jax.ShapeDtypeStruct((rows, LANES), F32),
                                  jax.ShapeDtypeStruct((SUBLANES, rows), F32),
                                  jax.ShapeDtypeStruct((N_EXPERTS, LANES), F32)],
        scratch_shapes=scratch + [pltpu.VMEM((tl, MIX_WIDTH), BF16), pltpu.VMEM((tl, tl), BF16),
                                  pltpu.VMEM((N_EXPERTS, LANES), F32)],
        compiler_params=_params(2),
        name="attention_post",
    )(*args, *post)


def _post_body(first_step, omix, x, wout_ref, g2_ref, wrt_ref, base_ref,
               xmid_ref, hn_ref, wcol_ref, rt_ref, cnt_ref, tri_scr, run_scr):
    tm = x.shape[0]

    @pl.when(first_step)
    def _init():
        r = lax.broadcasted_iota(jnp.int32, (tm, tm), 0)
        c = lax.broadcasted_iota(jnp.int32, (tm, tm), 1)
        tri_scr[...] = jnp.where(r < c, 1.0, 0.0).astype(BF16)
        run_scr[...] = base_ref[...]

    xm = x + jnp.dot(omix, wout_ref[...], preferred_element_type=F32)
    xmid_ref[...] = xm
    hn = _row_rms(xm) * g2_ref[...]
    hn_ref[...] = hn
    lt = lax.dot_general(wrt_ref[...], hn.astype(BF16), (((1,), (1,)), ((), ())), preferred_element_type=F32)
    row8 = lax.broadcasted_iota(jnp.int32, (SUBLANES, tm), 0)
    big = jnp.int32(SUBLANES)
    gl = jnp.where(row8 < N_GROUPS, lt[0:SUBLANES], NEG_INF)
    gmax = jnp.max(gl, axis=0, keepdims=True)
    gsum = jnp.sum(jnp.exp(gl - gmax), axis=0, keepdims=True)
    g_sel = jnp.min(jnp.where(gl == gmax, row8, big), axis=0, keepdims=True)
    p_sel = 1.0 / gsum
    el = lt[ROUTER_EXPERT_ROW0:ROUTER_EXPERT_ROW0 + EXPERTS_PER_GROUP]
    for g in range(1, N_GROUPS):
        lo = ROUTER_EXPERT_ROW0 + g * EXPERTS_PER_GROUP
        el = jnp.where(g_sel == g, lt[lo:lo + EXPERTS_PER_GROUP], el)
    m1 = jnp.max(el, axis=0, keepdims=True)
    i1 = jnp.min(jnp.where(el == m1, row8, big), axis=0, keepdims=True)
    el2 = jnp.where(row8 == i1, NEG_INF, el)
    m2 = jnp.max(el2, axis=0, keepdims=True)
    i2 = jnp.min(jnp.where(el2 == m2, row8, big), axis=0, keepdims=True)
    e2 = jnp.exp(m2 - m1)
    inv = 1.0 / (1.0 + e2)
    w1 = p_sel * inv
    w2 = p_sel * (e2 * inv)
    eid1 = g_sel * EXPERTS_PER_GROUP + i1
    eid2 = g_sel * EXPERTS_PER_GROUP + i2

    rowe = lax.broadcasted_iota(jnp.int32, (N_EXPERTS, tm), 0)
    oh = jnp.where((rowe == eid1) | (rowe == eid2), 1.0, 0.0).astype(BF16)
    run = run_scr[...]
    pref = jnp.dot(oh, tri_scr[...], preferred_element_type=F32) + jnp.concatenate([run] * (tm // LANES), axis=1)
    r1 = jnp.sum(jnp.where(rowe == eid1, pref, 0.0), axis=0, keepdims=True)
    r2 = jnp.sum(jnp.where(rowe == eid2, pref, 0.0), axis=0, keepdims=True)
    run = run + jnp.dot(oh, jnp.ones((tm, LANES), BF16), preferred_element_type=F32)
    run_scr[...] = run
    cnt_ref[...] = run

    out = jnp.zeros((SUBLANES, tm), F32)
    for k, v in enumerate([eid1.astype(F32), eid2.astype(F32), w1, w2, r1, r2]):
        out = jnp.where(row8 == k, v, out)
    rt_ref[...] = out
    rowl = lax.broadcasted_iota(jnp.int32, (LANES, tm), 0)
    wcol_ref[...] = jnp.where(rowl == 0, w1, jnp.where(rowl == 1, w2, 0.0)).T


def _post_kernel(omix_ref, x_ref, *refs):
    _post_body(pl.program_id(0) == 0, omix_ref[...], x_ref[...], *refs)


def _post(omix, x2d, w_out_bf, g2, w_router_t_bf, base_cnt, tm):
    t_rows = x2d.shape[0]
    assert tm % LANES == 0
    row = lambda i: (i, 0)
    const = lambda i: (0, 0)
    return pl.pallas_call(
        _post_kernel,
        grid=(t_rows // tm,),
        in_specs=[pl.BlockSpec((tm, MIX_WIDTH), row),
                  pl.BlockSpec((tm, D_MODEL), row),
                  pl.BlockSpec((MIX_WIDTH, D_MODEL), const),
                  pl.BlockSpec((1, D_MODEL), const),
                  pl.BlockSpec((ROUTER_ROWS, D_MODEL), const),
                  pl.BlockSpec((N_EXPERTS, LANES), const)],
        out_specs=[pl.BlockSpec((tm, D_MODEL), row),
                   pl.BlockSpec((tm, D_MODEL), row),
                   pl.BlockSpec((tm, LANES), row),
                   pl.BlockSpec((SUBLANES, tm), lambda i: (0, i)),
                   pl.BlockSpec((N_EXPERTS, LANES), const)],
        out_shape=[jax.ShapeDtypeStruct((t_rows, D_MODEL), F32),
                   jax.ShapeDtypeStruct((t_rows, D_MODEL), F32),
                   jax.ShapeDtypeStruct((t_rows, LANES), F32),
                   jax.ShapeDtypeStruct((SUBLANES, t_rows), F32),
                   jax.ShapeDtypeStruct((N_EXPERTS, LANES), F32)],
        scratch_shapes=[pltpu.VMEM((tm, tm), BF16), pltpu.VMEM((N_EXPERTS, LANES), F32)],
        compiler_params=_params(1),
        name="post",
    )(omix, x2d, w_out_bf, g2, w_router_t_bf, base_cnt)


def _step_major(pos, tm):
    return [pos[e].reshape(-1, 1, tm) for e in range(2)]


def _dispatch_kernel(pos0_ref, pos1_ref, *refs, tm, group_steps):
    pos_refs = (pos0_ref, pos1_ref)
    hn_refs = refs[:len(group_steps)]
    xs_ref, sem = refs[len(group_steps):]
    i = pl.program_id(0)

    first = 0
    for hn_ref, steps in zip(hn_refs, group_steps):
        @pl.when((i >= first) & (i < first + steps))
        def _rows(hn_ref=hn_ref):
            def body(blk, carry):
                for k in range(SUBLANES):
                    r = blk * SUBLANES + k
                    for e in range(2):
                        pltpu.make_async_copy(hn_ref.at[blk, pl.ds(k, 1)], xs_ref.at[pl.ds(pos_refs[e][0, 0, r], 1)],
                                              sem).start(priority=e)
                return carry

            lax.fori_loop(0, tm // SUBLANES, body, 0)
        first += steps

    pltpu.make_async_copy(xs_ref.at[pl.ds(0, 2 * tm)], xs_ref.at[pl.ds(0, 2 * tm)], sem).wait()


def _dispatch(pos_list, hn_list, tm):
    group_steps = tuple(hn.shape[0] // tm for hn in hn_list)
    n_steps = sum(group_steps)
    assert tm % SUBLANES == 0
    pos3 = [jnp.concatenate(parts, axis=0) for parts in zip(*[_step_major(p, tm) for p in pos_list])]
    smem = pl.BlockSpec((1, 1, tm), lambda i: (i, 0, 0), memory_space=pltpu.SMEM)
    hn_specs = []
    first = 0
    for steps in group_steps:
        hn_specs.append(pl.BlockSpec((tm // SUBLANES, SUBLANES, D_MODEL),
                                     lambda i, first=first, steps=steps: (jnp.clip(i - first, 0, steps - 1), 0, 0)))
        first += steps
    n_rows_out = 2 * sum(hn.shape[0] for hn in hn_list)
    hn_list = [hn.reshape(hn.shape[0] // SUBLANES, SUBLANES, D_MODEL) for hn in hn_list]
    return pl.pallas_call(
        functools.partial(_dispatch_kernel, tm=tm, group_steps=group_steps),
        grid=(n_steps,),
        in_specs=[smem, smem] + hn_specs,
        out_specs=pl.BlockSpec(memory_space=pl.ANY),
        out_shape=jax.ShapeDtypeStruct((n_rows_out, D_MODEL), F32),
        scratch_shapes=[pltpu.SemaphoreType.DMA(())],
        compiler_params=_params(1),
        name="dispatch",
    )(*pos3, *hn_list)


def _expert_kernel(vt_ref, ve_ref, lo_ref, hi_ref, nv_ref, x_ref, wgu_ref, wd_ref, y_ref):
    v = pl.program_id(0)

    @pl.when(v < nv_ref[0])
    def _compute():
        x = x_ref[...].astype(BF16)
        gu = jnp.dot(x, wgu_ref[0], preferred_element_type=F32)
        g = gu[:, 0:EXPERT_FF]
        a = (g * jax.nn.sigmoid(g) * gu[:, EXPERT_FF:2 * EXPERT_FF]).astype(BF16)
        y = jnp.dot(a, wd_ref[0], preferred_element_type=F32)
        lo, hi = lo_ref[v], hi_ref[v]

        @pl.when(lo == 0)
        def _first_visit():
            y_ref[...] = y

        @pl.when(lo > 0)
        def _later_visit():
            row = lax.broadcasted_iota(jnp.int32, y.shape, 0)
            y_ref[...] = jnp.where((row >= lo) & (row < hi), y, y_ref[...])


def _experts(vis_tile, vis_expert, vis_lo, vis_hi, n_vis, xs, wgu_bf, wd_bf, tm):
    n_steps = vis_tile.shape[0]
    last = lambda v, nv: jnp.minimum(v, nv[0] - 1)
    return pl.pallas_call(
        _expert_kernel,
        grid_spec=pltpu.PrefetchScalarGridSpec(
            num_scalar_prefetch=5,
            grid=(n_steps,),
            in_specs=[pl.BlockSpec((tm, D_MODEL), lambda v, vt, ve, lo, hi, nv: (vt[last(v, nv)], 0)),
                      pl.BlockSpec((1, D_MODEL, 2 * EXPERT_FF), lambda v, vt, ve, lo, hi, nv: (ve[last(v, nv)], 0, 0)),
                      pl.BlockSpec((1, EXPERT_FF, D_MODEL), lambda v, vt, ve, lo, hi, nv: (ve[last(v, nv)], 0, 0))],
            out_specs=pl.BlockSpec((tm, D_MODEL), lambda v, vt, ve, lo, hi, nv: (vt[last(v, nv)], 0))),
        out_shape=jax.ShapeDtypeStruct(xs.shape, F32),
        compiler_params=_params(1),
        name="experts",
    )(vis_tile, vis_expert, vis_lo, vis_hi, n_vis, xs, wgu_bf, wd_bf)


def _combine_kernel(pos0_ref, pos1_ref, nxt0_ref, nxt1_ref, ys_ref, xmid_ref, wcol_ref, out_ref, ybuf, sems,
                    *, tm, n_steps):
    i = pl.program_id(0)
    slot = i % 2

    def issue(p_refs, to_slot):
        def body(blk, carry):
            for k in range(SUBLANES):
                r = blk * SUBLANES + k
                for e in range(2):
                    pltpu.make_async_copy(ys_ref.at[pl.ds(p_refs[e][0, 0, r], 1)],
                                          ybuf.at[to_slot, e, blk, pl.ds(k, 1)], sems.at[to_slot]).start(priority=e)
            return carry

        lax.fori_loop(0, tm // SUBLANES, body, 0)

    @pl.when(i == 0)
    def _first():
        issue((pos0_ref, pos1_ref), 0)

    @pl.when(i + 1 < n_steps)
    def _ahead():
        issue((nxt0_ref, nxt1_ref), 1 - slot)

    for e in range(2):
        pltpu.make_async_copy(ybuf.at[slot, e], ybuf.at[slot, e], sems.at[slot]).wait()
    w = wcol_ref[...]
    y0 = ybuf[slot, 0].reshape(tm, D_MODEL)
    y1 = ybuf[slot, 1].reshape(tm, D_MODEL)
    out_ref[...] = xmid_ref[...] + w[:, 0:1] * y0 + w[:, 1:2] * y1


def _combine(pos, ys, xmid, wcol, tm):
    t_rows = xmid.shape[0]
    n_steps = t_rows // tm
    assert tm % SUBLANES == 0
    pos3 = _step_major(pos, tm)
    row = lambda i: (i, 0)
    cur = pl.BlockSpec((1, 1, tm), lambda i: (i, 0, 0), memory_space=pltpu.SMEM)
    nxt = pl.BlockSpec((1, 1, tm), lambda i: (jnp.minimum(i + 1, n_steps - 1), 0, 0), memory_space=pltpu.SMEM)
    return pl.pallas_call(
        functools.partial(_combine_kernel, tm=tm, n_steps=n_steps),
        grid=(n_steps,),
        in_specs=[cur, cur, nxt, nxt,
                  pl.BlockSpec(memory_space=pl.ANY),
                  pl.BlockSpec((tm, D_MODEL), row),
                  pl.BlockSpec((tm, LANES), row)],
        out_specs=pl.BlockSpec((tm, D_MODEL), row),
        out_shape=jax.ShapeDtypeStruct((t_rows, D_MODEL), F32),
        scratch_shapes=[pltpu.VMEM((2, 2, tm // SUBLANES, SUBLANES, D_MODEL), F32), pltpu.SemaphoreType.DMA((2,))],
        compiler_params=_params(1),
        name="combine",
    )(*pos3, *pos3, ys, xmid, wcol)


def _bucket(ends, idx):
    n = jnp.sum((ends[None, :] <= idx[:, None]).astype(jnp.int32), axis=1)
    return jnp.minimum(n, ends.shape[0] - 1)


def _tile_for(rows, pref):
    tm = min(pref, rows)
    assert rows % tm == 0
    return tm


def kernel(x_prompt, x_sample, cache_ret_state, cache_swa_k, cache_swa_v, meta_tokens, norm1_g, w_in, q_norm_g,
           k_norm_g, ret_norm_g, attn_sinks, w_out, norm2_g, w_group, w_expert, w_gate, w_up, w_down):
    assert norm1_g.shape[0] == 1, "single-layer trunk"
    bp, lp, _ = x_prompt.shape
    bs, ls, _ = x_sample.shape
    n_pairs = RET_HEADS // 2

    g1 = norm1_g[0][None, :]
    g2 = norm2_g[0][None, :]
    w_in_bf = w_in[0].astype(BF16)
    w_out_bf = w_out[0].astype(BF16)
    qg2 = jnp.tile(q_norm_g[0], 2)[None, :]
    kg2 = jnp.tile(k_norm_g[0], 2)[None, :]
    rng = ret_norm_g[0].reshape(RET_HEADS, RET_DV)
    sink_tab = jnp.broadcast_to(jnp.repeat(attn_sinks[0], CHUNK).reshape(SWA_KV_HEADS, 1, 4 * CHUNK),
                                (SWA_KV_HEADS, SUBLANES, 4 * CHUNK))
    w_router_t = jnp.zeros((ROUTER_ROWS, D_MODEL), F32)
    w_router_t = w_router_t.at[0:N_GROUPS].set(w_group[0].T)
    w_router_t = w_router_t.at[ROUTER_EXPERT_ROW0:ROUTER_EXPERT_ROW0 + N_EXPERTS].set(w_expert[0].T)
    w_router_t_bf = w_router_t.astype(BF16)
    wgu_bf = jnp.concatenate([w_gate[0], w_up[0]], axis=-1).astype(BF16)
    wd_bf = w_down[0].astype(BF16)

    meta_rows = 2 * CHUNK
    m_pad = jnp.zeros((meta_rows, D_MODEL), F32).at[0:N_META].set(meta_tokens)
    m_rqk, m_rv, _, _, m_skv = _proj(m_pad, jnp.arange(meta_rows, dtype=jnp.int32), meta_rows, g1, w_in_bf, qg2, kg2)
    s_meta = _meta_state(m_rqk, m_rv)[None]
    meta_kv = m_skv[0:N_META]

    groups = [
        dict(x=x_prompt.reshape(bp * lp, D_MODEL), n=bp, seq=lp, pos0=N_META, has_hist=False, s0=s_meta,
             hist=jnp.zeros((1, WINDOW, 2 * SWA_KV), F32)),
        dict(x=x_sample.reshape(bs * ls, D_MODEL), n=bs, seq=ls, pos0=N_META + PAST_LEN, has_hist=True,
             s0=cache_ret_state[0].reshape(bs, n_pairs, 2 * RET_DK, RET_DV),
             hist=jnp.concatenate([cache_swa_k[0].reshape(bs, WINDOW, SWA_KV),
                                   cache_swa_v[0].reshape(bs, WINDOW, SWA_KV)], axis=-1)),
    ]

    base_cnt = jnp.zeros((N_EXPERTS, LANES), F32)
    for g in groups:
        rows = g["n"] * g["seq"]
        tm = _tile_for(rows, PROJ_TILE)
        pos = g["pos0"] + jnp.arange(g["seq"], dtype=jnp.int32)
        if g["seq"] < tm:
            assert tm % g["seq"] == 0
            pos = jnp.tile(pos, tm // g["seq"])
        else:
            assert g["seq"] % tm == 0
        rqk, rv, gate, sq, skv = _proj(g["x"], pos, tm, g1, w_in_bf, qg2, kg2)
        tl = min(ATTN_TILE, g["seq"])
        attn = functools.partial(_attention, rqk, rv, gate, sq, skv, meta_kv, g["hist"], g["s0"], rng, sink_tab,
                                 n_streams=g["n"], seq=g["seq"], tl=tl, has_hist=g["has_hist"])
        if tl % LANES == 0:
            s_out, kv_out, xmid, hn, wcol, route_t, base_cnt = attn(post=(g["x"], w_out_bf, g2, w_router_t_bf, base_cnt))
        else:
            omix, s_out, kv_out = attn()
            xmid, hn, wcol, route_t, base_cnt = _post(omix, g["x"], w_out_bf, g2, w_router_t_bf, base_cnt,
                                                      _tile_for(rows, POST_TILE))
        g.update(xmid=xmid, hn=hn, wcol=wcol, route_t=route_t, s_out=s_out, kv_out=kv_out)

    te = EXPERT_TILE
    total_rows = sum(g["n"] * g["seq"] for g in groups)
    assert (2 * total_rows) % te == 0
    n_row_tiles = (2 * total_rows) // te
    counts = base_cnt[:, 0].astype(jnp.int32)
    off = jnp.cumsum(counts) - counts
    first_tile = off // te
    n_vis_e = jnp.where(counts > 0, (off + counts - 1) // te - first_tile + 1, 0)
    vis_end = jnp.cumsum(n_vis_e)
    n_vis = vis_end[-1:].astype(jnp.int32)
    v = jnp.arange(n_row_tiles + N_EXPERTS, dtype=jnp.int32)
    vis_expert = _bucket(vis_end, v)
    pick = lambda table: jnp.sum(jnp.where(vis_expert[:, None] == jnp.arange(N_EXPERTS, dtype=jnp.int32)[None, :],
                                           table[None, :], 0), axis=1)
    vis_tile = jnp.clip(pick(first_tile) + v - pick(vis_end - n_vis_e), 0, n_row_tiles - 1).astype(jnp.int32)
    vis_lo = jnp.clip(pick(off) - vis_tile * te, 0, te).astype(jnp.int32)
    vis_hi = jnp.clip(pick(off + counts) - vis_tile * te, 0, te).astype(jnp.int32)

    for g in groups:
        eid = g["route_t"][0:2].astype(jnp.int32)
        off_sel = jnp.sum(jnp.where(eid[None] == jnp.arange(N_EXPERTS, dtype=jnp.int32)[:, None, None],
                                    off[:, None, None], 0), axis=0)
        g["pos"] = (off_sel + g["route_t"][4:6].astype(jnp.int32)).astype(jnp.int32)
    xs = _dispatch([g["pos"] for g in groups], [g["hn"] for g in groups], MOVE_TILE)

    ys = _experts(vis_tile, vis_expert.astype(jnp.int32), vis_lo, vis_hi, n_vis, xs, wgu_bf, wd_bf, te)

    outs = []
    for g in groups:
        rows = g["n"] * g["seq"]
        y = _combine(g["pos"], ys, g["xmid"], g["wcol"], _tile_for(rows, MOVE_TILE))
        outs.append(y.reshape(g["n"], g["seq"], D_MODEL))

    def caches(g):
        kv = g["kv_out"]
        k = kv[:, :, 0:SWA_KV].reshape(g["n"], WINDOW, SWA_KV_HEADS, SWA_HD)[None]
        v = kv[:, :, SWA_KV:2 * SWA_KV].reshape(g["n"], WINDOW, SWA_KV_HEADS, SWA_HD)[None]
        s = g["s_out"].reshape(g["n"], RET_HEADS, RET_DK, RET_DV)[None]
        return s, k, v

    sp, kp, vp = caches(groups[0])
    ss, ks, vs = caches(groups[1])
    return (outs[0], outs[1], sp, kp, vp, ss, ks, vs)
```

```python
import functools

import numpy as np
import jax
import jax.numpy as jnp
from jax import lax
from jax.experimental import pallas as pl
from jax.experimental.pallas import tpu as pltpu

F32 = jnp.float32
BF16 = jnp.bfloat16

D_MODEL = 1024
PAST_LEN = 4096
CHUNK = 64
N_META = 16
RET_HEADS = 4
RET_DK = 64
RET_DV = 128
SWA_HEADS = 8
SWA_KV_HEADS = 2
SWA_HD = 64
WINDOW = 128
ROPE_THETA = 10000.0
N_GROUPS = 4
EXPERTS_PER_GROUP = 8
N_EXPERTS = N_GROUPS * EXPERTS_PER_GROUP
EXPERT_FF = 256
EPS = 1e-6
NEG_INF = -1e30
LOG2E = float(np.log2(np.e))
RET_Q = RET_HEADS * RET_DK
RET_V = RET_HEADS * RET_DV
SWA_Q = SWA_HEADS * SWA_HD
SWA_KV = SWA_KV_HEADS * SWA_HD
MIX_WIDTH = RET_V + SWA_Q
IN_WIDTH = 2 * RET_Q + 2 * RET_V + SWA_Q + 2 * SWA_KV

LANES = 128
PROJ_TILE = 1024
POST_TILE = 512
ATTN_TILE = 256
EXPERT_TILE = 512
MOVE_TILE = 256
SUBLANES = 8
ROUTER_EXPERT_ROW0 = 8
ROUTER_ROWS = 64
META_ROWS = 64
VMEM_LIMIT = 56 * 1024 * 1024

_LOG_G = [float(np.log1p(-np.exp2(-5.0 - h))) for h in range(RET_HEADS)]


def _params(n_axes):
    return pltpu.CompilerParams(dimension_semantics=("arbitrary",) * n_axes, vmem_limit_bytes=VMEM_LIMIT)


def _split_bf16(a):
    hi = a.astype(BF16)
    return hi, (a - hi.astype(F32)).astype(BF16)


def _split_dot(a, w2):
    hi, lo = _split_bf16(a)
    return jnp.dot(jnp.concatenate([hi, lo], axis=1), w2, preferred_element_type=F32)


def _lane_sum(a):
    return _split_dot(a, jnp.ones((2 * LANES, LANES), BF16))


def _head_sum_matrix():
    i = lax.broadcasted_iota(jnp.int32, (2 * LANES, LANES), 0) % LANES
    j = lax.broadcasted_iota(jnp.int32, (2 * LANES, LANES), 1)
    return jnp.where((i < SWA_HD) == (j < SWA_HD), 1.0, 0.0).astype(BF16)


def _rope(t, c, s1, s2):
    half = SWA_HD // 2
    return t * c + pltpu.roll(t, LANES - half, 1) * s1 + pltpu.roll(t, half, 1) * s2


def _head_rms(t, g, head_w):
    ms = _split_dot(t * t, head_w) * (1.0 / SWA_HD)
    return t * lax.rsqrt(ms + EPS) * g


def _row_rms(x):
    n_tiles = x.shape[1] // LANES
    ss = x[:, 0:LANES] * x[:, 0:LANES]
    for j in range(1, n_tiles):
        ss = ss + x[:, j * LANES:(j + 1) * LANES] * x[:, j * LANES:(j + 1) * LANES]
    r = lax.rsqrt(_lane_sum(ss) * (1.0 / x.shape[1]) + EPS)
    return x * jnp.concatenate([r] * n_tiles, axis=1)


def _proj_kernel(x_ref, g1_ref, w_ref, qg_ref, kg_ref, cos_ref, s1_ref, s2_ref,
                 rqk_ref, rv_ref, gate_ref, sq_ref, skv_ref):
    xn = (_row_rms(x_ref[...]) * g1_ref[...]).astype(BF16)
    c, s1, s2 = cos_ref[...], s1_ref[...], s2_ref[...]
    head_w = _head_sum_matrix()

    def seg(a, b):
        return jnp.dot(xn, w_ref[:, a:b], preferred_element_type=F32)

    def tile(h, j):
        return h[:, j * LANES:(j + 1) * LANES]

    h = seg(0, 2 * RET_Q)
    for j in range(2):
        rqk_ref[:, j * LANES:(j + 1) * LANES] = _rope(tile(h, j), c, s1, s2).astype(BF16)
    for j in range(2, 4):
        rqk_ref[:, j * LANES:(j + 1) * LANES] = (_rope(tile(h, j), c, s1, s2) * (RET_DK ** -0.5)).astype(BF16)
    a = 2 * RET_Q
    rv_ref[...] = seg(a, a + RET_V).astype(BF16)
    a += RET_V
    g = seg(a, a + RET_V)
    gate_ref[...] = (g * jax.nn.sigmoid(g)).astype(BF16)
    a += RET_V
    h = seg(a, a + SWA_Q)
    qg = qg_ref[...]
    for j in range(SWA_Q // LANES):
        sq_ref[:, j * LANES:(j + 1) * LANES] = _rope(_head_rms(tile(h, j), qg, head_w), c, s1, s2).astype(BF16)
    a += SWA_Q
    h = seg(a, a + 2 * SWA_KV)
    skv_ref[:, 0:LANES] = _rope(_head_rms(tile(h, 0), kg_ref[...], head_w), c, s1, s2)
    skv_ref[:, LANES:2 * LANES] = tile(h, 1)


def _rope_tables(pos):
    half = SWA_HD // 2
    inv = ROPE_THETA ** (-jnp.arange(half, dtype=F32) / half)
    ang = pos.astype(F32)[:, None] * inv[None, :]
    cos, sin = jnp.cos(ang), jnp.sin(ang)
    z = jnp.zeros_like(sin)
    return (jnp.tile(cos, (1, 4)),
            jnp.tile(jnp.concatenate([-sin, z], axis=1), (1, 2)),
            jnp.tile(jnp.concatenate([z, sin], axis=1), (1, 2)))


def _proj(x2d, pos_rows, tm, g1, w_in_bf, qg2, kg2):
    t_rows = x2d.shape[0]
    n_tiles = t_rows // tm
    n_pos_tiles = pos_rows.shape[0] // tm
    cos, s1, s2 = _rope_tables(pos_rows)
    row = lambda i: (i, 0)
    const = lambda i: (0, 0)
    tab = lambda i: (i % n_pos_tiles, 0)
    return pl.pallas_call(
        _proj_kernel,
        grid=(n_tiles,),
        in_specs=[pl.BlockSpec((tm, D_MODEL), row),
                  pl.BlockSpec((1, D_MODEL), const),
                  pl.BlockSpec((D_MODEL, IN_WIDTH), const),
                  pl.BlockSpec((1, LANES), const),
                  pl.BlockSpec((1, LANES), const),
                  pl.BlockSpec((tm, LANES), tab),
                  pl.BlockSpec((tm, LANES), tab),
                  pl.BlockSpec((tm, LANES), tab)],
        out_specs=[pl.BlockSpec((tm, 2 * RET_Q), row),
                   pl.BlockSpec((tm, RET_V), row),
                   pl.BlockSpec((tm, RET_V), row),
                   pl.BlockSpec((tm, SWA_Q), row),
                   pl.BlockSpec((tm, 2 * SWA_KV), row)],
        out_shape=[jax.ShapeDtypeStruct((t_rows, 2 * RET_Q), BF16),
                   jax.ShapeDtypeStruct((t_rows, RET_V), BF16),
                   jax.ShapeDtypeStruct((t_rows, RET_V), BF16),
                   jax.ShapeDtypeStruct((t_rows, SWA_Q), BF16),
                   jax.ShapeDtypeStruct((t_rows, 2 * SWA_KV), F32)],
        compiler_params=_params(1),
        name="proj",
    )(x2d, g1, w_in_bf, qg2, kg2, cos, s1, s2)


def _pair_update(k_bf, v0_bf, v1_bf, wt):
    kw = (k_bf.astype(F32) * wt).astype(BF16)
    dn = (((0,), (0,)), ((), ()))
    a0 = lax.dot_general(kw, v0_bf, dn, preferred_element_type=F32)
    a1 = lax.dot_general(kw, v1_bf, dn, preferred_element_type=F32)
    top = lax.broadcasted_iota(jnp.int32, a0.shape, 0) < RET_DK
    return jnp.where(top, a0, a1)


def _decay_rows(n, pair, rows_back_from):
    i = lax.broadcasted_iota(jnp.int32, (n, LANES), 0).astype(F32)
    lane = lax.broadcasted_iota(jnp.int32, (n, LANES), 1)
    lg = jnp.where(lane < RET_DK, _LOG_G[2 * pair], _LOG_G[2 * pair + 1])
    return jnp.exp((rows_back_from - i) * lg)


def _meta_state_kernel(rqk_ref, rv_ref, s_ref, *, n_rows):
    for p in range(RET_HEADS // 2):
        k = rqk_ref[:, RET_Q + p * LANES:RET_Q + (p + 1) * LANES]
        wt = _decay_rows(n_rows, p, float(N_META - 1))
        s_ref[p] = _pair_update(k, rv_ref[:, (2 * p) * LANES:(2 * p + 1) * LANES],
                                rv_ref[:, (2 * p + 1) * LANES:(2 * p + 2) * LANES], wt)


def _meta_state(m_rqk, m_rv):
    n_rows = m_rqk.shape[0]
    return pl.pallas_call(
        functools.partial(_meta_state_kernel, n_rows=n_rows),
        out_shape=jax.ShapeDtypeStruct((RET_HEADS // 2, 2 * RET_DK, RET_DV), F32),
        name="meta_state",
    )(m_rqk, m_rv)


def _dup_halves(a, lo_mask):
    sw = pltpu.roll(a, SWA_HD, 1)
    return jnp.where(lo_mask, a, sw), jnp.where(lo_mask, sw, a)


def _attn_kernel(rqk_ref, rv_ref, gate_ref, sq_ref, skv_ref, meta_ref, hist_ref, s0_ref, rng_ref, sink_ref,
                 omix_ref, sout_ref, kvout_ref,
                 s_scr, kd_scr, vd_scr, mk_scr, mv_scr, dec_scr, wt_scr, cs_scr, gam_scr,
                 *, tl, has_hist):
    b = pl.program_id(0)
    t = pl.program_id(1)
    nt = pl.num_programs(1)
    n_chunks = tl // CHUNK
    n_pairs = RET_HEADS // 2
    lo_tl = lax.broadcasted_iota(jnp.int32, (tl, LANES), 1) < SWA_HD
    lo_c = lax.broadcasted_iota(jnp.int32, (CHUNK, LANES), 1) < SWA_HD

    @pl.when((b == 0) & (t == 0))
    def _tables():
        i = lax.broadcasted_iota(jnp.int32, (tl, tl), 0)
        j = lax.broadcasted_iota(jnp.int32, (tl, tl), 1)
        diff = (i - j).astype(F32)
        row = lax.broadcasted_iota(jnp.int32, (tl, LANES), 0).astype(F32)
        for h in range(RET_HEADS):
            dec_scr[h] = jnp.where(diff >= 0.0, jnp.exp(jnp.maximum(diff, 0.0) * _LOG_G[h]), 0.0)
            cs_scr[h] = jnp.exp((row + 1.0) * _LOG_G[h])
        top = lax.broadcasted_iota(jnp.int32, (2 * RET_DK, RET_DV), 0) < RET_DK
        for p in range(n_pairs):
            wt_scr[p] = _decay_rows(tl, p, float(tl - 1))
            gam_scr[p] = jnp.where(top, jnp.exp(jnp.float32(tl * _LOG_G[2 * p])), jnp.exp(jnp.float32(tl * _LOG_G[2 * p + 1])))
        lo_m = lax.broadcasted_iota(jnp.int32, (N_META, LANES), 1) < SWA_HD
        mk0, mk1 = _dup_halves(meta_ref[:, 0:LANES], lo_m)
        mv0, mv1 = _dup_halves(meta_ref[:, LANES:2 * LANES], lo_m)
        mk_scr[...] = jnp.zeros(mk_scr.shape, BF16)
        mv_scr[...] = jnp.zeros(mv_scr.shape, BF16)
        mk_scr[0, 0:N_META] = mk0.astype(BF16)
        mk_scr[1, 0:N_META] = mk1.astype(BF16)
        mv_scr[0, 0:N_META] = mv0.astype(BF16)
        mv_scr[1, 0:N_META] = mv1.astype(BF16)

    @pl.when(t == 0)
    def _stream_start():
        s_scr[...] = s0_ref[0]
        if has_hist:
            lo_w = lax.broadcasted_iota(jnp.int32, (WINDOW, LANES), 1) < SWA_HD
            k0, k1 = _dup_halves(hist_ref[0, :, 0:LANES], lo_w)
            v0, v1 = _dup_halves(hist_ref[0, :, LANES:2 * LANES], lo_w)
            kd_scr[0, 0:WINDOW] = k0.astype(BF16)
            kd_scr[1, 0:WINDOW] = k1.astype(BF16)
            vd_scr[0, 0:WINDOW] = v0.astype(BF16)
            vd_scr[1, 0:WINDOW] = v1.astype(BF16)
        else:
            z = jnp.zeros((WINDOW, LANES), BF16)
            for kv in range(SWA_KV_HEADS):
                kd_scr[kv, 0:WINDOW] = z
                vd_scr[kv, 0:WINDOW] = z

    k0, k1 = _dup_halves(skv_ref[:, 0:LANES], lo_tl)
    v0, v1 = _dup_halves(skv_ref[:, LANES:2 * LANES], lo_tl)
    kd_scr[0, WINDOW:WINDOW + tl] = k0.astype(BF16)
    kd_scr[1, WINDOW:WINDOW + tl] = k1.astype(BF16)
    vd_scr[0, WINDOW:WINDOW + tl] = v0.astype(BF16)
    vd_scr[1, WINDOW:WINDOW + tl] = v1.astype(BF16)

    band = WINDOW + CHUNK
    n_keys = META_ROWS + band
    n_q = 4 * CHUNK
    scale2 = (SWA_HD ** -0.5) * LOG2E
    krow = lax.broadcasted_iota(jnp.int32, (n_keys, n_q), 0)
    zero_c = jnp.zeros((CHUNK, LANES), BF16)
    ones_v = jnp.ones((n_keys, LANES), BF16)
    for c in range(n_chunks):
        if has_hist:
            first_valid = META_ROWS
        else:
            first_valid = jnp.where(t == 0, max(META_ROWS + WINDOW - c * CHUNK, META_ROWS), META_ROWS)
        valid_t = (krow < N_META) | (krow >= first_valid)
        r0 = c * CHUNK
        for kv in range(SWA_KV_HEADS):
            keys = jnp.concatenate([mk_scr[kv], kd_scr[kv, r0:r0 + band]], axis=0)
            vals = jnp.concatenate([mv_scr[kv], vd_scr[kv, r0:r0 + band]], axis=0)
            qa = sq_ref[r0:r0 + CHUNK, (2 * kv) * LANES:(2 * kv + 1) * LANES]
            qb = sq_ref[r0:r0 + CHUNK, (2 * kv + 1) * LANES:(2 * kv + 2) * LANES]
            lhs = jnp.concatenate([jnp.where(lo_c, qa, zero_c), jnp.where(lo_c, zero_c, qa),
                                   jnp.where(lo_c, qb, zero_c), jnp.where(lo_c, zero_c, qb)], axis=0)
            s_t = lax.dot_general(keys, lhs, (((1,), (1,)), ((), ())), preferred_element_type=F32) * scale2
            s_t = jnp.where(valid_t, s_t, NEG_INF)
            s_t = jnp.where(krow == N_META, sink_ref[kv, 0:1, :] * LOG2E, s_t)
            e_t = jnp.exp2(s_t - jnp.max(s_t, axis=0, keepdims=True)).astype(BF16)
            ov = lax.dot_general(e_t, jnp.concatenate([vals, ones_v], axis=1), (((0,), (0,)), ((), ())),
                                 preferred_element_type=F32)
            o = ov[:, 0:LANES] * (1.0 / ov[:, LANES:2 * LANES])
            oa = jnp.where(lo_c, o[0:CHUNK], o[CHUNK:2 * CHUNK])
            ob = jnp.where(lo_c, o[2 * CHUNK:3 * CHUNK], o[3 * CHUNK:4 * CHUNK])
            base = RET_V + (2 * kv) * LANES
            omix_ref[r0:r0 + CHUNK, base:base + LANES] = oa.astype(BF16)
            omix_ref[r0:r0 + CHUNK, base + LANES:base + 2 * LANES] = ob.astype(BF16)

    zero_t = jnp.zeros((tl, LANES), BF16)
    for p in range(n_pairs):
        q = rqk_ref[:, p * LANES:(p + 1) * LANES]
        k = rqk_ref[:, RET_Q + p * LANES:RET_Q + (p + 1) * LANES]
        lhs = jnp.concatenate([jnp.where(lo_tl, q, zero_t), jnp.where(lo_tl, zero_t, q)], axis=0)
        s = lax.dot_general(lhs, k, (((1,), (1,)), ((), ())), preferred_element_type=F32)
        cross = jnp.dot(lhs, s_scr[p].astype(BF16), preferred_element_type=F32)
        for i in range(2):
            h = 2 * p + i
            v = rv_ref[:, h * LANES:(h + 1) * LANES]
            a = (s[i * tl:(i + 1) * tl] * dec_scr[h]).astype(BF16)
            o = jnp.dot(a, v, preferred_element_type=F32) + cross[i * tl:(i + 1) * tl] * cs_scr[h]
            r = o * lax.rsqrt(_lane_sum(o * o) * (1.0 / RET_DV) + EPS) * rng_ref[h:h + 1, :]
            omix_ref[:, h * LANES:(h + 1) * LANES] = (r * gate_ref[:, h * LANES:(h + 1) * LANES].astype(F32)).astype(BF16)
        u = _pair_update(k, rv_ref[:, (2 * p) * LANES:(2 * p + 1) * LANES],
                         rv_ref[:, (2 * p + 1) * LANES:(2 * p + 2) * LANES], wt_scr[p])
        s_scr[p] = gam_scr[p] * s_scr[p] + u

    if tl >= WINDOW:
        @pl.when(t + 1 < nt)
        def _carry_window():
            for kv in range(SWA_KV_HEADS):
                kd_scr[kv, 0:WINDOW] = kd_scr[kv, tl:tl + WINDOW]
                vd_scr[kv, 0:WINDOW] = vd_scr[kv, tl:tl + WINDOW]

    @pl.when(t + 1 == nt)
    def _stream_end():
        sout_ref[0] = s_scr[...]
        if tl >= WINDOW:
            kvout_ref[0] = skv_ref[tl - WINDOW:tl, :]
        else:
            kvout_ref[0, 0:WINDOW - tl] = hist_ref[0, tl:WINDOW, :]
            kvout_ref[0, WINDOW - tl:WINDOW] = skv_ref[...]


def _attention(rqk, rv, gate, sq, skv, meta_kv, hist_kv, s0, rng, sink_tab, *, n_streams, seq, tl, has_hist):
    nt = seq // tl
    assert tl % CHUNK == 0 and seq % tl == 0
    assert tl >= WINDOW or (nt == 1 and has_hist)
    n_pairs = RET_HEADS // 2
    s0_shared = s0.shape[0] == 1
    row = lambda b, t: (b * nt + t, 0)
    const2 = lambda b, t: (0, 0)
    const3 = lambda b, t: (0, 0, 0)
    per_b3 = lambda b, t: (b, 0, 0)
    s0_map = (lambda b, t: (0, 0, 0, 0)) if s0_shared else (lambda b, t: (b, 0, 0, 0))
    hist_map = per_b3 if has_hist else const3
    rows = n_streams * seq
    return pl.pallas_call(
        functools.partial(_attn_kernel, tl=tl, has_hist=has_hist),
        grid=(n_streams, nt),
        in_specs=[pl.BlockSpec((tl, 2 * RET_Q), row),
                  pl.BlockSpec((tl, RET_V), row),
                  pl.BlockSpec((tl, RET_V), row),
                  pl.BlockSpec((tl, SWA_Q), row),
                  pl.BlockSpec((tl, 2 * SWA_KV), row),
                  pl.BlockSpec((N_META, 2 * SWA_KV), const2),
                  pl.BlockSpec((1, WINDOW, 2 * SWA_KV), hist_map),
                  pl.BlockSpec((1, n_pairs, 2 * RET_DK, RET_DV), s0_map),
                  pl.BlockSpec((RET_HEADS, RET_DV), const2),
                  pl.BlockSpec((SWA_KV_HEADS, SUBLANES, 4 * CHUNK), const3)],
        out_specs=[pl.BlockSpec((tl, MIX_WIDTH), row),
                   pl.BlockSpec((1, n_pairs, 2 * RET_DK, RET_DV), lambda b, t: (b, 0, 0, 0)),
                   pl.BlockSpec((1, WINDOW, 2 * SWA_KV), per_b3)],
        out_shape=[jax.ShapeDtypeStruct((rows, MIX_WIDTH), BF16),
                   jax.ShapeDtypeStruct((n_streams, n_pairs, 2 * RET_DK, RET_DV), F32),
                   jax.ShapeDtypeStruct((n_streams, WINDOW, 2 * SWA_KV), F32)],
        scratch_shapes=[pltpu.VMEM((n_pairs, 2 * RET_DK, RET_DV), F32),
                        pltpu.VMEM((SWA_KV_HEADS, WINDOW + tl, LANES), BF16),
                        pltpu.VMEM((SWA_KV_HEADS, WINDOW + tl, LANES), BF16),
                        pltpu.VMEM((SWA_KV_HEADS, META_ROWS, LANES), BF16),
                        pltpu.VMEM((SWA_KV_HEADS, META_ROWS, LANES), BF16),
                        pltpu.VMEM((RET_HEADS, tl, tl), F32),
                        pltpu.VMEM((n_pairs, tl, LANES), F32),
                        pltpu.VMEM((RET_HEADS, tl, RET_DV), F32),
                        pltpu.VMEM((n_pairs, 2 * RET_DK, RET_DV), F32)],
        compiler_params=_params(2),
        name="attention",
    )(rqk, rv, gate, sq, skv, meta_kv, hist_kv, s0, rng, sink_tab)


def _post_kernel(omix_ref, x_ref, wout_ref, g2_ref, wrt_ref, base_ref,
                 xmid_ref, hn_ref, wcol_ref, rt_ref, cnt_ref, tri_scr, run_scr):
    i = pl.program_id(0)
    tm = x_ref.shape[0]

    @pl.when(i == 0)
    def _init():
        r = lax.broadcasted_iota(jnp.int32, (tm, tm), 0)
        c = lax.broadcasted_iota(jnp.int32, (tm, tm), 1)
        tri_scr[...] = jnp.where(r < c, 1.0, 0.0).astype(BF16)
        run_scr[...] = base_ref[...]

    xm = x_ref[...] + jnp.dot(omix_ref[...], wout_ref[...], preferred_element_type=F32)
    xmid_ref[...] = xm
    hn = _row_rms(xm) * g2_ref[...]
    hn_ref[...] = hn
    lt = lax.dot_general(wrt_ref[...], hn.astype(BF16), (((1,), (1,)), ((), ())), preferred_element_type=F32)
    row8 = lax.broadcasted_iota(jnp.int32, (SUBLANES, tm), 0)
    big = jnp.int32(SUBLANES)
    gl = jnp.where(row8 < N_GROUPS, lt[0:SUBLANES], NEG_INF)
    gmax = jnp.max(gl, axis=0, keepdims=True)
    gsum = jnp.sum(jnp.exp(gl - gmax), axis=0, keepdims=True)
    g_sel = jnp.min(jnp.where(gl == gmax, row8, big), axis=0, keepdims=True)
    p_sel = 1.0 / gsum
    el = lt[ROUTER_EXPERT_ROW0:ROUTER_EXPERT_ROW0 + EXPERTS_PER_GROUP]
    for g in range(1, N_GROUPS):
        lo = ROUTER_EXPERT_ROW0 + g * EXPERTS_PER_GROUP
        el = jnp.where(g_sel == g, lt[lo:lo + EXPERTS_PER_GROUP], el)
    m1 = jnp.max(el, axis=0, keepdims=True)
    i1 = jnp.min(jnp.where(el == m1, row8, big), axis=0, keepdims=True)
    el2 = jnp.where(row8 == i1, NEG_INF, el)
    m2 = jnp.max(el2, axis=0, keepdims=True)
    i2 = jnp.min(jnp.where(el2 == m2, row8, big), axis=0, keepdims=True)
    e2 = jnp.exp(m2 - m1)
    inv = 1.0 / (1.0 + e2)
    w1 = p_sel * inv
    w2 = p_sel * (e2 * inv)
    eid1 = g_sel * EXPERTS_PER_GROUP + i1
    eid2 = g_sel * EXPERTS_PER_GROUP + i2

    rowe = lax.broadcasted_iota(jnp.int32, (N_EXPERTS, tm), 0)
    oh = jnp.where((rowe == eid1) | (rowe == eid2), 1.0, 0.0).astype(BF16)
    run = run_scr[...]
    pref = jnp.dot(oh, tri_scr[...], preferred_element_type=F32) + jnp.concatenate([run] * (tm // LANES), axis=1)
    r1 = jnp.sum(jnp.where(rowe == eid1, pref, 0.0), axis=0, keepdims=True)
    r2 = jnp.sum(jnp.where(rowe == eid2, pref, 0.0), axis=0, keepdims=True)
    run = run + jnp.dot(oh, jnp.ones((tm, LANES), BF16), preferred_element_type=F32)
    run_scr[...] = run
    cnt_ref[...] = run

    out = jnp.zeros((SUBLANES, tm), F32)
    for k, v in enumerate([eid1.astype(F32), eid2.astype(F32), w1, w2, r1, r2]):
        out = jnp.where(row8 == k, v, out)
    rt_ref[...] = out
    rowl = lax.broadcasted_iota(jnp.int32, (LANES, tm), 0)
    wcol_ref[...] = jnp.where(rowl == 0, w1, jnp.where(rowl == 1, w2, 0.0)).T


def _post(omix, x2d, w_out_bf, g2, w_router_t_bf, base_cnt, tm):
    t_rows = x2d.shape[0]
    assert tm % LANES == 0
    row = lambda i: (i, 0)
    const = lambda i: (0, 0)
    return pl.pallas_call(
        _post_kernel,
        grid=(t_rows // tm,),
        in_specs=[pl.BlockSpec((tm, MIX_WIDTH), row),
                  pl.BlockSpec((tm, D_MODEL), row),
                  pl.BlockSpec((MIX_WIDTH, D_MODEL), const),
                  pl.BlockSpec((1, D_MODEL), const),
                  pl.BlockSpec((ROUTER_ROWS, D_MODEL), const),
                  pl.BlockSpec((N_EXPERTS, LANES), const)],
        out_specs=[pl.BlockSpec((tm, D_MODEL), row),
                   pl.BlockSpec((tm, D_MODEL), row),
                   pl.BlockSpec((tm, LANES), row),
                   pl.BlockSpec((SUBLANES, tm), lambda i: (0, i)),
                   pl.BlockSpec((N_EXPERTS, LANES), const)],
        out_shape=[jax.ShapeDtypeStruct((t_rows, D_MODEL), F32),
                   jax.ShapeDtypeStruct((t_rows, D_MODEL), F32),
                   jax.ShapeDtypeStruct((t_rows, LANES), F32),
                   jax.ShapeDtypeStruct((SUBLANES, t_rows), F32),
                   jax.ShapeDtypeStruct((N_EXPERTS, LANES), F32)],
        scratch_shapes=[pltpu.VMEM((tm, tm), BF16), pltpu.VMEM((N_EXPERTS, LANES), F32)],
        compiler_params=_params(1),
        name="post",
    )(omix, x2d, w_out_bf, g2, w_router_t_bf, base_cnt)


def _step_major(pos, tm):
    return [pos[e].reshape(-1, 1, tm) for e in range(2)]


def _dispatch_kernel(pos0_ref, pos1_ref, *refs, tm, group_steps):
    pos_refs = (pos0_ref, pos1_ref)
    hn_refs = refs[:len(group_steps)]
    xs_ref, sem = refs[len(group_steps):]
    i = pl.program_id(0)

    first = 0
    for hn_ref, steps in zip(hn_refs, group_steps):
        @pl.when((i >= first) & (i < first + steps))
        def _rows(hn_ref=hn_ref):
            def body(blk, carry):
                for k in range(SUBLANES):
                    r = blk * SUBLANES + k
                    for e in range(2):
                        pltpu.make_async_copy(hn_ref.at[blk, pl.ds(k, 1)], xs_ref.at[pl.ds(pos_refs[e][0, 0, r], 1)],
                                              sem).start(priority=e)
                return carry

            lax.fori_loop(0, tm // SUBLANES, body, 0)
        first += steps

    pltpu.make_async_copy(xs_ref.at[pl.ds(0, 2 * tm)], xs_ref.at[pl.ds(0, 2 * tm)], sem).wait()


def _dispatch(pos_list, hn_list, tm):
    group_steps = tuple(hn.shape[0] // tm for hn in hn_list)
    n_steps = sum(group_steps)
    assert tm % SUBLANES == 0
    pos3 = [jnp.concatenate(parts, axis=0) for parts in zip(*[_step_major(p, tm) for p in pos_list])]
    smem = pl.BlockSpec((1, 1, tm), lambda i: (i, 0, 0), memory_space=pltpu.SMEM)
    hn_specs = []
    first = 0
    for steps in group_steps:
        hn_specs.append(pl.BlockSpec((tm // SUBLANES, SUBLANES, D_MODEL),
                                     lambda i, first=first, steps=steps: (jnp.clip(i - first, 0, steps - 1), 0, 0)))
        first += steps
    n_rows_out = 2 * sum(hn.shape[0] for hn in hn_list)
    hn_list = [hn.reshape(hn.shape[0] // SUBLANES, SUBLANES, D_MODEL) for hn in hn_list]
    return pl.pallas_call(
        functools.partial(_dispatch_kernel, tm=tm, group_steps=group_steps),
        grid=(n_steps,),
        in_specs=[smem, smem] + hn_specs,
        out_specs=pl.BlockSpec(memory_space=pl.ANY),
        out_shape=jax.ShapeDtypeStruct((n_rows_out, D_MODEL), F32),
        scratch_shapes=[pltpu.SemaphoreType.DMA(())],
        compiler_params=_params(1),
        name="dispatch",
    )(*pos3, *hn_list)


def _expert_kernel(vt_ref, ve_ref, lo_ref, hi_ref, nv_ref, x_ref, wg_ref, wu_ref, wd_ref, y_ref, wgu_scr, wd_scr):
    v = pl.program_id(0)

    @pl.when(v < nv_ref[0])
    def _compute():
        @pl.when((v == 0) | (ve_ref[v] != ve_ref[jnp.maximum(v - 1, 0)]))
        def _new_expert():
            wgu_scr[:, 0:EXPERT_FF] = wg_ref[0].astype(BF16)
            wgu_scr[:, EXPERT_FF:2 * EXPERT_FF] = wu_ref[0].astype(BF16)
            wd_scr[...] = wd_ref[0].astype(BF16)

        x = x_ref[...].astype(BF16)
        gu = jnp.dot(x, wgu_scr[...], preferred_element_type=F32)
        g = gu[:, 0:EXPERT_FF]
        a = (g * jax.nn.sigmoid(g) * gu[:, EXPERT_FF:2 * EXPERT_FF]).astype(BF16)
        y = jnp.dot(a, wd_scr[...], preferred_element_type=F32)
        lo, hi = lo_ref[v], hi_ref[v]

        @pl.when(lo == 0)
        def _first_visit():
            y_ref[...] = y

        @pl.when(lo > 0)
        def _later_visit():
            row = lax.broadcasted_iota(jnp.int32, y.shape, 0)
            y_ref[...] = jnp.where((row >= lo) & (row < hi), y, y_ref[...])


def _experts(vis_tile, vis_expert, vis_lo, vis_hi, n_vis, xs, w_gate, w_up, w_down, tm):
    n_steps = vis_tile.shape[0]
    last = lambda v, nv: jnp.minimum(v, nv[0] - 1)
    of_expert = lambda v, vt, ve, lo, hi, nv: (ve[last(v, nv)], 0, 0)
    return pl.pallas_call(
        _expert_kernel,
        grid_spec=pltpu.PrefetchScalarGridSpec(
            num_scalar_prefetch=5,
            grid=(n_steps,),
            in_specs=[pl.BlockSpec((tm, D_MODEL), lambda v, vt, ve, lo, hi, nv: (vt[last(v, nv)], 0)),
                      pl.BlockSpec((1, D_MODEL, EXPERT_FF), of_expert),
                      pl.BlockSpec((1, D_MODEL, EXPERT_FF), of_expert),
                      pl.BlockSpec((1, EXPERT_FF, D_MODEL), of_expert)],
            out_specs=pl.BlockSpec((tm, D_MODEL), lambda v, vt, ve, lo, hi, nv: (vt[last(v, nv)], 0)),
            scratch_shapes=[pltpu.VMEM((D_MODEL, 2 * EXPERT_FF), BF16), pltpu.VMEM((EXPERT_FF, D_MODEL), BF16)]),
        out_shape=jax.ShapeDtypeStruct(xs.shape, F32),
        compiler_params=_params(1),
        name="experts",
    )(vis_tile, vis_expert, vis_lo, vis_hi, n_vis, xs, w_gate, w_up, w_down)


def _combine_kernel(pos0_ref, pos1_ref, nxt0_ref, nxt1_ref, ys_ref, xmid_ref, wcol_ref, out_ref, ybuf, sems,
                    *, tm, n_steps):
    i = pl.program_id(0)
    slot = i % 2

    def issue(p_refs, to_slot):
        def body(blk, carry):
            for k in range(SUBLANES):
                r = blk * SUBLANES + k
                for e in range(2):
                    pltpu.make_async_copy(ys_ref.at[pl.ds(p_refs[e][0, 0, r], 1)],
                                          ybuf.at[to_slot, e, blk, pl.ds(k, 1)], sems.at[to_slot]).start(priority=e)
            return carry

        lax.fori_loop(0, tm // SUBLANES, body, 0)

    @pl.when(i == 0)
    def _first():
        issue((pos0_ref, pos1_ref), 0)

    @pl.when(i + 1 < n_steps)
    def _ahead():
        issue((nxt0_ref, nxt1_ref), 1 - slot)

    for e in range(2):
        pltpu.make_async_copy(ybuf.at[slot, e], ybuf.at[slot, e], sems.at[slot]).wait()
    w = wcol_ref[...]
    y0 = ybuf[slot, 0].reshape(tm, D_MODEL)
    y1 = ybuf[slot, 1].reshape(tm, D_MODEL)
    out_ref[...] = xmid_ref[...] + w[:, 0:1] * y0 + w[:, 1:2] * y1


def _combine(pos, ys, xmid, wcol, tm):
    t_rows = xmid.shape[0]
    n_steps = t_rows // tm
    assert tm % SUBLANES == 0
    pos3 = _step_major(pos, tm)
    row = lambda i: (i, 0)
    cur = pl.BlockSpec((1, 1, tm), lambda i: (i, 0, 0), memory_space=pltpu.SMEM)
    nxt = pl.BlockSpec((1, 1, tm), lambda i: (jnp.minimum(i + 1, n_steps - 1), 0, 0), memory_space=pltpu.SMEM)
    return pl.pallas_call(
        functools.partial(_combine_kernel, tm=tm, n_steps=n_steps),
        grid=(n_steps,),
        in_specs=[cur, cur, nxt, nxt,
                  pl.BlockSpec(memory_space=pl.ANY),
                  pl.BlockSpec((tm, D_MODEL), row),
                  pl.BlockSpec((tm, LANES), row)],
        out_specs=pl.BlockSpec((tm, D_MODEL), row),
        out_shape=jax.ShapeDtypeStruct((t_rows, D_MODEL), F32),
        scratch_shapes=[pltpu.VMEM((2, 2, tm // SUBLANES, SUBLANES, D_MODEL), F32), pltpu.SemaphoreType.DMA((2,))],
        compiler_params=_params(1),
        name="combine",
    )(*pos3, *pos3, ys, xmid, wcol)


def _bucket(ends, idx):
    n = jnp.sum((ends[None, :] <= idx[:, None]).astype(jnp.int32), axis=1)
    return jnp.minimum(n, ends.shape[0] - 1)


def _tile_for(rows, pref):
    tm = min(pref, rows)
    assert rows % tm == 0
    return tm


def kernel(x_prompt, x_sample, cache_ret_state, cache_swa_k, cache_swa_v, meta_tokens, norm1_g, w_in, q_norm_g,
           k_norm_g, ret_norm_g, attn_sinks, w_out, norm2_g, w_group, w_expert, w_gate, w_up, w_down):
    assert norm1_g.shape[0] == 1, "single-layer trunk"
    bp, lp, _ = x_prompt.shape
    bs, ls, _ = x_sample.shape
    n_pairs = RET_HEADS // 2

    g1 = norm1_g[0][None, :]
    g2 = norm2_g[0][None, :]
    w_in_bf = w_in[0].astype(BF16)
    w_out_bf = w_out[0].astype(BF16)
    qg2 = jnp.tile(q_norm_g[0], 2)[None, :]
    kg2 = jnp.tile(k_norm_g[0], 2)[None, :]
    rng = ret_norm_g[0].reshape(RET_HEADS, RET_DV)
    sink_tab = jnp.broadcast_to(jnp.repeat(attn_sinks[0], CHUNK).reshape(SWA_KV_HEADS, 1, 4 * CHUNK),
                                (SWA_KV_HEADS, SUBLANES, 4 * CHUNK))
    w_router_t = jnp.zeros((ROUTER_ROWS, D_MODEL), F32)
    w_router_t = w_router_t.at[0:N_GROUPS].set(w_group[0].T)
    w_router_t = w_router_t.at[ROUTER_EXPERT_ROW0:ROUTER_EXPERT_ROW0 + N_EXPERTS].set(w_expert[0].T)
    w_router_t_bf = w_router_t.astype(BF16)

    meta_rows = 2 * CHUNK
    m_pad = jnp.zeros((meta_rows, D_MODEL), F32).at[0:N_META].set(meta_tokens)
    m_rqk, m_rv, _, _, m_skv = _proj(m_pad, jnp.arange(meta_rows, dtype=jnp.int32), meta_rows, g1, w_in_bf, qg2, kg2)
    s_meta = _meta_state(m_rqk, m_rv)[None]
    meta_kv = m_skv[0:N_META]

    groups = [
        dict(x=x_prompt.reshape(bp * lp, D_MODEL), n=bp, seq=lp, pos0=N_META, has_hist=False, s0=s_meta,
             hist=jnp.zeros((1, WINDOW, 2 * SWA_KV), F32)),
        dict(x=x_sample.reshape(bs * ls, D_MODEL), n=bs, seq=ls, pos0=N_META + PAST_LEN, has_hist=True,
             s0=cache_ret_state[0].reshape(bs, n_pairs, 2 * RET_DK, RET_DV),
             hist=jnp.concatenate([cache_swa_k[0].reshape(bs, WINDOW, SWA_KV),
                                   cache_swa_v[0].reshape(bs, WINDOW, SWA_KV)], axis=-1)),
    ]

    base_cnt = jnp.zeros((N_EXPERTS, LANES), F32)
    for g in groups:
        rows = g["n"] * g["seq"]
        tm = _tile_for(rows, PROJ_TILE)
        pos = g["pos0"] + jnp.arange(g["seq"], dtype=jnp.int32)
        if g["seq"] < tm:
            assert tm % g["seq"] == 0
            pos = jnp.tile(pos, tm // g["seq"])
        else:
            assert g["seq"] % tm == 0
        rqk, rv, gate, sq, skv = _proj(g["x"], pos, tm, g1, w_in_bf, qg2, kg2)
        tl = min(ATTN_TILE, g["seq"])
        omix, s_out, kv_out = _attention(rqk, rv, gate, sq, skv, meta_kv, g["hist"], g["s0"], rng, sink_tab,
                                         n_streams=g["n"], seq=g["seq"], tl=tl, has_hist=g["has_hist"])
        xmid, hn, wcol, route_t, base_cnt = _post(omix, g["x"], w_out_bf, g2, w_router_t_bf, base_cnt,
                                                  _tile_for(rows, POST_TILE))
        g.update(xmid=xmid, hn=hn, wcol=wcol, route_t=route_t, s_out=s_out, kv_out=kv_out)

    te = EXPERT_TILE
    total_rows = sum(g["n"] * g["seq"] for g in groups)
    assert (2 * total_rows) % te == 0
    n_row_tiles = (2 * total_rows) // te
    counts = base_cnt[:, 0].astype(jnp.int32)
    off = jnp.cumsum(counts) - counts
    first_tile = off // te
    n_vis_e = jnp.where(counts > 0, (off + counts - 1) // te - first_tile + 1, 0)
    vis_end = jnp.cumsum(n_vis_e)
    n_vis = vis_end[-1:].astype(jnp.int32)
    v = jnp.arange(n_row_tiles + N_EXPERTS, dtype=jnp.int32)
    vis_expert = _bucket(vis_end, v)
    pick = lambda table: jnp.sum(jnp.where(vis_expert[:, None] == jnp.arange(N_EXPERTS, dtype=jnp.int32)[None, :],
                                           table[None, :], 0), axis=1)
    vis_tile = jnp.clip(pick(first_tile) + v - pick(vis_end - n_vis_e), 0, n_row_tiles - 1).astype(jnp.int32)
    vis_lo = jnp.clip(pick(off) - vis_tile * te, 0, te).astype(jnp.int32)
    vis_hi = jnp.clip(pick(off + counts) - vis_tile * te, 0, te).astype(jnp.int32)

    for g in groups:
        eid = g["route_t"][0:2].astype(jnp.int32)
        off_sel = jnp.sum(jnp.where(eid[None] == jnp.arange(N_EXPERTS, dtype=jnp.int32)[:, None, None],
                                    off[:, None, None], 0), axis=0)
        g["pos"] = (off_sel + g["route_t"][4:6].astype(jnp.int32)).astype(jnp.int32)
    xs = _dispatch([g["pos"] for g in groups], [g["hn"] for g in groups], MOVE_TILE)

    ys = _experts(vis_tile, vis_expert.astype(jnp.int32), vis_lo, vis_hi, n_vis, xs, w_gate[0], w_up[0], w_down[0], te)

    outs = []
    for g in groups:
        rows = g["n"] * g["seq"]
        y = _combine(g["pos"], ys, g["xmid"], g["wcol"], _tile_for(rows, MOVE_TILE))
        outs.append(y.reshape(g["n"], g["seq"], D_MODEL))

    def caches(g):
        kv = g["kv_out"]
        k = kv[:, :, 0:SWA_KV].reshape(g["n"], WINDOW, SWA_KV_HEADS, SWA_HD)[None]
        v = kv[:, :, SWA_KV:2 * SWA_KV].reshape(g["n"], WINDOW, SWA_KV_HEADS, SWA_HD)[None]
        s = g["s_out"].reshape(g["n"], RET_HEADS, RET_DK, RET_DV)[None]
        return s, k, v

    sp, kp, vp = caches(groups[0])
    ss, ks, vs = caches(groups[1])
    return (outs[0], outs[1], sp, kp, vp, ss, ks, vs)
```

```python
import functools

import numpy as np
import jax
import jax.numpy as jnp
from jax import lax
from jax.experimental import pallas as pl
from jax.experimental.pallas import tpu as pltpu

F32 = jnp.float32
BF16 = jnp.bfloat16

D_MODEL = 1024
PAST_LEN = 4096
CHUNK = 64
N_META = 16
RET_HEADS = 4
RET_DK = 64
RET_DV = 128
SWA_HEADS = 8
SWA_KV_HEADS = 2
SWA_HD = 64
WINDOW = 128
ROPE_THETA = 10000.0
N_GROUPS = 4
EXPERTS_PER_GROUP = 8
N_EXPERTS = N_GROUPS * EXPERTS_PER_GROUP
EXPERT_FF = 256
EPS = 1e-6
NEG_INF = -1e30
LOG2E = float(np.log2(np.e))
RET_Q = RET_HEADS * RET_DK
RET_V = RET_HEADS * RET_DV
SWA_Q = SWA_HEADS * SWA_HD
SWA_KV = SWA_KV_HEADS * SWA_HD
MIX_WIDTH = RET_V + SWA_Q
IN_WIDTH = 2 * RET_Q + 2 * RET_V + SWA_Q + 2 * SWA_KV

LANES = 128
PROJ_TILE = 1024
POST_TILE = 512
ATTN_TILE = 256
EXPERT_TILE = 512
MOVE_TILE = 256
SUBLANES = 8
ROUTER_EXPERT_ROW0 = 8
ROUTER_ROWS = 64
META_ROWS = 64
VMEM_LIMIT = 56 * 1024 * 1024

_LOG_G = [float(np.log1p(-np.exp2(-5.0 - h))) for h in range(RET_HEADS)]


def _params(n_axes):
    return pltpu.CompilerParams(dimension_semantics=("arbitrary",) * n_axes, vmem_limit_bytes=VMEM_LIMIT)


def _split_bf16(a):
    hi = a.astype(BF16)
    return hi, (a - hi.astype(F32)).astype(BF16)


def _split_dot(a, w2):
    hi, lo = _split_bf16(a)
    return jnp.dot(jnp.concatenate([hi, lo], axis=1), w2, preferred_element_type=F32)


def _lane_sum(a):
    return _split_dot(a, jnp.ones((2 * LANES, LANES), BF16))


def _head_sum_matrix():
    i = lax.broadcasted_iota(jnp.int32, (2 * LANES, LANES), 0) % LANES
    j = lax.broadcasted_iota(jnp.int32, (2 * LANES, LANES), 1)
    return jnp.where((i < SWA_HD) == (j < SWA_HD), 1.0, 0.0).astype(BF16)


def _rope(t, c, s1, s2):
    half = SWA_HD // 2
    return t * c + pltpu.roll(t, LANES - half, 1) * s1 + pltpu.roll(t, half, 1) * s2


def _head_rms(t, g, head_w):
    ms = _split_dot(t * t, head_w) * (1.0 / SWA_HD)
    return t * lax.rsqrt(ms + EPS) * g


def _row_rms(x):
    n_tiles = x.shape[1] // LANES
    ss = x[:, 0:LANES] * x[:, 0:LANES]
    for j in range(1, n_tiles):
        ss = ss + x[:, j * LANES:(j + 1) * LANES] * x[:, j * LANES:(j + 1) * LANES]
    r = lax.rsqrt(_lane_sum(ss) * (1.0 / x.shape[1]) + EPS)
    return x * jnp.concatenate([r] * n_tiles, axis=1)


def _proj_kernel(x_ref, g1_ref, w_ref, qg_ref, kg_ref, cos_ref, s1_ref, s2_ref,
                 rqk_ref, rv_ref, gate_ref, sq_ref, skv_ref):
    xn = (_row_rms(x_ref[...]) * g1_ref[...]).astype(BF16)
    c, s1, s2 = cos_ref[...], s1_ref[...], s2_ref[...]
    head_w = _head_sum_matrix()

    def seg(a, b):
        return jnp.dot(xn, w_ref[:, a:b], preferred_element_type=F32)

    def tile(h, j):
        return h[:, j * LANES:(j + 1) * LANES]

    h = seg(0, 2 * RET_Q)
    for j in range(2):
        rqk_ref[:, j * LANES:(j + 1) * LANES] = _rope(tile(h, j), c, s1, s2).astype(BF16)
    for j in range(2, 4):
        rqk_ref[:, j * LANES:(j + 1) * LANES] = (_rope(tile(h, j), c, s1, s2) * (RET_DK ** -0.5)).astype(BF16)
    a = 2 * RET_Q
    rv_ref[...] = seg(a, a + RET_V).astype(BF16)
    a += RET_V
    g = seg(a, a + RET_V)
    gate_ref[...] = (g * jax.nn.sigmoid(g)).astype(BF16)
    a += RET_V
    h = seg(a, a + SWA_Q)
    qg = qg_ref[...]
    for j in range(SWA_Q // LANES):
        sq_ref[:, j * LANES:(j + 1) * LANES] = _rope(_head_rms(tile(h, j), qg, head_w), c, s1, s2).astype(BF16)
    a += SWA_Q
    h = seg(a, a + 2 * SWA_KV)
    skv_ref[:, 0:LANES] = _rope(_head_rms(tile(h, 0), kg_ref[...], head_w), c, s1, s2)
    skv_ref[:, LANES:2 * LANES] = tile(h, 1)


def _rope_tables(pos):
    half = SWA_HD // 2
    inv = ROPE_THETA ** (-jnp.arange(half, dtype=F32) / half)
    ang = pos.astype(F32)[:, None] * inv[None, :]
    cos, sin = jnp.cos(ang), jnp.sin(ang)
    z = jnp.zeros_like(sin)
    return (jnp.tile(cos, (1, 4)),
            jnp.tile(jnp.concatenate([-sin, z], axis=1), (1, 2)),
            jnp.tile(jnp.concatenate([z, sin], axis=1), (1, 2)))


def _proj(x2d, pos_rows, tm, g1, w_in_bf, qg2, kg2):
    t_rows = x2d.shape[0]
    n_tiles = t_rows // tm
    n_pos_tiles = pos_rows.shape[0] // tm
    cos, s1, s2 = _rope_tables(pos_rows)
    row = lambda i: (i, 0)
    const = lambda i: (0, 0)
    tab = lambda i: (i % n_pos_tiles, 0)
    return pl.pallas_call(
        _proj_kernel,
        grid=(n_tiles,),
        in_specs=[pl.BlockSpec((tm, D_MODEL), row),
                  pl.BlockSpec((1, D_MODEL), const),
                  pl.BlockSpec((D_MODEL, IN_WIDTH), const),
                  pl.BlockSpec((1, LANES), const),
                  pl.BlockSpec((1, LANES), const),
                  pl.BlockSpec((tm, LANES), tab),
                  pl.BlockSpec((tm, LANES), tab),
                  pl.BlockSpec((tm, LANES), tab)],
        out_specs=[pl.BlockSpec((tm, 2 * RET_Q), row),
                   pl.BlockSpec((tm, RET_V), row),
                   pl.BlockSpec((tm, RET_V), row),
                   pl.BlockSpec((tm, SWA_Q), row),
                   pl.BlockSpec((tm, 2 * SWA_KV), row)],
        out_shape=[jax.ShapeDtypeStruct((t_rows, 2 * RET_Q), BF16),
                   jax.ShapeDtypeStruct((t_rows, RET_V), BF16),
                   jax.ShapeDtypeStruct((t_rows, RET_V), BF16),
                   jax.ShapeDtypeStruct((t_rows, SWA_Q), BF16),
                   jax.ShapeDtypeStruct((t_rows, 2 * SWA_KV), F32)],
        compiler_params=_params(1),
        name="proj",
    )(x2d, g1, w_in_bf, qg2, kg2, cos, s1, s2)


def _pair_update(k_bf, v0_bf, v1_bf, wt):
    kw = (k_bf.astype(F32) * wt).astype(BF16)
    dn = (((0,), (0,)), ((), ()))
    a0 = lax.dot_general(kw, v0_bf, dn, preferred_element_type=F32)
    a1 = lax.dot_general(kw, v1_bf, dn, preferred_element_type=F32)
    top = lax.broadcasted_iota(jnp.int32, a0.shape, 0) < RET_DK
    return jnp.where(top, a0, a1)


def _decay_rows(n, pair, rows_back_from):
    i = lax.broadcasted_iota(jnp.int32, (n, LANES), 0).astype(F32)
    lane = lax.broadcasted_iota(jnp.int32, (n, LANES), 1)
    lg = jnp.where(lane < RET_DK, _LOG_G[2 * pair], _LOG_G[2 * pair + 1])
    return jnp.exp((rows_back_from - i) * lg)


def _meta_state_kernel(rqk_ref, rv_ref, s_ref, *, n_rows):
    for p in range(RET_HEADS // 2):
        k = rqk_ref[:, RET_Q + p * LANES:RET_Q + (p + 1) * LANES]
        wt = _decay_rows(n_rows, p, float(N_META - 1))
        s_ref[p] = _pair_update(k, rv_ref[:, (2 * p) * LANES:(2 * p + 1) * LANES],
                                rv_ref[:, (2 * p + 1) * LANES:(2 * p + 2) * LANES], wt)


def _meta_state(m_rqk, m_rv):
    n_rows = m_rqk.shape[0]
    return pl.pallas_call(
        functools.partial(_meta_state_kernel, n_rows=n_rows),
        out_shape=jax.ShapeDtypeStruct((RET_HEADS // 2, 2 * RET_DK, RET_DV), F32),
        name="meta_state",
    )(m_rqk, m_rv)


def _dup_halves(a, lo_mask):
    sw = pltpu.roll(a, SWA_HD, 1)
    return jnp.where(lo_mask, a, sw), jnp.where(lo_mask, sw, a)


def _attn_kernel(rqk_ref, rv_ref, gate_ref, sq_ref, skv_ref, meta_ref, hist_ref, s0_ref, rng_ref, sink_ref,
                 omix_ref, sout_ref, kvout_ref,
                 s_scr, kd_scr, vd_scr, mk_scr, mv_scr, dec_scr, wt_scr, cs_scr, gam_scr,
                 *, tl, has_hist):
    b = pl.program_id(0)
    t = pl.program_id(1)
    nt = pl.num_programs(1)
    n_chunks = tl // CHUNK
    n_pairs = RET_HEADS // 2
    lo_tl = lax.broadcasted_iota(jnp.int32, (tl, LANES), 1) < SWA_HD
    lo_c = lax.broadcasted_iota(jnp.int32, (CHUNK, LANES), 1) < SWA_HD

    @pl.when((b == 0) & (t == 0))
    def _tables():
        i = lax.broadcasted_iota(jnp.int32, (tl, tl), 0)
        j = lax.broadcasted_iota(jnp.int32, (tl, tl), 1)
        diff = (i - j).astype(F32)
        row = lax.broadcasted_iota(jnp.int32, (tl, LANES), 0).astype(F32)
        for h in range(RET_HEADS):
            dec_scr[h] = jnp.where(diff >= 0.0, jnp.exp(jnp.maximum(diff, 0.0) * _LOG_G[h]), 0.0)
            cs_scr[h] = jnp.exp((row + 1.0) * _LOG_G[h])
        top = lax.broadcasted_iota(jnp.int32, (2 * RET_DK, RET_DV), 0) < RET_DK
        for p in range(n_pairs):
            wt_scr[p] = _decay_rows(tl, p, float(tl - 1))
            gam_scr[p] = jnp.where(top, jnp.exp(jnp.float32(tl * _LOG_G[2 * p])), jnp.exp(jnp.float32(tl * _LOG_G[2 * p + 1])))
        lo_m = lax.broadcasted_iota(jnp.int32, (N_META, LANES), 1) < SWA_HD
        mk0, mk1 = _dup_halves(meta_ref[:, 0:LANES], lo_m)
        mv0, mv1 = _dup_halves(meta_ref[:, LANES:2 * LANES], lo_m)
        mk_scr[...] = jnp.zeros(mk_scr.shape, BF16)
        mv_scr[...] = jnp.zeros(mv_scr.shape, BF16)
        mk_scr[0, 0:N_META] = mk0.astype(BF16)
        mk_scr[1, 0:N_META] = mk1.astype(BF16)
        mv_scr[0, 0:N_META] = mv0.astype(BF16)
        mv_scr[1, 0:N_META] = mv1.astype(BF16)

    @pl.when(t == 0)
    def _stream_start():
        s_scr[...] = s0_ref[0]
        if has_hist:
            lo_w = lax.broadcasted_iota(jnp.int32, (WINDOW, LANES), 1) < SWA_HD
            k0, k1 = _dup_halves(hist_ref[0, :, 0:LANES], lo_w)
            v0, v1 = _dup_halves(hist_ref[0, :, LANES:2 * LANES], lo_w)
            kd_scr[0, 0:WINDOW] = k0.astype(BF16)
            kd_scr[1, 0:WINDOW] = k1.astype(BF16)
            vd_scr[0, 0:WINDOW] = v0.astype(BF16)
            vd_scr[1, 0:WINDOW] = v1.astype(BF16)
        else:
            z = jnp.zeros((WINDOW, LANES), BF16)
            for kv in range(SWA_KV_HEADS):
                kd_scr[kv, 0:WINDOW] = z
                vd_scr[kv, 0:WINDOW] = z

    k0, k1 = _dup_halves(skv_ref[:, 0:LANES], lo_tl)
    v0, v1 = _dup_halves(skv_ref[:, LANES:2 * LANES], lo_tl)
    kd_scr[0, WINDOW:WINDOW + tl] = k0.astype(BF16)
    kd_scr[1, WINDOW:WINDOW + tl] = k1.astype(BF16)
    vd_scr[0, WINDOW:WINDOW + tl] = v0.astype(BF16)
    vd_scr[1, WINDOW:WINDOW + tl] = v1.astype(BF16)

    band = WINDOW + CHUNK
    n_keys = META_ROWS + band
    n_q = 4 * CHUNK
    scale2 = (SWA_HD ** -0.5) * LOG2E
    krow = lax.broadcasted_iota(jnp.int32, (n_keys, n_q), 0)
    zero_c = jnp.zeros((CHUNK, LANES), BF16)
    ones_v = jnp.ones((n_keys, LANES), BF16)
    for c in range(n_chunks):
        if has_hist:
            first_valid = META_ROWS
        else:
            first_valid = jnp.where(t == 0, max(META_ROWS + WINDOW - c * CHUNK, META_ROWS), META_ROWS)
        valid_t = (krow < N_META) | (krow >= first_valid)
        r0 = c * CHUNK
        for kv in range(SWA_KV_HEADS):
            keys = jnp.concatenate([mk_scr[kv], kd_scr[kv, r0:r0 + band]], axis=0)
            vals = jnp.concatenate([mv_scr[kv], vd_scr[kv, r0:r0 + band]], axis=0)
            qa = sq_ref[r0:r0 + CHUNK, (2 * kv) * LANES:(2 * kv + 1) * LANES]
            qb = sq_ref[r0:r0 + CHUNK, (2 * kv + 1) * LANES:(2 * kv + 2) * LANES]
            lhs = jnp.concatenate([jnp.where(lo_c, qa, zero_c), jnp.where(lo_c, zero_c, qa),
                                   jnp.where(lo_c, qb, zero_c), jnp.where(lo_c, zero_c, qb)], axis=0)
            s_t = lax.dot_general(keys, lhs, (((1,), (1,)), ((), ())), preferred_element_type=F32) * scale2
            s_t = jnp.where(valid_t, s_t, NEG_INF)
            s_t = jnp.where(krow == N_META, sink_ref[kv, 0:1, :] * LOG2E, s_t)
            e_t = jnp.exp2(s_t - jnp.max(s_t, axis=0, keepdims=True)).astype(BF16)
            ov = lax.dot_general(e_t, jnp.concatenate([vals, ones_v], axis=1), (((0,), (0,)), ((), ())),
                                 preferred_element_type=F32)
            o = ov[:, 0:LANES] * (1.0 / ov[:, LANES:2 * LANES])
            oa = jnp.where(lo_c, o[0:CHUNK], o[CHUNK:2 * CHUNK])
            ob = jnp.where(lo_c, o[2 * CHUNK:3 * CHUNK], o[3 * CHUNK:4 * CHUNK])
            base = RET_V + (2 * kv) * LANES
            omix_ref[r0:r0 + CHUNK, base:base + LANES] = oa.astype(BF16)
            omix_ref[r0:r0 + CHUNK, base + LANES:base + 2 * LANES] = ob.astype(BF16)

    zero_t = jnp.zeros((tl, LANES), BF16)
    for p in range(n_pairs):
        q = rqk_ref[:, p * LANES:(p + 1) * LANES]
        k = rqk_ref[:, RET_Q + p * LANES:RET_Q + (p + 1) * LANES]
        lhs = jnp.concatenate([jnp.where(lo_tl, q, zero_t), jnp.where(lo_tl, zero_t, q)], axis=0)
        s = lax.dot_general(lhs, k, (((1,), (1,)), ((), ())), preferred_element_type=F32)
        cross = jnp.dot(lhs, s_scr[p].astype(BF16), preferred_element_type=F32)
        for i in range(2):
            h = 2 * p + i
            v = rv_ref[:, h * LANES:(h + 1) * LANES]
            a = (s[i * tl:(i + 1) * tl] * dec_scr[h]).astype(BF16)
            o = jnp.dot(a, v, preferred_element_type=F32) + cross[i * tl:(i + 1) * tl] * cs_scr[h]
            r = o * lax.rsqrt(_lane_sum(o * o) * (1.0 / RET_DV) + EPS) * rng_ref[h:h + 1, :]
            omix_ref[:, h * LANES:(h + 1) * LANES] = (r * gate_ref[:, h * LANES:(h + 1) * LANES].astype(F32)).astype(BF16)
        u = _pair_update(k, rv_ref[:, (2 * p) * LANES:(2 * p + 1) * LANES],
                         rv_ref[:, (2 * p + 1) * LANES:(2 * p + 2) * LANES], wt_scr[p])
        s_scr[p] = gam_scr[p] * s_scr[p] + u

    if tl >= WINDOW:
        @pl.when(t + 1 < nt)
        def _carry_window():
            for kv in range(SWA_KV_HEADS):
                kd_scr[kv, 0:WINDOW] = kd_scr[kv, tl:tl + WINDOW]
                vd_scr[kv, 0:WINDOW] = vd_scr[kv, tl:tl + WINDOW]

    @pl.when(t + 1 == nt)
    def _stream_end():
        sout_ref[0] = s_scr[...]
        if tl >= WINDOW:
            kvout_ref[0] = skv_ref[tl - WINDOW:tl, :]
        else:
            kvout_ref[0, 0:WINDOW - tl] = hist_ref[0, tl:WINDOW, :]
            kvout_ref[0, WINDOW - tl:WINDOW] = skv_ref[...]


def _attention(rqk, rv, gate, sq, skv, meta_kv, hist_kv, s0, rng, sink_tab, *, n_streams, seq, tl, has_hist):
    nt = seq // tl
    assert tl % CHUNK == 0 and seq % tl == 0
    assert tl >= WINDOW or (nt == 1 and has_hist)
    n_pairs = RET_HEADS // 2
    s0_shared = s0.shape[0] == 1
    row = lambda b, t: (b * nt + t, 0)
    const2 = lambda b, t: (0, 0)
    const3 = lambda b, t: (0, 0, 0)
    per_b3 = lambda b, t: (b, 0, 0)
    s0_map = (lambda b, t: (0, 0, 0, 0)) if s0_shared else (lambda b, t: (b, 0, 0, 0))
    hist_map = per_b3 if has_hist else const3
    rows = n_streams * seq
    return pl.pallas_call(
        functools.partial(_attn_kernel, tl=tl, has_hist=has_hist),
        grid=(n_streams, nt),
        in_specs=[pl.BlockSpec((tl, 2 * RET_Q), row),
                  pl.BlockSpec((tl, RET_V), row),
                  pl.BlockSpec((tl, RET_V), row),
                  pl.BlockSpec((tl, SWA_Q), row),
                  pl.BlockSpec((tl, 2 * SWA_KV), row),
                  pl.BlockSpec((N_META, 2 * SWA_KV), const2),
                  pl.BlockSpec((1, WINDOW, 2 * SWA_KV), hist_map),
                  pl.BlockSpec((1, n_pairs, 2 * RET_DK, RET_DV), s0_map),
                  pl.BlockSpec((RET_HEADS, RET_DV), const2),
                  pl.BlockSpec((SWA_KV_HEADS, SUBLANES, 4 * CHUNK), const3)],
        out_specs=[pl.BlockSpec((tl, MIX_WIDTH), row),
                   pl.BlockSpec((1, n_pairs, 2 * RET_DK, RET_DV), lambda b, t: (b, 0, 0, 0)),
                   pl.BlockSpec((1, WINDOW, 2 * SWA_KV), per_b3)],
        out_shape=[jax.ShapeDtypeStruct((rows, MIX_WIDTH), BF16),
                   jax.ShapeDtypeStruct((n_streams, n_pairs, 2 * RET_DK, RET_DV), F32),
                   jax.ShapeDtypeStruct((n_streams, WINDOW, 2 * SWA_KV), F32)],
        scratch_shapes=[pltpu.VMEM((n_pairs, 2 * RET_DK, RET_DV), F32),
                        pltpu.VMEM((SWA_KV_HEADS, WINDOW + tl, LANES), BF16),
                        pltpu.VMEM((SWA_KV_HEADS, WINDOW + tl, LANES), BF16),
                        pltpu.VMEM((SWA_KV_HEADS, META_ROWS, LANES), BF16),
                        pltpu.VMEM((SWA_KV_HEADS, META_ROWS, LANES), BF16),
                        pltpu.VMEM((RET_HEADS, tl, tl), F32),
                        pltpu.VMEM((n_pairs, tl, LANES), F32),
                        pltpu.VMEM((RET_HEADS, tl, RET_DV), F32),
                        pltpu.VMEM((n_pairs, 2 * RET_DK, RET_DV), F32)],
        compiler_params=_params(2),
        name="attention",
    )(rqk, rv, gate, sq, skv, meta_kv, hist_kv, s0, rng, sink_tab)


def _post_kernel(omix_ref, x_ref, wout_ref, g2_ref, wrt_ref, base_ref,
                 xmid_ref, wcol_ref, rt_ref, cnt_ref, tri_scr, run_scr):
    i = pl.program_id(0)
    tm = x_ref.shape[0]

    @pl.when(i == 0)
    def _init():
        r = lax.broadcasted_iota(jnp.int32, (tm, tm), 0)
        c = lax.broadcasted_iota(jnp.int32, (tm, tm), 1)
        tri_scr[...] = jnp.where(r < c, 1.0, 0.0).astype(BF16)
        run_scr[...] = base_ref[...]

    xm = x_ref[...] + jnp.dot(omix_ref[...], wout_ref[...], preferred_element_type=F32)
    xmid_ref[...] = xm
    hn = _row_rms(xm) * g2_ref[...]
    lt = lax.dot_general(wrt_ref[...], hn.astype(BF16), (((1,), (1,)), ((), ())), preferred_element_type=F32)
    row8 = lax.broadcasted_iota(jnp.int32, (SUBLANES, tm), 0)
    big = jnp.int32(SUBLANES)
    gl = jnp.where(row8 < N_GROUPS, lt[0:SUBLANES], NEG_INF)
    gmax = jnp.max(gl, axis=0, keepdims=True)
    gsum = jnp.sum(jnp.exp(gl - gmax), axis=0, keepdims=True)
    g_sel = jnp.min(jnp.where(gl == gmax, row8, big), axis=0, keepdims=True)
    p_sel = 1.0 / gsum
    el = lt[ROUTER_EXPERT_ROW0:ROUTER_EXPERT_ROW0 + EXPERTS_PER_GROUP]
    for g in range(1, N_GROUPS):
        lo = ROUTER_EXPERT_ROW0 + g * EXPERTS_PER_GROUP
        el = jnp.where(g_sel == g, lt[lo:lo + EXPERTS_PER_GROUP], el)
    m1 = jnp.max(el, axis=0, keepdims=True)
    i1 = jnp.min(jnp.where(el == m1, row8, big), axis=0, keepdims=True)
    el2 = jnp.where(row8 == i1, NEG_INF, el)
    m2 = jnp.max(el2, axis=0, keepdims=True)
    i2 = jnp.min(jnp.where(el2 == m2, row8, big), axis=0, keepdims=True)
    e2 = jnp.exp(m2 - m1)
    inv = 1.0 / (1.0 + e2)
    w1 = p_sel * inv
    w2 = p_sel * (e2 * inv)
    eid1 = g_sel * EXPERTS_PER_GROUP + i1
    eid2 = g_sel * EXPERTS_PER_GROUP + i2

    rowe = lax.broadcasted_iota(jnp.int32, (N_EXPERTS, tm), 0)
    oh = jnp.where((rowe == eid1) | (rowe == eid2), 1.0, 0.0).astype(BF16)
    run = run_scr[...]
    pref = jnp.dot(oh, tri_scr[...], preferred_element_type=F32) + jnp.concatenate([run] * (tm // LANES), axis=1)
    r1 = jnp.sum(jnp.where(rowe == eid1, pref, 0.0), axis=0, keepdims=True)
    r2 = jnp.sum(jnp.where(rowe == eid2, pref, 0.0), axis=0, keepdims=True)
    run = run + jnp.dot(oh, jnp.ones((tm, LANES), BF16), preferred_element_type=F32)
    run_scr[...] = run
    cnt_ref[...] = run

    out = jnp.zeros((SUBLANES, tm), F32)
    for k, v in enumerate([eid1.astype(F32), eid2.astype(F32), w1, w2, r1, r2]):
        out = jnp.where(row8 == k, v, out)
    rt_ref[...] = out
    rowl = lax.broadcasted_iota(jnp.int32, (LANES, tm), 0)
    wcol_ref[...] = jnp.where(rowl == 0, w1, jnp.where(rowl == 1, w2, 0.0)).T


def _post(omix, x2d, w_out_bf, g2, w_router_t_bf, base_cnt, tm):
    t_rows = x2d.shape[0]
    assert tm % LANES == 0
    row = lambda i: (i, 0)
    const = lambda i: (0, 0)
    return pl.pallas_call(
        _post_kernel,
        grid=(t_rows // tm,),
        in_specs=[pl.BlockSpec((tm, MIX_WIDTH), row),
                  pl.BlockSpec((tm, D_MODEL), row),
                  pl.BlockSpec((MIX_WIDTH, D_MODEL), const),
                  pl.BlockSpec((1, D_MODEL), const),
                  pl.BlockSpec((ROUTER_ROWS, D_MODEL), const),
                  pl.BlockSpec((N_EXPERTS, LANES), const)],
        out_specs=[pl.BlockSpec((tm, D_MODEL), row),
                   pl.BlockSpec((tm, LANES), row),
                   pl.BlockSpec((SUBLANES, tm), lambda i: (0, i)),
                   pl.BlockSpec((N_EXPERTS, LANES), const)],
        out_shape=[jax.ShapeDtypeStruct((t_rows, D_MODEL), F32),
                   jax.ShapeDtypeStruct((t_rows, LANES), F32),
                   jax.ShapeDtypeStruct((SUBLANES, t_rows), F32),
                   jax.ShapeDtypeStruct((N_EXPERTS, LANES), F32)],
        scratch_shapes=[pltpu.VMEM((tm, tm), BF16), pltpu.VMEM((N_EXPERTS, LANES), F32)],
        compiler_params=_params(1),
        name="post",
    )(omix, x2d, w_out_bf, g2, w_router_t_bf, base_cnt)


def _step_major(pos, tm):
    return [pos[e].reshape(-1, 1, tm) for e in range(2)]


def _dispatch_kernel(pos0_ref, pos1_ref, *refs, tm, group_steps):
    pos_refs = (pos0_ref, pos1_ref)
    n_g = len(group_steps)
    xmid_refs = refs[:n_g]
    g2_ref, xs_ref, hbuf, sems = refs[n_g:]
    n_steps = sum(group_steps)
    i = pl.program_id(0)
    slot = i % 2

    first = 0
    for xmid_ref, steps in zip(xmid_refs, group_steps):
        @pl.when((i >= first) & (i < first + steps))
        def _normalise(xmid_ref=xmid_ref):
            hn = _row_rms(xmid_ref[...]) * g2_ref[...]
            hbuf[slot] = hn.reshape(tm // SUBLANES, SUBLANES, D_MODEL)
        first += steps

    def body(blk, carry):
        for k in range(SUBLANES):
            r = blk * SUBLANES + k
            for e in range(2):
                pltpu.make_async_copy(hbuf.at[slot, blk, pl.ds(k, 1)], xs_ref.at[pl.ds(pos_refs[e][0, 0, r], 1)],
                                      sems.at[slot]).start(priority=e)
        return carry

    lax.fori_loop(0, tm // SUBLANES, body, 0)

    def drain(which):
        pltpu.make_async_copy(xs_ref.at[pl.ds(0, 2 * tm)], xs_ref.at[pl.ds(0, 2 * tm)], sems.at[which]).wait()

    @pl.when(i > 0)
    def _previous():
        drain(1 - slot)

    @pl.when(i == n_steps - 1)
    def _last():
        drain(slot)


def _dispatch(pos_list, xmid_list, g2, tm):
    group_steps = tuple(x.shape[0] // tm for x in xmid_list)
    n_steps = sum(group_steps)
    assert tm % SUBLANES == 0
    pos3 = [jnp.concatenate(parts, axis=0) for parts in zip(*[_step_major(p, tm) for p in pos_list])]
    smem = pl.BlockSpec((1, 1, tm), lambda i: (i, 0, 0), memory_space=pltpu.SMEM)
    x_specs = []
    first = 0
    for steps in group_steps:
        x_specs.append(pl.BlockSpec((tm, D_MODEL),
                                    lambda i, first=first, steps=steps: (jnp.clip(i - first, 0, steps - 1), 0)))
        first += steps
    n_rows_out = 2 * sum(x.shape[0] for x in xmid_list)
    return pl.pallas_call(
        functools.partial(_dispatch_kernel, tm=tm, group_steps=group_steps),
        grid=(n_steps,),
        in_specs=[smem, smem] + x_specs + [pl.BlockSpec((1, D_MODEL), lambda i: (0, 0))],
        out_specs=pl.BlockSpec(memory_space=pl.ANY),
        out_shape=jax.ShapeDtypeStruct((n_rows_out, D_MODEL), F32),
        scratch_shapes=[pltpu.VMEM((2, tm // SUBLANES, SUBLANES, D_MODEL), F32), pltpu.SemaphoreType.DMA((2,))],
        compiler_params=_params(1),
        name="dispatch",
    )(*pos3, *xmid_list, g2)


def _expert_kernel(vt_ref, ve_ref, lo_ref, hi_ref, nv_ref, x_ref, wg_ref, wu_ref, wd_ref, y_ref, wgu_scr, wd_scr):
    v = pl.program_id(0)

    @pl.when(v < nv_ref[0])
    def _compute():
        @pl.when((v == 0) | (ve_ref[v] != ve_ref[jnp.maximum(v - 1, 0)]))
        def _new_expert():
            wgu_scr[:, 0:EXPERT_FF] = wg_ref[0].astype(BF16)
            wgu_scr[:, EXPERT_FF:2 * EXPERT_FF] = wu_ref[0].astype(BF16)
            wd_scr[...] = wd_ref[0].astype(BF16)

        x = x_ref[...].astype(BF16)
        gu = jnp.dot(x, wgu_scr[...], preferred_element_type=F32)
        g = gu[:, 0:EXPERT_FF]
        a = (g * jax.nn.sigmoid(g) * gu[:, EXPERT_FF:2 * EXPERT_FF]).astype(BF16)
        y = jnp.dot(a, wd_scr[...], preferred_element_type=F32)
        lo, hi = lo_ref[v], hi_ref[v]

        @pl.when(lo == 0)
        def _first_visit():
            y_ref[...] = y

        @pl.when(lo > 0)
        def _later_visit():
            row = lax.broadcasted_iota(jnp.int32, y.shape, 0)
            y_ref[...] = jnp.where((row >= lo) & (row < hi), y, y_ref[...])


def _experts(vis_tile, vis_expert, vis_lo, vis_hi, n_vis, xs, w_gate, w_up, w_down, tm):
    n_steps = vis_tile.shape[0]
    last = lambda v, nv: jnp.minimum(v, nv[0] - 1)
    of_expert = lambda v, vt, ve, lo, hi, nv: (ve[last(v, nv)], 0, 0)
    return pl.pallas_call(
        _expert_kernel,
        grid_spec=pltpu.PrefetchScalarGridSpec(
            num_scalar_prefetch=5,
            grid=(n_steps,),
            in_specs=[pl.BlockSpec((tm, D_MODEL), lambda v, vt, ve, lo, hi, nv: (vt[last(v, nv)], 0)),
                      pl.BlockSpec((1, D_MODEL, EXPERT_FF), of_expert),
                      pl.BlockSpec((1, D_MODEL, EXPERT_FF), of_expert),
                      pl.BlockSpec((1, EXPERT_FF, D_MODEL), of_expert)],
            out_specs=pl.BlockSpec((tm, D_MODEL), lambda v, vt, ve, lo, hi, nv: (vt[last(v, nv)], 0)),
            scratch_shapes=[pltpu.VMEM((D_MODEL, 2 * EXPERT_FF), BF16), pltpu.VMEM((EXPERT_FF, D_MODEL), BF16)]),
        out_shape=jax.ShapeDtypeStruct(xs.shape, F32),
        compiler_params=_params(1),
        name="experts",
    )(vis_tile, vis_expert, vis_lo, vis_hi, n_vis, xs, w_gate, w_up, w_down)


def _combine_kernel(pos0_ref, pos1_ref, nxt0_ref, nxt1_ref, ys_ref, xmid_ref, wcol_ref, out_ref, ybuf, sems,
                    *, tm, n_steps):
    i = pl.program_id(0)
    slot = i % 2

    def issue(p_refs, to_slot):
        def body(blk, carry):
            for k in range(SUBLANES):
                r = blk * SUBLANES + k
                for e in range(2):
                    pltpu.make_async_copy(ys_ref.at[pl.ds(p_refs[e][0, 0, r], 1)],
                                          ybuf.at[to_slot, e, blk, pl.ds(k, 1)], sems.at[to_slot]).start(priority=e)
            return carry

        lax.fori_loop(0, tm // SUBLANES, body, 0)

    @pl.when(i == 0)
    def _first():
        issue((pos0_ref, pos1_ref), 0)

    @pl.when(i + 1 < n_steps)
    def _ahead():
        issue((nxt0_ref, nxt1_ref), 1 - slot)

    for e in range(2):
        pltpu.make_async_copy(ybuf.at[slot, e], ybuf.at[slot, e], sems.at[slot]).wait()
    w = wcol_ref[...]
    y0 = ybuf[slot, 0].reshape(tm, D_MODEL)
    y1 = ybuf[slot, 1].reshape(tm, D_MODEL)
    out_ref[...] = xmid_ref[...] + w[:, 0:1] * y0 + w[:, 1:2] * y1


def _combine(pos, ys, xmid, wcol, tm):
    t_rows = xmid.shape[0]
    n_steps = t_rows // tm
    assert tm % SUBLANES == 0
    pos3 = _step_major(pos, tm)
    row = lambda i: (i, 0)
    cur = pl.BlockSpec((1, 1, tm), lambda i: (i, 0, 0), memory_space=pltpu.SMEM)
    nxt = pl.BlockSpec((1, 1, tm), lambda i: (jnp.minimum(i + 1, n_steps - 1), 0, 0), memory_space=pltpu.SMEM)
    return pl.pallas_call(
        functools.partial(_combine_kernel, tm=tm, n_steps=n_steps),
        grid=(n_steps,),
        in_specs=[cur, cur, nxt, nxt,
                  pl.BlockSpec(memory_space=pl.ANY),
                  pl.BlockSpec((tm, D_MODEL), row),
                  pl.BlockSpec((tm, LANES), row)],
        out_specs=pl.BlockSpec((tm, D_MODEL), row),
        out_shape=jax.ShapeDtypeStruct((t_rows, D_MODEL), F32),
        scratch_shapes=[pltpu.VMEM((2, 2, tm // SUBLANES, SUBLANES, D_MODEL), F32), pltpu.SemaphoreType.DMA((2,))],
        compiler_params=_params(1),
        name="combine",
    )(*pos3, *pos3, ys, xmid, wcol)


def _bucket(ends, idx):
    n = jnp.sum((ends[None, :] <= idx[:, None]).astype(jnp.int32), axis=1)
    return jnp.minimum(n, ends.shape[0] - 1)


def _tile_for(rows, pref):
    tm = min(pref, rows)
    assert rows % tm == 0
    return tm


def kernel(x_prompt, x_sample, cache_ret_state, cache_swa_k, cache_swa_v, meta_tokens, norm1_g, w_in, q_norm_g,
           k_norm_g, ret_norm_g, attn_sinks, w_out, norm2_g, w_group, w_expert, w_gate, w_up, w_down):
    assert norm1_g.shape[0] == 1, "single-layer trunk"
    bp, lp, _ = x_prompt.shape
    bs, ls, _ = x_sample.shape
    n_pairs = RET_HEADS // 2

    g1 = norm1_g[0][None, :]
    g2 = norm2_g[0][None, :]
    w_in_bf = w_in[0].astype(BF16)
    w_out_bf = w_out[0].astype(BF16)
    qg2 = jnp.tile(q_norm_g[0], 2)[None, :]
    kg2 = jnp.tile(k_norm_g[0], 2)[None, :]
    rng = ret_norm_g[0].reshape(RET_HEADS, RET_DV)
    sink_tab = jnp.broadcast_to(jnp.repeat(attn_sinks[0], CHUNK).reshape(SWA_KV_HEADS, 1, 4 * CHUNK),
                                (SWA_KV_HEADS, SUBLANES, 4 * CHUNK))
    w_router_t = jnp.zeros((ROUTER_ROWS, D_MODEL), F32)
    w_router_t = w_router_t.at[0:N_GROUPS].set(w_group[0].T)
    w_router_t = w_router_t.at[ROUTER_EXPERT_ROW0:ROUTER_EXPERT_ROW0 + N_EXPERTS].set(w_expert[0].T)
    w_router_t_bf = w_router_t.astype(BF16)

    meta_rows = 2 * CHUNK
    m_pad = jnp.zeros((meta_rows, D_MODEL), F32).at[0:N_META].set(meta_tokens)
    m_rqk, m_rv, _, _, m_skv = _proj(m_pad, jnp.arange(meta_rows, dtype=jnp.int32), meta_rows, g1, w_in_bf, qg2, kg2)
    s_meta = _meta_state(m_rqk, m_rv)[None]
    meta_kv = m_skv[0:N_META]

    groups = [
        dict(x=x_prompt.reshape(bp * lp, D_MODEL), n=bp, seq=lp, pos0=N_META, has_hist=False, s0=s_meta,
             hist=jnp.zeros((1, WINDOW, 2 * SWA_KV), F32)),
        dict(x=x_sample.reshape(bs * ls, D_MODEL), n=bs, seq=ls, pos0=N_META + PAST_LEN, has_hist=True,
             s0=cache_ret_state[0].reshape(bs, n_pairs, 2 * RET_DK, RET_DV),
             hist=jnp.concatenate([cache_swa_k[0].reshape(bs, WINDOW, SWA_KV),
                                   cache_swa_v[0].reshape(bs, WINDOW, SWA_KV)], axis=-1)),
    ]

    base_cnt = jnp.zeros((N_EXPERTS, LANES), F32)
    for g in groups:
        rows = g["n"] * g["seq"]
        tm = _tile_for(rows, PROJ_TILE)
        pos = g["pos0"] + jnp.arange(g["seq"], dtype=jnp.int32)
        if g["seq"] < tm:
            assert tm % g["seq"] == 0
            pos = jnp.tile(pos, tm // g["seq"])
        else:
            assert g["seq"] % tm == 0
        rqk, rv, gate, sq, skv = _proj(g["x"], pos, tm, g1, w_in_bf, qg2, kg2)
        tl = min(ATTN_TILE, g["seq"])
        omix, s_out, kv_out = _attention(rqk, rv, gate, sq, skv, meta_kv, g["hist"], g["s0"], rng, sink_tab,
                                         n_streams=g["n"], seq=g["seq"], tl=tl, has_hist=g["has_hist"])
        xmid, wcol, route_t, base_cnt = _post(omix, g["x"], w_out_bf, g2, w_router_t_bf, base_cnt,
                                              _tile_for(rows, POST_TILE))
        g.update(xmid=xmid, wcol=wcol, route_t=route_t, s_out=s_out, kv_out=kv_out)

    te = EXPERT_TILE
    total_rows = sum(g["n"] * g["seq"] for g in groups)
    assert (2 * total_rows) % te == 0
    n_row_tiles = (2 * total_rows) // te
    counts = base_cnt[:, 0].astype(jnp.int32)
    off = jnp.cumsum(counts) - counts
    first_tile = off // te
    n_vis_e = jnp.where(counts > 0, (off + counts - 1) // te - first_tile + 1, 0)
    vis_end = jnp.cumsum(n_vis_e)
    n_vis = vis_end[-1:].astype(jnp.int32)
    v = jnp.arange(n_row_tiles + N_EXPERTS, dtype=jnp.int32)
    vis_expert = _bucket(vis_end, v)
    pick = lambda table: jnp.sum(jnp.where(vis_expert[:, None] == jnp.arange(N_EXPERTS, dtype=jnp.int32)[None, :],
                                           table[None, :], 0), axis=1)
    vis_tile = jnp.clip(pick(first_tile) + v - pick(vis_end - n_vis_e), 0, n_row_tiles - 1).astype(jnp.int32)
    vis_lo = jnp.clip(pick(off) - vis_tile * te, 0, te).astype(jnp.int32)
    vis_hi = jnp.clip(pick(off + counts) - vis_tile * te, 0, te).astype(jnp.int32)

    for g in groups:
        eid = g["route_t"][0:2].astype(jnp.int32)
        off_sel = jnp.sum(jnp.where(eid[None] == jnp.arange(N_EXPERTS, dtype=jnp.int32)[:, None, None],
                                    off[:, None, None], 0), axis=0)
        g["pos"] = (off_sel + g["route_t"][4:6].astype(jnp.int32)).astype(jnp.int32)
    xs = _dispatch([g["pos"] for g in groups], [g["xmid"] for g in groups], g2, MOVE_TILE)

    ys = _experts(vis_tile, vis_expert.astype(jnp.int32), vis_lo, vis_hi, n_vis, xs, w_gate[0], w_up[0], w_down[0], te)

    outs = []
    for g in groups:
        rows = g["n"] * g["seq"]
        y = _combine(g["pos"], ys, g["xmid"], g["wcol"], _tile_for(rows, MOVE_TILE))
        outs.append(y.reshape(g["n"], g["seq"], D_MODEL))

    def caches(g):
        kv = g["kv_out"]
        k = kv[:, :, 0:SWA_KV].reshape(g["n"], WINDOW, SWA_KV_HEADS, SWA_HD)[None]
        v = kv[:, :, SWA_KV:2 * SWA_KV].reshape(g["n"], WINDOW, SWA_KV_HEADS, SWA_HD)[None]
        s = g["s_out"].reshape(g["n"], RET_HEADS, RET_DK, RET_DV)[None]
        return s, k, v

    sp, kp, vp = caches(groups[0])
    ss, ks, vs = caches(groups[1])
    return (outs[0], outs[1], sp, kp, vp, ss, ks, vs)
```

```python
import functools

import numpy as np
import jax
import jax.numpy as jnp
from jax import lax
from jax.experimental import pallas as pl
from jax.experimental.pallas import tpu as pltpu

F32 = jnp.float32
BF16 = jnp.bfloat16

D_MODEL = 1024
PAST_LEN = 4096
CHUNK = 64
N_META = 16
RET_HEADS = 4
RET_DK = 64
RET_DV = 128
SWA_HEADS = 8
SWA_KV_HEADS = 2
SWA_HD = 64
WINDOW = 128
ROPE_THETA = 10000.0
N_GROUPS = 4
EXPERTS_PER_GROUP = 8
N_EXPERTS = N_GROUPS * EXPERTS_PER_GROUP
EXPERT_FF = 256
EPS = 1e-6
NEG_INF = -1e30
LOG2E = float(np.log2(np.e))
RET_Q = RET_HEADS * RET_DK
RET_V = RET_HEADS * RET_DV
SWA_Q = SWA_HEADS * SWA_HD
SWA_KV = SWA_KV_HEADS * SWA_HD
MIX_WIDTH = RET_V + SWA_Q
IN_WIDTH = 2 * RET_Q + 2 * RET_V + SWA_Q + 2 * SWA_KV

LANES = 128
PROJ_TILE = 1024
POST_TILE = 1024
ATTN_TILE = 256
EXPERT_TILE = 512
MOVE_TILE = 512
SUBLANES = 8
ROUTER_EXPERT_ROW0 = 8
ROUTER_ROWS = 64
META_ROWS = 64
VMEM_LIMIT = 56 * 1024 * 1024

_LOG_G = [float(np.log1p(-np.exp2(-5.0 - h))) for h in range(RET_HEADS)]


def _params(n_axes):
    return pltpu.CompilerParams(dimension_semantics=("arbitrary",) * n_axes, vmem_limit_bytes=VMEM_LIMIT)


def _split_bf16(a):
    hi = a.astype(BF16)
    return hi, (a - hi.astype(F32)).astype(BF16)


def _split_dot(a, w2):
    hi, lo = _split_bf16(a)
    return jnp.dot(jnp.concatenate([hi, lo], axis=1), w2, preferred_element_type=F32)


def _lane_sum(a):
    return _split_dot(a, jnp.ones((2 * LANES, LANES), BF16))


def _head_sum_matrix():
    i = lax.broadcasted_iota(jnp.int32, (2 * LANES, LANES), 0) % LANES
    j = lax.broadcasted_iota(jnp.int32, (2 * LANES, LANES), 1)
    return jnp.where((i < SWA_HD) == (j < SWA_HD), 1.0, 0.0).astype(BF16)


def _rope(t, c, s1, s2):
    half = SWA_HD // 2
    return t * c + pltpu.roll(t, LANES - half, 1) * s1 + pltpu.roll(t, half, 1) * s2


def _head_rms(t, g, head_w):
    ms = _split_dot(t * t, head_w) * (1.0 / SWA_HD)
    return t * lax.rsqrt(ms + EPS) * g


def _row_rms(x):
    n_tiles = x.shape[1] // LANES
    ss = x[:, 0:LANES] * x[:, 0:LANES]
    for j in range(1, n_tiles):
        ss = ss + x[:, j * LANES:(j + 1) * LANES] * x[:, j * LANES:(j + 1) * LANES]
    r = lax.rsqrt(_lane_sum(ss) * (1.0 / x.shape[1]) + EPS)
    return x * jnp.concatenate([r] * n_tiles, axis=1)


def _proj_kernel(x_ref, g1_ref, w_ref, qg_ref, kg_ref, cos_ref, s1_ref, s2_ref,
                 rqk_ref, rv_ref, gate_ref, sq_ref, skv_ref):
    xn = (_row_rms(x_ref[...]) * g1_ref[...]).astype(BF16)
    c, s1, s2 = cos_ref[...], s1_ref[...], s2_ref[...]
    head_w = _head_sum_matrix()

    def seg(a, b):
        return jnp.dot(xn, w_ref[:, a:b], preferred_element_type=F32)

    def tile(h, j):
        return h[:, j * LANES:(j + 1) * LANES]

    h = seg(0, 2 * RET_Q)
    for j in range(2):
        rqk_ref[:, j * LANES:(j + 1) * LANES] = _rope(tile(h, j), c, s1, s2).astype(BF16)
    for j in range(2, 4):
        rqk_ref[:, j * LANES:(j + 1) * LANES] = (_rope(tile(h, j), c, s1, s2) * (RET_DK ** -0.5)).astype(BF16)
    a = 2 * RET_Q
    rv_ref[...] = seg(a, a + RET_V).astype(BF16)
    a += RET_V
    g = seg(a, a + RET_V)
    gate_ref[...] = (g * jax.nn.sigmoid(g)).astype(BF16)
    a += RET_V
    h = seg(a, a + SWA_Q)
    qg = qg_ref[...]
    for j in range(SWA_Q // LANES):
        sq_ref[:, j * LANES:(j + 1) * LANES] = _rope(_head_rms(tile(h, j), qg, head_w), c, s1, s2).astype(BF16)
    a += SWA_Q
    h = seg(a, a + 2 * SWA_KV)
    skv_ref[:, 0:LANES] = _rope(_head_rms(tile(h, 0), kg_ref[...], head_w), c, s1, s2)
    skv_ref[:, LANES:2 * LANES] = tile(h, 1)


def _rope_tables(pos):
    half = SWA_HD // 2
    inv = ROPE_THETA ** (-jnp.arange(half, dtype=F32) / half)
    ang = pos.astype(F32)[:, None] * inv[None, :]
    cos, sin = jnp.cos(ang), jnp.sin(ang)
    z = jnp.zeros_like(sin)
    return (jnp.tile(cos, (1, 4)),
            jnp.tile(jnp.concatenate([-sin, z], axis=1), (1, 2)),
            jnp.tile(jnp.concatenate([z, sin], axis=1), (1, 2)))


def _proj(x2d, pos_rows, tm, g1, w_in_bf, qg2, kg2):
    t_rows = x2d.shape[0]
    n_tiles = t_rows // tm
    n_pos_tiles = pos_rows.shape[0] // tm
    cos, s1, s2 = _rope_tables(pos_rows)
    row = lambda i: (i, 0)
    const = lambda i: (0, 0)
    tab = lambda i: (i % n_pos_tiles, 0)
    return pl.pallas_call(
        _proj_kernel,
        grid=(n_tiles,),
        in_specs=[pl.BlockSpec((tm, D_MODEL), row),
                  pl.BlockSpec((1, D_MODEL), const),
                  pl.BlockSpec((D_MODEL, IN_WIDTH), const),
                  pl.BlockSpec((1, LANES), const),
                  pl.BlockSpec((1, LANES), const),
                  pl.BlockSpec((tm, LANES), tab),
                  pl.BlockSpec((tm, LANES), tab),
                  pl.BlockSpec((tm, LANES), tab)],
        out_specs=[pl.BlockSpec((tm, 2 * RET_Q), row),
                   pl.BlockSpec((tm, RET_V), row),
                   pl.BlockSpec((tm, RET_V), row),
                   pl.BlockSpec((tm, SWA_Q), row),
                   pl.BlockSpec((tm, 2 * SWA_KV), row)],
        out_shape=[jax.ShapeDtypeStruct((t_rows, 2 * RET_Q), BF16),
                   jax.ShapeDtypeStruct((t_rows, RET_V), BF16),
                   jax.ShapeDtypeStruct((t_rows, RET_V), BF16),
                   jax.ShapeDtypeStruct((t_rows, SWA_Q), BF16),
                   jax.ShapeDtypeStruct((t_rows, 2 * SWA_KV), F32)],
        compiler_params=_params(1),
        name="proj",
    )(x2d, g1, w_in_bf, qg2, kg2, cos, s1, s2)


def _pair_update(k_bf, v0_bf, v1_bf, wt):
    kw = (k_bf.astype(F32) * wt).astype(BF16)
    dn = (((0,), (0,)), ((), ()))
    a0 = lax.dot_general(kw, v0_bf, dn, preferred_element_type=F32)
    a1 = lax.dot_general(kw, v1_bf, dn, preferred_element_type=F32)
    top = lax.broadcasted_iota(jnp.int32, a0.shape, 0) < RET_DK
    return jnp.where(top, a0, a1)


def _decay_rows(n, pair, rows_back_from):
    i = lax.broadcasted_iota(jnp.int32, (n, LANES), 0).astype(F32)
    lane = lax.broadcasted_iota(jnp.int32, (n, LANES), 1)
    lg = jnp.where(lane < RET_DK, _LOG_G[2 * pair], _LOG_G[2 * pair + 1])
    return jnp.exp((rows_back_from - i) * lg)


def _meta_state_kernel(rqk_ref, rv_ref, s_ref, *, n_rows):
    for p in range(RET_HEADS // 2):
        k = rqk_ref[:, RET_Q + p * LANES:RET_Q + (p + 1) * LANES]
        wt = _decay_rows(n_rows, p, float(N_META - 1))
        s_ref[p] = _pair_update(k, rv_ref[:, (2 * p) * LANES:(2 * p + 1) * LANES],
                                rv_ref[:, (2 * p + 1) * LANES:(2 * p + 2) * LANES], wt)


def _meta_state(m_rqk, m_rv):
    n_rows = m_rqk.shape[0]
    return pl.pallas_call(
        functools.partial(_meta_state_kernel, n_rows=n_rows),
        out_shape=jax.ShapeDtypeStruct((RET_HEADS // 2, 2 * RET_DK, RET_DV), F32),
        name="meta_state",
    )(m_rqk, m_rv)


def _dup_halves(a, lo_mask):
    sw = pltpu.roll(a, SWA_HD, 1)
    return jnp.where(lo_mask, a, sw), jnp.where(lo_mask, sw, a)


def _attn_kernel(rqk_ref, rv_ref, gate_ref, sq_ref, skv_ref, meta_ref, hist_ref, s0_ref, rng_ref, sink_ref,
                 omix_ref, sout_ref, kvout_ref,
                 s_scr, kd_scr, vd_scr, mk_scr, mv_scr, dec_scr, wt_scr, cs_scr, gam_scr,
                 *, tl, has_hist):
    b = pl.program_id(0)
    t = pl.program_id(1)
    nt = pl.num_programs(1)
    n_chunks = tl // CHUNK
    n_pairs = RET_HEADS // 2
    lo_tl = lax.broadcasted_iota(jnp.int32, (tl, LANES), 1) < SWA_HD
    lo_c = lax.broadcasted_iota(jnp.int32, (CHUNK, LANES), 1) < SWA_HD

    @pl.when((b == 0) & (t == 0))
    def _tables():
        i = lax.broadcasted_iota(jnp.int32, (tl, tl), 0)
        j = lax.broadcasted_iota(jnp.int32, (tl, tl), 1)
        diff = (i - j).astype(F32)
        row = lax.broadcasted_iota(jnp.int32, (tl, LANES), 0).astype(F32)
        for h in range(RET_HEADS):
            dec_scr[h] = jnp.where(diff >= 0.0, jnp.exp(jnp.maximum(diff, 0.0) * _LOG_G[h]), 0.0)
            cs_scr[h] = jnp.exp((row + 1.0) * _LOG_G[h])
        top = lax.broadcasted_iota(jnp.int32, (2 * RET_DK, RET_DV), 0) < RET_DK
        for p in range(n_pairs):
            wt_scr[p] = _decay_rows(tl, p, float(tl - 1))
            gam_scr[p] = jnp.where(top, jnp.exp(jnp.float32(tl * _LOG_G[2 * p])), jnp.exp(jnp.float32(tl * _LOG_G[2 * p + 1])))
        lo_m = lax.broadcasted_iota(jnp.int32, (N_META, LANES), 1) < SWA_HD
        mk0, mk1 = _dup_halves(meta_ref[:, 0:LANES], lo_m)
        mv0, mv1 = _dup_halves(meta_ref[:, LANES:2 * LANES], lo_m)
        mk_scr[...] = jnp.zeros(mk_scr.shape, BF16)
        mv_scr[...] = jnp.zeros(mv_scr.shape, BF16)
        mk_scr[0, 0:N_META] = mk0.astype(BF16)
        mk_scr[1, 0:N_META] = mk1.astype(BF16)
        mv_scr[0, 0:N_META] = mv0.astype(BF16)
        mv_scr[1, 0:N_META] = mv1.astype(BF16)

    @pl.when(t == 0)
    def _stream_start():
        s_scr[...] = s0_ref[0]
        if has_hist:
            lo_w = lax.broadcasted_iota(jnp.int32, (WINDOW, LANES), 1) < SWA_HD
            k0, k1 = _dup_halves(hist_ref[0, :, 0:LANES], lo_w)
            v0, v1 = _dup_halves(hist_ref[0, :, LANES:2 * LANES], lo_w)
            kd_scr[0, 0:WINDOW] = k0.astype(BF16)
            kd_scr[1, 0:WINDOW] = k1.astype(BF16)
            vd_scr[0, 0:WINDOW] = v0.astype(BF16)
            vd_scr[1, 0:WINDOW] = v1.astype(BF16)
        else:
            z = jnp.zeros((WINDOW, LANES), BF16)
            for kv in range(SWA_KV_HEADS):
                kd_scr[kv, 0:WINDOW] = z
                vd_scr[kv, 0:WINDOW] = z

    k0, k1 = _dup_halves(skv_ref[:, 0:LANES], lo_tl)
    v0, v1 = _dup_halves(skv_ref[:, LANES:2 * LANES], lo_tl)
    kd_scr[0, WINDOW:WINDOW + tl] = k0.astype(BF16)
    kd_scr[1, WINDOW:WINDOW + tl] = k1.astype(BF16)
    vd_scr[0, WINDOW:WINDOW + tl] = v0.astype(BF16)
    vd_scr[1, WINDOW:WINDOW + tl] = v1.astype(BF16)

    band = WINDOW + CHUNK
    n_keys = META_ROWS + band
    n_q = 4 * CHUNK
    scale2 = (SWA_HD ** -0.5) * LOG2E
    krow = lax.broadcasted_iota(jnp.int32, (n_keys, n_q), 0)
    zero_c = jnp.zeros((CHUNK, LANES), BF16)
    ones_v = jnp.ones((n_keys, LANES), BF16)
    for c in range(n_chunks):
        if has_hist:
            first_valid = META_ROWS
        else:
            first_valid = jnp.where(t == 0, max(META_ROWS + WINDOW - c * CHUNK, META_ROWS), META_ROWS)
        valid_t = (krow < N_META) | (krow >= first_valid)
        r0 = c * CHUNK
        for kv in range(SWA_KV_HEADS):
            keys = jnp.concatenate([mk_scr[kv], kd_scr[kv, r0:r0 + band]], axis=0)
            vals = jnp.concatenate([mv_scr[kv], vd_scr[kv, r0:r0 + band]], axis=0)
            qa = sq_ref[r0:r0 + CHUNK, (2 * kv) * LANES:(2 * kv + 1) * LANES]
            qb = sq_ref[r0:r0 + CHUNK, (2 * kv + 1) * LANES:(2 * kv + 2) * LANES]
            lhs = jnp.concatenate([jnp.where(lo_c, qa, zero_c), jnp.where(lo_c, zero_c, qa),
                                   jnp.where(lo_c, qb, zero_c), jnp.where(lo_c, zero_c, qb)], axis=0)
            s_t = lax.dot_general(keys, lhs, (((1,), (1,)), ((), ())), preferred_element_type=F32) * scale2
            s_t = jnp.where(valid_t, s_t, NEG_INF)
            s_t = jnp.where(krow == N_META, sink_ref[kv, 0:1, :] * LOG2E, s_t)
            e_t = jnp.exp2(s_t - jnp.max(s_t, axis=0, keepdims=True)).astype(BF16)
            ov = lax.dot_general(e_t, jnp.concatenate([vals, ones_v], axis=1), (((0,), (0,)), ((), ())),
                                 preferred_element_type=F32)
            o = ov[:, 0:LANES] * (1.0 / ov[:, LANES:2 * LANES])
            oa = jnp.where(lo_c, o[0:CHUNK], o[CHUNK:2 * CHUNK])
            ob = jnp.where(lo_c, o[2 * CHUNK:3 * CHUNK], o[3 * CHUNK:4 * CHUNK])
            base = RET_V + (2 * kv) * LANES
            omix_ref[r0:r0 + CHUNK, base:base + LANES] = oa.astype(BF16)
            omix_ref[r0:r0 + CHUNK, base + LANES:base + 2 * LANES] = ob.astype(BF16)

    zero_t = jnp.zeros((tl, LANES), BF16)
    for p in range(n_pairs):
        q = rqk_ref[:, p * LANES:(p + 1) * LANES]
        k = rqk_ref[:, RET_Q + p * LANES:RET_Q + (p + 1) * LANES]
        lhs = jnp.concatenate([jnp.where(lo_tl, q, zero_t), jnp.where(lo_tl, zero_t, q)], axis=0)
        s = lax.dot_general(lhs, k, (((1,), (1,)), ((), ())), preferred_element_type=F32)
        cross = jnp.dot(lhs, s_scr[p].astype(BF16), preferred_element_type=F32)
        for i in range(2):
            h = 2 * p + i
            v = rv_ref[:, h * LANES:(h + 1) * LANES]
            a = (s[i * tl:(i + 1) * tl] * dec_scr[h]).astype(BF16)
            o = jnp.dot(a, v, preferred_element_type=F32) + cross[i * tl:(i + 1) * tl] * cs_scr[h]
            r = o * lax.rsqrt(_lane_sum(o * o) * (1.0 / RET_DV) + EPS) * rng_ref[h:h + 1, :]
            omix_ref[:, h * LANES:(h + 1) * LANES] = (r * gate_ref[:, h * LANES:(h + 1) * LANES].astype(F32)).astype(BF16)
        u = _pair_update(k, rv_ref[:, (2 * p) * LANES:(2 * p + 1) * LANES],
                         rv_ref[:, (2 * p + 1) * LANES:(2 * p + 2) * LANES], wt_scr[p])
        s_scr[p] = gam_scr[p] * s_scr[p] + u

    if tl >= WINDOW:
        @pl.when(t + 1 < nt)
        def _carry_window():
            for kv in range(SWA_KV_HEADS):
                kd_scr[kv, 0:WINDOW] = kd_scr[kv, tl:tl + WINDOW]
                vd_scr[kv, 0:WINDOW] = vd_scr[kv, tl:tl + WINDOW]

    @pl.when(t + 1 == nt)
    def _stream_end():
        sout_ref[0] = s_scr[...]
        if tl >= WINDOW:
            kvout_ref[0] = skv_ref[tl - WINDOW:tl, :]
        else:
            kvout_ref[0, 0:WINDOW - tl] = hist_ref[0, tl:WINDOW, :]
            kvout_ref[0, WINDOW - tl:WINDOW] = skv_ref[...]


def _attention(rqk, rv, gate, sq, skv, meta_kv, hist_kv, s0, rng, sink_tab, *, n_streams, seq, tl, has_hist):
    nt = seq // tl
    assert tl % CHUNK == 0 and seq % tl == 0
    assert tl >= WINDOW or (nt == 1 and has_hist)
    n_pairs = RET_HEADS // 2
    s0_shared = s0.shape[0] == 1
    row = lambda b, t: (b * nt + t, 0)
    const2 = lambda b, t: (0, 0)
    const3 = lambda b, t: (0, 0, 0)
    per_b3 = lambda b, t: (b, 0, 0)
    s0_map = (lambda b, t: (0, 0, 0, 0)) if s0_shared else (lambda b, t: (b, 0, 0, 0))
    hist_map = per_b3 if has_hist else const3
    rows = n_streams * seq
    return pl.pallas_call(
        functools.partial(_attn_kernel, tl=tl, has_hist=has_hist),
        grid=(n_streams, nt),
        in_specs=[pl.BlockSpec((tl, 2 * RET_Q), row),
                  pl.BlockSpec((tl, RET_V), row),
                  pl.BlockSpec((tl, RET_V), row),
                  pl.BlockSpec((tl, SWA_Q), row),
                  pl.BlockSpec((tl, 2 * SWA_KV), row),
                  pl.BlockSpec((N_META, 2 * SWA_KV), const2),
                  pl.BlockSpec((1, WINDOW, 2 * SWA_KV), hist_map),
                  pl.BlockSpec((1, n_pairs, 2 * RET_DK, RET_DV), s0_map),
                  pl.BlockSpec((RET_HEADS, RET_DV), const2),
                  pl.BlockSpec((SWA_KV_HEADS, SUBLANES, 4 * CHUNK), const3)],
        out_specs=[pl.BlockSpec((tl, MIX_WIDTH), row),
                   pl.BlockSpec((1, n_pairs, 2 * RET_DK, RET_DV), lambda b, t: (b, 0, 0, 0)),
                   pl.BlockSpec((1, WINDOW, 2 * SWA_KV), per_b3)],
        out_shape=[jax.ShapeDtypeStruct((rows, MIX_WIDTH), BF16),
                   jax.ShapeDtypeStruct((n_streams, n_pairs, 2 * RET_DK, RET_DV), F32),
                   jax.ShapeDtypeStruct((n_streams, WINDOW, 2 * SWA_KV), F32)],
        scratch_shapes=[pltpu.VMEM((n_pairs, 2 * RET_DK, RET_DV), F32),
                        pltpu.VMEM((SWA_KV_HEADS, WINDOW + tl, LANES), BF16),
                        pltpu.VMEM((SWA_KV_HEADS, WINDOW + tl, LANES), BF16),
                        pltpu.VMEM((SWA_KV_HEADS, META_ROWS, LANES), BF16),
                        pltpu.VMEM((SWA_KV_HEADS, META_ROWS, LANES), BF16),
                        pltpu.VMEM((RET_HEADS, tl, tl), F32),
                        pltpu.VMEM((n_pairs, tl, LANES), F32),
                        pltpu.VMEM((RET_HEADS, tl, RET_DV), F32),
                        pltpu.VMEM((n_pairs, 2 * RET_DK, RET_DV), F32)],
        compiler_params=_params(2),
        name="attention",
    )(rqk, rv, gate, sq, skv, meta_kv, hist_kv, s0, rng, sink_tab)


def _post_kernel(omix_ref, x_ref, wout_ref, g2_ref, wrt_ref, base_ref,
                 xmid_ref, wcol_ref, rt_ref, cnt_ref, tri_scr, run_scr):
    i = pl.program_id(0)
    tm = x_ref.shape[0]

    @pl.when(i == 0)
    def _init():
        r = lax.broadcasted_iota(jnp.int32, (tm, tm), 0)
        c = lax.broadcasted_iota(jnp.int32, (tm, tm), 1)
        tri_scr[...] = jnp.where(r < c, 1.0, 0.0).astype(BF16)
        run_scr[...] = base_ref[...]

    xm = x_ref[...] + jnp.dot(omix_ref[...], wout_ref[...], preferred_element_type=F32)
    xmid_ref[...] = xm
    hn = _row_rms(xm) * g2_ref[...]
    lt = lax.dot_general(wrt_ref[...], hn.astype(BF16), (((1,), (1,)), ((), ())), preferred_element_type=F32)
    row8 = lax.broadcasted_iota(jnp.int32, (SUBLANES, tm), 0)
    big = jnp.int32(SUBLANES)
    gl = jnp.where(row8 < N_GROUPS, lt[0:SUBLANES], NEG_INF)
    gmax = jnp.max(gl, axis=0, keepdims=True)
    gsum = jnp.sum(jnp.exp(gl - gmax), axis=0, keepdims=True)
    g_sel = jnp.min(jnp.where(gl == gmax, row8, big), axis=0, keepdims=True)
    p_sel = 1.0 / gsum
    el = lt[ROUTER_EXPERT_ROW0:ROUTER_EXPERT_ROW0 + EXPERTS_PER_GROUP]
    for g in range(1, N_GROUPS):
        lo = ROUTER_EXPERT_ROW0 + g * EXPERTS_PER_GROUP
        el = jnp.where(g_sel == g, lt[lo:lo + EXPERTS_PER_GROUP], el)
    m1 = jnp.max(el, axis=0, keepdims=True)
    i1 = jnp.min(jnp.where(el == m1, row8, big), axis=0, keepdims=True)
    el2 = jnp.where(row8 == i1, NEG_INF, el)
    m2 = jnp.max(el2, axis=0, keepdims=True)
    i2 = jnp.min(jnp.where(el2 == m2, row8, big), axis=0, keepdims=True)
    e2 = jnp.exp(m2 - m1)
    inv = 1.0 / (1.0 + e2)
    w1 = p_sel * inv
    w2 = p_sel * (e2 * inv)
    eid1 = g_sel * EXPERTS_PER_GROUP + i1
    eid2 = g_sel * EXPERTS_PER_GROUP + i2

    rowe = lax.broadcasted_iota(jnp.int32, (N_EXPERTS, tm), 0)
    oh = jnp.where((rowe == eid1) | (rowe == eid2), 1.0, 0.0).astype(BF16)
    run = run_scr[...]
    pref = jnp.dot(oh, tri_scr[...], preferred_element_type=F32) + jnp.concatenate([run] * (tm // LANES), axis=1)
    r1 = jnp.sum(jnp.where(rowe == eid1, pref, 0.0), axis=0, keepdims=True)
    r2 = jnp.sum(jnp.where(rowe == eid2, pref, 0.0), axis=0, keepdims=True)
    run = run + jnp.dot(oh, jnp.ones((tm, LANES), BF16), preferred_element_type=F32)
    run_scr[...] = run
    cnt_ref[...] = run

    out = jnp.zeros((SUBLANES, tm), F32)
    for k, v in enumerate([eid1.astype(F32), eid2.astype(F32), w1, w2, r1, r2]):
        out = jnp.where(row8 == k, v, out)
    rt_ref[...] = out
    rowl = lax.broadcasted_iota(jnp.int32, (LANES, tm), 0)
    wcol_ref[...] = jnp.where(rowl == 0, w1, jnp.where(rowl == 1, w2, 0.0)).T


def _post(omix, x2d, w_out_bf, g2, w_router_t_bf, base_cnt, tm):
    t_rows = x2d.shape[0]
    assert tm % LANES == 0
    row = lambda i: (i, 0)
    const = lambda i: (0, 0)
    return pl.pallas_call(
        _post_kernel,
        grid=(t_rows // tm,),
        in_specs=[pl.BlockSpec((tm, MIX_WIDTH), row),
                  pl.BlockSpec((tm, D_MODEL), row),
                  pl.BlockSpec((MIX_WIDTH, D_MODEL), const),
                  pl.BlockSpec((1, D_MODEL), const),
                  pl.BlockSpec((ROUTER_ROWS, D_MODEL), const),
                  pl.BlockSpec((N_EXPERTS, LANES), const)],
        out_specs=[pl.BlockSpec((tm, D_MODEL), row),
                   pl.BlockSpec((tm, LANES), row),
                   pl.BlockSpec((SUBLANES, tm), lambda i: (0, i)),
                   pl.BlockSpec((N_EXPERTS, LANES), const)],
        out_shape=[jax.ShapeDtypeStruct((t_rows, D_MODEL), F32),
                   jax.ShapeDtypeStruct((t_rows, LANES), F32),
                   jax.ShapeDtypeStruct((SUBLANES, t_rows), F32),
                   jax.ShapeDtypeStruct((N_EXPERTS, LANES), F32)],
        scratch_shapes=[pltpu.VMEM((tm, tm), BF16), pltpu.VMEM((N_EXPERTS, LANES), F32)],
        compiler_params=_params(1),
        name="post",
    )(omix, x2d, w_out_bf, g2, w_router_t_bf, base_cnt)


def _step_major(pos, tm):
    return [pos[e].reshape(-1, 1, tm) for e in range(2)]


def _dispatch_kernel(pos0_ref, pos1_ref, *refs, tm, group_steps):
    pos_refs = (pos0_ref, pos1_ref)
    n_g = len(group_steps)
    xmid_refs = refs[:n_g]
    g2_ref, xs_ref, hbuf, sems = refs[n_g:]
    n_steps = sum(group_steps)
    i = pl.program_id(0)
    slot = i % 2

    first = 0
    for xmid_ref, steps in zip(xmid_refs, group_steps):
        @pl.when((i >= first) & (i < first + steps))
        def _normalise(xmid_ref=xmid_ref):
            hn = _row_rms(xmid_ref[...]) * g2_ref[...]
            hbuf[slot] = hn.reshape(tm // SUBLANES, SUBLANES, D_MODEL)
        first += steps

    def body(blk, carry):
        for k in range(SUBLANES):
            r = blk * SUBLANES + k
            for e in range(2):
                pltpu.make_async_copy(hbuf.at[slot, blk, pl.ds(k, 1)], xs_ref.at[pl.ds(pos_refs[e][0, 0, r], 1)],
                                      sems.at[slot]).start(priority=e)
        return carry

    lax.fori_loop(0, tm // SUBLANES, body, 0)

    def drain(which):
        pltpu.make_async_copy(xs_ref.at[pl.ds(0, 2 * tm)], xs_ref.at[pl.ds(0, 2 * tm)], sems.at[which]).wait()

    @pl.when(i > 0)
    def _previous():
        drain(1 - slot)

    @pl.when(i == n_steps - 1)
    def _last():
        drain(slot)


def _dispatch(pos_list, xmid_list, g2, tm):
    group_steps = tuple(x.shape[0] // tm for x in xmid_list)
    n_steps = sum(group_steps)
    assert tm % SUBLANES == 0
    pos3 = [jnp.concatenate(parts, axis=0) for parts in zip(*[_step_major(p, tm) for p in pos_list])]
    smem = pl.BlockSpec((1, 1, tm), lambda i: (i, 0, 0), memory_space=pltpu.SMEM)
    x_specs = []
    first = 0
    for steps in group_steps:
        x_specs.append(pl.BlockSpec((tm, D_MODEL),
                                    lambda i, first=first, steps=steps: (jnp.clip(i - first, 0, steps - 1), 0)))
        first += steps
    n_rows_out = 2 * sum(x.shape[0] for x in xmid_list)
    return pl.pallas_call(
        functools.partial(_dispatch_kernel, tm=tm, group_steps=group_steps),
        grid=(n_steps,),
        in_specs=[smem, smem] + x_specs + [pl.BlockSpec((1, D_MODEL), lambda i: (0, 0))],
        out_specs=pl.BlockSpec(memory_space=pl.ANY),
        out_shape=jax.ShapeDtypeStruct((n_rows_out, D_MODEL), F32),
        scratch_shapes=[pltpu.VMEM((2, tm // SUBLANES, SUBLANES, D_MODEL), F32), pltpu.SemaphoreType.DMA((2,))],
        compiler_params=_params(1),
        name="dispatch",
    )(*pos3, *xmid_list, g2)


def _expert_kernel(vt_ref, ve_ref, lo_ref, hi_ref, nv_ref, x_ref, wg_ref, wu_ref, wd_ref, y_ref, wgu_scr, wd_scr):
    v = pl.program_id(0)

    @pl.when(v < nv_ref[0])
    def _compute():
        @pl.when((v == 0) | (ve_ref[v] != ve_ref[jnp.maximum(v - 1, 0)]))
        def _new_expert():
            wgu_scr[:, 0:EXPERT_FF] = wg_ref[0].astype(BF16)
            wgu_scr[:, EXPERT_FF:2 * EXPERT_FF] = wu_ref[0].astype(BF16)
            wd_scr[...] = wd_ref[0].astype(BF16)

        x = x_ref[...].astype(BF16)
        gu = jnp.dot(x, wgu_scr[...], preferred_element_type=F32)
        g = gu[:, 0:EXPERT_FF]
        a = (g * jax.nn.sigmoid(g) * gu[:, EXPERT_FF:2 * EXPERT_FF]).astype(BF16)
        y = jnp.dot(a, wd_scr[...], preferred_element_type=F32)
        lo, hi = lo_ref[v], hi_ref[v]

        @pl.when(lo == 0)
        def _first_visit():
            y_ref[...] = y

        @pl.when(lo > 0)
        def _later_visit():
            row = lax.broadcasted_iota(jnp.int32, y.shape, 0)
            y_ref[...] = jnp.where((row >= lo) & (row < hi), y, y_ref[...])


def _experts(vis_tile, vis_expert, vis_lo, vis_hi, n_vis, xs, w_gate, w_up, w_down, tm):
    n_steps = vis_tile.shape[0]
    last = lambda v, nv: jnp.minimum(v, nv[0] - 1)
    of_expert = lambda v, vt, ve, lo, hi, nv: (ve[last(v, nv)], 0, 0)
    return pl.pallas_call(
        _expert_kernel,
        grid_spec=pltpu.PrefetchScalarGridSpec(
            num_scalar_prefetch=5,
            grid=(n_steps,),
            in_specs=[pl.BlockSpec((tm, D_MODEL), lambda v, vt, ve, lo, hi, nv: (vt[last(v, nv)], 0)),
                      pl.BlockSpec((1, D_MODEL, EXPERT_FF), of_expert),
                      pl.BlockSpec((1, D_MODEL, EXPERT_FF), of_expert),
                      pl.BlockSpec((1, EXPERT_FF, D_MODEL), of_expert)],
            out_specs=pl.BlockSpec((tm, D_MODEL), lambda v, vt, ve, lo, hi, nv: (vt[last(v, nv)], 0)),
            scratch_shapes=[pltpu.VMEM((D_MODEL, 2 * EXPERT_FF), BF16), pltpu.VMEM((EXPERT_FF, D_MODEL), BF16)]),
        out_shape=jax.ShapeDtypeStruct(xs.shape, F32),
        compiler_params=_params(1),
        name="experts",
    )(vis_tile, vis_expert, vis_lo, vis_hi, n_vis, xs, w_gate, w_up, w_down)


def _combine_kernel(pos0_ref, pos1_ref, nxt0_ref, nxt1_ref, ys_ref, xmid_ref, wcol_ref, out_ref, ybuf, sems,
                    *, tm, n_steps):
    i = pl.program_id(0)
    slot = i % 2

    def issue(p_refs, to_slot):
        def body(blk, carry):
            for k in range(SUBLANES):
                r = blk * SUBLANES + k
                for e in range(2):
                    pltpu.make_async_copy(ys_ref.at[pl.ds(p_refs[e][0, 0, r], 1)],
                                          ybuf.at[to_slot, e, blk, pl.ds(k, 1)], sems.at[to_slot]).start(priority=e)
            return carry

        lax.fori_loop(0, tm // SUBLANES, body, 0)

    @pl.when(i == 0)
    def _first():
        issue((pos0_ref, pos1_ref), 0)

    @pl.when(i + 1 < n_steps)
    def _ahead():
        issue((nxt0_ref, nxt1_ref), 1 - slot)

    for e in range(2):
        pltpu.make_async_copy(ybuf.at[slot, e], ybuf.at[slot, e], sems.at[slot]).wait()
    w = wcol_ref[...]
    y0 = ybuf[slot, 0].reshape(tm, D_MODEL)
    y1 = ybuf[slot, 1].reshape(tm, D_MODEL)
    out_ref[...] = xmid_ref[...] + w[:, 0:1] * y0 + w[:, 1:2] * y1


def _combine(pos, ys, xmid, wcol, tm):
    t_rows = xmid.shape[0]
    n_steps = t_rows // tm
    assert tm % SUBLANES == 0
    pos3 = _step_major(pos, tm)
    row = lambda i: (i, 0)
    cur = pl.BlockSpec((1, 1, tm), lambda i: (i, 0, 0), memory_space=pltpu.SMEM)
    nxt = pl.BlockSpec((1, 1, tm), lambda i: (jnp.minimum(i + 1, n_steps - 1), 0, 0), memory_space=pltpu.SMEM)
    return pl.pallas_call(
        functools.partial(_combine_kernel, tm=tm, n_steps=n_steps),
        grid=(n_steps,),
        in_specs=[cur, cur, nxt, nxt,
                  pl.BlockSpec(memory_space=pl.ANY),
                  pl.BlockSpec((tm, D_MODEL), row),
                  pl.BlockSpec((tm, LANES), row)],
        out_specs=pl.BlockSpec((tm, D_MODEL), row),
        out_shape=jax.ShapeDtypeStruct((t_rows, D_MODEL), F32),
        scratch_shapes=[pltpu.VMEM((2, 2, tm // SUBLANES, SUBLANES, D_MODEL), F32), pltpu.SemaphoreType.DMA((2,))],
        compiler_params=_params(1),
        name="combine",
    )(*pos3, *pos3, ys, xmid, wcol)


def _bucket(ends, idx):
    n = jnp.sum((ends[None, :] <= idx[:, None]).astype(jnp.int32), axis=1)
    return jnp.minimum(n, ends.shape[0] - 1)


def _tile_for(rows, pref):
    tm = min(pref, rows)
    assert rows % tm == 0
    return tm


def kernel(x_prompt, x_sample, cache_ret_state, cache_swa_k, cache_swa_v, meta_tokens, norm1_g, w_in, q_norm_g,
           k_norm_g, ret_norm_g, attn_sinks, w_out, norm2_g, w_group, w_expert, w_gate, w_up, w_down):
    assert norm1_g.shape[0] == 1, "single-layer trunk"
    bp, lp, _ = x_prompt.shape
    bs, ls, _ = x_sample.shape
    n_pairs = RET_HEADS // 2

    g1 = norm1_g[0][None, :]
    g2 = norm2_g[0][None, :]
    w_in_bf = w_in[0].astype(BF16)
    w_out_bf = w_out[0].astype(BF16)
    qg2 = jnp.tile(q_norm_g[0], 2)[None, :]
    kg2 = jnp.tile(k_norm_g[0], 2)[None, :]
    rng = ret_norm_g[0].reshape(RET_HEADS, RET_DV)
    sink_tab = jnp.broadcast_to(jnp.repeat(attn_sinks[0], CHUNK).reshape(SWA_KV_HEADS, 1, 4 * CHUNK),
                                (SWA_KV_HEADS, SUBLANES, 4 * CHUNK))
    w_router_t = jnp.zeros((ROUTER_ROWS, D_MODEL), F32)
    w_router_t = w_router_t.at[0:N_GROUPS].set(w_group[0].T)
    w_router_t = w_router_t.at[ROUTER_EXPERT_ROW0:ROUTER_EXPERT_ROW0 + N_EXPERTS].set(w_expert[0].T)
    w_router_t_bf = w_router_t.astype(BF16)

    meta_rows = 2 * CHUNK
    m_pad = jnp.zeros((meta_rows, D_MODEL), F32).at[0:N_META].set(meta_tokens)
    m_rqk, m_rv, _, _, m_skv = _proj(m_pad, jnp.arange(meta_rows, dtype=jnp.int32), meta_rows, g1, w_in_bf, qg2, kg2)
    s_meta = _meta_state(m_rqk, m_rv)[None]
    meta_kv = m_skv[0:N_META]

    groups = [
        dict(x=x_prompt.reshape(bp * lp, D_MODEL), n=bp, seq=lp, pos0=N_META, has_hist=False, s0=s_meta,
             hist=jnp.zeros((1, WINDOW, 2 * SWA_KV), F32)),
        dict(x=x_sample.reshape(bs * ls, D_MODEL), n=bs, seq=ls, pos0=N_META + PAST_LEN, has_hist=True,
             s0=cache_ret_state[0].reshape(bs, n_pairs, 2 * RET_DK, RET_DV),
             hist=jnp.concatenate([cache_swa_k[0].reshape(bs, WINDOW, SWA_KV),
                                   cache_swa_v[0].reshape(bs, WINDOW, SWA_KV)], axis=-1)),
    ]

    base_cnt = jnp.zeros((N_EXPERTS, LANES), F32)
    for g in groups:
        rows = g["n"] * g["seq"]
        tm = _tile_for(rows, PROJ_TILE)
        pos = g["pos0"] + jnp.arange(g["seq"], dtype=jnp.int32)
        if g["seq"] < tm:
            assert tm % g["seq"] == 0
            pos = jnp.tile(pos, tm // g["seq"])
        else:
            assert g["seq"] % tm == 0
        rqk, rv, gate, sq, skv = _proj(g["x"], pos, tm, g1, w_in_bf, qg2, kg2)
        tl = min(ATTN_TILE, g["seq"])
        omix, s_out, kv_out = _attention(rqk, rv, gate, sq, skv, meta_kv, g["hist"], g["s0"], rng, sink_tab,
                                         n_streams=g["n"], seq=g["seq"], tl=tl, has_hist=g["has_hist"])
        xmid, wcol, route_t, base_cnt = _post(omix, g["x"], w_out_bf, g2, w_router_t_bf, base_cnt,
                                              _tile_for(rows, POST_TILE))
        g.update(xmid=xmid, wcol=wcol, route_t=route_t, s_out=s_out, kv_out=kv_out)

    te = EXPERT_TILE
    total_rows = sum(g["n"] * g["seq"] for g in groups)
    assert (2 * total_rows) % te == 0
    n_row_tiles = (2 * total_rows) // te
    counts = base_cnt[:, 0].astype(jnp.int32)
    off = jnp.cumsum(counts) - counts
    first_tile = off // te
    n_vis_e = jnp.where(counts > 0, (off + counts - 1) // te - first_tile + 1, 0)
    vis_end = jnp.cumsum(n_vis_e)
    n_vis = vis_end[-1:].astype(jnp.int32)
    v = jnp.arange(n_row_tiles + N_EXPERTS, dtype=jnp.int32)
    vis_expert = _bucket(vis_end, v)
    pick = lambda table: jnp.sum(jnp.where(vis_expert[:, None] == jnp.arange(N_EXPERTS, dtype=jnp.int32)[None, :],
                                           table[None, :], 0), axis=1)
    vis_tile = jnp.clip(pick(first_tile) + v - pick(vis_end - n_vis_e), 0, n_row_tiles - 1).astype(jnp.int32)
    vis_lo = jnp.clip(pick(off) - vis_tile * te, 0, te).astype(jnp.int32)
    vis_hi = jnp.clip(pick(off + counts) - vis_tile * te, 0, te).astype(jnp.int32)

    for g in groups:
        eid = g["route_t"][0:2].astype(jnp.int32)
        off_sel = jnp.sum(jnp.where(eid[None] == jnp.arange(N_EXPERTS, dtype=jnp.int32)[:, None, None],
                                    off[:, None, None], 0), axis=0)
        g["pos"] = (off_sel + g["route_t"][4:6].astype(jnp.int32)).astype(jnp.int32)
    xs = _dispatch([g["pos"] for g in groups], [g["xmid"] for g in groups], g2,
                   _tile_for(min(g["n"] * g["seq"] for g in groups), MOVE_TILE))

    ys = _experts(vis_tile, vis_expert.astype(jnp.int32), vis_lo, vis_hi, n_vis, xs, w_gate[0], w_up[0], w_down[0], te)

    outs = []
    for g in groups:
        rows = g["n"] * g["seq"]
        y = _combine(g["pos"], ys, g["xmid"], g["wcol"], _tile_for(rows, MOVE_TILE))
        outs.append(y.reshape(g["n"], g["seq"], D_MODEL))

    def caches(g):
        kv = g["kv_out"]
        k = kv[:, :, 0:SWA_KV].reshape(g["n"], WINDOW, SWA_KV_HEADS, SWA_HD)[None]
        v = kv[:, :, SWA_KV:2 * SWA_KV].reshape(g["n"], WINDOW, SWA_KV_HEADS, SWA_HD)[None]
        s = g["s_out"].reshape(g["n"], RET_HEADS, RET_DK, RET_DV)[None]
        return s, k, v

    sp, kp, vp = caches(groups[0])
    ss, ks, vs = caches(groups[1])
    return (outs[0], outs[1], sp, kp, vp, ss, ks, vs)
```

```python
import functools

import numpy as np
import jax
import jax.numpy as jnp
from jax import lax
from jax.experimental import pallas as pl
from jax.experimental.pallas import tpu as pltpu

F32 = jnp.float32
BF16 = jnp.bfloat16

D_MODEL = 1024
PAST_LEN = 4096
CHUNK = 64
N_META = 16
RET_HEADS = 4
RET_DK = 64
RET_DV = 128
SWA_HEADS = 8
SWA_KV_HEADS = 2
SWA_HD = 64
WINDOW = 128
ROPE_THETA = 10000.0
N_GROUPS = 4
EXPERTS_PER_GROUP = 8
N_EXPERTS = N_GROUPS * EXPERTS_PER_GROUP
EXPERT_FF = 256
EPS = 1e-6
NEG_INF = -1e30
LOG2E = float(np.log2(np.e))
RET_Q = RET_HEADS * RET_DK
RET_V = RET_HEADS * RET_DV
SWA_Q = SWA_HEADS * SWA_HD
SWA_KV = SWA_KV_HEADS * SWA_HD
MIX_WIDTH = RET_V + SWA_Q
IN_WIDTH = 2 * RET_Q + 2 * RET_V + SWA_Q + 2 * SWA_KV

LANES = 128
PROJ_TILE = 1024
POST_TILE = 1024
ATTN_TILE = 256
EXPERT_TILE = 1024
MOVE_TILE = 512
SUBLANES = 8
ROUTER_EXPERT_ROW0 = 8
ROUTER_ROWS = 64
META_ROWS = 64
VMEM_LIMIT = 56 * 1024 * 1024

_LOG_G = [float(np.log1p(-np.exp2(-5.0 - h))) for h in range(RET_HEADS)]


def _params(n_axes):
    return pltpu.CompilerParams(dimension_semantics=("arbitrary",) * n_axes, vmem_limit_bytes=VMEM_LIMIT)


def _split_bf16(a):
    hi = a.astype(BF16)
    return hi, (a - hi.astype(F32)).astype(BF16)


def _split_dot(a, w2):
    hi, lo = _split_bf16(a)
    return jnp.dot(jnp.concatenate([hi, lo], axis=1), w2, preferred_element_type=F32)


def _lane_sum(a):
    return _split_dot(a, jnp.ones((2 * LANES, LANES), BF16))


def _head_sum_matrix():
    i = lax.broadcasted_iota(jnp.int32, (2 * LANES, LANES), 0) % LANES
    j = lax.broadcasted_iota(jnp.int32, (2 * LANES, LANES), 1)
    return jnp.where((i < SWA_HD) == (j < SWA_HD), 1.0, 0.0).astype(BF16)


def _rope(t, c, s1, s2):
    half = SWA_HD // 2
    return t * c + pltpu.roll(t, LANES - half, 1) * s1 + pltpu.roll(t, half, 1) * s2


def _head_rms(t, g, head_w):
    ms = _split_dot(t * t, head_w) * (1.0 / SWA_HD)
    return t * lax.rsqrt(ms + EPS) * g


def _row_rms(x):
    n_tiles = x.shape[1] // LANES
    ss = x[:, 0:LANES] * x[:, 0:LANES]
    for j in range(1, n_tiles):
        ss = ss + x[:, j * LANES:(j + 1) * LANES] * x[:, j * LANES:(j + 1) * LANES]
    r = lax.rsqrt(_lane_sum(ss) * (1.0 / x.shape[1]) + EPS)
    return x * jnp.concatenate([r] * n_tiles, axis=1)


def _proj_kernel(x_ref, g1_ref, w_ref, qg_ref, kg_ref, cos_ref, s1_ref, s2_ref,
                 rqk_ref, rv_ref, gate_ref, sq_ref, skv_ref):
    xn = (_row_rms(x_ref[...]) * g1_ref[...]).astype(BF16)
    c, s1, s2 = cos_ref[...], s1_ref[...], s2_ref[...]
    head_w = _head_sum_matrix()

    def seg(a, b):
        return jnp.dot(xn, w_ref[:, a:b], preferred_element_type=F32)

    def tile(h, j):
        return h[:, j * LANES:(j + 1) * LANES]

    h = seg(0, 2 * RET_Q)
    for j in range(2):
        rqk_ref[:, j * LANES:(j + 1) * LANES] = _rope(tile(h, j), c, s1, s2).astype(BF16)
    for j in range(2, 4):
        rqk_ref[:, j * LANES:(j + 1) * LANES] = (_rope(tile(h, j), c, s1, s2) * (RET_DK ** -0.5)).astype(BF16)
    a = 2 * RET_Q
    rv_ref[...] = seg(a, a + RET_V).astype(BF16)
    a += RET_V
    g = seg(a, a + RET_V)
    gate_ref[...] = (g * jax.nn.sigmoid(g)).astype(BF16)
    a += RET_V
    h = seg(a, a + SWA_Q)
    qg = qg_ref[...]
    for j in range(SWA_Q // LANES):
        sq_ref[:, j * LANES:(j + 1) * LANES] = _rope(_head_rms(tile(h, j), qg, head_w), c, s1, s2).astype(BF16)
    a += SWA_Q
    h = seg(a, a + 2 * SWA_KV)
    skv_ref[:, 0:LANES] = _rope(_head_rms(tile(h, 0), kg_ref[...], head_w), c, s1, s2)
    skv_ref[:, LANES:2 * LANES] = tile(h, 1)


def _rope_tables(pos):
    half = SWA_HD // 2
    inv = ROPE_THETA ** (-jnp.arange(half, dtype=F32) / half)
    ang = pos.astype(F32)[:, None] * inv[None, :]
    cos, sin = jnp.cos(ang), jnp.sin(ang)
    z = jnp.zeros_like(sin)
    return (jnp.tile(cos, (1, 4)),
            jnp.tile(jnp.concatenate([-sin, z], axis=1), (1, 2)),
            jnp.tile(jnp.concatenate([z, sin], axis=1), (1, 2)))


def _proj(x2d, pos_rows, tm, g1, w_in_bf, qg2, kg2):
    t_rows = x2d.shape[0]
    n_tiles = t_rows // tm
    n_pos_tiles = pos_rows.shape[0] // tm
    cos, s1, s2 = _rope_tables(pos_rows)
    row = lambda i: (i, 0)
    const = lambda i: (0, 0)
    tab = lambda i: (i % n_pos_tiles, 0)
    return pl.pallas_call(
        _proj_kernel,
        grid=(n_tiles,),
        in_specs=[pl.BlockSpec((tm, D_MODEL), row),
                  pl.BlockSpec((1, D_MODEL), const),
                  pl.BlockSpec((D_MODEL, IN_WIDTH), const),
                  pl.BlockSpec((1, LANES), const),
                  pl.BlockSpec((1, LANES), const),
                  pl.BlockSpec((tm, LANES), tab),
                  pl.BlockSpec((tm, LANES), tab),
                  pl.BlockSpec((tm, LANES), tab)],
        out_specs=[pl.BlockSpec((tm, 2 * RET_Q), row),
                   pl.BlockSpec((tm, RET_V), row),
                   pl.BlockSpec((tm, RET_V), row),
                   pl.BlockSpec((tm, SWA_Q), row),
                   pl.BlockSpec((tm, 2 * SWA_KV), row)],
        out_shape=[jax.ShapeDtypeStruct((t_rows, 2 * RET_Q), BF16),
                   jax.ShapeDtypeStruct((t_rows, RET_V), BF16),
                   jax.ShapeDtypeStruct((t_rows, RET_V), BF16),
                   jax.ShapeDtypeStruct((t_rows, SWA_Q), BF16),
                   jax.ShapeDtypeStruct((t_rows, 2 * SWA_KV), F32)],
        compiler_params=_params(1),
        name="proj",
    )(x2d, g1, w_in_bf, qg2, kg2, cos, s1, s2)


def _pair_update(k_bf, v0_bf, v1_bf, wt):
    kw = (k_bf.astype(F32) * wt).astype(BF16)
    dn = (((0,), (0,)), ((), ()))
    a0 = lax.dot_general(kw, v0_bf, dn, preferred_element_type=F32)
    a1 = lax.dot_general(kw, v1_bf, dn, preferred_element_type=F32)
    top = lax.broadcasted_iota(jnp.int32, a0.shape, 0) < RET_DK
    return jnp.where(top, a0, a1)


def _decay_rows(n, pair, rows_back_from):
    i = lax.broadcasted_iota(jnp.int32, (n, LANES), 0).astype(F32)
    lane = lax.broadcasted_iota(jnp.int32, (n, LANES), 1)
    lg = jnp.where(lane < RET_DK, _LOG_G[2 * pair], _LOG_G[2 * pair + 1])
    return jnp.exp((rows_back_from - i) * lg)


def _meta_state_kernel(rqk_ref, rv_ref, s_ref, *, n_rows):
    for p in range(RET_HEADS // 2):
        k = rqk_ref[:, RET_Q + p * LANES:RET_Q + (p + 1) * LANES]
        wt = _decay_rows(n_rows, p, float(N_META - 1))
        s_ref[p] = _pair_update(k, rv_ref[:, (2 * p) * LANES:(2 * p + 1) * LANES],
                                rv_ref[:, (2 * p + 1) * LANES:(2 * p + 2) * LANES], wt)


def _meta_state(m_rqk, m_rv):
    n_rows = m_rqk.shape[0]
    return pl.pallas_call(
        functools.partial(_meta_state_kernel, n_rows=n_rows),
        out_shape=jax.ShapeDtypeStruct((RET_HEADS // 2, 2 * RET_DK, RET_DV), F32),
        name="meta_state",
    )(m_rqk, m_rv)


def _dup_halves(a, lo_mask):
    sw = pltpu.roll(a, SWA_HD, 1)
    return jnp.where(lo_mask, a, sw), jnp.where(lo_mask, sw, a)


def _attn_kernel(rqk_ref, rv_ref, gate_ref, sq_ref, skv_ref, meta_ref, hist_ref, s0_ref, rng_ref, sink_ref,
                 omix_ref, sout_ref, kvout_ref,
                 s_scr, kd_scr, vd_scr, mk_scr, mv_scr, dec_scr, wt_scr, cs_scr, gam_scr,
                 *, tl, has_hist):
    b = pl.program_id(0)
    t = pl.program_id(1)
    nt = pl.num_programs(1)
    n_chunks = tl // CHUNK
    n_pairs = RET_HEADS // 2
    lo_tl = lax.broadcasted_iota(jnp.int32, (tl, LANES), 1) < SWA_HD
    lo_c = lax.broadcasted_iota(jnp.int32, (CHUNK, LANES), 1) < SWA_HD

    @pl.when((b == 0) & (t == 0))
    def _tables():
        i = lax.broadcasted_iota(jnp.int32, (tl, tl), 0)
        j = lax.broadcasted_iota(jnp.int32, (tl, tl), 1)
        diff = (i - j).astype(F32)
        row = lax.broadcasted_iota(jnp.int32, (tl, LANES), 0).astype(F32)
        for h in range(RET_HEADS):
            dec_scr[h] = jnp.where(diff >= 0.0, jnp.exp(jnp.maximum(diff, 0.0) * _LOG_G[h]), 0.0)
            cs_scr[h] = jnp.exp((row + 1.0) * _LOG_G[h])
        top = lax.broadcasted_iota(jnp.int32, (2 * RET_DK, RET_DV), 0) < RET_DK
        for p in range(n_pairs):
            wt_scr[p] = _decay_rows(tl, p, float(tl - 1))
            gam_scr[p] = jnp.where(top, jnp.exp(jnp.float32(tl * _LOG_G[2 * p])), jnp.exp(jnp.float32(tl * _LOG_G[2 * p + 1])))
        lo_m = lax.broadcasted_iota(jnp.int32, (N_META, LANES), 1) < SWA_HD
        mk0, mk1 = _dup_halves(meta_ref[:, 0:LANES], lo_m)
        mv0, mv1 = _dup_halves(meta_ref[:, LANES:2 * LANES], lo_m)
        mk_scr[...] = jnp.zeros(mk_scr.shape, BF16)
        mv_scr[...] = jnp.zeros(mv_scr.shape, BF16)
        mk_scr[0, 0:N_META] = mk0.astype(BF16)
        mk_scr[1, 0:N_META] = mk1.astype(BF16)
        mv_scr[0, 0:N_META] = mv0.astype(BF16)
        mv_scr[1, 0:N_META] = mv1.astype(BF16)

    @pl.when(t == 0)
    def _stream_start():
        s_scr[...] = s0_ref[0]
        if has_hist:
            lo_w = lax.broadcasted_iota(jnp.int32, (WINDOW, LANES), 1) < SWA_HD
            k0, k1 = _dup_halves(hist_ref[0, :, 0:LANES], lo_w)
            v0, v1 = _dup_halves(hist_ref[0, :, LANES:2 * LANES], lo_w)
            kd_scr[0, 0:WINDOW] = k0.astype(BF16)
            kd_scr[1, 0:WINDOW] = k1.astype(BF16)
            vd_scr[0, 0:WINDOW] = v0.astype(BF16)
            vd_scr[1, 0:WINDOW] = v1.astype(BF16)
        else:
            z = jnp.zeros((WINDOW, LANES), BF16)
            for kv in range(SWA_KV_HEADS):
                kd_scr[kv, 0:WINDOW] = z
                vd_scr[kv, 0:WINDOW] = z

    k0, k1 = _dup_halves(skv_ref[:, 0:LANES], lo_tl)
    v0, v1 = _dup_halves(skv_ref[:, LANES:2 * LANES], lo_tl)
    kd_scr[0, WINDOW:WINDOW + tl] = k0.astype(BF16)
    kd_scr[1, WINDOW:WINDOW + tl] = k1.astype(BF16)
    vd_scr[0, WINDOW:WINDOW + tl] = v0.astype(BF16)
    vd_scr[1, WINDOW:WINDOW + tl] = v1.astype(BF16)

    band = WINDOW + CHUNK
    n_keys = META_ROWS + band
    n_q = 4 * CHUNK
    scale2 = (SWA_HD ** -0.5) * LOG2E
    krow = lax.broadcasted_iota(jnp.int32, (n_keys, n_q), 0)
    zero_c = jnp.zeros((CHUNK, LANES), BF16)
    ones_v = jnp.ones((n_keys, LANES), BF16)
    for c in range(n_chunks):
        if has_hist:
            first_valid = META_ROWS
        else:
            first_valid = jnp.where(t == 0, max(META_ROWS + WINDOW - c * CHUNK, META_ROWS), META_ROWS)
        valid_t = (krow < N_META) | (krow >= first_valid)
        r0 = c * CHUNK
        for kv in range(SWA_KV_HEADS):
            keys = jnp.concatenate([mk_scr[kv], kd_scr[kv, r0:r0 + band]], axis=0)
            vals = jnp.concatenate([mv_scr[kv], vd_scr[kv, r0:r0 + band]], axis=0)
            qa = sq_ref[r0:r0 + CHUNK, (2 * kv) * LANES:(2 * kv + 1) * LANES]
            qb = sq_ref[r0:r0 + CHUNK, (2 * kv + 1) * LANES:(2 * kv + 2) * LANES]
            lhs = jnp.concatenate([jnp.where(lo_c, qa, zero_c), jnp.where(lo_c, zero_c, qa),
                                   jnp.where(lo_c, qb, zero_c), jnp.where(lo_c, zero_c, qb)], axis=0)
            s_t = lax.dot_general(keys, lhs, (((1,), (1,)), ((), ())), preferred_element_type=F32) * scale2
            s_t = jnp.where(valid_t, s_t, NEG_INF)
            s_t = jnp.where(krow == N_META, sink_ref[kv, 0:1, :] * LOG2E, s_t)
            e_t = jnp.exp2(s_t - jnp.max(s_t, axis=0, keepdims=True)).astype(BF16)
            ov = lax.dot_general(e_t, jnp.concatenate([vals, ones_v], axis=1), (((0,), (0,)), ((), ())),
                                 preferred_element_type=F32)
            o = ov[:, 0:LANES] * (1.0 / ov[:, LANES:2 * LANES])
            oa = jnp.where(lo_c, o[0:CHUNK], o[CHUNK:2 * CHUNK])
            ob = jnp.where(lo_c, o[2 * CHUNK:3 * CHUNK], o[3 * CHUNK:4 * CHUNK])
            base = RET_V + (2 * kv) * LANES
            omix_ref[r0:r0 + CHUNK, base:base + LANES] = oa.astype(BF16)
            omix_ref[r0:r0 + CHUNK, base + LANES:base + 2 * LANES] = ob.astype(BF16)

    zero_t = jnp.zeros((tl, LANES), BF16)
    for p in range(n_pairs):
        q = rqk_ref[:, p * LANES:(p + 1) * LANES]
        k = rqk_ref[:, RET_Q + p * LANES:RET_Q + (p + 1) * LANES]
        lhs = jnp.concatenate([jnp.where(lo_tl, q, zero_t), jnp.where(lo_tl, zero_t, q)], axis=0)
        s = lax.dot_general(lhs, k, (((1,), (1,)), ((), ())), preferred_element_type=F32)
        cross = jnp.dot(lhs, s_scr[p].astype(BF16), preferred_element_type=F32)
        for i in range(2):
            h = 2 * p + i
            v = rv_ref[:, h * LANES:(h + 1) * LANES]
            a = (s[i * tl:(i + 1) * tl] * dec_scr[h]).astype(BF16)
            o = jnp.dot(a, v, preferred_element_type=F32) + cross[i * tl:(i + 1) * tl] * cs_scr[h]
            r = o * lax.rsqrt(_lane_sum(o * o) * (1.0 / RET_DV) + EPS) * rng_ref[h:h + 1, :]
            omix_ref[:, h * LANES:(h + 1) * LANES] = (r * gate_ref[:, h * LANES:(h + 1) * LANES].astype(F32)).astype(BF16)
        u = _pair_update(k, rv_ref[:, (2 * p) * LANES:(2 * p + 1) * LANES],
                         rv_ref[:, (2 * p + 1) * LANES:(2 * p + 2) * LANES], wt_scr[p])
        s_scr[p] = gam_scr[p] * s_scr[p] + u

    if tl >= WINDOW:
        @pl.when(t + 1 < nt)
        def _carry_window():
            for kv in range(SWA_KV_HEADS):
                kd_scr[kv, 0:WINDOW] = kd_scr[kv, tl:tl + WINDOW]
                vd_scr[kv, 0:WINDOW] = vd_scr[kv, tl:tl + WINDOW]

    @pl.when(t + 1 == nt)
    def _stream_end():
        sout_ref[0] = s_scr[...]
        if tl >= WINDOW:
            kvout_ref[0] = skv_ref[tl - WINDOW:tl, :]
        else:
            kvout_ref[0, 0:WINDOW - tl] = hist_ref[0, tl:WINDOW, :]
            kvout_ref[0, WINDOW - tl:WINDOW] = skv_ref[...]


def _attention(rqk, rv, gate, sq, skv, meta_kv, hist_kv, s0, rng, sink_tab, *, n_streams, seq, tl, has_hist):
    nt = seq // tl
    assert tl % CHUNK == 0 and seq % tl == 0
    assert tl >= WINDOW or (nt == 1 and has_hist)
    n_pairs = RET_HEADS // 2
    s0_shared = s0.shape[0] == 1
    row = lambda b, t: (b * nt + t, 0)
    const2 = lambda b, t: (0, 0)
    const3 = lambda b, t: (0, 0, 0)
    per_b3 = lambda b, t: (b, 0, 0)
    s0_map = (lambda b, t: (0, 0, 0, 0)) if s0_shared else (lambda b, t: (b, 0, 0, 0))
    hist_map = per_b3 if has_hist else const3
    rows = n_streams * seq
    return pl.pallas_call(
        functools.partial(_attn_kernel, tl=tl, has_hist=has_hist),
        grid=(n_streams, nt),
        in_specs=[pl.BlockSpec((tl, 2 * RET_Q), row),
                  pl.BlockSpec((tl, RET_V), row),
                  pl.BlockSpec((tl, RET_V), row),
                  pl.BlockSpec((tl, SWA_Q), row),
                  pl.BlockSpec((tl, 2 * SWA_KV), row),
                  pl.BlockSpec((N_META, 2 * SWA_KV), const2),
                  pl.BlockSpec((1, WINDOW, 2 * SWA_KV), hist_map),
                  pl.BlockSpec((1, n_pairs, 2 * RET_DK, RET_DV), s0_map),
                  pl.BlockSpec((RET_HEADS, RET_DV), const2),
                  pl.BlockSpec((SWA_KV_HEADS, SUBLANES, 4 * CHUNK), const3)],
        out_specs=[pl.BlockSpec((tl, MIX_WIDTH), row),
                   pl.BlockSpec((1, n_pairs, 2 * RET_DK, RET_DV), lambda b, t: (b, 0, 0, 0)),
                   pl.BlockSpec((1, WINDOW, 2 * SWA_KV), per_b3)],
        out_shape=[jax.ShapeDtypeStruct((rows, MIX_WIDTH), BF16),
                   jax.ShapeDtypeStruct((n_streams, n_pairs, 2 * RET_DK, RET_DV), F32),
                   jax.ShapeDtypeStruct((n_streams, WINDOW, 2 * SWA_KV), F32)],
        scratch_shapes=[pltpu.VMEM((n_pairs, 2 * RET_DK, RET_DV), F32),
                        pltpu.VMEM((SWA_KV_HEADS, WINDOW + tl, LANES), BF16),
                        pltpu.VMEM((SWA_KV_HEADS, WINDOW + tl, LANES), BF16),
                        pltpu.VMEM((SWA_KV_HEADS, META_ROWS, LANES), BF16),
                        pltpu.VMEM((SWA_KV_HEADS, META_ROWS, LANES), BF16),
                        pltpu.VMEM((RET_HEADS, tl, tl), F32),
                        pltpu.VMEM((n_pairs, tl, LANES), F32),
                        pltpu.VMEM((RET_HEADS, tl, RET_DV), F32),
                        pltpu.VMEM((n_pairs, 2 * RET_DK, RET_DV), F32)],
        compiler_params=_params(2),
        name="attention",
    )(rqk, rv, gate, sq, skv, meta_kv, hist_kv, s0, rng, sink_tab)


def _post_kernel(omix_ref, x_ref, wout_ref, g2_ref, wrt_ref, base_ref,
                 xmid_ref, wcol_ref, rt_ref, cnt_ref, tri_scr, run_scr):
    i = pl.program_id(0)
    tm = x_ref.shape[0]

    @pl.when(i == 0)
    def _init():
        r = lax.broadcasted_iota(jnp.int32, (tm, tm), 0)
        c = lax.broadcasted_iota(jnp.int32, (tm, tm), 1)
        tri_scr[...] = jnp.where(r < c, 1.0, 0.0).astype(BF16)
        run_scr[...] = base_ref[...]

    xm = x_ref[...] + jnp.dot(omix_ref[...], wout_ref[...], preferred_element_type=F32)
    xmid_ref[...] = xm
    hn = _row_rms(xm) * g2_ref[...]
    lt = lax.dot_general(wrt_ref[...], hn.astype(BF16), (((1,), (1,)), ((), ())), preferred_element_type=F32)
    row8 = lax.broadcasted_iota(jnp.int32, (SUBLANES, tm), 0)
    big = jnp.int32(SUBLANES)
    gl = jnp.where(row8 < N_GROUPS, lt[0:SUBLANES], NEG_INF)
    gmax = jnp.max(gl, axis=0, keepdims=True)
    gsum = jnp.sum(jnp.exp(gl - gmax), axis=0, keepdims=True)
    g_sel = jnp.min(jnp.where(gl == gmax, row8, big), axis=0, keepdims=True)
    p_sel = 1.0 / gsum
    el = lt[ROUTER_EXPERT_ROW0:ROUTER_EXPERT_ROW0 + EXPERTS_PER_GROUP]
    for g in range(1, N_GROUPS):
        lo = ROUTER_EXPERT_ROW0 + g * EXPERTS_PER_GROUP
        el = jnp.where(g_sel == g, lt[lo:lo + EXPERTS_PER_GROUP], el)
    m1 = jnp.max(el, axis=0, keepdims=True)
    i1 = jnp.min(jnp.where(el == m1, row8, big), axis=0, keepdims=True)
    el2 = jnp.where(row8 == i1, NEG_INF, el)
    m2 = jnp.max(el2, axis=0, keepdims=True)
    i2 = jnp.min(jnp.where(el2 == m2, row8, big), axis=0, keepdims=True)
    e2 = jnp.exp(m2 - m1)
    inv = 1.0 / (1.0 + e2)
    w1 = p_sel * inv
    w2 = p_sel * (e2 * inv)
    eid1 = g_sel * EXPERTS_PER_GROUP + i1
    eid2 = g_sel * EXPERTS_PER_GROUP + i2

    rowe = lax.broadcasted_iota(jnp.int32, (N_EXPERTS, tm), 0)
    oh = jnp.where((rowe == eid1) | (rowe == eid2), 1.0, 0.0).astype(BF16)
    run = run_scr[...]
    pref = jnp.dot(oh, tri_scr[...], preferred_element_type=F32) + jnp.concatenate([run] * (tm // LANES), axis=1)
    r1 = jnp.sum(jnp.where(rowe == eid1, pref, 0.0), axis=0, keepdims=True)
    r2 = jnp.sum(jnp.where(rowe == eid2, pref, 0.0), axis=0, keepdims=True)
    run = run + jnp.dot(oh, jnp.ones((tm, LANES), BF16), preferred_element_type=F32)
    run_scr[...] = run
    cnt_ref[...] = run

    out = jnp.zeros((SUBLANES, tm), F32)
    for k, v in enumerate([eid1.astype(F32), eid2.astype(F32), w1, w2, r1, r2]):
        out = jnp.where(row8 == k, v, out)
    rt_ref[...] = out
    rowl = lax.broadcasted_iota(jnp.int32, (LANES, tm), 0)
    wcol_ref[...] = jnp.where(rowl == 0, w1, jnp.where(rowl == 1, w2, 0.0)).T


def _post(omix, x2d, w_out_bf, g2, w_router_t_bf, base_cnt, tm):
    t_rows = x2d.shape[0]
    assert tm % LANES == 0
    row = lambda i: (i, 0)
    const = lambda i: (0, 0)
    return pl.pallas_call(
        _post_kernel,
        grid=(t_rows // tm,),
        in_specs=[pl.BlockSpec((tm, MIX_WIDTH), row),
                  pl.BlockSpec((tm, D_MODEL), row),
                  pl.BlockSpec((MIX_WIDTH, D_MODEL), const),
                  pl.BlockSpec((1, D_MODEL), const),
                  pl.BlockSpec((ROUTER_ROWS, D_MODEL), const),
                  pl.BlockSpec((N_EXPERTS, LANES), const)],
        out_specs=[pl.BlockSpec((tm, D_MODEL), row),
                   pl.BlockSpec((tm, LANES), row),
                   pl.BlockSpec((SUBLANES, tm), lambda i: (0, i)),
                   pl.BlockSpec((N_EXPERTS, LANES), const)],
        out_shape=[jax.ShapeDtypeStruct((t_rows, D_MODEL), F32),
                   jax.ShapeDtypeStruct((t_rows, LANES), F32),
                   jax.ShapeDtypeStruct((SUBLANES, t_rows), F32),
                   jax.ShapeDtypeStruct((N_EXPERTS, LANES), F32)],
        scratch_shapes=[pltpu.VMEM((tm, tm), BF16), pltpu.VMEM((N_EXPERTS, LANES), F32)],
        compiler_params=_params(1),
        name="post",
    )(omix, x2d, w_out_bf, g2, w_router_t_bf, base_cnt)


def _step_major(pos, tm):
    return [pos[e].reshape(-1, 1, tm) for e in range(2)]


def _dispatch_kernel(pos0_ref, pos1_ref, *refs, tm, group_steps):
    pos_refs = (pos0_ref, pos1_ref)
    n_g = len(group_steps)
    xmid_refs = refs[:n_g]
    g2_ref, xs_ref, hbuf, sems = refs[n_g:]
    n_steps = sum(group_steps)
    i = pl.program_id(0)
    slot = i % 2

    first = 0
    for xmid_ref, steps in zip(xmid_refs, group_steps):
        @pl.when((i >= first) & (i < first + steps))
        def _normalise(xmid_ref=xmid_ref):
            hn = _row_rms(xmid_ref[...]) * g2_ref[...]
            hbuf[slot] = hn.reshape(tm // SUBLANES, SUBLANES, D_MODEL)
        first += steps

    def body(blk, carry):
        for k in range(SUBLANES):
            r = blk * SUBLANES + k
            for e in range(2):
                pltpu.make_async_copy(hbuf.at[slot, blk, pl.ds(k, 1)], xs_ref.at[pl.ds(pos_refs[e][0, 0, r], 1)],
                                      sems.at[slot]).start(priority=e)
        return carry

    lax.fori_loop(0, tm // SUBLANES, body, 0)

    def drain(which):
        pltpu.make_async_copy(xs_ref.at[pl.ds(0, 2 * tm)], xs_ref.at[pl.ds(0, 2 * tm)], sems.at[which]).wait()

    @pl.when(i > 0)
    def _previous():
        drain(1 - slot)

    @pl.when(i == n_steps - 1)
    def _last():
        drain(slot)


def _dispatch(pos_list, xmid_list, g2, tm):
    group_steps = tuple(x.shape[0] // tm for x in xmid_list)
    n_steps = sum(group_steps)
    assert tm % SUBLANES == 0
    pos3 = [jnp.concatenate(parts, axis=0) for parts in zip(*[_step_major(p, tm) for p in pos_list])]
    smem = pl.BlockSpec((1, 1, tm), lambda i: (i, 0, 0), memory_space=pltpu.SMEM)
    x_specs = []
    first = 0
    for steps in group_steps:
        x_specs.append(pl.BlockSpec((tm, D_MODEL),
                                    lambda i, first=first, steps=steps: (jnp.clip(i - first, 0, steps - 1), 0)))
        first += steps
    n_rows_out = 2 * sum(x.shape[0] for x in xmid_list)
    return pl.pallas_call(
        functools.partial(_dispatch_kernel, tm=tm, group_steps=group_steps),
        grid=(n_steps,),
        in_specs=[smem, smem] + x_specs + [pl.BlockSpec((1, D_MODEL), lambda i: (0, 0))],
        out_specs=pl.BlockSpec(memory_space=pl.ANY),
        out_shape=jax.ShapeDtypeStruct((n_rows_out, D_MODEL), F32),
        scratch_shapes=[pltpu.VMEM((2, tm // SUBLANES, SUBLANES, D_MODEL), F32), pltpu.SemaphoreType.DMA((2,))],
        compiler_params=_params(1),
        name="dispatch",
    )(*pos3, *xmid_list, g2)


def _expert_kernel(vt_ref, ve_ref, lo_ref, hi_ref, nv_ref, x_ref, wg_ref, wu_ref, wd_ref, y_ref, wgu_scr, wd_scr):
    v = pl.program_id(0)

    @pl.when(v < nv_ref[0])
    def _compute():
        @pl.when((v == 0) | (ve_ref[v] != ve_ref[jnp.maximum(v - 1, 0)]))
        def _new_expert():
            wgu_scr[:, 0:EXPERT_FF] = wg_ref[0].astype(BF16)
            wgu_scr[:, EXPERT_FF:2 * EXPERT_FF] = wu_ref[0].astype(BF16)
            wd_scr[...] = wd_ref[0].astype(BF16)

        x = x_ref[...].astype(BF16)
        gu = jnp.dot(x, wgu_scr[...], preferred_element_type=F32)
        g = gu[:, 0:EXPERT_FF]
        a = (g * jax.nn.sigmoid(g) * gu[:, EXPERT_FF:2 * EXPERT_FF]).astype(BF16)
        y = jnp.dot(a, wd_scr[...], preferred_element_type=F32)
        lo, hi = lo_ref[v], hi_ref[v]

        @pl.when(lo == 0)
        def _first_visit():
            y_ref[...] = y

        @pl.when(lo > 0)
        def _later_visit():
            row = lax.broadcasted_iota(jnp.int32, y.shape, 0)
            y_ref[...] = jnp.where((row >= lo) & (row < hi), y, y_ref[...])


def _experts(vis_tile, vis_expert, vis_lo, vis_hi, n_vis, xs, w_gate, w_up, w_down, tm):
    n_steps = vis_tile.shape[0]
    last = lambda v, nv: jnp.minimum(v, nv[0] - 1)
    of_expert = lambda v, vt, ve, lo, hi, nv: (ve[last(v, nv)], 0, 0)
    return pl.pallas_call(
        _expert_kernel,
        grid_spec=pltpu.PrefetchScalarGridSpec(
            num_scalar_prefetch=5,
            grid=(n_steps,),
            in_specs=[pl.BlockSpec((tm, D_MODEL), lambda v, vt, ve, lo, hi, nv: (vt[last(v, nv)], 0)),
                      pl.BlockSpec((1, D_MODEL, EXPERT_FF), of_expert),
                      pl.BlockSpec((1, D_MODEL, EXPERT_FF), of_expert),
                      pl.BlockSpec((1, EXPERT_FF, D_MODEL), of_expert)],
            out_specs=pl.BlockSpec((tm, D_MODEL), lambda v, vt, ve, lo, hi, nv: (vt[last(v, nv)], 0)),
            scratch_shapes=[pltpu.VMEM((D_MODEL, 2 * EXPERT_FF), BF16), pltpu.VMEM((EXPERT_FF, D_MODEL), BF16)]),
        out_shape=jax.ShapeDtypeStruct(xs.shape, F32),
        compiler_params=_params(1),
        name="experts",
    )(vis_tile, vis_expert, vis_lo, vis_hi, n_vis, xs, w_gate, w_up, w_down)


def _combine_kernel(pos0_ref, pos1_ref, nxt0_ref, nxt1_ref, ys_ref, xmid_ref, wcol_ref, out_ref, ybuf, sems,
                    *, tm, n_steps):
    i = pl.program_id(0)
    slot = i % 2

    def issue(p_refs, to_slot):
        def body(blk, carry):
            for k in range(SUBLANES):
                r = blk * SUBLANES + k
                for e in range(2):
                    pltpu.make_async_copy(ys_ref.at[pl.ds(p_refs[e][0, 0, r], 1)],
                                          ybuf.at[to_slot, e, blk, pl.ds(k, 1)], sems.at[to_slot]).start(priority=e)
            return carry

        lax.fori_loop(0, tm // SUBLANES, body, 0)

    @pl.when(i == 0)
    def _first():
        issue((pos0_ref, pos1_ref), 0)

    @pl.when(i + 1 < n_steps)
    def _ahead():
        issue((nxt0_ref, nxt1_ref), 1 - slot)

    for e in range(2):
        pltpu.make_async_copy(ybuf.at[slot, e], ybuf.at[slot, e], sems.at[slot]).wait()
    w = wcol_ref[...]
    y0 = ybuf[slot, 0].reshape(tm, D_MODEL)
    y1 = ybuf[slot, 1].reshape(tm, D_MODEL)
    out_ref[...] = xmid_ref[...] + w[:, 0:1] * y0 + w[:, 1:2] * y1


def _combine(pos, ys, xmid, wcol, tm):
    t_rows = xmid.shape[0]
    n_steps = t_rows // tm
    assert tm % SUBLANES == 0
    pos3 = _step_major(pos, tm)
    row = lambda i: (i, 0)
    cur = pl.BlockSpec((1, 1, tm), lambda i: (i, 0, 0), memory_space=pltpu.SMEM)
    nxt = pl.BlockSpec((1, 1, tm), lambda i: (jnp.minimum(i + 1, n_steps - 1), 0, 0), memory_space=pltpu.SMEM)
    return pl.pallas_call(
        functools.partial(_combine_kernel, tm=tm, n_steps=n_steps),
        grid=(n_steps,),
        in_specs=[cur, cur, nxt, nxt,
                  pl.BlockSpec(memory_space=pl.ANY),
                  pl.BlockSpec((tm, D_MODEL), row),
                  pl.BlockSpec((tm, LANES), row)],
        out_specs=pl.BlockSpec((tm, D_MODEL), row),
        out_shape=jax.ShapeDtypeStruct((t_rows, D_MODEL), F32),
        scratch_shapes=[pltpu.VMEM((2, 2, tm // SUBLANES, SUBLANES, D_MODEL), F32), pltpu.SemaphoreType.DMA((2,))],
        compiler_params=_params(1),
        name="combine",
    )(*pos3, *pos3, ys, xmid, wcol)


def _bucket(ends, idx):
    n = jnp.sum((ends[None, :] <= idx[:, None]).astype(jnp.int32), axis=1)
    return jnp.minimum(n, ends.shape[0] - 1)


def _tile_for(rows, pref):
    tm = min(pref, rows)
    assert rows % tm == 0
    return tm


def kernel(x_prompt, x_sample, cache_ret_state, cache_swa_k, cache_swa_v, meta_tokens, norm1_g, w_in, q_norm_g,
           k_norm_g, ret_norm_g, attn_sinks, w_out, norm2_g, w_group, w_expert, w_gate, w_up, w_down):
    assert norm1_g.shape[0] == 1, "single-layer trunk"
    bp, lp, _ = x_prompt.shape
    bs, ls, _ = x_sample.shape
    n_pairs = RET_HEADS // 2

    g1 = norm1_g[0][None, :]
    g2 = norm2_g[0][None, :]
    w_in_bf = w_in[0].astype(BF16)
    w_out_bf = w_out[0].astype(BF16)
    qg2 = jnp.tile(q_norm_g[0], 2)[None, :]
    kg2 = jnp.tile(k_norm_g[0], 2)[None, :]
    rng = ret_norm_g[0].reshape(RET_HEADS, RET_DV)
    sink_tab = jnp.broadcast_to(jnp.repeat(attn_sinks[0], CHUNK).reshape(SWA_KV_HEADS, 1, 4 * CHUNK),
                                (SWA_KV_HEADS, SUBLANES, 4 * CHUNK))
    w_router_t = jnp.zeros((ROUTER_ROWS, D_MODEL), F32)
    w_router_t = w_router_t.at[0:N_GROUPS].set(w_group[0].T)
    w_router_t = w_router_t.at[ROUTER_EXPERT_ROW0:ROUTER_EXPERT_ROW0 + N_EXPERTS].set(w_expert[0].T)
    w_router_t_bf = w_router_t.astype(BF16)

    meta_rows = 2 * CHUNK
    m_pad = jnp.zeros((meta_rows, D_MODEL), F32).at[0:N_META].set(meta_tokens)
    m_rqk, m_rv, _, _, m_skv = _proj(m_pad, jnp.arange(meta_rows, dtype=jnp.int32), meta_rows, g1, w_in_bf, qg2, kg2)
    s_meta = _meta_state(m_rqk, m_rv)[None]
    meta_kv = m_skv[0:N_META]

    groups = [
        dict(x=x_prompt.reshape(bp * lp, D_MODEL), n=bp, seq=lp, pos0=N_META, has_hist=False, s0=s_meta,
             hist=jnp.zeros((1, WINDOW, 2 * SWA_KV), F32)),
        dict(x=x_sample.reshape(bs * ls, D_MODEL), n=bs, seq=ls, pos0=N_META + PAST_LEN, has_hist=True,
             s0=cache_ret_state[0].reshape(bs, n_pairs, 2 * RET_DK, RET_DV),
             hist=jnp.concatenate([cache_swa_k[0].reshape(bs, WINDOW, SWA_KV),
                                   cache_swa_v[0].reshape(bs, WINDOW, SWA_KV)], axis=-1)),
    ]

    base_cnt = jnp.zeros((N_EXPERTS, LANES), F32)
    for g in groups:
        rows = g["n"] * g["seq"]
        tm = _tile_for(rows, PROJ_TILE)
        pos = g["pos0"] + jnp.arange(g["seq"], dtype=jnp.int32)
        if g["seq"] < tm:
            assert tm % g["seq"] == 0
            pos = jnp.tile(pos, tm // g["seq"])
        else:
            assert g["seq"] % tm == 0
        rqk, rv, gate, sq, skv = _proj(g["x"], pos, tm, g1, w_in_bf, qg2, kg2)
        tl = min(ATTN_TILE, g["seq"])
        omix, s_out, kv_out = _attention(rqk, rv, gate, sq, skv, meta_kv, g["hist"], g["s0"], rng, sink_tab,
                                         n_streams=g["n"], seq=g["seq"], tl=tl, has_hist=g["has_hist"])
        xmid, wcol, route_t, base_cnt = _post(omix, g["x"], w_out_bf, g2, w_router_t_bf, base_cnt,
                                              _tile_for(rows, POST_TILE))
        g.update(xmid=xmid, wcol=wcol, route_t=route_t, s_out=s_out, kv_out=kv_out)

    te = EXPERT_TILE
    total_rows = sum(g["n"] * g["seq"] for g in groups)
    assert (2 * total_rows) % te == 0
    n_row_tiles = (2 * total_rows) // te
    counts = base_cnt[:, 0].astype(jnp.int32)
    off = jnp.cumsum(counts) - counts
    first_tile = off // te
    n_vis_e = jnp.where(counts > 0, (off + counts - 1) // te - first_tile + 1, 0)
    vis_end = jnp.cumsum(n_vis_e)
    n_vis = vis_end[-1:].astype(jnp.int32)
    v = jnp.arange(n_row_tiles + N_EXPERTS, dtype=jnp.int32)
    vis_expert = _bucket(vis_end, v)
    pick = lambda table: jnp.sum(jnp.where(vis_expert[:, None] == jnp.arange(N_EXPERTS, dtype=jnp.int32)[None, :],
                                           table[None, :], 0), axis=1)
    vis_tile = jnp.clip(pick(first_tile) + v - pick(vis_end - n_vis_e), 0, n_row_tiles - 1).astype(jnp.int32)
    vis_lo = jnp.clip(pick(off) - vis_tile * te, 0, te).astype(jnp.int32)
    vis_hi = jnp.clip(pick(off + counts) - vis_tile * te, 0, te).astype(jnp.int32)

    for g in groups:
        eid = g["route_t"][0:2].astype(jnp.int32)
        off_sel = jnp.sum(jnp.where(eid[None] == jnp.arange(N_EXPERTS, dtype=jnp.int32)[:, None, None],
                                    off[:, None, None], 0), axis=0)
        g["pos"] = (off_sel + g["route_t"][4:6].astype(jnp.int32)).astype(jnp.int32)
    xs = _dispatch([g["pos"] for g in groups], [g["xmid"] for g in groups], g2,
                   _tile_for(min(g["n"] * g["seq"] for g in groups), MOVE_TILE))

    ys = _experts(vis_tile, vis_expert.astype(jnp.int32), vis_lo, vis_hi, n_vis, xs, w_gate[0], w_up[0], w_down[0], te)

    outs = []
    for g in groups:
        rows = g["n"] * g["seq"]
        y = _combine(g["pos"], ys, g["xmid"], g["wcol"], _tile_for(rows, MOVE_TILE))
        outs.append(y.reshape(g["n"], g["seq"], D_MODEL))

    def caches(g):
        kv = g["kv_out"]
        k = kv[:, :, 0:SWA_KV].reshape(g["n"], WINDOW, SWA_KV_HEADS, SWA_HD)[None]
        v = kv[:, :, SWA_KV:2 * SWA_KV].reshape(g["n"], WINDOW, SWA_KV_HEADS, SWA_HD)[None]
        s = g["s_out"].reshape(g["n"], RET_HEADS, RET_DK, RET_DV)[None]
        return s, k, v

    sp, kp, vp = caches(groups[0])
    ss, ks, vs = caches(groups[1])
    return (outs[0], outs[1], sp, kp, vp, ss, ks, vs)
```

```python
import functools

import numpy as np
import jax
import jax.numpy as jnp
from jax import lax
from jax.experimental import pallas as pl
from jax.experimental.pallas import tpu as pltpu

F32 = jnp.float32
BF16 = jnp.bfloat16

D_MODEL = 1024
PAST_LEN = 4096
CHUNK = 64
N_META = 16
RET_HEADS = 4
RET_DK = 64
RET_DV = 128
SWA_HEADS = 8
SWA_KV_HEADS = 2
SWA_HD = 64
WINDOW = 128
ROPE_THETA = 10000.0
N_GROUPS = 4
EXPERTS_PER_GROUP = 8
N_EXPERTS = N_GROUPS * EXPERTS_PER_GROUP
EXPERT_FF = 256
EPS = 1e-6
NEG_INF = -1e30
LOG2E = float(np.log2(np.e))
RET_Q = RET_HEADS * RET_DK
RET_V = RET_HEADS * RET_DV
SWA_Q = SWA_HEADS * SWA_HD
SWA_KV = SWA_KV_HEADS * SWA_HD
MIX_WIDTH = RET_V + SWA_Q
IN_WIDTH = 2 * RET_Q + 2 * RET_V + SWA_Q + 2 * SWA_KV

LANES = 128
PROJ_TILE = 1024
POST_TILE = 1024
ATTN_TILE = 256
EXPERT_TILE = 1024
MOVE_TILE = 512
SUBLANES = 8
ROUTER_EXPERT_ROW0 = 8
ROUTER_ROWS = 64
META_ROWS = 64
VMEM_LIMIT = 56 * 1024 * 1024

_LOG_G = [float(np.log1p(-np.exp2(-5.0 - h))) for h in range(RET_HEADS)]


def _params(n_axes):
    return pltpu.CompilerParams(dimension_semantics=("arbitrary",) * n_axes, vmem_limit_bytes=VMEM_LIMIT)


def _split_bf16(a):
    hi = a.astype(BF16)
    return hi, (a - hi.astype(F32)).astype(BF16)


def _split_dot(a, w2):
    hi, lo = _split_bf16(a)
    return jnp.dot(jnp.concatenate([hi, lo], axis=1), w2, preferred_element_type=F32)


def _lane_sum(a):
    return _split_dot(a, jnp.ones((2 * LANES, LANES), BF16))


def _head_sum_matrix():
    i = lax.broadcasted_iota(jnp.int32, (2 * LANES, LANES), 0) % LANES
    j = lax.broadcasted_iota(jnp.int32, (2 * LANES, LANES), 1)
    return jnp.where((i < SWA_HD) == (j < SWA_HD), 1.0, 0.0).astype(BF16)


def _rope(t, c, s1, s2):
    half = SWA_HD // 2
    return t * c + pltpu.roll(t, LANES - half, 1) * s1 + pltpu.roll(t, half, 1) * s2


def _head_rms(t, g, head_w):
    ms = _split_dot(t * t, head_w) * (1.0 / SWA_HD)
    return t * lax.rsqrt(ms + EPS) * g


def _row_rms(x):
    n_tiles = x.shape[1] // LANES
    ss = x[:, 0:LANES] * x[:, 0:LANES]
    for j in range(1, n_tiles):
        ss = ss + x[:, j * LANES:(j + 1) * LANES] * x[:, j * LANES:(j + 1) * LANES]
    r = lax.rsqrt(_lane_sum(ss) * (1.0 / x.shape[1]) + EPS)
    return x * jnp.concatenate([r] * n_tiles, axis=1)


def _proj_kernel(x_ref, g1_ref, w_ref, qg_ref, kg_ref, cos_ref, s1_ref, s2_ref,
                 rqk_ref, rv_ref, gate_ref, sq_ref, skv_ref):
    xn = (_row_rms(x_ref[...]) * g1_ref[...]).astype(BF16)
    c, s1, s2 = cos_ref[...], s1_ref[...], s2_ref[...]
    head_w = _head_sum_matrix()

    def seg(a, b):
        return jnp.dot(xn, w_ref[:, a:b], preferred_element_type=F32)

    def tile(h, j):
        return h[:, j * LANES:(j + 1) * LANES]

    h = seg(0, 2 * RET_Q)
    for j in range(2):
        rqk_ref[:, j * LANES:(j + 1) * LANES] = _rope(tile(h, j), c, s1, s2).astype(BF16)
    for j in range(2, 4):
        rqk_ref[:, j * LANES:(j + 1) * LANES] = (_rope(tile(h, j), c, s1, s2) * (RET_DK ** -0.5)).astype(BF16)
    a = 2 * RET_Q
    rv_ref[...] = seg(a, a + RET_V).astype(BF16)
    a += RET_V
    g = seg(a, a + RET_V)
    gate_ref[...] = (g * jax.nn.sigmoid(g)).astype(BF16)
    a += RET_V
    h = seg(a, a + SWA_Q)
    qg = qg_ref[...]
    for j in range(SWA_Q // LANES):
        sq_ref[:, j * LANES:(j + 1) * LANES] = _rope(_head_rms(tile(h, j), qg, head_w), c, s1, s2).astype(BF16)
    a += SWA_Q
    h = seg(a, a + 2 * SWA_KV)
    skv_ref[:, 0:LANES] = _rope(_head_rms(tile(h, 0), kg_ref[...], head_w), c, s1, s2)
    skv_ref[:, LANES:2 * LANES] = tile(h, 1)


def _rope_tables(pos):
    half = SWA_HD // 2
    inv = ROPE_THETA ** (-jnp.arange(half, dtype=F32) / half)
    ang = pos.astype(F32)[:, None] * inv[None, :]
    cos, sin = jnp.cos(ang), jnp.sin(ang)
    z = jnp.zeros_like(sin)
    heads_per_tile = LANES // SWA_HD
    return (jnp.tile(cos, (1, 2 * heads_per_tile)),
            jnp.tile(jnp.concatenate([-sin, z], axis=1), (1, heads_per_tile)),
            jnp.tile(jnp.concatenate([z, sin], axis=1), (1, heads_per_tile)))


def _proj(x2d, pos_rows, tm, g1, w_in_bf, qg2, kg2):
    t_rows = x2d.shape[0]
    n_tiles = t_rows // tm
    n_pos_tiles = pos_rows.shape[0] // tm
    cos, s1, s2 = _rope_tables(pos_rows)
    row = lambda i: (i, 0)
    const = lambda i: (0, 0)
    tab = lambda i: (i % n_pos_tiles, 0)
    return pl.pallas_call(
        _proj_kernel,
        grid=(n_tiles,),
        in_specs=[pl.BlockSpec((tm, D_MODEL), row),
                  pl.BlockSpec((1, D_MODEL), const),
                  pl.BlockSpec((D_MODEL, IN_WIDTH), const),
                  pl.BlockSpec((1, LANES), const),
                  pl.BlockSpec((1, LANES), const),
                  pl.BlockSpec((tm, LANES), tab),
                  pl.BlockSpec((tm, LANES), tab),
                  pl.BlockSpec((tm, LANES), tab)],
        out_specs=[pl.BlockSpec((tm, 2 * RET_Q), row),
                   pl.BlockSpec((tm, RET_V), row),
                   pl.BlockSpec((tm, RET_V), row),
                   pl.BlockSpec((tm, SWA_Q), row),
                   pl.BlockSpec((tm, 2 * SWA_KV), row)],
        out_shape=[jax.ShapeDtypeStruct((t_rows, 2 * RET_Q), BF16),
                   jax.ShapeDtypeStruct((t_rows, RET_V), BF16),
                   jax.ShapeDtypeStruct((t_rows, RET_V), BF16),
                   jax.ShapeDtypeStruct((t_rows, SWA_Q), BF16),
                   jax.ShapeDtypeStruct((t_rows, 2 * SWA_KV), F32)],
        compiler_params=_params(1),
        name="proj",
    )(x2d, g1, w_in_bf, qg2, kg2, cos, s1, s2)


def _pair_update(k_bf, v0_bf, v1_bf, wt):
    kw = (k_bf.astype(F32) * wt).astype(BF16)
    dn = (((0,), (0,)), ((), ()))
    a0 = lax.dot_general(kw, v0_bf, dn, preferred_element_type=F32)
    a1 = lax.dot_general(kw, v1_bf, dn, preferred_element_type=F32)
    top = lax.broadcasted_iota(jnp.int32, a0.shape, 0) < RET_DK
    return jnp.where(top, a0, a1)


def _decay_rows(n, pair, rows_back_from):
    i = lax.broadcasted_iota(jnp.int32, (n, LANES), 0).astype(F32)
    lane = lax.broadcasted_iota(jnp.int32, (n, LANES), 1)
    lg = jnp.where(lane < RET_DK, _LOG_G[2 * pair], _LOG_G[2 * pair + 1])
    return jnp.exp((rows_back_from - i) * lg)


def _meta_state_kernel(rqk_ref, rv_ref, s_ref, *, n_rows):
    for p in range(RET_HEADS // 2):
        k = rqk_ref[:, RET_Q + p * LANES:RET_Q + (p + 1) * LANES]
        wt = _decay_rows(n_rows, p, float(N_META - 1))
        s_ref[p] = _pair_update(k, rv_ref[:, (2 * p) * LANES:(2 * p + 1) * LANES],
                                rv_ref[:, (2 * p + 1) * LANES:(2 * p + 2) * LANES], wt)


def _meta_state(m_rqk, m_rv):
    n_rows = m_rqk.shape[0]
    return pl.pallas_call(
        functools.partial(_meta_state_kernel, n_rows=n_rows),
        out_shape=jax.ShapeDtypeStruct((RET_HEADS // 2, 2 * RET_DK, RET_DV), F32),
        name="meta_state",
    )(m_rqk, m_rv)


def _dup_halves(a, lo_mask):
    sw = pltpu.roll(a, SWA_HD, 1)
    return jnp.where(lo_mask, a, sw), jnp.where(lo_mask, sw, a)


def _attn_kernel(rqk_ref, rv_ref, gate_ref, sq_ref, skv_ref, meta_ref, hist_ref, s0_ref, rng_ref, sink_ref,
                 omix_ref, sout_ref, kvout_ref,
                 s_scr, kd_scr, vd_scr, mk_scr, mv_scr, dec_scr, wt_scr, cs_scr, gam_scr,
                 *, tl, has_hist):
    b = pl.program_id(0)
    t = pl.program_id(1)
    nt = pl.num_programs(1)
    n_chunks = tl // CHUNK
    n_pairs = RET_HEADS // 2
    lo_tl = lax.broadcasted_iota(jnp.int32, (tl, LANES), 1) < SWA_HD
    lo_c = lax.broadcasted_iota(jnp.int32, (CHUNK, LANES), 1) < SWA_HD

    @pl.when((b == 0) & (t == 0))
    def _tables():
        i = lax.broadcasted_iota(jnp.int32, (tl, tl), 0)
        j = lax.broadcasted_iota(jnp.int32, (tl, tl), 1)
        diff = (i - j).astype(F32)
        row = lax.broadcasted_iota(jnp.int32, (tl, LANES), 0).astype(F32)
        for h in range(RET_HEADS):
            dec_scr[h] = jnp.where(diff >= 0.0, jnp.exp(jnp.maximum(diff, 0.0) * _LOG_G[h]), 0.0)
            cs_scr[h] = jnp.exp((row + 1.0) * _LOG_G[h])
        top = lax.broadcasted_iota(jnp.int32, (2 * RET_DK, RET_DV), 0) < RET_DK
        for p in range(n_pairs):
            wt_scr[p] = _decay_rows(tl, p, float(tl - 1))
            gam_scr[p] = jnp.where(top, jnp.exp(jnp.float32(tl * _LOG_G[2 * p])), jnp.exp(jnp.float32(tl * _LOG_G[2 * p + 1])))
        lo_m = lax.broadcasted_iota(jnp.int32, (N_META, LANES), 1) < SWA_HD
        mk0, mk1 = _dup_halves(meta_ref[:, 0:LANES], lo_m)
        mv0, mv1 = _dup_halves(meta_ref[:, LANES:2 * LANES], lo_m)
        mk_scr[...] = jnp.zeros(mk_scr.shape, BF16)
        mv_scr[...] = jnp.zeros(mv_scr.shape, BF16)
        mk_scr[0, 0:N_META] = mk0.astype(BF16)
        mk_scr[1, 0:N_META] = mk1.astype(BF16)
        mv_scr[0, 0:N_META] = mv0.astype(BF16)
        mv_scr[1, 0:N_META] = mv1.astype(BF16)

    @pl.when(t == 0)
    def _stream_start():
        s_scr[...] = s0_ref[0]
        if has_hist:
            lo_w = lax.broadcasted_iota(jnp.int32, (WINDOW, LANES), 1) < SWA_HD
            k0, k1 = _dup_halves(hist_ref[0, :, 0:LANES], lo_w)
            v0, v1 = _dup_halves(hist_ref[0, :, LANES:2 * LANES], lo_w)
            kd_scr[0, 0:WINDOW] = k0.astype(BF16)
            kd_scr[1, 0:WINDOW] = k1.astype(BF16)
            vd_scr[0, 0:WINDOW] = v0.astype(BF16)
            vd_scr[1, 0:WINDOW] = v1.astype(BF16)
        else:
            z = jnp.zeros((WINDOW, LANES), BF16)
            for kv in range(SWA_KV_HEADS):
                kd_scr[kv, 0:WINDOW] = z
                vd_scr[kv, 0:WINDOW] = z

    k0, k1 = _dup_halves(skv_ref[:, 0:LANES], lo_tl)
    v0, v1 = _dup_halves(skv_ref[:, LANES:2 * LANES], lo_tl)
    kd_scr[0, WINDOW:WINDOW + tl] = k0.astype(BF16)
    kd_scr[1, WINDOW:WINDOW + tl] = k1.astype(BF16)
    vd_scr[0, WINDOW:WINDOW + tl] = v0.astype(BF16)
    vd_scr[1, WINDOW:WINDOW + tl] = v1.astype(BF16)

    band = WINDOW + CHUNK
    n_keys = META_ROWS + band
    n_q = 4 * CHUNK
    scale2 = (SWA_HD ** -0.5) * LOG2E
    krow = lax.broadcasted_iota(jnp.int32, (n_keys, n_q), 0)
    zero_c = jnp.zeros((CHUNK, LANES), BF16)
    ones_v = jnp.ones((n_keys, LANES), BF16)
    for c in range(n_chunks):
        if has_hist:
            first_valid = META_ROWS
        else:
            first_valid = jnp.where(t == 0, max(META_ROWS + WINDOW - c * CHUNK, META_ROWS), META_ROWS)
        valid_t = (krow < N_META) | (krow >= first_valid)
        r0 = c * CHUNK
        for kv in range(SWA_KV_HEADS):
            keys = jnp.concatenate([mk_scr[kv], kd_scr[kv, r0:r0 + band]], axis=0)
            vals = jnp.concatenate([mv_scr[kv], vd_scr[kv, r0:r0 + band]], axis=0)
            qa = sq_ref[r0:r0 + CHUNK, (2 * kv) * LANES:(2 * kv + 1) * LANES]
            qb = sq_ref[r0:r0 + CHUNK, (2 * kv + 1) * LANES:(2 * kv + 2) * LANES]
            lhs = jnp.concatenate([jnp.where(lo_c, qa, zero_c), jnp.where(lo_c, zero_c, qa),
                                   jnp.where(lo_c, qb, zero_c), jnp.where(lo_c, zero_c, qb)], axis=0)
            s_t = lax.dot_general(keys, lhs, (((1,), (1,)), ((), ())), preferred_element_type=F32) * scale2
            s_t = jnp.where(valid_t, s_t, NEG_INF)
            s_t = jnp.where(krow == N_META, sink_ref[kv, 0:1, :] * LOG2E, s_t)
            e_t = jnp.exp2(s_t - jnp.max(s_t, axis=0, keepdims=True)).astype(BF16)
            ov = lax.dot_general(e_t, jnp.concatenate([vals, ones_v], axis=1), (((0,), (0,)), ((), ())),
                                 preferred_element_type=F32)
            o = ov[:, 0:LANES] * (1.0 / ov[:, LANES:2 * LANES])
            oa = jnp.where(lo_c, o[0:CHUNK], o[CHUNK:2 * CHUNK])
            ob = jnp.where(lo_c, o[2 * CHUNK:3 * CHUNK], o[3 * CHUNK:4 * CHUNK])
            base = RET_V + (2 * kv) * LANES
            omix_ref[r0:r0 + CHUNK, base:base + LANES] = oa.astype(BF16)
            omix_ref[r0:r0 + CHUNK, base + LANES:base + 2 * LANES] = ob.astype(BF16)

    zero_t = jnp.zeros((tl, LANES), BF16)
    for p in range(n_pairs):
        q = rqk_ref[:, p * LANES:(p + 1) * LANES]
        k = rqk_ref[:, RET_Q + p * LANES:RET_Q + (p + 1) * LANES]
        lhs = jnp.concatenate([jnp.where(lo_tl, q, zero_t), jnp.where(lo_tl, zero_t, q)], axis=0)
        s = lax.dot_general(lhs, k, (((1,), (1,)), ((), ())), preferred_element_type=F32)
        cross = jnp.dot(lhs, s_scr[p].astype(BF16), preferred_element_type=F32)
        for i in range(2):
            h = 2 * p + i
            v = rv_ref[:, h * LANES:(h + 1) * LANES]
            a = (s[i * tl:(i + 1) * tl] * dec_scr[h]).astype(BF16)
            o = jnp.dot(a, v, preferred_element_type=F32) + cross[i * tl:(i + 1) * tl] * cs_scr[h]
            r = o * lax.rsqrt(_lane_sum(o * o) * (1.0 / RET_DV) + EPS) * rng_ref[h:h + 1, :]
            omix_ref[:, h * LANES:(h + 1) * LANES] = (r * gate_ref[:, h * LANES:(h + 1) * LANES].astype(F32)).astype(BF16)
        u = _pair_update(k, rv_ref[:, (2 * p) * LANES:(2 * p + 1) * LANES],
                         rv_ref[:, (2 * p + 1) * LANES:(2 * p + 2) * LANES], wt_scr[p])
        s_scr[p] = gam_scr[p] * s_scr[p] + u

    if tl >= WINDOW:
        @pl.when(t + 1 < nt)
        def _carry_window():
            for kv in range(SWA_KV_HEADS):
                kd_scr[kv, 0:WINDOW] = kd_scr[kv, tl:tl + WINDOW]
                vd_scr[kv, 0:WINDOW] = vd_scr[kv, tl:tl + WINDOW]

    @pl.when(t + 1 == nt)
    def _stream_end():
        sout_ref[0] = s_scr[...]
        if tl >= WINDOW:
            kvout_ref[0] = skv_ref[tl - WINDOW:tl, :]
        else:
            kvout_ref[0, 0:WINDOW - tl] = hist_ref[0, tl:WINDOW, :]
            kvout_ref[0, WINDOW - tl:WINDOW] = skv_ref[...]


def _attention(rqk, rv, gate, sq, skv, meta_kv, hist_kv, s0, rng, sink_tab, *, n_streams, seq, tl, has_hist):
    nt = seq // tl
    assert tl % CHUNK == 0 and seq % tl == 0
    assert tl >= WINDOW or (nt == 1 and has_hist)
    n_pairs = RET_HEADS // 2
    s0_shared = s0.shape[0] == 1
    row = lambda b, t: (b * nt + t, 0)
    const2 = lambda b, t: (0, 0)
    const3 = lambda b, t: (0, 0, 0)
    per_b3 = lambda b, t: (b, 0, 0)
    s0_map = (lambda b, t: (0, 0, 0, 0)) if s0_shared else (lambda b, t: (b, 0, 0, 0))
    hist_map = per_b3 if has_hist else const3
    rows = n_streams * seq
    return pl.pallas_call(
        functools.partial(_attn_kernel, tl=tl, has_hist=has_hist),
        grid=(n_streams, nt),
        in_specs=[pl.BlockSpec((tl, 2 * RET_Q), row),
                  pl.BlockSpec((tl, RET_V), row),
                  pl.BlockSpec((tl, RET_V), row),
                  pl.BlockSpec((tl, SWA_Q), row),
                  pl.BlockSpec((tl, 2 * SWA_KV), row),
                  pl.BlockSpec((N_META, 2 * SWA_KV), const2),
                  pl.BlockSpec((1, WINDOW, 2 * SWA_KV), hist_map),
                  pl.BlockSpec((1, n_pairs, 2 * RET_DK, RET_DV), s0_map),
                  pl.BlockSpec((RET_HEADS, RET_DV), const2),
                  pl.BlockSpec((SWA_KV_HEADS, SUBLANES, 4 * CHUNK), const3)],
        out_specs=[pl.BlockSpec((tl, MIX_WIDTH), row),
                   pl.BlockSpec((1, n_pairs, 2 * RET_DK, RET_DV), lambda b, t: (b, 0, 0, 0)),
                   pl.BlockSpec((1, WINDOW, 2 * SWA_KV), per_b3)],
        out_shape=[jax.ShapeDtypeStruct((rows, MIX_WIDTH), BF16),
                   jax.ShapeDtypeStruct((n_streams, n_pairs, 2 * RET_DK, RET_DV), F32),
                   jax.ShapeDtypeStruct((n_streams, WINDOW, 2 * SWA_KV), F32)],
        scratch_shapes=[pltpu.VMEM((n_pairs, 2 * RET_DK, RET_DV), F32),
                        pltpu.VMEM((SWA_KV_HEADS, WINDOW + tl, LANES), BF16),
                        pltpu.VMEM((SWA_KV_HEADS, WINDOW + tl, LANES), BF16),
                        pltpu.VMEM((SWA_KV_HEADS, META_ROWS, LANES), BF16),
                        pltpu.VMEM((SWA_KV_HEADS, META_ROWS, LANES), BF16),
                        pltpu.VMEM((RET_HEADS, tl, tl), F32),
                        pltpu.VMEM((n_pairs, tl, LANES), F32),
                        pltpu.VMEM((RET_HEADS, tl, RET_DV), F32),
                        pltpu.VMEM((n_pairs, 2 * RET_DK, RET_DV), F32)],
        compiler_params=_params(2),
        name="attention",
    )(rqk, rv, gate, sq, skv, meta_kv, hist_kv, s0, rng, sink_tab)


def _post_kernel(omix_ref, x_ref, wout_ref, g2_ref, wrt_ref, base_ref,
                 xmid_ref, wcol_ref, rt_ref, cnt_ref, tri_scr, run_scr):
    i = pl.program_id(0)
    tm = x_ref.shape[0]

    @pl.when(i == 0)
    def _init():
        r = lax.broadcasted_iota(jnp.int32, (tm, tm), 0)
        c = lax.broadcasted_iota(jnp.int32, (tm, tm), 1)
        tri_scr[...] = jnp.where(r < c, 1.0, 0.0).astype(BF16)
        run_scr[...] = base_ref[...]

    xm = x_ref[...] + jnp.dot(omix_ref[...], wout_ref[...], preferred_element_type=F32)
    xmid_ref[...] = xm
    hn = _row_rms(xm) * g2_ref[...]
    lt = lax.dot_general(wrt_ref[...], hn.astype(BF16), (((1,), (1,)), ((), ())), preferred_element_type=F32)
    row8 = lax.broadcasted_iota(jnp.int32, (SUBLANES, tm), 0)
    big = jnp.int32(SUBLANES)
    gl = jnp.where(row8 < N_GROUPS, lt[0:SUBLANES], NEG_INF)
    gmax = jnp.max(gl, axis=0, keepdims=True)
    gsum = jnp.sum(jnp.exp(gl - gmax), axis=0, keepdims=True)
    g_sel = jnp.min(jnp.where(gl == gmax, row8, big), axis=0, keepdims=True)
    p_sel = 1.0 / gsum
    el = lt[ROUTER_EXPERT_ROW0:ROUTER_EXPERT_ROW0 + EXPERTS_PER_GROUP]
    for g in range(1, N_GROUPS):
        lo = ROUTER_EXPERT_ROW0 + g * EXPERTS_PER_GROUP
        el = jnp.where(g_sel == g, lt[lo:lo + EXPERTS_PER_GROUP], el)
    m1 = jnp.max(el, axis=0, keepdims=True)
    i1 = jnp.min(jnp.where(el == m1, row8, big), axis=0, keepdims=True)
    el2 = jnp.where(row8 == i1, NEG_INF, el)
    m2 = jnp.max(el2, axis=0, keepdims=True)
    i2 = jnp.min(jnp.where(el2 == m2, row8, big), axis=0, keepdims=True)
    e2 = jnp.exp(m2 - m1)
    inv = 1.0 / (1.0 + e2)
    w1 = p_sel * inv
    w2 = p_sel * (e2 * inv)
    eid1 = g_sel * EXPERTS_PER_GROUP + i1
    eid2 = g_sel * EXPERTS_PER_GROUP + i2

    rowe = lax.broadcasted_iota(jnp.int32, (N_EXPERTS, tm), 0)
    oh = jnp.where((rowe == eid1) | (rowe == eid2), 1.0, 0.0).astype(BF16)
    run = run_scr[...]
    pref = jnp.dot(oh, tri_scr[...], preferred_element_type=F32) + jnp.concatenate([run] * (tm // LANES), axis=1)
    r1 = jnp.sum(jnp.where(rowe == eid1, pref, 0.0), axis=0, keepdims=True)
    r2 = jnp.sum(jnp.where(rowe == eid2, pref, 0.0), axis=0, keepdims=True)
    run = run + jnp.dot(oh, jnp.ones((tm, LANES), BF16), preferred_element_type=F32)
    run_scr[...] = run
    cnt_ref[...] = run

    out = jnp.zeros((SUBLANES, tm), F32)
    for k, v in enumerate([eid1.astype(F32), eid2.astype(F32), w1, w2, r1, r2]):
        out = jnp.where(row8 == k, v, out)
    rt_ref[...] = out
    rowl = lax.broadcasted_iota(jnp.int32, (LANES, tm), 0)
    wcol_ref[...] = jnp.where(rowl == 0, w1, jnp.where(rowl == 1, w2, 0.0)).T


def _post(omix, x2d, w_out_bf, g2, w_router_t_bf, base_cnt, tm):
    t_rows = x2d.shape[0]
    assert tm % LANES == 0
    row = lambda i: (i, 0)
    const = lambda i: (0, 0)
    return pl.pallas_call(
        _post_kernel,
        grid=(t_rows // tm,),
        in_specs=[pl.BlockSpec((tm, MIX_WIDTH), row),
                  pl.BlockSpec((tm, D_MODEL), row),
                  pl.BlockSpec((MIX_WIDTH, D_MODEL), const),
                  pl.BlockSpec((1, D_MODEL), const),
                  pl.BlockSpec((ROUTER_ROWS, D_MODEL), const),
                  pl.BlockSpec((N_EXPERTS, LANES), const)],
        out_specs=[pl.BlockSpec((tm, D_MODEL), row),
                   pl.BlockSpec((tm, LANES), row),
                   pl.BlockSpec((SUBLANES, tm), lambda i: (0, i)),
                   pl.BlockSpec((N_EXPERTS, LANES), const)],
        out_shape=[jax.ShapeDtypeStruct((t_rows, D_MODEL), F32),
                   jax.ShapeDtypeStruct((t_rows, LANES), F32),
                   jax.ShapeDtypeStruct((SUBLANES, t_rows), F32),
                   jax.ShapeDtypeStruct((N_EXPERTS, LANES), F32)],
        scratch_shapes=[pltpu.VMEM((tm, tm), BF16), pltpu.VMEM((N_EXPERTS, LANES), F32)],
        compiler_params=_params(1),
        name="post",
    )(omix, x2d, w_out_bf, g2, w_router_t_bf, base_cnt)


def _step_major(pos, tm):
    return [pos[e].reshape(-1, 1, tm) for e in range(2)]


def _dispatch_kernel(pos0_ref, pos1_ref, *refs, tm, group_steps):
    pos_refs = (pos0_ref, pos1_ref)
    n_g = len(group_steps)
    xmid_refs = refs[:n_g]
    g2_ref, xs_ref, hbuf, sems = refs[n_g:]
    n_steps = sum(group_steps)
    i = pl.program_id(0)
    slot = i % 2

    first = 0
    for xmid_ref, steps in zip(xmid_refs, group_steps):
        @pl.when((i >= first) & (i < first + steps))
        def _normalise(xmid_ref=xmid_ref):
            hn = _row_rms(xmid_ref[...]) * g2_ref[...]
            hbuf[slot] = hn.reshape(tm // SUBLANES, SUBLANES, D_MODEL)
        first += steps

    def body(blk, carry):
        for k in range(SUBLANES):
            r = blk * SUBLANES + k
            for e in range(2):
                pltpu.make_async_copy(hbuf.at[slot, blk, pl.ds(k, 1)], xs_ref.at[pl.ds(pos_refs[e][0, 0, r], 1)],
                                      sems.at[slot]).start(priority=e)
        return carry

    lax.fori_loop(0, tm // SUBLANES, body, 0)

    def drain(which):
        pltpu.make_async_copy(xs_ref.at[pl.ds(0, 2 * tm)], xs_ref.at[pl.ds(0, 2 * tm)], sems.at[which]).wait()

    @pl.when(i > 0)
    def _previous():
        drain(1 - slot)

    @pl.when(i == n_steps - 1)
    def _last():
        drain(slot)


def _dispatch(pos_list, xmid_list, g2, tm):
    group_steps = tuple(x.shape[0] // tm for x in xmid_list)
    n_steps = sum(group_steps)
    assert tm % SUBLANES == 0
    pos3 = [jnp.concatenate(parts, axis=0) for parts in zip(*[_step_major(p, tm) for p in pos_list])]
    smem = pl.BlockSpec((1, 1, tm), lambda i: (i, 0, 0), memory_space=pltpu.SMEM)
    x_specs = []
    first = 0
    for steps in group_steps:
        x_specs.append(pl.BlockSpec((tm, D_MODEL),
                                    lambda i, first=first, steps=steps: (jnp.clip(i - first, 0, steps - 1), 0)))
        first += steps
    n_rows_out = 2 * sum(x.shape[0] for x in xmid_list)
    return pl.pallas_call(
        functools.partial(_dispatch_kernel, tm=tm, group_steps=group_steps),
        grid=(n_steps,),
        in_specs=[smem, smem] + x_specs + [pl.BlockSpec((1, D_MODEL), lambda i: (0, 0))],
        out_specs=pl.BlockSpec(memory_space=pl.ANY),
        out_shape=jax.ShapeDtypeStruct((n_rows_out, D_MODEL), F32),
        scratch_shapes=[pltpu.VMEM((2, tm // SUBLANES, SUBLANES, D_MODEL), F32), pltpu.SemaphoreType.DMA((2,))],
        compiler_params=_params(1),
        name="dispatch",
    )(*pos3, *xmid_list, g2)


def _expert_kernel(vt_ref, ve_ref, lo_ref, hi_ref, nv_ref, x_ref, wg_ref, wu_ref, wd_ref, y_ref, wgu_scr, wd_scr):
    v = pl.program_id(0)

    @pl.when(v < nv_ref[0])
    def _compute():
        @pl.when((v == 0) | (ve_ref[v] != ve_ref[jnp.maximum(v - 1, 0)]))
        def _new_expert():
            wgu_scr[:, 0:EXPERT_FF] = wg_ref[0].astype(BF16)
            wgu_scr[:, EXPERT_FF:2 * EXPERT_FF] = wu_ref[0].astype(BF16)
            wd_scr[...] = wd_ref[0].astype(BF16)

        x = x_ref[...].astype(BF16)
        gu = jnp.dot(x, wgu_scr[...], preferred_element_type=F32)
        g = gu[:, 0:EXPERT_FF]
        a = (g * jax.nn.sigmoid(g) * gu[:, EXPERT_FF:2 * EXPERT_FF]).astype(BF16)
        y = jnp.dot(a, wd_scr[...], preferred_element_type=F32)
        lo, hi = lo_ref[v], hi_ref[v]

        @pl.when(lo == 0)
        def _first_visit():
            y_ref[...] = y

        @pl.when(lo > 0)
        def _later_visit():
            row = lax.broadcasted_iota(jnp.int32, y.shape, 0)
            y_ref[...] = jnp.where((row >= lo) & (row < hi), y, y_ref[...])


def _experts(vis_tile, vis_expert, vis_lo, vis_hi, n_vis, xs, w_gate, w_up, w_down, tm):
    n_steps = vis_tile.shape[0]
    last = lambda v, nv: jnp.minimum(v, nv[0] - 1)
    of_expert = lambda v, vt, ve, lo, hi, nv: (ve[last(v, nv)], 0, 0)
    return pl.pallas_call(
        _expert_kernel,
        grid_spec=pltpu.PrefetchScalarGridSpec(
            num_scalar_prefetch=5,
            grid=(n_steps,),
            in_specs=[pl.BlockSpec((tm, D_MODEL), lambda v, vt, ve, lo, hi, nv: (vt[last(v, nv)], 0)),
                      pl.BlockSpec((1, D_MODEL, EXPERT_FF), of_expert),
                      pl.BlockSpec((1, D_MODEL, EXPERT_FF), of_expert),
                      pl.BlockSpec((1, EXPERT_FF, D_MODEL), of_expert)],
            out_specs=pl.BlockSpec((tm, D_MODEL), lambda v, vt, ve, lo, hi, nv: (vt[last(v, nv)], 0)),
            scratch_shapes=[pltpu.VMEM((D_MODEL, 2 * EXPERT_FF), BF16), pltpu.VMEM((EXPERT_FF, D_MODEL), BF16)]),
        out_shape=jax.ShapeDtypeStruct(xs.shape, F32),
        compiler_params=_params(1),
        name="experts",
    )(vis_tile, vis_expert, vis_lo, vis_hi, n_vis, xs, w_gate, w_up, w_down)


def _combine_kernel(pos0_ref, pos1_ref, nxt0_ref, nxt1_ref, ys_ref, xmid_ref, wcol_ref, out_ref, ybuf, sems,
                    *, tm, n_steps):
    i = pl.program_id(0)
    slot = i % 2

    def issue(p_refs, to_slot):
        def body(blk, carry):
            for k in range(SUBLANES):
                r = blk * SUBLANES + k
                for e in range(2):
                    pltpu.make_async_copy(ys_ref.at[pl.ds(p_refs[e][0, 0, r], 1)],
                                          ybuf.at[to_slot, e, blk, pl.ds(k, 1)], sems.at[to_slot]).start(priority=e)
            return carry

        lax.fori_loop(0, tm // SUBLANES, body, 0)

    @pl.when(i == 0)
    def _first():
        issue((pos0_ref, pos1_ref), 0)

    @pl.when(i + 1 < n_steps)
    def _ahead():
        issue((nxt0_ref, nxt1_ref), 1 - slot)

    for e in range(2):
        pltpu.make_async_copy(ybuf.at[slot, e], ybuf.at[slot, e], sems.at[slot]).wait()
    w = wcol_ref[...]
    y0 = ybuf[slot, 0].reshape(tm, D_MODEL)
    y1 = ybuf[slot, 1].reshape(tm, D_MODEL)
    out_ref[...] = xmid_ref[...] + w[:, 0:1] * y0 + w[:, 1:2] * y1


def _combine(pos, ys, xmid, wcol, tm):
    t_rows = xmid.shape[0]
    n_steps = t_rows // tm
    assert tm % SUBLANES == 0
    pos3 = _step_major(pos, tm)
    row = lambda i: (i, 0)
    cur = pl.BlockSpec((1, 1, tm), lambda i: (i, 0, 0), memory_space=pltpu.SMEM)
    nxt = pl.BlockSpec((1, 1, tm), lambda i: (jnp.minimum(i + 1, n_steps - 1), 0, 0), memory_space=pltpu.SMEM)
    return pl.pallas_call(
        functools.partial(_combine_kernel, tm=tm, n_steps=n_steps),
        grid=(n_steps,),
        in_specs=[cur, cur, nxt, nxt,
                  pl.BlockSpec(memory_space=pl.ANY),
                  pl.BlockSpec((tm, D_MODEL), row),
                  pl.BlockSpec((tm, LANES), row)],
        out_specs=pl.BlockSpec((tm, D_MODEL), row),
        out_shape=jax.ShapeDtypeStruct((t_rows, D_MODEL), F32),
        scratch_shapes=[pltpu.VMEM((2, 2, tm // SUBLANES, SUBLANES, D_MODEL), F32), pltpu.SemaphoreType.DMA((2,))],
        compiler_params=_params(1),
        name="combine",
    )(*pos3, *pos3, ys, xmid, wcol)


def _bucket(ends, idx):
    n = jnp.sum((ends[None, :] <= idx[:, None]).astype(jnp.int32), axis=1)
    return jnp.minimum(n, ends.shape[0] - 1)


def _tile_for(rows, pref):
    tm = min(pref, rows)
    assert rows % tm == 0
    return tm


def kernel(x_prompt, x_sample, cache_ret_state, cache_swa_k, cache_swa_v, meta_tokens, norm1_g, w_in, q_norm_g,
           k_norm_g, ret_norm_g, attn_sinks, w_out, norm2_g, w_group, w_expert, w_gate, w_up, w_down):
    assert norm1_g.shape[0] == 1, "single-layer trunk"
    bp, lp, _ = x_prompt.shape
    bs, ls, _ = x_sample.shape
    n_pairs = RET_HEADS // 2

    g1 = norm1_g[0][None, :]
    g2 = norm2_g[0][None, :]
    w_in_bf = w_in[0].astype(BF16)
    w_out_bf = w_out[0].astype(BF16)
    qg2 = jnp.tile(q_norm_g[0], 2)[None, :]
    kg2 = jnp.tile(k_norm_g[0], 2)[None, :]
    rng = ret_norm_g[0].reshape(RET_HEADS, RET_DV)
    sink_tab = jnp.broadcast_to(jnp.repeat(attn_sinks[0], CHUNK).reshape(SWA_KV_HEADS, 1, 4 * CHUNK),
                                (SWA_KV_HEADS, SUBLANES, 4 * CHUNK))
    w_router_t = jnp.zeros((ROUTER_ROWS, D_MODEL), F32)
    w_router_t = w_router_t.at[0:N_GROUPS].set(w_group[0].T)
    w_router_t = w_router_t.at[ROUTER_EXPERT_ROW0:ROUTER_EXPERT_ROW0 + N_EXPERTS].set(w_expert[0].T)
    w_router_t_bf = w_router_t.astype(BF16)

    meta_rows = 2 * CHUNK
    m_pad = jnp.zeros((meta_rows, D_MODEL), F32).at[0:N_META].set(meta_tokens)
    m_rqk, m_rv, _, _, m_skv = _proj(m_pad, jnp.arange(meta_rows, dtype=jnp.int32), meta_rows, g1, w_in_bf, qg2, kg2)
    s_meta = _meta_state(m_rqk, m_rv)[None]
    meta_kv = m_skv[0:N_META]

    groups = [
        dict(x=x_prompt.reshape(bp * lp, D_MODEL), n=bp, seq=lp, pos0=N_META, has_hist=False, s0=s_meta,
             hist=jnp.zeros((1, WINDOW, 2 * SWA_KV), F32)),
        dict(x=x_sample.reshape(bs * ls, D_MODEL), n=bs, seq=ls, pos0=N_META + PAST_LEN, has_hist=True,
             s0=cache_ret_state[0].reshape(bs, n_pairs, 2 * RET_DK, RET_DV),
             hist=jnp.concatenate([cache_swa_k[0].reshape(bs, WINDOW, SWA_KV),
                                   cache_swa_v[0].reshape(bs, WINDOW, SWA_KV)], axis=-1)),
    ]

    base_cnt = jnp.zeros((N_EXPERTS, LANES), F32)
    for g in groups:
        rows = g["n"] * g["seq"]
        tm = _tile_for(rows, PROJ_TILE)
        pos = g["pos0"] + jnp.arange(g["seq"], dtype=jnp.int32)
        if g["seq"] < tm:
            assert tm % g["seq"] == 0
            pos = jnp.tile(pos, tm // g["seq"])
        else:
            assert g["seq"] % tm == 0
        rqk, rv, gate, sq, skv = _proj(g["x"], pos, tm, g1, w_in_bf, qg2, kg2)
        tl = min(ATTN_TILE, g["seq"])
        omix, s_out, kv_out = _attention(rqk, rv, gate, sq, skv, meta_kv, g["hist"], g["s0"], rng, sink_tab,
                                         n_streams=g["n"], seq=g["seq"], tl=tl, has_hist=g["has_hist"])
        xmid, wcol, route_t, base_cnt = _post(omix, g["x"], w_out_bf, g2, w_router_t_bf, base_cnt,
                                              _tile_for(rows, POST_TILE))
        g.update(xmid=xmid, wcol=wcol, route_t=route_t, s_out=s_out, kv_out=kv_out)

    te = EXPERT_TILE
    total_rows = sum(g["n"] * g["seq"] for g in groups)
    assert (2 * total_rows) % te == 0
    n_row_tiles = (2 * total_rows) // te
    counts = base_cnt[:, 0].astype(jnp.int32)
    off = jnp.cumsum(counts) - counts
    first_tile = off // te
    n_vis_e = jnp.where(counts > 0, (off + counts - 1) // te - first_tile + 1, 0)
    vis_end = jnp.cumsum(n_vis_e)
    n_vis = vis_end[-1:].astype(jnp.int32)
    v = jnp.arange(n_row_tiles + N_EXPERTS, dtype=jnp.int32)
    vis_expert = _bucket(vis_end, v)
    pick = lambda table: jnp.sum(jnp.where(vis_expert[:, None] == jnp.arange(N_EXPERTS, dtype=jnp.int32)[None, :],
                                           table[None, :], 0), axis=1)
    vis_tile = jnp.clip(pick(first_tile) + v - pick(vis_end - n_vis_e), 0, n_row_tiles - 1).astype(jnp.int32)
    vis_lo = jnp.clip(pick(off) - vis_tile * te, 0, te).astype(jnp.int32)
    vis_hi = jnp.clip(pick(off + counts) - vis_tile * te, 0, te).astype(jnp.int32)

    for g in groups:
        eid = g["route_t"][0:2].astype(jnp.int32)
        off_sel = jnp.sum(jnp.where(eid[None] == jnp.arange(N_EXPERTS, dtype=jnp.int32)[:, None, None],
                                    off[:, None, None], 0), axis=0)
        g["pos"] = (off_sel + g["route_t"][4:6].astype(jnp.int32)).astype(jnp.int32)
    xs = _dispatch([g["pos"] for g in groups], [g["xmid"] for g in groups], g2,
                   _tile_for(min(g["n"] * g["seq"] for g in groups), MOVE_TILE))

    ys = _experts(vis_tile, vis_expert.astype(jnp.int32), vis_lo, vis_hi, n_vis, xs, w_gate[0], w_up[0], w_down[0], te)

    outs = []
    for g in groups:
        rows = g["n"] * g["seq"]
        y = _combine(g["pos"], ys, g["xmid"], g["wcol"], _tile_for(rows, MOVE_TILE))
        outs.append(y.reshape(g["n"], g["seq"], D_MODEL))

    def caches(g):
        kv = g["kv_out"]
        k = kv[:, :, 0:SWA_KV].reshape(g["n"], WINDOW, SWA_KV_HEADS, SWA_HD)[None]
        v = kv[:, :, SWA_KV:2 * SWA_KV].reshape(g["n"], WINDOW, SWA_KV_HEADS, SWA_HD)[None]
        s = g["s_out"].reshape(g["n"], RET_HEADS, RET_DK, RET_DV)[None]
        return s, k, v

    sp, kp, vp = caches(groups[0])
    ss, ks, vs = caches(groups[1])
    return (outs[0], outs[1], sp, kp, vp, ss, ks, vs)
```

```python
import functools

import numpy as np
import jax
import jax.numpy as jnp
from jax import lax
from jax.experimental import pallas as pl
from jax.experimental.pallas import tpu as pltpu

F32 = jnp.float32
BF16 = jnp.bfloat16

D_MODEL = 1024
PAST_LEN = 4096
CHUNK = 64
N_META = 16
RET_HEADS = 4
RET_DK = 64
RET_DV = 128
SWA_HEADS = 8
SWA_KV_HEADS = 2
SWA_HD = 64
WINDOW = 128
ROPE_THETA = 10000.0
N_GROUPS = 4
EXPERTS_PER_GROUP = 8
N_EXPERTS = N_GROUPS * EXPERTS_PER_GROUP
EXPERT_FF = 256
EPS = 1e-6
NEG_INF = -1e30
LOG2E = float(np.log2(np.e))
RET_Q = RET_HEADS * RET_DK
RET_V = RET_HEADS * RET_DV
SWA_Q = SWA_HEADS * SWA_HD
SWA_KV = SWA_KV_HEADS * SWA_HD
MIX_WIDTH = RET_V + SWA_Q
IN_WIDTH = 2 * RET_Q + 2 * RET_V + SWA_Q + 2 * SWA_KV

LANES = 128
PROJ_TILE = 1024
POST_TILE = 1024
ATTN_TILE = 256
EXPERT_TILE = 1024
MOVE_TILE = 1024
SUBLANES = 8
ROUTER_EXPERT_ROW0 = 8
ROUTER_ROWS = 64
META_ROWS = 64
VMEM_LIMIT = 56 * 1024 * 1024

_LOG_G = [float(np.log1p(-np.exp2(-5.0 - h))) for h in range(RET_HEADS)]


def _params(n_axes):
    return pltpu.CompilerParams(dimension_semantics=("arbitrary",) * n_axes, vmem_limit_bytes=VMEM_LIMIT)


def _split_bf16(a):
    hi = a.astype(BF16)
    return hi, (a - hi.astype(F32)).astype(BF16)


def _split_dot(a, w2):
    hi, lo = _split_bf16(a)
    return jnp.dot(jnp.concatenate([hi, lo], axis=1), w2, preferred_element_type=F32)


def _lane_sum(a):
    return _split_dot(a, jnp.ones((2 * LANES, LANES), BF16))


def _head_sum_matrix():
    i = lax.broadcasted_iota(jnp.int32, (2 * LANES, LANES), 0) % LANES
    j = lax.broadcasted_iota(jnp.int32, (2 * LANES, LANES), 1)
    return jnp.where((i < SWA_HD) == (j < SWA_HD), 1.0, 0.0).astype(BF16)


def _rope(t, c, s1, s2):
    half = SWA_HD // 2
    return t * c + pltpu.roll(t, LANES - half, 1) * s1 + pltpu.roll(t, half, 1) * s2


def _head_rms(t, g, head_w):
    ms = _split_dot(t * t, head_w) * (1.0 / SWA_HD)
    return t * lax.rsqrt(ms + EPS) * g


def _row_rms(x):
    n_tiles = x.shape[1] // LANES
    ss = x[:, 0:LANES] * x[:, 0:LANES]
    for j in range(1, n_tiles):
        ss = ss + x[:, j * LANES:(j + 1) * LANES] * x[:, j * LANES:(j + 1) * LANES]
    r = lax.rsqrt(_lane_sum(ss) * (1.0 / x.shape[1]) + EPS)
    return x * jnp.concatenate([r] * n_tiles, axis=1)


def _proj_kernel(x_ref, g1_ref, w_ref, qg_ref, kg_ref, cos_ref, s1_ref, s2_ref,
                 rqk_ref, rv_ref, gate_ref, sq_ref, skv_ref):
    xn = (_row_rms(x_ref[...]) * g1_ref[...]).astype(BF16)
    c, s1, s2 = cos_ref[...], s1_ref[...], s2_ref[...]
    head_w = _head_sum_matrix()

    def seg(a, b):
        return jnp.dot(xn, w_ref[:, a:b], preferred_element_type=F32)

    def tile(h, j):
        return h[:, j * LANES:(j + 1) * LANES]

    h = seg(0, 2 * RET_Q)
    for j in range(2):
        rqk_ref[:, j * LANES:(j + 1) * LANES] = _rope(tile(h, j), c, s1, s2).astype(BF16)
    for j in range(2, 4):
        rqk_ref[:, j * LANES:(j + 1) * LANES] = (_rope(tile(h, j), c, s1, s2) * (RET_DK ** -0.5)).astype(BF16)
    a = 2 * RET_Q
    rv_ref[...] = seg(a, a + RET_V).astype(BF16)
    a += RET_V
    g = seg(a, a + RET_V)
    gate_ref[...] = (g * jax.nn.sigmoid(g)).astype(BF16)
    a += RET_V
    h = seg(a, a + SWA_Q)
    qg = qg_ref[...]
    for j in range(SWA_Q // LANES):
        sq_ref[:, j * LANES:(j + 1) * LANES] = _rope(_head_rms(tile(h, j), qg, head_w), c, s1, s2).astype(BF16)
    a += SWA_Q
    h = seg(a, a + 2 * SWA_KV)
    skv_ref[:, 0:LANES] = _rope(_head_rms(tile(h, 0), kg_ref[...], head_w), c, s1, s2)
    skv_ref[:, LANES:2 * LANES] = tile(h, 1)


def _rope_tables(pos):
    half = SWA_HD // 2
    inv = ROPE_THETA ** (-jnp.arange(half, dtype=F32) / half)
    ang = pos.astype(F32)[:, None] * inv[None, :]
    cos, sin = jnp.cos(ang), jnp.sin(ang)
    z = jnp.zeros_like(sin)
    heads_per_tile = LANES // SWA_HD
    return (jnp.tile(cos, (1, 2 * heads_per_tile)),
            jnp.tile(jnp.concatenate([-sin, z], axis=1), (1, heads_per_tile)),
            jnp.tile(jnp.concatenate([z, sin], axis=1), (1, heads_per_tile)))


def _proj(x2d, pos_rows, tm, g1, w_in_bf, qg2, kg2):
    t_rows = x2d.shape[0]
    n_tiles = t_rows // tm
    n_pos_tiles = pos_rows.shape[0] // tm
    cos, s1, s2 = _rope_tables(pos_rows)
    row = lambda i: (i, 0)
    const = lambda i: (0, 0)
    tab = lambda i: (i % n_pos_tiles, 0)
    return pl.pallas_call(
        _proj_kernel,
        grid=(n_tiles,),
        in_specs=[pl.BlockSpec((tm, D_MODEL), row),
                  pl.BlockSpec((1, D_MODEL), const),
                  pl.BlockSpec((D_MODEL, IN_WIDTH), const),
                  pl.BlockSpec((1, LANES), const),
                  pl.BlockSpec((1, LANES), const),
                  pl.BlockSpec((tm, LANES), tab),
                  pl.BlockSpec((tm, LANES), tab),
                  pl.BlockSpec((tm, LANES), tab)],
        out_specs=[pl.BlockSpec((tm, 2 * RET_Q), row),
                   pl.BlockSpec((tm, RET_V), row),
                   pl.BlockSpec((tm, RET_V), row),
                   pl.BlockSpec((tm, SWA_Q), row),
                   pl.BlockSpec((tm, 2 * SWA_KV), row)],
        out_shape=[jax.ShapeDtypeStruct((t_rows, 2 * RET_Q), BF16),
                   jax.ShapeDtypeStruct((t_rows, RET_V), BF16),
                   jax.ShapeDtypeStruct((t_rows, RET_V), BF16),
                   jax.ShapeDtypeStruct((t_rows, SWA_Q), BF16),
                   jax.ShapeDtypeStruct((t_rows, 2 * SWA_KV), F32)],
        compiler_params=_params(1),
        name="proj",
    )(x2d, g1, w_in_bf, qg2, kg2, cos, s1, s2)


def _pair_update(k_bf, v0_bf, v1_bf, wt):
    kw = (k_bf.astype(F32) * wt).astype(BF16)
    dn = (((0,), (0,)), ((), ()))
    a0 = lax.dot_general(kw, v0_bf, dn, preferred_element_type=F32)
    a1 = lax.dot_general(kw, v1_bf, dn, preferred_element_type=F32)
    top = lax.broadcasted_iota(jnp.int32, a0.shape, 0) < RET_DK
    return jnp.where(top, a0, a1)


def _decay_rows(n, pair, rows_back_from):
    i = lax.broadcasted_iota(jnp.int32, (n, LANES), 0).astype(F32)
    lane = lax.broadcasted_iota(jnp.int32, (n, LANES), 1)
    lg = jnp.where(lane < RET_DK, _LOG_G[2 * pair], _LOG_G[2 * pair + 1])
    return jnp.exp((rows_back_from - i) * lg)


def _meta_state_kernel(rqk_ref, rv_ref, s_ref, *, n_rows):
    for p in range(RET_HEADS // 2):
        k = rqk_ref[:, RET_Q + p * LANES:RET_Q + (p + 1) * LANES]
        wt = _decay_rows(n_rows, p, float(N_META - 1))
        s_ref[p] = _pair_update(k, rv_ref[:, (2 * p) * LANES:(2 * p + 1) * LANES],
                                rv_ref[:, (2 * p + 1) * LANES:(2 * p + 2) * LANES], wt)


def _meta_state(m_rqk, m_rv):
    n_rows = m_rqk.shape[0]
    return pl.pallas_call(
        functools.partial(_meta_state_kernel, n_rows=n_rows),
        out_shape=jax.ShapeDtypeStruct((RET_HEADS // 2, 2 * RET_DK, RET_DV), F32),
        name="meta_state",
    )(m_rqk, m_rv)


def _dup_halves(a, lo_mask):
    sw = pltpu.roll(a, SWA_HD, 1)
    return jnp.where(lo_mask, a, sw), jnp.where(lo_mask, sw, a)


def _attn_kernel(rqk_ref, rv_ref, gate_ref, sq_ref, skv_ref, meta_ref, hist_ref, s0_ref, rng_ref, sink_ref,
                 omix_ref, sout_ref, kvout_ref,
                 s_scr, kd_scr, vd_scr, mk_scr, mv_scr, dec_scr, wt_scr, cs_scr, gam_scr,
                 *, tl, has_hist):
    b = pl.program_id(0)
    t = pl.program_id(1)
    nt = pl.num_programs(1)
    n_chunks = tl // CHUNK
    n_pairs = RET_HEADS // 2
    lo_tl = lax.broadcasted_iota(jnp.int32, (tl, LANES), 1) < SWA_HD
    lo_c = lax.broadcasted_iota(jnp.int32, (CHUNK, LANES), 1) < SWA_HD

    @pl.when((b == 0) & (t == 0))
    def _tables():
        i = lax.broadcasted_iota(jnp.int32, (tl, tl), 0)
        j = lax.broadcasted_iota(jnp.int32, (tl, tl), 1)
        diff = (i - j).astype(F32)
        row = lax.broadcasted_iota(jnp.int32, (tl, LANES), 0).astype(F32)
        for h in range(RET_HEADS):
            dec_scr[h] = jnp.where(diff >= 0.0, jnp.exp(jnp.maximum(diff, 0.0) * _LOG_G[h]), 0.0)
            cs_scr[h] = jnp.exp((row + 1.0) * _LOG_G[h])
        top = lax.broadcasted_iota(jnp.int32, (2 * RET_DK, RET_DV), 0) < RET_DK
        for p in range(n_pairs):
            wt_scr[p] = _decay_rows(tl, p, float(tl - 1))
            gam_scr[p] = jnp.where(top, jnp.exp(jnp.float32(tl * _LOG_G[2 * p])), jnp.exp(jnp.float32(tl * _LOG_G[2 * p + 1])))
        lo_m = lax.broadcasted_iota(jnp.int32, (N_META, LANES), 1) < SWA_HD
        mk0, mk1 = _dup_halves(meta_ref[:, 0:LANES], lo_m)
        mv0, mv1 = _dup_halves(meta_ref[:, LANES:2 * LANES], lo_m)
        mk_scr[...] = jnp.zeros(mk_scr.shape, BF16)
        mv_scr[...] = jnp.zeros(mv_scr.shape, BF16)
        mk_scr[0, 0:N_META] = mk0.astype(BF16)
        mk_scr[1, 0:N_META] = mk1.astype(BF16)
        mv_scr[0, 0:N_META] = mv0.astype(BF16)
        mv_scr[1, 0:N_META] = mv1.astype(BF16)

    @pl.when(t == 0)
    def _stream_start():
        s_scr[...] = s0_ref[0]
        if has_hist:
            lo_w = lax.broadcasted_iota(jnp.int32, (WINDOW, LANES), 1) < SWA_HD
            k0, k1 = _dup_halves(hist_ref[0, :, 0:LANES], lo_w)
            v0, v1 = _dup_halves(hist_ref[0, :, LANES:2 * LANES], lo_w)
            kd_scr[0, 0:WINDOW] = k0.astype(BF16)
            kd_scr[1, 0:WINDOW] = k1.astype(BF16)
            vd_scr[0, 0:WINDOW] = v0.astype(BF16)
            vd_scr[1, 0:WINDOW] = v1.astype(BF16)
        else:
            z = jnp.zeros((WINDOW, LANES), BF16)
            for kv in range(SWA_KV_HEADS):
                kd_scr[kv, 0:WINDOW] = z
                vd_scr[kv, 0:WINDOW] = z

    k0, k1 = _dup_halves(skv_ref[:, 0:LANES], lo_tl)
    v0, v1 = _dup_halves(skv_ref[:, LANES:2 * LANES], lo_tl)
    kd_scr[0, WINDOW:WINDOW + tl] = k0.astype(BF16)
    kd_scr[1, WINDOW:WINDOW + tl] = k1.astype(BF16)
    vd_scr[0, WINDOW:WINDOW + tl] = v0.astype(BF16)
    vd_scr[1, WINDOW:WINDOW + tl] = v1.astype(BF16)

    band = WINDOW + CHUNK
    n_keys = META_ROWS + band
    n_q = 4 * CHUNK
    scale2 = (SWA_HD ** -0.5) * LOG2E
    krow = lax.broadcasted_iota(jnp.int32, (n_keys, n_q), 0)
    zero_c = jnp.zeros((CHUNK, LANES), BF16)
    ones_v = jnp.ones((n_keys, LANES), BF16)
    for c in range(n_chunks):
        if has_hist:
            first_valid = META_ROWS
        else:
            first_valid = jnp.where(t == 0, max(META_ROWS + WINDOW - c * CHUNK, META_ROWS), META_ROWS)
        valid_t = (krow < N_META) | (krow >= first_valid)
        r0 = c * CHUNK
        for kv in range(SWA_KV_HEADS):
            keys = jnp.concatenate([mk_scr[kv], kd_scr[kv, r0:r0 + band]], axis=0)
            vals = jnp.concatenate([mv_scr[kv], vd_scr[kv, r0:r0 + band]], axis=0)
            qa = sq_ref[r0:r0 + CHUNK, (2 * kv) * LANES:(2 * kv + 1) * LANES]
            qb = sq_ref[r0:r0 + CHUNK, (2 * kv + 1) * LANES:(2 * kv + 2) * LANES]
            lhs = jnp.concatenate([jnp.where(lo_c, qa, zero_c), jnp.where(lo_c, zero_c, qa),
                                   jnp.where(lo_c, qb, zero_c), jnp.where(lo_c, zero_c, qb)], axis=0)
            s_t = lax.dot_general(keys, lhs, (((1,), (1,)), ((), ())), preferred_element_type=F32) * scale2
            s_t = jnp.where(valid_t, s_t, NEG_INF)
            s_t = jnp.where(krow == N_META, sink_ref[kv, 0:1, :] * LOG2E, s_t)
            e_t = jnp.exp2(s_t - jnp.max(s_t, axis=0, keepdims=True)).astype(BF16)
            ov = lax.dot_general(e_t, jnp.concatenate([vals, ones_v], axis=1), (((0,), (0,)), ((), ())),
                                 preferred_element_type=F32)
            o = ov[:, 0:LANES] * (1.0 / ov[:, LANES:2 * LANES])
            oa = jnp.where(lo_c, o[0:CHUNK], o[CHUNK:2 * CHUNK])
            ob = jnp.where(lo_c, o[2 * CHUNK:3 * CHUNK], o[3 * CHUNK:4 * CHUNK])
            base = RET_V + (2 * kv) * LANES
            omix_ref[r0:r0 + CHUNK, base:base + LANES] = oa.astype(BF16)
            omix_ref[r0:r0 + CHUNK, base + LANES:base + 2 * LANES] = ob.astype(BF16)

    zero_t = jnp.zeros((tl, LANES), BF16)
    for p in range(n_pairs):
        q = rqk_ref[:, p * LANES:(p + 1) * LANES]
        k = rqk_ref[:, RET_Q + p * LANES:RET_Q + (p + 1) * LANES]
        lhs = jnp.concatenate([jnp.where(lo_tl, q, zero_t), jnp.where(lo_tl, zero_t, q)], axis=0)
        s = lax.dot_general(lhs, k, (((1,), (1,)), ((), ())), preferred_element_type=F32)
        cross = jnp.dot(lhs, s_scr[p].astype(BF16), preferred_element_type=F32)
        for i in range(2):
            h = 2 * p + i
            v = rv_ref[:, h * LANES:(h + 1) * LANES]
            a = (s[i * tl:(i + 1) * tl] * dec_scr[h]).astype(BF16)
            o = jnp.dot(a, v, preferred_element_type=F32) + cross[i * tl:(i + 1) * tl] * cs_scr[h]
            r = o * lax.rsqrt(_lane_sum(o * o) * (1.0 / RET_DV) + EPS) * rng_ref[h:h + 1, :]
            omix_ref[:, h * LANES:(h + 1) * LANES] = (r * gate_ref[:, h * LANES:(h + 1) * LANES].astype(F32)).astype(BF16)
        u = _pair_update(k, rv_ref[:, (2 * p) * LANES:(2 * p + 1) * LANES],
                         rv_ref[:, (2 * p + 1) * LANES:(2 * p + 2) * LANES], wt_scr[p])
        s_scr[p] = gam_scr[p] * s_scr[p] + u

    if tl >= WINDOW:
        @pl.when(t + 1 < nt)
        def _carry_window():
            for kv in range(SWA_KV_HEADS):
                kd_scr[kv, 0:WINDOW] = kd_scr[kv, tl:tl + WINDOW]
                vd_scr[kv, 0:WINDOW] = vd_scr[kv, tl:tl + WINDOW]

    @pl.when(t + 1 == nt)
    def _stream_end():
        sout_ref[0] = s_scr[...]
        if tl >= WINDOW:
            kvout_ref[0] = skv_ref[tl - WINDOW:tl, :]
        else:
            kvout_ref[0, 0:WINDOW - tl] = hist_ref[0, tl:WINDOW, :]
            kvout_ref[0, WINDOW - tl:WINDOW] = skv_ref[...]


def _attention(rqk, rv, gate, sq, skv, meta_kv, hist_kv, s0, rng, sink_tab, *, n_streams, seq, tl, has_hist):
    nt = seq // tl
    assert tl % CHUNK == 0 and seq % tl == 0
    assert tl >= WINDOW or (nt == 1 and has_hist)
    n_pairs = RET_HEADS // 2
    s0_shared = s0.shape[0] == 1
    row = lambda b, t: (b * nt + t, 0)
    const2 = lambda b, t: (0, 0)
    const3 = lambda b, t: (0, 0, 0)
    per_b3 = lambda b, t: (b, 0, 0)
    s0_map = (lambda b, t: (0, 0, 0, 0)) if s0_shared else (lambda b, t: (b, 0, 0, 0))
    hist_map = per_b3 if has_hist else const3
    rows = n_streams * seq
    return pl.pallas_call(
        functools.partial(_attn_kernel, tl=tl, has_hist=has_hist),
        grid=(n_streams, nt),
        in_specs=[pl.BlockSpec((tl, 2 * RET_Q), row),
                  pl.BlockSpec((tl, RET_V), row),
                  pl.BlockSpec((tl, RET_V), row),
                  pl.BlockSpec((tl, SWA_Q), row),
                  pl.BlockSpec((tl, 2 * SWA_KV), row),
                  pl.BlockSpec((N_META, 2 * SWA_KV), const2),
                  pl.BlockSpec((1, WINDOW, 2 * SWA_KV), hist_map),
                  pl.BlockSpec((1, n_pairs, 2 * RET_DK, RET_DV), s0_map),
                  pl.BlockSpec((RET_HEADS, RET_DV), const2),
                  pl.BlockSpec((SWA_KV_HEADS, SUBLANES, 4 * CHUNK), const3)],
        out_specs=[pl.BlockSpec((tl, MIX_WIDTH), row),
                   pl.BlockSpec((1, n_pairs, 2 * RET_DK, RET_DV), lambda b, t: (b, 0, 0, 0)),
                   pl.BlockSpec((1, WINDOW, 2 * SWA_KV), per_b3)],
        out_shape=[jax.ShapeDtypeStruct((rows, MIX_WIDTH), BF16),
                   jax.ShapeDtypeStruct((n_streams, n_pairs, 2 * RET_DK, RET_DV), F32),
                   jax.ShapeDtypeStruct((n_streams, WINDOW, 2 * SWA_KV), F32)],
        scratch_shapes=[pltpu.VMEM((n_pairs, 2 * RET_DK, RET_DV), F32),
                        pltpu.VMEM((SWA_KV_HEADS, WINDOW + tl, LANES), BF16),
                        pltpu.VMEM((SWA_KV_HEADS, WINDOW + tl, LANES), BF16),
                        pltpu.VMEM((SWA_KV_HEADS, META_ROWS, LANES), BF16),
                        pltpu.VMEM((SWA_KV_HEADS, META_ROWS, LANES), BF16),
                        pltpu.VMEM((RET_HEADS, tl, tl), F32),
                        pltpu.VMEM((n_pairs, tl, LANES), F32),
                        pltpu.VMEM((RET_HEADS, tl, RET_DV), F32),
                        pltpu.VMEM((n_pairs, 2 * RET_DK, RET_DV), F32)],
        compiler_params=_params(2),
        name="attention",
    )(rqk, rv, gate, sq, skv, meta_kv, hist_kv, s0, rng, sink_tab)


def _post_kernel(omix_ref, x_ref, wout_ref, g2_ref, wrt_ref, base_ref,
                 xmid_ref, wcol_ref, rt_ref, cnt_ref, tri_scr, run_scr):
    i = pl.program_id(0)
    tm = x_ref.shape[0]

    @pl.when(i == 0)
    def _init():
        r = lax.broadcasted_iota(jnp.int32, (tm, tm), 0)
        c = lax.broadcasted_iota(jnp.int32, (tm, tm), 1)
        tri_scr[...] = jnp.where(r < c, 1.0, 0.0).astype(BF16)
        run_scr[...] = base_ref[...]

    xm = x_ref[...] + jnp.dot(omix_ref[...], wout_ref[...], preferred_element_type=F32)
    xmid_ref[...] = xm
    hn = _row_rms(xm) * g2_ref[...]
    lt = lax.dot_general(wrt_ref[...], hn.astype(BF16), (((1,), (1,)), ((), ())), preferred_element_type=F32)
    row8 = lax.broadcasted_iota(jnp.int32, (SUBLANES, tm), 0)
    big = jnp.int32(SUBLANES)
    gl = jnp.where(row8 < N_GROUPS, lt[0:SUBLANES], NEG_INF)
    gmax = jnp.max(gl, axis=0, keepdims=True)
    gsum = jnp.sum(jnp.exp(gl - gmax), axis=0, keepdims=True)
    g_sel = jnp.min(jnp.where(gl == gmax, row8, big), axis=0, keepdims=True)
    p_sel = 1.0 / gsum
    el = lt[ROUTER_EXPERT_ROW0:ROUTER_EXPERT_ROW0 + EXPERTS_PER_GROUP]
    for g in range(1, N_GROUPS):
        lo = ROUTER_EXPERT_ROW0 + g * EXPERTS_PER_GROUP
        el = jnp.where(g_sel == g, lt[lo:lo + EXPERTS_PER_GROUP], el)
    m1 = jnp.max(el, axis=0, keepdims=True)
    i1 = jnp.min(jnp.where(el == m1, row8, big), axis=0, keepdims=True)
    el2 = jnp.where(row8 == i1, NEG_INF, el)
    m2 = jnp.max(el2, axis=0, keepdims=True)
    i2 = jnp.min(jnp.where(el2 == m2, row8, big), axis=0, keepdims=True)
    e2 = jnp.exp(m2 - m1)
    inv = 1.0 / (1.0 + e2)
    w1 = p_sel * inv
    w2 = p_sel * (e2 * inv)
    eid1 = g_sel * EXPERTS_PER_GROUP + i1
    eid2 = g_sel * EXPERTS_PER_GROUP + i2

    rowe = lax.broadcasted_iota(jnp.int32, (N_EXPERTS, tm), 0)
    oh = jnp.where((rowe == eid1) | (rowe == eid2), 1.0, 0.0).astype(BF16)
    run = run_scr[...]
    pref = jnp.dot(oh, tri_scr[...], preferred_element_type=F32) + jnp.concatenate([run] * (tm // LANES), axis=1)
    r1 = jnp.sum(jnp.where(rowe == eid1, pref, 0.0), axis=0, keepdims=True)
    r2 = jnp.sum(jnp.where(rowe == eid2, pref, 0.0), axis=0, keepdims=True)
    run = run + jnp.dot(oh, jnp.ones((tm, LANES), BF16), preferred_element_type=F32)
    run_scr[...] = run
    cnt_ref[...] = run

    out = jnp.zeros((SUBLANES, tm), F32)
    for k, v in enumerate([eid1.astype(F32), eid2.astype(F32), w1, w2, r1, r2]):
        out = jnp.where(row8 == k, v, out)
    rt_ref[...] = out
    rowl = lax.broadcasted_iota(jnp.int32, (LANES, tm), 0)
    wcol_ref[...] = jnp.where(rowl == 0, w1, jnp.where(rowl == 1, w2, 0.0)).T


def _post(omix, x2d, w_out_bf, g2, w_router_t_bf, base_cnt, tm):
    t_rows = x2d.shape[0]
    assert tm % LANES == 0
    row = lambda i: (i, 0)
    const = lambda i: (0, 0)
    return pl.pallas_call(
        _post_kernel,
        grid=(t_rows // tm,),
        in_specs=[pl.BlockSpec((tm, MIX_WIDTH), row),
                  pl.BlockSpec((tm, D_MODEL), row),
                  pl.BlockSpec((MIX_WIDTH, D_MODEL), const),
                  pl.BlockSpec((1, D_MODEL), const),
                  pl.BlockSpec((ROUTER_ROWS, D_MODEL), const),
                  pl.BlockSpec((N_EXPERTS, LANES), const)],
        out_specs=[pl.BlockSpec((tm, D_MODEL), row),
                   pl.BlockSpec((tm, LANES), row),
                   pl.BlockSpec((SUBLANES, tm), lambda i: (0, i)),
                   pl.BlockSpec((N_EXPERTS, LANES), const)],
        out_shape=[jax.ShapeDtypeStruct((t_rows, D_MODEL), F32),
                   jax.ShapeDtypeStruct((t_rows, LANES), F32),
                   jax.ShapeDtypeStruct((SUBLANES, t_rows), F32),
                   jax.ShapeDtypeStruct((N_EXPERTS, LANES), F32)],
        scratch_shapes=[pltpu.VMEM((tm, tm), BF16), pltpu.VMEM((N_EXPERTS, LANES), F32)],
        compiler_params=_params(1),
        name="post",
    )(omix, x2d, w_out_bf, g2, w_router_t_bf, base_cnt)


def _step_major(pos, tm):
    return [pos[e].reshape(-1, 1, tm) for e in range(2)]


def _dispatch_kernel(pos0_ref, pos1_ref, *refs, tm, group_steps):
    pos_refs = (pos0_ref, pos1_ref)
    n_g = len(group_steps)
    xmid_refs = refs[:n_g]
    g2_ref, xs_ref, hbuf, sems = refs[n_g:]
    n_steps = sum(group_steps)
    i = pl.program_id(0)
    slot = i % 2

    first = 0
    for xmid_ref, steps in zip(xmid_refs, group_steps):
        @pl.when((i >= first) & (i < first + steps))
        def _normalise(xmid_ref=xmid_ref):
            hn = _row_rms(xmid_ref[...]) * g2_ref[...]
            hbuf[slot] = hn.reshape(tm // SUBLANES, SUBLANES, D_MODEL)
        first += steps

    def body(blk, carry):
        for k in range(SUBLANES):
            r = blk * SUBLANES + k
            for e in range(2):
                pltpu.make_async_copy(hbuf.at[slot, blk, pl.ds(k, 1)], xs_ref.at[pl.ds(pos_refs[e][0, 0, r], 1)],
                                      sems.at[slot]).start(priority=e)
        return carry

    lax.fori_loop(0, tm // SUBLANES, body, 0)

    def drain(which):
        pltpu.make_async_copy(xs_ref.at[pl.ds(0, 2 * tm)], xs_ref.at[pl.ds(0, 2 * tm)], sems.at[which]).wait()

    @pl.when(i > 0)
    def _previous():
        drain(1 - slot)

    @pl.when(i == n_steps - 1)
    def _last():
        drain(slot)


def _dispatch(pos_list, xmid_list, g2, tm):
    group_steps = tuple(x.shape[0] // tm for x in xmid_list)
    n_steps = sum(group_steps)
    assert tm % SUBLANES == 0
    pos3 = [jnp.concatenate(parts, axis=0) for parts in zip(*[_step_major(p, tm) for p in pos_list])]
    smem = pl.BlockSpec((1, 1, tm), lambda i: (i, 0, 0), memory_space=pltpu.SMEM)
    x_specs = []
    first = 0
    for steps in group_steps:
        x_specs.append(pl.BlockSpec((tm, D_MODEL),
                                    lambda i, first=first, steps=steps: (jnp.clip(i - first, 0, steps - 1), 0)))
        first += steps
    n_rows_out = 2 * sum(x.shape[0] for x in xmid_list)
    return pl.pallas_call(
        functools.partial(_dispatch_kernel, tm=tm, group_steps=group_steps),
        grid=(n_steps,),
        in_specs=[smem, smem] + x_specs + [pl.BlockSpec((1, D_MODEL), lambda i: (0, 0))],
        out_specs=pl.BlockSpec(memory_space=pl.ANY),
        out_shape=jax.ShapeDtypeStruct((n_rows_out, D_MODEL), F32),
        scratch_shapes=[pltpu.VMEM((2, tm // SUBLANES, SUBLANES, D_MODEL), F32), pltpu.SemaphoreType.DMA((2,))],
        compiler_params=_params(1),
        name="dispatch",
    )(*pos3, *xmid_list, g2)


def _expert_kernel(vt_ref, ve_ref, lo_ref, hi_ref, nv_ref, x_ref, wg_ref, wu_ref, wd_ref, y_ref, wgu_scr, wd_scr):
    v = pl.program_id(0)

    @pl.when(v < nv_ref[0])
    def _compute():
        @pl.when((v == 0) | (ve_ref[v] != ve_ref[jnp.maximum(v - 1, 0)]))
        def _new_expert():
            wgu_scr[:, 0:EXPERT_FF] = wg_ref[0].astype(BF16)
            wgu_scr[:, EXPERT_FF:2 * EXPERT_FF] = wu_ref[0].astype(BF16)
            wd_scr[...] = wd_ref[0].astype(BF16)

        x = x_ref[...].astype(BF16)
        gu = jnp.dot(x, wgu_scr[...], preferred_element_type=F32)
        g = gu[:, 0:EXPERT_FF]
        a = (g * jax.nn.sigmoid(g) * gu[:, EXPERT_FF:2 * EXPERT_FF]).astype(BF16)
        y = jnp.dot(a, wd_scr[...], preferred_element_type=F32)
        lo, hi = lo_ref[v], hi_ref[v]

        @pl.when(lo == 0)
        def _first_visit():
            y_ref[...] = y

        @pl.when(lo > 0)
        def _later_visit():
            row = lax.broadcasted_iota(jnp.int32, y.shape, 0)
            y_ref[...] = jnp.where((row >= lo) & (row < hi), y, y_ref[...])


def _experts(vis_tile, vis_expert, vis_lo, vis_hi, n_vis, xs, w_gate, w_up, w_down, tm):
    n_steps = vis_tile.shape[0]
    last = lambda v, nv: jnp.minimum(v, nv[0] - 1)
    of_expert = lambda v, vt, ve, lo, hi, nv: (ve[last(v, nv)], 0, 0)
    return pl.pallas_call(
        _expert_kernel,
        grid_spec=pltpu.PrefetchScalarGridSpec(
            num_scalar_prefetch=5,
            grid=(n_steps,),
            in_specs=[pl.BlockSpec((tm, D_MODEL), lambda v, vt, ve, lo, hi, nv: (vt[last(v, nv)], 0)),
                      pl.BlockSpec((1, D_MODEL, EXPERT_FF), of_expert),
                      pl.BlockSpec((1, D_MODEL, EXPERT_FF), of_expert),
                      pl.BlockSpec((1, EXPERT_FF, D_MODEL), of_expert)],
            out_specs=pl.BlockSpec((tm, D_MODEL), lambda v, vt, ve, lo, hi, nv: (vt[last(v, nv)], 0)),
            scratch_shapes=[pltpu.VMEM((D_MODEL, 2 * EXPERT_FF), BF16), pltpu.VMEM((EXPERT_FF, D_MODEL), BF16)]),
        out_shape=jax.ShapeDtypeStruct(xs.shape, F32),
        compiler_params=_params(1),
        name="experts",
    )(vis_tile, vis_expert, vis_lo, vis_hi, n_vis, xs, w_gate, w_up, w_down)


def _combine_kernel(pos0_ref, pos1_ref, nxt0_ref, nxt1_ref, ys_ref, xmid_ref, wcol_ref, out_ref, ybuf, sems,
                    *, tm, n_steps):
    i = pl.program_id(0)
    slot = i % 2

    def issue(p_refs, to_slot):
        def body(blk, carry):
            for k in range(SUBLANES):
                r = blk * SUBLANES + k
                for e in range(2):
                    pltpu.make_async_copy(ys_ref.at[pl.ds(p_refs[e][0, 0, r], 1)],
                                          ybuf.at[to_slot, e, blk, pl.ds(k, 1)], sems.at[to_slot]).start(priority=e)
            return carry

        lax.fori_loop(0, tm // SUBLANES, body, 0)

    @pl.when(i == 0)
    def _first():
        issue((pos0_ref, pos1_ref), 0)

    @pl.when(i + 1 < n_steps)
    def _ahead():
        issue((nxt0_ref, nxt1_ref), 1 - slot)

    for e in range(2):
        pltpu.make_async_copy(ybuf.at[slot, e], ybuf.at[slot, e], sems.at[slot]).wait()
    w = wcol_ref[...]
    y0 = ybuf[slot, 0].reshape(tm, D_MODEL)
    y1 = ybuf[slot, 1].reshape(tm, D_MODEL)
    out_ref[...] = xmid_ref[...] + w[:, 0:1] * y0 + w[:, 1:2] * y1


def _combine(pos, ys, xmid, wcol, tm):
    t_rows = xmid.shape[0]
    n_steps = t_rows // tm
    assert tm % SUBLANES == 0
    pos3 = _step_major(pos, tm)
    row = lambda i: (i, 0)
    cur = pl.BlockSpec((1, 1, tm), lambda i: (i, 0, 0), memory_space=pltpu.SMEM)
    nxt = pl.BlockSpec((1, 1, tm), lambda i: (jnp.minimum(i + 1, n_steps - 1), 0, 0), memory_space=pltpu.SMEM)
    return pl.pallas_call(
        functools.partial(_combine_kernel, tm=tm, n_steps=n_steps),
        grid=(n_steps,),
        in_specs=[cur, cur, nxt, nxt,
                  pl.BlockSpec(memory_space=pl.ANY),
                  pl.BlockSpec((tm, D_MODEL), row),
                  pl.BlockSpec((tm, LANES), row)],
        out_specs=pl.BlockSpec((tm, D_MODEL), row),
        out_shape=jax.ShapeDtypeStruct((t_rows, D_MODEL), F32),
        scratch_shapes=[pltpu.VMEM((2, 2, tm // SUBLANES, SUBLANES, D_MODEL), F32), pltpu.SemaphoreType.DMA((2,))],
        compiler_params=_params(1),
        name="combine",
    )(*pos3, *pos3, ys, xmid, wcol)


def _bucket(ends, idx):
    n = jnp.sum((ends[None, :] <= idx[:, None]).astype(jnp.int32), axis=1)
    return jnp.minimum(n, ends.shape[0] - 1)


def _tile_for(rows, pref):
    tm = min(pref, rows)
    assert rows % tm == 0
    return tm


def kernel(x_prompt, x_sample, cache_ret_state, cache_swa_k, cache_swa_v, meta_tokens, norm1_g, w_in, q_norm_g,
           k_norm_g, ret_norm_g, attn_sinks, w_out, norm2_g, w_group, w_expert, w_gate, w_up, w_down):
    assert norm1_g.shape[0] == 1, "single-layer trunk"
    bp, lp, _ = x_prompt.shape
    bs, ls, _ = x_sample.shape
    n_pairs = RET_HEADS // 2

    g1 = norm1_g[0][None, :]
    g2 = norm2_g[0][None, :]
    w_in_bf = w_in[0].astype(BF16)
    w_out_bf = w_out[0].astype(BF16)
    qg2 = jnp.tile(q_norm_g[0], 2)[None, :]
    kg2 = jnp.tile(k_norm_g[0], 2)[None, :]
    rng = ret_norm_g[0].reshape(RET_HEADS, RET_DV)
    sink_tab = jnp.broadcast_to(jnp.repeat(attn_sinks[0], CHUNK).reshape(SWA_KV_HEADS, 1, 4 * CHUNK),
                                (SWA_KV_HEADS, SUBLANES, 4 * CHUNK))
    w_router_t = jnp.zeros((ROUTER_ROWS, D_MODEL), F32)
    w_router_t = w_router_t.at[0:N_GROUPS].set(w_group[0].T)
    w_router_t = w_router_t.at[ROUTER_EXPERT_ROW0:ROUTER_EXPERT_ROW0 + N_EXPERTS].set(w_expert[0].T)
    w_router_t_bf = w_router_t.astype(BF16)

    meta_rows = 2 * CHUNK
    m_pad = jnp.zeros((meta_rows, D_MODEL), F32).at[0:N_META].set(meta_tokens)
    m_rqk, m_rv, _, _, m_skv = _proj(m_pad, jnp.arange(meta_rows, dtype=jnp.int32), meta_rows, g1, w_in_bf, qg2, kg2)
    s_meta = _meta_state(m_rqk, m_rv)[None]
    meta_kv = m_skv[0:N_META]

    groups = [
        dict(x=x_prompt.reshape(bp * lp, D_MODEL), n=bp, seq=lp, pos0=N_META, has_hist=False, s0=s_meta,
             hist=jnp.zeros((1, WINDOW, 2 * SWA_KV), F32)),
        dict(x=x_sample.reshape(bs * ls, D_MODEL), n=bs, seq=ls, pos0=N_META + PAST_LEN, has_hist=True,
             s0=cache_ret_state[0].reshape(bs, n_pairs, 2 * RET_DK, RET_DV),
             hist=jnp.concatenate([cache_swa_k[0].reshape(bs, WINDOW, SWA_KV),
                                   cache_swa_v[0].reshape(bs, WINDOW, SWA_KV)], axis=-1)),
    ]

    base_cnt = jnp.zeros((N_EXPERTS, LANES), F32)
    for g in groups:
        rows = g["n"] * g["seq"]
        tm = _tile_for(rows, PROJ_TILE)
        pos = g["pos0"] + jnp.arange(g["seq"], dtype=jnp.int32)
        if g["seq"] < tm:
            assert tm % g["seq"] == 0
            pos = jnp.tile(pos, tm // g["seq"])
        else:
            assert g["seq"] % tm == 0
        rqk, rv, gate, sq, skv = _proj(g["x"], pos, tm, g1, w_in_bf, qg2, kg2)
        tl = min(ATTN_TILE, g["seq"])
        omix, s_out, kv_out = _attention(rqk, rv, gate, sq, skv, meta_kv, g["hist"], g["s0"], rng, sink_tab,
                                         n_streams=g["n"], seq=g["seq"], tl=tl, has_hist=g["has_hist"])
        xmid, wcol, route_t, base_cnt = _post(omix, g["x"], w_out_bf, g2, w_router_t_bf, base_cnt,
                                              _tile_for(rows, POST_TILE))
        g.update(xmid=xmid, wcol=wcol, route_t=route_t, s_out=s_out, kv_out=kv_out)

    te = EXPERT_TILE
    total_rows = sum(g["n"] * g["seq"] for g in groups)
    assert (2 * total_rows) % te == 0
    n_row_tiles = (2 * total_rows) // te
    counts = base_cnt[:, 0].astype(jnp.int32)
    off = jnp.cumsum(counts) - counts
    first_tile = off // te
    n_vis_e = jnp.where(counts > 0, (off + counts - 1) // te - first_tile + 1, 0)
    vis_end = jnp.cumsum(n_vis_e)
    n_vis = vis_end[-1:].astype(jnp.int32)
    v = jnp.arange(n_row_tiles + N_EXPERTS, dtype=jnp.int32)
    vis_expert = _bucket(vis_end, v)
    pick = lambda table: jnp.sum(jnp.where(vis_expert[:, None] == jnp.arange(N_EXPERTS, dtype=jnp.int32)[None, :],
                                           table[None, :], 0), axis=1)
    vis_tile = jnp.clip(pick(first_tile) + v - pick(vis_end - n_vis_e), 0, n_row_tiles - 1).astype(jnp.int32)
    vis_lo = jnp.clip(pick(off) - vis_tile * te, 0, te).astype(jnp.int32)
    vis_hi = jnp.clip(pick(off + counts) - vis_tile * te, 0, te).astype(jnp.int32)

    for g in groups:
        eid = g["route_t"][0:2].astype(jnp.int32)
        off_sel = jnp.sum(jnp.where(eid[None] == jnp.arange(N_EXPERTS, dtype=jnp.int32)[:, None, None],
                                    off[:, None, None], 0), axis=0)
        g["pos"] = (off_sel + g["route_t"][4:6].astype(jnp.int32)).astype(jnp.int32)
    xs = _dispatch([g["pos"] for g in groups], [g["xmid"] for g in groups], g2,
                   _tile_for(min(g["n"] * g["seq"] for g in groups), MOVE_TILE))

    ys = _experts(vis_tile, vis_expert.astype(jnp.int32), vis_lo, vis_hi, n_vis, xs, w_gate[0], w_up[0], w_down[0], te)

    outs = []
    for g in groups:
        rows = g["n"] * g["seq"]
        y = _combine(g["pos"], ys, g["xmid"], g["wcol"], _tile_for(rows, MOVE_TILE))
        outs.append(y.reshape(g["n"], g["seq"], D_MODEL))

    def caches(g):
        kv = g["kv_out"]
        k = kv[:, :, 0:SWA_KV].reshape(g["n"], WINDOW, SWA_KV_HEADS, SWA_HD)[None]
        v = kv[:, :, SWA_KV:2 * SWA_KV].reshape(g["n"], WINDOW, SWA_KV_HEADS, SWA_HD)[None]
        s = g["s_out"].reshape(g["n"], RET_HEADS, RET_DK, RET_DV)[None]
        return s, k, v

    sp, kp, vp = caches(groups[0])
    ss, ks, vs = caches(groups[1])
    return (outs[0], outs[1], sp, kp, vp, ss, ks, vs)
```

```python
import functools

import numpy as np
import jax
import jax.numpy as jnp
from jax import lax
from jax.experimental import pallas as pl
from jax.experimental.pallas import tpu as pltpu

F32 = jnp.float32
BF16 = jnp.bfloat16

D_MODEL = 1024
PAST_LEN = 4096
CHUNK = 64
N_META = 16
RET_HEADS = 4
RET_DK = 64
RET_DV = 128
SWA_HEADS = 8
SWA_KV_HEADS = 2
SWA_HD = 64
WINDOW = 128
ROPE_THETA = 10000.0
N_GROUPS = 4
EXPERTS_PER_GROUP = 8
N_EXPERTS = N_GROUPS * EXPERTS_PER_GROUP
EXPERT_FF = 256
EPS = 1e-6
NEG_INF = -1e30
LOG2E = float(np.log2(np.e))
RET_Q = RET_HEADS * RET_DK
RET_V = RET_HEADS * RET_DV
SWA_Q = SWA_HEADS * SWA_HD
SWA_KV = SWA_KV_HEADS * SWA_HD
MIX_WIDTH = RET_V + SWA_Q
IN_WIDTH = 2 * RET_Q + 2 * RET_V + SWA_Q + 2 * SWA_KV

LANES = 128
PROJ_TILE = 1024
POST_TILE = 1024
ATTN_TILE = 256
EXPERT_TILE = 1024
MOVE_TILE = 512
SUBLANES = 8
ROW_WIDTH = D_MODEL + LANES
TOKEN_LANE = 4
ROUTER_EXPERT_ROW0 = 8
ROUTER_ROWS = 64
META_ROWS = 64
VMEM_LIMIT = 56 * 1024 * 1024

_LOG_G = [float(np.log1p(-np.exp2(-5.0 - h))) for h in range(RET_HEADS)]


def _params(n_axes):
    return pltpu.CompilerParams(dimension_semantics=("arbitrary",) * n_axes, vmem_limit_bytes=VMEM_LIMIT)


def _split_bf16(a):
    hi = a.astype(BF16)
    return hi, (a - hi.astype(F32)).astype(BF16)


def _split_dot(a, w2):
    hi, lo = _split_bf16(a)
    return jnp.dot(jnp.concatenate([hi, lo], axis=1), w2, preferred_element_type=F32)


def _lane_sum(a):
    return _split_dot(a, jnp.ones((2 * LANES, LANES), BF16))


def _head_sum_matrix():
    i = lax.broadcasted_iota(jnp.int32, (2 * LANES, LANES), 0) % LANES
    j = lax.broadcasted_iota(jnp.int32, (2 * LANES, LANES), 1)
    return jnp.where((i < SWA_HD) == (j < SWA_HD), 1.0, 0.0).astype(BF16)


def _rope(t, c, s1, s2):
    half = SWA_HD // 2
    return t * c + pltpu.roll(t, LANES - half, 1) * s1 + pltpu.roll(t, half, 1) * s2


def _head_rms(t, g, head_w):
    ms = _split_dot(t * t, head_w) * (1.0 / SWA_HD)
    return t * lax.rsqrt(ms + EPS) * g


def _row_rms(x):
    n_tiles = x.shape[1] // LANES
    ss = x[:, 0:LANES] * x[:, 0:LANES]
    for j in range(1, n_tiles):
        ss = ss + x[:, j * LANES:(j + 1) * LANES] * x[:, j * LANES:(j + 1) * LANES]
    r = lax.rsqrt(_lane_sum(ss) * (1.0 / x.shape[1]) + EPS)
    return x * jnp.concatenate([r] * n_tiles, axis=1)


def _proj_kernel(x_ref, g1_ref, w_ref, qg_ref, kg_ref, cos_ref, s1_ref, s2_ref,
                 rqk_ref, rv_ref, gate_ref, sq_ref, skv_ref):
    xn = (_row_rms(x_ref[...]) * g1_ref[...]).astype(BF16)
    c, s1, s2 = cos_ref[...], s1_ref[...], s2_ref[...]
    head_w = _head_sum_matrix()

    def seg(a, b):
        return jnp.dot(xn, w_ref[:, a:b], preferred_element_type=F32)

    def tile(h, j):
        return h[:, j * LANES:(j + 1) * LANES]

    h = seg(0, 2 * RET_Q)
    for j in range(2):
        rqk_ref[:, j * LANES:(j + 1) * LANES] = _rope(tile(h, j), c, s1, s2).astype(BF16)
    for j in range(2, 4):
        rqk_ref[:, j * LANES:(j + 1) * LANES] = (_rope(tile(h, j), c, s1, s2) * (RET_DK ** -0.5)).astype(BF16)
    a = 2 * RET_Q
    rv_ref[...] = seg(a, a + RET_V).astype(BF16)
    a += RET_V
    g = seg(a, a + RET_V)
    gate_ref[...] = (g * jax.nn.sigmoid(g)).astype(BF16)
    a += RET_V
    h = seg(a, a + SWA_Q)
    qg = qg_ref[...]
    for j in range(SWA_Q // LANES):
        sq_ref[:, j * LANES:(j + 1) * LANES] = _rope(_head_rms(tile(h, j), qg, head_w), c, s1, s2).astype(BF16)
    a += SWA_Q
    h = seg(a, a + 2 * SWA_KV)
    skv_ref[:, 0:LANES] = _rope(_head_rms(tile(h, 0), kg_ref[...], head_w), c, s1, s2)
    skv_ref[:, LANES:2 * LANES] = tile(h, 1)


def _rope_tables(pos):
    half = SWA_HD // 2
    inv = ROPE_THETA ** (-jnp.arange(half, dtype=F32) / half)
    ang = pos.astype(F32)[:, None] * inv[None, :]
    cos, sin = jnp.cos(ang), jnp.sin(ang)
    z = jnp.zeros_like(sin)
    heads_per_tile = LANES // SWA_HD
    return (jnp.tile(cos, (1, 2 * heads_per_tile)),
            jnp.tile(jnp.concatenate([-sin, z], axis=1), (1, heads_per_tile)),
            jnp.tile(jnp.concatenate([z, sin], axis=1), (1, heads_per_tile)))


def _proj(x2d, pos_rows, tm, g1, w_in_bf, qg2, kg2):
    t_rows = x2d.shape[0]
    n_tiles = t_rows // tm
    n_pos_tiles = pos_rows.shape[0] // tm
    cos, s1, s2 = _rope_tables(pos_rows)
    row = lambda i: (i, 0)
    const = lambda i: (0, 0)
    tab = lambda i: (i % n_pos_tiles, 0)
    return pl.pallas_call(
        _proj_kernel,
        grid=(n_tiles,),
        in_specs=[pl.BlockSpec((tm, D_MODEL), row),
                  pl.BlockSpec((1, D_MODEL), const),
                  pl.BlockSpec((D_MODEL, IN_WIDTH), const),
                  pl.BlockSpec((1, LANES), const),
                  pl.BlockSpec((1, LANES), const),
                  pl.BlockSpec((tm, LANES), tab),
                  pl.BlockSpec((tm, LANES), tab),
                  pl.BlockSpec((tm, LANES), tab)],
        out_specs=[pl.BlockSpec((tm, 2 * RET_Q), row),
                   pl.BlockSpec((tm, RET_V), row),
                   pl.BlockSpec((tm, RET_V), row),
                   pl.BlockSpec((tm, SWA_Q), row),
                   pl.BlockSpec((tm, 2 * SWA_KV), row)],
        out_shape=[jax.ShapeDtypeStruct((t_rows, 2 * RET_Q), BF16),
                   jax.ShapeDtypeStruct((t_rows, RET_V), BF16),
                   jax.ShapeDtypeStruct((t_rows, RET_V), BF16),
                   jax.ShapeDtypeStruct((t_rows, SWA_Q), BF16),
                   jax.ShapeDtypeStruct((t_rows, 2 * SWA_KV), F32)],
        compiler_params=_params(1),
        name="proj",
    )(x2d, g1, w_in_bf, qg2, kg2, cos, s1, s2)


def _pair_update(k_bf, v0_bf, v1_bf, wt):
    kw = (k_bf.astype(F32) * wt).astype(BF16)
    dn = (((0,), (0,)), ((), ()))
    a0 = lax.dot_general(kw, v0_bf, dn, preferred_element_type=F32)
    a1 = lax.dot_general(kw, v1_bf, dn, preferred_element_type=F32)
    top = lax.broadcasted_iota(jnp.int32, a0.shape, 0) < RET_DK
    return jnp.where(top, a0, a1)


def _decay_rows(n, pair, rows_back_from):
    i = lax.broadcasted_iota(jnp.int32, (n, LANES), 0).astype(F32)
    lane = lax.broadcasted_iota(jnp.int32, (n, LANES), 1)
    lg = jnp.where(lane < RET_DK, _LOG_G[2 * pair], _LOG_G[2 * pair + 1])
    return jnp.exp((rows_back_from - i) * lg)


def _meta_state_kernel(rqk_ref, rv_ref, s_ref, *, n_rows):
    for p in range(RET_HEADS // 2):
        k = rqk_ref[:, RET_Q + p * LANES:RET_Q + (p + 1) * LANES]
        wt = _decay_rows(n_rows, p, float(N_META - 1))
        s_ref[p] = _pair_update(k, rv_ref[:, (2 * p) * LANES:(2 * p + 1) * LANES],
                                rv_ref[:, (2 * p + 1) * LANES:(2 * p + 2) * LANES], wt)


def _meta_state(m_rqk, m_rv):
    n_rows = m_rqk.shape[0]
    return pl.pallas_call(
        functools.partial(_meta_state_kernel, n_rows=n_rows),
        out_shape=jax.ShapeDtypeStruct((RET_HEADS // 2, 2 * RET_DK, RET_DV), F32),
        name="meta_state",
    )(m_rqk, m_rv)


def _dup_halves(a, lo_mask):
    sw = pltpu.roll(a, SWA_HD, 1)
    return jnp.where(lo_mask, a, sw), jnp.where(lo_mask, sw, a)


def _attn_kernel(rqk_ref, rv_ref, gate_ref, sq_ref, skv_ref, meta_ref, hist_ref, s0_ref, rng_ref, sink_ref,
                 omix_ref, sout_ref, kvout_ref,
                 s_scr, kd_scr, vd_scr, mk_scr, mv_scr, dec_scr, wt_scr, cs_scr, gam_scr,
                 *, tl, has_hist):
    b = pl.program_id(0)
    t = pl.program_id(1)
    nt = pl.num_programs(1)
    n_chunks = tl // CHUNK
    n_pairs = RET_HEADS // 2
    lo_tl = lax.broadcasted_iota(jnp.int32, (tl, LANES), 1) < SWA_HD
    lo_c = lax.broadcasted_iota(jnp.int32, (CHUNK, LANES), 1) < SWA_HD

    @pl.when((b == 0) & (t == 0))
    def _tables():
        i = lax.broadcasted_iota(jnp.int32, (tl, tl), 0)
        j = lax.broadcasted_iota(jnp.int32, (tl, tl), 1)
        diff = (i - j).astype(F32)
        row = lax.broadcasted_iota(jnp.int32, (tl, LANES), 0).astype(F32)
        for h in range(RET_HEADS):
            dec_scr[h] = jnp.where(diff >= 0.0, jnp.exp(jnp.maximum(diff, 0.0) * _LOG_G[h]), 0.0)
            cs_scr[h] = jnp.exp((row + 1.0) * _LOG_G[h])
        top = lax.broadcasted_iota(jnp.int32, (2 * RET_DK, RET_DV), 0) < RET_DK
        for p in range(n_pairs):
            wt_scr[p] = _decay_rows(tl, p, float(tl - 1))
            gam_scr[p] = jnp.where(top, jnp.exp(jnp.float32(tl * _LOG_G[2 * p])), jnp.exp(jnp.float32(tl * _LOG_G[2 * p + 1])))
        lo_m = lax.broadcasted_iota(jnp.int32, (N_META, LANES), 1) < SWA_HD
        mk0, mk1 = _dup_halves(meta_ref[:, 0:LANES], lo_m)
        mv0, mv1 = _dup_halves(meta_ref[:, LANES:2 * LANES], lo_m)
        mk_scr[...] = jnp.zeros(mk_scr.shape, BF16)
        mv_scr[...] = jnp.zeros(mv_scr.shape, BF16)
        mk_scr[0, 0:N_META] = mk0.astype(BF16)
        mk_scr[1, 0:N_META] = mk1.astype(BF16)
        mv_scr[0, 0:N_META] = mv0.astype(BF16)
        mv_scr[1, 0:N_META] = mv1.astype(BF16)

    @pl.when(t == 0)
    def _stream_start():
        s_scr[...] = s0_ref[0]
        if has_hist:
            lo_w = lax.broadcasted_iota(jnp.int32, (WINDOW, LANES), 1) < SWA_HD
            k0, k1 = _dup_halves(hist_ref[0, :, 0:LANES], lo_w)
            v0, v1 = _dup_halves(hist_ref[0, :, LANES:2 * LANES], lo_w)
            kd_scr[0, 0:WINDOW] = k0.astype(BF16)
            kd_scr[1, 0:WINDOW] = k1.astype(BF16)
            vd_scr[0, 0:WINDOW] = v0.astype(BF16)
            vd_scr[1, 0:WINDOW] = v1.astype(BF16)
        else:
            z = jnp.zeros((WINDOW, LANES), BF16)
            for kv in range(SWA_KV_HEADS):
                kd_scr[kv, 0:WINDOW] = z
                vd_scr[kv, 0:WINDOW] = z

    k0, k1 = _dup_halves(skv_ref[:, 0:LANES], lo_tl)
    v0, v1 = _dup_halves(skv_ref[:, LANES:2 * LANES], lo_tl)
    kd_scr[0, WINDOW:WINDOW + tl] = k0.astype(BF16)
    kd_scr[1, WINDOW:WINDOW + tl] = k1.astype(BF16)
    vd_scr[0, WINDOW:WINDOW + tl] = v0.astype(BF16)
    vd_scr[1, WINDOW:WINDOW + tl] = v1.astype(BF16)

    band = WINDOW + CHUNK
    n_keys = META_ROWS + band
    n_q = 4 * CHUNK
    scale2 = (SWA_HD ** -0.5) * LOG2E
    krow = lax.broadcasted_iota(jnp.int32, (n_keys, n_q), 0)
    zero_c = jnp.zeros((CHUNK, LANES), BF16)
    ones_v = jnp.ones((n_keys, LANES), BF16)
    for c in range(n_chunks):
        if has_hist:
            first_valid = META_ROWS
        else:
            first_valid = jnp.where(t == 0, max(META_ROWS + WINDOW - c * CHUNK, META_ROWS), META_ROWS)
        valid_t = (krow < N_META) | (krow >= first_valid)
        r0 = c * CHUNK
        for kv in range(SWA_KV_HEADS):
            keys = jnp.concatenate([mk_scr[kv], kd_scr[kv, r0:r0 + band]], axis=0)
            vals = jnp.concatenate([mv_scr[kv], vd_scr[kv, r0:r0 + band]], axis=0)
            qa = sq_ref[r0:r0 + CHUNK, (2 * kv) * LANES:(2 * kv + 1) * LANES]
            qb = sq_ref[r0:r0 + CHUNK, (2 * kv + 1) * LANES:(2 * kv + 2) * LANES]
            lhs = jnp.concatenate([jnp.where(lo_c, qa, zero_c), jnp.where(lo_c, zero_c, qa),
                                   jnp.where(lo_c, qb, zero_c), jnp.where(lo_c, zero_c, qb)], axis=0)
            s_t = lax.dot_general(keys, lhs, (((1,), (1,)), ((), ())), preferred_element_type=F32) * scale2
            s_t = jnp.where(valid_t, s_t, NEG_INF)
            s_t = jnp.where(krow == N_META, sink_ref[kv, 0:1, :] * LOG2E, s_t)
            e_t = jnp.exp2(s_t - jnp.max(s_t, axis=0, keepdims=True)).astype(BF16)
            ov = lax.dot_general(e_t, jnp.concatenate([vals, ones_v], axis=1), (((0,), (0,)), ((), ())),
                                 preferred_element_type=F32)
            o = ov[:, 0:LANES] * (1.0 / ov[:, LANES:2 * LANES])
            oa = jnp.where(lo_c, o[0:CHUNK], o[CHUNK:2 * CHUNK])
            ob = jnp.where(lo_c, o[2 * CHUNK:3 * CHUNK], o[3 * CHUNK:4 * CHUNK])
            base = RET_V + (2 * kv) * LANES
            omix_ref[r0:r0 + CHUNK, base:base + LANES] = oa.astype(BF16)
            omix_ref[r0:r0 + CHUNK, base + LANES:base + 2 * LANES] = ob.astype(BF16)

    zero_t = jnp.zeros((tl, LANES), BF16)
    for p in range(n_pairs):
        q = rqk_ref[:, p * LANES:(p + 1) * LANES]
        k = rqk_ref[:, RET_Q + p * LANES:RET_Q + (p + 1) * LANES]
        lhs = jnp.concatenate([jnp.where(lo_tl, q, zero_t), jnp.where(lo_tl, zero_t, q)], axis=0)
        s = lax.dot_general(lhs, k, (((1,), (1,)), ((), ())), preferred_element_type=F32)
        cross = jnp.dot(lhs, s_scr[p].astype(BF16), preferred_element_type=F32)
        for i in range(2):
            h = 2 * p + i
            v = rv_ref[:, h * LANES:(h + 1) * LANES]
            a = (s[i * tl:(i + 1) * tl] * dec_scr[h]).astype(BF16)
            o = jnp.dot(a, v, preferred_element_type=F32) + cross[i * tl:(i + 1) * tl] * cs_scr[h]
            r = o * lax.rsqrt(_lane_sum(o * o) * (1.0 / RET_DV) + EPS) * rng_ref[h:h + 1, :]
            omix_ref[:, h * LANES:(h + 1) * LANES] = (r * gate_ref[:, h * LANES:(h + 1) * LANES].astype(F32)).astype(BF16)
        u = _pair_update(k, rv_ref[:, (2 * p) * LANES:(2 * p + 1) * LANES],
                         rv_ref[:, (2 * p + 1) * LANES:(2 * p + 2) * LANES], wt_scr[p])
        s_scr[p] = gam_scr[p] * s_scr[p] + u

    if tl >= WINDOW:
        @pl.when(t + 1 < nt)
        def _carry_window():
            for kv in range(SWA_KV_HEADS):
                kd_scr[kv, 0:WINDOW] = kd_scr[kv, tl:tl + WINDOW]
                vd_scr[kv, 0:WINDOW] = vd_scr[kv, tl:tl + WINDOW]

    @pl.when(t + 1 == nt)
    def _stream_end():
        sout_ref[0] = s_scr[...]
        if tl >= WINDOW:
            kvout_ref[0] = skv_ref[tl - WINDOW:tl, :]
        else:
            kvout_ref[0, 0:WINDOW - tl] = hist_ref[0, tl:WINDOW, :]
            kvout_ref[0, WINDOW - tl:WINDOW] = skv_ref[...]


def _attention(rqk, rv, gate, sq, skv, meta_kv, hist_kv, s0, rng, sink_tab, *, n_streams, seq, tl, has_hist):
    nt = seq // tl
    assert tl % CHUNK == 0 and seq % tl == 0
    assert tl >= WINDOW or (nt == 1 and has_hist)
    n_pairs = RET_HEADS // 2
    s0_shared = s0.shape[0] == 1
    row = lambda b, t: (b * nt + t, 0)
    const2 = lambda b, t: (0, 0)
    const3 = lambda b, t: (0, 0, 0)
    per_b3 = lambda b, t: (b, 0, 0)
    s0_map = (lambda b, t: (0, 0, 0, 0)) if s0_shared else (lambda b, t: (b, 0, 0, 0))
    hist_map = per_b3 if has_hist else const3
    rows = n_streams * seq
    return pl.pallas_call(
        functools.partial(_attn_kernel, tl=tl, has_hist=has_hist),
        grid=(n_streams, nt),
        in_specs=[pl.BlockSpec((tl, 2 * RET_Q), row),
                  pl.BlockSpec((tl, RET_V), row),
                  pl.BlockSpec((tl, RET_V), row),
                  pl.BlockSpec((tl, SWA_Q), row),
                  pl.BlockSpec((tl, 2 * SWA_KV), row),
                  pl.BlockSpec((N_META, 2 * SWA_KV), const2),
                  pl.BlockSpec((1, WINDOW, 2 * SWA_KV), hist_map),
                  pl.BlockSpec((1, n_pairs, 2 * RET_DK, RET_DV), s0_map),
                  pl.BlockSpec((RET_HEADS, RET_DV), const2),
                  pl.BlockSpec((SWA_KV_HEADS, SUBLANES, 4 * CHUNK), const3)],
        out_specs=[pl.BlockSpec((tl, MIX_WIDTH), row),
                   pl.BlockSpec((1, n_pairs, 2 * RET_DK, RET_DV), lambda b, t: (b, 0, 0, 0)),
                   pl.BlockSpec((1, WINDOW, 2 * SWA_KV), per_b3)],
        out_shape=[jax.ShapeDtypeStruct((rows, MIX_WIDTH), BF16),
                   jax.ShapeDtypeStruct((n_streams, n_pairs, 2 * RET_DK, RET_DV), F32),
                   jax.ShapeDtypeStruct((n_streams, WINDOW, 2 * SWA_KV), F32)],
        scratch_shapes=[pltpu.VMEM((n_pairs, 2 * RET_DK, RET_DV), F32),
                        pltpu.VMEM((SWA_KV_HEADS, WINDOW + tl, LANES), BF16),
                        pltpu.VMEM((SWA_KV_HEADS, WINDOW + tl, LANES), BF16),
                        pltpu.VMEM((SWA_KV_HEADS, META_ROWS, LANES), BF16),
                        pltpu.VMEM((SWA_KV_HEADS, META_ROWS, LANES), BF16),
                        pltpu.VMEM((RET_HEADS, tl, tl), F32),
                        pltpu.VMEM((n_pairs, tl, LANES), F32),
                        pltpu.VMEM((RET_HEADS, tl, RET_DV), F32),
                        pltpu.VMEM((n_pairs, 2 * RET_DK, RET_DV), F32)],
        compiler_params=_params(2),
        name="attention",
    )(rqk, rv, gate, sq, skv, meta_kv, hist_kv, s0, rng, sink_tab)


def _post_kernel(omix_ref, x_ref, wout_ref, g2_ref, wrt_ref, base_ref,
                 xmid_ref, wcol_ref, rt_ref, cnt_ref, tri_scr, run_scr):
    i = pl.program_id(0)
    tm = x_ref.shape[0]

    @pl.when(i == 0)
    def _init():
        r = lax.broadcasted_iota(jnp.int32, (tm, tm), 0)
        c = lax.broadcasted_iota(jnp.int32, (tm, tm), 1)
        tri_scr[...] = jnp.where(r < c, 1.0, 0.0).astype(BF16)
        run_scr[...] = base_ref[...]

    xm = x_ref[...] + jnp.dot(omix_ref[...], wout_ref[...], preferred_element_type=F32)
    xmid_ref[...] = xm
    hn = _row_rms(xm) * g2_ref[...]
    lt = lax.dot_general(wrt_ref[...], hn.astype(BF16), (((1,), (1,)), ((), ())), preferred_element_type=F32)
    row8 = lax.broadcasted_iota(jnp.int32, (SUBLANES, tm), 0)
    big = jnp.int32(SUBLANES)
    gl = jnp.where(row8 < N_GROUPS, lt[0:SUBLANES], NEG_INF)
    gmax = jnp.max(gl, axis=0, keepdims=True)
    gsum = jnp.sum(jnp.exp(gl - gmax), axis=0, keepdims=True)
    g_sel = jnp.min(jnp.where(gl == gmax, row8, big), axis=0, keepdims=True)
    p_sel = 1.0 / gsum
    el = lt[ROUTER_EXPERT_ROW0:ROUTER_EXPERT_ROW0 + EXPERTS_PER_GROUP]
    for g in range(1, N_GROUPS):
        lo = ROUTER_EXPERT_ROW0 + g * EXPERTS_PER_GROUP
        el = jnp.where(g_sel == g, lt[lo:lo + EXPERTS_PER_GROUP], el)
    m1 = jnp.max(el, axis=0, keepdims=True)
    i1 = jnp.min(jnp.where(el == m1, row8, big), axis=0, keepdims=True)
    el2 = jnp.where(row8 == i1, NEG_INF, el)
    m2 = jnp.max(el2, axis=0, keepdims=True)
    i2 = jnp.min(jnp.where(el2 == m2, row8, big), axis=0, keepdims=True)
    e2 = jnp.exp(m2 - m1)
    inv = 1.0 / (1.0 + e2)
    w1 = p_sel * inv
    w2 = p_sel * (e2 * inv)
    eid1 = g_sel * EXPERTS_PER_GROUP + i1
    eid2 = g_sel * EXPERTS_PER_GROUP + i2

    rowe = lax.broadcasted_iota(jnp.int32, (N_EXPERTS, tm), 0)
    oh = jnp.where((rowe == eid1) | (rowe == eid2), 1.0, 0.0).astype(BF16)
    run = run_scr[...]
    pref = jnp.dot(oh, tri_scr[...], preferred_element_type=F32) + jnp.concatenate([run] * (tm // LANES), axis=1)
    r1 = jnp.sum(jnp.where(rowe == eid1, pref, 0.0), axis=0, keepdims=True)
    r2 = jnp.sum(jnp.where(rowe == eid2, pref, 0.0), axis=0, keepdims=True)
    run = run + jnp.dot(oh, jnp.ones((tm, LANES), BF16), preferred_element_type=F32)
    run_scr[...] = run
    cnt_ref[...] = run

    out = jnp.zeros((SUBLANES, tm), F32)
    for k, v in enumerate([eid1.astype(F32), eid2.astype(F32), w1, w2, r1, r2]):
        out = jnp.where(row8 == k, v, out)
    rt_ref[...] = out
    rowl = lax.broadcasted_iota(jnp.int32, (LANES, tm), 0)
    rec = jnp.zeros((LANES, tm), F32)
    for k, v in enumerate([w1, w2, eid1.astype(F32), eid2.astype(F32)]):
        rec = jnp.where(rowl == k, v, rec)
    wcol_ref[...] = rec.T


def _post(omix, x2d, w_out_bf, g2, w_router_t_bf, base_cnt, tm):
    t_rows = x2d.shape[0]
    assert tm % LANES == 0
    row = lambda i: (i, 0)
    const = lambda i: (0, 0)
    return pl.pallas_call(
        _post_kernel,
        grid=(t_rows // tm,),
        in_specs=[pl.BlockSpec((tm, MIX_WIDTH), row),
                  pl.BlockSpec((tm, D_MODEL), row),
                  pl.BlockSpec((MIX_WIDTH, D_MODEL), const),
                  pl.BlockSpec((1, D_MODEL), const),
                  pl.BlockSpec((ROUTER_ROWS, D_MODEL), const),
                  pl.BlockSpec((N_EXPERTS, LANES), const)],
        out_specs=[pl.BlockSpec((tm, D_MODEL), row),
                   pl.BlockSpec((tm, LANES), row),
                   pl.BlockSpec((SUBLANES, tm), lambda i: (0, i)),
                   pl.BlockSpec((N_EXPERTS, LANES), const)],
        out_shape=[jax.ShapeDtypeStruct((t_rows, D_MODEL), F32),
                   jax.ShapeDtypeStruct((t_rows, LANES), F32),
                   jax.ShapeDtypeStruct((SUBLANES, t_rows), F32),
                   jax.ShapeDtypeStruct((N_EXPERTS, LANES), F32)],
        scratch_shapes=[pltpu.VMEM((tm, tm), BF16), pltpu.VMEM((N_EXPERTS, LANES), F32)],
        compiler_params=_params(1),
        name="post",
    )(omix, x2d, w_out_bf, g2, w_router_t_bf, base_cnt)


def _step_major(pos, tm):
    return [pos[e].reshape(-1, 1, tm) for e in range(2)]


def _dispatch_kernel(pos0_ref, pos1_ref, *refs, tm, group_steps):
    pos_refs = (pos0_ref, pos1_ref)
    n_g = len(group_steps)
    xmid_refs, rec_refs = refs[:n_g], refs[n_g:2 * n_g]
    g2_ref, xs_ref, hbuf, sems = refs[2 * n_g:]
    n_steps = sum(group_steps)
    i = pl.program_id(0)
    slot = i % 2

    first = 0
    for xmid_ref, rec_ref, steps in zip(xmid_refs, rec_refs, group_steps):
        @pl.when((i >= first) & (i < first + steps))
        def _normalise(xmid_ref=xmid_ref, rec_ref=rec_ref, first=first):
            hn = _row_rms(xmid_ref[...]) * g2_ref[...]
            lane = lax.broadcasted_iota(jnp.int32, (tm, LANES), 1)
            tok = (first * tm + (i - first) * tm + lax.broadcasted_iota(jnp.int32, (tm, LANES), 0)).astype(F32)
            rec = jnp.where(lane == TOKEN_LANE, tok, rec_ref[...])
            hbuf[slot] = jnp.concatenate([hn, rec], axis=1).reshape(tm // SUBLANES, SUBLANES, ROW_WIDTH)
        first += steps

    def body(blk, carry):
        for k in range(SUBLANES):
            r = blk * SUBLANES + k
            for e in range(2):
                pltpu.make_async_copy(hbuf.at[slot, blk, pl.ds(k, 1)], xs_ref.at[pl.ds(pos_refs[e][0, 0, r], 1)],
                                      sems.at[slot]).start(priority=e)
        return carry

    lax.fori_loop(0, tm // SUBLANES, body, 0)

    def drain(which):
        pltpu.make_async_copy(xs_ref.at[pl.ds(0, 2 * tm)], xs_ref.at[pl.ds(0, 2 * tm)], sems.at[which]).wait()

    @pl.when(i > 0)
    def _previous():
        drain(1 - slot)

    @pl.when(i == n_steps - 1)
    def _last():
        drain(slot)


def _dispatch(pos_list, xmid_list, rec_list, g2, tm):
    group_steps = tuple(x.shape[0] // tm for x in xmid_list)
    n_steps = sum(group_steps)
    assert tm % SUBLANES == 0
    pos3 = [jnp.concatenate(parts, axis=0) for parts in zip(*[_step_major(p, tm) for p in pos_list])]
    smem = pl.BlockSpec((1, 1, tm), lambda i: (i, 0, 0), memory_space=pltpu.SMEM)
    x_specs, rec_specs = [], []
    first = 0
    for steps in group_steps:
        held = lambda i, first=first, steps=steps: (jnp.clip(i - first, 0, steps - 1), 0)
        x_specs.append(pl.BlockSpec((tm, D_MODEL), held))
        rec_specs.append(pl.BlockSpec((tm, LANES), held))
        first += steps
    n_rows_out = 2 * sum(x.shape[0] for x in xmid_list)
    return pl.pallas_call(
        functools.partial(_dispatch_kernel, tm=tm, group_steps=group_steps),
        grid=(n_steps,),
        in_specs=[smem, smem] + x_specs + rec_specs + [pl.BlockSpec((1, D_MODEL), lambda i: (0, 0))],
        out_specs=pl.BlockSpec(memory_space=pl.ANY),
        out_shape=jax.ShapeDtypeStruct((n_rows_out, ROW_WIDTH), F32),
        scratch_shapes=[pltpu.VMEM((2, tm // SUBLANES, SUBLANES, ROW_WIDTH), F32), pltpu.SemaphoreType.DMA((2,))],
        compiler_params=_params(1),
        name="dispatch",
    )(*pos3, *xmid_list, *rec_list, g2)


def _expert_kernel(vt_ref, ve_ref, lo_ref, hi_ref, nv_ref, x_ref, wg_ref, wu_ref, wd_ref, y2_ref,
                   wgu_scr, wd_scr, ybuf, dest_vmem, dest_smem, sems, dest_sem, *, tm):
    v = pl.program_id(0)
    n_vis = nv_ref[0]
    slot = v % 2

    def drain(which, n_rows):
        def eight(_, carry):
            pltpu.make_async_copy(y2_ref.at[pl.ds(0, SUBLANES)], y2_ref.at[pl.ds(0, SUBLANES)], sems.at[which]).wait()
            return carry

        def one(_, carry):
            pltpu.make_async_copy(y2_ref.at[pl.ds(0, 1)], y2_ref.at[pl.ds(0, 1)], sems.at[which]).wait()
            return carry

        lax.fori_loop(0, n_rows // SUBLANES, eight, 0)
        lax.fori_loop(0, n_rows % SUBLANES, one, 0)

    @pl.when(v < n_vis)
    def _compute():
        @pl.when((v == 0) | (ve_ref[v] != ve_ref[jnp.maximum(v - 1, 0)]))
        def _new_expert():
            wgu_scr[:, 0:EXPERT_FF] = wg_ref[0].astype(BF16)
            wgu_scr[:, EXPERT_FF:2 * EXPERT_FF] = wu_ref[0].astype(BF16)
            wd_scr[...] = wd_ref[0].astype(BF16)

        lo, hi = lo_ref[v], hi_ref[v]
        rec = x_ref[:, D_MODEL:ROW_WIDTH]
        second = jnp.where(rec[:, 2:3] == ve_ref[v].astype(F32), 0.0, 1.0)
        dest = 2.0 * rec[:, TOKEN_LANE:TOKEN_LANE + 1] + second
        dest_vmem[...] = jnp.broadcast_to(dest, (tm, LANES)).T[0:SUBLANES].astype(jnp.int32)
        to_smem = pltpu.make_async_copy(dest_vmem.at[pl.ds(0, 1)], dest_smem, dest_sem)
        to_smem.start()

        x = x_ref[:, 0:D_MODEL].astype(BF16)
        gu = jnp.dot(x, wgu_scr[...], preferred_element_type=F32)
        g = gu[:, 0:EXPERT_FF]
        a = (g * jax.nn.sigmoid(g) * gu[:, EXPERT_FF:2 * EXPERT_FF]).astype(BF16)
        y = jnp.dot(a, wd_scr[...], preferred_element_type=F32)
        ybuf[slot] = y.reshape(tm // SUBLANES, SUBLANES, D_MODEL)
        to_smem.wait()

        def body(pair, carry):
            for k in range(2):
                r = 2 * pair + k

                @pl.when((r >= lo) & (r < hi))
                def _row():
                    pltpu.make_async_copy(ybuf.at[slot, r // SUBLANES, pl.ds(r % SUBLANES, 1)],
                                          y2_ref.at[pl.ds(dest_smem[0, r], 1)], sems.at[slot]).start(priority=k)
            return carry

        lax.fori_loop(lo // 2, (hi + 1) // 2, body, 0)

        @pl.when(v > 0)
        def _previous():
            drain(1 - slot, hi_ref[v - 1] - lo_ref[v - 1])

        @pl.when(v == n_vis - 1)
        def _last():
            drain(slot, hi - lo)


def _experts(vis_tile, vis_expert, vis_lo, vis_hi, n_vis, xs, w_gate, w_up, w_down, tm):
    n_steps = vis_tile.shape[0]
    last = lambda v, nv: jnp.minimum(v, nv[0] - 1)
    of_expert = lambda v, vt, ve, lo, hi, nv: (ve[last(v, nv)], 0, 0)
    return pl.pallas_call(
        functools.partial(_expert_kernel, tm=tm),
        grid_spec=pltpu.PrefetchScalarGridSpec(
            num_scalar_prefetch=5,
            grid=(n_steps,),
            in_specs=[pl.BlockSpec((tm, ROW_WIDTH), lambda v, vt, ve, lo, hi, nv: (vt[last(v, nv)], 0)),
                      pl.BlockSpec((1, D_MODEL, EXPERT_FF), of_expert),
                      pl.BlockSpec((1, D_MODEL, EXPERT_FF), of_expert),
                      pl.BlockSpec((1, EXPERT_FF, D_MODEL), of_expert)],
            out_specs=pl.BlockSpec(memory_space=pl.ANY),
            scratch_shapes=[pltpu.VMEM((D_MODEL, 2 * EXPERT_FF), BF16), pltpu.VMEM((EXPERT_FF, D_MODEL), BF16),
                            pltpu.VMEM((2, tm // SUBLANES, SUBLANES, D_MODEL), F32),
                            pltpu.VMEM((SUBLANES, tm), jnp.int32), pltpu.SMEM((1, tm), jnp.int32),
                            pltpu.SemaphoreType.DMA((2,)), pltpu.SemaphoreType.DMA(())]),
        out_shape=jax.ShapeDtypeStruct((xs.shape[0], D_MODEL), F32),
        compiler_params=_params(1),
        name="experts",
    )(vis_tile, vis_expert, vis_lo, vis_hi, n_vis, xs, w_gate, w_up, w_down)


def _combine_kernel(y2_ref, xmid_ref, rec_ref, out_ref):
    w = rec_ref[...]
    out_ref[...] = xmid_ref[...] + w[:, 0:1] * y2_ref[:, 0:D_MODEL] + w[:, 1:2] * y2_ref[:, D_MODEL:2 * D_MODEL]


def _combine(y2, xmid, rec, row0, tm):
    t_rows = xmid.shape[0]
    assert row0 % tm == 0
    row = lambda i: (i, 0)
    return pl.pallas_call(
        _combine_kernel,
        grid=(t_rows // tm,),
        in_specs=[pl.BlockSpec((tm, 2 * D_MODEL), lambda i: (row0 // tm + i, 0)),
                  pl.BlockSpec((tm, D_MODEL), row),
                  pl.BlockSpec((tm, LANES), row)],
        out_specs=pl.BlockSpec((tm, D_MODEL), row),
        out_shape=jax.ShapeDtypeStruct((t_rows, D_MODEL), F32),
        compiler_params=_params(1),
        name="combine",
    )(y2, xmid, rec)


def _bucket(ends, idx):
    n = jnp.sum((ends[None, :] <= idx[:, None]).astype(jnp.int32), axis=1)
    return jnp.minimum(n, ends.shape[0] - 1)


def _tile_for(rows, pref):
    tm = min(pref, rows)
    assert rows % tm == 0
    return tm


def kernel(x_prompt, x_sample, cache_ret_state, cache_swa_k, cache_swa_v, meta_tokens, norm1_g, w_in, q_norm_g,
           k_norm_g, ret_norm_g, attn_sinks, w_out, norm2_g, w_group, w_expert, w_gate, w_up, w_down):
    assert norm1_g.shape[0] == 1, "single-layer trunk"
    bp, lp, _ = x_prompt.shape
    bs, ls, _ = x_sample.shape
    n_pairs = RET_HEADS // 2

    g1 = norm1_g[0][None, :]
    g2 = norm2_g[0][None, :]
    w_in_bf = w_in[0].astype(BF16)
    w_out_bf = w_out[0].astype(BF16)
    qg2 = jnp.tile(q_norm_g[0], 2)[None, :]
    kg2 = jnp.tile(k_norm_g[0], 2)[None, :]
    rng = ret_norm_g[0].reshape(RET_HEADS, RET_DV)
    sink_tab = jnp.broadcast_to(jnp.repeat(attn_sinks[0], CHUNK).reshape(SWA_KV_HEADS, 1, 4 * CHUNK),
                                (SWA_KV_HEADS, SUBLANES, 4 * CHUNK))
    w_router_t = jnp.zeros((ROUTER_ROWS, D_MODEL), F32)
    w_router_t = w_router_t.at[0:N_GROUPS].set(w_group[0].T)
    w_router_t = w_router_t.at[ROUTER_EXPERT_ROW0:ROUTER_EXPERT_ROW0 + N_EXPERTS].set(w_expert[0].T)
    w_router_t_bf = w_router_t.astype(BF16)

    meta_rows = 2 * CHUNK
    m_pad = jnp.zeros((meta_rows, D_MODEL), F32).at[0:N_META].set(meta_tokens)
    m_rqk, m_rv, _, _, m_skv = _proj(m_pad, jnp.arange(meta_rows, dtype=jnp.int32), meta_rows, g1, w_in_bf, qg2, kg2)
    s_meta = _meta_state(m_rqk, m_rv)[None]
    meta_kv = m_skv[0:N_META]

    groups = [
        dict(x=x_prompt.reshape(bp * lp, D_MODEL), n=bp, seq=lp, pos0=N_META, has_hist=False, s0=s_meta,
             hist=jnp.zeros((1, WINDOW, 2 * SWA_KV), F32)),
        dict(x=x_sample.reshape(bs * ls, D_MODEL), n=bs, seq=ls, pos0=N_META + PAST_LEN, has_hist=True,
             s0=cache_ret_state[0].reshape(bs, n_pairs, 2 * RET_DK, RET_DV),
             hist=jnp.concatenate([cache_swa_k[0].reshape(bs, WINDOW, SWA_KV),
                                   cache_swa_v[0].reshape(bs, WINDOW, SWA_KV)], axis=-1)),
    ]

    base_cnt = jnp.zeros((N_EXPERTS, LANES), F32)
    for g in groups:
        rows = g["n"] * g["seq"]
        tm = _tile_for(rows, PROJ_TILE)
        pos = g["pos0"] + jnp.arange(g["seq"], dtype=jnp.int32)
        if g["seq"] < tm:
            assert tm % g["seq"] == 0
            pos = jnp.tile(pos, tm // g["seq"])
        else:
            assert g["seq"] % tm == 0
        rqk, rv, gate, sq, skv = _proj(g["x"], pos, tm, g1, w_in_bf, qg2, kg2)
        tl = min(ATTN_TILE, g["seq"])
        omix, s_out, kv_out = _attention(rqk, rv, gate, sq, skv, meta_kv, g["hist"], g["s0"], rng, sink_tab,
                                         n_streams=g["n"], seq=g["seq"], tl=tl, has_hist=g["has_hist"])
        xmid, wcol, route_t, base_cnt = _post(omix, g["x"], w_out_bf, g2, w_router_t_bf, base_cnt,
                                              _tile_for(rows, POST_TILE))
        g.update(xmid=xmid, wcol=wcol, route_t=route_t, s_out=s_out, kv_out=kv_out)

    te = EXPERT_TILE
    total_rows = sum(g["n"] * g["seq"] for g in groups)
    assert (2 * total_rows) % te == 0
    n_row_tiles = (2 * total_rows) // te
    counts = base_cnt[:, 0].astype(jnp.int32)
    off = jnp.cumsum(counts) - counts
    first_tile = off // te
    n_vis_e = jnp.where(counts > 0, (off + counts - 1) // te - first_tile + 1, 0)
    vis_end = jnp.cumsum(n_vis_e)
    n_vis = vis_end[-1:].astype(jnp.int32)
    v = jnp.arange(n_row_tiles + N_EXPERTS, dtype=jnp.int32)
    vis_expert = _bucket(vis_end, v)
    pick = lambda table: jnp.sum(jnp.where(vis_expert[:, None] == jnp.arange(N_EXPERTS, dtype=jnp.int32)[None, :],
                                           table[None, :], 0), axis=1)
    vis_tile = jnp.clip(pick(first_tile) + v - pick(vis_end - n_vis_e), 0, n_row_tiles - 1).astype(jnp.int32)
    vis_lo = jnp.clip(pick(off) - vis_tile * te, 0, te).astype(jnp.int32)
    vis_hi = jnp.clip(pick(off + counts) - vis_tile * te, 0, te).astype(jnp.int32)

    for g in groups:
        eid = g["route_t"][0:2].astype(jnp.int32)
        off_sel = jnp.sum(jnp.where(eid[None] == jnp.arange(N_EXPERTS, dtype=jnp.int32)[:, None, None],
                                    off[:, None, None], 0), axis=0)
        g["pos"] = (off_sel + g["route_t"][4:6].astype(jnp.int32)).astype(jnp.int32)
    xs = _dispatch([g["pos"] for g in groups], [g["xmid"] for g in groups], [g["wcol"] for g in groups], g2,
                   _tile_for(min(g["n"] * g["seq"] for g in groups), MOVE_TILE))

    y2 = _experts(vis_tile, vis_expert.astype(jnp.int32), vis_lo, vis_hi, n_vis, xs, w_gate[0], w_up[0], w_down[0], te)
    y2 = y2.reshape(total_rows, 2 * D_MODEL)

    outs, row0 = [], 0
    for g in groups:
        rows = g["n"] * g["seq"]
        y = _combine(y2, g["xmid"], g["wcol"], row0, _tile_for(rows, MOVE_TILE))
        outs.append(y.reshape(g["n"], g["seq"], D_MODEL))
        row0 += rows

    def caches(g):
        kv = g["kv_out"]
        k = kv[:, :, 0:SWA_KV].reshape(g["n"], WINDOW, SWA_KV_HEADS, SWA_HD)[None]
        v = kv[:, :, SWA_KV:2 * SWA_KV].reshape(g["n"], WINDOW, SWA_KV_HEADS, SWA_HD)[None]
        s = g["s_out"].reshape(g["n"], RET_HEADS, RET_DK, RET_DV)[None]
        return s, k, v

    sp, kp, vp = caches(groups[0])
    ss, ks, vs = caches(groups[1])
    return (outs[0], outs[1], sp, kp, vp, ss, ks, vs)
```

```python
import functools

import numpy as np
import jax
import jax.numpy as jnp
from jax import lax
from jax.experimental import pallas as pl
from jax.experimental.pallas import tpu as pltpu

F32 = jnp.float32
BF16 = jnp.bfloat16

D_MODEL = 1024
PAST_LEN = 4096
CHUNK = 64
N_META = 16
RET_HEADS = 4
RET_DK = 64
RET_DV = 128
SWA_HEADS = 8
SWA_KV_HEADS = 2
SWA_HD = 64
WINDOW = 128
ROPE_THETA = 10000.0
N_GROUPS = 4
EXPERTS_PER_GROUP = 8
N_EXPERTS = N_GROUPS * EXPERTS_PER_GROUP
EXPERT_FF = 256
EPS = 1e-6
NEG_INF = -1e30
LOG2E = float(np.log2(np.e))
RET_Q = RET_HEADS * RET_DK
RET_V = RET_HEADS * RET_DV
SWA_Q = SWA_HEADS * SWA_HD
SWA_KV = SWA_KV_HEADS * SWA_HD
MIX_WIDTH = RET_V + SWA_Q
IN_WIDTH = 2 * RET_Q + 2 * RET_V + SWA_Q + 2 * SWA_KV

LANES = 128
PROJ_TILE = 1024
POST_TILE = 1024
ATTN_TILE = 256
EXPERT_TILE = 1024
MOVE_TILE = 512
SUBLANES = 8
ROW_WIDTH = D_MODEL + LANES
TOKEN_LANE = 4
ROUTER_EXPERT_ROW0 = 8
ROUTER_ROWS = 64
META_ROWS = 64
VMEM_LIMIT = 56 * 1024 * 1024

_LOG_G = [float(np.log1p(-np.exp2(-5.0 - h))) for h in range(RET_HEADS)]


def _params(n_axes):
    return pltpu.CompilerParams(dimension_semantics=("arbitrary",) * n_axes, vmem_limit_bytes=VMEM_LIMIT)


def _split_bf16(a):
    hi = a.astype(BF16)
    return hi, (a - hi.astype(F32)).astype(BF16)


def _split_dot(a, w2):
    hi, lo = _split_bf16(a)
    return jnp.dot(jnp.concatenate([hi, lo], axis=1), w2, preferred_element_type=F32)


def _lane_sum(a):
    return _split_dot(a, jnp.ones((2 * LANES, LANES), BF16))


def _head_sum_matrix():
    i = lax.broadcasted_iota(jnp.int32, (2 * LANES, LANES), 0) % LANES
    j = lax.broadcasted_iota(jnp.int32, (2 * LANES, LANES), 1)
    return jnp.where((i < SWA_HD) == (j < SWA_HD), 1.0, 0.0).astype(BF16)


def _rope(t, c, s1, s2):
    half = SWA_HD // 2
    return t * c + pltpu.roll(t, LANES - half, 1) * s1 + pltpu.roll(t, half, 1) * s2


def _head_rms(t, g, head_w):
    ms = _split_dot(t * t, head_w) * (1.0 / SWA_HD)
    return t * lax.rsqrt(ms + EPS) * g


def _row_rms(x):
    n_tiles = x.shape[1] // LANES
    ss = x[:, 0:LANES] * x[:, 0:LANES]
    for j in range(1, n_tiles):
        ss = ss + x[:, j * LANES:(j + 1) * LANES] * x[:, j * LANES:(j + 1) * LANES]
    r = lax.rsqrt(_lane_sum(ss) * (1.0 / x.shape[1]) + EPS)
    return x * jnp.concatenate([r] * n_tiles, axis=1)


def _proj_kernel(x_ref, g1_ref, w_ref, qg_ref, kg_ref, cos_ref, s1_ref, s2_ref,
                 rqk_ref, rv_ref, gate_ref, sq_ref, skv_ref):
    xn = (_row_rms(x_ref[...]) * g1_ref[...]).astype(BF16)
    c, s1, s2 = cos_ref[...], s1_ref[...], s2_ref[...]
    head_w = _head_sum_matrix()

    def seg(a, b):
        return jnp.dot(xn, w_ref[:, a:b], preferred_element_type=F32)

    def tile(h, j):
        return h[:, j * LANES:(j + 1) * LANES]

    h = seg(0, 2 * RET_Q)
    for j in range(2):
        rqk_ref[:, j * LANES:(j + 1) * LANES] = _rope(tile(h, j), c, s1, s2).astype(BF16)
    for j in range(2, 4):
        rqk_ref[:, j * LANES:(j + 1) * LANES] = (_rope(tile(h, j), c, s1, s2) * (RET_DK ** -0.5)).astype(BF16)
    a = 2 * RET_Q
    rv_ref[...] = seg(a, a + RET_V).astype(BF16)
    a += RET_V
    g = seg(a, a + RET_V)
    gate_ref[...] = (g * jax.nn.sigmoid(g)).astype(BF16)
    a += RET_V
    h = seg(a, a + SWA_Q)
    qg = qg_ref[...]
    for j in range(SWA_Q // LANES):
        sq_ref[:, j * LANES:(j + 1) * LANES] = _rope(_head_rms(tile(h, j), qg, head_w), c, s1, s2).astype(BF16)
    a += SWA_Q
    h = seg(a, a + 2 * SWA_KV)
    skv_ref[:, 0:LANES] = _rope(_head_rms(tile(h, 0), kg_ref[...], head_w), c, s1, s2)
    skv_ref[:, LANES:2 * LANES] = tile(h, 1)


def _rope_tables(pos):
    half = SWA_HD // 2
    inv = ROPE_THETA ** (-jnp.arange(half, dtype=F32) / half)
    ang = pos.astype(F32)[:, None] * inv[None, :]
    cos, sin = jnp.cos(ang), jnp.sin(ang)
    z = jnp.zeros_like(sin)
    heads_per_tile = LANES // SWA_HD
    return (jnp.tile(cos, (1, 2 * heads_per_tile)),
            jnp.tile(jnp.concatenate([-sin, z], axis=1), (1, heads_per_tile)),
            jnp.tile(jnp.concatenate([z, sin], axis=1), (1, heads_per_tile)))


def _proj(x2d, pos_rows, tm, g1, w_in_bf, qg2, kg2):
    t_rows = x2d.shape[0]
    n_tiles = t_rows // tm
    n_pos_tiles = pos_rows.shape[0] // tm
    cos, s1, s2 = _rope_tables(pos_rows)
    row = lambda i: (i, 0)
    const = lambda i: (0, 0)
    tab = lambda i: (i % n_pos_tiles, 0)
    return pl.pallas_call(
        _proj_kernel,
        grid=(n_tiles,),
        in_specs=[pl.BlockSpec((tm, D_MODEL), row),
                  pl.BlockSpec((1, D_MODEL), const),
                  pl.BlockSpec((D_MODEL, IN_WIDTH), const),
                  pl.BlockSpec((1, LANES), const),
                  pl.BlockSpec((1, LANES), const),
                  pl.BlockSpec((tm, LANES), tab),
                  pl.BlockSpec((tm, LANES), tab),
                  pl.BlockSpec((tm, LANES), tab)],
        out_specs=[pl.BlockSpec((tm, 2 * RET_Q), row),
                   pl.BlockSpec((tm, RET_V), row),
                   pl.BlockSpec((tm, RET_V), row),
                   pl.BlockSpec((tm, SWA_Q), row),
                   pl.BlockSpec((tm, 2 * SWA_KV), row)],
        out_shape=[jax.ShapeDtypeStruct((t_rows, 2 * RET_Q), BF16),
                   jax.ShapeDtypeStruct((t_rows, RET_V), BF16),
                   jax.ShapeDtypeStruct((t_rows, RET_V), BF16),
                   jax.ShapeDtypeStruct((t_rows, SWA_Q), BF16),
                   jax.ShapeDtypeStruct((t_rows, 2 * SWA_KV), F32)],
        compiler_params=_params(1),
        name="proj",
    )(x2d, g1, w_in_bf, qg2, kg2, cos, s1, s2)


def _pair_update(k_bf, v0_bf, v1_bf, wt):
    kw = (k_bf.astype(F32) * wt).astype(BF16)
    dn = (((0,), (0,)), ((), ()))
    a0 = lax.dot_general(kw, v0_bf, dn, preferred_element_type=F32)
    a1 = lax.dot_general(kw, v1_bf, dn, preferred_element_type=F32)
    top = lax.broadcasted_iota(jnp.int32, a0.shape, 0) < RET_DK
    return jnp.where(top, a0, a1)


def _decay_rows(n, pair, rows_back_from):
    i = lax.broadcasted_iota(jnp.int32, (n, LANES), 0).astype(F32)
    lane = lax.broadcasted_iota(jnp.int32, (n, LANES), 1)
    lg = jnp.where(lane < RET_DK, _LOG_G[2 * pair], _LOG_G[2 * pair + 1])
    return jnp.exp((rows_back_from - i) * lg)


def _meta_state_kernel(rqk_ref, rv_ref, s_ref, *, n_rows):
    for p in range(RET_HEADS // 2):
        k = rqk_ref[:, RET_Q + p * LANES:RET_Q + (p + 1) * LANES]
        wt = _decay_rows(n_rows, p, float(N_META - 1))
        s_ref[p] = _pair_update(k, rv_ref[:, (2 * p) * LANES:(2 * p + 1) * LANES],
                                rv_ref[:, (2 * p + 1) * LANES:(2 * p + 2) * LANES], wt)


def _meta_state(m_rqk, m_rv):
    n_rows = m_rqk.shape[0]
    return pl.pallas_call(
        functools.partial(_meta_state_kernel, n_rows=n_rows),
        out_shape=jax.ShapeDtypeStruct((RET_HEADS // 2, 2 * RET_DK, RET_DV), F32),
        name="meta_state",
    )(m_rqk, m_rv)


def _dup_halves(a, lo_mask):
    sw = pltpu.roll(a, SWA_HD, 1)
    return jnp.where(lo_mask, a, sw), jnp.where(lo_mask, sw, a)


def _attn_kernel(rqk_ref, rv_ref, gate_ref, sq_ref, skv_ref, meta_ref, hist_ref, s0_ref, rng_ref, sink_ref,
                 omix_ref, sout_ref, kvout_ref,
                 s_scr, kd_scr, vd_scr, mk_scr, mv_scr, dec_scr, wt_scr, cs_scr, gam_scr,
                 *, tl, has_hist):
    b = pl.program_id(0)
    t = pl.program_id(1)
    nt = pl.num_programs(1)
    n_chunks = tl // CHUNK
    n_pairs = RET_HEADS // 2
    lo_tl = lax.broadcasted_iota(jnp.int32, (tl, LANES), 1) < SWA_HD
    lo_c = lax.broadcasted_iota(jnp.int32, (CHUNK, LANES), 1) < SWA_HD

    @pl.when((b == 0) & (t == 0))
    def _tables():
        i = lax.broadcasted_iota(jnp.int32, (tl, tl), 0)
        j = lax.broadcasted_iota(jnp.int32, (tl, tl), 1)
        diff = (i - j).astype(F32)
        row = lax.broadcasted_iota(jnp.int32, (tl, LANES), 0).astype(F32)
        for h in range(RET_HEADS):
            dec_scr[h] = jnp.where(diff >= 0.0, jnp.exp(jnp.maximum(diff, 0.0) * _LOG_G[h]), 0.0)
            cs_scr[h] = jnp.exp((row + 1.0) * _LOG_G[h])
        top = lax.broadcasted_iota(jnp.int32, (2 * RET_DK, RET_DV), 0) < RET_DK
        for p in range(n_pairs):
            wt_scr[p] = _decay_rows(tl, p, float(tl - 1))
            gam_scr[p] = jnp.where(top, jnp.exp(jnp.float32(tl * _LOG_G[2 * p])), jnp.exp(jnp.float32(tl * _LOG_G[2 * p + 1])))
        lo_m = lax.broadcasted_iota(jnp.int32, (N_META, LANES), 1) < SWA_HD
        mk0, mk1 = _dup_halves(meta_ref[:, 0:LANES], lo_m)
        mv0, mv1 = _dup_halves(meta_ref[:, LANES:2 * LANES], lo_m)
        mk_scr[...] = jnp.zeros(mk_scr.shape, BF16)
        mv_scr[...] = jnp.zeros(mv_scr.shape, BF16)
        mk_scr[0, 0:N_META] = mk0.astype(BF16)
        mk_scr[1, 0:N_META] = mk1.astype(BF16)
        mv_scr[0, 0:N_META] = mv0.astype(BF16)
        mv_scr[1, 0:N_META] = mv1.astype(BF16)

    @pl.when(t == 0)
    def _stream_start():
        s_scr[...] = s0_ref[0]
        if has_hist:
            lo_w = lax.broadcasted_iota(jnp.int32, (WINDOW, LANES), 1) < SWA_HD
            k0, k1 = _dup_halves(hist_ref[0, :, 0:LANES], lo_w)
            v0, v1 = _dup_halves(hist_ref[0, :, LANES:2 * LANES], lo_w)
            kd_scr[0, 0:WINDOW] = k0.astype(BF16)
            kd_scr[1, 0:WINDOW] = k1.astype(BF16)
            vd_scr[0, 0:WINDOW] = v0.astype(BF16)
            vd_scr[1, 0:WINDOW] = v1.astype(BF16)
        else:
            z = jnp.zeros((WINDOW, LANES), BF16)
            for kv in range(SWA_KV_HEADS):
                kd_scr[kv, 0:WINDOW] = z
                vd_scr[kv, 0:WINDOW] = z

    k0, k1 = _dup_halves(skv_ref[:, 0:LANES], lo_tl)
    v0, v1 = _dup_halves(skv_ref[:, LANES:2 * LANES], lo_tl)
    kd_scr[0, WINDOW:WINDOW + tl] = k0.astype(BF16)
    kd_scr[1, WINDOW:WINDOW + tl] = k1.astype(BF16)
    vd_scr[0, WINDOW:WINDOW + tl] = v0.astype(BF16)
    vd_scr[1, WINDOW:WINDOW + tl] = v1.astype(BF16)

    band = WINDOW + CHUNK
    n_keys = META_ROWS + band
    n_q = 4 * CHUNK
    scale2 = (SWA_HD ** -0.5) * LOG2E
    krow = lax.broadcasted_iota(jnp.int32, (n_keys, n_q), 0)
    zero_c = jnp.zeros((CHUNK, LANES), BF16)
    ones_v = jnp.ones((n_keys, LANES), BF16)
    for c in range(n_chunks):
        if has_hist:
            first_valid = META_ROWS
        else:
            first_valid = jnp.where(t == 0, max(META_ROWS + WINDOW - c * CHUNK, META_ROWS), META_ROWS)
        valid_t = (krow < N_META) | (krow >= first_valid)
        r0 = c * CHUNK
        for kv in range(SWA_KV_HEADS):
            keys = jnp.concatenate([mk_scr[kv], kd_scr[kv, r0:r0 + band]], axis=0)
            vals = jnp.concatenate([mv_scr[kv], vd_scr[kv, r0:r0 + band]], axis=0)
            qa = sq_ref[r0:r0 + CHUNK, (2 * kv) * LANES:(2 * kv + 1) * LANES]
            qb = sq_ref[r0:r0 + CHUNK, (2 * kv + 1) * LANES:(2 * kv + 2) * LANES]
            lhs = jnp.concatenate([jnp.where(lo_c, qa, zero_c), jnp.where(lo_c, zero_c, qa),
                                   jnp.where(lo_c, qb, zero_c), jnp.where(lo_c, zero_c, qb)], axis=0)
            s_t = lax.dot_general(keys, lhs, (((1,), (1,)), ((), ())), preferred_element_type=F32) * scale2
            s_t = jnp.where(valid_t, s_t, NEG_INF)
            s_t = jnp.where(krow == N_META, sink_ref[kv, 0:1, :] * LOG2E, s_t)
            e_t = jnp.exp2(s_t - jnp.max(s_t, axis=0, keepdims=True)).astype(BF16)
            ov = lax.dot_general(e_t, jnp.concatenate([vals, ones_v], axis=1), (((0,), (0,)), ((), ())),
                                 preferred_element_type=F32)
            o = ov[:, 0:LANES] * (1.0 / ov[:, LANES:2 * LANES])
            oa = jnp.where(lo_c, o[0:CHUNK], o[CHUNK:2 * CHUNK])
            ob = jnp.where(lo_c, o[2 * CHUNK:3 * CHUNK], o[3 * CHUNK:4 * CHUNK])
            base = RET_V + (2 * kv) * LANES
            omix_ref[r0:r0 + CHUNK, base:base + LANES] = oa.astype(BF16)
            omix_ref[r0:r0 + CHUNK, base + LANES:base + 2 * LANES] = ob.astype(BF16)

    zero_t = jnp.zeros((tl, LANES), BF16)
    for p in range(n_pairs):
        q = rqk_ref[:, p * LANES:(p + 1) * LANES]
        k = rqk_ref[:, RET_Q + p * LANES:RET_Q + (p + 1) * LANES]
        lhs = jnp.concatenate([jnp.where(lo_tl, q, zero_t), jnp.where(lo_tl, zero_t, q)], axis=0)
        s = lax.dot_general(lhs, k, (((1,), (1,)), ((), ())), preferred_element_type=F32)
        cross = jnp.dot(lhs, s_scr[p].astype(BF16), preferred_element_type=F32)
        for i in range(2):
            h = 2 * p + i
            v = rv_ref[:, h * LANES:(h + 1) * LANES]
            a = (s[i * tl:(i + 1) * tl] * dec_scr[h]).astype(BF16)
            o = jnp.dot(a, v, preferred_element_type=F32) + cross[i * tl:(i + 1) * tl] * cs_scr[h]
            r = o * lax.rsqrt(_lane_sum(o * o) * (1.0 / RET_DV) + EPS) * rng_ref[h:h + 1, :]
            omix_ref[:, h * LANES:(h + 1) * LANES] = (r * gate_ref[:, h * LANES:(h + 1) * LANES].astype(F32)).astype(BF16)
        u = _pair_update(k, rv_ref[:, (2 * p) * LANES:(2 * p + 1) * LANES],
                         rv_ref[:, (2 * p + 1) * LANES:(2 * p + 2) * LANES], wt_scr[p])
        s_scr[p] = gam_scr[p] * s_scr[p] + u

    if tl >= WINDOW:
        @pl.when(t + 1 < nt)
        def _carry_window():
            for kv in range(SWA_KV_HEADS):
                kd_scr[kv, 0:WINDOW] = kd_scr[kv, tl:tl + WINDOW]
                vd_scr[kv, 0:WINDOW] = vd_scr[kv, tl:tl + WINDOW]

    @pl.when(t + 1 == nt)
    def _stream_end():
        sout_ref[0] = s_scr[...]
        if tl >= WINDOW:
            kvout_ref[0] = skv_ref[tl - WINDOW:tl, :]
        else:
            kvout_ref[0, 0:WINDOW - tl] = hist_ref[0, tl:WINDOW, :]
            kvout_ref[0, WINDOW - tl:WINDOW] = skv_ref[...]


def _attention(rqk, rv, gate, sq, skv, meta_kv, hist_kv, s0, rng, sink_tab, *, n_streams, seq, tl, has_hist):
    nt = seq // tl
    assert tl % CHUNK == 0 and seq % tl == 0
    assert tl >= WINDOW or (nt == 1 and has_hist)
    n_pairs = RET_HEADS // 2
    s0_shared = s0.shape[0] == 1
    row = lambda b, t: (b * nt + t, 0)
    const2 = lambda b, t: (0, 0)
    const3 = lambda b, t: (0, 0, 0)
    per_b3 = lambda b, t: (b, 0, 0)
    s0_map = (lambda b, t: (0, 0, 0, 0)) if s0_shared else (lambda b, t: (b, 0, 0, 0))
    hist_map = per_b3 if has_hist else const3
    rows = n_streams * seq
    return pl.pallas_call(
        functools.partial(_attn_kernel, tl=tl, has_hist=has_hist),
        grid=(n_streams, nt),
        in_specs=[pl.BlockSpec((tl, 2 * RET_Q), row),
                  pl.BlockSpec((tl, RET_V), row),
                  pl.BlockSpec((tl, RET_V), row),
                  pl.BlockSpec((tl, SWA_Q), row),
                  pl.BlockSpec((tl, 2 * SWA_KV), row),
                  pl.BlockSpec((N_META, 2 * SWA_KV), const2),
                  pl.BlockSpec((1, WINDOW, 2 * SWA_KV), hist_map),
                  pl.BlockSpec((1, n_pairs, 2 * RET_DK, RET_DV), s0_map),
                  pl.BlockSpec((RET_HEADS, RET_DV), const2),
                  pl.BlockSpec((SWA_KV_HEADS, SUBLANES, 4 * CHUNK), const3)],
        out_specs=[pl.BlockSpec((tl, MIX_WIDTH), row),
                   pl.BlockSpec((1, n_pairs, 2 * RET_DK, RET_DV), lambda b, t: (b, 0, 0, 0)),
                   pl.BlockSpec((1, WINDOW, 2 * SWA_KV), per_b3)],
        out_shape=[jax.ShapeDtypeStruct((rows, MIX_WIDTH), BF16),
                   jax.ShapeDtypeStruct((n_streams, n_pairs, 2 * RET_DK, RET_DV), F32),
                   jax.ShapeDtypeStruct((n_streams, WINDOW, 2 * SWA_KV), F32)],
        scratch_shapes=[pltpu.VMEM((n_pairs, 2 * RET_DK, RET_DV), F32),
                        pltpu.VMEM((SWA_KV_HEADS, WINDOW + tl, LANES), BF16),
                        pltpu.VMEM((SWA_KV_HEADS, WINDOW + tl, LANES), BF16),
                        pltpu.VMEM((SWA_KV_HEADS, META_ROWS, LANES), BF16),
                        pltpu.VMEM((SWA_KV_HEADS, META_ROWS, LANES), BF16),
                        pltpu.VMEM((RET_HEADS, tl, tl), F32),
                        pltpu.VMEM((n_pairs, tl, LANES), F32),
                        pltpu.VMEM((RET_HEADS, tl, RET_DV), F32),
                        pltpu.VMEM((n_pairs, 2 * RET_DK, RET_DV), F32)],
        compiler_params=_params(2),
        name="attention",
    )(rqk, rv, gate, sq, skv, meta_kv, hist_kv, s0, rng, sink_tab)


def _post_kernel(omix_ref, x_ref, wout_ref, g2_ref, wrt_ref, base_ref,
                 xmid_ref, wcol_ref, rt_ref, cnt_ref, tri_scr, run_scr):
    i = pl.program_id(0)
    tm = x_ref.shape[0]

    @pl.when(i == 0)
    def _init():
        r = lax.broadcasted_iota(jnp.int32, (tm, tm), 0)
        c = lax.broadcasted_iota(jnp.int32, (tm, tm), 1)
        tri_scr[...] = jnp.where(r < c, 1.0, 0.0).astype(BF16)
        run_scr[...] = base_ref[...]

    xm = x_ref[...] + jnp.dot(omix_ref[...], wout_ref[...], preferred_element_type=F32)
    xmid_ref[...] = xm
    hn = _row_rms(xm) * g2_ref[...]
    lt = lax.dot_general(wrt_ref[...], hn.astype(BF16), (((1,), (1,)), ((), ())), preferred_element_type=F32)
    row8 = lax.broadcasted_iota(jnp.int32, (SUBLANES, tm), 0)
    big = jnp.int32(SUBLANES)
    gl = jnp.where(row8 < N_GROUPS, lt[0:SUBLANES], NEG_INF)
    gmax = jnp.max(gl, axis=0, keepdims=True)
    gsum = jnp.sum(jnp.exp(gl - gmax), axis=0, keepdims=True)
    g_sel = jnp.min(jnp.where(gl == gmax, row8, big), axis=0, keepdims=True)
    p_sel = 1.0 / gsum
    el = lt[ROUTER_EXPERT_ROW0:ROUTER_EXPERT_ROW0 + EXPERTS_PER_GROUP]
    for g in range(1, N_GROUPS):
        lo = ROUTER_EXPERT_ROW0 + g * EXPERTS_PER_GROUP
        el = jnp.where(g_sel == g, lt[lo:lo + EXPERTS_PER_GROUP], el)
    m1 = jnp.max(el, axis=0, keepdims=True)
    i1 = jnp.min(jnp.where(el == m1, row8, big), axis=0, keepdims=True)
    el2 = jnp.where(row8 == i1, NEG_INF, el)
    m2 = jnp.max(el2, axis=0, keepdims=True)
    i2 = jnp.min(jnp.where(el2 == m2, row8, big), axis=0, keepdims=True)
    e2 = jnp.exp(m2 - m1)
    inv = 1.0 / (1.0 + e2)
    w1 = p_sel * inv
    w2 = p_sel * (e2 * inv)
    eid1 = g_sel * EXPERTS_PER_GROUP + i1
    eid2 = g_sel * EXPERTS_PER_GROUP + i2

    rowe = lax.broadcasted_iota(jnp.int32, (N_EXPERTS, tm), 0)
    oh = jnp.where((rowe == eid1) | (rowe == eid2), 1.0, 0.0).astype(BF16)
    run = run_scr[...]
    pref = jnp.dot(oh, tri_scr[...], preferred_element_type=F32) + jnp.concatenate([run] * (tm // LANES), axis=1)
    r1 = jnp.sum(jnp.where(rowe == eid1, pref, 0.0), axis=0, keepdims=True)
    r2 = jnp.sum(jnp.where(rowe == eid2, pref, 0.0), axis=0, keepdims=True)
    run = run + jnp.dot(oh, jnp.ones((tm, LANES), BF16), preferred_element_type=F32)
    run_scr[...] = run
    cnt_ref[...] = run

    out = jnp.zeros((SUBLANES, tm), F32)
    for k, v in enumerate([eid1.astype(F32), eid2.astype(F32), w1, w2, r1, r2]):
        out = jnp.where(row8 == k, v, out)
    rt_ref[...] = out
    rowl = lax.broadcasted_iota(jnp.int32, (LANES, tm), 0)
    rec = jnp.zeros((LANES, tm), F32)
    for k, v in enumerate([w1, w2, eid1.astype(F32), eid2.astype(F32)]):
        rec = jnp.where(rowl == k, v, rec)
    wcol_ref[...] = rec.T


def _post(omix, x2d, w_out_bf, g2, w_router_t_bf, base_cnt, tm):
    t_rows = x2d.shape[0]
    assert tm % LANES == 0
    row = lambda i: (i, 0)
    const = lambda i: (0, 0)
    return pl.pallas_call(
        _post_kernel,
        grid=(t_rows // tm,),
        in_specs=[pl.BlockSpec((tm, MIX_WIDTH), row),
                  pl.BlockSpec((tm, D_MODEL), row),
                  pl.BlockSpec((MIX_WIDTH, D_MODEL), const),
                  pl.BlockSpec((1, D_MODEL), const),
                  pl.BlockSpec((ROUTER_ROWS, D_MODEL), const),
                  pl.BlockSpec((N_EXPERTS, LANES), const)],
        out_specs=[pl.BlockSpec((tm, D_MODEL), row),
                   pl.BlockSpec((tm, LANES), row),
                   pl.BlockSpec((SUBLANES, tm), lambda i: (0, i)),
                   pl.BlockSpec((N_EXPERTS, LANES), const)],
        out_shape=[jax.ShapeDtypeStruct((t_rows, D_MODEL), F32),
                   jax.ShapeDtypeStruct((t_rows, LANES), F32),
                   jax.ShapeDtypeStruct((SUBLANES, t_rows), F32),
                   jax.ShapeDtypeStruct((N_EXPERTS, LANES), F32)],
        scratch_shapes=[pltpu.VMEM((tm, tm), BF16), pltpu.VMEM((N_EXPERTS, LANES), F32)],
        compiler_params=_params(1),
        name="post",
    )(omix, x2d, w_out_bf, g2, w_router_t_bf, base_cnt)


def _step_major(pos, tm):
    return [pos[e].reshape(-1, 1, tm) for e in range(2)]


def _dispatch_kernel(pos0_ref, pos1_ref, *refs, tm, group_steps):
    pos_refs = (pos0_ref, pos1_ref)
    n_g = len(group_steps)
    xmid_refs, rec_refs = refs[:n_g], refs[n_g:2 * n_g]
    g2_ref, xs_ref, hbuf, sems = refs[2 * n_g:]
    n_steps = sum(group_steps)
    i = pl.program_id(0)
    slot = i % 2

    first = 0
    for xmid_ref, rec_ref, steps in zip(xmid_refs, rec_refs, group_steps):
        @pl.when((i >= first) & (i < first + steps))
        def _normalise(xmid_ref=xmid_ref, rec_ref=rec_ref, first=first):
            hn = _row_rms(xmid_ref[...]) * g2_ref[...]
            lane = lax.broadcasted_iota(jnp.int32, (tm, LANES), 1)
            tok = (first * tm + (i - first) * tm + lax.broadcasted_iota(jnp.int32, (tm, LANES), 0)).astype(F32)
            rec = jnp.where(lane == TOKEN_LANE, tok, rec_ref[...])
            hbuf[slot] = jnp.concatenate([hn, rec], axis=1).reshape(tm // SUBLANES, SUBLANES, ROW_WIDTH)
        first += steps

    def body(blk, carry):
        for k in range(SUBLANES):
            r = blk * SUBLANES + k
            for e in range(2):
                pltpu.make_async_copy(hbuf.at[slot, blk, pl.ds(k, 1)], xs_ref.at[pl.ds(pos_refs[e][0, 0, r], 1)],
                                      sems.at[slot]).start(priority=e)
        return carry

    lax.fori_loop(0, tm // SUBLANES, body, 0)

    def drain(which):
        pltpu.make_async_copy(xs_ref.at[pl.ds(0, 2 * tm)], xs_ref.at[pl.ds(0, 2 * tm)], sems.at[which]).wait()

    @pl.when(i > 0)
    def _previous():
        drain(1 - slot)

    @pl.when(i == n_steps - 1)
    def _last():
        drain(slot)


def _dispatch(pos_list, xmid_list, rec_list, g2, tm):
    group_steps = tuple(x.shape[0] // tm for x in xmid_list)
    n_steps = sum(group_steps)
    assert tm % SUBLANES == 0
    pos3 = [jnp.concatenate(parts, axis=0) for parts in zip(*[_step_major(p, tm) for p in pos_list])]
    smem = pl.BlockSpec((1, 1, tm), lambda i: (i, 0, 0), memory_space=pltpu.SMEM)
    x_specs, rec_specs = [], []
    first = 0
    for steps in group_steps:
        held = lambda i, first=first, steps=steps: (jnp.clip(i - first, 0, steps - 1), 0)
        x_specs.append(pl.BlockSpec((tm, D_MODEL), held))
        rec_specs.append(pl.BlockSpec((tm, LANES), held))
        first += steps
    n_rows_out = 2 * sum(x.shape[0] for x in xmid_list)
    return pl.pallas_call(
        functools.partial(_dispatch_kernel, tm=tm, group_steps=group_steps),
        grid=(n_steps,),
        in_specs=[smem, smem] + x_specs + rec_specs + [pl.BlockSpec((1, D_MODEL), lambda i: (0, 0))],
        out_specs=pl.BlockSpec(memory_space=pl.ANY),
        out_shape=jax.ShapeDtypeStruct((n_rows_out, ROW_WIDTH), F32),
        scratch_shapes=[pltpu.VMEM((2, tm // SUBLANES, SUBLANES, ROW_WIDTH), F32), pltpu.SemaphoreType.DMA((2,))],
        compiler_params=_params(1),
        name="dispatch",
    )(*pos3, *xmid_list, *rec_list, g2)


def _expert_kernel(vt_ref, ve_ref, lo_ref, hi_ref, nv_ref, x_ref, wg_ref, wu_ref, wd_ref, y2_ref,
                   wgu_scr, wd_scr, ybuf, dest_vmem, dest_smem, sems, dest_sem, *, tm, n_tok):
    v = pl.program_id(0)
    n_vis = nv_ref[0]
    slot = v % 2

    def drain(which, n_rows):
        def eight(_, carry):
            pltpu.make_async_copy(y2_ref.at[pl.ds(0, SUBLANES)], y2_ref.at[pl.ds(0, SUBLANES)], sems.at[which]).wait()
            return carry

        def one(_, carry):
            pltpu.make_async_copy(y2_ref.at[pl.ds(0, 1)], y2_ref.at[pl.ds(0, 1)], sems.at[which]).wait()
            return carry

        lax.fori_loop(0, n_rows // SUBLANES, eight, 0)
        lax.fori_loop(0, n_rows % SUBLANES, one, 0)

    @pl.when(v < n_vis)
    def _compute():
        @pl.when((v == 0) | (ve_ref[v] != ve_ref[jnp.maximum(v - 1, 0)]))
        def _new_expert():
            wgu_scr[:, 0:EXPERT_FF] = wg_ref[0].astype(BF16)
            wgu_scr[:, EXPERT_FF:2 * EXPERT_FF] = wu_ref[0].astype(BF16)
            wd_scr[...] = wd_ref[0].astype(BF16)

        lo, hi = lo_ref[v], hi_ref[v]
        rec = x_ref[:, D_MODEL:ROW_WIDTH]
        second = jnp.where(rec[:, 2:3] == ve_ref[v].astype(F32), 0.0, float(n_tok))
        dest = rec[:, TOKEN_LANE:TOKEN_LANE + 1] + second
        dest_vmem[...] = jnp.broadcast_to(dest, (tm, LANES)).T[0:SUBLANES].astype(jnp.int32)
        to_smem = pltpu.make_async_copy(dest_vmem.at[pl.ds(0, 1)], dest_smem, dest_sem)
        to_smem.start()

        x = x_ref[:, 0:D_MODEL].astype(BF16)
        gu = jnp.dot(x, wgu_scr[...], preferred_element_type=F32)
        g = gu[:, 0:EXPERT_FF]
        a = (g * jax.nn.sigmoid(g) * gu[:, EXPERT_FF:2 * EXPERT_FF]).astype(BF16)
        y = jnp.dot(a, wd_scr[...], preferred_element_type=F32)
        ybuf[slot] = y.reshape(tm // SUBLANES, SUBLANES, D_MODEL)
        to_smem.wait()

        def scatter(blk, k):
            pltpu.make_async_copy(ybuf.at[slot, blk, pl.ds(k, 1)],
                                  y2_ref.at[pl.ds(dest_smem[0, blk * SUBLANES + k], 1)],
                                  sems.at[slot]).start(priority=k % 2)

        def partial_group(blk, first_row, end_row):
            for k in range(SUBLANES):
                @pl.when((blk * SUBLANES + k >= first_row) & (blk * SUBLANES + k < end_row))
                def _row(k=k):
                    scatter(blk, k)

        def full_group(blk, carry):
            for k in range(SUBLANES):
                scatter(blk, k)
            return carry

        grp_lo = (lo + SUBLANES - 1) // SUBLANES
        grp_hi = hi // SUBLANES
        partial_group(lo // SUBLANES, lo, jnp.minimum(hi, grp_lo * SUBLANES))
        lax.fori_loop(grp_lo, grp_hi, full_group, 0)
        partial_group(grp_hi, jnp.maximum(grp_hi, grp_lo) * SUBLANES, hi)

        @pl.when(v > 0)
        def _previous():
            drain(1 - slot, hi_ref[v - 1] - lo_ref[v - 1])

        @pl.when(v == n_vis - 1)
        def _last():
            drain(slot, hi - lo)


def _experts(vis_tile, vis_expert, vis_lo, vis_hi, n_vis, xs, w_gate, w_up, w_down, tm):
    n_steps = vis_tile.shape[0]
    last = lambda v, nv: jnp.minimum(v, nv[0] - 1)
    of_expert = lambda v, vt, ve, lo, hi, nv: (ve[last(v, nv)], 0, 0)
    return pl.pallas_call(
        functools.partial(_expert_kernel, tm=tm, n_tok=xs.shape[0] // 2),
        grid_spec=pltpu.PrefetchScalarGridSpec(
            num_scalar_prefetch=5,
            grid=(n_steps,),
            in_specs=[pl.BlockSpec((tm, ROW_WIDTH), lambda v, vt, ve, lo, hi, nv: (vt[last(v, nv)], 0)),
                      pl.BlockSpec((1, D_MODEL, EXPERT_FF), of_expert),
                      pl.BlockSpec((1, D_MODEL, EXPERT_FF), of_expert),
                      pl.BlockSpec((1, EXPERT_FF, D_MODEL), of_expert)],
            out_specs=pl.BlockSpec(memory_space=pl.ANY),
            scratch_shapes=[pltpu.VMEM((D_MODEL, 2 * EXPERT_FF), BF16), pltpu.VMEM((EXPERT_FF, D_MODEL), BF16),
                            pltpu.VMEM((2, tm // SUBLANES, SUBLANES, D_MODEL), F32),
                            pltpu.VMEM((SUBLANES, tm), jnp.int32), pltpu.SMEM((1, tm), jnp.int32),
                            pltpu.SemaphoreType.DMA((2,)), pltpu.SemaphoreType.DMA(())]),
        out_shape=jax.ShapeDtypeStruct((xs.shape[0], D_MODEL), F32),
        compiler_params=_params(1),
        name="experts",
    )(vis_tile, vis_expert, vis_lo, vis_hi, n_vis, xs, w_gate, w_up, w_down)


def _combine_kernel(ya_ref, yb_ref, xmid_ref, rec_ref, out_ref):
    w = rec_ref[...]
    out_ref[...] = xmid_ref[...] + w[:, 0:1] * ya_ref[...] + w[:, 1:2] * yb_ref[...]


def _combine(y2, xmid, rec, row0, tm):
    t_rows = xmid.shape[0]
    n_tok = y2.shape[0] // 2
    assert row0 % tm == 0 and n_tok % tm == 0
    row = lambda i: (i, 0)
    return pl.pallas_call(
        _combine_kernel,
        grid=(t_rows // tm,),
        in_specs=[pl.BlockSpec((tm, D_MODEL), lambda i: (row0 // tm + i, 0)),
                  pl.BlockSpec((tm, D_MODEL), lambda i: ((n_tok + row0) // tm + i, 0)),
                  pl.BlockSpec((tm, D_MODEL), row),
                  pl.BlockSpec((tm, LANES), row)],
        out_specs=pl.BlockSpec((tm, D_MODEL), row),
        out_shape=jax.ShapeDtypeStruct((t_rows, D_MODEL), F32),
        compiler_params=_params(1),
        name="combine",
    )(y2, y2, xmid, rec)


def _bucket(ends, idx):
    n = jnp.sum((ends[None, :] <= idx[:, None]).astype(jnp.int32), axis=1)
    return jnp.minimum(n, ends.shape[0] - 1)


def _tile_for(rows, pref):
    tm = min(pref, rows)
    assert rows % tm == 0
    return tm


def kernel(x_prompt, x_sample, cache_ret_state, cache_swa_k, cache_swa_v, meta_tokens, norm1_g, w_in, q_norm_g,
           k_norm_g, ret_norm_g, attn_sinks, w_out, norm2_g, w_group, w_expert, w_gate, w_up, w_down):
    assert norm1_g.shape[0] == 1, "single-layer trunk"
    bp, lp, _ = x_prompt.shape
    bs, ls, _ = x_sample.shape
    n_pairs = RET_HEADS // 2

    g1 = norm1_g[0][None, :]
    g2 = norm2_g[0][None, :]
    w_in_bf = w_in[0].astype(BF16)
    w_out_bf = w_out[0].astype(BF16)
    qg2 = jnp.tile(q_norm_g[0], 2)[None, :]
    kg2 = jnp.tile(k_norm_g[0], 2)[None, :]
    rng = ret_norm_g[0].reshape(RET_HEADS, RET_DV)
    sink_tab = jnp.broadcast_to(jnp.repeat(attn_sinks[0], CHUNK).reshape(SWA_KV_HEADS, 1, 4 * CHUNK),
                                (SWA_KV_HEADS, SUBLANES, 4 * CHUNK))
    w_router_t = jnp.zeros((ROUTER_ROWS, D_MODEL), F32)
    w_router_t = w_router_t.at[0:N_GROUPS].set(w_group[0].T)
    w_router_t = w_router_t.at[ROUTER_EXPERT_ROW0:ROUTER_EXPERT_ROW0 + N_EXPERTS].set(w_expert[0].T)
    w_router_t_bf = w_router_t.astype(BF16)

    meta_rows = 2 * CHUNK
    m_pad = jnp.zeros((meta_rows, D_MODEL), F32).at[0:N_META].set(meta_tokens)
    m_rqk, m_rv, _, _, m_skv = _proj(m_pad, jnp.arange(meta_rows, dtype=jnp.int32), meta_rows, g1, w_in_bf, qg2, kg2)
    s_meta = _meta_state(m_rqk, m_rv)[None]
    meta_kv = m_skv[0:N_META]

    groups = [
        dict(x=x_prompt.reshape(bp * lp, D_MODEL), n=bp, seq=lp, pos0=N_META, has_hist=False, s0=s_meta,
             hist=jnp.zeros((1, WINDOW, 2 * SWA_KV), F32)),
        dict(x=x_sample.reshape(bs * ls, D_MODEL), n=bs, seq=ls, pos0=N_META + PAST_LEN, has_hist=True,
             s0=cache_ret_state[0].reshape(bs, n_pairs, 2 * RET_DK, RET_DV),
             hist=jnp.concatenate([cache_swa_k[0].reshape(bs, WINDOW, SWA_KV),
                                   cache_swa_v[0].reshape(bs, WINDOW, SWA_KV)], axis=-1)),
    ]

    base_cnt = jnp.zeros((N_EXPERTS, LANES), F32)
    for g in groups:
        rows = g["n"] * g["seq"]
        tm = _tile_for(rows, PROJ_TILE)
        pos = g["pos0"] + jnp.arange(g["seq"], dtype=jnp.int32)
        if g["seq"] < tm:
            assert tm % g["seq"] == 0
            pos = jnp.tile(pos, tm // g["seq"])
        else:
            assert g["seq"] % tm == 0
        rqk, rv, gate, sq, skv = _proj(g["x"], pos, tm, g1, w_in_bf, qg2, kg2)
        tl = min(ATTN_TILE, g["seq"])
        omix, s_out, kv_out = _attention(rqk, rv, gate, sq, skv, meta_kv, g["hist"], g["s0"], rng, sink_tab,
                                         n_streams=g["n"], seq=g["seq"], tl=tl, has_hist=g["has_hist"])
        xmid, wcol, route_t, base_cnt = _post(omix, g["x"], w_out_bf, g2, w_router_t_bf, base_cnt,
                                              _tile_for(rows, POST_TILE))
        g.update(xmid=xmid, wcol=wcol, route_t=route_t, s_out=s_out, kv_out=kv_out)

    te = EXPERT_TILE
    total_rows = sum(g["n"] * g["seq"] for g in groups)
    assert (2 * total_rows) % te == 0
    n_row_tiles = (2 * total_rows) // te
    counts = base_cnt[:, 0].astype(jnp.int32)
    off = jnp.cumsum(counts) - counts
    first_tile = off // te
    n_vis_e = jnp.where(counts > 0, (off + counts - 1) // te - first_tile + 1, 0)
    vis_end = jnp.cumsum(n_vis_e)
    n_vis = vis_end[-1:].astype(jnp.int32)
    v = jnp.arange(n_row_tiles + N_EXPERTS, dtype=jnp.int32)
    vis_expert = _bucket(vis_end, v)
    pick = lambda table: jnp.sum(jnp.where(vis_expert[:, None] == jnp.arange(N_EXPERTS, dtype=jnp.int32)[None, :],
                                           table[None, :], 0), axis=1)
    vis_tile = jnp.clip(pick(first_tile) + v - pick(vis_end - n_vis_e), 0, n_row_tiles - 1).astype(jnp.int32)
    vis_lo = jnp.clip(pick(off) - vis_tile * te, 0, te).astype(jnp.int32)
    vis_hi = jnp.clip(pick(off + counts) - vis_tile * te, 0, te).astype(jnp.int32)

    for g in groups:
        eid = g["route_t"][0:2].astype(jnp.int32)
        off_sel = jnp.sum(jnp.where(eid[None] == jnp.arange(N_EXPERTS, dtype=jnp.int32)[:, None, None],
                                    off[:, None, None], 0), axis=0)
        g["pos"] = (off_sel + g["route_t"][4:6].astype(jnp.int32)).astype(jnp.int32)
    xs = _dispatch([g["pos"] for g in groups], [g["xmid"] for g in groups], [g["wcol"] for g in groups], g2,
                   _tile_for(min(g["n"] * g["seq"] for g in groups), MOVE_TILE))

    y2 = _experts(vis_tile, vis_expert.astype(jnp.int32), vis_lo, vis_hi, n_vis, xs, w_gate[0], w_up[0], w_down[0], te)

    outs, row0 = [], 0
    for g in groups:
        rows = g["n"] * g["seq"]
        y = _combine(y2, g["xmid"], g["wcol"], row0, _tile_for(rows, MOVE_TILE))
        outs.append(y.reshape(g["n"], g["seq"], D_MODEL))
        row0 += rows

    def caches(g):
        kv = g["kv_out"]
        k = kv[:, :, 0:SWA_KV].reshape(g["n"], WINDOW, SWA_KV_HEADS, SWA_HD)[None]
        v = kv[:, :, SWA_KV:2 * SWA_KV].reshape(g["n"], WINDOW, SWA_KV_HEADS, SWA_HD)[None]
        s = g["s_out"].reshape(g["n"], RET_HEADS, RET_DK, RET_DV)[None]
        return s, k, v

    sp, kp, vp = caches(groups[0])
    ss, ks, vs = caches(groups[1])
    return (outs[0], outs[1], sp, kp, vp, ss, ks, vs)
```

```python
import functools

import numpy as np
import jax
import jax.numpy as jnp
from jax import lax
from jax.experimental import pallas as pl
from jax.experimental.pallas import tpu as pltpu

F32 = jnp.float32
BF16 = jnp.bfloat16

D_MODEL = 1024
PAST_LEN = 4096
CHUNK = 64
N_META = 16
RET_HEADS = 4
RET_DK = 64
RET_DV = 128
SWA_HEADS = 8
SWA_KV_HEADS = 2
SWA_HD = 64
WINDOW = 128
ROPE_THETA = 10000.0
N_GROUPS = 4
EXPERTS_PER_GROUP = 8
N_EXPERTS = N_GROUPS * EXPERTS_PER_GROUP
EXPERT_FF = 256
EPS = 1e-6
NEG_INF = -1e30
LOG2E = float(np.log2(np.e))
RET_Q = RET_HEADS * RET_DK
RET_V = RET_HEADS * RET_DV
SWA_Q = SWA_HEADS * SWA_HD
SWA_KV = SWA_KV_HEADS * SWA_HD
MIX_WIDTH = RET_V + SWA_Q
IN_WIDTH = 2 * RET_Q + 2 * RET_V + SWA_Q + 2 * SWA_KV

LANES = 128
PROJ_TILE = 1024
POST_TILE = 1024
ATTN_TILE = 256
EXPERT_TILE = 1024
MOVE_TILE = 512
SUBLANES = 8
ROUTER_EXPERT_ROW0 = 8
ROUTER_ROWS = 64
META_ROWS = 32
VMEM_LIMIT = 56 * 1024 * 1024

_LOG_G = [float(np.log1p(-np.exp2(-5.0 - h))) for h in range(RET_HEADS)]


def _params(n_axes):
    return pltpu.CompilerParams(dimension_semantics=("arbitrary",) * n_axes, vmem_limit_bytes=VMEM_LIMIT)


def _split_bf16(a):
    hi = a.astype(BF16)
    return hi, (a - hi.astype(F32)).astype(BF16)


def _split_dot(a, w2):
    hi, lo = _split_bf16(a)
    return jnp.dot(jnp.concatenate([hi, lo], axis=1), w2, preferred_element_type=F32)


def _lane_sum(a):
    return _split_dot(a, jnp.ones((2 * LANES, LANES), BF16))


def _head_sum_matrix():
    i = lax.broadcasted_iota(jnp.int32, (2 * LANES, LANES), 0) % LANES
    j = lax.broadcasted_iota(jnp.int32, (2 * LANES, LANES), 1)
    return jnp.where((i < SWA_HD) == (j < SWA_HD), 1.0, 0.0).astype(BF16)


def _rope(t, c, s1, s2):
    half = SWA_HD // 2
    return t * c + pltpu.roll(t, LANES - half, 1) * s1 + pltpu.roll(t, half, 1) * s2


def _head_rms(t, g, head_w):
    ms = _split_dot(t * t, head_w) * (1.0 / SWA_HD)
    return t * lax.rsqrt(ms + EPS) * g


def _head_rms_pair(ta, tb, g, pair_w):
    ss = jnp.concatenate([(ta * ta).astype(BF16), (tb * tb).astype(BF16)], axis=1)
    ms = jnp.dot(ss, pair_w, preferred_element_type=F32) * (1.0 / SWA_HD)
    return ta * lax.rsqrt(ms[:, 0:LANES] + EPS) * g, tb * lax.rsqrt(ms[:, LANES:2 * LANES] + EPS) * g


def _row_rms(x):
    n_tiles = x.shape[1] // LANES
    ss = x[:, 0:LANES] * x[:, 0:LANES]
    for j in range(1, n_tiles):
        ss = ss + x[:, j * LANES:(j + 1) * LANES] * x[:, j * LANES:(j + 1) * LANES]
    r = lax.rsqrt(_lane_sum(ss) * (1.0 / x.shape[1]) + EPS)
    return x * jnp.concatenate([r] * n_tiles, axis=1)


def _proj_kernel(x_ref, g1_ref, w_ref, qg_ref, kg_ref, cos_ref, s1_ref, s2_ref,
                 rqk_ref, rv_ref, gate_ref, sq_ref, skv_ref):
    xn = (_row_rms(x_ref[...]) * g1_ref[...]).astype(BF16)
    c, s1, s2 = cos_ref[...], s1_ref[...], s2_ref[...]
    head_w = _head_sum_matrix()

    def seg(a, b):
        return jnp.dot(xn, w_ref[:, a:b], preferred_element_type=F32)

    def tile(h, j):
        return h[:, j * LANES:(j + 1) * LANES]

    h = seg(0, 2 * RET_Q)
    for j in range(2):
        rqk_ref[:, j * LANES:(j + 1) * LANES] = _rope(tile(h, j), c, s1, s2).astype(BF16)
    for j in range(2, 4):
        rqk_ref[:, j * LANES:(j + 1) * LANES] = (_rope(tile(h, j), c, s1, s2) * (RET_DK ** -0.5)).astype(BF16)
    a = 2 * RET_Q
    rv_ref[...] = seg(a, a + RET_V).astype(BF16)
    a += RET_V
    g = seg(a, a + RET_V)
    gate_ref[...] = (g * jax.nn.sigmoid(g)).astype(BF16)
    a += RET_V
    h = seg(a, a + SWA_Q)
    qg = qg_ref[...]
    pair_w = jnp.concatenate([jnp.concatenate([head_w[0:LANES], jnp.zeros((LANES, LANES), BF16)], axis=1),
                              jnp.concatenate([jnp.zeros((LANES, LANES), BF16), head_w[0:LANES]], axis=1)], axis=0)
    for j in range(0, SWA_Q // LANES, 2):
        qa, qb = _head_rms_pair(tile(h, j), tile(h, j + 1), qg, pair_w)
        sq_ref[:, j * LANES:(j + 1) * LANES] = _rope(qa, c, s1, s2).astype(BF16)
        sq_ref[:, (j + 1) * LANES:(j + 2) * LANES] = _rope(qb, c, s1, s2).astype(BF16)
    a += SWA_Q
    h = seg(a, a + 2 * SWA_KV)
    skv_ref[:, 0:LANES] = _rope(_head_rms(tile(h, 0), kg_ref[...], head_w), c, s1, s2)
    skv_ref[:, LANES:2 * LANES] = tile(h, 1)


def _rope_tables(pos):
    half = SWA_HD // 2
    inv = ROPE_THETA ** (-jnp.arange(half, dtype=F32) / half)
    ang = pos.astype(F32)[:, None] * inv[None, :]
    cos, sin = jnp.cos(ang), jnp.sin(ang)
    z = jnp.zeros_like(sin)
    heads_per_tile = LANES // SWA_HD
    return (jnp.tile(cos, (1, 2 * heads_per_tile)),
            jnp.tile(jnp.concatenate([-sin, z], axis=1), (1, heads_per_tile)),
            jnp.tile(jnp.concatenate([z, sin], axis=1), (1, heads_per_tile)))


def _proj(x2d, pos_rows, tm, g1, w_in_bf, qg2, kg2):
    t_rows = x2d.shape[0]
    n_tiles = t_rows // tm
    n_pos_tiles = pos_rows.shape[0] // tm
    cos, s1, s2 = _rope_tables(pos_rows)
    row = lambda i: (i, 0)
    const = lambda i: (0, 0)
    tab = lambda i: (i % n_pos_tiles, 0)
    return pl.pallas_call(
        _proj_kernel,
        grid=(n_tiles,),
        in_specs=[pl.BlockSpec((tm, D_MODEL), row),
                  pl.BlockSpec((1, D_MODEL), const),
                  pl.BlockSpec((D_MODEL, IN_WIDTH), const),
                  pl.BlockSpec((1, LANES), const),
                  pl.BlockSpec((1, LANES), const),
                  pl.BlockSpec((tm, LANES), tab),
                  pl.BlockSpec((tm, LANES), tab),
                  pl.BlockSpec((tm, LANES), tab)],
        out_specs=[pl.BlockSpec((tm, 2 * RET_Q), row),
                   pl.BlockSpec((tm, RET_V), row),
                   pl.BlockSpec((tm, RET_V), row),
                   pl.BlockSpec((tm, SWA_Q), row),
                   pl.BlockSpec((tm, 2 * SWA_KV), row)],
        out_shape=[jax.ShapeDtypeStruct((t_rows, 2 * RET_Q), BF16),
                   jax.ShapeDtypeStruct((t_rows, RET_V), BF16),
                   jax.ShapeDtypeStruct((t_rows, RET_V), BF16),
                   jax.ShapeDtypeStruct((t_rows, SWA_Q), BF16),
                   jax.ShapeDtypeStruct((t_rows, 2 * SWA_KV), F32)],
        compiler_params=_params(1),
        name="proj",
    )(x2d, g1, w_in_bf, qg2, kg2, cos, s1, s2)


def _pair_update(k_bf, v0_bf, v1_bf, wt):
    kw = (k_bf.astype(F32) * wt).astype(BF16)
    dn = (((0,), (0,)), ((), ()))
    a0 = lax.dot_general(kw, v0_bf, dn, preferred_element_type=F32)
    a1 = lax.dot_general(kw, v1_bf, dn, preferred_element_type=F32)
    top = lax.broadcasted_iota(jnp.int32, a0.shape, 0) < RET_DK
    return jnp.where(top, a0, a1)


def _decay_rows(n, pair, rows_back_from):
    i = lax.broadcasted_iota(jnp.int32, (n, LANES), 0).astype(F32)
    lane = lax.broadcasted_iota(jnp.int32, (n, LANES), 1)
    lg = jnp.where(lane < RET_DK, _LOG_G[2 * pair], _LOG_G[2 * pair + 1])
    return jnp.exp((rows_back_from - i) * lg)


def _meta_state_kernel(rqk_ref, rv_ref, s_ref, *, n_rows):
    for p in range(RET_HEADS // 2):
        k = rqk_ref[:, RET_Q + p * LANES:RET_Q + (p + 1) * LANES]
        wt = _decay_rows(n_rows, p, float(N_META - 1))
        s_ref[p] = _pair_update(k, rv_ref[:, (2 * p) * LANES:(2 * p + 1) * LANES],
                                rv_ref[:, (2 * p + 1) * LANES:(2 * p + 2) * LANES], wt)


def _meta_state(m_rqk, m_rv):
    n_rows = m_rqk.shape[0]
    return pl.pallas_call(
        functools.partial(_meta_state_kernel, n_rows=n_rows),
        out_shape=jax.ShapeDtypeStruct((RET_HEADS // 2, 2 * RET_DK, RET_DV), F32),
        name="meta_state",
    )(m_rqk, m_rv)


def _dup_halves(a, lo_mask):
    sw = pltpu.roll(a, SWA_HD, 1)
    return jnp.where(lo_mask, a, sw), jnp.where(lo_mask, sw, a)


def _attn_kernel(rqk_ref, rv_ref, gate_ref, sq_ref, skv_ref, meta_ref, hist_ref, s0_ref, rng_ref, sink_ref,
                 omix_ref, sout_ref, kvout_ref,
                 s_scr, kd_scr, vd_scr, mk_scr, mv_scr, dec_scr, wt_scr, cs_scr, gam_scr,
                 *, tl, has_hist):
    b = pl.program_id(0)
    t = pl.program_id(1)
    nt = pl.num_programs(1)
    n_chunks = tl // CHUNK
    n_pairs = RET_HEADS // 2
    lo_tl = lax.broadcasted_iota(jnp.int32, (tl, LANES), 1) < SWA_HD
    lo_c = lax.broadcasted_iota(jnp.int32, (CHUNK, LANES), 1) < SWA_HD

    @pl.when((b == 0) & (t == 0))
    def _tables():
        i = lax.broadcasted_iota(jnp.int32, (tl, tl), 0)
        j = lax.broadcasted_iota(jnp.int32, (tl, tl), 1)
        diff = (i - j).astype(F32)
        row = lax.broadcasted_iota(jnp.int32, (tl, LANES), 0).astype(F32)
        for h in range(RET_HEADS):
            dec_scr[h] = jnp.where(diff >= 0.0, jnp.exp(jnp.maximum(diff, 0.0) * _LOG_G[h]), 0.0)
            cs_scr[h] = jnp.exp((row + 1.0) * _LOG_G[h])
        top = lax.broadcasted_iota(jnp.int32, (2 * RET_DK, RET_DV), 0) < RET_DK
        for p in range(n_pairs):
            wt_scr[p] = _decay_rows(tl, p, float(tl - 1))
            gam_scr[p] = jnp.where(top, jnp.exp(jnp.float32(tl * _LOG_G[2 * p])), jnp.exp(jnp.float32(tl * _LOG_G[2 * p + 1])))
        lo_m = lax.broadcasted_iota(jnp.int32, (N_META, LANES), 1) < SWA_HD
        mk0, mk1 = _dup_halves(meta_ref[:, 0:LANES], lo_m)
        mv0, mv1 = _dup_halves(meta_ref[:, LANES:2 * LANES], lo_m)
        mk_scr[...] = jnp.zeros(mk_scr.shape, BF16)
        mv_scr[...] = jnp.zeros(mv_scr.shape, BF16)
        mk_scr[0, 0:N_META] = mk0.astype(BF16)
        mk_scr[1, 0:N_META] = mk1.astype(BF16)
        mv_scr[0, 0:N_META] = mv0.astype(BF16)
        mv_scr[1, 0:N_META] = mv1.astype(BF16)

    @pl.when(t == 0)
    def _stream_start():
        s_scr[...] = s0_ref[0]
        if has_hist:
            lo_w = lax.broadcasted_iota(jnp.int32, (WINDOW, LANES), 1) < SWA_HD
            k0, k1 = _dup_halves(hist_ref[0, :, 0:LANES], lo_w)
            v0, v1 = _dup_halves(hist_ref[0, :, LANES:2 * LANES], lo_w)
            kd_scr[0, 0:WINDOW] = k0.astype(BF16)
            kd_scr[1, 0:WINDOW] = k1.astype(BF16)
            vd_scr[0, 0:WINDOW] = v0.astype(BF16)
            vd_scr[1, 0:WINDOW] = v1.astype(BF16)
        else:
            z = jnp.zeros((WINDOW, LANES), BF16)
            for kv in range(SWA_KV_HEADS):
                kd_scr[kv, 0:WINDOW] = z
                vd_scr[kv, 0:WINDOW] = z

    k0, k1 = _dup_halves(skv_ref[:, 0:LANES], lo_tl)
    v0, v1 = _dup_halves(skv_ref[:, LANES:2 * LANES], lo_tl)
    kd_scr[0, WINDOW:WINDOW + tl] = k0.astype(BF16)
    kd_scr[1, WINDOW:WINDOW + tl] = k1.astype(BF16)
    vd_scr[0, WINDOW:WINDOW + tl] = v0.astype(BF16)
    vd_scr[1, WINDOW:WINDOW + tl] = v1.astype(BF16)

    band = WINDOW + CHUNK
    n_keys = META_ROWS + band
    n_q = 4 * CHUNK
    scale2 = (SWA_HD ** -0.5) * LOG2E
    krow = lax.broadcasted_iota(jnp.int32, (n_keys, n_q), 0)
    zero_c = jnp.zeros((CHUNK, LANES), BF16)
    ones_v = jnp.ones((n_keys, LANES), BF16)
    for c in range(n_chunks):
        if has_hist:
            first_valid = META_ROWS
        else:
            first_valid = jnp.where(t == 0, max(META_ROWS + WINDOW - c * CHUNK, META_ROWS), META_ROWS)
        valid_t = (krow < N_META) | (krow >= first_valid)
        r0 = c * CHUNK
        for kv in range(SWA_KV_HEADS):
            keys = jnp.concatenate([mk_scr[kv], kd_scr[kv, r0:r0 + band]], axis=0)
            vals = jnp.concatenate([mv_scr[kv], vd_scr[kv, r0:r0 + band]], axis=0)
            qa = sq_ref[r0:r0 + CHUNK, (2 * kv) * LANES:(2 * kv + 1) * LANES]
            qb = sq_ref[r0:r0 + CHUNK, (2 * kv + 1) * LANES:(2 * kv + 2) * LANES]
            lhs = jnp.concatenate([jnp.where(lo_c, qa, zero_c), jnp.where(lo_c, zero_c, qa),
                                   jnp.where(lo_c, qb, zero_c), jnp.where(lo_c, zero_c, qb)], axis=0)
            s_t = lax.dot_general(keys, lhs, (((1,), (1,)), ((), ())), preferred_element_type=F32) * scale2
            s_t = jnp.where(valid_t, s_t, NEG_INF)
            s_t = jnp.where(krow == N_META, sink_ref[kv, 0:1, :] * LOG2E, s_t)
            e_t = jnp.exp2(s_t - jnp.max(s_t, axis=0, keepdims=True)).astype(BF16)
            ov = lax.dot_general(e_t, jnp.concatenate([vals, ones_v], axis=1), (((0,), (0,)), ((), ())),
                                 preferred_element_type=F32)
            o = ov[:, 0:LANES] * (1.0 / ov[:, LANES:2 * LANES])
            oa = jnp.where(lo_c, o[0:CHUNK], o[CHUNK:2 * CHUNK])
            ob = jnp.where(lo_c, o[2 * CHUNK:3 * CHUNK], o[3 * CHUNK:4 * CHUNK])
            base = RET_V + (2 * kv) * LANES
            omix_ref[r0:r0 + CHUNK, base:base + LANES] = oa.astype(BF16)
            omix_ref[r0:r0 + CHUNK, base + LANES:base + 2 * LANES] = ob.astype(BF16)

    zero_t = jnp.zeros((tl, LANES), BF16)
    for p in range(n_pairs):
        q = rqk_ref[:, p * LANES:(p + 1) * LANES]
        k = rqk_ref[:, RET_Q + p * LANES:RET_Q + (p + 1) * LANES]
        lhs = jnp.concatenate([jnp.where(lo_tl, q, zero_t), jnp.where(lo_tl, zero_t, q)], axis=0)
        s = lax.dot_general(lhs, k, (((1,), (1,)), ((), ())), preferred_element_type=F32)
        cross = jnp.dot(lhs, s_scr[p].astype(BF16), preferred_element_type=F32)
        for i in range(2):
            h = 2 * p + i
            v = rv_ref[:, h * LANES:(h + 1) * LANES]
            a = (s[i * tl:(i + 1) * tl] * dec_scr[h]).astype(BF16)
            o = jnp.dot(a, v, preferred_element_type=F32) + cross[i * tl:(i + 1) * tl] * cs_scr[h]
            r = o * lax.rsqrt(_lane_sum(o * o) * (1.0 / RET_DV) + EPS) * rng_ref[h:h + 1, :]
            omix_ref[:, h * LANES:(h + 1) * LANES] = (r * gate_ref[:, h * LANES:(h + 1) * LANES].astype(F32)).astype(BF16)
        u = _pair_update(k, rv_ref[:, (2 * p) * LANES:(2 * p + 1) * LANES],
                         rv_ref[:, (2 * p + 1) * LANES:(2 * p + 2) * LANES], wt_scr[p])
        s_scr[p] = gam_scr[p] * s_scr[p] + u

    if tl >= WINDOW:
        @pl.when(t + 1 < nt)
        def _carry_window():
            for kv in range(SWA_KV_HEADS):
                kd_scr[kv, 0:WINDOW] = kd_scr[kv, tl:tl + WINDOW]
                vd_scr[kv, 0:WINDOW] = vd_scr[kv, tl:tl + WINDOW]

    @pl.when(t + 1 == nt)
    def _stream_end():
        sout_ref[0] = s_scr[...]
        if tl >= WINDOW:
            kvout_ref[0] = skv_ref[tl - WINDOW:tl, :]
        else:
            kvout_ref[0, 0:WINDOW - tl] = hist_ref[0, tl:WINDOW, :]
            kvout_ref[0, WINDOW - tl:WINDOW] = skv_ref[...]


def _attention(rqk, rv, gate, sq, skv, meta_kv, hist_kv, s0, rng, sink_tab, *, n_streams, seq, tl, has_hist):
    nt = seq // tl
    assert tl % CHUNK == 0 and seq % tl == 0
    assert tl >= WINDOW or (nt == 1 and has_hist)
    n_pairs = RET_HEADS // 2
    s0_shared = s0.shape[0] == 1
    row = lambda b, t: (b * nt + t, 0)
    const2 = lambda b, t: (0, 0)
    const3 = lambda b, t: (0, 0, 0)
    per_b3 = lambda b, t: (b, 0, 0)
    s0_map = (lambda b, t: (0, 0, 0, 0)) if s0_shared else (lambda b, t: (b, 0, 0, 0))
    hist_map = per_b3 if has_hist else const3
    rows = n_streams * seq
    return pl.pallas_call(
        functools.partial(_attn_kernel, tl=tl, has_hist=has_hist),
        grid=(n_streams, nt),
        in_specs=[pl.BlockSpec((tl, 2 * RET_Q), row),
                  pl.BlockSpec((tl, RET_V), row),
                  pl.BlockSpec((tl, RET_V), row),
                  pl.BlockSpec((tl, SWA_Q), row),
                  pl.BlockSpec((tl, 2 * SWA_KV), row),
                  pl.BlockSpec((N_META, 2 * SWA_KV), const2),
                  pl.BlockSpec((1, WINDOW, 2 * SWA_KV), hist_map),
                  pl.BlockSpec((1, n_pairs, 2 * RET_DK, RET_DV), s0_map),
                  pl.BlockSpec((RET_HEADS, RET_DV), const2),
                  pl.BlockSpec((SWA_KV_HEADS, SUBLANES, 4 * CHUNK), const3)],
        out_specs=[pl.BlockSpec((tl, MIX_WIDTH), row),
                   pl.BlockSpec((1, n_pairs, 2 * RET_DK, RET_DV), lambda b, t: (b, 0, 0, 0)),
                   pl.BlockSpec((1, WINDOW, 2 * SWA_KV), per_b3)],
        out_shape=[jax.ShapeDtypeStruct((rows, MIX_WIDTH), BF16),
                   jax.ShapeDtypeStruct((n_streams, n_pairs, 2 * RET_DK, RET_DV), F32),
                   jax.ShapeDtypeStruct((n_streams, WINDOW, 2 * SWA_KV), F32)],
        scratch_shapes=[pltpu.VMEM((n_pairs, 2 * RET_DK, RET_DV), F32),
                        pltpu.VMEM((SWA_KV_HEADS, WINDOW + tl, LANES), BF16),
                        pltpu.VMEM((SWA_KV_HEADS, WINDOW + tl, LANES), BF16),
                        pltpu.VMEM((SWA_KV_HEADS, META_ROWS, LANES), BF16),
                        pltpu.VMEM((SWA_KV_HEADS, META_ROWS, LANES), BF16),
                        pltpu.VMEM((RET_HEADS, tl, tl), F32),
                        pltpu.VMEM((n_pairs, tl, LANES), F32),
                        pltpu.VMEM((RET_HEADS, tl, RET_DV), F32),
                        pltpu.VMEM((n_pairs, 2 * RET_DK, RET_DV), F32)],
        compiler_params=_params(2),
        name="attention",
    )(rqk, rv, gate, sq, skv, meta_kv, hist_kv, s0, rng, sink_tab)


def _post_kernel(omix_ref, x_ref, wout_ref, g2_ref, wrt_ref, base_ref,
                 xmid_ref, wcol_ref, rt_ref, cnt_ref, tri_scr, run_scr):
    i = pl.program_id(0)
    tm = x_ref.shape[0]

    @pl.when(i == 0)
    def _init():
        r = lax.broadcasted_iota(jnp.int32, (tm, tm), 0)
        c = lax.broadcasted_iota(jnp.int32, (tm, tm), 1)
        tri_scr[...] = jnp.where(r < c, 1.0, 0.0).astype(BF16)
        run_scr[...] = base_ref[...]

    xm = x_ref[...] + jnp.dot(omix_ref[...], wout_ref[...], preferred_element_type=F32)
    xmid_ref[...] = xm
    hn = _row_rms(xm) * g2_ref[...]
    lt = lax.dot_general(wrt_ref[...], hn.astype(BF16), (((1,), (1,)), ((), ())), preferred_element_type=F32)
    row8 = lax.broadcasted_iota(jnp.int32, (SUBLANES, tm), 0)
    big = jnp.int32(SUBLANES)
    gl = jnp.where(row8 < N_GROUPS, lt[0:SUBLANES], NEG_INF)
    gmax = jnp.max(gl, axis=0, keepdims=True)
    gsum = jnp.sum(jnp.exp(gl - gmax), axis=0, keepdims=True)
    g_sel = jnp.min(jnp.where(gl == gmax, row8, big), axis=0, keepdims=True)
    p_sel = 1.0 / gsum
    el = lt[ROUTER_EXPERT_ROW0:ROUTER_EXPERT_ROW0 + EXPERTS_PER_GROUP]
    for g in range(1, N_GROUPS):
        lo = ROUTER_EXPERT_ROW0 + g * EXPERTS_PER_GROUP
        el = jnp.where(g_sel == g, lt[lo:lo + EXPERTS_PER_GROUP], el)
    m1 = jnp.max(el, axis=0, keepdims=True)
    i1 = jnp.min(jnp.where(el == m1, row8, big), axis=0, keepdims=True)
    el2 = jnp.where(row8 == i1, NEG_INF, el)
    m2 = jnp.max(el2, axis=0, keepdims=True)
    i2 = jnp.min(jnp.where(el2 == m2, row8, big), axis=0, keepdims=True)
    e2 = jnp.exp(m2 - m1)
    inv = 1.0 / (1.0 + e2)
    w1 = p_sel * inv
    w2 = p_sel * (e2 * inv)
    eid1 = g_sel * EXPERTS_PER_GROUP + i1
    eid2 = g_sel * EXPERTS_PER_GROUP + i2

    rowe = lax.broadcasted_iota(jnp.int32, (N_EXPERTS, tm), 0)
    oh = jnp.where((rowe == eid1) | (rowe == eid2), 1.0, 0.0).astype(BF16)
    run = run_scr[...]
    pref = jnp.dot(oh, tri_scr[...], preferred_element_type=F32) + jnp.concatenate([run] * (tm // LANES), axis=1)
    r1 = jnp.sum(jnp.where(rowe == eid1, pref, 0.0), axis=0, keepdims=True)
    r2 = jnp.sum(jnp.where(rowe == eid2, pref, 0.0), axis=0, keepdims=True)
    run = run + jnp.dot(oh, jnp.ones((tm, LANES), BF16), preferred_element_type=F32)
    run_scr[...] = run
    cnt_ref[...] = run

    out = jnp.zeros((SUBLANES, tm), F32)
    for k, v in enumerate([eid1.astype(F32), eid2.astype(F32), w1, w2, r1, r2]):
        out = jnp.where(row8 == k, v, out)
    rt_ref[...] = out
    rowl = lax.broadcasted_iota(jnp.int32, (LANES, tm), 0)
    wcol_ref[...] = jnp.where(rowl == 0, w1, jnp.where(rowl == 1, w2, 0.0)).T


def _post(omix, x2d, w_out_bf, g2, w_router_t_bf, base_cnt, tm):
    t_rows = x2d.shape[0]
    assert tm % LANES == 0
    row = lambda i: (i, 0)
    const = lambda i: (0, 0)
    return pl.pallas_call(
        _post_kernel,
        grid=(t_rows // tm,),
        in_specs=[pl.BlockSpec((tm, MIX_WIDTH), row),
                  pl.BlockSpec((tm, D_MODEL), row),
                  pl.BlockSpec((MIX_WIDTH, D_MODEL), const),
                  pl.BlockSpec((1, D_MODEL), const),
                  pl.BlockSpec((ROUTER_ROWS, D_MODEL), const),
                  pl.BlockSpec((N_EXPERTS, LANES), const)],
        out_specs=[pl.BlockSpec((tm, D_MODEL), row),
                   pl.BlockSpec((tm, LANES), row),
                   pl.BlockSpec((SUBLANES, tm), lambda i: (0, i)),
                   pl.BlockSpec((N_EXPERTS, LANES), const)],
        out_shape=[jax.ShapeDtypeStruct((t_rows, D_MODEL), F32),
                   jax.ShapeDtypeStruct((t_rows, LANES), F32),
                   jax.ShapeDtypeStruct((SUBLANES, t_rows), F32),
                   jax.ShapeDtypeStruct((N_EXPERTS, LANES), F32)],
        scratch_shapes=[pltpu.VMEM((tm, tm), BF16), pltpu.VMEM((N_EXPERTS, LANES), F32)],
        compiler_params=_params(1),
        name="post",
    )(omix, x2d, w_out_bf, g2, w_router_t_bf, base_cnt)


def _step_major(pos, tm):
    return [pos[e].reshape(-1, 1, tm) for e in range(2)]


def _dispatch_kernel(pos0_ref, pos1_ref, *refs, tm, group_steps):
    pos_refs = (pos0_ref, pos1_ref)
    n_g = len(group_steps)
    xmid_refs = refs[:n_g]
    g2_ref, xs_ref, hbuf, sems = refs[n_g:]
    n_steps = sum(group_steps)
    i = pl.program_id(0)
    slot = i % 2

    first = 0
    for xmid_ref, steps in zip(xmid_refs, group_steps):
        @pl.when((i >= first) & (i < first + steps))
        def _normalise(xmid_ref=xmid_ref):
            hn = _row_rms(xmid_ref[...]) * g2_ref[...]
            hbuf[slot] = hn.reshape(tm // SUBLANES, SUBLANES, D_MODEL)
        first += steps

    def body(blk, carry):
        for k in range(SUBLANES):
            r = blk * SUBLANES + k
            for e in range(2):
                pltpu.make_async_copy(hbuf.at[slot, blk, pl.ds(k, 1)], xs_ref.at[pl.ds(pos_refs[e][0, 0, r], 1)],
                                      sems.at[slot]).start(priority=e)
        return carry

    lax.fori_loop(0, tm // SUBLANES, body, 0)

    def drain(which):
        pltpu.make_async_copy(xs_ref.at[pl.ds(0, 2 * tm)], xs_ref.at[pl.ds(0, 2 * tm)], sems.at[which]).wait()

    @pl.when(i > 0)
    def _previous():
        drain(1 - slot)

    @pl.when(i == n_steps - 1)
    def _last():
        drain(slot)


def _dispatch(pos_list, xmid_list, g2, tm):
    group_steps = tuple(x.shape[0] // tm for x in xmid_list)
    n_steps = sum(group_steps)
    assert tm % SUBLANES == 0
    pos3 = [jnp.concatenate(parts, axis=0) for parts in zip(*[_step_major(p, tm) for p in pos_list])]
    smem = pl.BlockSpec((1, 1, tm), lambda i: (i, 0, 0), memory_space=pltpu.SMEM)
    x_specs = []
    first = 0
    for steps in group_steps:
        x_specs.append(pl.BlockSpec((tm, D_MODEL),
                                    lambda i, first=first, steps=steps: (jnp.clip(i - first, 0, steps - 1), 0)))
        first += steps
    n_rows_out = 2 * sum(x.shape[0] for x in xmid_list)
    return pl.pallas_call(
        functools.partial(_dispatch_kernel, tm=tm, group_steps=group_steps),
        grid=(n_steps,),
        in_specs=[smem, smem] + x_specs + [pl.BlockSpec((1, D_MODEL), lambda i: (0, 0))],
        out_specs=pl.BlockSpec(memory_space=pl.ANY),
        out_shape=jax.ShapeDtypeStruct((n_rows_out, D_MODEL), F32),
        scratch_shapes=[pltpu.VMEM((2, tm // SUBLANES, SUBLANES, D_MODEL), F32), pltpu.SemaphoreType.DMA((2,))],
        compiler_params=_params(1),
        name="dispatch",
    )(*pos3, *xmid_list, g2)


def _expert_kernel(vt_ref, ve_ref, lo_ref, hi_ref, nv_ref, x_ref, wg_ref, wu_ref, wd_ref, y_ref, wgu_scr, wd_scr):
    v = pl.program_id(0)

    @pl.when(v < nv_ref[0])
    def _compute():
        @pl.when((v == 0) | (ve_ref[v] != ve_ref[jnp.maximum(v - 1, 0)]))
        def _new_expert():
            wgu_scr[:, 0:EXPERT_FF] = wg_ref[0].astype(BF16)
            wgu_scr[:, EXPERT_FF:2 * EXPERT_FF] = wu_ref[0].astype(BF16)
            wd_scr[...] = wd_ref[0].astype(BF16)

        x = x_ref[...].astype(BF16)
        gu = jnp.dot(x, wgu_scr[...], preferred_element_type=F32)
        g = gu[:, 0:EXPERT_FF]
        a = (g * jax.nn.sigmoid(g) * gu[:, EXPERT_FF:2 * EXPERT_FF]).astype(BF16)
        y = jnp.dot(a, wd_scr[...], preferred_element_type=F32)
        lo, hi = lo_ref[v], hi_ref[v]

        @pl.when(lo == 0)
        def _first_visit():
            y_ref[...] = y

        @pl.when(lo > 0)
        def _later_visit():
            row = lax.broadcasted_iota(jnp.int32, y.shape, 0)
            y_ref[...] = jnp.where((row >= lo) & (row < hi), y, y_ref[...])


def _experts(vis_tile, vis_expert, vis_lo, vis_hi, n_vis, xs, w_gate, w_up, w_down, tm):
    n_steps = vis_tile.shape[0]
    last = lambda v, nv: jnp.minimum(v, nv[0] - 1)
    of_expert = lambda v, vt, ve, lo, hi, nv: (ve[last(v, nv)], 0, 0)
    return pl.pallas_call(
        _expert_kernel,
        grid_spec=pltpu.PrefetchScalarGridSpec(
            num_scalar_prefetch=5,
            grid=(n_steps,),
            in_specs=[pl.BlockSpec((tm, D_MODEL), lambda v, vt, ve, lo, hi, nv: (vt[last(v, nv)], 0)),
                      pl.BlockSpec((1, D_MODEL, EXPERT_FF), of_expert),
                      pl.BlockSpec((1, D_MODEL, EXPERT_FF), of_expert),
                      pl.BlockSpec((1, EXPERT_FF, D_MODEL), of_expert)],
            out_specs=pl.BlockSpec((tm, D_MODEL), lambda v, vt, ve, lo, hi, nv: (vt[last(v, nv)], 0)),
            scratch_shapes=[pltpu.VMEM((D_MODEL, 2 * EXPERT_FF), BF16), pltpu.VMEM((EXPERT_FF, D_MODEL), BF16)]),
        out_shape=jax.ShapeDtypeStruct(xs.shape, F32),
        compiler_params=_params(1),
        name="experts",
    )(vis_tile, vis_expert, vis_lo, vis_hi, n_vis, xs, w_gate, w_up, w_down)


def _combine_kernel(pos0_ref, pos1_ref, nxt0_ref, nxt1_ref, ys_ref, xmid_ref, wcol_ref, out_ref, ybuf, sems,
                    *, tm, n_steps):
    i = pl.program_id(0)
    slot = i % 2

    def issue(p_refs, to_slot):
        def body(blk, carry):
            for k in range(SUBLANES):
                r = blk * SUBLANES + k
                for e in range(2):
                    pltpu.make_async_copy(ys_ref.at[pl.ds(p_refs[e][0, 0, r], 1)],
                                          ybuf.at[to_slot, e, blk, pl.ds(k, 1)], sems.at[to_slot]).start(priority=e)
            return carry

        lax.fori_loop(0, tm // SUBLANES, body, 0)

    @pl.when(i == 0)
    def _first():
        issue((pos0_ref, pos1_ref), 0)

    @pl.when(i + 1 < n_steps)
    def _ahead():
        issue((nxt0_ref, nxt1_ref), 1 - slot)

    for e in range(2):
        pltpu.make_async_copy(ybuf.at[slot, e], ybuf.at[slot, e], sems.at[slot]).wait()
    w = wcol_ref[...]
    y0 = ybuf[slot, 0].reshape(tm, D_MODEL)
    y1 = ybuf[slot, 1].reshape(tm, D_MODEL)
    out_ref[...] = xmid_ref[...] + w[:, 0:1] * y0 + w[:, 1:2] * y1


def _combine(pos, ys, xmid, wcol, tm):
    t_rows = xmid.shape[0]
    n_steps = t_rows // tm
    assert tm % SUBLANES == 0
    pos3 = _step_major(pos, tm)
    row = lambda i: (i, 0)
    cur = pl.BlockSpec((1, 1, tm), lambda i: (i, 0, 0), memory_space=pltpu.SMEM)
    nxt = pl.BlockSpec((1, 1, tm), lambda i: (jnp.minimum(i + 1, n_steps - 1), 0, 0), memory_space=pltpu.SMEM)
    return pl.pallas_call(
        functools.partial(_combine_kernel, tm=tm, n_steps=n_steps),
        grid=(n_steps,),
        in_specs=[cur, cur, nxt, nxt,
                  pl.BlockSpec(memory_space=pl.ANY),
                  pl.BlockSpec((tm, D_MODEL), row),
                  pl.BlockSpec((tm, LANES), row)],
        out_specs=pl.BlockSpec((tm, D_MODEL), row),
        out_shape=jax.ShapeDtypeStruct((t_rows, D_MODEL), F32),
        scratch_shapes=[pltpu.VMEM((2, 2, tm // SUBLANES, SUBLANES, D_MODEL), F32), pltpu.SemaphoreType.DMA((2,))],
        compiler_params=_params(1),
        name="combine",
    )(*pos3, *pos3, ys, xmid, wcol)


def _bucket(ends, idx):
    n = jnp.sum((ends[None, :] <= idx[:, None]).astype(jnp.int32), axis=1)
    return jnp.minimum(n, ends.shape[0] - 1)


def _tile_for(rows, pref):
    tm = min(pref, rows)
    assert rows % tm == 0
    return tm


def kernel(x_prompt, x_sample, cache_ret_state, cache_swa_k, cache_swa_v, meta_tokens, norm1_g, w_in, q_norm_g,
           k_norm_g, ret_norm_g, attn_sinks, w_out, norm2_g, w_group, w_expert, w_gate, w_up, w_down):
    assert norm1_g.shape[0] == 1, "single-layer trunk"
    bp, lp, _ = x_prompt.shape
    bs, ls, _ = x_sample.shape
    n_pairs = RET_HEADS // 2

    g1 = norm1_g[0][None, :]
    g2 = norm2_g[0][None, :]
    w_in_bf = w_in[0].astype(BF16)
    w_out_bf = w_out[0].astype(BF16)
    qg2 = jnp.tile(q_norm_g[0], 2)[None, :]
    kg2 = jnp.tile(k_norm_g[0], 2)[None, :]
    rng = ret_norm_g[0].reshape(RET_HEADS, RET_DV)
    sink_tab = jnp.broadcast_to(jnp.repeat(attn_sinks[0], CHUNK).reshape(SWA_KV_HEADS, 1, 4 * CHUNK),
                                (SWA_KV_HEADS, SUBLANES, 4 * CHUNK))
    w_router_t = jnp.zeros((ROUTER_ROWS, D_MODEL), F32)
    w_router_t = w_router_t.at[0:N_GROUPS].set(w_group[0].T)
    w_router_t = w_router_t.at[ROUTER_EXPERT_ROW0:ROUTER_EXPERT_ROW0 + N_EXPERTS].set(w_expert[0].T)
    w_router_t_bf = w_router_t.astype(BF16)

    meta_rows = 2 * CHUNK
    m_pad = jnp.zeros((meta_rows, D_MODEL), F32).at[0:N_META].set(meta_tokens)
    m_rqk, m_rv, _, _, m_skv = _proj(m_pad, jnp.arange(meta_rows, dtype=jnp.int32), meta_rows, g1, w_in_bf, qg2, kg2)
    s_meta = _meta_state(m_rqk, m_rv)[None]
    meta_kv = m_skv[0:N_META]

    groups = [
        dict(x=x_prompt.reshape(bp * lp, D_MODEL), n=bp, seq=lp, pos0=N_META, has_hist=False, s0=s_meta,
             hist=jnp.zeros((1, WINDOW, 2 * SWA_KV), F32)),
        dict(x=x_sample.reshape(bs * ls, D_MODEL), n=bs, seq=ls, pos0=N_META + PAST_LEN, has_hist=True,
             s0=cache_ret_state[0].reshape(bs, n_pairs, 2 * RET_DK, RET_DV),
             hist=jnp.concatenate([cache_swa_k[0].reshape(bs, WINDOW, SWA_KV),
                                   cache_swa_v[0].reshape(bs, WINDOW, SWA_KV)], axis=-1)),
    ]

    base_cnt = jnp.zeros((N_EXPERTS, LANES), F32)
    for g in groups:
        rows = g["n"] * g["seq"]
        tm = _tile_for(rows, PROJ_TILE)
        pos = g["pos0"] + jnp.arange(g["seq"], dtype=jnp.int32)
        if g["seq"] < tm:
            assert tm % g["seq"] == 0
            pos = jnp.tile(pos, tm // g["seq"])
        else:
            assert g["seq"] % tm == 0
        rqk, rv, gate, sq, skv = _proj(g["x"], pos, tm, g1, w_in_bf, qg2, kg2)
        tl = min(ATTN_TILE, g["seq"])
        omix, s_out, kv_out = _attention(rqk, rv, gate, sq, skv, meta_kv, g["hist"], g["s0"], rng, sink_tab,
                                         n_streams=g["n"], seq=g["seq"], tl=tl, has_hist=g["has_hist"])
        xmid, wcol, route_t, base_cnt = _post(omix, g["x"], w_out_bf, g2, w_router_t_bf, base_cnt,
                                              _tile_for(rows, POST_TILE))
        g.update(xmid=xmid, wcol=wcol, route_t=route_t, s_out=s_out, kv_out=kv_out)

    te = EXPERT_TILE
    total_rows = sum(g["n"] * g["seq"] for g in groups)
    assert (2 * total_rows) % te == 0
    n_row_tiles = (2 * total_rows) // te
    counts = base_cnt[:, 0].astype(jnp.int32)
    off = jnp.cumsum(counts) - counts
    first_tile = off // te
    n_vis_e = jnp.where(counts > 0, (off + counts - 1) // te - first_tile + 1, 0)
    vis_end = jnp.cumsum(n_vis_e)
    n_vis = vis_end[-1:].astype(jnp.int32)
    v = jnp.arange(n_row_tiles + N_EXPERTS, dtype=jnp.int32)
    vis_expert = _bucket(vis_end, v)
    pick = lambda table: jnp.sum(jnp.where(vis_expert[:, None] == jnp.arange(N_EXPERTS, dtype=jnp.int32)[None, :],
                                           table[None, :], 0), axis=1)
    vis_tile = jnp.clip(pick(first_tile) + v - pick(vis_end - n_vis_e), 0, n_row_tiles - 1).astype(jnp.int32)
    vis_lo = jnp.clip(pick(off) - vis_tile * te, 0, te).astype(jnp.int32)
    vis_hi = jnp.clip(pick(off + counts) - vis_tile * te, 0, te).astype(jnp.int32)

    for g in groups:
        eid = g["route_t"][0:2].astype(jnp.int32)
        off_sel = jnp.sum(jnp.where(eid[None] == jnp.arange(N_EXPERTS, dtype=jnp.int32)[:, None, None],
                                    off[:, None, None], 0), axis=0)
        g["pos"] = (off_sel + g["route_t"][4:6].astype(jnp.int32)).astype(jnp.int32)
    xs = _dispatch([g["pos"] for g in groups], [g["xmid"] for g in groups], g2,
                   _tile_for(min(g["n"] * g["seq"] for g in groups), MOVE_TILE))

    ys = _experts(vis_tile, vis_expert.astype(jnp.int32), vis_lo, vis_hi, n_vis, xs, w_gate[0], w_up[0], w_down[0], te)

    outs = []
    for g in groups:
        rows = g["n"] * g["seq"]
        y = _combine(g["pos"], ys, g["xmid"], g["wcol"], _tile_for(rows, MOVE_TILE))
        outs.append(y.reshape(g["n"], g["seq"], D_MODEL))

    def caches(g):
        kv = g["kv_out"]
        k = kv[:, :, 0:SWA_KV].reshape(g["n"], WINDOW, SWA_KV_HEADS, SWA_HD)[None]
        v = kv[:, :, SWA_KV:2 * SWA_KV].reshape(g["n"], WINDOW, SWA_KV_HEADS, SWA_HD)[None]
        s = g["s_out"].reshape(g["n"], RET_HEADS, RET_DK, RET_DV)[None]
        return s, k, v

    sp, kp, vp = caches(groups[0])
    ss, ks, vs = caches(groups[1])
    return (outs[0], outs[1], sp, kp, vp, ss, ks, vs)
```

```python
import functools

import numpy as np
import jax
import jax.numpy as jnp
from jax import lax
from jax.experimental import pallas as pl
from jax.experimental.pallas import tpu as pltpu

F32 = jnp.float32
BF16 = jnp.bfloat16

D_MODEL = 1024
PAST_LEN = 4096
CHUNK = 64
N_META = 16
RET_HEADS = 4
RET_DK = 64
RET_DV = 128
SWA_HEADS = 8
SWA_KV_HEADS = 2
SWA_HD = 64
WINDOW = 128
ROPE_THETA = 10000.0
N_GROUPS = 4
EXPERTS_PER_GROUP = 8
N_EXPERTS = N_GROUPS * EXPERTS_PER_GROUP
EXPERT_FF = 256
EPS = 1e-6
NEG_INF = -1e30
LOG2E = float(np.log2(np.e))
RET_Q = RET_HEADS * RET_DK
RET_V = RET_HEADS * RET_DV
SWA_Q = SWA_HEADS * SWA_HD
SWA_KV = SWA_KV_HEADS * SWA_HD
MIX_WIDTH = RET_V + SWA_Q
IN_WIDTH = 2 * RET_Q + 2 * RET_V + SWA_Q + 2 * SWA_KV

LANES = 128
PROJ_TILE = 1024
POST_TILE = 1024
ATTN_TILE = 256
EXPERT_TILE = 1024
MOVE_TILE = 512
SUBLANES = 8
ROUTER_EXPERT_ROW0 = 8
ROUTER_ROWS = 64
META_ROWS = 32
VMEM_LIMIT = 56 * 1024 * 1024

_LOG_G = [float(np.log1p(-np.exp2(-5.0 - h))) for h in range(RET_HEADS)]


def _params(n_axes):
    return pltpu.CompilerParams(dimension_semantics=("arbitrary",) * n_axes, vmem_limit_bytes=VMEM_LIMIT)


def _split_bf16(a):
    hi = a.astype(BF16)
    return hi, (a - hi.astype(F32)).astype(BF16)


def _split_dot(a, w2):
    hi, lo = _split_bf16(a)
    return jnp.dot(jnp.concatenate([hi, lo], axis=1), w2, preferred_element_type=F32)


def _lane_sum(a):
    return _split_dot(a, jnp.ones((2 * LANES, LANES), BF16))


def _head_sum_matrix():
    i = lax.broadcasted_iota(jnp.int32, (2 * LANES, LANES), 0) % LANES
    j = lax.broadcasted_iota(jnp.int32, (2 * LANES, LANES), 1)
    return jnp.where((i < SWA_HD) == (j < SWA_HD), 1.0, 0.0).astype(BF16)


def _rope(t, c, s1, s2):
    half = SWA_HD // 2
    return t * c + pltpu.roll(t, LANES - half, 1) * s1 + pltpu.roll(t, half, 1) * s2


def _head_rms(t, g, head_w):
    ms = _split_dot(t * t, head_w) * (1.0 / SWA_HD)
    return t * lax.rsqrt(ms + EPS) * g


def _head_rms_pair(ta, tb, g, pair_w):
    ss = jnp.concatenate([(ta * ta).astype(BF16), (tb * tb).astype(BF16)], axis=1)
    ms = jnp.dot(ss, pair_w, preferred_element_type=F32) * (1.0 / SWA_HD)
    return ta * lax.rsqrt(ms[:, 0:LANES] + EPS) * g, tb * lax.rsqrt(ms[:, LANES:2 * LANES] + EPS) * g


def _row_rms(x):
    n_tiles = x.shape[1] // LANES
    ss = x[:, 0:LANES] * x[:, 0:LANES]
    for j in range(1, n_tiles):
        ss = ss + x[:, j * LANES:(j + 1) * LANES] * x[:, j * LANES:(j + 1) * LANES]
    r = lax.rsqrt(_lane_sum(ss) * (1.0 / x.shape[1]) + EPS)
    return x * jnp.concatenate([r] * n_tiles, axis=1)


def _proj_kernel(x_ref, g1_ref, w_ref, qg_ref, kg_ref, cos_ref, s1_ref, s2_ref,
                 rqk_ref, rv_ref, gate_ref, sq_ref, skv_ref):
    xn = (_row_rms(x_ref[...]) * g1_ref[...]).astype(BF16)
    c, s1, s2 = cos_ref[...], s1_ref[...], s2_ref[...]
    head_w = _head_sum_matrix()

    def seg(a, b):
        return jnp.dot(xn, w_ref[:, a:b], preferred_element_type=F32)

    def tile(h, j):
        return h[:, j * LANES:(j + 1) * LANES]

    h = seg(0, 2 * RET_Q)
    for j in range(2):
        rqk_ref[:, j * LANES:(j + 1) * LANES] = _rope(tile(h, j), c, s1, s2).astype(BF16)
    for j in range(2, 4):
        rqk_ref[:, j * LANES:(j + 1) * LANES] = (_rope(tile(h, j), c, s1, s2) * (RET_DK ** -0.5)).astype(BF16)
    a = 2 * RET_Q
    rv_ref[...] = seg(a, a + RET_V).astype(BF16)
    a += RET_V
    g = seg(a, a + RET_V)
    gate_ref[...] = (g * jax.nn.sigmoid(g)).astype(BF16)
    a += RET_V
    h = seg(a, a + SWA_Q)
    qg = qg_ref[...]
    pair_w = jnp.concatenate([jnp.concatenate([head_w[0:LANES], jnp.zeros((LANES, LANES), BF16)], axis=1),
                              jnp.concatenate([jnp.zeros((LANES, LANES), BF16), head_w[0:LANES]], axis=1)], axis=0)
    for j in range(0, SWA_Q // LANES, 2):
        qa, qb = _head_rms_pair(tile(h, j), tile(h, j + 1), qg, pair_w)
        sq_ref[:, j * LANES:(j + 1) * LANES] = _rope(qa, c, s1, s2).astype(BF16)
        sq_ref[:, (j + 1) * LANES:(j + 2) * LANES] = _rope(qb, c, s1, s2).astype(BF16)
    a += SWA_Q
    h = seg(a, a + 2 * SWA_KV)
    skv_ref[:, 0:LANES] = _rope(_head_rms(tile(h, 0), kg_ref[...], head_w), c, s1, s2)
    skv_ref[:, LANES:2 * LANES] = tile(h, 1)


def _rope_tables(pos):
    half = SWA_HD // 2
    inv = ROPE_THETA ** (-jnp.arange(half, dtype=F32) / half)
    ang = pos.astype(F32)[:, None] * inv[None, :]
    cos, sin = jnp.cos(ang), jnp.sin(ang)
    z = jnp.zeros_like(sin)
    heads_per_tile = LANES // SWA_HD
    return (jnp.tile(cos, (1, 2 * heads_per_tile)),
            jnp.tile(jnp.concatenate([-sin, z], axis=1), (1, heads_per_tile)),
            jnp.tile(jnp.concatenate([z, sin], axis=1), (1, heads_per_tile)))


def _proj(x2d, pos_rows, tm, g1, w_in_bf, qg2, kg2):
    t_rows = x2d.shape[0]
    n_tiles = t_rows // tm
    n_pos_tiles = pos_rows.shape[0] // tm
    cos, s1, s2 = _rope_tables(pos_rows)
    row = lambda i: (i, 0)
    const = lambda i: (0, 0)
    tab = lambda i: (i % n_pos_tiles, 0)
    return pl.pallas_call(
        _proj_kernel,
        grid=(n_tiles,),
        in_specs=[pl.BlockSpec((tm, D_MODEL), row),
                  pl.BlockSpec((1, D_MODEL), const),
                  pl.BlockSpec((D_MODEL, IN_WIDTH), const),
                  pl.BlockSpec((1, LANES), const),
                  pl.BlockSpec((1, LANES), const),
                  pl.BlockSpec((tm, LANES), tab),
                  pl.BlockSpec((tm, LANES), tab),
                  pl.BlockSpec((tm, LANES), tab)],
        out_specs=[pl.BlockSpec((tm, 2 * RET_Q), row),
                   pl.BlockSpec((tm, RET_V), row),
                   pl.BlockSpec((tm, RET_V), row),
                   pl.BlockSpec((tm, SWA_Q), row),
                   pl.BlockSpec((tm, 2 * SWA_KV), row)],
        out_shape=[jax.ShapeDtypeStruct((t_rows, 2 * RET_Q), BF16),
                   jax.ShapeDtypeStruct((t_rows, RET_V), BF16),
                   jax.ShapeDtypeStruct((t_rows, RET_V), BF16),
                   jax.ShapeDtypeStruct((t_rows, SWA_Q), BF16),
                   jax.ShapeDtypeStruct((t_rows, 2 * SWA_KV), F32)],
        compiler_params=_params(1),
        name="proj",
    )(x2d, g1, w_in_bf, qg2, kg2, cos, s1, s2)


def _pair_update(k_bf, v0_bf, v1_bf, wt):
    kw = (k_bf.astype(F32) * wt).astype(BF16)
    dn = (((0,), (0,)), ((), ()))
    a0 = lax.dot_general(kw, v0_bf, dn, preferred_element_type=F32)
    a1 = lax.dot_general(kw, v1_bf, dn, preferred_element_type=F32)
    top = lax.broadcasted_iota(jnp.int32, a0.shape, 0) < RET_DK
    return jnp.where(top, a0, a1)


def _decay_rows(n, pair, rows_back_from):
    i = lax.broadcasted_iota(jnp.int32, (n, LANES), 0).astype(F32)
    lane = lax.broadcasted_iota(jnp.int32, (n, LANES), 1)
    lg = jnp.where(lane < RET_DK, _LOG_G[2 * pair], _LOG_G[2 * pair + 1])
    return jnp.exp((rows_back_from - i) * lg)


def _meta_state_kernel(rqk_ref, rv_ref, s_ref, *, n_rows):
    for p in range(RET_HEADS // 2):
        k = rqk_ref[:, RET_Q + p * LANES:RET_Q + (p + 1) * LANES]
        wt = _decay_rows(n_rows, p, float(N_META - 1))
        s_ref[p] = _pair_update(k, rv_ref[:, (2 * p) * LANES:(2 * p + 1) * LANES],
                                rv_ref[:, (2 * p + 1) * LANES:(2 * p + 2) * LANES], wt)


def _meta_state(m_rqk, m_rv):
    n_rows = m_rqk.shape[0]
    return pl.pallas_call(
        functools.partial(_meta_state_kernel, n_rows=n_rows),
        out_shape=jax.ShapeDtypeStruct((RET_HEADS // 2, 2 * RET_DK, RET_DV), F32),
        name="meta_state",
    )(m_rqk, m_rv)


def _dup_halves(a, lo_mask):
    sw = pltpu.roll(a, SWA_HD, 1)
    return jnp.where(lo_mask, a, sw), jnp.where(lo_mask, sw, a)


def _attn_kernel(rqk_ref, rv_ref, gate_ref, sq_ref, skv_ref, meta_ref, hist_ref, s0_ref, rng_ref, sink_ref,
                 omix_ref, sout_ref, kvout_ref,
                 s_scr, kd_scr, vd_scr, mk_scr, mv_scr, dec_scr, wt_scr, cs_scr, gam_scr,
                 *, tl, has_hist):
    b = pl.program_id(0)
    t = pl.program_id(1)
    nt = pl.num_programs(1)
    n_chunks = tl // CHUNK
    n_pairs = RET_HEADS // 2
    lo_tl = lax.broadcasted_iota(jnp.int32, (tl, LANES), 1) < SWA_HD
    lo_c = lax.broadcasted_iota(jnp.int32, (CHUNK, LANES), 1) < SWA_HD

    @pl.when((b == 0) & (t == 0))
    def _tables():
        i = lax.broadcasted_iota(jnp.int32, (tl, tl), 0)
        j = lax.broadcasted_iota(jnp.int32, (tl, tl), 1)
        diff = (i - j).astype(F32)
        row = lax.broadcasted_iota(jnp.int32, (tl, LANES), 0).astype(F32)
        for h in range(RET_HEADS):
            dec_scr[h] = jnp.where(diff >= 0.0, jnp.exp(jnp.maximum(diff, 0.0) * _LOG_G[h]), 0.0)
            cs_scr[h] = jnp.exp((row + 1.0) * _LOG_G[h])
        top = lax.broadcasted_iota(jnp.int32, (2 * RET_DK, RET_DV), 0) < RET_DK
        for p in range(n_pairs):
            wt_scr[p] = _decay_rows(tl, p, float(tl - 1))
            gam_scr[p] = jnp.where(top, jnp.exp(jnp.float32(tl * _LOG_G[2 * p])), jnp.exp(jnp.float32(tl * _LOG_G[2 * p + 1])))
        lo_m = lax.broadcasted_iota(jnp.int32, (N_META, LANES), 1) < SWA_HD
        mk0, mk1 = _dup_halves(meta_ref[:, 0:LANES], lo_m)
        mv0, mv1 = _dup_halves(meta_ref[:, LANES:2 * LANES], lo_m)
        mk_scr[...] = jnp.zeros(mk_scr.shape, BF16)
        mv_scr[...] = jnp.zeros(mv_scr.shape, BF16)
        mk_scr[0, 0:N_META] = mk0.astype(BF16)
        mk_scr[1, 0:N_META] = mk1.astype(BF16)
        mv_scr[0, 0:N_META] = mv0.astype(BF16)
        mv_scr[1, 0:N_META] = mv1.astype(BF16)

    @pl.when(t == 0)
    def _stream_start():
        s_scr[...] = s0_ref[0]
        if has_hist:
            lo_w = lax.broadcasted_iota(jnp.int32, (WINDOW, LANES), 1) < SWA_HD
            k0, k1 = _dup_halves(hist_ref[0, :, 0:LANES], lo_w)
            v0, v1 = _dup_halves(hist_ref[0, :, LANES:2 * LANES], lo_w)
            kd_scr[0, 0:WINDOW] = k0.astype(BF16)
            kd_scr[1, 0:WINDOW] = k1.astype(BF16)
            vd_scr[0, 0:WINDOW] = v0.astype(BF16)
            vd_scr[1, 0:WINDOW] = v1.astype(BF16)
        else:
            z = jnp.zeros((WINDOW, LANES), BF16)
            for kv in range(SWA_KV_HEADS):
                kd_scr[kv, 0:WINDOW] = z
                vd_scr[kv, 0:WINDOW] = z

    k0, k1 = _dup_halves(skv_ref[:, 0:LANES], lo_tl)
    v0, v1 = _dup_halves(skv_ref[:, LANES:2 * LANES], lo_tl)
    kd_scr[0, WINDOW:WINDOW + tl] = k0.astype(BF16)
    kd_scr[1, WINDOW:WINDOW + tl] = k1.astype(BF16)
    vd_scr[0, WINDOW:WINDOW + tl] = v0.astype(BF16)
    vd_scr[1, WINDOW:WINDOW + tl] = v1.astype(BF16)

    band = WINDOW + CHUNK
    n_keys = META_ROWS + band
    n_q = 4 * CHUNK
    scale2 = (SWA_HD ** -0.5) * LOG2E
    krow = lax.broadcasted_iota(jnp.int32, (n_keys, n_q), 0)
    zero_c = jnp.zeros((CHUNK, LANES), BF16)
    ones_v = jnp.ones((n_keys, LANES), BF16)
    for c in range(n_chunks):
        if has_hist:
            first_valid = META_ROWS
        else:
            first_valid = jnp.where(t == 0, max(META_ROWS + WINDOW - c * CHUNK, META_ROWS), META_ROWS)
        valid_t = (krow < N_META) | (krow >= first_valid)
        r0 = c * CHUNK
        for kv in range(SWA_KV_HEADS):
            keys = jnp.concatenate([mk_scr[kv], kd_scr[kv, r0:r0 + band]], axis=0)
            vals = jnp.concatenate([mv_scr[kv], vd_scr[kv, r0:r0 + band]], axis=0)
            qa = sq_ref[r0:r0 + CHUNK, (2 * kv) * LANES:(2 * kv + 1) * LANES]
            qb = sq_ref[r0:r0 + CHUNK, (2 * kv + 1) * LANES:(2 * kv + 2) * LANES]
            lhs = jnp.concatenate([jnp.where(lo_c, qa, zero_c), jnp.where(lo_c, zero_c, qa),
                                   jnp.where(lo_c, qb, zero_c), jnp.where(lo_c, zero_c, qb)], axis=0)
            s_t = lax.dot_general(keys, lhs, (((1,), (1,)), ((), ())), preferred_element_type=F32) * scale2
            s_t = jnp.where(valid_t, s_t, NEG_INF)
            s_t = jnp.where(krow == N_META, sink_ref[kv, 0:1, :] * LOG2E, s_t)
            e_t = jnp.exp2(s_t - jnp.max(s_t, axis=0, keepdims=True)).astype(BF16)
            ov = lax.dot_general(e_t, jnp.concatenate([vals, ones_v], axis=1), (((0,), (0,)), ((), ())),
                                 preferred_element_type=F32)
            o = ov[:, 0:LANES] * (1.0 / ov[:, LANES:2 * LANES])
            oa = jnp.where(lo_c, o[0:CHUNK], o[CHUNK:2 * CHUNK])
            ob = jnp.where(lo_c, o[2 * CHUNK:3 * CHUNK], o[3 * CHUNK:4 * CHUNK])
            base = RET_V + (2 * kv) * LANES
            omix_ref[r0:r0 + CHUNK, base:base + LANES] = oa.astype(BF16)
            omix_ref[r0:r0 + CHUNK, base + LANES:base + 2 * LANES] = ob.astype(BF16)

    zero_t = jnp.zeros((tl, LANES), BF16)
    rr = lax.broadcasted_iota(jnp.int32, (2 * LANES, 2 * LANES), 0) < LANES
    cc = lax.broadcasted_iota(jnp.int32, (2 * LANES, 2 * LANES), 1) < LANES
    ones_pair = jnp.where(rr == cc, 1.0, 0.0).astype(BF16)
    for p in range(n_pairs):
        q = rqk_ref[:, p * LANES:(p + 1) * LANES]
        k = rqk_ref[:, RET_Q + p * LANES:RET_Q + (p + 1) * LANES]
        lhs = jnp.concatenate([jnp.where(lo_tl, q, zero_t), jnp.where(lo_tl, zero_t, q)], axis=0)
        s = lax.dot_general(lhs, k, (((1,), (1,)), ((), ())), preferred_element_type=F32)
        cross = jnp.dot(lhs, s_scr[p].astype(BF16), preferred_element_type=F32)
        outs = []
        for i in range(2):
            h = 2 * p + i
            v = rv_ref[:, h * LANES:(h + 1) * LANES]
            a = (s[i * tl:(i + 1) * tl] * dec_scr[h]).astype(BF16)
            outs.append(jnp.dot(a, v, preferred_element_type=F32) + cross[i * tl:(i + 1) * tl] * cs_scr[h])
        ss = jnp.concatenate([(o * o).astype(BF16) for o in outs], axis=1)
        ms = jnp.dot(ss, ones_pair, preferred_element_type=F32) * (1.0 / RET_DV)
        for i, o in enumerate(outs):
            h = 2 * p + i
            r = o * lax.rsqrt(ms[:, i * LANES:(i + 1) * LANES] + EPS) * rng_ref[h:h + 1, :]
            omix_ref[:, h * LANES:(h + 1) * LANES] = (r * gate_ref[:, h * LANES:(h + 1) * LANES].astype(F32)).astype(BF16)
        u = _pair_update(k, rv_ref[:, (2 * p) * LANES:(2 * p + 1) * LANES],
                         rv_ref[:, (2 * p + 1) * LANES:(2 * p + 2) * LANES], wt_scr[p])
        s_scr[p] = gam_scr[p] * s_scr[p] + u

    if tl >= WINDOW:
        @pl.when(t + 1 < nt)
        def _carry_window():
            for kv in range(SWA_KV_HEADS):
                kd_scr[kv, 0:WINDOW] = kd_scr[kv, tl:tl + WINDOW]
                vd_scr[kv, 0:WINDOW] = vd_scr[kv, tl:tl + WINDOW]

    @pl.when(t + 1 == nt)
    def _stream_end():
        sout_ref[0] = s_scr[...]
        if tl >= WINDOW:
            kvout_ref[0] = skv_ref[tl - WINDOW:tl, :]
        else:
            kvout_ref[0, 0:WINDOW - tl] = hist_ref[0, tl:WINDOW, :]
            kvout_ref[0, WINDOW - tl:WINDOW] = skv_ref[...]


def _attention(rqk, rv, gate, sq, skv, meta_kv, hist_kv, s0, rng, sink_tab, *, n_streams, seq, tl, has_hist):
    nt = seq // tl
    assert tl % CHUNK == 0 and seq % tl == 0
    assert tl >= WINDOW or (nt == 1 and has_hist)
    n_pairs = RET_HEADS // 2
    s0_shared = s0.shape[0] == 1
    row = lambda b, t: (b * nt + t, 0)
    const2 = lambda b, t: (0, 0)
    const3 = lambda b, t: (0, 0, 0)
    per_b3 = lambda b, t: (b, 0, 0)
    s0_map = (lambda b, t: (0, 0, 0, 0)) if s0_shared else (lambda b, t: (b, 0, 0, 0))
    hist_map = per_b3 if has_hist else const3
    rows = n_streams * seq
    return pl.pallas_call(
        functools.partial(_attn_kernel, tl=tl, has_hist=has_hist),
        grid=(n_streams, nt),
        in_specs=[pl.BlockSpec((tl, 2 * RET_Q), row),
                  pl.BlockSpec((tl, RET_V), row),
                  pl.BlockSpec((tl, RET_V), row),
                  pl.BlockSpec((tl, SWA_Q), row),
                  pl.BlockSpec((tl, 2 * SWA_KV), row),
                  pl.BlockSpec((N_META, 2 * SWA_KV), const2),
                  pl.BlockSpec((1, WINDOW, 2 * SWA_KV), hist_map),
                  pl.BlockSpec((1, n_pairs, 2 * RET_DK, RET_DV), s0_map),
                  pl.BlockSpec((RET_HEADS, RET_DV), const2),
                  pl.BlockSpec((SWA_KV_HEADS, SUBLANES, 4 * CHUNK), const3)],
        out_specs=[pl.BlockSpec((tl, MIX_WIDTH), row),
                   pl.BlockSpec((1, n_pairs, 2 * RET_DK, RET_DV), lambda b, t: (b, 0, 0, 0)),
                   pl.BlockSpec((1, WINDOW, 2 * SWA_KV), per_b3)],
        out_shape=[jax.ShapeDtypeStruct((rows, MIX_WIDTH), BF16),
                   jax.ShapeDtypeStruct((n_streams, n_pairs, 2 * RET_DK, RET_DV), F32),
                   jax.ShapeDtypeStruct((n_streams, WINDOW, 2 * SWA_KV), F32)],
        scratch_shapes=[pltpu.VMEM((n_pairs, 2 * RET_DK, RET_DV), F32),
                        pltpu.VMEM((SWA_KV_HEADS, WINDOW + tl, LANES), BF16),
                        pltpu.VMEM((SWA_KV_HEADS, WINDOW + tl, LANES), BF16),
                        pltpu.VMEM((SWA_KV_HEADS, META_ROWS, LANES), BF16),
                        pltpu.VMEM((SWA_KV_HEADS, META_ROWS, LANES), BF16),
                        pltpu.VMEM((RET_HEADS, tl, tl), F32),
                        pltpu.VMEM((n_pairs, tl, LANES), F32),
                        pltpu.VMEM((RET_HEADS, tl, RET_DV), F32),
                        pltpu.VMEM((n_pairs, 2 * RET_DK, RET_DV), F32)],
        compiler_params=_params(2),
        name="attention",
    )(rqk, rv, gate, sq, skv, meta_kv, hist_kv, s0, rng, sink_tab)


def _post_kernel(omix_ref, x_ref, wout_ref, g2_ref, wrt_ref, base_ref,
                 xmid_ref, wcol_ref, rt_ref, cnt_ref, tri_scr, run_scr):
    i = pl.program_id(0)
    tm = x_ref.shape[0]

    @pl.when(i == 0)
    def _init():
        r = lax.broadcasted_iota(jnp.int32, (tm, tm), 0)
        c = lax.broadcasted_iota(jnp.int32, (tm, tm), 1)
        tri_scr[...] = jnp.where(r < c, 1.0, 0.0).astype(BF16)
        run_scr[...] = base_ref[...]

    xm = x_ref[...] + jnp.dot(omix_ref[...], wout_ref[...], preferred_element_type=F32)
    xmid_ref[...] = xm
    hn = _row_rms(xm) * g2_ref[...]
    lt = lax.dot_general(wrt_ref[...], hn.astype(BF16), (((1,), (1,)), ((), ())), preferred_element_type=F32)
    row8 = lax.broadcasted_iota(jnp.int32, (SUBLANES, tm), 0)
    big = jnp.int32(SUBLANES)
    gl = jnp.where(row8 < N_GROUPS, lt[0:SUBLANES], NEG_INF)
    gmax = jnp.max(gl, axis=0, keepdims=True)
    gsum = jnp.sum(jnp.exp(gl - gmax), axis=0, keepdims=True)
    g_sel = jnp.min(jnp.where(gl == gmax, row8, big), axis=0, keepdims=True)
    p_sel = 1.0 / gsum
    el = lt[ROUTER_EXPERT_ROW0:ROUTER_EXPERT_ROW0 + EXPERTS_PER_GROUP]
    for g in range(1, N_GROUPS):
        lo = ROUTER_EXPERT_ROW0 + g * EXPERTS_PER_GROUP
        el = jnp.where(g_sel == g, lt[lo:lo + EXPERTS_PER_GROUP], el)
    m1 = jnp.max(el, axis=0, keepdims=True)
    i1 = jnp.min(jnp.where(el == m1, row8, big), axis=0, keepdims=True)
    el2 = jnp.where(row8 == i1, NEG_INF, el)
    m2 = jnp.max(el2, axis=0, keepdims=True)
    i2 = jnp.min(jnp.where(el2 == m2, row8, big), axis=0, keepdims=True)
    e2 = jnp.exp(m2 - m1)
    inv = 1.0 / (1.0 + e2)
    w1 = p_sel * inv
    w2 = p_sel * (e2 * inv)
    eid1 = g_sel * EXPERTS_PER_GROUP + i1
    eid2 = g_sel * EXPERTS_PER_GROUP + i2

    rowe = lax.broadcasted_iota(jnp.int32, (N_EXPERTS, tm), 0)
    oh = jnp.where((rowe == eid1) | (rowe == eid2), 1.0, 0.0).astype(BF16)
    run = run_scr[...]
    pref = jnp.dot(oh, tri_scr[...], preferred_element_type=F32) + jnp.concatenate([run] * (tm // LANES), axis=1)
    r1 = jnp.sum(jnp.where(rowe == eid1, pref, 0.0), axis=0, keepdims=True)
    r2 = jnp.sum(jnp.where(rowe == eid2, pref, 0.0), axis=0, keepdims=True)
    run = run + jnp.dot(oh, jnp.ones((tm, LANES), BF16), preferred_element_type=F32)
    run_scr[...] = run
    cnt_ref[...] = run

    out = jnp.zeros((SUBLANES, tm), F32)
    for k, v in enumerate([eid1.astype(F32), eid2.astype(F32), w1, w2, r1, r2]):
        out = jnp.where(row8 == k, v, out)
    rt_ref[...] = out
    rowl = lax.broadcasted_iota(jnp.int32, (LANES, tm), 0)
    wcol_ref[...] = jnp.where(rowl == 0, w1, jnp.where(rowl == 1, w2, 0.0)).T


def _post(omix, x2d, w_out_bf, g2, w_router_t_bf, base_cnt, tm):
    t_rows = x2d.shape[0]
    assert tm % LANES == 0
    row = lambda i: (i, 0)
    const = lambda i: (0, 0)
    return pl.pallas_call(
        _post_kernel,
        grid=(t_rows // tm,),
        in_specs=[pl.BlockSpec((tm, MIX_WIDTH), row),
                  pl.BlockSpec((tm, D_MODEL), row),
                  pl.BlockSpec((MIX_WIDTH, D_MODEL), const),
                  pl.BlockSpec((1, D_MODEL), const),
                  pl.BlockSpec((ROUTER_ROWS, D_MODEL), const),
                  pl.BlockSpec((N_EXPERTS, LANES), const)],
        out_specs=[pl.BlockSpec((tm, D_MODEL), row),
                   pl.BlockSpec((tm, LANES), row),
                   pl.BlockSpec((SUBLANES, tm), lambda i: (0, i)),
                   pl.BlockSpec((N_EXPERTS, LANES), const)],
        out_shape=[jax.ShapeDtypeStruct((t_rows, D_MODEL), F32),
                   jax.ShapeDtypeStruct((t_rows, LANES), F32),
                   jax.ShapeDtypeStruct((SUBLANES, t_rows), F32),
                   jax.ShapeDtypeStruct((N_EXPERTS, LANES), F32)],
        scratch_shapes=[pltpu.VMEM((tm, tm), BF16), pltpu.VMEM((N_EXPERTS, LANES), F32)],
        compiler_params=_params(1),
        name="post",
    )(omix, x2d, w_out_bf, g2, w_router_t_bf, base_cnt)


def _step_major(pos, tm):
    return [pos[e].reshape(-1, 1, tm) for e in range(2)]


def _dispatch_kernel(pos0_ref, pos1_ref, *refs, tm, group_steps):
    pos_refs = (pos0_ref, pos1_ref)
    n_g = len(group_steps)
    xmid_refs = refs[:n_g]
    g2_ref, xs_ref, hbuf, sems = refs[n_g:]
    n_steps = sum(group_steps)
    i = pl.program_id(0)
    slot = i % 2

    first = 0
    for xmid_ref, steps in zip(xmid_refs, group_steps):
        @pl.when((i >= first) & (i < first + steps))
        def _normalise(xmid_ref=xmid_ref):
            hn = _row_rms(xmid_ref[...]) * g2_ref[...]
            hbuf[slot] = hn.reshape(tm // SUBLANES, SUBLANES, D_MODEL)
        first += steps

    def body(blk, carry):
        for k in range(SUBLANES):
            r = blk * SUBLANES + k
            for e in range(2):
                pltpu.make_async_copy(hbuf.at[slot, blk, pl.ds(k, 1)], xs_ref.at[pl.ds(pos_refs[e][0, 0, r], 1)],
                                      sems.at[slot]).start(priority=e)
        return carry

    lax.fori_loop(0, tm // SUBLANES, body, 0)

    def drain(which):
        pltpu.make_async_copy(xs_ref.at[pl.ds(0, 2 * tm)], xs_ref.at[pl.ds(0, 2 * tm)], sems.at[which]).wait()

    @pl.when(i > 0)
    def _previous():
        drain(1 - slot)

    @pl.when(i == n_steps - 1)
    def _last():
        drain(slot)


def _dispatch(pos_list, xmid_list, g2, tm):
    group_steps = tuple(x.shape[0] // tm for x in xmid_list)
    n_steps = sum(group_steps)
    assert tm % SUBLANES == 0
    pos3 = [jnp.concatenate(parts, axis=0) for parts in zip(*[_step_major(p, tm) for p in pos_list])]
    smem = pl.BlockSpec((1, 1, tm), lambda i: (i, 0, 0), memory_space=pltpu.SMEM)
    x_specs = []
    first = 0
    for steps in group_steps:
        x_specs.append(pl.BlockSpec((tm, D_MODEL),
                                    lambda i, first=first, steps=steps: (jnp.clip(i - first, 0, steps - 1), 0)))
        first += steps
    n_rows_out = 2 * sum(x.shape[0] for x in xmid_list)
    return pl.pallas_call(
        functools.partial(_dispatch_kernel, tm=tm, group_steps=group_steps),
        grid=(n_steps,),
        in_specs=[smem, smem] + x_specs + [pl.BlockSpec((1, D_MODEL), lambda i: (0, 0))],
        out_specs=pl.BlockSpec(memory_space=pl.ANY),
        out_shape=jax.ShapeDtypeStruct((n_rows_out, D_MODEL), F32),
        scratch_shapes=[pltpu.VMEM((2, tm // SUBLANES, SUBLANES, D_MODEL), F32), pltpu.SemaphoreType.DMA((2,))],
        compiler_params=_params(1),
        name="dispatch",
    )(*pos3, *xmid_list, g2)


def _expert_kernel(vt_ref, ve_ref, lo_ref, hi_ref, nv_ref, x_ref, wg_ref, wu_ref, wd_ref, y_ref, wgu_scr, wd_scr):
    v = pl.program_id(0)

    @pl.when(v < nv_ref[0])
    def _compute():
        @pl.when((v == 0) | (ve_ref[v] != ve_ref[jnp.maximum(v - 1, 0)]))
        def _new_expert():
            wgu_scr[:, 0:EXPERT_FF] = wg_ref[0].astype(BF16)
            wgu_scr[:, EXPERT_FF:2 * EXPERT_FF] = wu_ref[0].astype(BF16)
            wd_scr[...] = wd_ref[0].astype(BF16)

        x = x_ref[...].astype(BF16)
        gu = jnp.dot(x, wgu_scr[...], preferred_element_type=F32)
        g = gu[:, 0:EXPERT_FF]
        a = (g * jax.nn.sigmoid(g) * gu[:, EXPERT_FF:2 * EXPERT_FF]).astype(BF16)
        y = jnp.dot(a, wd_scr[...], preferred_element_type=F32)
        lo, hi = lo_ref[v], hi_ref[v]

        @pl.when(lo == 0)
        def _first_visit():
            y_ref[...] = y

        @pl.when(lo > 0)
        def _later_visit():
            row = lax.broadcasted_iota(jnp.int32, y.shape, 0)
            y_ref[...] = jnp.where((row >= lo) & (row < hi), y, y_ref[...])


def _experts(vis_tile, vis_expert, vis_lo, vis_hi, n_vis, xs, w_gate, w_up, w_down, tm):
    n_steps = vis_tile.shape[0]
    last = lambda v, nv: jnp.minimum(v, nv[0] - 1)
    of_expert = lambda v, vt, ve, lo, hi, nv: (ve[last(v, nv)], 0, 0)
    return pl.pallas_call(
        _expert_kernel,
        grid_spec=pltpu.PrefetchScalarGridSpec(
            num_scalar_prefetch=5,
            grid=(n_steps,),
            in_specs=[pl.BlockSpec((tm, D_MODEL), lambda v, vt, ve, lo, hi, nv: (vt[last(v, nv)], 0)),
                      pl.BlockSpec((1, D_MODEL, EXPERT_FF), of_expert),
                      pl.BlockSpec((1, D_MODEL, EXPERT_FF), of_expert),
                      pl.BlockSpec((1, EXPERT_FF, D_MODEL), of_expert)],
            out_specs=pl.BlockSpec((tm, D_MODEL), lambda v, vt, ve, lo, hi, nv: (vt[last(v, nv)], 0)),
            scratch_shapes=[pltpu.VMEM((D_MODEL, 2 * EXPERT_FF), BF16), pltpu.VMEM((EXPERT_FF, D_MODEL), BF16)]),
        out_shape=jax.ShapeDtypeStruct(xs.shape, F32),
        compiler_params=_params(1),
        name="experts",
    )(vis_tile, vis_expert, vis_lo, vis_hi, n_vis, xs, w_gate, w_up, w_down)


def _combine_kernel(pos0_ref, pos1_ref, nxt0_ref, nxt1_ref, ys_ref, xmid_ref, wcol_ref, out_ref, ybuf, sems,
                    *, tm, n_steps):
    i = pl.program_id(0)
    slot = i % 2

    def issue(p_refs, to_slot):
        def body(blk, carry):
            for k in range(SUBLANES):
                r = blk * SUBLANES + k
                for e in range(2):
                    pltpu.make_async_copy(ys_ref.at[pl.ds(p_refs[e][0, 0, r], 1)],
                                          ybuf.at[to_slot, e, blk, pl.ds(k, 1)], sems.at[to_slot]).start(priority=e)
            return carry

        lax.fori_loop(0, tm // SUBLANES, body, 0)

    @pl.when(i == 0)
    def _first():
        issue((pos0_ref, pos1_ref), 0)

    @pl.when(i + 1 < n_steps)
    def _ahead():
        issue((nxt0_ref, nxt1_ref), 1 - slot)

    for e in range(2):
        pltpu.make_async_copy(ybuf.at[slot, e], ybuf.at[slot, e], sems.at[slot]).wait()
    w = wcol_ref[...]
    y0 = ybuf[slot, 0].reshape(tm, D_MODEL)
    y1 = ybuf[slot, 1].reshape(tm, D_MODEL)
    out_ref[...] = xmid_ref[...] + w[:, 0:1] * y0 + w[:, 1:2] * y1


def _combine(pos, ys, xmid, wcol, tm):
    t_rows = xmid.shape[0]
    n_steps = t_rows // tm
    assert tm % SUBLANES == 0
    pos3 = _step_major(pos, tm)
    row = lambda i: (i, 0)
    cur = pl.BlockSpec((1, 1, tm), lambda i: (i, 0, 0), memory_space=pltpu.SMEM)
    nxt = pl.BlockSpec((1, 1, tm), lambda i: (jnp.minimum(i + 1, n_steps - 1), 0, 0), memory_space=pltpu.SMEM)
    return pl.pallas_call(
        functools.partial(_combine_kernel, tm=tm, n_steps=n_steps),
        grid=(n_steps,),
        in_specs=[cur, cur, nxt, nxt,
                  pl.BlockSpec(memory_space=pl.ANY),
                  pl.BlockSpec((tm, D_MODEL), row),
                  pl.BlockSpec((tm, LANES), row)],
        out_specs=pl.BlockSpec((tm, D_MODEL), row),
        out_shape=jax.ShapeDtypeStruct((t_rows, D_MODEL), F32),
        scratch_shapes=[pltpu.VMEM((2, 2, tm // SUBLANES, SUBLANES, D_MODEL), F32), pltpu.SemaphoreType.DMA((2,))],
        compiler_params=_params(1),
        name="combine",
    )(*pos3, *pos3, ys, xmid, wcol)


def _bucket(ends, idx):
    n = jnp.sum((ends[None, :] <= idx[:, None]).astype(jnp.int32), axis=1)
    return jnp.minimum(n, ends.shape[0] - 1)


def _tile_for(rows, pref):
    tm = min(pref, rows)
    assert rows % tm == 0
    return tm


def kernel(x_prompt, x_sample, cache_ret_state, cache_swa_k, cache_swa_v, meta_tokens, norm1_g, w_in, q_norm_g,
           k_norm_g, ret_norm_g, attn_sinks, w_out, norm2_g, w_group, w_expert, w_gate, w_up, w_down):
    assert norm1_g.shape[0] == 1, "single-layer trunk"
    bp, lp, _ = x_prompt.shape
    bs, ls, _ = x_sample.shape
    n_pairs = RET_HEADS // 2

    g1 = norm1_g[0][None, :]
    g2 = norm2_g[0][None, :]
    w_in_bf = w_in[0].astype(BF16)
    w_out_bf = w_out[0].astype(BF16)
    qg2 = jnp.tile(q_norm_g[0], 2)[None, :]
    kg2 = jnp.tile(k_norm_g[0], 2)[None, :]
    rng = ret_norm_g[0].reshape(RET_HEADS, RET_DV)
    sink_tab = jnp.broadcast_to(jnp.repeat(attn_sinks[0], CHUNK).reshape(SWA_KV_HEADS, 1, 4 * CHUNK),
                                (SWA_KV_HEADS, SUBLANES, 4 * CHUNK))
    w_router_t = jnp.zeros((ROUTER_ROWS, D_MODEL), F32)
    w_router_t = w_router_t.at[0:N_GROUPS].set(w_group[0].T)
    w_router_t = w_router_t.at[ROUTER_EXPERT_ROW0:ROUTER_EXPERT_ROW0 + N_EXPERTS].set(w_expert[0].T)
    w_router_t_bf = w_router_t.astype(BF16)

    meta_rows = 2 * CHUNK
    m_pad = jnp.zeros((meta_rows, D_MODEL), F32).at[0:N_META].set(meta_tokens)
    m_rqk, m_rv, _, _, m_skv = _proj(m_pad, jnp.arange(meta_rows, dtype=jnp.int32), meta_rows, g1, w_in_bf, qg2, kg2)
    s_meta = _meta_state(m_rqk, m_rv)[None]
    meta_kv = m_skv[0:N_META]

    groups = [
        dict(x=x_prompt.reshape(bp * lp, D_MODEL), n=bp, seq=lp, pos0=N_META, has_hist=False, s0=s_meta,
             hist=jnp.zeros((1, WINDOW, 2 * SWA_KV), F32)),
        dict(x=x_sample.reshape(bs * ls, D_MODEL), n=bs, seq=ls, pos0=N_META + PAST_LEN, has_hist=True,
             s0=cache_ret_state[0].reshape(bs, n_pairs, 2 * RET_DK, RET_DV),
             hist=jnp.concatenate([cache_swa_k[0].reshape(bs, WINDOW, SWA_KV),
                                   cache_swa_v[0].reshape(bs, WINDOW, SWA_KV)], axis=-1)),
    ]

    base_cnt = jnp.zeros((N_EXPERTS, LANES), F32)
    for g in groups:
        rows = g["n"] * g["seq"]
        tm = _tile_for(rows, PROJ_TILE)
        pos = g["pos0"] + jnp.arange(g["seq"], dtype=jnp.int32)
        if g["seq"] < tm:
            assert tm % g["seq"] == 0
            pos = jnp.tile(pos, tm // g["seq"])
        else:
            assert g["seq"] % tm == 0
        rqk, rv, gate, sq, skv = _proj(g["x"], pos, tm, g1, w_in_bf, qg2, kg2)
        tl = min(ATTN_TILE, g["seq"])
        omix, s_out, kv_out = _attention(rqk, rv, gate, sq, skv, meta_kv, g["hist"], g["s0"], rng, sink_tab,
                                         n_streams=g["n"], seq=g["seq"], tl=tl, has_hist=g["has_hist"])
        xmid, wcol, route_t, base_cnt = _post(omix, g["x"], w_out_bf, g2, w_router_t_bf, base_cnt,
                                              _tile_for(rows, POST_TILE))
        g.update(xmid=xmid, wcol=wcol, route_t=route_t, s_out=s_out, kv_out=kv_out)

    te = EXPERT_TILE
    total_rows = sum(g["n"] * g["seq"] for g in groups)
    assert (2 * total_rows) % te == 0
    n_row_tiles = (2 * total_rows) // te
    counts = base_cnt[:, 0].astype(jnp.int32)
    off = jnp.cumsum(counts) - counts
    first_tile = off // te
    n_vis_e = jnp.where(counts > 0, (off + counts - 1) // te - first_tile + 1, 0)
    vis_end = jnp.cumsum(n_vis_e)
    n_vis = vis_end[-1:].astype(jnp.int32)
    v = jnp.arange(n_row_tiles + N_EXPERTS, dtype=jnp.int32)
    vis_expert = _bucket(vis_end, v)
    pick = lambda table: jnp.sum(jnp.where(vis_expert[:, None] == jnp.arange(N_EXPERTS, dtype=jnp.int32)[None, :],
                                           table[None, :], 0), axis=1)
    vis_tile = jnp.clip(pick(first_tile) + v - pick(vis_end - n_vis_e), 0, n_row_tiles - 1).astype(jnp.int32)
    vis_lo = jnp.clip(pick(off) - vis_tile * te, 0, te).astype(jnp.int32)
    vis_hi = jnp.clip(pick(off + counts) - vis_tile * te, 0, te).astype(jnp.int32)

    for g in groups:
        eid = g["route_t"][0:2].astype(jnp.int32)
        off_sel = jnp.sum(jnp.where(eid[None] == jnp.arange(N_EXPERTS, dtype=jnp.int32)[:, None, None],
                                    off[:, None, None], 0), axis=0)
        g["pos"] = (off_sel + g["route_t"][4:6].astype(jnp.int32)).astype(jnp.int32)
    xs = _dispatch([g["pos"] for g in groups], [g["xmid"] for g in groups], g2,
                   _tile_for(min(g["n"] * g["seq"] for g in groups), MOVE_TILE))

    ys = _experts(vis_tile, vis_expert.astype(jnp.int32), vis_lo, vis_hi, n_vis, xs, w_gate[0], w_up[0], w_down[0], te)

    outs = []
    for g in groups:
        rows = g["n"] * g["seq"]
        y = _combine(g["pos"], ys, g["xmid"], g["wcol"], _tile_for(rows, MOVE_TILE))
        outs.append(y.reshape(g["n"], g["seq"], D_MODEL))

    def caches(g):
        kv = g["kv_out"]
        k = kv[:, :, 0:SWA_KV].reshape(g["n"], WINDOW, SWA_KV_HEADS, SWA_HD)[None]
        v = kv[:, :, SWA_KV:2 * SWA_KV].reshape(g["n"], WINDOW, SWA_KV_HEADS, SWA_HD)[None]
        s = g["s_out"].reshape(g["n"], RET_HEADS, RET_DK, RET_DV)[None]
        return s, k, v

    sp, kp, vp = caches(groups[0])
    ss, ks, vs = caches(groups[1])
    return (outs[0], outs[1], sp, kp, vp, ss, ks, vs)
```

```python
import functools

import numpy as np
import jax
import jax.numpy as jnp
from jax import lax
from jax.experimental import pallas as pl
from jax.experimental.pallas import tpu as pltpu

F32 = jnp.float32
BF16 = jnp.bfloat16

D_MODEL = 1024
PAST_LEN = 4096
CHUNK = 64
N_META = 16
RET_HEADS = 4
RET_DK = 64
RET_DV = 128
SWA_HEADS = 8
SWA_KV_HEADS = 2
SWA_HD = 64
WINDOW = 128
ROPE_THETA = 10000.0
N_GROUPS = 4
EXPERTS_PER_GROUP = 8
N_EXPERTS = N_GROUPS * EXPERTS_PER_GROUP
EXPERT_FF = 256
EPS = 1e-6
NEG_INF = -1e30
LOG2E = float(np.log2(np.e))
RET_Q = RET_HEADS * RET_DK
RET_V = RET_HEADS * RET_DV
SWA_Q = SWA_HEADS * SWA_HD
SWA_KV = SWA_KV_HEADS * SWA_HD
MIX_WIDTH = RET_V + SWA_Q
IN_WIDTH = 2 * RET_Q + 2 * RET_V + SWA_Q + 2 * SWA_KV

LANES = 128
PROJ_TILE = 1024
POST_TILE = 1024
ATTN_TILE = 256
EXPERT_TILE = 1024
EXPERT_SUB = 256
MOVE_TILE = 512
SUBLANES = 8
ROUTER_EXPERT_ROW0 = 8
ROUTER_ROWS = 64
META_ROWS = 32
VMEM_LIMIT = 56 * 1024 * 1024

_LOG_G = [float(np.log1p(-np.exp2(-5.0 - h))) for h in range(RET_HEADS)]


def _params(n_axes):
    return pltpu.CompilerParams(dimension_semantics=("arbitrary",) * n_axes, vmem_limit_bytes=VMEM_LIMIT)


def _split_bf16(a):
    hi = a.astype(BF16)
    return hi, (a - hi.astype(F32)).astype(BF16)


def _split_dot(a, w2):
    hi, lo = _split_bf16(a)
    return jnp.dot(jnp.concatenate([hi, lo], axis=1), w2, preferred_element_type=F32)


def _lane_sum(a):
    return _split_dot(a, jnp.ones((2 * LANES, LANES), BF16))


def _head_sum_matrix():
    i = lax.broadcasted_iota(jnp.int32, (2 * LANES, LANES), 0) % LANES
    j = lax.broadcasted_iota(jnp.int32, (2 * LANES, LANES), 1)
    return jnp.where((i < SWA_HD) == (j < SWA_HD), 1.0, 0.0).astype(BF16)


def _rope(t, c, s1, s2):
    half = SWA_HD // 2
    return t * c + pltpu.roll(t, LANES - half, 1) * s1 + pltpu.roll(t, half, 1) * s2


def _head_rms(t, g, head_w):
    ms = _split_dot(t * t, head_w) * (1.0 / SWA_HD)
    return t * lax.rsqrt(ms + EPS) * g


def _head_rms_pair(ta, tb, g, pair_w):
    ss = jnp.concatenate([(ta * ta).astype(BF16), (tb * tb).astype(BF16)], axis=1)
    ms = jnp.dot(ss, pair_w, preferred_element_type=F32) * (1.0 / SWA_HD)
    return ta * lax.rsqrt(ms[:, 0:LANES] + EPS) * g, tb * lax.rsqrt(ms[:, LANES:2 * LANES] + EPS) * g


def _row_rms(x):
    n_tiles = x.shape[1] // LANES
    ss = x[:, 0:LANES] * x[:, 0:LANES]
    for j in range(1, n_tiles):
        ss = ss + x[:, j * LANES:(j + 1) * LANES] * x[:, j * LANES:(j + 1) * LANES]
    r = lax.rsqrt(_lane_sum(ss) * (1.0 / x.shape[1]) + EPS)
    return x * jnp.concatenate([r] * n_tiles, axis=1)


def _proj_kernel(x_ref, g1_ref, w_ref, qg_ref, kg_ref, cos_ref, s1_ref, s2_ref,
                 rqk_ref, rv_ref, gate_ref, sq_ref, skv_ref):
    xn = (_row_rms(x_ref[...]) * g1_ref[...]).astype(BF16)
    c, s1, s2 = cos_ref[...], s1_ref[...], s2_ref[...]
    head_w = _head_sum_matrix()

    def seg(a, b):
        return jnp.dot(xn, w_ref[:, a:b], preferred_element_type=F32)

    def tile(h, j):
        return h[:, j * LANES:(j + 1) * LANES]

    h = seg(0, 2 * RET_Q)
    for j in range(2):
        rqk_ref[:, j * LANES:(j + 1) * LANES] = _rope(tile(h, j), c, s1, s2).astype(BF16)
    for j in range(2, 4):
        rqk_ref[:, j * LANES:(j + 1) * LANES] = (_rope(tile(h, j), c, s1, s2) * (RET_DK ** -0.5)).astype(BF16)
    a = 2 * RET_Q
    rv_ref[...] = seg(a, a + RET_V).astype(BF16)
    a += RET_V
    g = seg(a, a + RET_V)
    gate_ref[...] = (g * jax.nn.sigmoid(g)).astype(BF16)
    a += RET_V
    h = seg(a, a + SWA_Q)
    qg = qg_ref[...]
    pair_w = jnp.concatenate([jnp.concatenate([head_w[0:LANES], jnp.zeros((LANES, LANES), BF16)], axis=1),
                              jnp.concatenate([jnp.zeros((LANES, LANES), BF16), head_w[0:LANES]], axis=1)], axis=0)
    for j in range(0, SWA_Q // LANES, 2):
        qa, qb = _head_rms_pair(tile(h, j), tile(h, j + 1), qg, pair_w)
        sq_ref[:, j * LANES:(j + 1) * LANES] = _rope(qa, c, s1, s2).astype(BF16)
        sq_ref[:, (j + 1) * LANES:(j + 2) * LANES] = _rope(qb, c, s1, s2).astype(BF16)
    a += SWA_Q
    h = seg(a, a + 2 * SWA_KV)
    skv_ref[:, 0:LANES] = _rope(_head_rms(tile(h, 0), kg_ref[...], head_w), c, s1, s2)
    skv_ref[:, LANES:2 * LANES] = tile(h, 1)


def _rope_tables(pos):
    half = SWA_HD // 2
    inv = ROPE_THETA ** (-jnp.arange(half, dtype=F32) / half)
    ang = pos.astype(F32)[:, None] * inv[None, :]
    cos, sin = jnp.cos(ang), jnp.sin(ang)
    z = jnp.zeros_like(sin)
    heads_per_tile = LANES // SWA_HD
    return (jnp.tile(cos, (1, 2 * heads_per_tile)),
            jnp.tile(jnp.concatenate([-sin, z], axis=1), (1, heads_per_tile)),
            jnp.tile(jnp.concatenate([z, sin], axis=1), (1, heads_per_tile)))


def _proj(x2d, pos_rows, tm, g1, w_in_bf, qg2, kg2):
    t_rows = x2d.shape[0]
    n_tiles = t_rows // tm
    n_pos_tiles = pos_rows.shape[0] // tm
    cos, s1, s2 = _rope_tables(pos_rows)
    row = lambda i: (i, 0)
    const = lambda i: (0, 0)
    tab = lambda i: (i % n_pos_tiles, 0)
    return pl.pallas_call(
        _proj_kernel,
        grid=(n_tiles,),
        in_specs=[pl.BlockSpec((tm, D_MODEL), row),
                  pl.BlockSpec((1, D_MODEL), const),
                  pl.BlockSpec((D_MODEL, IN_WIDTH), const),
                  pl.BlockSpec((1, LANES), const),
                  pl.BlockSpec((1, LANES), const),
                  pl.BlockSpec((tm, LANES), tab),
                  pl.BlockSpec((tm, LANES), tab),
                  pl.BlockSpec((tm, LANES), tab)],
        out_specs=[pl.BlockSpec((tm, 2 * RET_Q), row),
                   pl.BlockSpec((tm, RET_V), row),
                   pl.BlockSpec((tm, RET_V), row),
                   pl.BlockSpec((tm, SWA_Q), row),
                   pl.BlockSpec((tm, 2 * SWA_KV), row)],
        out_shape=[jax.ShapeDtypeStruct((t_rows, 2 * RET_Q), BF16),
                   jax.ShapeDtypeStruct((t_rows, RET_V), BF16),
                   jax.ShapeDtypeStruct((t_rows, RET_V), BF16),
                   jax.ShapeDtypeStruct((t_rows, SWA_Q), BF16),
                   jax.ShapeDtypeStruct((t_rows, 2 * SWA_KV), F32)],
        compiler_params=_params(1),
        name="proj",
    )(x2d, g1, w_in_bf, qg2, kg2, cos, s1, s2)


def _pair_update(k_bf, v0_bf, v1_bf, wt):
    kw = (k_bf.astype(F32) * wt).astype(BF16)
    dn = (((0,), (0,)), ((), ()))
    a0 = lax.dot_general(kw, v0_bf, dn, preferred_element_type=F32)
    a1 = lax.dot_general(kw, v1_bf, dn, preferred_element_type=F32)
    top = lax.broadcasted_iota(jnp.int32, a0.shape, 0) < RET_DK
    return jnp.where(top, a0, a1)


def _decay_rows(n, pair, rows_back_from):
    i = lax.broadcasted_iota(jnp.int32, (n, LANES), 0).astype(F32)
    lane = lax.broadcasted_iota(jnp.int32, (n, LANES), 1)
    lg = jnp.where(lane < RET_DK, _LOG_G[2 * pair], _LOG_G[2 * pair + 1])
    return jnp.exp((rows_back_from - i) * lg)


def _meta_state_kernel(rqk_ref, rv_ref, s_ref, *, n_rows):
    for p in range(RET_HEADS // 2):
        k = rqk_ref[:, RET_Q + p * LANES:RET_Q + (p + 1) * LANES]
        wt = _decay_rows(n_rows, p, float(N_META - 1))
        s_ref[p] = _pair_update(k, rv_ref[:, (2 * p) * LANES:(2 * p + 1) * LANES],
                                rv_ref[:, (2 * p + 1) * LANES:(2 * p + 2) * LANES], wt)


def _meta_state(m_rqk, m_rv):
    n_rows = m_rqk.shape[0]
    return pl.pallas_call(
        functools.partial(_meta_state_kernel, n_rows=n_rows),
        out_shape=jax.ShapeDtypeStruct((RET_HEADS // 2, 2 * RET_DK, RET_DV), F32),
        name="meta_state",
    )(m_rqk, m_rv)


def _dup_halves(a, lo_mask):
    sw = pltpu.roll(a, SWA_HD, 1)
    return jnp.where(lo_mask, a, sw), jnp.where(lo_mask, sw, a)


def _attn_kernel(rqk_ref, rv_ref, gate_ref, sq_ref, skv_ref, meta_ref, hist_ref, s0_ref, rng_ref, sink_ref,
                 omix_ref, sout_ref, kvout_ref,
                 s_scr, kd_scr, vd_scr, mk_scr, mv_scr, dec_scr, wt_scr, cs_scr, gam_scr,
                 *, tl, has_hist):
    b = pl.program_id(0)
    t = pl.program_id(1)
    nt = pl.num_programs(1)
    n_chunks = tl // CHUNK
    n_pairs = RET_HEADS // 2
    lo_tl = lax.broadcasted_iota(jnp.int32, (tl, LANES), 1) < SWA_HD
    lo_c = lax.broadcasted_iota(jnp.int32, (CHUNK, LANES), 1) < SWA_HD

    @pl.when((b == 0) & (t == 0))
    def _tables():
        i = lax.broadcasted_iota(jnp.int32, (tl, tl), 0)
        j = lax.broadcasted_iota(jnp.int32, (tl, tl), 1)
        diff = (i - j).astype(F32)
        row = lax.broadcasted_iota(jnp.int32, (tl, LANES), 0).astype(F32)
        for h in range(RET_HEADS):
            dec_scr[h] = jnp.where(diff >= 0.0, jnp.exp(jnp.maximum(diff, 0.0) * _LOG_G[h]), 0.0)
            cs_scr[h] = jnp.exp((row + 1.0) * _LOG_G[h])
        top = lax.broadcasted_iota(jnp.int32, (2 * RET_DK, RET_DV), 0) < RET_DK
        for p in range(n_pairs):
            wt_scr[p] = _decay_rows(tl, p, float(tl - 1))
            gam_scr[p] = jnp.where(top, jnp.exp(jnp.float32(tl * _LOG_G[2 * p])), jnp.exp(jnp.float32(tl * _LOG_G[2 * p + 1])))
        lo_m = lax.broadcasted_iota(jnp.int32, (N_META, LANES), 1) < SWA_HD
        mk0, mk1 = _dup_halves(meta_ref[:, 0:LANES], lo_m)
        mv0, mv1 = _dup_halves(meta_ref[:, LANES:2 * LANES], lo_m)
        mk_scr[...] = jnp.zeros(mk_scr.shape, BF16)
        mv_scr[...] = jnp.zeros(mv_scr.shape, BF16)
        mk_scr[0, 0:N_META] = mk0.astype(BF16)
        mk_scr[1, 0:N_META] = mk1.astype(BF16)
        mv_scr[0, 0:N_META] = mv0.astype(BF16)
        mv_scr[1, 0:N_META] = mv1.astype(BF16)

    @pl.when(t == 0)
    def _stream_start():
        s_scr[...] = s0_ref[0]
        if has_hist:
            lo_w = lax.broadcasted_iota(jnp.int32, (WINDOW, LANES), 1) < SWA_HD
            k0, k1 = _dup_halves(hist_ref[0, :, 0:LANES], lo_w)
            v0, v1 = _dup_halves(hist_ref[0, :, LANES:2 * LANES], lo_w)
            kd_scr[0, 0:WINDOW] = k0.astype(BF16)
            kd_scr[1, 0:WINDOW] = k1.astype(BF16)
            vd_scr[0, 0:WINDOW] = v0.astype(BF16)
            vd_scr[1, 0:WINDOW] = v1.astype(BF16)
        else:
            z = jnp.zeros((WINDOW, LANES), BF16)
            for kv in range(SWA_KV_HEADS):
                kd_scr[kv, 0:WINDOW] = z
                vd_scr[kv, 0:WINDOW] = z

    k0, k1 = _dup_halves(skv_ref[:, 0:LANES], lo_tl)
    v0, v1 = _dup_halves(skv_ref[:, LANES:2 * LANES], lo_tl)
    kd_scr[0, WINDOW:WINDOW + tl] = k0.astype(BF16)
    kd_scr[1, WINDOW:WINDOW + tl] = k1.astype(BF16)
    vd_scr[0, WINDOW:WINDOW + tl] = v0.astype(BF16)
    vd_scr[1, WINDOW:WINDOW + tl] = v1.astype(BF16)

    band = WINDOW + CHUNK
    n_keys = META_ROWS + band
    n_q = 4 * CHUNK
    scale2 = (SWA_HD ** -0.5) * LOG2E
    krow = lax.broadcasted_iota(jnp.int32, (n_keys, n_q), 0)
    zero_c = jnp.zeros((CHUNK, LANES), BF16)
    ones_v = jnp.ones((n_keys, LANES), BF16)
    for c in range(n_chunks):
        if has_hist:
            first_valid = META_ROWS
        else:
            first_valid = jnp.where(t == 0, max(META_ROWS + WINDOW - c * CHUNK, META_ROWS), META_ROWS)
        valid_t = (krow < N_META) | (krow >= first_valid)
        r0 = c * CHUNK
        for kv in range(SWA_KV_HEADS):
            keys = jnp.concatenate([mk_scr[kv], kd_scr[kv, r0:r0 + band]], axis=0)
            vals = jnp.concatenate([mv_scr[kv], vd_scr[kv, r0:r0 + band]], axis=0)
            qa = sq_ref[r0:r0 + CHUNK, (2 * kv) * LANES:(2 * kv + 1) * LANES]
            qb = sq_ref[r0:r0 + CHUNK, (2 * kv + 1) * LANES:(2 * kv + 2) * LANES]
            lhs = jnp.concatenate([jnp.where(lo_c, qa, zero_c), jnp.where(lo_c, zero_c, qa),
                                   jnp.where(lo_c, qb, zero_c), jnp.where(lo_c, zero_c, qb)], axis=0)
            s_t = lax.dot_general(keys, lhs, (((1,), (1,)), ((), ())), preferred_element_type=F32) * scale2
            s_t = jnp.where(valid_t, s_t, NEG_INF)
            s_t = jnp.where(krow == N_META, sink_ref[kv, 0:1, :] * LOG2E, s_t)
            e_t = jnp.exp2(s_t - jnp.max(s_t, axis=0, keepdims=True)).astype(BF16)
            ov = lax.dot_general(e_t, jnp.concatenate([vals, ones_v], axis=1), (((0,), (0,)), ((), ())),
                                 preferred_element_type=F32)
            o = ov[:, 0:LANES] * (1.0 / ov[:, LANES:2 * LANES])
            oa = jnp.where(lo_c, o[0:CHUNK], o[CHUNK:2 * CHUNK])
            ob = jnp.where(lo_c, o[2 * CHUNK:3 * CHUNK], o[3 * CHUNK:4 * CHUNK])
            base = RET_V + (2 * kv) * LANES
            omix_ref[r0:r0 + CHUNK, base:base + LANES] = oa.astype(BF16)
            omix_ref[r0:r0 + CHUNK, base + LANES:base + 2 * LANES] = ob.astype(BF16)

    zero_t = jnp.zeros((tl, LANES), BF16)
    rr = lax.broadcasted_iota(jnp.int32, (2 * LANES, 2 * LANES), 0) < LANES
    cc = lax.broadcasted_iota(jnp.int32, (2 * LANES, 2 * LANES), 1) < LANES
    ones_pair = jnp.where(rr == cc, 1.0, 0.0).astype(BF16)
    for p in range(n_pairs):
        q = rqk_ref[:, p * LANES:(p + 1) * LANES]
        k = rqk_ref[:, RET_Q + p * LANES:RET_Q + (p + 1) * LANES]
        lhs = jnp.concatenate([jnp.where(lo_tl, q, zero_t), jnp.where(lo_tl, zero_t, q)], axis=0)
        s = lax.dot_general(lhs, k, (((1,), (1,)), ((), ())), preferred_element_type=F32)
        cross = jnp.dot(lhs, s_scr[p].astype(BF16), preferred_element_type=F32)
        outs = []
        for i in range(2):
            h = 2 * p + i
            v = rv_ref[:, h * LANES:(h + 1) * LANES]
            a = (s[i * tl:(i + 1) * tl] * dec_scr[h]).astype(BF16)
            outs.append(jnp.dot(a, v, preferred_element_type=F32) + cross[i * tl:(i + 1) * tl] * cs_scr[h])
        ss = jnp.concatenate([(o * o).astype(BF16) for o in outs], axis=1)
        ms = jnp.dot(ss, ones_pair, preferred_element_type=F32) * (1.0 / RET_DV)
        for i, o in enumerate(outs):
            h = 2 * p + i
            r = o * lax.rsqrt(ms[:, i * LANES:(i + 1) * LANES] + EPS) * rng_ref[h:h + 1, :]
            omix_ref[:, h * LANES:(h + 1) * LANES] = (r * gate_ref[:, h * LANES:(h + 1) * LANES].astype(F32)).astype(BF16)
        u = _pair_update(k, rv_ref[:, (2 * p) * LANES:(2 * p + 1) * LANES],
                         rv_ref[:, (2 * p + 1) * LANES:(2 * p + 2) * LANES], wt_scr[p])
        s_scr[p] = gam_scr[p] * s_scr[p] + u

    if tl >= WINDOW:
        @pl.when(t + 1 < nt)
        def _carry_window():
            for kv in range(SWA_KV_HEADS):
                kd_scr[kv, 0:WINDOW] = kd_scr[kv, tl:tl + WINDOW]
                vd_scr[kv, 0:WINDOW] = vd_scr[kv, tl:tl + WINDOW]

    @pl.when(t + 1 == nt)
    def _stream_end():
        sout_ref[0] = s_scr[...]
        if tl >= WINDOW:
            kvout_ref[0] = skv_ref[tl - WINDOW:tl, :]
        else:
            kvout_ref[0, 0:WINDOW - tl] = hist_ref[0, tl:WINDOW, :]
            kvout_ref[0, WINDOW - tl:WINDOW] = skv_ref[...]


def _attention(rqk, rv, gate, sq, skv, meta_kv, hist_kv, s0, rng, sink_tab, *, n_streams, seq, tl, has_hist):
    nt = seq // tl
    assert tl % CHUNK == 0 and seq % tl == 0
    assert tl >= WINDOW or (nt == 1 and has_hist)
    n_pairs = RET_HEADS // 2
    s0_shared = s0.shape[0] == 1
    row = lambda b, t: (b * nt + t, 0)
    const2 = lambda b, t: (0, 0)
    const3 = lambda b, t: (0, 0, 0)
    per_b3 = lambda b, t: (b, 0, 0)
    s0_map = (lambda b, t: (0, 0, 0, 0)) if s0_shared else (lambda b, t: (b, 0, 0, 0))
    hist_map = per_b3 if has_hist else const3
    rows = n_streams * seq
    return pl.pallas_call(
        functools.partial(_attn_kernel, tl=tl, has_hist=has_hist),
        grid=(n_streams, nt),
        in_specs=[pl.BlockSpec((tl, 2 * RET_Q), row),
                  pl.BlockSpec((tl, RET_V), row),
                  pl.BlockSpec((tl, RET_V), row),
                  pl.BlockSpec((tl, SWA_Q), row),
                  pl.BlockSpec((tl, 2 * SWA_KV), row),
                  pl.BlockSpec((N_META, 2 * SWA_KV), const2),
                  pl.BlockSpec((1, WINDOW, 2 * SWA_KV), hist_map),
                  pl.BlockSpec((1, n_pairs, 2 * RET_DK, RET_DV), s0_map),
                  pl.BlockSpec((RET_HEADS, RET_DV), const2),
                  pl.BlockSpec((SWA_KV_HEADS, SUBLANES, 4 * CHUNK), const3)],
        out_specs=[pl.BlockSpec((tl, MIX_WIDTH), row),
                   pl.BlockSpec((1, n_pairs, 2 * RET_DK, RET_DV), lambda b, t: (b, 0, 0, 0)),
                   pl.BlockSpec((1, WINDOW, 2 * SWA_KV), per_b3)],
        out_shape=[jax.ShapeDtypeStruct((rows, MIX_WIDTH), BF16),
                   jax.ShapeDtypeStruct((n_streams, n_pairs, 2 * RET_DK, RET_DV), F32),
                   jax.ShapeDtypeStruct((n_streams, WINDOW, 2 * SWA_KV), F32)],
        scratch_shapes=[pltpu.VMEM((n_pairs, 2 * RET_DK, RET_DV), F32),
                        pltpu.VMEM((SWA_KV_HEADS, WINDOW + tl, LANES), BF16),
                        pltpu.VMEM((SWA_KV_HEADS, WINDOW + tl, LANES), BF16),
                        pltpu.VMEM((SWA_KV_HEADS, META_ROWS, LANES), BF16),
                        pltpu.VMEM((SWA_KV_HEADS, META_ROWS, LANES), BF16),
                        pltpu.VMEM((RET_HEADS, tl, tl), F32),
                        pltpu.VMEM((n_pairs, tl, LANES), F32),
                        pltpu.VMEM((RET_HEADS, tl, RET_DV), F32),
                        pltpu.VMEM((n_pairs, 2 * RET_DK, RET_DV), F32)],
        compiler_params=_params(2),
        name="attention",
    )(rqk, rv, gate, sq, skv, meta_kv, hist_kv, s0, rng, sink_tab)


def _post_kernel(omix_ref, x_ref, wout_ref, g2_ref, wrt_ref, base_ref,
                 xmid_ref, wcol_ref, rt_ref, cnt_ref, tri_scr, run_scr):
    i = pl.program_id(0)
    tm = x_ref.shape[0]

    @pl.when(i == 0)
    def _init():
        r = lax.broadcasted_iota(jnp.int32, (tm, tm), 0)
        c = lax.broadcasted_iota(jnp.int32, (tm, tm), 1)
        tri_scr[...] = jnp.where(r < c, 1.0, 0.0).astype(BF16)
        run_scr[...] = base_ref[...]

    xm = x_ref[...] + jnp.dot(omix_ref[...], wout_ref[...], preferred_element_type=F32)
    xmid_ref[...] = xm
    hn = _row_rms(xm) * g2_ref[...]
    lt = lax.dot_general(wrt_ref[...], hn.astype(BF16), (((1,), (1,)), ((), ())), preferred_element_type=F32)
    row8 = lax.broadcasted_iota(jnp.int32, (SUBLANES, tm), 0)
    big = jnp.int32(SUBLANES)
    gl = jnp.where(row8 < N_GROUPS, lt[0:SUBLANES], NEG_INF)
    gmax = jnp.max(gl, axis=0, keepdims=True)
    gsum = jnp.sum(jnp.exp(gl - gmax), axis=0, keepdims=True)
    g_sel = jnp.min(jnp.where(gl == gmax, row8, big), axis=0, keepdims=True)
    p_sel = 1.0 / gsum
    el = lt[ROUTER_EXPERT_ROW0:ROUTER_EXPERT_ROW0 + EXPERTS_PER_GROUP]
    for g in range(1, N_GROUPS):
        lo = ROUTER_EXPERT_ROW0 + g * EXPERTS_PER_GROUP
        el = jnp.where(g_sel == g, lt[lo:lo + EXPERTS_PER_GROUP], el)
    m1 = jnp.max(el, axis=0, keepdims=True)
    i1 = jnp.min(jnp.where(el == m1, row8, big), axis=0, keepdims=True)
    el2 = jnp.where(row8 == i1, NEG_INF, el)
    m2 = jnp.max(el2, axis=0, keepdims=True)
    i2 = jnp.min(jnp.where(el2 == m2, row8, big), axis=0, keepdims=True)
    e2 = jnp.exp(m2 - m1)
    inv = 1.0 / (1.0 + e2)
    w1 = p_sel * inv
    w2 = p_sel * (e2 * inv)
    eid1 = g_sel * EXPERTS_PER_GROUP + i1
    eid2 = g_sel * EXPERTS_PER_GROUP + i2

    rowe = lax.broadcasted_iota(jnp.int32, (N_EXPERTS, tm), 0)
    oh = jnp.where((rowe == eid1) | (rowe == eid2), 1.0, 0.0).astype(BF16)
    run = run_scr[...]
    pref = jnp.dot(oh, tri_scr[...], preferred_element_type=F32) + jnp.concatenate([run] * (tm // LANES), axis=1)
    r1 = jnp.sum(jnp.where(rowe == eid1, pref, 0.0), axis=0, keepdims=True)
    r2 = jnp.sum(jnp.where(rowe == eid2, pref, 0.0), axis=0, keepdims=True)
    run = run + jnp.dot(oh, jnp.ones((tm, LANES), BF16), preferred_element_type=F32)
    run_scr[...] = run
    cnt_ref[...] = run

    out = jnp.zeros((SUBLANES, tm), F32)
    for k, v in enumerate([eid1.astype(F32), eid2.astype(F32), w1, w2, r1, r2]):
        out = jnp.where(row8 == k, v, out)
    rt_ref[...] = out
    rowl = lax.broadcasted_iota(jnp.int32, (LANES, tm), 0)
    wcol_ref[...] = jnp.where(rowl == 0, w1, jnp.where(rowl == 1, w2, 0.0)).T


def _post(omix, x2d, w_out_bf, g2, w_router_t_bf, base_cnt, tm):
    t_rows = x2d.shape[0]
    assert tm % LANES == 0
    row = lambda i: (i, 0)
    const = lambda i: (0, 0)
    return pl.pallas_call(
        _post_kernel,
        grid=(t_rows // tm,),
        in_specs=[pl.BlockSpec((tm, MIX_WIDTH), row),
                  pl.BlockSpec((tm, D_MODEL), row),
                  pl.BlockSpec((MIX_WIDTH, D_MODEL), const),
                  pl.BlockSpec((1, D_MODEL), const),
                  pl.BlockSpec((ROUTER_ROWS, D_MODEL), const),
                  pl.BlockSpec((N_EXPERTS, LANES), const)],
        out_specs=[pl.BlockSpec((tm, D_MODEL), row),
                   pl.BlockSpec((tm, LANES), row),
                   pl.BlockSpec((SUBLANES, tm), lambda i: (0, i)),
                   pl.BlockSpec((N_EXPERTS, LANES), const)],
        out_shape=[jax.ShapeDtypeStruct((t_rows, D_MODEL), F32),
                   jax.ShapeDtypeStruct((t_rows, LANES), F32),
                   jax.ShapeDtypeStruct((SUBLANES, t_rows), F32),
                   jax.ShapeDtypeStruct((N_EXPERTS, LANES), F32)],
        scratch_shapes=[pltpu.VMEM((tm, tm), BF16), pltpu.VMEM((N_EXPERTS, LANES), F32)],
        compiler_params=_params(1),
        name="post",
    )(omix, x2d, w_out_bf, g2, w_router_t_bf, base_cnt)


def _step_major(pos, tm):
    return [pos[e].reshape(-1, 1, tm) for e in range(2)]


def _dispatch_kernel(pos0_ref, pos1_ref, *refs, tm, group_steps):
    pos_refs = (pos0_ref, pos1_ref)
    n_g = len(group_steps)
    xmid_refs = refs[:n_g]
    g2_ref, xs_ref, hbuf, sems = refs[n_g:]
    n_steps = sum(group_steps)
    i = pl.program_id(0)
    slot = i % 2

    first = 0
    for xmid_ref, steps in zip(xmid_refs, group_steps):
        @pl.when((i >= first) & (i < first + steps))
        def _normalise(xmid_ref=xmid_ref):
            hn = _row_rms(xmid_ref[...]) * g2_ref[...]
            hbuf[slot] = hn.reshape(tm // SUBLANES, SUBLANES, D_MODEL)
        first += steps

    def body(blk, carry):
        for k in range(SUBLANES):
            r = blk * SUBLANES + k
            for e in range(2):
                pltpu.make_async_copy(hbuf.at[slot, blk, pl.ds(k, 1)], xs_ref.at[pl.ds(pos_refs[e][0, 0, r], 1)],
                                      sems.at[slot]).start(priority=e)
        return carry

    lax.fori_loop(0, tm // SUBLANES, body, 0)

    def drain(which):
        pltpu.make_async_copy(xs_ref.at[pl.ds(0, 2 * tm)], xs_ref.at[pl.ds(0, 2 * tm)], sems.at[which]).wait()

    @pl.when(i > 0)
    def _previous():
        drain(1 - slot)

    @pl.when(i == n_steps - 1)
    def _last():
        drain(slot)


def _dispatch(pos_list, xmid_list, g2, tm):
    group_steps = tuple(x.shape[0] // tm for x in xmid_list)
    n_steps = sum(group_steps)
    assert tm % SUBLANES == 0
    pos3 = [jnp.concatenate(parts, axis=0) for parts in zip(*[_step_major(p, tm) for p in pos_list])]
    smem = pl.BlockSpec((1, 1, tm), lambda i: (i, 0, 0), memory_space=pltpu.SMEM)
    x_specs = []
    first = 0
    for steps in group_steps:
        x_specs.append(pl.BlockSpec((tm, D_MODEL),
                                    lambda i, first=first, steps=steps: (jnp.clip(i - first, 0, steps - 1), 0)))
        first += steps
    n_rows_out = 2 * sum(x.shape[0] for x in xmid_list)
    return pl.pallas_call(
        functools.partial(_dispatch_kernel, tm=tm, group_steps=group_steps),
        grid=(n_steps,),
        in_specs=[smem, smem] + x_specs + [pl.BlockSpec((1, D_MODEL), lambda i: (0, 0))],
        out_specs=pl.BlockSpec(memory_space=pl.ANY),
        out_shape=jax.ShapeDtypeStruct((n_rows_out, D_MODEL), F32),
        scratch_shapes=[pltpu.VMEM((2, tm // SUBLANES, SUBLANES, D_MODEL), F32), pltpu.SemaphoreType.DMA((2,))],
        compiler_params=_params(1),
        name="dispatch",
    )(*pos3, *xmid_list, g2)


def _expert_kernel(vt_ref, ve_ref, lo_ref, hi_ref, nv_ref, x_ref, wg_ref, wu_ref, wd_ref, y_ref, wgu_scr, wd_scr):
    v = pl.program_id(0)

    @pl.when(v < nv_ref[0])
    def _compute():
        @pl.when((v == 0) | (ve_ref[v] != ve_ref[jnp.maximum(v - 1, 0)]))
        def _new_expert():
            wgu_scr[:, 0:EXPERT_FF] = wg_ref[0].astype(BF16)
            wgu_scr[:, EXPERT_FF:2 * EXPERT_FF] = wu_ref[0].astype(BF16)
            wd_scr[...] = wd_ref[0].astype(BF16)

        lo, hi = lo_ref[v], hi_ref[v]

        for s0 in range(0, x_ref.shape[0], EXPERT_SUB):
            @pl.when((lo < s0 + EXPERT_SUB) & (hi > s0))
            def _row_block(s0=s0):
                x = x_ref[s0:s0 + EXPERT_SUB, :].astype(BF16)
                gu = jnp.dot(x, wgu_scr[...], preferred_element_type=F32)
                g = gu[:, 0:EXPERT_FF]
                a = (g * jax.nn.sigmoid(g) * gu[:, EXPERT_FF:2 * EXPERT_FF]).astype(BF16)
                y = jnp.dot(a, wd_scr[...], preferred_element_type=F32)

                @pl.when(lo <= s0)
                def _first_visit():
                    y_ref[s0:s0 + EXPERT_SUB, :] = y

                @pl.when(lo > s0)
                def _later_visit():
                    row = lax.broadcasted_iota(jnp.int32, y.shape, 0) + s0
                    y_ref[s0:s0 + EXPERT_SUB, :] = jnp.where((row >= lo) & (row < hi), y, y_ref[s0:s0 + EXPERT_SUB, :])


def _experts(vis_tile, vis_expert, vis_lo, vis_hi, n_vis, xs, w_gate, w_up, w_down, tm):
    n_steps = vis_tile.shape[0]
    last = lambda v, nv: jnp.minimum(v, nv[0] - 1)
    of_expert = lambda v, vt, ve, lo, hi, nv: (ve[last(v, nv)], 0, 0)
    return pl.pallas_call(
        _expert_kernel,
        grid_spec=pltpu.PrefetchScalarGridSpec(
            num_scalar_prefetch=5,
            grid=(n_steps,),
            in_specs=[pl.BlockSpec((tm, D_MODEL), lambda v, vt, ve, lo, hi, nv: (vt[last(v, nv)], 0)),
                      pl.BlockSpec((1, D_MODEL, EXPERT_FF), of_expert),
                      pl.BlockSpec((1, D_MODEL, EXPERT_FF), of_expert),
                      pl.BlockSpec((1, EXPERT_FF, D_MODEL), of_expert)],
            out_specs=pl.BlockSpec((tm, D_MODEL), lambda v, vt, ve, lo, hi, nv: (vt[last(v, nv)], 0)),
            scratch_shapes=[pltpu.VMEM((D_MODEL, 2 * EXPERT_FF), BF16), pltpu.VMEM((EXPERT_FF, D_MODEL), BF16)]),
        out_shape=jax.ShapeDtypeStruct(xs.shape, F32),
        compiler_params=_params(1),
        name="experts",
    )(vis_tile, vis_expert, vis_lo, vis_hi, n_vis, xs, w_gate, w_up, w_down)


def _combine_kernel(pos0_ref, pos1_ref, nxt0_ref, nxt1_ref, ys_ref, xmid_ref, wcol_ref, out_ref, ybuf, sems,
                    *, tm, n_steps):
    i = pl.program_id(0)
    slot = i % 2

    def issue(p_refs, to_slot):
        def body(blk, carry):
            for k in range(SUBLANES):
                r = blk * SUBLANES + k
                for e in range(2):
                    pltpu.make_async_copy(ys_ref.at[pl.ds(p_refs[e][0, 0, r], 1)],
                                          ybuf.at[to_slot, e, blk, pl.ds(k, 1)], sems.at[to_slot]).start(priority=e)
            return carry

        lax.fori_loop(0, tm // SUBLANES, body, 0)

    @pl.when(i == 0)
    def _first():
        issue((pos0_ref, pos1_ref), 0)

    @pl.when(i + 1 < n_steps)
    def _ahead():
        issue((nxt0_ref, nxt1_ref), 1 - slot)

    for e in range(2):
        pltpu.make_async_copy(ybuf.at[slot, e], ybuf.at[slot, e], sems.at[slot]).wait()
    w = wcol_ref[...]
    y0 = ybuf[slot, 0].reshape(tm, D_MODEL)
    y1 = ybuf[slot, 1].reshape(tm, D_MODEL)
    out_ref[...] = xmid_ref[...] + w[:, 0:1] * y0 + w[:, 1:2] * y1


def _combine(pos, ys, xmid, wcol, tm):
    t_rows = xmid.shape[0]
    n_steps = t_rows // tm
    assert tm % SUBLANES == 0
    pos3 = _step_major(pos, tm)
    row = lambda i: (i, 0)
    cur = pl.BlockSpec((1, 1, tm), lambda i: (i, 0, 0), memory_space=pltpu.SMEM)
    nxt = pl.BlockSpec((1, 1, tm), lambda i: (jnp.minimum(i + 1, n_steps - 1), 0, 0), memory_space=pltpu.SMEM)
    return pl.pallas_call(
        functools.partial(_combine_kernel, tm=tm, n_steps=n_steps),
        grid=(n_steps,),
        in_specs=[cur, cur, nxt, nxt,
                  pl.BlockSpec(memory_space=pl.ANY),
                  pl.BlockSpec((tm, D_MODEL), row),
                  pl.BlockSpec((tm, LANES), row)],
        out_specs=pl.BlockSpec((tm, D_MODEL), row),
        out_shape=jax.ShapeDtypeStruct((t_rows, D_MODEL), F32),
        scratch_shapes=[pltpu.VMEM((2, 2, tm // SUBLANES, SUBLANES, D_MODEL), F32), pltpu.SemaphoreType.DMA((2,))],
        compiler_params=_params(1),
        name="combine",
    )(*pos3, *pos3, ys, xmid, wcol)


def _bucket(ends, idx):
    n = jnp.sum((ends[None, :] <= idx[:, None]).astype(jnp.int32), axis=1)
    return jnp.minimum(n, ends.shape[0] - 1)


def _tile_for(rows, pref):
    tm = min(pref, rows)
    assert rows % tm == 0
    return tm


def kernel(x_prompt, x_sample, cache_ret_state, cache_swa_k, cache_swa_v, meta_tokens, norm1_g, w_in, q_norm_g,
           k_norm_g, ret_norm_g, attn_sinks, w_out, norm2_g, w_group, w_expert, w_gate, w_up, w_down):
    assert norm1_g.shape[0] == 1, "single-layer trunk"
    bp, lp, _ = x_prompt.shape
    bs, ls, _ = x_sample.shape
    n_pairs = RET_HEADS // 2

    g1 = norm1_g[0][None, :]
    g2 = norm2_g[0][None, :]
    w_in_bf = w_in[0].astype(BF16)
    w_out_bf = w_out[0].astype(BF16)
    qg2 = jnp.tile(q_norm_g[0], 2)[None, :]
    kg2 = jnp.tile(k_norm_g[0], 2)[None, :]
    rng = ret_norm_g[0].reshape(RET_HEADS, RET_DV)
    sink_tab = jnp.broadcast_to(jnp.repeat(attn_sinks[0], CHUNK).reshape(SWA_KV_HEADS, 1, 4 * CHUNK),
                                (SWA_KV_HEADS, SUBLANES, 4 * CHUNK))
    w_router_t = jnp.zeros((ROUTER_ROWS, D_MODEL), F32)
    w_router_t = w_router_t.at[0:N_GROUPS].set(w_group[0].T)
    w_router_t = w_router_t.at[ROUTER_EXPERT_ROW0:ROUTER_EXPERT_ROW0 + N_EXPERTS].set(w_expert[0].T)
    w_router_t_bf = w_router_t.astype(BF16)

    meta_rows = 2 * CHUNK
    m_pad = jnp.zeros((meta_rows, D_MODEL), F32).at[0:N_META].set(meta_tokens)
    m_rqk, m_rv, _, _, m_skv = _proj(m_pad, jnp.arange(meta_rows, dtype=jnp.int32), meta_rows, g1, w_in_bf, qg2, kg2)
    s_meta = _meta_state(m_rqk, m_rv)[None]
    meta_kv = m_skv[0:N_META]

    groups = [
        dict(x=x_prompt.reshape(bp * lp, D_MODEL), n=bp, seq=lp, pos0=N_META, has_hist=False, s0=s_meta,
             hist=jnp.zeros((1, WINDOW, 2 * SWA_KV), F32)),
        dict(x=x_sample.reshape(bs * ls, D_MODEL), n=bs, seq=ls, pos0=N_META + PAST_LEN, has_hist=True,
             s0=cache_ret_state[0].reshape(bs, n_pairs, 2 * RET_DK, RET_DV),
             hist=jnp.concatenate([cache_swa_k[0].reshape(bs, WINDOW, SWA_KV),
                                   cache_swa_v[0].reshape(bs, WINDOW, SWA_KV)], axis=-1)),
    ]

    base_cnt = jnp.zeros((N_EXPERTS, LANES), F32)
    for g in groups:
        rows = g["n"] * g["seq"]
        tm = _tile_for(rows, PROJ_TILE)
        pos = g["pos0"] + jnp.arange(g["seq"], dtype=jnp.int32)
        if g["seq"] < tm:
            assert tm % g["seq"] == 0
            pos = jnp.tile(pos, tm // g["seq"])
        else:
            assert g["seq"] % tm == 0
        rqk, rv, gate, sq, skv = _proj(g["x"], pos, tm, g1, w_in_bf, qg2, kg2)
        tl = min(ATTN_TILE, g["seq"])
        omix, s_out, kv_out = _attention(rqk, rv, gate, sq, skv, meta_kv, g["hist"], g["s0"], rng, sink_tab,
                                         n_streams=g["n"], seq=g["seq"], tl=tl, has_hist=g["has_hist"])
        xmid, wcol, route_t, base_cnt = _post(omix, g["x"], w_out_bf, g2, w_router_t_bf, base_cnt,
                                              _tile_for(rows, POST_TILE))
        g.update(xmid=xmid, wcol=wcol, route_t=route_t, s_out=s_out, kv_out=kv_out)

    te = EXPERT_TILE
    total_rows = sum(g["n"] * g["seq"] for g in groups)
    assert (2 * total_rows) % te == 0
    n_row_tiles = (2 * total_rows) // te
    counts = base_cnt[:, 0].astype(jnp.int32)
    off = jnp.cumsum(counts) - counts
    first_tile = off // te
    n_vis_e = jnp.where(counts > 0, (off + counts - 1) // te - first_tile + 1, 0)
    vis_end = jnp.cumsum(n_vis_e)
    n_vis = vis_end[-1:].astype(jnp.int32)
    v = jnp.arange(n_row_tiles + N_EXPERTS, dtype=jnp.int32)
    vis_expert = _bucket(vis_end, v)
    pick = lambda table: jnp.sum(jnp.where(vis_expert[:, None] == jnp.arange(N_EXPERTS, dtype=jnp.int32)[None, :],
                                           table[None, :], 0), axis=1)
    vis_tile = jnp.clip(pick(first_tile) + v - pick(vis_end - n_vis_e), 0, n_row_tiles - 1).astype(jnp.int32)
    vis_lo = jnp.clip(pick(off) - vis_tile * te, 0, te).astype(jnp.int32)
    vis_hi = jnp.clip(pick(off + counts) - vis_tile * te, 0, te).astype(jnp.int32)

    for g in groups:
        eid = g["route_t"][0:2].astype(jnp.int32)
        off_sel = jnp.sum(jnp.where(eid[None] == jnp.arange(N_EXPERTS, dtype=jnp.int32)[:, None, None],
                                    off[:, None, None], 0), axis=0)
        g["pos"] = (off_sel + g["route_t"][4:6].astype(jnp.int32)).astype(jnp.int32)
    xs = _dispatch([g["pos"] for g in groups], [g["xmid"] for g in groups], g2,
                   _tile_for(min(g["n"] * g["seq"] for g in groups), MOVE_TILE))

    ys = _experts(vis_tile, vis_expert.astype(jnp.int32), vis_lo, vis_hi, n_vis, xs, w_gate[0], w_up[0], w_down[0], te)

    outs = []
    for g in groups:
        rows = g["n"] * g["seq"]
        y = _combine(g["pos"], ys, g["xmid"], g["wcol"], _tile_for(rows, MOVE_TILE))
        outs.append(y.reshape(g["n"], g["seq"], D_MODEL))

    def caches(g):
        kv = g["kv_out"]
        k = kv[:, :, 0:SWA_KV].reshape(g["n"], WINDOW, SWA_KV_HEADS, SWA_HD)[None]
        v = kv[:, :, SWA_KV:2 * SWA_KV].reshape(g["n"], WINDOW, SWA_KV_HEADS, SWA_HD)[None]
        s = g["s_out"].reshape(g["n"], RET_HEADS, RET_DK, RET_DV)[None]
        return s, k, v

    sp, kp, vp = caches(groups[0])
    ss, ks, vs = caches(groups[1])
    return (outs[0], outs[1], sp, kp, vp, ss, ks, vs)
```

```python
import functools

import numpy as np
import jax
import jax.numpy as jnp
from jax import lax
from jax.experimental import pallas as pl
from jax.experimental.pallas import tpu as pltpu

F32 = jnp.float32
BF16 = jnp.bfloat16

D_MODEL = 1024
PAST_LEN = 4096
CHUNK = 64
N_META = 16
RET_HEADS = 4
RET_DK = 64
RET_DV = 128
SWA_HEADS = 8
SWA_KV_HEADS = 2
SWA_HD = 64
WINDOW = 128
ROPE_THETA = 10000.0
N_GROUPS = 4
EXPERTS_PER_GROUP = 8
N_EXPERTS = N_GROUPS * EXPERTS_PER_GROUP
EXPERT_FF = 256
EPS = 1e-6
NEG_INF = -1e30
LOG2E = float(np.log2(np.e))
RET_Q = RET_HEADS * RET_DK
RET_V = RET_HEADS * RET_DV
SWA_Q = SWA_HEADS * SWA_HD
SWA_KV = SWA_KV_HEADS * SWA_HD
MIX_WIDTH = RET_V + SWA_Q
IN_WIDTH = 2 * RET_Q + 2 * RET_V + SWA_Q + 2 * SWA_KV

LANES = 128
PROJ_TILE = 1024
POST_TILE = 1024
ATTN_TILE = 256
EXPERT_TILE = 1024
EXPERT_RING = 3
MOVE_TILE = 512
SUBLANES = 8
ROUTER_EXPERT_ROW0 = 8
ROUTER_ROWS = 64
META_ROWS = 32
VMEM_LIMIT = 56 * 1024 * 1024

_LOG_G = [float(np.log1p(-np.exp2(-5.0 - h))) for h in range(RET_HEADS)]


def _params(n_axes):
    return pltpu.CompilerParams(dimension_semantics=("arbitrary",) * n_axes, vmem_limit_bytes=VMEM_LIMIT)


def _split_bf16(a):
    hi = a.astype(BF16)
    return hi, (a - hi.astype(F32)).astype(BF16)


def _split_dot(a, w2):
    hi, lo = _split_bf16(a)
    return jnp.dot(jnp.concatenate([hi, lo], axis=1), w2, preferred_element_type=F32)


def _lane_sum(a):
    return _split_dot(a, jnp.ones((2 * LANES, LANES), BF16))


def _head_sum_matrix():
    i = lax.broadcasted_iota(jnp.int32, (2 * LANES, LANES), 0) % LANES
    j = lax.broadcasted_iota(jnp.int32, (2 * LANES, LANES), 1)
    return jnp.where((i < SWA_HD) == (j < SWA_HD), 1.0, 0.0).astype(BF16)


def _rope(t, c, s1, s2):
    half = SWA_HD // 2
    return t * c + pltpu.roll(t, LANES - half, 1) * s1 + pltpu.roll(t, half, 1) * s2


def _head_rms(t, g, head_w):
    ms = _split_dot(t * t, head_w) * (1.0 / SWA_HD)
    return t * lax.rsqrt(ms + EPS) * g


def _head_rms_pair(ta, tb, g, pair_w):
    ss = jnp.concatenate([(ta * ta).astype(BF16), (tb * tb).astype(BF16)], axis=1)
    ms = jnp.dot(ss, pair_w, preferred_element_type=F32) * (1.0 / SWA_HD)
    return ta * lax.rsqrt(ms[:, 0:LANES] + EPS) * g, tb * lax.rsqrt(ms[:, LANES:2 * LANES] + EPS) * g


def _row_rms(x):
    n_tiles = x.shape[1] // LANES
    ss = x[:, 0:LANES] * x[:, 0:LANES]
    for j in range(1, n_tiles):
        ss = ss + x[:, j * LANES:(j + 1) * LANES] * x[:, j * LANES:(j + 1) * LANES]
    r = lax.rsqrt(_lane_sum(ss) * (1.0 / x.shape[1]) + EPS)
    return x * jnp.concatenate([r] * n_tiles, axis=1)


def _proj_kernel(x_ref, g1_ref, w_ref, qg_ref, kg_ref, cos_ref, s1_ref, s2_ref,
                 rqk_ref, rv_ref, gate_ref, sq_ref, skv_ref):
    xn = (_row_rms(x_ref[...]) * g1_ref[...]).astype(BF16)
    c, s1, s2 = cos_ref[...], s1_ref[...], s2_ref[...]
    head_w = _head_sum_matrix()

    def seg(a, b):
        return jnp.dot(xn, w_ref[:, a:b], preferred_element_type=F32)

    def tile(h, j):
        return h[:, j * LANES:(j + 1) * LANES]

    h = seg(0, 2 * RET_Q)
    for j in range(2):
        rqk_ref[:, j * LANES:(j + 1) * LANES] = _rope(tile(h, j), c, s1, s2).astype(BF16)
    for j in range(2, 4):
        rqk_ref[:, j * LANES:(j + 1) * LANES] = (_rope(tile(h, j), c, s1, s2) * (RET_DK ** -0.5)).astype(BF16)
    a = 2 * RET_Q
    rv_ref[...] = seg(a, a + RET_V).astype(BF16)
    a += RET_V
    g = seg(a, a + RET_V)
    gate_ref[...] = (g * jax.nn.sigmoid(g)).astype(BF16)
    a += RET_V
    h = seg(a, a + SWA_Q)
    qg = qg_ref[...]
    pair_w = jnp.concatenate([jnp.concatenate([head_w[0:LANES], jnp.zeros((LANES, LANES), BF16)], axis=1),
                              jnp.concatenate([jnp.zeros((LANES, LANES), BF16), head_w[0:LANES]], axis=1)], axis=0)
    for j in range(0, SWA_Q // LANES, 2):
        qa, qb = _head_rms_pair(tile(h, j), tile(h, j + 1), qg, pair_w)
        sq_ref[:, j * LANES:(j + 1) * LANES] = _rope(qa, c, s1, s2).astype(BF16)
        sq_ref[:, (j + 1) * LANES:(j + 2) * LANES] = _rope(qb, c, s1, s2).astype(BF16)
    a += SWA_Q
    h = seg(a, a + 2 * SWA_KV)
    skv_ref[:, 0:LANES] = _rope(_head_rms(tile(h, 0), kg_ref[...], head_w), c, s1, s2)
    skv_ref[:, LANES:2 * LANES] = tile(h, 1)


def _rope_tables(pos):
    half = SWA_HD // 2
    inv = ROPE_THETA ** (-jnp.arange(half, dtype=F32) / half)
    ang = pos.astype(F32)[:, None] * inv[None, :]
    cos, sin = jnp.cos(ang), jnp.sin(ang)
    z = jnp.zeros_like(sin)
    heads_per_tile = LANES // SWA_HD
    return (jnp.tile(cos, (1, 2 * heads_per_tile)),
            jnp.tile(jnp.concatenate([-sin, z], axis=1), (1, heads_per_tile)),
            jnp.tile(jnp.concatenate([z, sin], axis=1), (1, heads_per_tile)))


def _proj(x2d, pos_rows, tm, g1, w_in_bf, qg2, kg2):
    t_rows = x2d.shape[0]
    n_tiles = t_rows // tm
    n_pos_tiles = pos_rows.shape[0] // tm
    cos, s1, s2 = _rope_tables(pos_rows)
    row = lambda i: (i, 0)
    const = lambda i: (0, 0)
    tab = lambda i: (i % n_pos_tiles, 0)
    return pl.pallas_call(
        _proj_kernel,
        grid=(n_tiles,),
        in_specs=[pl.BlockSpec((tm, D_MODEL), row),
                  pl.BlockSpec((1, D_MODEL), const),
                  pl.BlockSpec((D_MODEL, IN_WIDTH), const),
                  pl.BlockSpec((1, LANES), const),
                  pl.BlockSpec((1, LANES), const),
                  pl.BlockSpec((tm, LANES), tab),
                  pl.BlockSpec((tm, LANES), tab),
                  pl.BlockSpec((tm, LANES), tab)],
        out_specs=[pl.BlockSpec((tm, 2 * RET_Q), row),
                   pl.BlockSpec((tm, RET_V), row),
                   pl.BlockSpec((tm, RET_V), row),
                   pl.BlockSpec((tm, SWA_Q), row),
                   pl.BlockSpec((tm, 2 * SWA_KV), row)],
        out_shape=[jax.ShapeDtypeStruct((t_rows, 2 * RET_Q), BF16),
                   jax.ShapeDtypeStruct((t_rows, RET_V), BF16),
                   jax.ShapeDtypeStruct((t_rows, RET_V), BF16),
                   jax.ShapeDtypeStruct((t_rows, SWA_Q), BF16),
                   jax.ShapeDtypeStruct((t_rows, 2 * SWA_KV), F32)],
        compiler_params=_params(1),
        name="proj",
    )(x2d, g1, w_in_bf, qg2, kg2, cos, s1, s2)


def _pair_update(k_bf, v0_bf, v1_bf, wt):
    kw = (k_bf.astype(F32) * wt).astype(BF16)
    dn = (((0,), (0,)), ((), ()))
    a0 = lax.dot_general(kw, v0_bf, dn, preferred_element_type=F32)
    a1 = lax.dot_general(kw, v1_bf, dn, preferred_element_type=F32)
    top = lax.broadcasted_iota(jnp.int32, a0.shape, 0) < RET_DK
    return jnp.where(top, a0, a1)


def _decay_rows(n, pair, rows_back_from):
    i = lax.broadcasted_iota(jnp.int32, (n, LANES), 0).astype(F32)
    lane = lax.broadcasted_iota(jnp.int32, (n, LANES), 1)
    lg = jnp.where(lane < RET_DK, _LOG_G[2 * pair], _LOG_G[2 * pair + 1])
    return jnp.exp((rows_back_from - i) * lg)


def _meta_state_kernel(rqk_ref, rv_ref, s_ref, *, n_rows):
    for p in range(RET_HEADS // 2):
        k = rqk_ref[:, RET_Q + p * LANES:RET_Q + (p + 1) * LANES]
        wt = _decay_rows(n_rows, p, float(N_META - 1))
        s_ref[p] = _pair_update(k, rv_ref[:, (2 * p) * LANES:(2 * p + 1) * LANES],
                                rv_ref[:, (2 * p + 1) * LANES:(2 * p + 2) * LANES], wt)


def _meta_state(m_rqk, m_rv):
    n_rows = m_rqk.shape[0]
    return pl.pallas_call(
        functools.partial(_meta_state_kernel, n_rows=n_rows),
        out_shape=jax.ShapeDtypeStruct((RET_HEADS // 2, 2 * RET_DK, RET_DV), F32),
        name="meta_state",
    )(m_rqk, m_rv)


def _dup_halves(a, lo_mask):
    sw = pltpu.roll(a, SWA_HD, 1)
    return jnp.where(lo_mask, a, sw), jnp.where(lo_mask, sw, a)


def _attn_kernel(rqk_ref, rv_ref, gate_ref, sq_ref, skv_ref, meta_ref, hist_ref, s0_ref, rng_ref, sink_ref,
                 omix_ref, sout_ref, kvout_ref,
                 s_scr, kd_scr, vd_scr, mk_scr, mv_scr, dec_scr, wt_scr, cs_scr, gam_scr,
                 *, tl, has_hist):
    b = pl.program_id(0)
    t = pl.program_id(1)
    nt = pl.num_programs(1)
    n_chunks = tl // CHUNK
    n_pairs = RET_HEADS // 2
    lo_tl = lax.broadcasted_iota(jnp.int32, (tl, LANES), 1) < SWA_HD
    lo_c = lax.broadcasted_iota(jnp.int32, (CHUNK, LANES), 1) < SWA_HD

    @pl.when((b == 0) & (t == 0))
    def _tables():
        i = lax.broadcasted_iota(jnp.int32, (tl, tl), 0)
        j = lax.broadcasted_iota(jnp.int32, (tl, tl), 1)
        diff = (i - j).astype(F32)
        row = lax.broadcasted_iota(jnp.int32, (tl, LANES), 0).astype(F32)
        for h in range(RET_HEADS):
            dec_scr[h] = jnp.where(diff >= 0.0, jnp.exp(jnp.maximum(diff, 0.0) * _LOG_G[h]), 0.0)
            cs_scr[h] = jnp.exp((row + 1.0) * _LOG_G[h])
        top = lax.broadcasted_iota(jnp.int32, (2 * RET_DK, RET_DV), 0) < RET_DK
        for p in range(n_pairs):
            wt_scr[p] = _decay_rows(tl, p, float(tl - 1))
            gam_scr[p] = jnp.where(top, jnp.exp(jnp.float32(tl * _LOG_G[2 * p])), jnp.exp(jnp.float32(tl * _LOG_G[2 * p + 1])))
        lo_m = lax.broadcasted_iota(jnp.int32, (N_META, LANES), 1) < SWA_HD
        mk0, mk1 = _dup_halves(meta_ref[:, 0:LANES], lo_m)
        mv0, mv1 = _dup_halves(meta_ref[:, LANES:2 * LANES], lo_m)
        mk_scr[...] = jnp.zeros(mk_scr.shape, BF16)
        mv_scr[...] = jnp.zeros(mv_scr.shape, BF16)
        mk_scr[0, 0:N_META] = mk0.astype(BF16)
        mk_scr[1, 0:N_META] = mk1.astype(BF16)
        mv_scr[0, 0:N_META] = mv0.astype(BF16)
        mv_scr[1, 0:N_META] = mv1.astype(BF16)

    @pl.when(t == 0)
    def _stream_start():
        s_scr[...] = s0_ref[0]
        if has_hist:
            lo_w = lax.broadcasted_iota(jnp.int32, (WINDOW, LANES), 1) < SWA_HD
            k0, k1 = _dup_halves(hist_ref[0, :, 0:LANES], lo_w)
            v0, v1 = _dup_halves(hist_ref[0, :, LANES:2 * LANES], lo_w)
            kd_scr[0, 0:WINDOW] = k0.astype(BF16)
            kd_scr[1, 0:WINDOW] = k1.astype(BF16)
            vd_scr[0, 0:WINDOW] = v0.astype(BF16)
            vd_scr[1, 0:WINDOW] = v1.astype(BF16)
        else:
            z = jnp.zeros((WINDOW, LANES), BF16)
            for kv in range(SWA_KV_HEADS):
                kd_scr[kv, 0:WINDOW] = z
                vd_scr[kv, 0:WINDOW] = z

    k0, k1 = _dup_halves(skv_ref[:, 0:LANES], lo_tl)
    v0, v1 = _dup_halves(skv_ref[:, LANES:2 * LANES], lo_tl)
    kd_scr[0, WINDOW:WINDOW + tl] = k0.astype(BF16)
    kd_scr[1, WINDOW:WINDOW + tl] = k1.astype(BF16)
    vd_scr[0, WINDOW:WINDOW + tl] = v0.astype(BF16)
    vd_scr[1, WINDOW:WINDOW + tl] = v1.astype(BF16)

    band = WINDOW + CHUNK
    n_keys = META_ROWS + band
    n_q = 4 * CHUNK
    scale2 = (SWA_HD ** -0.5) * LOG2E
    krow = lax.broadcasted_iota(jnp.int32, (n_keys, n_q), 0)
    zero_c = jnp.zeros((CHUNK, LANES), BF16)
    ones_v = jnp.ones((n_keys, LANES), BF16)
    for c in range(n_chunks):
        if has_hist:
            first_valid = META_ROWS
        else:
            first_valid = jnp.where(t == 0, max(META_ROWS + WINDOW - c * CHUNK, META_ROWS), META_ROWS)
        valid_t = (krow < N_META) | (krow >= first_valid)
        r0 = c * CHUNK
        for kv in range(SWA_KV_HEADS):
            keys = jnp.concatenate([mk_scr[kv], kd_scr[kv, r0:r0 + band]], axis=0)
            vals = jnp.concatenate([mv_scr[kv], vd_scr[kv, r0:r0 + band]], axis=0)
            qa = sq_ref[r0:r0 + CHUNK, (2 * kv) * LANES:(2 * kv + 1) * LANES]
            qb = sq_ref[r0:r0 + CHUNK, (2 * kv + 1) * LANES:(2 * kv + 2) * LANES]
            lhs = jnp.concatenate([jnp.where(lo_c, qa, zero_c), jnp.where(lo_c, zero_c, qa),
                                   jnp.where(lo_c, qb, zero_c), jnp.where(lo_c, zero_c, qb)], axis=0)
            s_t = lax.dot_general(keys, lhs, (((1,), (1,)), ((), ())), preferred_element_type=F32) * scale2
            s_t = jnp.where(valid_t, s_t, NEG_INF)
            s_t = jnp.where(krow == N_META, sink_ref[kv, 0:1, :] * LOG2E, s_t)
            e_t = jnp.exp2(s_t - jnp.max(s_t, axis=0, keepdims=True)).astype(BF16)
            ov = lax.dot_general(e_t, jnp.concatenate([vals, ones_v], axis=1), (((0,), (0,)), ((), ())),
                                 preferred_element_type=F32)
            o = ov[:, 0:LANES] * (1.0 / ov[:, LANES:2 * LANES])
            oa = jnp.where(lo_c, o[0:CHUNK], o[CHUNK:2 * CHUNK])
            ob = jnp.where(lo_c, o[2 * CHUNK:3 * CHUNK], o[3 * CHUNK:4 * CHUNK])
            base = RET_V + (2 * kv) * LANES
            omix_ref[r0:r0 + CHUNK, base:base + LANES] = oa.astype(BF16)
            omix_ref[r0:r0 + CHUNK, base + LANES:base + 2 * LANES] = ob.astype(BF16)

    zero_t = jnp.zeros((tl, LANES), BF16)
    rr = lax.broadcasted_iota(jnp.int32, (2 * LANES, 2 * LANES), 0) < LANES
    cc = lax.broadcasted_iota(jnp.int32, (2 * LANES, 2 * LANES), 1) < LANES
    ones_pair = jnp.where(rr == cc, 1.0, 0.0).astype(BF16)
    for p in range(n_pairs):
        q = rqk_ref[:, p * LANES:(p + 1) * LANES]
        k = rqk_ref[:, RET_Q + p * LANES:RET_Q + (p + 1) * LANES]
        lhs = jnp.concatenate([jnp.where(lo_tl, q, zero_t), jnp.where(lo_tl, zero_t, q)], axis=0)
        s = lax.dot_general(lhs, k, (((1,), (1,)), ((), ())), preferred_element_type=F32)
        cross = jnp.dot(lhs, s_scr[p].astype(BF16), preferred_element_type=F32)
        outs = []
        for i in range(2):
            h = 2 * p + i
            v = rv_ref[:, h * LANES:(h + 1) * LANES]
            a = (s[i * tl:(i + 1) * tl] * dec_scr[h]).astype(BF16)
            outs.append(jnp.dot(a, v, preferred_element_type=F32) + cross[i * tl:(i + 1) * tl] * cs_scr[h])
        ss = jnp.concatenate([(o * o).astype(BF16) for o in outs], axis=1)
        ms = jnp.dot(ss, ones_pair, preferred_element_type=F32) * (1.0 / RET_DV)
        for i, o in enumerate(outs):
            h = 2 * p + i
            r = o * lax.rsqrt(ms[:, i * LANES:(i + 1) * LANES] + EPS) * rng_ref[h:h + 1, :]
            omix_ref[:, h * LANES:(h + 1) * LANES] = (r * gate_ref[:, h * LANES:(h + 1) * LANES].astype(F32)).astype(BF16)
        u = _pair_update(k, rv_ref[:, (2 * p) * LANES:(2 * p + 1) * LANES],
                         rv_ref[:, (2 * p + 1) * LANES:(2 * p + 2) * LANES], wt_scr[p])
        s_scr[p] = gam_scr[p] * s_scr[p] + u

    if tl >= WINDOW:
        @pl.when(t + 1 < nt)
        def _carry_window():
            for kv in range(SWA_KV_HEADS):
                kd_scr[kv, 0:WINDOW] = kd_scr[kv, tl:tl + WINDOW]
                vd_scr[kv, 0:WINDOW] = vd_scr[kv, tl:tl + WINDOW]

    @pl.when(t + 1 == nt)
    def _stream_end():
        sout_ref[0] = s_scr[...]
        if tl >= WINDOW:
            kvout_ref[0] = skv_ref[tl - WINDOW:tl, :]
        else:
            kvout_ref[0, 0:WINDOW - tl] = hist_ref[0, tl:WINDOW, :]
            kvout_ref[0, WINDOW - tl:WINDOW] = skv_ref[...]


def _attention(rqk, rv, gate, sq, skv, meta_kv, hist_kv, s0, rng, sink_tab, *, n_streams, seq, tl, has_hist):
    nt = seq // tl
    assert tl % CHUNK == 0 and seq % tl == 0
    assert tl >= WINDOW or (nt == 1 and has_hist)
    n_pairs = RET_HEADS // 2
    s0_shared = s0.shape[0] == 1
    row = lambda b, t: (b * nt + t, 0)
    const2 = lambda b, t: (0, 0)
    const3 = lambda b, t: (0, 0, 0)
    per_b3 = lambda b, t: (b, 0, 0)
    s0_map = (lambda b, t: (0, 0, 0, 0)) if s0_shared else (lambda b, t: (b, 0, 0, 0))
    hist_map = per_b3 if has_hist else const3
    rows = n_streams * seq
    return pl.pallas_call(
        functools.partial(_attn_kernel, tl=tl, has_hist=has_hist),
        grid=(n_streams, nt),
        in_specs=[pl.BlockSpec((tl, 2 * RET_Q), row),
                  pl.BlockSpec((tl, RET_V), row),
                  pl.BlockSpec((tl, RET_V), row),
                  pl.BlockSpec((tl, SWA_Q), row),
                  pl.BlockSpec((tl, 2 * SWA_KV), row),
                  pl.BlockSpec((N_META, 2 * SWA_KV), const2),
                  pl.BlockSpec((1, WINDOW, 2 * SWA_KV), hist_map),
                  pl.BlockSpec((1, n_pairs, 2 * RET_DK, RET_DV), s0_map),
                  pl.BlockSpec((RET_HEADS, RET_DV), const2),
                  pl.BlockSpec((SWA_KV_HEADS, SUBLANES, 4 * CHUNK), const3)],
        out_specs=[pl.BlockSpec((tl, MIX_WIDTH), row),
                   pl.BlockSpec((1, n_pairs, 2 * RET_DK, RET_DV), lambda b, t: (b, 0, 0, 0)),
                   pl.BlockSpec((1, WINDOW, 2 * SWA_KV), per_b3)],
        out_shape=[jax.ShapeDtypeStruct((rows, MIX_WIDTH), BF16),
                   jax.ShapeDtypeStruct((n_streams, n_pairs, 2 * RET_DK, RET_DV), F32),
                   jax.ShapeDtypeStruct((n_streams, WINDOW, 2 * SWA_KV), F32)],
        scratch_shapes=[pltpu.VMEM((n_pairs, 2 * RET_DK, RET_DV), F32),
                        pltpu.VMEM((SWA_KV_HEADS, WINDOW + tl, LANES), BF16),
                        pltpu.VMEM((SWA_KV_HEADS, WINDOW + tl, LANES), BF16),
                        pltpu.VMEM((SWA_KV_HEADS, META_ROWS, LANES), BF16),
                        pltpu.VMEM((SWA_KV_HEADS, META_ROWS, LANES), BF16),
                        pltpu.VMEM((RET_HEADS, tl, tl), F32),
                        pltpu.VMEM((n_pairs, tl, LANES), F32),
                        pltpu.VMEM((RET_HEADS, tl, RET_DV), F32),
                        pltpu.VMEM((n_pairs, 2 * RET_DK, RET_DV), F32)],
        compiler_params=_params(2),
        name="attention",
    )(rqk, rv, gate, sq, skv, meta_kv, hist_kv, s0, rng, sink_tab)


def _post_kernel(omix_ref, x_ref, wout_ref, g2_ref, wrt_ref, base_ref,
                 xmid_ref, wcol_ref, rt_ref, cnt_ref, tri_scr, run_scr):
    i = pl.program_id(0)
    tm = x_ref.shape[0]

    @pl.when(i == 0)
    def _init():
        r = lax.broadcasted_iota(jnp.int32, (tm, tm), 0)
        c = lax.broadcasted_iota(jnp.int32, (tm, tm), 1)
        tri_scr[...] = jnp.where(r < c, 1.0, 0.0).astype(BF16)
        run_scr[...] = base_ref[...]

    xm = x_ref[...] + jnp.dot(omix_ref[...], wout_ref[...], preferred_element_type=F32)
    xmid_ref[...] = xm
    hn = _row_rms(xm) * g2_ref[...]
    lt = lax.dot_general(wrt_ref[...], hn.astype(BF16), (((1,), (1,)), ((), ())), preferred_element_type=F32)
    row8 = lax.broadcasted_iota(jnp.int32, (SUBLANES, tm), 0)
    big = jnp.int32(SUBLANES)
    gl = jnp.where(row8 < N_GROUPS, lt[0:SUBLANES], NEG_INF)
    gmax = jnp.max(gl, axis=0, keepdims=True)
    gsum = jnp.sum(jnp.exp(gl - gmax), axis=0, keepdims=True)
    g_sel = jnp.min(jnp.where(gl == gmax, row8, big), axis=0, keepdims=True)
    p_sel = 1.0 / gsum
    el = lt[ROUTER_EXPERT_ROW0:ROUTER_EXPERT_ROW0 + EXPERTS_PER_GROUP]
    for g in range(1, N_GROUPS):
        lo = ROUTER_EXPERT_ROW0 + g * EXPERTS_PER_GROUP
        el = jnp.where(g_sel == g, lt[lo:lo + EXPERTS_PER_GROUP], el)
    m1 = jnp.max(el, axis=0, keepdims=True)
    i1 = jnp.min(jnp.where(el == m1, row8, big), axis=0, keepdims=True)
    el2 = jnp.where(row8 == i1, NEG_INF, el)
    m2 = jnp.max(el2, axis=0, keepdims=True)
    i2 = jnp.min(jnp.where(el2 == m2, row8, big), axis=0, keepdims=True)
    e2 = jnp.exp(m2 - m1)
    inv = 1.0 / (1.0 + e2)
    w1 = p_sel * inv
    w2 = p_sel * (e2 * inv)
    eid1 = g_sel * EXPERTS_PER_GROUP + i1
    eid2 = g_sel * EXPERTS_PER_GROUP + i2

    rowe = lax.broadcasted_iota(jnp.int32, (N_EXPERTS, tm), 0)
    oh = jnp.where((rowe == eid1) | (rowe == eid2), 1.0, 0.0).astype(BF16)
    run = run_scr[...]
    pref = jnp.dot(oh, tri_scr[...], preferred_element_type=F32) + jnp.concatenate([run] * (tm // LANES), axis=1)
    r1 = jnp.sum(jnp.where(rowe == eid1, pref, 0.0), axis=0, keepdims=True)
    r2 = jnp.sum(jnp.where(rowe == eid2, pref, 0.0), axis=0, keepdims=True)
    run = run + jnp.dot(oh, jnp.ones((tm, LANES), BF16), preferred_element_type=F32)
    run_scr[...] = run
    cnt_ref[...] = run

    out = jnp.zeros((SUBLANES, tm), F32)
    for k, v in enumerate([eid1.astype(F32), eid2.astype(F32), w1, w2, r1, r2]):
        out = jnp.where(row8 == k, v, out)
    rt_ref[...] = out
    rowl = lax.broadcasted_iota(jnp.int32, (LANES, tm), 0)
    wcol_ref[...] = jnp.where(rowl == 0, w1, jnp.where(rowl == 1, w2, 0.0)).T


def _post(omix, x2d, w_out_bf, g2, w_router_t_bf, base_cnt, tm):
    t_rows = x2d.shape[0]
    assert tm % LANES == 0
    row = lambda i: (i, 0)
    const = lambda i: (0, 0)
    return pl.pallas_call(
        _post_kernel,
        grid=(t_rows // tm,),
        in_specs=[pl.BlockSpec((tm, MIX_WIDTH), row),
                  pl.BlockSpec((tm, D_MODEL), row),
                  pl.BlockSpec((MIX_WIDTH, D_MODEL), const),
                  pl.BlockSpec((1, D_MODEL), const),
                  pl.BlockSpec((ROUTER_ROWS, D_MODEL), const),
                  pl.BlockSpec((N_EXPERTS, LANES), const)],
        out_specs=[pl.BlockSpec((tm, D_MODEL), row),
                   pl.BlockSpec((tm, LANES), row),
                   pl.BlockSpec((SUBLANES, tm), lambda i: (0, i)),
                   pl.BlockSpec((N_EXPERTS, LANES), const)],
        out_shape=[jax.ShapeDtypeStruct((t_rows, D_MODEL), F32),
                   jax.ShapeDtypeStruct((t_rows, LANES), F32),
                   jax.ShapeDtypeStruct((SUBLANES, t_rows), F32),
                   jax.ShapeDtypeStruct((N_EXPERTS, LANES), F32)],
        scratch_shapes=[pltpu.VMEM((tm, tm), BF16), pltpu.VMEM((N_EXPERTS, LANES), F32)],
        compiler_params=_params(1),
        name="post",
    )(omix, x2d, w_out_bf, g2, w_router_t_bf, base_cnt)


def _step_major(pos, tm):
    return [pos[e].reshape(-1, 1, tm) for e in range(2)]


def _dispatch_kernel(pos0_ref, pos1_ref, *refs, tm, group_steps):
    pos_refs = (pos0_ref, pos1_ref)
    n_g = len(group_steps)
    xmid_refs = refs[:n_g]
    g2_ref, xs_ref, hbuf, sems = refs[n_g:]
    n_steps = sum(group_steps)
    i = pl.program_id(0)
    slot = i % 2

    first = 0
    for xmid_ref, steps in zip(xmid_refs, group_steps):
        @pl.when((i >= first) & (i < first + steps))
        def _normalise(xmid_ref=xmid_ref):
            hn = _row_rms(xmid_ref[...]) * g2_ref[...]
            hbuf[slot] = hn.reshape(tm // SUBLANES, SUBLANES, D_MODEL)
        first += steps

    def body(blk, carry):
        for k in range(SUBLANES):
            r = blk * SUBLANES + k
            for e in range(2):
                pltpu.make_async_copy(hbuf.at[slot, blk, pl.ds(k, 1)], xs_ref.at[pl.ds(pos_refs[e][0, 0, r], 1)],
                                      sems.at[slot]).start(priority=e)
        return carry

    lax.fori_loop(0, tm // SUBLANES, body, 0)

    def drain(which):
        pltpu.make_async_copy(xs_ref.at[pl.ds(0, 2 * tm)], xs_ref.at[pl.ds(0, 2 * tm)], sems.at[which]).wait()

    @pl.when(i > 0)
    def _previous():
        drain(1 - slot)

    @pl.when(i == n_steps - 1)
    def _last():
        drain(slot)


def _dispatch(pos_list, xmid_list, g2, tm):
    group_steps = tuple(x.shape[0] // tm for x in xmid_list)
    n_steps = sum(group_steps)
    assert tm % SUBLANES == 0
    pos3 = [jnp.concatenate(parts, axis=0) for parts in zip(*[_step_major(p, tm) for p in pos_list])]
    smem = pl.BlockSpec((1, 1, tm), lambda i: (i, 0, 0), memory_space=pltpu.SMEM)
    x_specs = []
    first = 0
    for steps in group_steps:
        x_specs.append(pl.BlockSpec((tm, D_MODEL),
                                    lambda i, first=first, steps=steps: (jnp.clip(i - first, 0, steps - 1), 0)))
        first += steps
    n_rows_out = 2 * sum(x.shape[0] for x in xmid_list)
    return pl.pallas_call(
        functools.partial(_dispatch_kernel, tm=tm, group_steps=group_steps),
        grid=(n_steps,),
        in_specs=[smem, smem] + x_specs + [pl.BlockSpec((1, D_MODEL), lambda i: (0, 0))],
        out_specs=pl.BlockSpec(memory_space=pl.ANY),
        out_shape=jax.ShapeDtypeStruct((n_rows_out, D_MODEL), F32),
        scratch_shapes=[pltpu.VMEM((2, tm // SUBLANES, SUBLANES, D_MODEL), F32), pltpu.SemaphoreType.DMA((2,))],
        compiler_params=_params(1),
        name="dispatch",
    )(*pos3, *xmid_list, g2)


def _expert_kernel(vt_ref, ve_ref, lo_ref, hi_ref, nv_ref, x_hbm, wg_ref, wu_ref, wd_ref, y_ref, wgu_scr, wd_scr,
                   xbuf, xsem, *, tm, n_tiles):
    v = pl.program_id(0)

    def fetch(t):
        start = t * tm if isinstance(t, int) else pl.multiple_of(t * tm, tm)
        slot = t % EXPERT_RING
        return pltpu.make_async_copy(x_hbm.at[pl.ds(start, tm)], xbuf.at[slot], xsem.at[slot])

    @pl.when(v == 0)
    def _prime():
        for t in range(min(EXPERT_RING - 1, n_tiles)):
            fetch(t).start()

    @pl.when(v < nv_ref[0])
    def _compute():
        t = vt_ref[v]

        @pl.when((v == 0) | (t != vt_ref[jnp.maximum(v - 1, 0)]))
        def _new_tile():
            fetch(t).wait()

            @pl.when(t + EXPERT_RING - 1 < n_tiles)
            def _ahead():
                fetch(t + EXPERT_RING - 1).start()

        @pl.when((v == 0) | (ve_ref[v] != ve_ref[jnp.maximum(v - 1, 0)]))
        def _new_expert():
            wgu_scr[:, 0:EXPERT_FF] = wg_ref[0].astype(BF16)
            wgu_scr[:, EXPERT_FF:2 * EXPERT_FF] = wu_ref[0].astype(BF16)
            wd_scr[...] = wd_ref[0].astype(BF16)

        x = xbuf[t % EXPERT_RING].astype(BF16)
        gu = jnp.dot(x, wgu_scr[...], preferred_element_type=F32)
        g = gu[:, 0:EXPERT_FF]
        a = (g * jax.nn.sigmoid(g) * gu[:, EXPERT_FF:2 * EXPERT_FF]).astype(BF16)
        y = jnp.dot(a, wd_scr[...], preferred_element_type=F32)
        lo, hi = lo_ref[v], hi_ref[v]

        @pl.when(lo == 0)
        def _first_visit():
            y_ref[...] = y

        @pl.when(lo > 0)
        def _later_visit():
            row = lax.broadcasted_iota(jnp.int32, y.shape, 0)
            y_ref[...] = jnp.where((row >= lo) & (row < hi), y, y_ref[...])


def _experts(vis_tile, vis_expert, vis_lo, vis_hi, n_vis, xs, w_gate, w_up, w_down, tm):
    n_steps = vis_tile.shape[0]
    last = lambda v, nv: jnp.minimum(v, nv[0] - 1)
    of_expert = lambda v, vt, ve, lo, hi, nv: (ve[last(v, nv)], 0, 0)
    assert xs.shape[0] % tm == 0
    return pl.pallas_call(
        functools.partial(_expert_kernel, tm=tm, n_tiles=xs.shape[0] // tm),
        grid_spec=pltpu.PrefetchScalarGridSpec(
            num_scalar_prefetch=5,
            grid=(n_steps,),
            in_specs=[pl.BlockSpec(memory_space=pl.ANY),
                      pl.BlockSpec((1, D_MODEL, EXPERT_FF), of_expert),
                      pl.BlockSpec((1, D_MODEL, EXPERT_FF), of_expert),
                      pl.BlockSpec((1, EXPERT_FF, D_MODEL), of_expert)],
            out_specs=pl.BlockSpec((tm, D_MODEL), lambda v, vt, ve, lo, hi, nv: (vt[last(v, nv)], 0)),
            scratch_shapes=[pltpu.VMEM((D_MODEL, 2 * EXPERT_FF), BF16), pltpu.VMEM((EXPERT_FF, D_MODEL), BF16),
                            pltpu.VMEM((EXPERT_RING, tm, D_MODEL), F32), pltpu.SemaphoreType.DMA((EXPERT_RING,))]),
        out_shape=jax.ShapeDtypeStruct(xs.shape, F32),
        compiler_params=_params(1),
        name="experts",
    )(vis_tile, vis_expert, vis_lo, vis_hi, n_vis, xs, w_gate, w_up, w_down)


def _combine_kernel(pos0_ref, pos1_ref, nxt0_ref, nxt1_ref, ys_ref, xmid_ref, wcol_ref, out_ref, ybuf, sems,
                    *, tm, n_steps):
    i = pl.program_id(0)
    slot = i % 2

    def issue(p_refs, to_slot):
        def body(blk, carry):
            for k in range(SUBLANES):
                r = blk * SUBLANES + k
                for e in range(2):
                    pltpu.make_async_copy(ys_ref.at[pl.ds(p_refs[e][0, 0, r], 1)],
                                          ybuf.at[to_slot, e, blk, pl.ds(k, 1)], sems.at[to_slot]).start(priority=e)
            return carry

        lax.fori_loop(0, tm // SUBLANES, body, 0)

    @pl.when(i == 0)
    def _first():
        issue((pos0_ref, pos1_ref), 0)

    @pl.when(i + 1 < n_steps)
    def _ahead():
        issue((nxt0_ref, nxt1_ref), 1 - slot)

    for e in range(2):
        pltpu.make_async_copy(ybuf.at[slot, e], ybuf.at[slot, e], sems.at[slot]).wait()
    w = wcol_ref[...]
    y0 = ybuf[slot, 0].reshape(tm, D_MODEL)
    y1 = ybuf[slot, 1].reshape(tm, D_MODEL)
    out_ref[...] = xmid_ref[...] + w[:, 0:1] * y0 + w[:, 1:2] * y1


def _combine(pos, ys, xmid, wcol, tm):
    t_rows = xmid.shape[0]
    n_steps = t_rows // tm
    assert tm % SUBLANES == 0
    pos3 = _step_major(pos, tm)
    row = lambda i: (i, 0)
    cur = pl.BlockSpec((1, 1, tm), lambda i: (i, 0, 0), memory_space=pltpu.SMEM)
    nxt = pl.BlockSpec((1, 1, tm), lambda i: (jnp.minimum(i + 1, n_steps - 1), 0, 0), memory_space=pltpu.SMEM)
    return pl.pallas_call(
        functools.partial(_combine_kernel, tm=tm, n_steps=n_steps),
        grid=(n_steps,),
        in_specs=[cur, cur, nxt, nxt,
                  pl.BlockSpec(memory_space=pl.ANY),
                  pl.BlockSpec((tm, D_MODEL), row),
                  pl.BlockSpec((tm, LANES), row)],
        out_specs=pl.BlockSpec((tm, D_MODEL), row),
        out_shape=jax.ShapeDtypeStruct((t_rows, D_MODEL), F32),
        scratch_shapes=[pltpu.VMEM((2, 2, tm // SUBLANES, SUBLANES, D_MODEL), F32), pltpu.SemaphoreType.DMA((2,))],
        compiler_params=_params(1),
        name="combine",
    )(*pos3, *pos3, ys, xmid, wcol)


def _bucket(ends, idx):
    n = jnp.sum((ends[None, :] <= idx[:, None]).astype(jnp.int32), axis=1)
    return jnp.minimum(n, ends.shape[0] - 1)


def _tile_for(rows, pref):
    tm = min(pref, rows)
    assert rows % tm == 0
    return tm


def kernel(x_prompt, x_sample, cache_ret_state, cache_swa_k, cache_swa_v, meta_tokens, norm1_g, w_in, q_norm_g,
           k_norm_g, ret_norm_g, attn_sinks, w_out, norm2_g, w_group, w_expert, w_gate, w_up, w_down):
    assert norm1_g.shape[0] == 1, "single-layer trunk"
    bp, lp, _ = x_prompt.shape
    bs, ls, _ = x_sample.shape
    n_pairs = RET_HEADS // 2

    g1 = norm1_g[0][None, :]
    g2 = norm2_g[0][None, :]
    w_in_bf = w_in[0].astype(BF16)
    w_out_bf = w_out[0].astype(BF16)
    qg2 = jnp.tile(q_norm_g[0], 2)[None, :]
    kg2 = jnp.tile(k_norm_g[0], 2)[None, :]
    rng = ret_norm_g[0].reshape(RET_HEADS, RET_DV)
    sink_tab = jnp.broadcast_to(jnp.repeat(attn_sinks[0], CHUNK).reshape(SWA_KV_HEADS, 1, 4 * CHUNK),
                                (SWA_KV_HEADS, SUBLANES, 4 * CHUNK))
    w_router_t = jnp.zeros((ROUTER_ROWS, D_MODEL), F32)
    w_router_t = w_router_t.at[0:N_GROUPS].set(w_group[0].T)
    w_router_t = w_router_t.at[ROUTER_EXPERT_ROW0:ROUTER_EXPERT_ROW0 + N_EXPERTS].set(w_expert[0].T)
    w_router_t_bf = w_router_t.astype(BF16)

    meta_rows = 2 * CHUNK
    m_pad = jnp.zeros((meta_rows, D_MODEL), F32).at[0:N_META].set(meta_tokens)
    m_rqk, m_rv, _, _, m_skv = _proj(m_pad, jnp.arange(meta_rows, dtype=jnp.int32), meta_rows, g1, w_in_bf, qg2, kg2)
    s_meta = _meta_state(m_rqk, m_rv)[None]
    meta_kv = m_skv[0:N_META]

    groups = [
        dict(x=x_prompt.reshape(bp * lp, D_MODEL), n=bp, seq=lp, pos0=N_META, has_hist=False, s0=s_meta,
             hist=jnp.zeros((1, WINDOW, 2 * SWA_KV), F32)),
        dict(x=x_sample.reshape(bs * ls, D_MODEL), n=bs, seq=ls, pos0=N_META + PAST_LEN, has_hist=True,
             s0=cache_ret_state[0].reshape(bs, n_pairs, 2 * RET_DK, RET_DV),
             hist=jnp.concatenate([cache_swa_k[0].reshape(bs, WINDOW, SWA_KV),
                                   cache_swa_v[0].reshape(bs, WINDOW, SWA_KV)], axis=-1)),
    ]

    base_cnt = jnp.zeros((N_EXPERTS, LANES), F32)
    for g in groups:
        rows = g["n"] * g["seq"]
        tm = _tile_for(rows, PROJ_TILE)
        pos = g["pos0"] + jnp.arange(g["seq"], dtype=jnp.int32)
        if g["seq"] < tm:
            assert tm % g["seq"] == 0
            pos = jnp.tile(pos, tm // g["seq"])
        else:
            assert g["seq"] % tm == 0
        rqk, rv, gate, sq, skv = _proj(g["x"], pos, tm, g1, w_in_bf, qg2, kg2)
        tl = min(ATTN_TILE, g["seq"])
        omix, s_out, kv_out = _attention(rqk, rv, gate, sq, skv, meta_kv, g["hist"], g["s0"], rng, sink_tab,
                                         n_streams=g["n"], seq=g["seq"], tl=tl, has_hist=g["has_hist"])
        xmid, wcol, route_t, base_cnt = _post(omix, g["x"], w_out_bf, g2, w_router_t_bf, base_cnt,
                                              _tile_for(rows, POST_TILE))
        g.update(xmid=xmid, wcol=wcol, route_t=route_t, s_out=s_out, kv_out=kv_out)

    te = EXPERT_TILE
    total_rows = sum(g["n"] * g["seq"] for g in groups)
    assert (2 * total_rows) % te == 0
    n_row_tiles = (2 * total_rows) // te
    counts = base_cnt[:, 0].astype(jnp.int32)
    off = jnp.cumsum(counts) - counts
    first_tile = off // te
    n_vis_e = jnp.where(counts > 0, (off + counts - 1) // te - first_tile + 1, 0)
    vis_end = jnp.cumsum(n_vis_e)
    n_vis = vis_end[-1:].astype(jnp.int32)
    v = jnp.arange(n_row_tiles + N_EXPERTS, dtype=jnp.int32)
    vis_expert = _bucket(vis_end, v)
    pick = lambda table: jnp.sum(jnp.where(vis_expert[:, None] == jnp.arange(N_EXPERTS, dtype=jnp.int32)[None, :],
                                           table[None, :], 0), axis=1)
    vis_tile = jnp.clip(pick(first_tile) + v - pick(vis_end - n_vis_e), 0, n_row_tiles - 1).astype(jnp.int32)
    vis_lo = jnp.clip(pick(off) - vis_tile * te, 0, te).astype(jnp.int32)
    vis_hi = jnp.clip(pick(off + counts) - vis_tile * te, 0, te).astype(jnp.int32)

    for g in groups:
        eid = g["route_t"][0:2].astype(jnp.int32)
        off_sel = jnp.sum(jnp.where(eid[None] == jnp.arange(N_EXPERTS, dtype=jnp.int32)[:, None, None],
                                    off[:, None, None], 0), axis=0)
        g["pos"] = (off_sel + g["route_t"][4:6].astype(jnp.int32)).astype(jnp.int32)
    xs = _dispatch([g["pos"] for g in groups], [g["xmid"] for g in groups], g2,
                   _tile_for(min(g["n"] * g["seq"] for g in groups), MOVE_TILE))

    ys = _experts(vis_tile, vis_expert.astype(jnp.int32), vis_lo, vis_hi, n_vis, xs, w_gate[0], w_up[0], w_down[0], te)

    outs = []
    for g in groups:
        rows = g["n"] * g["seq"]
        y = _combine(g["pos"], ys, g["xmid"], g["wcol"], _tile_for(rows, MOVE_TILE))
        outs.append(y.reshape(g["n"], g["seq"], D_MODEL))

    def caches(g):
        kv = g["kv_out"]
        k = kv[:, :, 0:SWA_KV].reshape(g["n"], WINDOW, SWA_KV_HEADS, SWA_HD)[None]
        v = kv[:, :, SWA_KV:2 * SWA_KV].reshape(g["n"], WINDOW, SWA_KV_HEADS, SWA_HD)[None]
        s = g["s_out"].reshape(g["n"], RET_HEADS, RET_DK, RET_DV)[None]
        return s, k, v

    sp, kp, vp = caches(groups[0])
    ss, ks, vs = caches(groups[1])
    return (outs[0], outs[1], sp, kp, vp, ss, ks, vs)
```

```python
import functools

import numpy as np
import jax
import jax.numpy as jnp
from jax import lax
from jax.experimental import pallas as pl
from jax.experimental.pallas import tpu as pltpu

F32 = jnp.float32
BF16 = jnp.bfloat16

D_MODEL = 1024
PAST_LEN = 4096
CHUNK = 64
N_META = 16
RET_HEADS = 4
RET_DK = 64
RET_DV = 128
SWA_HEADS = 8
SWA_KV_HEADS = 2
SWA_HD = 64
WINDOW = 128
ROPE_THETA = 10000.0
N_GROUPS = 4
EXPERTS_PER_GROUP = 8
N_EXPERTS = N_GROUPS * EXPERTS_PER_GROUP
EXPERT_FF = 256
EPS = 1e-6
NEG_INF = -1e30
LOG2E = float(np.log2(np.e))
RET_Q = RET_HEADS * RET_DK
RET_V = RET_HEADS * RET_DV
SWA_Q = SWA_HEADS * SWA_HD
SWA_KV = SWA_KV_HEADS * SWA_HD
MIX_WIDTH = RET_V + SWA_Q
IN_WIDTH = 2 * RET_Q + 2 * RET_V + SWA_Q + 2 * SWA_KV

LANES = 128
PROJ_TILE = 1024
POST_TILE = 1024
ATTN_TILE = 256
EXPERT_TILE = 1024
EXPERT_RING = 4
MOVE_TILE = 512
SUBLANES = 8
ROUTER_EXPERT_ROW0 = 8
ROUTER_ROWS = 64
META_ROWS = 32
VMEM_LIMIT = 56 * 1024 * 1024

_LOG_G = [float(np.log1p(-np.exp2(-5.0 - h))) for h in range(RET_HEADS)]


def _params(n_axes):
    return pltpu.CompilerParams(dimension_semantics=("arbitrary",) * n_axes, vmem_limit_bytes=VMEM_LIMIT)


def _split_bf16(a):
    hi = a.astype(BF16)
    return hi, (a - hi.astype(F32)).astype(BF16)


def _split_dot(a, w2):
    hi, lo = _split_bf16(a)
    return jnp.dot(jnp.concatenate([hi, lo], axis=1), w2, preferred_element_type=F32)


def _lane_sum(a):
    return _split_dot(a, jnp.ones((2 * LANES, LANES), BF16))


def _head_sum_matrix():
    i = lax.broadcasted_iota(jnp.int32, (2 * LANES, LANES), 0) % LANES
    j = lax.broadcasted_iota(jnp.int32, (2 * LANES, LANES), 1)
    return jnp.where((i < SWA_HD) == (j < SWA_HD), 1.0, 0.0).astype(BF16)


def _rope(t, c, s1, s2):
    half = SWA_HD // 2
    return t * c + pltpu.roll(t, LANES - half, 1) * s1 + pltpu.roll(t, half, 1) * s2


def _head_rms(t, g, head_w):
    ms = _split_dot(t * t, head_w) * (1.0 / SWA_HD)
    return t * lax.rsqrt(ms + EPS) * g


def _head_rms_pair(ta, tb, g, pair_w):
    ss = jnp.concatenate([(ta * ta).astype(BF16), (tb * tb).astype(BF16)], axis=1)
    ms = jnp.dot(ss, pair_w, preferred_element_type=F32) * (1.0 / SWA_HD)
    return ta * lax.rsqrt(ms[:, 0:LANES] + EPS) * g, tb * lax.rsqrt(ms[:, LANES:2 * LANES] + EPS) * g


def _row_rms(x):
    n_tiles = x.shape[1] // LANES
    ss = x[:, 0:LANES] * x[:, 0:LANES]
    for j in range(1, n_tiles):
        ss = ss + x[:, j * LANES:(j + 1) * LANES] * x[:, j * LANES:(j + 1) * LANES]
    r = lax.rsqrt(_lane_sum(ss) * (1.0 / x.shape[1]) + EPS)
    return x * jnp.concatenate([r] * n_tiles, axis=1)


def _proj_kernel(x_ref, g1_ref, w_ref, qg_ref, kg_ref, cos_ref, s1_ref, s2_ref,
                 rqk_ref, rv_ref, gate_ref, sq_ref, skv_ref):
    xn = (_row_rms(x_ref[...]) * g1_ref[...]).astype(BF16)
    c, s1, s2 = cos_ref[...], s1_ref[...], s2_ref[...]
    head_w = _head_sum_matrix()

    def seg(a, b):
        return jnp.dot(xn, w_ref[:, a:b], preferred_element_type=F32)

    def tile(h, j):
        return h[:, j * LANES:(j + 1) * LANES]

    h = seg(0, 2 * RET_Q)
    for j in range(2):
        rqk_ref[:, j * LANES:(j + 1) * LANES] = _rope(tile(h, j), c, s1, s2).astype(BF16)
    for j in range(2, 4):
        rqk_ref[:, j * LANES:(j + 1) * LANES] = (_rope(tile(h, j), c, s1, s2) * (RET_DK ** -0.5)).astype(BF16)
    a = 2 * RET_Q
    rv_ref[...] = seg(a, a + RET_V).astype(BF16)
    a += RET_V
    g = seg(a, a + RET_V)
    gate_ref[...] = (g * jax.nn.sigmoid(g)).astype(BF16)
    a += RET_V
    h = seg(a, a + SWA_Q)
    qg = qg_ref[...]
    pair_w = jnp.concatenate([jnp.concatenate([head_w[0:LANES], jnp.zeros((LANES, LANES), BF16)], axis=1),
                              jnp.concatenate([jnp.zeros((LANES, LANES), BF16), head_w[0:LANES]], axis=1)], axis=0)
    for j in range(0, SWA_Q // LANES, 2):
        qa, qb = _head_rms_pair(tile(h, j), tile(h, j + 1), qg, pair_w)
        sq_ref[:, j * LANES:(j + 1) * LANES] = _rope(qa, c, s1, s2).astype(BF16)
        sq_ref[:, (j + 1) * LANES:(j + 2) * LANES] = _rope(qb, c, s1, s2).astype(BF16)
    a += SWA_Q
    h = seg(a, a + 2 * SWA_KV)
    skv_ref[:, 0:LANES] = _rope(_head_rms(tile(h, 0), kg_ref[...], head_w), c, s1, s2)
    skv_ref[:, LANES:2 * LANES] = tile(h, 1)


def _rope_tables(pos):
    half = SWA_HD // 2
    inv = ROPE_THETA ** (-jnp.arange(half, dtype=F32) / half)
    ang = pos.astype(F32)[:, None] * inv[None, :]
    cos, sin = jnp.cos(ang), jnp.sin(ang)
    z = jnp.zeros_like(sin)
    heads_per_tile = LANES // SWA_HD
    return (jnp.tile(cos, (1, 2 * heads_per_tile)),
            jnp.tile(jnp.concatenate([-sin, z], axis=1), (1, heads_per_tile)),
            jnp.tile(jnp.concatenate([z, sin], axis=1), (1, heads_per_tile)))


def _proj(x2d, pos_rows, tm, g1, w_in_bf, qg2, kg2):
    t_rows = x2d.shape[0]
    n_tiles = t_rows // tm
    n_pos_tiles = pos_rows.shape[0] // tm
    cos, s1, s2 = _rope_tables(pos_rows)
    row = lambda i: (i, 0)
    const = lambda i: (0, 0)
    tab = lambda i: (i % n_pos_tiles, 0)
    return pl.pallas_call(
        _proj_kernel,
        grid=(n_tiles,),
        in_specs=[pl.BlockSpec((tm, D_MODEL), row),
                  pl.BlockSpec((1, D_MODEL), const),
                  pl.BlockSpec((D_MODEL, IN_WIDTH), const),
                  pl.BlockSpec((1, LANES), const),
                  pl.BlockSpec((1, LANES), const),
                  pl.BlockSpec((tm, LANES), tab),
                  pl.BlockSpec((tm, LANES), tab),
                  pl.BlockSpec((tm, LANES), tab)],
        out_specs=[pl.BlockSpec((tm, 2 * RET_Q), row),
                   pl.BlockSpec((tm, RET_V), row),
                   pl.BlockSpec((tm, RET_V), row),
                   pl.BlockSpec((tm, SWA_Q), row),
                   pl.BlockSpec((tm, 2 * SWA_KV), row)],
        out_shape=[jax.ShapeDtypeStruct((t_rows, 2 * RET_Q), BF16),
                   jax.ShapeDtypeStruct((t_rows, RET_V), BF16),
                   jax.ShapeDtypeStruct((t_rows, RET_V), BF16),
                   jax.ShapeDtypeStruct((t_rows, SWA_Q), BF16),
                   jax.ShapeDtypeStruct((t_rows, 2 * SWA_KV), F32)],
        compiler_params=_params(1),
        name="proj",
    )(x2d, g1, w_in_bf, qg2, kg2, cos, s1, s2)


def _pair_update(k_bf, v0_bf, v1_bf, wt):
    kw = (k_bf.astype(F32) * wt).astype(BF16)
    dn = (((0,), (0,)), ((), ()))
    a0 = lax.dot_general(kw, v0_bf, dn, preferred_element_type=F32)
    a1 = lax.dot_general(kw, v1_bf, dn, preferred_element_type=F32)
    top = lax.broadcasted_iota(jnp.int32, a0.shape, 0) < RET_DK
    return jnp.where(top, a0, a1)


def _decay_rows(n, pair, rows_back_from):
    i = lax.broadcasted_iota(jnp.int32, (n, LANES), 0).astype(F32)
    lane = lax.broadcasted_iota(jnp.int32, (n, LANES), 1)
    lg = jnp.where(lane < RET_DK, _LOG_G[2 * pair], _LOG_G[2 * pair + 1])
    return jnp.exp((rows_back_from - i) * lg)


def _meta_state_kernel(rqk_ref, rv_ref, s_ref, *, n_rows):
    for p in range(RET_HEADS // 2):
        k = rqk_ref[:, RET_Q + p * LANES:RET_Q + (p + 1) * LANES]
        wt = _decay_rows(n_rows, p, float(N_META - 1))
        s_ref[p] = _pair_update(k, rv_ref[:, (2 * p) * LANES:(2 * p + 1) * LANES],
                                rv_ref[:, (2 * p + 1) * LANES:(2 * p + 2) * LANES], wt)


def _meta_state(m_rqk, m_rv):
    n_rows = m_rqk.shape[0]
    return pl.pallas_call(
        functools.partial(_meta_state_kernel, n_rows=n_rows),
        out_shape=jax.ShapeDtypeStruct((RET_HEADS // 2, 2 * RET_DK, RET_DV), F32),
        name="meta_state",
    )(m_rqk, m_rv)


def _dup_halves(a, lo_mask):
    sw = pltpu.roll(a, SWA_HD, 1)
    return jnp.where(lo_mask, a, sw), jnp.where(lo_mask, sw, a)


def _attn_kernel(rqk_ref, rv_ref, gate_ref, sq_ref, skv_ref, meta_ref, hist_ref, s0_ref, rng_ref, sink_ref,
                 omix_ref, sout_ref, kvout_ref,
                 s_scr, kd_scr, vd_scr, mk_scr, mv_scr, dec_scr, wt_scr, cs_scr, gam_scr,
                 *, tl, has_hist):
    b = pl.program_id(0)
    t = pl.program_id(1)
    nt = pl.num_programs(1)
    n_chunks = tl // CHUNK
    n_pairs = RET_HEADS // 2
    lo_tl = lax.broadcasted_iota(jnp.int32, (tl, LANES), 1) < SWA_HD
    lo_c = lax.broadcasted_iota(jnp.int32, (CHUNK, LANES), 1) < SWA_HD

    @pl.when((b == 0) & (t == 0))
    def _tables():
        i = lax.broadcasted_iota(jnp.int32, (tl, tl), 0)
        j = lax.broadcasted_iota(jnp.int32, (tl, tl), 1)
        diff = (i - j).astype(F32)
        row = lax.broadcasted_iota(jnp.int32, (tl, LANES), 0).astype(F32)
        for h in range(RET_HEADS):
            dec_scr[h] = jnp.where(diff >= 0.0, jnp.exp(jnp.maximum(diff, 0.0) * _LOG_G[h]), 0.0)
            cs_scr[h] = jnp.exp((row + 1.0) * _LOG_G[h])
        top = lax.broadcasted_iota(jnp.int32, (2 * RET_DK, RET_DV), 0) < RET_DK
        for p in range(n_pairs):
            wt_scr[p] = _decay_rows(tl, p, float(tl - 1))
            gam_scr[p] = jnp.where(top, jnp.exp(jnp.float32(tl * _LOG_G[2 * p])), jnp.exp(jnp.float32(tl * _LOG_G[2 * p + 1])))
        lo_m = lax.broadcasted_iota(jnp.int32, (N_META, LANES), 1) < SWA_HD
        mk0, mk1 = _dup_halves(meta_ref[:, 0:LANES], lo_m)
        mv0, mv1 = _dup_halves(meta_ref[:, LANES:2 * LANES], lo_m)
        mk_scr[...] = jnp.zeros(mk_scr.shape, BF16)
        mv_scr[...] = jnp.zeros(mv_scr.shape, BF16)
        mk_scr[0, 0:N_META] = mk0.astype(BF16)
        mk_scr[1, 0:N_META] = mk1.astype(BF16)
        mv_scr[0, 0:N_META] = mv0.astype(BF16)
        mv_scr[1, 0:N_META] = mv1.astype(BF16)

    @pl.when(t == 0)
    def _stream_start():
        s_scr[...] = s0_ref[0]
        if has_hist:
            lo_w = lax.broadcasted_iota(jnp.int32, (WINDOW, LANES), 1) < SWA_HD
            k0, k1 = _dup_halves(hist_ref[0, :, 0:LANES], lo_w)
            v0, v1 = _dup_halves(hist_ref[0, :, LANES:2 * LANES], lo_w)
            kd_scr[0, 0:WINDOW] = k0.astype(BF16)
            kd_scr[1, 0:WINDOW] = k1.astype(BF16)
            vd_scr[0, 0:WINDOW] = v0.astype(BF16)
            vd_scr[1, 0:WINDOW] = v1.astype(BF16)
        else:
            z = jnp.zeros((WINDOW, LANES), BF16)
            for kv in range(SWA_KV_HEADS):
                kd_scr[kv, 0:WINDOW] = z
                vd_scr[kv, 0:WINDOW] = z

    k0, k1 = _dup_halves(skv_ref[:, 0:LANES], lo_tl)
    v0, v1 = _dup_halves(skv_ref[:, LANES:2 * LANES], lo_tl)
    kd_scr[0, WINDOW:WINDOW + tl] = k0.astype(BF16)
    kd_scr[1, WINDOW:WINDOW + tl] = k1.astype(BF16)
    vd_scr[0, WINDOW:WINDOW + tl] = v0.astype(BF16)
    vd_scr[1, WINDOW:WINDOW + tl] = v1.astype(BF16)

    band = WINDOW + CHUNK
    n_keys = META_ROWS + band
    n_q = 4 * CHUNK
    scale2 = (SWA_HD ** -0.5) * LOG2E
    krow = lax.broadcasted_iota(jnp.int32, (n_keys, n_q), 0)
    zero_c = jnp.zeros((CHUNK, LANES), BF16)
    ones_v = jnp.ones((n_keys, LANES), BF16)
    for c in range(n_chunks):
        if has_hist:
            first_valid = META_ROWS
        else:
            first_valid = jnp.where(t == 0, max(META_ROWS + WINDOW - c * CHUNK, META_ROWS), META_ROWS)
        valid_t = (krow < N_META) | (krow >= first_valid)
        r0 = c * CHUNK
        for kv in range(SWA_KV_HEADS):
            keys = jnp.concatenate([mk_scr[kv], kd_scr[kv, r0:r0 + band]], axis=0)
            vals = jnp.concatenate([mv_scr[kv], vd_scr[kv, r0:r0 + band]], axis=0)
            qa = sq_ref[r0:r0 + CHUNK, (2 * kv) * LANES:(2 * kv + 1) * LANES]
            qb = sq_ref[r0:r0 + CHUNK, (2 * kv + 1) * LANES:(2 * kv + 2) * LANES]
            lhs = jnp.concatenate([jnp.where(lo_c, qa, zero_c), jnp.where(lo_c, zero_c, qa),
                                   jnp.where(lo_c, qb, zero_c), jnp.where(lo_c, zero_c, qb)], axis=0)
            s_t = lax.dot_general(keys, lhs, (((1,), (1,)), ((), ())), preferred_element_type=F32) * scale2
            s_t = jnp.where(valid_t, s_t, NEG_INF)
            s_t = jnp.where(krow == N_META, sink_ref[kv, 0:1, :] * LOG2E, s_t)
            e_t = jnp.exp2(s_t - jnp.max(s_t, axis=0, keepdims=True)).astype(BF16)
            ov = lax.dot_general(e_t, jnp.concatenate([vals, ones_v], axis=1), (((0,), (0,)), ((), ())),
                                 preferred_element_type=F32)
            o = ov[:, 0:LANES] * (1.0 / ov[:, LANES:2 * LANES])
            oa = jnp.where(lo_c, o[0:CHUNK], o[CHUNK:2 * CHUNK])
            ob = jnp.where(lo_c, o[2 * CHUNK:3 * CHUNK], o[3 * CHUNK:4 * CHUNK])
            base = RET_V + (2 * kv) * LANES
            omix_ref[r0:r0 + CHUNK, base:base + LANES] = oa.astype(BF16)
            omix_ref[r0:r0 + CHUNK, base + LANES:base + 2 * LANES] = ob.astype(BF16)

    zero_t = jnp.zeros((tl, LANES), BF16)
    rr = lax.broadcasted_iota(jnp.int32, (2 * LANES, 2 * LANES), 0) < LANES
    cc = lax.broadcasted_iota(jnp.int32, (2 * LANES, 2 * LANES), 1) < LANES
    ones_pair = jnp.where(rr == cc, 1.0, 0.0).astype(BF16)
    for p in range(n_pairs):
        q = rqk_ref[:, p * LANES:(p + 1) * LANES]
        k = rqk_ref[:, RET_Q + p * LANES:RET_Q + (p + 1) * LANES]
        lhs = jnp.concatenate([jnp.where(lo_tl, q, zero_t), jnp.where(lo_tl, zero_t, q)], axis=0)
        s = lax.dot_general(lhs, k, (((1,), (1,)), ((), ())), preferred_element_type=F32)
        cross = jnp.dot(lhs, s_scr[p].astype(BF16), preferred_element_type=F32)
        outs = []
        for i in range(2):
            h = 2 * p + i
            v = rv_ref[:, h * LANES:(h + 1) * LANES]
            a = (s[i * tl:(i + 1) * tl] * dec_scr[h]).astype(BF16)
            outs.append(jnp.dot(a, v, preferred_element_type=F32) + cross[i * tl:(i + 1) * tl] * cs_scr[h])
        ss = jnp.concatenate([(o * o).astype(BF16) for o in outs], axis=1)
        ms = jnp.dot(ss, ones_pair, preferred_element_type=F32) * (1.0 / RET_DV)
        for i, o in enumerate(outs):
            h = 2 * p + i
            r = o * lax.rsqrt(ms[:, i * LANES:(i + 1) * LANES] + EPS) * rng_ref[h:h + 1, :]
            omix_ref[:, h * LANES:(h + 1) * LANES] = (r * gate_ref[:, h * LANES:(h + 1) * LANES].astype(F32)).astype(BF16)
        u = _pair_update(k, rv_ref[:, (2 * p) * LANES:(2 * p + 1) * LANES],
                         rv_ref[:, (2 * p + 1) * LANES:(2 * p + 2) * LANES], wt_scr[p])
        s_scr[p] = gam_scr[p] * s_scr[p] + u

    if tl >= WINDOW:
        @pl.when(t + 1 < nt)
        def _carry_window():
            for kv in range(SWA_KV_HEADS):
                kd_scr[kv, 0:WINDOW] = kd_scr[kv, tl:tl + WINDOW]
                vd_scr[kv, 0:WINDOW] = vd_scr[kv, tl:tl + WINDOW]

    @pl.when(t + 1 == nt)
    def _stream_end():
        sout_ref[0] = s_scr[...]
        if tl >= WINDOW:
            kvout_ref[0] = skv_ref[tl - WINDOW:tl, :]
        else:
            kvout_ref[0, 0:WINDOW - tl] = hist_ref[0, tl:WINDOW, :]
            kvout_ref[0, WINDOW - tl:WINDOW] = skv_ref[...]


def _attention(rqk, rv, gate, sq, skv, meta_kv, hist_kv, s0, rng, sink_tab, *, n_streams, seq, tl, has_hist):
    nt = seq // tl
    assert tl % CHUNK == 0 and seq % tl == 0
    assert tl >= WINDOW or (nt == 1 and has_hist)
    n_pairs = RET_HEADS // 2
    s0_shared = s0.shape[0] == 1
    row = lambda b, t: (b * nt + t, 0)
    const2 = lambda b, t: (0, 0)
    const3 = lambda b, t: (0, 0, 0)
    per_b3 = lambda b, t: (b, 0, 0)
    s0_map = (lambda b, t: (0, 0, 0, 0)) if s0_shared else (lambda b, t: (b, 0, 0, 0))
    hist_map = per_b3 if has_hist else const3
    rows = n_streams * seq
    return pl.pallas_call(
        functools.partial(_attn_kernel, tl=tl, has_hist=has_hist),
        grid=(n_streams, nt),
        in_specs=[pl.BlockSpec((tl, 2 * RET_Q), row),
                  pl.BlockSpec((tl, RET_V), row),
                  pl.BlockSpec((tl, RET_V), row),
                  pl.BlockSpec((tl, SWA_Q), row),
                  pl.BlockSpec((tl, 2 * SWA_KV), row),
                  pl.BlockSpec((N_META, 2 * SWA_KV), const2),
                  pl.BlockSpec((1, WINDOW, 2 * SWA_KV), hist_map),
                  pl.BlockSpec((1, n_pairs, 2 * RET_DK, RET_DV), s0_map),
                  pl.BlockSpec((RET_HEADS, RET_DV), const2),
                  pl.BlockSpec((SWA_KV_HEADS, SUBLANES, 4 * CHUNK), const3)],
        out_specs=[pl.BlockSpec((tl, MIX_WIDTH), row),
                   pl.BlockSpec((1, n_pairs, 2 * RET_DK, RET_DV), lambda b, t: (b, 0, 0, 0)),
                   pl.BlockSpec((1, WINDOW, 2 * SWA_KV), per_b3)],
        out_shape=[jax.ShapeDtypeStruct((rows, MIX_WIDTH), BF16),
                   jax.ShapeDtypeStruct((n_streams, n_pairs, 2 * RET_DK, RET_DV), F32),
                   jax.ShapeDtypeStruct((n_streams, WINDOW, 2 * SWA_KV), F32)],
        scratch_shapes=[pltpu.VMEM((n_pairs, 2 * RET_DK, RET_DV), F32),
                        pltpu.VMEM((SWA_KV_HEADS, WINDOW + tl, LANES), BF16),
                        pltpu.VMEM((SWA_KV_HEADS, WINDOW + tl, LANES), BF16),
                        pltpu.VMEM((SWA_KV_HEADS, META_ROWS, LANES), BF16),
                        pltpu.VMEM((SWA_KV_HEADS, META_ROWS, LANES), BF16),
                        pltpu.VMEM((RET_HEADS, tl, tl), F32),
                        pltpu.VMEM((n_pairs, tl, LANES), F32),
                        pltpu.VMEM((RET_HEADS, tl, RET_DV), F32),
                        pltpu.VMEM((n_pairs, 2 * RET_DK, RET_DV), F32)],
        compiler_params=_params(2),
        name="attention",
    )(rqk, rv, gate, sq, skv, meta_kv, hist_kv, s0, rng, sink_tab)


def _post_kernel(omix_ref, x_ref, wout_ref, g2_ref, wrt_ref, base_ref,
                 xmid_ref, wcol_ref, rt_ref, cnt_ref, tri_scr, run_scr):
    i = pl.program_id(0)
    tm = x_ref.shape[0]

    @pl.when(i == 0)
    def _init():
        r = lax.broadcasted_iota(jnp.int32, (tm, tm), 0)
        c = lax.broadcasted_iota(jnp.int32, (tm, tm), 1)
        tri_scr[...] = jnp.where(r < c, 1.0, 0.0).astype(BF16)
        run_scr[...] = base_ref[...]

    xm = x_ref[...] + jnp.dot(omix_ref[...], wout_ref[...], preferred_element_type=F32)
    xmid_ref[...] = xm
    hn = _row_rms(xm) * g2_ref[...]
    lt = lax.dot_general(wrt_ref[...], hn.astype(BF16), (((1,), (1,)), ((), ())), preferred_element_type=F32)
    row8 = lax.broadcasted_iota(jnp.int32, (SUBLANES, tm), 0)
    big = jnp.int32(SUBLANES)
    gl = jnp.where(row8 < N_GROUPS, lt[0:SUBLANES], NEG_INF)
    gmax = jnp.max(gl, axis=0, keepdims=True)
    gsum = jnp.sum(jnp.exp(gl - gmax), axis=0, keepdims=True)
    g_sel = jnp.min(jnp.where(gl == gmax, row8, big), axis=0, keepdims=True)
    p_sel = 1.0 / gsum
    el = lt[ROUTER_EXPERT_ROW0:ROUTER_EXPERT_ROW0 + EXPERTS_PER_GROUP]
    for g in range(1, N_GROUPS):
        lo = ROUTER_EXPERT_ROW0 + g * EXPERTS_PER_GROUP
        el = jnp.where(g_sel == g, lt[lo:lo + EXPERTS_PER_GROUP], el)
    m1 = jnp.max(el, axis=0, keepdims=True)
    i1 = jnp.min(jnp.where(el == m1, row8, big), axis=0, keepdims=True)
    el2 = jnp.where(row8 == i1, NEG_INF, el)
    m2 = jnp.max(el2, axis=0, keepdims=True)
    i2 = jnp.min(jnp.where(el2 == m2, row8, big), axis=0, keepdims=True)
    e2 = jnp.exp(m2 - m1)
    inv = 1.0 / (1.0 + e2)
    w1 = p_sel * inv
    w2 = p_sel * (e2 * inv)
    eid1 = g_sel * EXPERTS_PER_GROUP + i1
    eid2 = g_sel * EXPERTS_PER_GROUP + i2

    rowe = lax.broadcasted_iota(jnp.int32, (N_EXPERTS, tm), 0)
    oh = jnp.where((rowe == eid1) | (rowe == eid2), 1.0, 0.0).astype(BF16)
    run = run_scr[...]
    pref = jnp.dot(oh, tri_scr[...], preferred_element_type=F32) + jnp.concatenate([run] * (tm // LANES), axis=1)
    r1 = jnp.sum(jnp.where(rowe == eid1, pref, 0.0), axis=0, keepdims=True)
    r2 = jnp.sum(jnp.where(rowe == eid2, pref, 0.0), axis=0, keepdims=True)
    run = run + jnp.dot(oh, jnp.ones((tm, LANES), BF16), preferred_element_type=F32)
    run_scr[...] = run
    cnt_ref[...] = run

    out = jnp.zeros((SUBLANES, tm), F32)
    for k, v in enumerate([eid1.astype(F32), eid2.astype(F32), w1, w2, r1, r2]):
        out = jnp.where(row8 == k, v, out)
    rt_ref[...] = out
    rowl = lax.broadcasted_iota(jnp.int32, (LANES, tm), 0)
    wcol_ref[...] = jnp.where(rowl == 0, w1, jnp.where(rowl == 1, w2, 0.0)).T


def _post(omix, x2d, w_out_bf, g2, w_router_t_bf, base_cnt, tm):
    t_rows = x2d.shape[0]
    assert tm % LANES == 0
    row = lambda i: (i, 0)
    const = lambda i: (0, 0)
    return pl.pallas_call(
        _post_kernel,
        grid=(t_rows // tm,),
        in_specs=[pl.BlockSpec((tm, MIX_WIDTH), row),
                  pl.BlockSpec((tm, D_MODEL), row),
                  pl.BlockSpec((MIX_WIDTH, D_MODEL), const),
                  pl.BlockSpec((1, D_MODEL), const),
                  pl.BlockSpec((ROUTER_ROWS, D_MODEL), const),
                  pl.BlockSpec((N_EXPERTS, LANES), const)],
        out_specs=[pl.BlockSpec((tm, D_MODEL), row),
                   pl.BlockSpec((tm, LANES), row),
                   pl.BlockSpec((SUBLANES, tm), lambda i: (0, i)),
                   pl.BlockSpec((N_EXPERTS, LANES), const)],
        out_shape=[jax.ShapeDtypeStruct((t_rows, D_MODEL), F32),
                   jax.ShapeDtypeStruct((t_rows, LANES), F32),
                   jax.ShapeDtypeStruct((SUBLANES, t_rows), F32),
                   jax.ShapeDtypeStruct((N_EXPERTS, LANES), F32)],
        scratch_shapes=[pltpu.VMEM((tm, tm), BF16), pltpu.VMEM((N_EXPERTS, LANES), F32)],
        compiler_params=_params(1),
        name="post",
    )(omix, x2d, w_out_bf, g2, w_router_t_bf, base_cnt)


def _step_major(pos, tm):
    return [pos[e].reshape(-1, 1, tm) for e in range(2)]


def _dispatch_kernel(pos0_ref, pos1_ref, *refs, tm, group_steps):
    pos_refs = (pos0_ref, pos1_ref)
    n_g = len(group_steps)
    xmid_refs = refs[:n_g]
    g2_ref, xs_ref, hbuf, sems = refs[n_g:]
    n_steps = sum(group_steps)
    i = pl.program_id(0)
    slot = i % 2

    first = 0
    for xmid_ref, steps in zip(xmid_refs, group_steps):
        @pl.when((i >= first) & (i < first + steps))
        def _normalise(xmid_ref=xmid_ref):
            hn = _row_rms(xmid_ref[...]) * g2_ref[...]
            hbuf[slot] = hn.reshape(tm // SUBLANES, SUBLANES, D_MODEL)
        first += steps

    def body(blk, carry):
        for k in range(SUBLANES):
            r = blk * SUBLANES + k
            for e in range(2):
                pltpu.make_async_copy(hbuf.at[slot, blk, pl.ds(k, 1)], xs_ref.at[pl.ds(pos_refs[e][0, 0, r], 1)],
                                      sems.at[slot]).start(priority=e)
        return carry

    lax.fori_loop(0, tm // SUBLANES, body, 0)

    def drain(which):
        pltpu.make_async_copy(xs_ref.at[pl.ds(0, 2 * tm)], xs_ref.at[pl.ds(0, 2 * tm)], sems.at[which]).wait()

    @pl.when(i > 0)
    def _previous():
        drain(1 - slot)

    @pl.when(i == n_steps - 1)
    def _last():
        drain(slot)


def _dispatch(pos_list, xmid_list, g2, tm):
    group_steps = tuple(x.shape[0] // tm for x in xmid_list)
    n_steps = sum(group_steps)
    assert tm % SUBLANES == 0
    pos3 = [jnp.concatenate(parts, axis=0) for parts in zip(*[_step_major(p, tm) for p in pos_list])]
    smem = pl.BlockSpec((1, 1, tm), lambda i: (i, 0, 0), memory_space=pltpu.SMEM)
    x_specs = []
    first = 0
    for steps in group_steps:
        x_specs.append(pl.BlockSpec((tm, D_MODEL),
                                    lambda i, first=first, steps=steps: (jnp.clip(i - first, 0, steps - 1), 0)))
        first += steps
    n_rows_out = 2 * sum(x.shape[0] for x in xmid_list)
    return pl.pallas_call(
        functools.partial(_dispatch_kernel, tm=tm, group_steps=group_steps),
        grid=(n_steps,),
        in_specs=[smem, smem] + x_specs + [pl.BlockSpec((1, D_MODEL), lambda i: (0, 0))],
        out_specs=pl.BlockSpec(memory_space=pl.ANY),
        out_shape=jax.ShapeDtypeStruct((n_rows_out, D_MODEL), F32),
        scratch_shapes=[pltpu.VMEM((2, tm // SUBLANES, SUBLANES, D_MODEL), F32), pltpu.SemaphoreType.DMA((2,))],
        compiler_params=_params(1),
        name="dispatch",
    )(*pos3, *xmid_list, g2)


def _expert_kernel(vt_ref, ve_ref, lo_ref, hi_ref, nv_ref, x_hbm, wg_ref, wu_ref, wd_ref, y_ref, wgu_scr, wd_scr,
                   xbuf, xsem, *, tm, n_tiles):
    v = pl.program_id(0)

    def fetch(t):
        start = t * tm if isinstance(t, int) else pl.multiple_of(t * tm, tm)
        slot = t % EXPERT_RING
        return pltpu.make_async_copy(x_hbm.at[pl.ds(start, tm)], xbuf.at[slot], xsem.at[slot])

    @pl.when(v == 0)
    def _prime():
        for t in range(min(EXPERT_RING - 1, n_tiles)):
            fetch(t).start()

    @pl.when(v < nv_ref[0])
    def _compute():
        t = vt_ref[v]

        @pl.when((v == 0) | (t != vt_ref[jnp.maximum(v - 1, 0)]))
        def _new_tile():
            fetch(t).wait()

            @pl.when(t + EXPERT_RING - 1 < n_tiles)
            def _ahead():
                fetch(t + EXPERT_RING - 1).start()

        @pl.when((v == 0) | (ve_ref[v] != ve_ref[jnp.maximum(v - 1, 0)]))
        def _new_expert():
            wgu_scr[:, 0:EXPERT_FF] = wg_ref[0].astype(BF16)
            wgu_scr[:, EXPERT_FF:2 * EXPERT_FF] = wu_ref[0].astype(BF16)
            wd_scr[...] = wd_ref[0].astype(BF16)

        x = xbuf[t % EXPERT_RING].astype(BF16)
        gu = jnp.dot(x, wgu_scr[...], preferred_element_type=F32)
        g = gu[:, 0:EXPERT_FF]
        a = (g * jax.nn.sigmoid(g) * gu[:, EXPERT_FF:2 * EXPERT_FF]).astype(BF16)
        y = jnp.dot(a, wd_scr[...], preferred_element_type=F32)
        lo, hi = lo_ref[v], hi_ref[v]

        @pl.when(lo == 0)
        def _first_visit():
            y_ref[...] = y

        @pl.when(lo > 0)
        def _later_visit():
            row = lax.broadcasted_iota(jnp.int32, y.shape, 0)
            y_ref[...] = jnp.where((row >= lo) & (row < hi), y, y_ref[...])


def _experts(vis_tile, vis_expert, vis_lo, vis_hi, n_vis, xs, w_gate, w_up, w_down, tm):
    n_steps = vis_tile.shape[0]
    last = lambda v, nv: jnp.minimum(v, nv[0] - 1)
    of_expert = lambda v, vt, ve, lo, hi, nv: (ve[last(v, nv)], 0, 0)
    assert xs.shape[0] % tm == 0
    return pl.pallas_call(
        functools.partial(_expert_kernel, tm=tm, n_tiles=xs.shape[0] // tm),
        grid_spec=pltpu.PrefetchScalarGridSpec(
            num_scalar_prefetch=5,
            grid=(n_steps,),
            in_specs=[pl.BlockSpec(memory_space=pl.ANY),
                      pl.BlockSpec((1, D_MODEL, EXPERT_FF), of_expert),
                      pl.BlockSpec((1, D_MODEL, EXPERT_FF), of_expert),
                      pl.BlockSpec((1, EXPERT_FF, D_MODEL), of_expert)],
            out_specs=pl.BlockSpec((tm, D_MODEL), lambda v, vt, ve, lo, hi, nv: (vt[last(v, nv)], 0)),
            scratch_shapes=[pltpu.VMEM((D_MODEL, 2 * EXPERT_FF), BF16), pltpu.VMEM((EXPERT_FF, D_MODEL), BF16),
                            pltpu.VMEM((EXPERT_RING, tm, D_MODEL), F32), pltpu.SemaphoreType.DMA((EXPERT_RING,))]),
        out_shape=jax.ShapeDtypeStruct(xs.shape, F32),
        compiler_params=_params(1),
        name="experts",
    )(vis_tile, vis_expert, vis_lo, vis_hi, n_vis, xs, w_gate, w_up, w_down)


def _combine_kernel(pos0_ref, pos1_ref, nxt0_ref, nxt1_ref, ys_ref, xmid_ref, wcol_ref, out_ref, ybuf, sems,
                    *, tm, n_steps):
    i = pl.program_id(0)
    slot = i % 2

    def issue(p_refs, to_slot):
        def body(blk, carry):
            for k in range(SUBLANES):
                r = blk * SUBLANES + k
                for e in range(2):
                    pltpu.make_async_copy(ys_ref.at[pl.ds(p_refs[e][0, 0, r], 1)],
                                          ybuf.at[to_slot, e, blk, pl.ds(k, 1)], sems.at[to_slot]).start(priority=e)
            return carry

        lax.fori_loop(0, tm // SUBLANES, body, 0)

    @pl.when(i == 0)
    def _first():
        issue((pos0_ref, pos1_ref), 0)

    @pl.when(i + 1 < n_steps)
    def _ahead():
        issue((nxt0_ref, nxt1_ref), 1 - slot)

    for e in range(2):
        pltpu.make_async_copy(ybuf.at[slot, e], ybuf.at[slot, e], sems.at[slot]).wait()
    w = wcol_ref[...]
    y0 = ybuf[slot, 0].reshape(tm, D_MODEL)
    y1 = ybuf[slot, 1].reshape(tm, D_MODEL)
    out_ref[...] = xmid_ref[...] + w[:, 0:1] * y0 + w[:, 1:2] * y1


def _combine(pos, ys, xmid, wcol, tm):
    t_rows = xmid.shape[0]
    n_steps = t_rows // tm
    assert tm % SUBLANES == 0
    pos3 = _step_major(pos, tm)
    row = lambda i: (i, 0)
    cur = pl.BlockSpec((1, 1, tm), lambda i: (i, 0, 0), memory_space=pltpu.SMEM)
    nxt = pl.BlockSpec((1, 1, tm), lambda i: (jnp.minimum(i + 1, n_steps - 1), 0, 0), memory_space=pltpu.SMEM)
    return pl.pallas_call(
        functools.partial(_combine_kernel, tm=tm, n_steps=n_steps),
        grid=(n_steps,),
        in_specs=[cur, cur, nxt, nxt,
                  pl.BlockSpec(memory_space=pl.ANY),
                  pl.BlockSpec((tm, D_MODEL), row),
                  pl.BlockSpec((tm, LANES), row)],
        out_specs=pl.BlockSpec((tm, D_MODEL), row),
        out_shape=jax.ShapeDtypeStruct((t_rows, D_MODEL), F32),
        scratch_shapes=[pltpu.VMEM((2, 2, tm // SUBLANES, SUBLANES, D_MODEL), F32), pltpu.SemaphoreType.DMA((2,))],
        compiler_params=_params(1),
        name="combine",
    )(*pos3, *pos3, ys, xmid, wcol)


def _bucket(ends, idx):
    n = jnp.sum((ends[None, :] <= idx[:, None]).astype(jnp.int32), axis=1)
    return jnp.minimum(n, ends.shape[0] - 1)


def _tile_for(rows, pref):
    tm = min(pref, rows)
    assert rows % tm == 0
    return tm


def kernel(x_prompt, x_sample, cache_ret_state, cache_swa_k, cache_swa_v, meta_tokens, norm1_g, w_in, q_norm_g,
           k_norm_g, ret_norm_g, attn_sinks, w_out, norm2_g, w_group, w_expert, w_gate, w_up, w_down):
    assert norm1_g.shape[0] == 1, "single-layer trunk"
    bp, lp, _ = x_prompt.shape
    bs, ls, _ = x_sample.shape
    n_pairs = RET_HEADS // 2

    g1 = norm1_g[0][None, :]
    g2 = norm2_g[0][None, :]
    w_in_bf = w_in[0].astype(BF16)
    w_out_bf = w_out[0].astype(BF16)
    qg2 = jnp.tile(q_norm_g[0], 2)[None, :]
    kg2 = jnp.tile(k_norm_g[0], 2)[None, :]
    rng = ret_norm_g[0].reshape(RET_HEADS, RET_DV)
    sink_tab = jnp.broadcast_to(jnp.repeat(attn_sinks[0], CHUNK).reshape(SWA_KV_HEADS, 1, 4 * CHUNK),
                                (SWA_KV_HEADS, SUBLANES, 4 * CHUNK))
    w_router_t = jnp.zeros((ROUTER_ROWS, D_MODEL), F32)
    w_router_t = w_router_t.at[0:N_GROUPS].set(w_group[0].T)
    w_router_t = w_router_t.at[ROUTER_EXPERT_ROW0:ROUTER_EXPERT_ROW0 + N_EXPERTS].set(w_expert[0].T)
    w_router_t_bf = w_router_t.astype(BF16)

    meta_rows = 2 * CHUNK
    m_pad = jnp.zeros((meta_rows, D_MODEL), F32).at[0:N_META].set(meta_tokens)
    m_rqk, m_rv, _, _, m_skv = _proj(m_pad, jnp.arange(meta_rows, dtype=jnp.int32), meta_rows, g1, w_in_bf, qg2, kg2)
    s_meta = _meta_state(m_rqk, m_rv)[None]
    meta_kv = m_skv[0:N_META]

    groups = [
        dict(x=x_prompt.reshape(bp * lp, D_MODEL), n=bp, seq=lp, pos0=N_META, has_hist=False, s0=s_meta,
             hist=jnp.zeros((1, WINDOW, 2 * SWA_KV), F32)),
        dict(x=x_sample.reshape(bs * ls, D_MODEL), n=bs, seq=ls, pos0=N_META + PAST_LEN, has_hist=True,
             s0=cache_ret_state[0].reshape(bs, n_pairs, 2 * RET_DK, RET_DV),
             hist=jnp.concatenate([cache_swa_k[0].reshape(bs, WINDOW, SWA_KV),
                                   cache_swa_v[0].reshape(bs, WINDOW, SWA_KV)], axis=-1)),
    ]

    base_cnt = jnp.zeros((N_EXPERTS, LANES), F32)
    for g in groups:
        rows = g["n"] * g["seq"]
        tm = _tile_for(rows, PROJ_TILE)
        pos = g["pos0"] + jnp.arange(g["seq"], dtype=jnp.int32)
        if g["seq"] < tm:
            assert tm % g["seq"] == 0
            pos = jnp.tile(pos, tm // g["seq"])
        else:
            assert g["seq"] % tm == 0
        rqk, rv, gate, sq, skv = _proj(g["x"], pos, tm, g1, w_in_bf, qg2, kg2)
        tl = min(ATTN_TILE, g["seq"])
        omix, s_out, kv_out = _attention(rqk, rv, gate, sq, skv, meta_kv, g["hist"], g["s0"], rng, sink_tab,
                                         n_streams=g["n"], seq=g["seq"], tl=tl, has_hist=g["has_hist"])
        xmid, wcol, route_t, base_cnt = _post(omix, g["x"], w_out_bf, g2, w_router_t_bf, base_cnt,
                                              _tile_for(rows, POST_TILE))
        g.update(xmid=xmid, wcol=wcol, route_t=route_t, s_out=s_out, kv_out=kv_out)

    te = EXPERT_TILE
    total_rows = sum(g["n"] * g["seq"] for g in groups)
    assert (2 * total_rows) % te == 0
    n_row_tiles = (2 * total_rows) // te
    counts = base_cnt[:, 0].astype(jnp.int32)
    off = jnp.cumsum(counts) - counts
    first_tile = off // te
    n_vis_e = jnp.where(counts > 0, (off + counts - 1) // te - first_tile + 1, 0)
    vis_end = jnp.cumsum(n_vis_e)
    n_vis = vis_end[-1:].astype(jnp.int32)
    v = jnp.arange(n_row_tiles + N_EXPERTS, dtype=jnp.int32)
    vis_expert = _bucket(vis_end, v)
    pick = lambda table: jnp.sum(jnp.where(vis_expert[:, None] == jnp.arange(N_EXPERTS, dtype=jnp.int32)[None, :],
                                           table[None, :], 0), axis=1)
    vis_tile = jnp.clip(pick(first_tile) + v - pick(vis_end - n_vis_e), 0, n_row_tiles - 1).astype(jnp.int32)
    vis_lo = jnp.clip(pick(off) - vis_tile * te, 0, te).astype(jnp.int32)
    vis_hi = jnp.clip(pick(off + counts) - vis_tile * te, 0, te).astype(jnp.int32)

    for g in groups:
        eid = g["route_t"][0:2].astype(jnp.int32)
        off_sel = jnp.sum(jnp.where(eid[None] == jnp.arange(N_EXPERTS, dtype=jnp.int32)[:, None, None],
                                    off[:, None, None], 0), axis=0)
        g["pos"] = (off_sel + g["route_t"][4:6].astype(jnp.int32)).astype(jnp.int32)
    xs = _dispatch([g["pos"] for g in groups], [g["xmid"] for g in groups], g2,
                   _tile_for(min(g["n"] * g["seq"] for g in groups), MOVE_TILE))

    ys = _experts(vis_tile, vis_expert.astype(jnp.int32), vis_lo, vis_hi, n_vis, xs, w_gate[0], w_up[0], w_down[0], te)

    outs = []
    for g in groups:
        rows = g["n"] * g["seq"]
        y = _combine(g["pos"], ys, g["xmid"], g["wcol"], _tile_for(rows, MOVE_TILE))
        outs.append(y.reshape(g["n"], g["seq"], D_MODEL))

    def caches(g):
        kv = g["kv_out"]
        k = kv[:, :, 0:SWA_KV].reshape(g["n"], WINDOW, SWA_KV_HEADS, SWA_HD)[None]
        v = kv[:, :, SWA_KV:2 * SWA_KV].reshape(g["n"], WINDOW, SWA_KV_HEADS, SWA_HD)[None]
        s = g["s_out"].reshape(g["n"], RET_HEADS, RET_DK, RET_DV)[None]
        return s, k, v

    sp, kp, vp = caches(groups[0])
    ss, ks, vs = caches(groups[1])
    return (outs[0], outs[1], sp, kp, vp, ss, ks, vs)
```

```python
import functools

import numpy as np
import jax
import jax.numpy as jnp
from jax import lax
from jax.experimental import pallas as pl
from jax.experimental.pallas import tpu as pltpu

F32 = jnp.float32
BF16 = jnp.bfloat16

D_MODEL = 1024
PAST_LEN = 4096
CHUNK = 64
N_META = 16
RET_HEADS = 4
RET_DK = 64
RET_DV = 128
SWA_HEADS = 8
SWA_KV_HEADS = 2
SWA_HD = 64
WINDOW = 128
ROPE_THETA = 10000.0
N_GROUPS = 4
EXPERTS_PER_GROUP = 8
N_EXPERTS = N_GROUPS * EXPERTS_PER_GROUP
EXPERT_FF = 256
EPS = 1e-6
NEG_INF = -1e30
LOG2E = float(np.log2(np.e))
RET_Q = RET_HEADS * RET_DK
RET_V = RET_HEADS * RET_DV
SWA_Q = SWA_HEADS * SWA_HD
SWA_KV = SWA_KV_HEADS * SWA_HD
MIX_WIDTH = RET_V + SWA_Q
IN_WIDTH = 2 * RET_Q + 2 * RET_V + SWA_Q + 2 * SWA_KV

LANES = 128
PROJ_TILE = 1024
POST_TILE = 1024
ATTN_TILE = 256
EXPERT_TILE = 1024
POST_RING = 3
EXPERT_RING = 3
MOVE_TILE = 512
SUBLANES = 8
ROUTER_EXPERT_ROW0 = 8
ROUTER_ROWS = 64
META_ROWS = 32
VMEM_LIMIT = 56 * 1024 * 1024

_LOG_G = [float(np.log1p(-np.exp2(-5.0 - h))) for h in range(RET_HEADS)]


def _params(n_axes):
    return pltpu.CompilerParams(dimension_semantics=("arbitrary",) * n_axes, vmem_limit_bytes=VMEM_LIMIT)


def _split_bf16(a):
    hi = a.astype(BF16)
    return hi, (a - hi.astype(F32)).astype(BF16)


def _split_dot(a, w2):
    hi, lo = _split_bf16(a)
    return jnp.dot(jnp.concatenate([hi, lo], axis=1), w2, preferred_element_type=F32)


def _lane_sum(a):
    return _split_dot(a, jnp.ones((2 * LANES, LANES), BF16))


def _head_sum_matrix():
    i = lax.broadcasted_iota(jnp.int32, (2 * LANES, LANES), 0) % LANES
    j = lax.broadcasted_iota(jnp.int32, (2 * LANES, LANES), 1)
    return jnp.where((i < SWA_HD) == (j < SWA_HD), 1.0, 0.0).astype(BF16)


def _rope(t, c, s1, s2):
    half = SWA_HD // 2
    return t * c + pltpu.roll(t, LANES - half, 1) * s1 + pltpu.roll(t, half, 1) * s2


def _head_rms(t, g, head_w):
    ms = _split_dot(t * t, head_w) * (1.0 / SWA_HD)
    return t * lax.rsqrt(ms + EPS) * g


def _head_rms_pair(ta, tb, g, pair_w):
    ss = jnp.concatenate([(ta * ta).astype(BF16), (tb * tb).astype(BF16)], axis=1)
    ms = jnp.dot(ss, pair_w, preferred_element_type=F32) * (1.0 / SWA_HD)
    return ta * lax.rsqrt(ms[:, 0:LANES] + EPS) * g, tb * lax.rsqrt(ms[:, LANES:2 * LANES] + EPS) * g


def _row_rms(x):
    n_tiles = x.shape[1] // LANES
    ss = x[:, 0:LANES] * x[:, 0:LANES]
    for j in range(1, n_tiles):
        ss = ss + x[:, j * LANES:(j + 1) * LANES] * x[:, j * LANES:(j + 1) * LANES]
    r = lax.rsqrt(_lane_sum(ss) * (1.0 / x.shape[1]) + EPS)
    return x * jnp.concatenate([r] * n_tiles, axis=1)


def _proj_kernel(x_ref, g1_ref, w_ref, qg_ref, kg_ref, cos_ref, s1_ref, s2_ref,
                 rqk_ref, rv_ref, gate_ref, sq_ref, skv_ref):
    xn = (_row_rms(x_ref[...]) * g1_ref[...]).astype(BF16)
    c, s1, s2 = cos_ref[...], s1_ref[...], s2_ref[...]
    head_w = _head_sum_matrix()

    def seg(a, b):
        return jnp.dot(xn, w_ref[:, a:b], preferred_element_type=F32)

    def tile(h, j):
        return h[:, j * LANES:(j + 1) * LANES]

    h = seg(0, 2 * RET_Q)
    for j in range(2):
        rqk_ref[:, j * LANES:(j + 1) * LANES] = _rope(tile(h, j), c, s1, s2).astype(BF16)
    for j in range(2, 4):
        rqk_ref[:, j * LANES:(j + 1) * LANES] = (_rope(tile(h, j), c, s1, s2) * (RET_DK ** -0.5)).astype(BF16)
    a = 2 * RET_Q
    rv_ref[...] = seg(a, a + RET_V).astype(BF16)
    a += RET_V
    g = seg(a, a + RET_V)
    gate_ref[...] = (g * jax.nn.sigmoid(g)).astype(BF16)
    a += RET_V
    h = seg(a, a + SWA_Q)
    qg = qg_ref[...]
    pair_w = jnp.concatenate([jnp.concatenate([head_w[0:LANES], jnp.zeros((LANES, LANES), BF16)], axis=1),
                              jnp.concatenate([jnp.zeros((LANES, LANES), BF16), head_w[0:LANES]], axis=1)], axis=0)
    for j in range(0, SWA_Q // LANES, 2):
        qa, qb = _head_rms_pair(tile(h, j), tile(h, j + 1), qg, pair_w)
        sq_ref[:, j * LANES:(j + 1) * LANES] = _rope(qa, c, s1, s2).astype(BF16)
        sq_ref[:, (j + 1) * LANES:(j + 2) * LANES] = _rope(qb, c, s1, s2).astype(BF16)
    a += SWA_Q
    h = seg(a, a + 2 * SWA_KV)
    skv_ref[:, 0:LANES] = _rope(_head_rms(tile(h, 0), kg_ref[...], head_w), c, s1, s2)
    skv_ref[:, LANES:2 * LANES] = tile(h, 1)


def _rope_tables(pos):
    half = SWA_HD // 2
    inv = ROPE_THETA ** (-jnp.arange(half, dtype=F32) / half)
    ang = pos.astype(F32)[:, None] * inv[None, :]
    cos, sin = jnp.cos(ang), jnp.sin(ang)
    z = jnp.zeros_like(sin)
    heads_per_tile = LANES // SWA_HD
    return (jnp.tile(cos, (1, 2 * heads_per_tile)),
            jnp.tile(jnp.concatenate([-sin, z], axis=1), (1, heads_per_tile)),
            jnp.tile(jnp.concatenate([z, sin], axis=1), (1, heads_per_tile)))


def _proj(x2d, pos_rows, tm, g1, w_in_bf, qg2, kg2):
    t_rows = x2d.shape[0]
    n_tiles = t_rows // tm
    n_pos_tiles = pos_rows.shape[0] // tm
    cos, s1, s2 = _rope_tables(pos_rows)
    row = lambda i: (i, 0)
    const = lambda i: (0, 0)
    tab = lambda i: (i % n_pos_tiles, 0)
    return pl.pallas_call(
        _proj_kernel,
        grid=(n_tiles,),
        in_specs=[pl.BlockSpec((tm, D_MODEL), row),
                  pl.BlockSpec((1, D_MODEL), const),
                  pl.BlockSpec((D_MODEL, IN_WIDTH), const),
                  pl.BlockSpec((1, LANES), const),
                  pl.BlockSpec((1, LANES), const),
                  pl.BlockSpec((tm, LANES), tab),
                  pl.BlockSpec((tm, LANES), tab),
                  pl.BlockSpec((tm, LANES), tab)],
        out_specs=[pl.BlockSpec((tm, 2 * RET_Q), row),
                   pl.BlockSpec((tm, RET_V), row),
                   pl.BlockSpec((tm, RET_V), row),
                   pl.BlockSpec((tm, SWA_Q), row),
                   pl.BlockSpec((tm, 2 * SWA_KV), row)],
        out_shape=[jax.ShapeDtypeStruct((t_rows, 2 * RET_Q), BF16),
                   jax.ShapeDtypeStruct((t_rows, RET_V), BF16),
                   jax.ShapeDtypeStruct((t_rows, RET_V), BF16),
                   jax.ShapeDtypeStruct((t_rows, SWA_Q), BF16),
                   jax.ShapeDtypeStruct((t_rows, 2 * SWA_KV), F32)],
        compiler_params=_params(1),
        name="proj",
    )(x2d, g1, w_in_bf, qg2, kg2, cos, s1, s2)


def _pair_update(k_bf, v0_bf, v1_bf, wt):
    kw = (k_bf.astype(F32) * wt).astype(BF16)
    dn = (((0,), (0,)), ((), ()))
    a0 = lax.dot_general(kw, v0_bf, dn, preferred_element_type=F32)
    a1 = lax.dot_general(kw, v1_bf, dn, preferred_element_type=F32)
    top = lax.broadcasted_iota(jnp.int32, a0.shape, 0) < RET_DK
    return jnp.where(top, a0, a1)


def _decay_rows(n, pair, rows_back_from):
    i = lax.broadcasted_iota(jnp.int32, (n, LANES), 0).astype(F32)
    lane = lax.broadcasted_iota(jnp.int32, (n, LANES), 1)
    lg = jnp.where(lane < RET_DK, _LOG_G[2 * pair], _LOG_G[2 * pair + 1])
    return jnp.exp((rows_back_from - i) * lg)


def _meta_state_kernel(rqk_ref, rv_ref, s_ref, *, n_rows):
    for p in range(RET_HEADS // 2):
        k = rqk_ref[:, RET_Q + p * LANES:RET_Q + (p + 1) * LANES]
        wt = _decay_rows(n_rows, p, float(N_META - 1))
        s_ref[p] = _pair_update(k, rv_ref[:, (2 * p) * LANES:(2 * p + 1) * LANES],
                                rv_ref[:, (2 * p + 1) * LANES:(2 * p + 2) * LANES], wt)


def _meta_state(m_rqk, m_rv):
    n_rows = m_rqk.shape[0]
    return pl.pallas_call(
        functools.partial(_meta_state_kernel, n_rows=n_rows),
        out_shape=jax.ShapeDtypeStruct((RET_HEADS // 2, 2 * RET_DK, RET_DV), F32),
        name="meta_state",
    )(m_rqk, m_rv)


def _dup_halves(a, lo_mask):
    sw = pltpu.roll(a, SWA_HD, 1)
    return jnp.where(lo_mask, a, sw), jnp.where(lo_mask, sw, a)


def _attn_kernel(rqk_ref, rv_ref, gate_ref, sq_ref, skv_ref, meta_ref, hist_ref, s0_ref, rng_ref, sink_ref,
                 omix_ref, sout_ref, kvout_ref,
                 s_scr, kd_scr, vd_scr, mk_scr, mv_scr, dec_scr, wt_scr, cs_scr, gam_scr,
                 *, tl, has_hist):
    b = pl.program_id(0)
    t = pl.program_id(1)
    nt = pl.num_programs(1)
    n_chunks = tl // CHUNK
    n_pairs = RET_HEADS // 2
    lo_tl = lax.broadcasted_iota(jnp.int32, (tl, LANES), 1) < SWA_HD
    lo_c = lax.broadcasted_iota(jnp.int32, (CHUNK, LANES), 1) < SWA_HD

    @pl.when((b == 0) & (t == 0))
    def _tables():
        i = lax.broadcasted_iota(jnp.int32, (tl, tl), 0)
        j = lax.broadcasted_iota(jnp.int32, (tl, tl), 1)
        diff = (i - j).astype(F32)
        row = lax.broadcasted_iota(jnp.int32, (tl, LANES), 0).astype(F32)
        for h in range(RET_HEADS):
            dec_scr[h] = jnp.where(diff >= 0.0, jnp.exp(jnp.maximum(diff, 0.0) * _LOG_G[h]), 0.0)
            cs_scr[h] = jnp.exp((row + 1.0) * _LOG_G[h])
        top = lax.broadcasted_iota(jnp.int32, (2 * RET_DK, RET_DV), 0) < RET_DK
        for p in range(n_pairs):
            wt_scr[p] = _decay_rows(tl, p, float(tl - 1))
            gam_scr[p] = jnp.where(top, jnp.exp(jnp.float32(tl * _LOG_G[2 * p])), jnp.exp(jnp.float32(tl * _LOG_G[2 * p + 1])))
        lo_m = lax.broadcasted_iota(jnp.int32, (N_META, LANES), 1) < SWA_HD
        mk0, mk1 = _dup_halves(meta_ref[:, 0:LANES], lo_m)
        mv0, mv1 = _dup_halves(meta_ref[:, LANES:2 * LANES], lo_m)
        mk_scr[...] = jnp.zeros(mk_scr.shape, BF16)
        mv_scr[...] = jnp.zeros(mv_scr.shape, BF16)
        mk_scr[0, 0:N_META] = mk0.astype(BF16)
        mk_scr[1, 0:N_META] = mk1.astype(BF16)
        mv_scr[0, 0:N_META] = mv0.astype(BF16)
        mv_scr[1, 0:N_META] = mv1.astype(BF16)

    @pl.when(t == 0)
    def _stream_start():
        s_scr[...] = s0_ref[0]
        if has_hist:
            lo_w = lax.broadcasted_iota(jnp.int32, (WINDOW, LANES), 1) < SWA_HD
            k0, k1 = _dup_halves(hist_ref[0, :, 0:LANES], lo_w)
            v0, v1 = _dup_halves(hist_ref[0, :, LANES:2 * LANES], lo_w)
            kd_scr[0, 0:WINDOW] = k0.astype(BF16)
            kd_scr[1, 0:WINDOW] = k1.astype(BF16)
            vd_scr[0, 0:WINDOW] = v0.astype(BF16)
            vd_scr[1, 0:WINDOW] = v1.astype(BF16)
        else:
            z = jnp.zeros((WINDOW, LANES), BF16)
            for kv in range(SWA_KV_HEADS):
                kd_scr[kv, 0:WINDOW] = z
                vd_scr[kv, 0:WINDOW] = z

    k0, k1 = _dup_halves(skv_ref[:, 0:LANES], lo_tl)
    v0, v1 = _dup_halves(skv_ref[:, LANES:2 * LANES], lo_tl)
    kd_scr[0, WINDOW:WINDOW + tl] = k0.astype(BF16)
    kd_scr[1, WINDOW:WINDOW + tl] = k1.astype(BF16)
    vd_scr[0, WINDOW:WINDOW + tl] = v0.astype(BF16)
    vd_scr[1, WINDOW:WINDOW + tl] = v1.astype(BF16)

    band = WINDOW + CHUNK
    n_keys = META_ROWS + band
    n_q = 4 * CHUNK
    scale2 = (SWA_HD ** -0.5) * LOG2E
    krow = lax.broadcasted_iota(jnp.int32, (n_keys, n_q), 0)
    zero_c = jnp.zeros((CHUNK, LANES), BF16)
    ones_v = jnp.ones((n_keys, LANES), BF16)
    for c in range(n_chunks):
        if has_hist:
            first_valid = META_ROWS
        else:
            first_valid = jnp.where(t == 0, max(META_ROWS + WINDOW - c * CHUNK, META_ROWS), META_ROWS)
        valid_t = (krow < N_META) | (krow >= first_valid)
        r0 = c * CHUNK
        for kv in range(SWA_KV_HEADS):
            keys = jnp.concatenate([mk_scr[kv], kd_scr[kv, r0:r0 + band]], axis=0)
            vals = jnp.concatenate([mv_scr[kv], vd_scr[kv, r0:r0 + band]], axis=0)
            qa = sq_ref[r0:r0 + CHUNK, (2 * kv) * LANES:(2 * kv + 1) * LANES]
            qb = sq_ref[r0:r0 + CHUNK, (2 * kv + 1) * LANES:(2 * kv + 2) * LANES]
            lhs = jnp.concatenate([jnp.where(lo_c, qa, zero_c), jnp.where(lo_c, zero_c, qa),
                                   jnp.where(lo_c, qb, zero_c), jnp.where(lo_c, zero_c, qb)], axis=0)
            s_t = lax.dot_general(keys, lhs, (((1,), (1,)), ((), ())), preferred_element_type=F32) * scale2
            s_t = jnp.where(valid_t, s_t, NEG_INF)
            s_t = jnp.where(krow == N_META, sink_ref[kv, 0:1, :] * LOG2E, s_t)
            e_t = jnp.exp2(s_t - jnp.max(s_t, axis=0, keepdims=True)).astype(BF16)
            ov = lax.dot_general(e_t, jnp.concatenate([vals, ones_v], axis=1), (((0,), (0,)), ((), ())),
                                 preferred_element_type=F32)
            o = ov[:, 0:LANES] * (1.0 / ov[:, LANES:2 * LANES])
            oa = jnp.where(lo_c, o[0:CHUNK], o[CHUNK:2 * CHUNK])
            ob = jnp.where(lo_c, o[2 * CHUNK:3 * CHUNK], o[3 * CHUNK:4 * CHUNK])
            base = RET_V + (2 * kv) * LANES
            omix_ref[r0:r0 + CHUNK, base:base + LANES] = oa.astype(BF16)
            omix_ref[r0:r0 + CHUNK, base + LANES:base + 2 * LANES] = ob.astype(BF16)

    zero_t = jnp.zeros((tl, LANES), BF16)
    rr = lax.broadcasted_iota(jnp.int32, (2 * LANES, 2 * LANES), 0) < LANES
    cc = lax.broadcasted_iota(jnp.int32, (2 * LANES, 2 * LANES), 1) < LANES
    ones_pair = jnp.where(rr == cc, 1.0, 0.0).astype(BF16)
    for p in range(n_pairs):
        q = rqk_ref[:, p * LANES:(p + 1) * LANES]
        k = rqk_ref[:, RET_Q + p * LANES:RET_Q + (p + 1) * LANES]
        lhs = jnp.concatenate([jnp.where(lo_tl, q, zero_t), jnp.where(lo_tl, zero_t, q)], axis=0)
        s = lax.dot_general(lhs, k, (((1,), (1,)), ((), ())), preferred_element_type=F32)
        cross = jnp.dot(lhs, s_scr[p].astype(BF16), preferred_element_type=F32)
        outs = []
        for i in range(2):
            h = 2 * p + i
            v = rv_ref[:, h * LANES:(h + 1) * LANES]
            a = (s[i * tl:(i + 1) * tl] * dec_scr[h]).astype(BF16)
            outs.append(jnp.dot(a, v, preferred_element_type=F32) + cross[i * tl:(i + 1) * tl] * cs_scr[h])
        ss = jnp.concatenate([(o * o).astype(BF16) for o in outs], axis=1)
        ms = jnp.dot(ss, ones_pair, preferred_element_type=F32) * (1.0 / RET_DV)
        for i, o in enumerate(outs):
            h = 2 * p + i
            r = o * lax.rsqrt(ms[:, i * LANES:(i + 1) * LANES] + EPS) * rng_ref[h:h + 1, :]
            omix_ref[:, h * LANES:(h + 1) * LANES] = (r * gate_ref[:, h * LANES:(h + 1) * LANES].astype(F32)).astype(BF16)
        u = _pair_update(k, rv_ref[:, (2 * p) * LANES:(2 * p + 1) * LANES],
                         rv_ref[:, (2 * p + 1) * LANES:(2 * p + 2) * LANES], wt_scr[p])
        s_scr[p] = gam_scr[p] * s_scr[p] + u

    if tl >= WINDOW:
        @pl.when(t + 1 < nt)
        def _carry_window():
            for kv in range(SWA_KV_HEADS):
                kd_scr[kv, 0:WINDOW] = kd_scr[kv, tl:tl + WINDOW]
                vd_scr[kv, 0:WINDOW] = vd_scr[kv, tl:tl + WINDOW]

    @pl.when(t + 1 == nt)
    def _stream_end():
        sout_ref[0] = s_scr[...]
        if tl >= WINDOW:
            kvout_ref[0] = skv_ref[tl - WINDOW:tl, :]
        else:
            kvout_ref[0, 0:WINDOW - tl] = hist_ref[0, tl:WINDOW, :]
            kvout_ref[0, WINDOW - tl:WINDOW] = skv_ref[...]


def _attention(rqk, rv, gate, sq, skv, meta_kv, hist_kv, s0, rng, sink_tab, *, n_streams, seq, tl, has_hist):
    nt = seq // tl
    assert tl % CHUNK == 0 and seq % tl == 0
    assert tl >= WINDOW or (nt == 1 and has_hist)
    n_pairs = RET_HEADS // 2
    s0_shared = s0.shape[0] == 1
    row = lambda b, t: (b * nt + t, 0)
    const2 = lambda b, t: (0, 0)
    const3 = lambda b, t: (0, 0, 0)
    per_b3 = lambda b, t: (b, 0, 0)
    s0_map = (lambda b, t: (0, 0, 0, 0)) if s0_shared else (lambda b, t: (b, 0, 0, 0))
    hist_map = per_b3 if has_hist else const3
    rows = n_streams * seq
    return pl.pallas_call(
        functools.partial(_attn_kernel, tl=tl, has_hist=has_hist),
        grid=(n_streams, nt),
        in_specs=[pl.BlockSpec((tl, 2 * RET_Q), row),
                  pl.BlockSpec((tl, RET_V), row),
                  pl.BlockSpec((tl, RET_V), row),
                  pl.BlockSpec((tl, SWA_Q), row),
                  pl.BlockSpec((tl, 2 * SWA_KV), row),
                  pl.BlockSpec((N_META, 2 * SWA_KV), const2),
                  pl.BlockSpec((1, WINDOW, 2 * SWA_KV), hist_map),
                  pl.BlockSpec((1, n_pairs, 2 * RET_DK, RET_DV), s0_map),
                  pl.BlockSpec((RET_HEADS, RET_DV), const2),
                  pl.BlockSpec((SWA_KV_HEADS, SUBLANES, 4 * CHUNK), const3)],
        out_specs=[pl.BlockSpec((tl, MIX_WIDTH), row),
                   pl.BlockSpec((1, n_pairs, 2 * RET_DK, RET_DV), lambda b, t: (b, 0, 0, 0)),
                   pl.BlockSpec((1, WINDOW, 2 * SWA_KV), per_b3)],
        out_shape=[jax.ShapeDtypeStruct((rows, MIX_WIDTH), BF16),
                   jax.ShapeDtypeStruct((n_streams, n_pairs, 2 * RET_DK, RET_DV), F32),
                   jax.ShapeDtypeStruct((n_streams, WINDOW, 2 * SWA_KV), F32)],
        scratch_shapes=[pltpu.VMEM((n_pairs, 2 * RET_DK, RET_DV), F32),
                        pltpu.VMEM((SWA_KV_HEADS, WINDOW + tl, LANES), BF16),
                        pltpu.VMEM((SWA_KV_HEADS, WINDOW + tl, LANES), BF16),
                        pltpu.VMEM((SWA_KV_HEADS, META_ROWS, LANES), BF16),
                        pltpu.VMEM((SWA_KV_HEADS, META_ROWS, LANES), BF16),
                        pltpu.VMEM((RET_HEADS, tl, tl), F32),
                        pltpu.VMEM((n_pairs, tl, LANES), F32),
                        pltpu.VMEM((RET_HEADS, tl, RET_DV), F32),
                        pltpu.VMEM((n_pairs, 2 * RET_DK, RET_DV), F32)],
        compiler_params=_params(2),
        name="attention",
    )(rqk, rv, gate, sq, skv, meta_kv, hist_kv, s0, rng, sink_tab)


def _post_kernel(omix_ref, x_hbm, wout_ref, g2_ref, wrt_ref, base_ref,
                 xmid_ref, wcol_ref, rt_ref, cnt_ref, tri_scr, run_scr, xbuf, xsem, *, tm, n_tiles):
    i = pl.program_id(0)

    def fetch(t):
        start = t * tm if isinstance(t, int) else pl.multiple_of(t * tm, tm)
        slot = t % POST_RING
        return pltpu.make_async_copy(x_hbm.at[pl.ds(start, tm)], xbuf.at[slot], xsem.at[slot])

    @pl.when(i == 0)
    def _init():
        for t in range(min(POST_RING - 1, n_tiles)):
            fetch(t).start()
        r = lax.broadcasted_iota(jnp.int32, (tm, tm), 0)
        c = lax.broadcasted_iota(jnp.int32, (tm, tm), 1)
        tri_scr[...] = jnp.where(r < c, 1.0, 0.0).astype(BF16)
        run_scr[...] = base_ref[...]

    fetch(i).wait()

    @pl.when(i + POST_RING - 1 < n_tiles)
    def _ahead():
        fetch(i + POST_RING - 1).start()

    xm = xbuf[i % POST_RING] + jnp.dot(omix_ref[...], wout_ref[...], preferred_element_type=F32)
    xmid_ref[...] = xm
    hn = _row_rms(xm) * g2_ref[...]
    lt = lax.dot_general(wrt_ref[...], hn.astype(BF16), (((1,), (1,)), ((), ())), preferred_element_type=F32)
    row8 = lax.broadcasted_iota(jnp.int32, (SUBLANES, tm), 0)
    big = jnp.int32(SUBLANES)
    gl = jnp.where(row8 < N_GROUPS, lt[0:SUBLANES], NEG_INF)
    gmax = jnp.max(gl, axis=0, keepdims=True)
    gsum = jnp.sum(jnp.exp(gl - gmax), axis=0, keepdims=True)
    g_sel = jnp.min(jnp.where(gl == gmax, row8, big), axis=0, keepdims=True)
    p_sel = 1.0 / gsum
    el = lt[ROUTER_EXPERT_ROW0:ROUTER_EXPERT_ROW0 + EXPERTS_PER_GROUP]
    for g in range(1, N_GROUPS):
        lo = ROUTER_EXPERT_ROW0 + g * EXPERTS_PER_GROUP
        el = jnp.where(g_sel == g, lt[lo:lo + EXPERTS_PER_GROUP], el)
    m1 = jnp.max(el, axis=0, keepdims=True)
    i1 = jnp.min(jnp.where(el == m1, row8, big), axis=0, keepdims=True)
    el2 = jnp.where(row8 == i1, NEG_INF, el)
    m2 = jnp.max(el2, axis=0, keepdims=True)
    i2 = jnp.min(jnp.where(el2 == m2, row8, big), axis=0, keepdims=True)
    e2 = jnp.exp(m2 - m1)
    inv = 1.0 / (1.0 + e2)
    w1 = p_sel * inv
    w2 = p_sel * (e2 * inv)
    eid1 = g_sel * EXPERTS_PER_GROUP + i1
    eid2 = g_sel * EXPERTS_PER_GROUP + i2

    rowe = lax.broadcasted_iota(jnp.int32, (N_EXPERTS, tm), 0)
    oh = jnp.where((rowe == eid1) | (rowe == eid2), 1.0, 0.0).astype(BF16)
    run = run_scr[...]
    pref = jnp.dot(oh, tri_scr[...], preferred_element_type=F32) + jnp.concatenate([run] * (tm // LANES), axis=1)
    r1 = jnp.sum(jnp.where(rowe == eid1, pref, 0.0), axis=0, keepdims=True)
    r2 = jnp.sum(jnp.where(rowe == eid2, pref, 0.0), axis=0, keepdims=True)
    run = run + jnp.dot(oh, jnp.ones((tm, LANES), BF16), preferred_element_type=F32)
    run_scr[...] = run
    cnt_ref[...] = run

    out = jnp.zeros((SUBLANES, tm), F32)
    for k, v in enumerate([eid1.astype(F32), eid2.astype(F32), w1, w2, r1, r2]):
        out = jnp.where(row8 == k, v, out)
    rt_ref[...] = out
    rowl = lax.broadcasted_iota(jnp.int32, (LANES, tm), 0)
    wcol_ref[...] = jnp.where(rowl == 0, w1, jnp.where(rowl == 1, w2, 0.0)).T


def _post(omix, x2d, w_out_bf, g2, w_router_t_bf, base_cnt, tm):
    t_rows = x2d.shape[0]
    assert tm % LANES == 0
    row = lambda i: (i, 0)
    const = lambda i: (0, 0)
    assert t_rows % tm == 0
    return pl.pallas_call(
        functools.partial(_post_kernel, tm=tm, n_tiles=t_rows // tm),
        grid=(t_rows // tm,),
        in_specs=[pl.BlockSpec((tm, MIX_WIDTH), row),
                  pl.BlockSpec(memory_space=pl.ANY),
                  pl.BlockSpec((MIX_WIDTH, D_MODEL), const),
                  pl.BlockSpec((1, D_MODEL), const),
                  pl.BlockSpec((ROUTER_ROWS, D_MODEL), const),
                  pl.BlockSpec((N_EXPERTS, LANES), const)],
        out_specs=[pl.BlockSpec((tm, D_MODEL), row),
                   pl.BlockSpec((tm, LANES), row),
                   pl.BlockSpec((SUBLANES, tm), lambda i: (0, i)),
                   pl.BlockSpec((N_EXPERTS, LANES), const)],
        out_shape=[jax.ShapeDtypeStruct((t_rows, D_MODEL), F32),
                   jax.ShapeDtypeStruct((t_rows, LANES), F32),
                   jax.ShapeDtypeStruct((SUBLANES, t_rows), F32),
                   jax.ShapeDtypeStruct((N_EXPERTS, LANES), F32)],
        scratch_shapes=[pltpu.VMEM((tm, tm), BF16), pltpu.VMEM((N_EXPERTS, LANES), F32),
                        pltpu.VMEM((POST_RING, tm, D_MODEL), F32), pltpu.SemaphoreType.DMA((POST_RING,))],
        compiler_params=_params(1),
        name="post",
    )(omix, x2d, w_out_bf, g2, w_router_t_bf, base_cnt)


def _step_major(pos, tm):
    return [pos[e].reshape(-1, 1, tm) for e in range(2)]


def _dispatch_kernel(pos0_ref, pos1_ref, *refs, tm, group_steps):
    pos_refs = (pos0_ref, pos1_ref)
    n_g = len(group_steps)
    xmid_refs = refs[:n_g]
    g2_ref, xs_ref, hbuf, sems = refs[n_g:]
    n_steps = sum(group_steps)
    i = pl.program_id(0)
    slot = i % 2

    first = 0
    for xmid_ref, steps in zip(xmid_refs, group_steps):
        @pl.when((i >= first) & (i < first + steps))
        def _normalise(xmid_ref=xmid_ref):
            hn = _row_rms(xmid_ref[...]) * g2_ref[...]
            hbuf[slot] = hn.reshape(tm // SUBLANES, SUBLANES, D_MODEL)
        first += steps

    def body(blk, carry):
        for k in range(SUBLANES):
            r = blk * SUBLANES + k
            for e in range(2):
                pltpu.make_async_copy(hbuf.at[slot, blk, pl.ds(k, 1)], xs_ref.at[pl.ds(pos_refs[e][0, 0, r], 1)],
                                      sems.at[slot]).start(priority=e)
        return carry

    lax.fori_loop(0, tm // SUBLANES, body, 0)

    def drain(which):
        pltpu.make_async_copy(xs_ref.at[pl.ds(0, 2 * tm)], xs_ref.at[pl.ds(0, 2 * tm)], sems.at[which]).wait()

    @pl.when(i > 0)
    def _previous():
        drain(1 - slot)

    @pl.when(i == n_steps - 1)
    def _last():
        drain(slot)


def _dispatch(pos_list, xmid_list, g2, tm):
    group_steps = tuple(x.shape[0] // tm for x in xmid_list)
    n_steps = sum(group_steps)
    assert tm % SUBLANES == 0
    pos3 = [jnp.concatenate(parts, axis=0) for parts in zip(*[_step_major(p, tm) for p in pos_list])]
    smem = pl.BlockSpec((1, 1, tm), lambda i: (i, 0, 0), memory_space=pltpu.SMEM)
    x_specs = []
    first = 0
    for steps in group_steps:
        x_specs.append(pl.BlockSpec((tm, D_MODEL),
                                    lambda i, first=first, steps=steps: (jnp.clip(i - first, 0, steps - 1), 0)))
        first += steps
    n_rows_out = 2 * sum(x.shape[0] for x in xmid_list)
    return pl.pallas_call(
        functools.partial(_dispatch_kernel, tm=tm, group_steps=group_steps),
        grid=(n_steps,),
        in_specs=[smem, smem] + x_specs + [pl.BlockSpec((1, D_MODEL), lambda i: (0, 0))],
        out_specs=pl.BlockSpec(memory_space=pl.ANY),
        out_shape=jax.ShapeDtypeStruct((n_rows_out, D_MODEL), F32),
        scratch_shapes=[pltpu.VMEM((2, tm // SUBLANES, SUBLANES, D_MODEL), F32), pltpu.SemaphoreType.DMA((2,))],
        compiler_params=_params(1),
        name="dispatch",
    )(*pos3, *xmid_list, g2)


def _expert_kernel(vt_ref, ve_ref, lo_ref, hi_ref, nv_ref, x_hbm, wg_ref, wu_ref, wd_ref, y_ref, wgu_scr, wd_scr,
                   xbuf, xsem, *, tm, n_tiles):
    v = pl.program_id(0)

    def fetch(t):
        start = t * tm if isinstance(t, int) else pl.multiple_of(t * tm, tm)
        slot = t % EXPERT_RING
        return pltpu.make_async_copy(x_hbm.at[pl.ds(start, tm)], xbuf.at[slot], xsem.at[slot])

    @pl.when(v == 0)
    def _prime():
        for t in range(min(EXPERT_RING - 1, n_tiles)):
            fetch(t).start()

    @pl.when(v < nv_ref[0])
    def _compute():
        t = vt_ref[v]

        @pl.when((v == 0) | (t != vt_ref[jnp.maximum(v - 1, 0)]))
        def _new_tile():
            fetch(t).wait()

            @pl.when(t + EXPERT_RING - 1 < n_tiles)
            def _ahead():
                fetch(t + EXPERT_RING - 1).start()

        @pl.when((v == 0) | (ve_ref[v] != ve_ref[jnp.maximum(v - 1, 0)]))
        def _new_expert():
            wgu_scr[:, 0:EXPERT_FF] = wg_ref[0].astype(BF16)
            wgu_scr[:, EXPERT_FF:2 * EXPERT_FF] = wu_ref[0].astype(BF16)
            wd_scr[...] = wd_ref[0].astype(BF16)

        x = xbuf[t % EXPERT_RING].astype(BF16)
        gu = jnp.dot(x, wgu_scr[...], preferred_element_type=F32)
        g = gu[:, 0:EXPERT_FF]
        a = (g * jax.nn.sigmoid(g) * gu[:, EXPERT_FF:2 * EXPERT_FF]).astype(BF16)
        y = jnp.dot(a, wd_scr[...], preferred_element_type=F32)
        lo, hi = lo_ref[v], hi_ref[v]

        @pl.when(lo == 0)
        def _first_visit():
            y_ref[...] = y

        @pl.when(lo > 0)
        def _later_visit():
            row = lax.broadcasted_iota(jnp.int32, y.shape, 0)
            y_ref[...] = jnp.where((row >= lo) & (row < hi), y, y_ref[...])


def _experts(vis_tile, vis_expert, vis_lo, vis_hi, n_vis, xs, w_gate, w_up, w_down, tm):
    n_steps = vis_tile.shape[0]
    last = lambda v, nv: jnp.minimum(v, nv[0] - 1)
    of_expert = lambda v, vt, ve, lo, hi, nv: (ve[last(v, nv)], 0, 0)
    assert xs.shape[0] % tm == 0
    return pl.pallas_call(
        functools.partial(_expert_kernel, tm=tm, n_tiles=xs.shape[0] // tm),
        grid_spec=pltpu.PrefetchScalarGridSpec(
            num_scalar_prefetch=5,
            grid=(n_steps,),
            in_specs=[pl.BlockSpec(memory_space=pl.ANY),
                      pl.BlockSpec((1, D_MODEL, EXPERT_FF), of_expert),
                      pl.BlockSpec((1, D_MODEL, EXPERT_FF), of_expert),
                      pl.BlockSpec((1, EXPERT_FF, D_MODEL), of_expert)],
            out_specs=pl.BlockSpec((tm, D_MODEL), lambda v, vt, ve, lo, hi, nv: (vt[last(v, nv)], 0)),
            scratch_shapes=[pltpu.VMEM((D_MODEL, 2 * EXPERT_FF), BF16), pltpu.VMEM((EXPERT_FF, D_MODEL), BF16),
                            pltpu.VMEM((EXPERT_RING, tm, D_MODEL), F32), pltpu.SemaphoreType.DMA((EXPERT_RING,))]),
        out_shape=jax.ShapeDtypeStruct(xs.shape, F32),
        compiler_params=_params(1),
        name="experts",
    )(vis_tile, vis_expert, vis_lo, vis_hi, n_vis, xs, w_gate, w_up, w_down)


def _combine_kernel(pos0_ref, pos1_ref, nxt0_ref, nxt1_ref, ys_ref, xmid_ref, wcol_ref, out_ref, ybuf, sems,
                    *, tm, n_steps):
    i = pl.program_id(0)
    slot = i % 2

    def issue(p_refs, to_slot):
        def body(blk, carry):
            for k in range(SUBLANES):
                r = blk * SUBLANES + k
                for e in range(2):
                    pltpu.make_async_copy(ys_ref.at[pl.ds(p_refs[e][0, 0, r], 1)],
                                          ybuf.at[to_slot, e, blk, pl.ds(k, 1)], sems.at[to_slot]).start(priority=e)
            return carry

        lax.fori_loop(0, tm // SUBLANES, body, 0)

    @pl.when(i == 0)
    def _first():
        issue((pos0_ref, pos1_ref), 0)

    @pl.when(i + 1 < n_steps)
    def _ahead():
        issue((nxt0_ref, nxt1_ref), 1 - slot)

    for e in range(2):
        pltpu.make_async_copy(ybuf.at[slot, e], ybuf.at[slot, e], sems.at[slot]).wait()
    w = wcol_ref[...]
    y0 = ybuf[slot, 0].reshape(tm, D_MODEL)
    y1 = ybuf[slot, 1].reshape(tm, D_MODEL)
    out_ref[...] = xmid_ref[...] + w[:, 0:1] * y0 + w[:, 1:2] * y1


def _combine(pos, ys, xmid, wcol, tm):
    t_rows = xmid.shape[0]
    n_steps = t_rows // tm
    assert tm % SUBLANES == 0
    pos3 = _step_major(pos, tm)
    row = lambda i: (i, 0)
    cur = pl.BlockSpec((1, 1, tm), lambda i: (i, 0, 0), memory_space=pltpu.SMEM)
    nxt = pl.BlockSpec((1, 1, tm), lambda i: (jnp.minimum(i + 1, n_steps - 1), 0, 0), memory_space=pltpu.SMEM)
    return pl.pallas_call(
        functools.partial(_combine_kernel, tm=tm, n_steps=n_steps),
        grid=(n_steps,),
        in_specs=[cur, cur, nxt, nxt,
                  pl.BlockSpec(memory_space=pl.ANY),
                  pl.BlockSpec((tm, D_MODEL), row),
                  pl.BlockSpec((tm, LANES), row)],
        out_specs=pl.BlockSpec((tm, D_MODEL), row),
        out_shape=jax.ShapeDtypeStruct((t_rows, D_MODEL), F32),
        scratch_shapes=[pltpu.VMEM((2, 2, tm // SUBLANES, SUBLANES, D_MODEL), F32), pltpu.SemaphoreType.DMA((2,))],
        compiler_params=_params(1),
        name="combine",
    )(*pos3, *pos3, ys, xmid, wcol)


def _bucket(ends, idx):
    n = jnp.sum((ends[None, :] <= idx[:, None]).astype(jnp.int32), axis=1)
    return jnp.minimum(n, ends.shape[0] - 1)


def _tile_for(rows, pref):
    tm = min(pref, rows)
    assert rows % tm == 0
    return tm


def kernel(x_prompt, x_sample, cache_ret_state, cache_swa_k, cache_swa_v, meta_tokens, norm1_g, w_in, q_norm_g,
           k_norm_g, ret_norm_g, attn_sinks, w_out, norm2_g, w_group, w_expert, w_gate, w_up, w_down):
    assert norm1_g.shape[0] == 1, "single-layer trunk"
    bp, lp, _ = x_prompt.shape
    bs, ls, _ = x_sample.shape
    n_pairs = RET_HEADS // 2

    g1 = norm1_g[0][None, :]
    g2 = norm2_g[0][None, :]
    w_in_bf = w_in[0].astype(BF16)
    w_out_bf = w_out[0].astype(BF16)
    qg2 = jnp.tile(q_norm_g[0], 2)[None, :]
    kg2 = jnp.tile(k_norm_g[0], 2)[None, :]
    rng = ret_norm_g[0].reshape(RET_HEADS, RET_DV)
    sink_tab = jnp.broadcast_to(jnp.repeat(attn_sinks[0], CHUNK).reshape(SWA_KV_HEADS, 1, 4 * CHUNK),
                                (SWA_KV_HEADS, SUBLANES, 4 * CHUNK))
    w_router_t = jnp.zeros((ROUTER_ROWS, D_MODEL), F32)
    w_router_t = w_router_t.at[0:N_GROUPS].set(w_group[0].T)
    w_router_t = w_router_t.at[ROUTER_EXPERT_ROW0:ROUTER_EXPERT_ROW0 + N_EXPERTS].set(w_expert[0].T)
    w_router_t_bf = w_router_t.astype(BF16)

    meta_rows = 2 * CHUNK
    m_pad = jnp.zeros((meta_rows, D_MODEL), F32).at[0:N_META].set(meta_tokens)
    m_rqk, m_rv, _, _, m_skv = _proj(m_pad, jnp.arange(meta_rows, dtype=jnp.int32), meta_rows, g1, w_in_bf, qg2, kg2)
    s_meta = _meta_state(m_rqk, m_rv)[None]
    meta_kv = m_skv[0:N_META]

    groups = [
        dict(x=x_prompt.reshape(bp * lp, D_MODEL), n=bp, seq=lp, pos0=N_META, has_hist=False, s0=s_meta,
             hist=jnp.zeros((1, WINDOW, 2 * SWA_KV), F32)),
        dict(x=x_sample.reshape(bs * ls, D_MODEL), n=bs, seq=ls, pos0=N_META + PAST_LEN, has_hist=True,
             s0=cache_ret_state[0].reshape(bs, n_pairs, 2 * RET_DK, RET_DV),
             hist=jnp.concatenate([cache_swa_k[0].reshape(bs, WINDOW, SWA_KV),
                                   cache_swa_v[0].reshape(bs, WINDOW, SWA_KV)], axis=-1)),
    ]

    base_cnt = jnp.zeros((N_EXPERTS, LANES), F32)
    for g in groups:
        rows = g["n"] * g["seq"]
        tm = _tile_for(rows, PROJ_TILE)
        pos = g["pos0"] + jnp.arange(g["seq"], dtype=jnp.int32)
        if g["seq"] < tm:
            assert tm % g["seq"] == 0
            pos = jnp.tile(pos, tm // g["seq"])
        else:
            assert g["seq"] % tm == 0
        rqk, rv, gate, sq, skv = _proj(g["x"], pos, tm, g1, w_in_bf, qg2, kg2)
        tl = min(ATTN_TILE, g["seq"])
        omix, s_out, kv_out = _attention(rqk, rv, gate, sq, skv, meta_kv, g["hist"], g["s0"], rng, sink_tab,
                                         n_streams=g["n"], seq=g["seq"], tl=tl, has_hist=g["has_hist"])
        xmid, wcol, route_t, base_cnt = _post(omix, g["x"], w_out_bf, g2, w_router_t_bf, base_cnt,
                                              _tile_for(rows, POST_TILE))
        g.update(xmid=xmid, wcol=wcol, route_t=route_t, s_out=s_out, kv_out=kv_out)

    te = EXPERT_TILE
    total_rows = sum(g["n"] * g["seq"] for g in groups)
    assert (2 * total_rows) % te == 0
    n_row_tiles = (2 * total_rows) // te
    counts = base_cnt[:, 0].astype(jnp.int32)
    off = jnp.cumsum(counts) - counts
    first_tile = off // te
    n_vis_e = jnp.where(counts > 0, (off + counts - 1) // te - first_tile + 1, 0)
    vis_end = jnp.cumsum(n_vis_e)
    n_vis = vis_end[-1:].astype(jnp.int32)
    v = jnp.arange(n_row_tiles + N_EXPERTS, dtype=jnp.int32)
    vis_expert = _bucket(vis_end, v)
    pick = lambda table: jnp.sum(jnp.where(vis_expert[:, None] == jnp.arange(N_EXPERTS, dtype=jnp.int32)[None, :],
                                           table[None, :], 0), axis=1)
    vis_tile = jnp.clip(pick(first_tile) + v - pick(vis_end - n_vis_e), 0, n_row_tiles - 1).astype(jnp.int32)
    vis_lo = jnp.clip(pick(off) - vis_tile * te, 0, te).astype(jnp.int32)
    vis_hi = jnp.clip(pick(off + counts) - vis_tile * te, 0, te).astype(jnp.int32)

    for g in groups:
        eid = g["route_t"][0:2].astype(jnp.int32)
        off_sel = jnp.sum(jnp.where(eid[None] == jnp.arange(N_EXPERTS, dtype=jnp.int32)[:, None, None],
                                    off[:, None, None], 0), axis=0)
        g["pos"] = (off_sel + g["route_t"][4:6].astype(jnp.int32)).astype(jnp.int32)
    xs = _dispatch([g["pos"] for g in groups], [g["xmid"] for g in groups], g2,
                   _tile_for(min(g["n"] * g["seq"] for g in groups), MOVE_TILE))

    ys = _experts(vis_tile, vis_expert.astype(jnp.int32), vis_lo, vis_hi, n_vis, xs, w_gate[0], w_up[0], w_down[0], te)

    outs = []
    for g in groups:
        rows = g["n"] * g["seq"]
        y = _combine(g["pos"], ys, g["xmid"], g["wcol"], _tile_for(rows, MOVE_TILE))
        outs.append(y.reshape(g["n"], g["seq"], D_MODEL))

    def caches(g):
        kv = g["kv_out"]
        k = kv[:, :, 0:SWA_KV].reshape(g["n"], WINDOW, SWA_KV_HEADS, SWA_HD)[None]
        v = kv[:, :, SWA_KV:2 * SWA_KV].reshape(g["n"], WINDOW, SWA_KV_HEADS, SWA_HD)[None]
        s = g["s_out"].reshape(g["n"], RET_HEADS, RET_DK, RET_DV)[None]
        return s, k, v

    sp, kp, vp = caches(groups[0])
    ss, ks, vs = caches(groups[1])
    return (outs[0], outs[1], sp, kp, vp, ss, ks, vs)
```
